```python
import jax, jax.numpy as jnp
from jax import lax
import numpy as np

D_MODEL = 1024
BATCH = 32
SEQ = 256
DEPTH = 2
DEC_BATCH = 8
DEC_SEQ = 2048
PAST_LEN = 256

GRID_W = 64
N_EVEN = (DEPTH + 1) // 2
N_ODD = DEPTH // 2
A_WIDTH = D_MODEL // 2
A_GROUPS = 4
A_GC = A_WIDTH // A_GROUPS
CHUNK_A = 128
B_WIDTH = D_MODEL - A_WIDTH
CONV_W = 31
C_HEADS = 8
C_DK = 128
C_DV = D_MODEL // C_HEADS
C_FDIM = C_HEADS * C_DK
SCAN_CHUNK = 64
N_EXPERTS = 64
TOP_K = 8
N_GROUPS = 8
TOPK_GROUPS = 4
D_EXPERT = 256
D_SHARED = 256
ROUTED_SCALE = 2.5
MOE_BLOCK = 128
EPS = 1e-6

kernel_name = 'hybrid_gmlp_conformer_hgrn2_moe_dit_step'


def rmsnorm(x, g):
    xf = x.astype(jnp.float32)
    y = xf * lax.rsqrt(jnp.mean(xf * xf, axis=-1, keepdims=True) + EPS)
    return (y * g.astype(jnp.float32)).astype(x.dtype)


def layernorm(x, g):
    xf = x.astype(jnp.float32)
    mu = jnp.mean(xf, axis=-1, keepdims=True)
    xc = xf - mu
    y = xc * lax.rsqrt(jnp.mean(xc * xc, axis=-1, keepdims=True) + EPS)
    return (y * g.astype(jnp.float32)).astype(x.dtype)


def ada_params(cvec, w, b):
    m = jax.nn.silu(cvec) @ w + b
    return jnp.split(m, 6, axis=-1)


def modulate(x, g, shift, scale):
    return rmsnorm(x, g) * (1 + scale[:, None, :]) + shift[:, None, :]


def depthwise_conv(h, w, b):
    pad = CONV_W // 2
    y = lax.conv_general_dilated(h, w[:, None, :], (1,), [(pad, pad)],
                                 dimension_numbers=('NWC', 'WIO', 'NWC'),
                                 feature_group_count=h.shape[-1])
    return y + b


def mixer_ab(h, w_in, w_sg, b_sg, g_sg, w_dw, b_dw, g_cv, grid):
    B_, L, _ = h.shape
    u, v, a, gt = jnp.split(h @ w_in, 4, axis=-1)
    u = jax.nn.gelu(u)
    v = layernorm(jax.nn.gelu(v), g_sg)
    nc = L // CHUNK_A
    vh = v.reshape(B_, nc, CHUNK_A, A_GROUPS, A_GC)
    mixed = jnp.einsum('gpq,bnqgc->bnpgc', w_sg, vh) + b_sg.T[:, :, None]
    y_a = u * mixed.reshape(B_, L, A_WIDTH)
    hb = a * jax.nn.sigmoid(gt)
    if grid:
        rows = L // GRID_W
        hb = hb.reshape(B_ * rows, GRID_W, B_WIDTH)
    hb = depthwise_conv(hb, w_dw, b_dw).reshape(B_, L, B_WIDTH)
    y_b = jax.nn.silu(layernorm(hb, g_cv))
    return jnp.concatenate([y_a, y_b], axis=-1)


def gla_chunked(q, k, v, logf, s0):
    B_, L, H, _ = q.shape
    nc = L // SCAN_CHUNK

    def chunks(t):
        return t.reshape(B_, nc, SCAN_CHUNK, H, t.shape[-1]).transpose(1, 0, 3, 2, 4)

    qc, kc, vc, gc = chunks(q), chunks(k), chunks(v), chunks(logf)
    bcum = jnp.cumsum(gc, axis=3)
    b_mid = bcum[:, :, :, SCAN_CHUNK // 2 - 1:SCAN_CHUNK // 2]
    b_last = bcum[:, :, :, -1:]
    qe = qc * jnp.exp(bcum - b_mid)
    ke = kc * jnp.exp(b_mid - bcum)
    att = jnp.einsum('nbhtd,nbhsd->nbhts', qe, ke)
    tril = jnp.tril(jnp.ones((SCAN_CHUNK, SCAN_CHUNK), dtype=bool))
    att = jnp.where(tril, att, 0.0)
    o_intra = jnp.einsum('nbhts,nbhsv->nbhtv', att, vc)
    q_dec = qc * jnp.exp(bcum)
    k_dec = kc * jnp.exp(b_last - bcum)
    decay = jnp.exp(b_last[:, :, :, 0, :])

    def step(state, inp):
        qd, kd, vv, dc = inp
        o = jnp.einsum('bhtd,bhdv->bhtv', qd, state)
        state = dc[..., None] * state + jnp.einsum('bhtd,bhtv->bhdv', kd, vv)
        return state, o

    s_fin, o_inter = lax.scan(step, s0, (q_dec, k_dec, vc, decay))
    o = (o_intra + o_inter).transpose(1, 0, 3, 2, 4).reshape(B_, L, H, v.shape[-1])
    return o, s_fin


def mixer_hgrn(h, w_in, lb, g_o, s0):
    B_, L, _ = h.shape
    q, f_fw, f_bw, v, gate = jnp.split(
        h @ w_in, [C_FDIM, 2 * C_FDIM, 3 * C_FDIM, 3 * C_FDIM + D_MODEL], axis=-1)

    def heads(t):
        return t.astype(jnp.float32).reshape(B_, L, C_HEADS, -1)

    q = heads(jax.nn.silu(q)) * (C_DK ** -0.5)
    v = heads(v)
    lbf = lb.astype(jnp.float32).reshape(2, C_HEADS, C_DK)

    def gates(f, lbd):
        fg = lbd + (1 - lbd) * jax.nn.sigmoid(heads(f))
        return 1 - fg, jnp.log(fg)

    k_fw, lf_fw = gates(f_fw, lbf[0])
    k_bw, lf_bw = gates(f_bw, lbf[1])
    s0 = s0.astype(jnp.float32)
    o_fw, s_fw = gla_chunked(q, k_fw, v, lf_fw, s0[:, 0])

    def rev(t):
        return jnp.flip(t, axis=1)

    o_bw, s_bw = gla_chunked(rev(q), rev(k_bw), rev(v), rev(lf_bw), s0[:, 1])
    o = o_fw + rev(o_bw)
    o = rmsnorm(o, g_o) * jax.nn.silu(heads(gate))
    return o.reshape(B_, L, D_MODEL).astype(h.dtype), jnp.stack([s_fw, s_bw], axis=1)


def moe(x, w_router, b_router, w_gate, w_up, w_down, w_sh_gate, w_sh_up, w_sh_down):
    B_, L, D = x.shape
    T = B_ * L
    xf = x.reshape(T, D)
    scores = jax.nn.sigmoid((xf @ w_router).astype(jnp.float32))
    biased = scores + b_router.astype(jnp.float32)
    grp = lax.top_k(biased.reshape(T, N_GROUPS, N_EXPERTS // N_GROUPS), 2)[0].sum(-1)
    _, gidx = lax.top_k(grp, TOPK_GROUPS)
    gmask = jax.nn.one_hot(gidx, N_GROUPS, dtype=jnp.float32).sum(1)
    emask = jnp.repeat(gmask, N_EXPERTS // N_GROUPS, axis=-1) > 0
    _, eidx = lax.top_k(jnp.where(emask, biased, -jnp.inf), TOP_K)
    sel = jax.nn.one_hot(eidx, N_EXPERTS, dtype=jnp.float32).sum(1)
    wts = scores * sel
    wts = (wts / jnp.sum(wts, axis=-1, keepdims=True) * ROUTED_SCALE).astype(x.dtype)

    def expert_block(args):
        xb, wb = args
        hg = jnp.einsum('td,edf->tef', xb, w_gate)
        hu = jnp.einsum('td,edf->tef', xb, w_up)
        hh = jax.nn.silu(hg) * hu * wb[:, :, None]
        return jnp.einsum('tef,efd->td', hh, w_down)

    nb = T // MOE_BLOCK
    routed = lax.map(expert_block, (xf.reshape(nb, MOE_BLOCK, D),
                                    wts.reshape(nb, MOE_BLOCK, N_EXPERTS))).reshape(T, D)
    shared = (jax.nn.silu(xf @ w_sh_gate) * (xf @ w_sh_up)) @ w_sh_down
    return (routed + shared).reshape(B_, L, D)


def layer(l, x, cvec, grid, s0, P, lb_all):
    sh1, sc1, g1, sh2, sc2, g2 = ada_params(cvec, P['w_ada'][l], P['b_ada'][l])
    h = modulate(x, P['norm_mix'][l], sh1, sc1)
    j = l // 2
    new_s = None
    if l % 2 == 0:
        mixed = mixer_ab(h, P['w_in_ab'][j], P['w_sg'][j], P['b_sg'][j], P['norm_sg'][j],
                         P['w_dw'][j], P['b_dw'][j], P['norm_cv'][j], grid)
    else:
        mixed, new_s = mixer_hgrn(h, P['w_in_hgrn'][j], lb_all[l], P['norm_o'][j], s0)
    x = x + g1[:, None, :] * (mixed @ P['w_out'][l])
    h = modulate(x, P['norm_ffn'][l], sh2, sc2)
    x = x + g2[:, None, :] * moe(h, P['w_router'][l], P['b_router'][l], P['w_gate'][l],
                                  P['w_up'][l], P['w_down'][l], P['w_sh_gate'][l],
                                  P['w_sh_up'][l], P['w_sh_down'][l])
    return x, new_s


def setup_inputs(seed: int = 0) -> dict:
    key = jax.random.key(seed)
    ks = jax.random.split(key, 32)

    def nrm(k, shape, scale):
        return jax.random.normal(k, shape, jnp.float32) * scale

    def gain(k, shape):
        return 1.0 + nrm(k, shape, 0.05)

    D = D_MODEL
    return {
        'x_prompt': nrm(ks[0], (BATCH, SEQ, D), 1.0),
        'x_sample': nrm(ks[1], (DEC_BATCH, DEC_SEQ, D), 1.0),
        'state_hgrn': nrm(ks[2], (DEC_BATCH, N_ODD, 2, C_HEADS, C_DK, C_DV), 0.5),
        'c': nrm(ks[3], (DEC_BATCH, D), 1.0),
        'c_ctx': nrm(ks[4], (D,), 1.0),
        'w_ada': nrm(ks[5], (DEPTH, D, 6 * D), 0.5 * D ** -0.5),
        'b_ada': nrm(ks[6], (DEPTH, 6 * D), 0.1),
        'norm_mix': gain(ks[7], (DEPTH, D)),
        'norm_ffn': gain(ks[8], (DEPTH, D)),
        'w_out': nrm(ks[9], (DEPTH, D, D), D ** -0.5),
        'w_in_ab': nrm(ks[10], (N_EVEN, D, 2 * A_WIDTH + 2 * B_WIDTH), D ** -0.5),
        'w_sg': nrm(ks[11], (N_EVEN, A_GROUPS, CHUNK_A, CHUNK_A), CHUNK_A ** -0.5),
        'b_sg': gain(ks[12], (N_EVEN, A_GROUPS, CHUNK_A)),
        'norm_sg': gain(ks[13], (N_EVEN, A_WIDTH)),
        'w_dw': nrm(ks[14], (N_EVEN, CONV_W, B_WIDTH), CONV_W ** -0.5),
        'b_dw': nrm(ks[15], (N_EVEN, B_WIDTH), 0.02),
        'norm_cv': gain(ks[16], (N_EVEN, B_WIDTH)),
        'w_in_hgrn': nrm(ks[17], (N_ODD, D, 3 * C_FDIM + 2 * D), D ** -0.5),
        'lb_raw': nrm(ks[18], (DEPTH, 2, C_FDIM), 0.1),
        'norm_o': gain(ks[19], (N_ODD, C_DV)),
        'w_router': nrm(ks[20], (DEPTH, D, N_EXPERTS), D ** -0.5),
        'b_router': nrm(ks[21], (DEPTH, N_EXPERTS), 0.01),
        'w_gate': nrm(ks[22], (DEPTH, N_EXPERTS, D, D_EXPERT), D ** -0.5),
        'w_up': nrm(ks[23], (DEPTH, N_EXPERTS, D, D_EXPERT), D ** -0.5),
        'w_down': nrm(ks[24], (DEPTH, N_EXPERTS, D_EXPERT, D), D_EXPERT ** -0.5),
        'w_sh_gate': nrm(ks[25], (DEPTH, D, D_SHARED), D ** -0.5),
        'w_sh_up': nrm(ks[26], (DEPTH, D, D_SHARED), D ** -0.5),
        'w_sh_down': nrm(ks[27], (DEPTH, D_SHARED, D), D_SHARED ** -0.5),
        'norm_final': gain(ks[28], (D,)),
    }


def reference(x_prompt, x_sample, state_hgrn, c, c_ctx, w_ada, b_ada, norm_mix, norm_ffn,
              w_out, w_in_ab, w_sg, b_sg, norm_sg, w_dw, b_dw, norm_cv, w_in_hgrn, lb_raw,
              norm_o, w_router, b_router, w_gate, w_up, w_down, w_sh_gate, w_sh_up,
              w_sh_down, norm_final):
    P = {'w_ada': w_ada, 'b_ada': b_ada, 'norm_mix': norm_mix, 'norm_ffn': norm_ffn,
         'w_out': w_out, 'w_in_ab': w_in_ab, 'w_sg': w_sg, 'b_sg': b_sg, 'norm_sg': norm_sg,
         'w_dw': w_dw, 'b_dw': b_dw, 'norm_cv': norm_cv, 'w_in_hgrn': w_in_hgrn,
         'norm_o': norm_o, 'w_router': w_router, 'b_router': b_router, 'w_gate': w_gate,
         'w_up': w_up, 'w_down': w_down, 'w_sh_gate': w_sh_gate, 'w_sh_up': w_sh_up,
         'w_sh_down': w_sh_down}
    lb_sm = jax.nn.softmax(lb_raw.astype(jnp.float32), axis=0)
    lb_all = jnp.cumsum(lb_sm, axis=0) - lb_sm[0]
    nb_ctx = x_prompt.shape[0]
    cctx = jnp.broadcast_to(c_ctx, (nb_ctx, D_MODEL))
    xp = x_prompt
    xs = x_sample
    new_states = []
    for l in range(DEPTH):
        s0_ctx = None
        s0_lat = None
        if l % 2 == 1:
            s0_ctx = jnp.zeros((nb_ctx, 2, C_HEADS, C_DK, C_DV), jnp.float32)
            s0_lat = state_hgrn[:, l // 2]
        xp, s_ctx = layer(l, xp, cctx, False, s0_ctx, P, lb_all)
        xs, _ = layer(l, xs, c, True, s0_lat, P, lb_all)
        if s_ctx is not None:
            new_states.append(s_ctx)
    new_state_hgrn = jnp.stack(new_states, axis=1)
    y_prompt = rmsnorm(xp, norm_final)
    y_sample = rmsnorm(xs, norm_final)
    return (y_prompt, y_sample, new_state_hgrn)
```

```python
import functools

import jax
import jax.numpy as jnp
from jax import lax
from jax.experimental import pallas as pl
from jax.experimental.pallas import tpu as pltpu

F32 = jnp.float32
BF16 = jnp.bfloat16
HIGHEST = lax.Precision.HIGHEST

D_MODEL = 1024
BATCH = 32
SEQ = 256
DEPTH = 2
DEC_BATCH = 8
DEC_SEQ = 2048
GRID_W = 64
A_WIDTH = D_MODEL // 2
A_GROUPS = 4
A_GC = A_WIDTH // A_GROUPS
CHUNK_A = 128
B_WIDTH = D_MODEL - A_WIDTH
CONV_W = 31
CONV_PAD = CONV_W // 2
C_HEADS = 8
C_DK = 128
C_DV = D_MODEL // C_HEADS
C_FDIM = C_HEADS * C_DK
SCAN_CHUNK = 64
N_EXPERTS = 64
TOP_K = 8
N_GROUPS = 8
GROUP_SIZE = N_EXPERTS // N_GROUPS
TOPK_GROUPS = 4
D_EXPERT = 256
D_SHARED = 256
ROUTED_SCALE = 2.5
EPS = 1e-6

TM = 256
T_P = BATCH * SEQ
T_S = DEC_BATCH * DEC_SEQ
T = T_P + T_S
NT_P = T_P // TM
NT_S = T_S // TM
NT = NT_P + NT_S
TILES_PER_LAT = DEC_SEQ // TM
N_PAIRS = T * TOP_K
NB = N_PAIRS // TM + N_EXPERTS
P_ROWS = NB * TM
N_MOD_ROWS = 16
CONV_HALO = 16
VMEM_LIMIT = 48 * 1024 * 1024


def _mod_row(i):
    return jnp.where(i < NT_P, 0, 1 + (i - NT_P) // TILES_PER_LAT)


def _silu(x):
    return x * jax.nn.sigmoid(x)


def _gelu(x):
    return x * (0.5 * (1.0 + jnp.tanh(0.7978845608028654 * (x + 0.044715 * (x * x * x)))))


def _rms(x, g):
    return x * lax.rsqrt(jnp.mean(x * x, axis=-1, keepdims=True) + EPS) * g


def _layernorm(x, g):
    xc = x - jnp.mean(x, axis=-1, keepdims=True)
    return xc * lax.rsqrt(jnp.mean(xc * xc, axis=-1, keepdims=True) + EPS) * g


def _modulate(x, g, shift, scale):
    return _rms(x, g) * (1.0 + scale) + shift


def _bdot(a, b):
    return jnp.dot(a, b, preferred_element_type=F32)


def _dot_nt(a, b, precision=None):
    return lax.dot_general(a, b, (((1,), (1,)), ((), ())), precision=precision,
                           preferred_element_type=F32)


def _dot_tn(a, b):
    return lax.dot_general(a, b, (((0,), (0,)), ((), ())), preferred_element_type=F32)


def _params(*sem):
    return pltpu.CompilerParams(dimension_semantics=sem, vmem_limit_bytes=VMEM_LIMIT)


def _const_spec(shape):
    nd = len(shape)
    return pl.BlockSpec(shape, lambda *_: (0,) * nd)


def _ada_body(c_ref, w_ref, b_ref, o_ref):
    s = _silu(c_ref[...])
    o_ref[...] = jnp.dot(s, w_ref[...], precision=HIGHEST, preferred_element_type=F32) + b_ref[...]


def _ada_tables(cvecs, w_ada, b_ada):
    out = pl.pallas_call(
        _ada_body,
        out_shape=jax.ShapeDtypeStruct((DEPTH, N_MOD_ROWS, 6 * D_MODEL), F32),
        grid=(DEPTH, 6),
        in_specs=[
            _const_spec((N_MOD_ROWS, D_MODEL)),
            pl.BlockSpec((None, D_MODEL, D_MODEL), lambda l, j: (l, 0, j)),
            pl.BlockSpec((None, 1, D_MODEL), lambda l, j: (l, 0, j)),
        ],
        out_specs=pl.BlockSpec((None, N_MOD_ROWS, D_MODEL), lambda l, j: (l, 0, j)),
        compiler_params=_params("parallel", "parallel"),
        name="ada_tables",
    )(cvecs, w_ada, b_ada.reshape(DEPTH, 1, 6 * D_MODEL))
    return out.reshape(DEPTH, N_MOD_ROWS, 6, D_MODEL)


def _mod_spec():
    return pl.BlockSpec((None, 6, D_MODEL), lambda i: (_mod_row(i), 0, 0))


def _tile_spec():
    return pl.BlockSpec((TM, D_MODEL), lambda i: (i, 0))


def _conv_segment(pad_ref, conv_ref, wdw_ref, pad_base, out_base, seg):
    rb = min(seg, 64)
    for cb in range(B_WIDTH // 128):
        cs = slice(cb * 128, (cb + 1) * 128)
        for r0 in range(0, seg, rb):
            acc = jnp.zeros((rb, 128), F32)
            for k in range(CONV_W):
                off = pad_base + CONV_HALO - CONV_PAD + k + r0
                acc = acc + wdw_ref[k:k + 1, cs] * pad_ref[off:off + rb, cs]
            conv_ref[out_base + r0:out_base + r0 + rb, cs] = acc


def _l0_body(x_ref, mod_ref, nmix_ref, win_ref, wsg_ref, bsg_ref, nsg_ref, wdw_ref, bdw_ref,
             ncv_ref, wout_ref, o_ref, cat_ref, pad_ref, conv_ref):
    i = pl.program_id(0)
    x = x_ref[...]
    h = _modulate(x, nmix_ref[...], mod_ref[0:1, :], mod_ref[1:2, :])
    z = _bdot(h.astype(BF16), win_ref[...])
    u = _gelu(z[:, :A_WIDTH])
    vb = _layernorm(_gelu(z[:, A_WIDTH:2 * A_WIDTH]), nsg_ref[...]).astype(BF16)
    for n in range(TM // CHUNK_A):
        rs = slice(n * CHUNK_A, (n + 1) * CHUNK_A)
        for g in range(A_GROUPS):
            cs = slice(g * A_GC, (g + 1) * A_GC)
            m = _bdot(wsg_ref[g], vb[rs, cs]) + bsg_ref[g]
            cat_ref[rs, cs] = (u[rs, cs] * m).astype(BF16)
    hb = z[:, 2 * A_WIDTH:2 * A_WIDTH + B_WIDTH] * jax.nn.sigmoid(z[:, 2 * A_WIDTH + B_WIDTH:])

    def conv_tile(seg):
        stride = seg + 2 * CONV_HALO
        halo = jnp.zeros((CONV_HALO, B_WIDTH), F32)
        for s in range(TM // seg):
            b = s * stride
            pad_ref[b:b + CONV_HALO, :] = halo
            pad_ref[b + CONV_HALO:b + CONV_HALO + seg, :] = hb[s * seg:(s + 1) * seg, :]
            pad_ref[b + CONV_HALO + seg:b + stride, :] = halo
        for s in range(TM // seg):
            _conv_segment(pad_ref, conv_ref, wdw_ref, s * stride, s * seg, seg)

    @pl.when(i < NT_P)
    def _():
        conv_tile(SEQ)

    @pl.when(i >= NT_P)
    def _():
        conv_tile(GRID_W)

    yb = _layernorm(conv_ref[...] + bdw_ref[...], ncv_ref[...])
    cat_ref[:, A_WIDTH:] = _silu(yb).astype(BF16)
    out = _bdot(cat_ref[...], wout_ref[...])
    o_ref[...] = x + mod_ref[2:3, :] * out


def _l0_mixer(x, mod, norm_mix, w_in, w_sg, b_sg, norm_sg, w_dw, b_dw, norm_cv, w_out):
    assert TM == SEQ and TM % GRID_W == 0 and TM % CHUNK_A == 0
    pad_rows = (TM // GRID_W) * (GRID_W + 2 * CONV_HALO)
    assert pad_rows >= SEQ + 2 * CONV_HALO
    return pl.pallas_call(
        _l0_body,
        out_shape=jax.ShapeDtypeStruct((T, D_MODEL), F32),
        grid=(NT,),
        in_specs=[
            _tile_spec(), _mod_spec(),
            _const_spec((1, D_MODEL)),
            _const_spec((D_MODEL, 2 * A_WIDTH + 2 * B_WIDTH)),
            _const_spec((A_GROUPS, CHUNK_A, CHUNK_A)),
            _const_spec((A_GROUPS, CHUNK_A, 1)),
            _const_spec((1, A_WIDTH)),
            _const_spec((CONV_W, B_WIDTH)),
            _const_spec((1, B_WIDTH)),
            _const_spec((1, B_WIDTH)),
            _const_spec((D_MODEL, D_MODEL)),
        ],
        out_specs=_tile_spec(),
        scratch_shapes=[
            pltpu.VMEM((TM, D_MODEL), BF16),
            pltpu.VMEM((pad_rows, B_WIDTH), F32),
            pltpu.VMEM((TM, B_WIDTH), F32),
        ],
        compiler_params=_params("parallel"),
        name="l0_mixer",
    )(x, mod, norm_mix.reshape(1, -1), w_in.astype(BF16), w_sg.astype(BF16),
      b_sg.reshape(A_GROUPS, CHUNK_A, 1), norm_sg.reshape(1, -1), w_dw, b_dw.reshape(1, -1),
      norm_cv.reshape(1, -1), w_out.astype(BF16))


def _route(scores, biased):
    n = scores.shape[-1]
    shp = (N_GROUPS, GROUP_SIZE, n)
    s3 = scores.reshape(shp)
    b3 = biased.reshape(shp)
    m_iota = lax.broadcasted_iota(jnp.int32, shp, 1).astype(F32)
    g_iota = lax.broadcasted_iota(jnp.int32, shp, 0).astype(F32)
    e_iota = g_iota * GROUP_SIZE + m_iota
    neg = -jnp.inf

    def amax1(v):
        return jnp.max(v, axis=1, keepdims=True)

    def amin1(v):
        return jnp.min(v, axis=1, keepdims=True)

    m1 = amax1(b3)
    i1 = amin1(jnp.where(b3 == m1, m_iota, float(GROUP_SIZE)))
    m2 = amax1(jnp.where(m_iota == i1, neg, b3))
    grp = m1 + m2
    gi1 = lax.broadcasted_iota(jnp.int32, grp.shape, 0).astype(F32)
    gmask = jnp.zeros(grp.shape, jnp.bool_)
    for _ in range(TOPK_GROUPS):
        gm = jnp.max(grp, axis=0, keepdims=True)
        gi = jnp.min(jnp.where(grp == gm, gi1, float(N_GROUPS)), axis=0, keepdims=True)
        hit = gi1 == gi
        gmask = jnp.logical_or(gmask, hit)
        grp = jnp.where(hit, neg, grp)
    cand = jnp.where(gmask, b3, neg)
    ids, vals = [], []
    for _ in range(TOP_K):
        mx = jnp.max(amax1(cand), axis=0, keepdims=True)
        ei = jnp.min(amin1(jnp.where(cand == mx, e_iota, float(N_EXPERTS))), axis=0, keepdims=True)
        hit = e_iota == ei
        sv = jnp.sum(jnp.sum(jnp.where(hit, s3, 0.0), axis=1, keepdims=True), axis=0, keepdims=True)
        ids.append(ei.reshape(1, n))
        vals.append(sv.reshape(1, n))
        cand = jnp.where(hit, neg, cand)
    return jnp.concatenate(ids, axis=0).astype(jnp.int32), jnp.concatenate(vals, axis=0)


def _moe_pre_body(x_ref, mod_ref, nffn_ref, wrt_ref, br_ref, wsg_ref, wsu_ref, wsd_ref,
                  h_ref, eidx_ref, w8_ref, acc_ref):
    x = x_ref[...]
    h = _modulate(x, nffn_ref[...], mod_ref[3:4, :], mod_ref[4:5, :])
    h_ref[...] = h
    logits_t = _dot_nt(wrt_ref[...], h, precision=HIGHEST)
    scores = jax.nn.sigmoid(logits_t)
    eidx, sv = _route(scores, scores + br_ref[...])
    eidx_ref[...] = eidx
    w8_ref[...] = sv / jnp.sum(sv, axis=0, keepdims=True) * ROUTED_SCALE
    hb = h.astype(BF16)
    sh = _bdot((_silu(_bdot(hb, wsg_ref[...])) * _bdot(hb, wsu_ref[...])).astype(BF16), wsd_ref[...])
    acc_ref[...] = x + mod_ref[5:6, :] * sh


def _moe_pre(x, mod, norm_ffn, w_router, b_router, w_sh_gate, w_sh_up, w_sh_down):
    return pl.pallas_call(
        _moe_pre_body,
        out_shape=(
            jax.ShapeDtypeStruct((T, D_MODEL), F32),
            jax.ShapeDtypeStruct((TOP_K, T), jnp.int32),
            jax.ShapeDtypeStruct((TOP_K, T), F32),
            jax.ShapeDtypeStruct((T, D_MODEL), F32),
        ),
        grid=(NT,),
        in_specs=[
            _tile_spec(), _mod_spec(),
            _const_spec((1, D_MODEL)),
            _const_spec((N_EXPERTS, D_MODEL)),
            _const_spec((N_EXPERTS, 1)),
            _const_spec((D_MODEL, D_SHARED)),
            _const_spec((D_MODEL, D_SHARED)),
            _const_spec((D_SHARED, D_MODEL)),
        ],
        out_specs=(
            _tile_spec(),
            pl.BlockSpec((TOP_K, TM), lambda i: (0, i)),
            pl.BlockSpec((TOP_K, TM), lambda i: (0, i)),
            _tile_spec(),
        ),
        compiler_params=_params("parallel"),
        name="moe_router_shared",
    )(x, mod, norm_ffn.reshape(1, -1), w_router.T, b_router.reshape(N_EXPERTS, 1),
      w_sh_gate.astype(BF16), w_sh_up.astype(BF16), w_sh_down.astype(BF16))


def _expert_body(be_ref, nu_ref, xs_ref, wp_ref, wg_ref, wu_ref, wd_ref, ys_ref):
    @pl.when(pl.program_id(0) < nu_ref[0])
    def _():
        xb = xs_ref[...].astype(BF16)
        hg = _bdot(xb, wg_ref[...].astype(BF16))
        hu = _bdot(xb, wu_ref[...].astype(BF16))
        hh = _silu(hg) * hu * wp_ref[...]
        ys_ref[...] = _bdot(hh.astype(BF16), wd_ref[...].astype(BF16))


def _experts(block_expert, n_used, xs, w_rows, w_gate, w_up, w_down):
    def row_map(j, be, nu):
        return (jnp.minimum(j, nu[0] - 1), 0)

    def w_map(j, be, nu):
        return (be[jnp.minimum(j, nu[0] - 1)], 0, 0)

    return pl.pallas_call(
        _expert_body,
        out_shape=jax.ShapeDtypeStruct((P_ROWS, D_MODEL), F32),
        grid_spec=pltpu.PrefetchScalarGridSpec(
            num_scalar_prefetch=2,
            grid=(NB,),
            in_specs=[
                pl.BlockSpec((TM, D_MODEL), row_map),
                pl.BlockSpec((TM, 1), row_map),
                pl.BlockSpec((None, D_MODEL, D_EXPERT), w_map),
                pl.BlockSpec((None, D_MODEL, D_EXPERT), w_map),
                pl.BlockSpec((None, D_EXPERT, D_MODEL), w_map),
            ],
            out_specs=pl.BlockSpec((TM, D_MODEL), row_map),
        ),
        compiler_params=_params("arbitrary"),
        name="moe_experts",
    )(block_expert, n_used, xs, w_rows, w_gate, w_up, w_down)


def _dispatch_plan(eidx_t, w8_t):
    flat_e = eidx_t.T.reshape(-1)
    flat_w = w8_t.T.reshape(-1)
    onehot = (eidx_t.T[:, :, None] == jnp.arange(N_EXPERTS, dtype=jnp.int32)).any(axis=1)
    onehot = onehot.astype(jnp.int32)
    incl = jnp.cumsum(onehot, axis=0)
    counts = incl[-1]
    nblk = (counts + TM - 1) // TM
    blk_end = jnp.cumsum(nblk)
    blk_start = blk_end - nblk
    n_used = blk_end[-1]
    cstart = jnp.cumsum(counts) - counts
    block_expert = jnp.minimum(
        jnp.searchsorted(blk_end, jnp.arange(NB, dtype=jnp.int32), side="right"),
        N_EXPERTS - 1).astype(jnp.int32)
    rank = jnp.take_along_axis(incl - onehot, eidx_t.T, axis=1)
    pos = blk_start[eidx_t.T] * TM + rank
    order = jnp.argsort(flat_e, stable=True).astype(jnp.int32)
    p = jnp.arange(P_ROWS, dtype=jnp.int32)
    e_p = block_expert[p // TM]
    r_p = p - blk_start[e_p] * TM
    valid = jnp.logical_and(r_p < counts[e_p], p // TM < n_used)
    src = order[jnp.clip(cstart[e_p] + r_p, 0, N_PAIRS - 1)]
    tok_p = jnp.where(valid, src // TOP_K, 0)
    w_p = jnp.where(valid, flat_w[src], 0.0)
    return block_expert, n_used.reshape(1).astype(jnp.int32), tok_p, w_p.reshape(P_ROWS, 1), pos


def _moe_routed(h, eidx_t, w8_t, w_gate, w_up, w_down):
    block_expert, n_used, tok_p, w_p, pos = _dispatch_plan(eidx_t, w8_t)
    xs = jnp.take(h, tok_p, axis=0)
    ys = _experts(block_expert, n_used, xs, w_p, w_gate, w_up, w_down)
    return jnp.take(ys, pos, axis=0).sum(axis=1)


N_SLABS = (3 * C_FDIM + 2 * D_MODEL) // 128


def _hgrn_in_body(acc_ref, rt_ref, mod0_ref, mod1_ref, nmix_ref, win_ref, x_ref, z_ref):
    x = acc_ref[...] + mod0_ref[5:6, :] * rt_ref[...]
    x_ref[...] = x
    hb = _modulate(x, nmix_ref[...], mod1_ref[0:1, :], mod1_ref[1:2, :]).astype(BF16)
    for s in range(N_SLABS // C_HEADS):
        zz = _bdot(hb, win_ref[:, s * D_MODEL:(s + 1) * D_MODEL])
        for hh in range(C_HEADS):
            z_ref[s * C_HEADS + hh] = zz[:, hh * 128:(hh + 1) * 128]


def _hgrn_in(acc, routed, mod0, mod1, norm_mix, w_in):
    return pl.pallas_call(
        _hgrn_in_body,
        out_shape=(
            jax.ShapeDtypeStruct((T, D_MODEL), F32),
            jax.ShapeDtypeStruct((N_SLABS, T, 128), F32),
        ),
        grid=(NT,),
        in_specs=[
            _tile_spec(), _tile_spec(), _mod_spec(), _mod_spec(),
            _const_spec((1, D_MODEL)),
            _const_spec((D_MODEL, 3 * C_FDIM + 2 * D_MODEL)),
        ],
        out_specs=(
            _tile_spec(),
            pl.BlockSpec((N_SLABS, TM, 128), lambda i: (0, i, 0)),
        ),
        compiler_params=_params("parallel"),
        name="hgrn_in_proj",
    )(acc, routed, mod0, mod1, norm_mix.reshape(1, -1), w_in.astype(BF16))


def _gla_body(q_ref, f_ref, v_ref, lb_ref, s0_ref, o_ref, ns_ref, st_ref, *, rev):
    j = pl.program_id(0)
    ti = NT - 1 - j if rev else j
    is_ctx = ti < NT_P
    first_lat = (ti - NT_P) % TILES_PER_LAT == (TILES_PER_LAT - 1 if rev else 0)

    @pl.when(is_ctx)
    def _():
        st_ref[...] = jnp.zeros(st_ref.shape, F32)

    @pl.when(jnp.logical_and(jnp.logical_not(is_ctx), first_lat))
    def _():
        st_ref[...] = s0_ref[...]

    row = lax.broadcasted_iota(jnp.int32, (SCAN_CHUNK, SCAN_CHUNK), 0)
    col = lax.broadcasted_iota(jnp.int32, (SCAN_CHUNK, SCAN_CHUNK), 1)
    seen = (col >= row) if rev else (col <= row)
    cum_w = seen.astype(BF16)
    mid = SCAN_CHUNK // 2 if rev else SCAN_CHUNK // 2 - 1
    last = 0 if rev else SCAN_CHUNK - 1
    n_chunks = TM // SCAN_CHUNK
    order = range(n_chunks - 1, -1, -1) if rev else range(n_chunks)

    def head(hd, carry):
        lb = lb_ref[hd]
        qs = _silu(q_ref[hd]) * (C_DK ** -0.5)
        fg = lb + (1.0 - lb) * jax.nn.sigmoid(f_ref[hd])
        kk = 1.0 - fg
        lf = jnp.log(fg)
        vv = v_ref[hd].astype(BF16)
        st = st_ref[hd]
        for c in order:
            sl = slice(c * SCAN_CHUNK, (c + 1) * SCAN_CHUNK)
            g = lf[sl]
            g_hi = g.astype(BF16)
            r1 = g - g_hi.astype(F32)
            g_mid = r1.astype(BF16)
            g_lo = (r1 - g_mid.astype(F32)).astype(BF16)
            bcum = _bdot(cum_w, g_hi) + _bdot(cum_w, g_mid) + _bdot(cum_w, g_lo)
            b_mid = bcum[mid:mid + 1, :]
            b_last = bcum[last:last + 1, :]
            qc, kc, vc = qs[sl], kk[sl], vv[sl]
            qe = (qc * jnp.exp(bcum - b_mid)).astype(BF16)
            ke = (kc * jnp.exp(b_mid - bcum)).astype(BF16)
            att = jnp.where(seen, _dot_nt(qe, ke), 0.0)
            o = _bdot(att.astype(BF16), vc)
            q_dec = (qc * jnp.exp(bcum)).astype(BF16)
            k_dec = (kc * jnp.exp(b_last - bcum)).astype(BF16)
            o = o + _dot_nt(q_dec, st.astype(BF16))
            st = jnp.exp(b_last) * st + _dot_tn(vc, k_dec)
            o_ref[hd, pl.ds(c * SCAN_CHUNK, SCAN_CHUNK), :] = o
        st_ref[hd] = st
        return carry

    lax.fori_loop(0, C_HEADS, head, 0)

    @pl.when(is_ctx)
    def _():
        ns_ref[...] = st_ref[...]


def _gla(z3, lb_dir, s0t_dir, *, rev):
    def ti_of(j):
        return NT - 1 - j if rev else j

    f_slab = 2 if rev else 1

    def lat_map(j):
        return (jnp.clip((ti_of(j) - NT_P) // TILES_PER_LAT, 0, DEC_BATCH - 1), 0, 0, 0)

    return pl.pallas_call(
        functools.partial(_gla_body, rev=rev),
        out_shape=(
            jax.ShapeDtypeStruct((C_HEADS, T, C_DV), F32),
            jax.ShapeDtypeStruct((BATCH, C_HEADS, C_DV, C_DK), F32),
        ),
        grid=(NT,),
        in_specs=[
            pl.BlockSpec((C_HEADS, TM, 128), lambda j: (0, ti_of(j), 0)),
            pl.BlockSpec((C_HEADS, TM, 128), lambda j: (f_slab, ti_of(j), 0)),
            pl.BlockSpec((C_HEADS, TM, 128), lambda j: (3, ti_of(j), 0)),
            _const_spec((C_HEADS, 1, C_DK)),
            pl.BlockSpec((None, C_HEADS, C_DV, C_DK), lat_map),
        ],
        out_specs=(
            pl.BlockSpec((C_HEADS, TM, C_DV), lambda j: (0, ti_of(j), 0)),
            pl.BlockSpec((None, C_HEADS, C_DV, C_DK),
                         lambda j: (jnp.minimum(ti_of(j), NT_P - 1), 0, 0, 0)),
        ),
        scratch_shapes=[pltpu.VMEM((C_HEADS, C_DV, C_DK), F32)],
        compiler_params=_params("arbitrary"),
        name="gla_bwd" if rev else "gla_fwd",
    )(z3, z3, z3, lb_dir, s0t_dir)


def _hgrn_out_body(ofw_ref, obw_ref, gate_ref, x_ref, mod_ref, no_ref, wout_ref, o_ref, cat_ref):
    for hd in range(C_HEADS):
        o = ofw_ref[hd] + obw_ref[hd]
        cat_ref[:, hd * C_DV:(hd + 1) * C_DV] = (_rms(o, no_ref[...]) * _silu(gate_ref[hd])).astype(BF16)
    o_ref[...] = x_ref[...] + mod_ref[2:3, :] * _bdot(cat_ref[...], wout_ref[...])


def _hgrn_out(o_fw, o_bw, z3, x, mod, norm_o, w_out):
    head_spec = pl.BlockSpec((C_HEADS, TM, C_DV), lambda i: (0, i, 0))
    return pl.pallas_call(
        _hgrn_out_body,
        out_shape=jax.ShapeDtypeStruct((T, D_MODEL), F32),
        grid=(NT,),
        in_specs=[
            head_spec, head_spec,
            pl.BlockSpec((C_HEADS, TM, 128), lambda i: (4, i, 0)),
            _tile_spec(), _mod_spec(),
            _const_spec((1, C_DV)),
            _const_spec((D_MODEL, D_MODEL)),
        ],
        out_specs=_tile_spec(),
        scratch_shapes=[pltpu.VMEM((TM, D_MODEL), BF16)],
        compiler_params=_params("parallel"),
        name="hgrn_out_proj",
    )(o_fw, o_bw, z3, x, mod, norm_o.reshape(1, -1), w_out.astype(BF16))


def _final_body(acc_ref, rt_ref, mod_ref, nf_ref, o_ref):
    x = acc_ref[...] + mod_ref[5:6, :] * rt_ref[...]
    o_ref[...] = _rms(x, nf_ref[...])


def _final(acc, routed, mod, norm_final, tile0, n_tiles):
    in_tile = pl.BlockSpec((TM, D_MODEL), lambda i: (i + tile0, 0))
    return pl.pallas_call(
        _final_body,
        out_shape=jax.ShapeDtypeStruct((n_tiles * TM, D_MODEL), F32),
        grid=(n_tiles,),
        in_specs=[
            in_tile, in_tile,
            pl.BlockSpec((None, 6, D_MODEL), lambda i: (_mod_row(i + tile0), 0, 0)),
            _const_spec((1, D_MODEL)),
        ],
        out_specs=pl.BlockSpec((TM, D_MODEL), lambda i: (i, 0)),
        compiler_params=_params("parallel"),
        name="final_norm",
    )(acc, routed, mod, norm_final.reshape(1, -1))


def kernel(x_prompt, x_sample, state_hgrn, c, c_ctx, w_ada, b_ada, norm_mix, norm_ffn, w_out, w_in_ab, w_sg, b_sg, norm_sg, w_dw, b_dw, norm_cv, w_in_hgrn, lb_raw, norm_o, w_router, b_router, w_gate, w_up, w_down, w_sh_gate, w_sh_up, w_sh_down, norm_final):
    x = jnp.concatenate([x_prompt.reshape(T_P, D_MODEL), x_sample.reshape(T_S, D_MODEL)], axis=0)
    cvecs = jnp.concatenate(
        [c_ctx.reshape(1, D_MODEL), c, jnp.zeros((N_MOD_ROWS - 1 - DEC_BATCH, D_MODEL), F32)], axis=0)
    mods = _ada_tables(cvecs, w_ada, b_ada)
    lb_sm = jax.nn.softmax(lb_raw.astype(F32), axis=0)
    lb1 = (jnp.cumsum(lb_sm, axis=0) - lb_sm[0])[1].reshape(2, C_HEADS, 1, C_DK)

    def moe(l, xin):
        h, eidx_t, w8_t, acc = _moe_pre(xin, mods[l], norm_ffn[l], w_router[l], b_router[l],
                                        w_sh_gate[l], w_sh_up[l], w_sh_down[l])
        return acc, _moe_routed(h, eidx_t, w8_t, w_gate[l], w_up[l], w_down[l])

    x = _l0_mixer(x, mods[0], norm_mix[0], w_in_ab[0], w_sg[0], b_sg[0], norm_sg[0], w_dw[0],
                  b_dw[0], norm_cv[0], w_out[0])
    acc, routed = moe(0, x)
    x, z3 = _hgrn_in(acc, routed, mods[0], mods[1], norm_mix[1], w_in_hgrn[0])
    s0t = jnp.swapaxes(state_hgrn[:, 0].astype(F32), -1, -2)
    o_fw, ns_fw = _gla(z3, lb1[0], s0t[:, 0], rev=False)
    o_bw, ns_bw = _gla(z3, lb1[1], s0t[:, 1], rev=True)
    x = _hgrn_out(o_fw, o_bw, z3, x, mods[1], norm_o[0], w_out[1])
    acc, routed = moe(1, x)
    y_p = _final(acc, routed, mods[1], norm_final, 0, NT_P).reshape(BATCH, SEQ, D_MODEL)
    y_s = _final(acc, routed, mods[1], norm_final, NT_P, NT_S).reshape(DEC_BATCH, DEC_SEQ, D_MODEL)
    new_state = jnp.swapaxes(jnp.stack([ns_fw, ns_bw], axis=1), -1, -2)[:, None]
    return (y_p, y_s, new_state)
```

```python
import functools

import jax
import jax.numpy as jnp
from jax import lax
from jax.experimental import pallas as pl
from jax.experimental.pallas import tpu as pltpu
from jax.experimental.pallas import tpu_sc as plsc

F32 = jnp.float32
BF16 = jnp.bfloat16
HIGHEST = lax.Precision.HIGHEST

D_MODEL = 1024
BATCH = 32
SEQ = 256
DEPTH = 2
DEC_BATCH = 8
DEC_SEQ = 2048
GRID_W = 64
A_WIDTH = D_MODEL // 2
A_GROUPS = 4
A_GC = A_WIDTH // A_GROUPS
CHUNK_A = 128
B_WIDTH = D_MODEL - A_WIDTH
CONV_W = 31
CONV_PAD = CONV_W // 2
C_HEADS = 8
C_DK = 128
C_DV = D_MODEL // C_HEADS
C_FDIM = C_HEADS * C_DK
SCAN_CHUNK = 64
N_EXPERTS = 64
TOP_K = 8
N_GROUPS = 8
GROUP_SIZE = N_EXPERTS // N_GROUPS
TOPK_GROUPS = 4
D_EXPERT = 256
D_SHARED = 256
ROUTED_SCALE = 2.5
EPS = 1e-6

TM = 256
T_P = BATCH * SEQ
T_S = DEC_BATCH * DEC_SEQ
T = T_P + T_S
NT_P = T_P // TM
NT_S = T_S // TM
NT = NT_P + NT_S
TILES_PER_LAT = DEC_SEQ // TM
N_PAIRS = T * TOP_K
NB = N_PAIRS // TM + N_EXPERTS
P_ROWS = NB * TM
N_MOD_ROWS = 16
CONV_HALO = 16
VMEM_LIMIT = 48 * 1024 * 1024
SC_CORES = 2
SC_SUBCORES = 16
SC_WORKERS = SC_CORES * SC_SUBCORES
SC_CHUNK = 32


def _mod_row(i):
    return jnp.where(i < NT_P, 0, 1 + (i - NT_P) // TILES_PER_LAT)


def _silu(x):
    return x * jax.nn.sigmoid(x)


def _gelu(x):
    return x * (0.5 * (1.0 + jnp.tanh(0.7978845608028654 * (x + 0.044715 * (x * x * x)))))


def _rms(x, g):
    return x * lax.rsqrt(jnp.mean(x * x, axis=-1, keepdims=True) + EPS) * g


def _layernorm(x, g):
    xc = x - jnp.mean(x, axis=-1, keepdims=True)
    return xc * lax.rsqrt(jnp.mean(xc * xc, axis=-1, keepdims=True) + EPS) * g


def _modulate(x, g, shift, scale):
    return _rms(x, g) * (1.0 + scale) + shift


def _bdot(a, b):
    return jnp.dot(a, b, preferred_element_type=F32)


def _dot_nt(a, b, precision=None):
    return lax.dot_general(a, b, (((1,), (1,)), ((), ())), precision=precision,
                           preferred_element_type=F32)


def _dot_tn(a, b):
    return lax.dot_general(a, b, (((0,), (0,)), ((), ())), preferred_element_type=F32)


def _params(*sem):
    return pltpu.CompilerParams(dimension_semantics=sem, vmem_limit_bytes=VMEM_LIMIT)


def _const_spec(shape):
    nd = len(shape)
    return pl.BlockSpec(shape, lambda *_: (0,) * nd)


def _ada_body(c_ref, w_ref, b_ref, o_ref):
    s = _silu(c_ref[...])
    o_ref[...] = jnp.dot(s, w_ref[...], precision=HIGHEST, preferred_element_type=F32) + b_ref[...]


def _ada_tables(cvecs, w_ada, b_ada):
    out = pl.pallas_call(
        _ada_body,
        out_shape=jax.ShapeDtypeStruct((DEPTH, N_MOD_ROWS, 6 * D_MODEL), F32),
        grid=(DEPTH, 6),
        in_specs=[
            _const_spec((N_MOD_ROWS, D_MODEL)),
            pl.BlockSpec((None, D_MODEL, D_MODEL), lambda l, j: (l, 0, j)),
            pl.BlockSpec((None, 1, D_MODEL), lambda l, j: (l, 0, j)),
        ],
        out_specs=pl.BlockSpec((None, N_MOD_ROWS, D_MODEL), lambda l, j: (l, 0, j)),
        compiler_params=_params("parallel", "parallel"),
        name="ada_tables",
    )(cvecs, w_ada, b_ada.reshape(DEPTH, 1, 6 * D_MODEL))
    return out.reshape(DEPTH, N_MOD_ROWS, 6, D_MODEL)


def _mod_spec():
    return pl.BlockSpec((None, 6, D_MODEL), lambda i: (_mod_row(i), 0, 0))


def _tile_spec():
    return pl.BlockSpec((TM, D_MODEL), lambda i: (i, 0))


def _conv_segment(pad_ref, conv_ref, wdw_ref, pad_base, out_base, seg):
    rb = min(seg, 64)
    for cb in range(B_WIDTH // 128):
        cs = slice(cb * 128, (cb + 1) * 128)
        for r0 in range(0, seg, rb):
            acc = jnp.zeros((rb, 128), F32)
            for k in range(CONV_W):
                off = pad_base + CONV_HALO - CONV_PAD + k + r0
                acc = acc + wdw_ref[k:k + 1, cs] * pad_ref[off:off + rb, cs]
            conv_ref[out_base + r0:out_base + r0 + rb, cs] = acc


def _l0_body(x_ref, mod_ref, nmix_ref, win_ref, wsg_ref, bsg_ref, nsg_ref, wdw_ref, bdw_ref,
             ncv_ref, wout_ref, o_ref, cat_ref, pad_ref, conv_ref):
    i = pl.program_id(0)
    x = x_ref[...]
    h = _modulate(x, nmix_ref[...], mod_ref[0:1, :], mod_ref[1:2, :])
    z = _bdot(h.astype(BF16), win_ref[...])
    u = _gelu(z[:, :A_WIDTH])
    vb = _layernorm(_gelu(z[:, A_WIDTH:2 * A_WIDTH]), nsg_ref[...]).astype(BF16)
    for n in range(TM // CHUNK_A):
        rs = slice(n * CHUNK_A, (n + 1) * CHUNK_A)
        for g in range(A_GROUPS):
            cs = slice(g * A_GC, (g + 1) * A_GC)
            m = _bdot(wsg_ref[g], vb[rs, cs]) + bsg_ref[g]
            cat_ref[rs, cs] = (u[rs, cs] * m).astype(BF16)
    hb = z[:, 2 * A_WIDTH:2 * A_WIDTH + B_WIDTH] * jax.nn.sigmoid(z[:, 2 * A_WIDTH + B_WIDTH:])

    def conv_tile(seg):
        stride = seg + 2 * CONV_HALO
        halo = jnp.zeros((CONV_HALO, B_WIDTH), F32)
        for s in range(TM // seg):
            b = s * stride
            pad_ref[b:b + CONV_HALO, :] = halo
            pad_ref[b + CONV_HALO:b + CONV_HALO + seg, :] = hb[s * seg:(s + 1) * seg, :]
            pad_ref[b + CONV_HALO + seg:b + stride, :] = halo
        for s in range(TM // seg):
            _conv_segment(pad_ref, conv_ref, wdw_ref, s * stride, s * seg, seg)

    @pl.when(i < NT_P)
    def _():
        conv_tile(SEQ)

    @pl.when(i >= NT_P)
    def _():
        conv_tile(GRID_W)

    yb = _layernorm(conv_ref[...] + bdw_ref[...], ncv_ref[...])
    cat_ref[:, A_WIDTH:] = _silu(yb).astype(BF16)
    out = _bdot(cat_ref[...], wout_ref[...])
    o_ref[...] = x + mod_ref[2:3, :] * out


def _l0_mixer(x, mod, norm_mix, w_in, w_sg, b_sg, norm_sg, w_dw, b_dw, norm_cv, w_out):
    assert TM == SEQ and TM % GRID_W == 0 and TM % CHUNK_A == 0
    pad_rows = (TM // GRID_W) * (GRID_W + 2 * CONV_HALO)
    assert pad_rows >= SEQ + 2 * CONV_HALO
    return pl.pallas_call(
        _l0_body,
        out_shape=jax.ShapeDtypeStruct((T, D_MODEL), F32),
        grid=(NT,),
        in_specs=[
            _tile_spec(), _mod_spec(),
            _const_spec((1, D_MODEL)),
            _const_spec((D_MODEL, 2 * A_WIDTH + 2 * B_WIDTH)),
            _const_spec((A_GROUPS, CHUNK_A, CHUNK_A)),
            _const_spec((A_GROUPS, CHUNK_A, 1)),
            _const_spec((1, A_WIDTH)),
            _const_spec((CONV_W, B_WIDTH)),
            _const_spec((1, B_WIDTH)),
            _const_spec((1, B_WIDTH)),
            _const_spec((D_MODEL, D_MODEL)),
        ],
        out_specs=_tile_spec(),
        scratch_shapes=[
            pltpu.VMEM((TM, D_MODEL), BF16),
            pltpu.VMEM((pad_rows, B_WIDTH), F32),
            pltpu.VMEM((TM, B_WIDTH), F32),
        ],
        compiler_params=_params("parallel"),
        name="l0_mixer",
    )(x, mod, norm_mix.reshape(1, -1), w_in.astype(BF16), w_sg.astype(BF16),
      b_sg.reshape(A_GROUPS, CHUNK_A, 1), norm_sg.reshape(1, -1), w_dw, b_dw.reshape(1, -1),
      norm_cv.reshape(1, -1), w_out.astype(BF16))


def _route(scores, biased):
    n = scores.shape[-1]
    shp = (N_GROUPS, GROUP_SIZE, n)
    s3 = scores.reshape(shp)
    b3 = biased.reshape(shp)
    m_iota = lax.broadcasted_iota(jnp.int32, shp, 1).astype(F32)
    g_iota = lax.broadcasted_iota(jnp.int32, shp, 0).astype(F32)
    e_iota = g_iota * GROUP_SIZE + m_iota
    neg = -jnp.inf

    def amax1(v):
        return jnp.max(v, axis=1, keepdims=True)

    def amin1(v):
        return jnp.min(v, axis=1, keepdims=True)

    m1 = amax1(b3)
    i1 = amin1(jnp.where(b3 == m1, m_iota, float(GROUP_SIZE)))
    m2 = amax1(jnp.where(m_iota == i1, neg, b3))
    grp = m1 + m2
    gi1 = lax.broadcasted_iota(jnp.int32, grp.shape, 0).astype(F32)
    gmask = jnp.zeros(grp.shape, jnp.bool_)
    for _ in range(TOPK_GROUPS):
        gm = jnp.max(grp, axis=0, keepdims=True)
        gi = jnp.min(jnp.where(grp == gm, gi1, float(N_GROUPS)), axis=0, keepdims=True)
        hit = gi1 == gi
        gmask = jnp.logical_or(gmask, hit)
        grp = jnp.where(hit, neg, grp)
    cand = jnp.where(gmask, b3, neg)
    ids, vals, hits = [], [], []
    for _ in range(TOP_K):
        mx = jnp.max(amax1(cand), axis=0, keepdims=True)
        ei = jnp.min(amin1(jnp.where(cand == mx, e_iota, float(N_EXPERTS))), axis=0, keepdims=True)
        hit = e_iota == ei
        ids.append(ei.reshape(1, n))
        vals.append(_pick(hit, s3))
        hits.append(hit)
        cand = jnp.where(hit, neg, cand)
    return jnp.concatenate(ids, axis=0).astype(jnp.int32), jnp.concatenate(vals, axis=0), hits


def _pick(hit, v3):
    s = jnp.sum(jnp.sum(jnp.where(hit, v3, 0.0), axis=1, keepdims=True), axis=0, keepdims=True)
    return s.reshape(1, v3.shape[-1])


def _moe_pre_body(x_ref, mod_ref, nffn_ref, wrt_ref, br_ref, wsg_ref, wsu_ref, wsd_ref,
                  h_ref, eidx_ref, w8_ref, rank_ref, cnt_ref, acc_ref, run_ref):
    @pl.when(pl.program_id(0) == 0)
    def _():
        run_ref[...] = jnp.zeros(run_ref.shape, F32)

    x = x_ref[...]
    h = _modulate(x, nffn_ref[...], mod_ref[3:4, :], mod_ref[4:5, :])
    h_ref[...] = h
    logits_t = _dot_nt(wrt_ref[...], h, precision=HIGHEST)
    scores = jax.nn.sigmoid(logits_t)
    eidx, sv, hits = _route(scores, scores + br_ref[...])
    eidx_ref[...] = eidx
    w8_ref[...] = sv / jnp.sum(sv, axis=0, keepdims=True) * ROUTED_SCALE
    sel3 = hits[0]
    for hit in hits[1:]:
        sel3 = jnp.logical_or(sel3, hit)
    sel = sel3.astype(F32).reshape(N_EXPERTS, TM)
    earlier = (lax.broadcasted_iota(jnp.int32, (TM, TM), 0)
               < lax.broadcasted_iota(jnp.int32, (TM, TM), 1)).astype(BF16)
    rank3 = (_bdot(sel.astype(BF16), earlier) + run_ref[...]).reshape(N_GROUPS, GROUP_SIZE, TM)
    rank_ref[...] = jnp.concatenate([_pick(hit, rank3) for hit in hits], axis=0).astype(jnp.int32)
    run_ref[...] = run_ref[...] + jnp.sum(sel, axis=1, keepdims=True)
    cnt_ref[...] = run_ref[...].astype(jnp.int32)
    hb = h.astype(BF16)
    sh = _bdot((_silu(_bdot(hb, wsg_ref[...])) * _bdot(hb, wsu_ref[...])).astype(BF16), wsd_ref[...])
    acc_ref[...] = x + mod_ref[5:6, :] * sh


def _moe_pre(x, mod, norm_ffn, w_router, b_router, w_sh_gate, w_sh_up, w_sh_down):
    return pl.pallas_call(
        _moe_pre_body,
        out_shape=(
            jax.ShapeDtypeStruct((T, D_MODEL), F32),
            jax.ShapeDtypeStruct((TOP_K, T), jnp.int32),
            jax.ShapeDtypeStruct((TOP_K, T), F32),
            jax.ShapeDtypeStruct((TOP_K, T), jnp.int32),
            jax.ShapeDtypeStruct((N_EXPERTS, 1), jnp.int32),
            jax.ShapeDtypeStruct((T, D_MODEL), F32),
        ),
        grid=(NT,),
        in_specs=[
            _tile_spec(), _mod_spec(),
            _const_spec((1, D_MODEL)),
            _const_spec((N_EXPERTS, D_MODEL)),
            _const_spec((N_EXPERTS, 1)),
            _const_spec((D_MODEL, D_SHARED)),
            _const_spec((D_MODEL, D_SHARED)),
            _const_spec((D_SHARED, D_MODEL)),
        ],
        out_specs=(
            _tile_spec(),
            pl.BlockSpec((TOP_K, TM), lambda i: (0, i)),
            pl.BlockSpec((TOP_K, TM), lambda i: (0, i)),
            pl.BlockSpec((TOP_K, TM), lambda i: (0, i)),
            _const_spec((N_EXPERTS, 1)),
            _tile_spec(),
        ),
        scratch_shapes=[pltpu.VMEM((N_EXPERTS, 1), F32)],
        compiler_params=_params("arbitrary"),
        name="moe_router_shared",
    )(x, mod, norm_ffn.reshape(1, -1), w_router.T, b_router.reshape(N_EXPERTS, 1),
      w_sh_gate.astype(BF16), w_sh_up.astype(BF16), w_sh_down.astype(BF16))


def _expert_body(be_ref, nv_ref, nu_ref, xs_ref, wg_ref, wu_ref, wd_ref, ys_ref):
    j = pl.program_id(0)

    @pl.when(j < nu_ref[0])
    def _():
        live = lax.broadcasted_iota(jnp.int32, (TM, 1), 0) < nv_ref[j]
        xb = jnp.where(live, xs_ref[...], 0.0).astype(BF16)
        hg = _bdot(xb, wg_ref[...].astype(BF16))
        hu = _bdot(xb, wu_ref[...].astype(BF16))
        ys_ref[...] = _bdot((_silu(hg) * hu).astype(BF16), wd_ref[...].astype(BF16))


def _experts(block_expert, block_rows, n_used, xs, w_gate, w_up, w_down):
    def row_map(j, be, nv, nu):
        return (jnp.minimum(j, nu[0] - 1), 0)

    def w_map(j, be, nv, nu):
        return (be[jnp.minimum(j, nu[0] - 1)], 0, 0)

    return pl.pallas_call(
        _expert_body,
        out_shape=jax.ShapeDtypeStruct((P_ROWS, D_MODEL), F32),
        grid_spec=pltpu.PrefetchScalarGridSpec(
            num_scalar_prefetch=3,
            grid=(NB,),
            in_specs=[
                pl.BlockSpec((TM, D_MODEL), row_map),
                pl.BlockSpec((None, D_MODEL, D_EXPERT), w_map),
                pl.BlockSpec((None, D_MODEL, D_EXPERT), w_map),
                pl.BlockSpec((None, D_EXPERT, D_MODEL), w_map),
            ],
            out_specs=pl.BlockSpec((TM, D_MODEL), row_map),
        ),
        compiler_params=_params("arbitrary"),
        name="moe_experts",
    )(block_expert, block_rows, n_used, xs, w_gate, w_up, w_down)


def _positions_body(start_ref, eidx_ref, rank_ref, pos_ref):
    eidx = eidx_ref[...]
    base = jnp.zeros(eidx.shape, jnp.int32)
    for e in range(N_EXPERTS):
        base = jnp.where(eidx == e, start_ref[e], base)
    pos_ref[...] = base * TM + rank_ref[...]


def _positions(blk_start, eidx_t, rank_t):
    full = pl.BlockSpec((TOP_K, T), lambda i, s: (0, 0))
    return pl.pallas_call(
        _positions_body,
        out_shape=jax.ShapeDtypeStruct((TOP_K, T), jnp.int32),
        grid_spec=pltpu.PrefetchScalarGridSpec(
            num_scalar_prefetch=1, grid=(1,), in_specs=[full, full], out_specs=full),
        compiler_params=_params("arbitrary"),
        name="moe_positions",
    )(blk_start, eidx_t, rank_t)


def _sc_mesh():
    return plsc.VectorSubcoreMesh(core_axis_name="c", subcore_axis_name="s")


def _sc_worker():
    return lax.axis_index("s") * SC_CORES + lax.axis_index("c")


def _sc_scatter_rows(h, pos_rows):
    c = SC_CHUNK
    n_chunks = T // SC_WORKERS // c

    @functools.partial(
        pl.kernel, mesh=_sc_mesh(),
        out_type=jax.ShapeDtypeStruct((P_ROWS, D_MODEL), h.dtype),
        scratch_types=[pltpu.VMEM((TOP_K, c), jnp.int32), pltpu.VMEM((c, D_MODEL), h.dtype)],
        name="moe_dispatch_scatter",
    )
    def scatter(h_hbm, pos_hbm, xs_hbm, idx_v, rows_v):
        wid = _sc_worker()

        @pl.loop(0, n_chunks)
        def _(i):
            g = wid * n_chunks + i
            pltpu.sync_copy(pos_hbm.at[pl.ds(g * TOP_K, TOP_K)], idx_v)
            pltpu.sync_copy(h_hbm.at[pl.ds(g * c, c)], rows_v)
            for k in range(TOP_K):
                pltpu.sync_copy(rows_v, xs_hbm.at[idx_v.at[k]])

    return scatter(h, pos_rows)


def _sc_gather_rows(table, idx):
    c = SC_CHUNK
    n = idx.shape[0]
    n_chunks = n // SC_WORKERS // c

    @functools.partial(
        pl.kernel, mesh=_sc_mesh(),
        out_type=jax.ShapeDtypeStruct((n, table.shape[1]), table.dtype),
        scratch_types=[pltpu.VMEM((c,), jnp.int32), pltpu.VMEM((c, table.shape[1]), table.dtype)],
        name="moe_combine_gather",
    )
    def gather(table_hbm, idx_hbm, out_hbm, idx_v, rows_v):
        wid = _sc_worker()

        @pl.loop(0, n_chunks)
        def _(i):
            off = (wid * n_chunks + i) * c
            pltpu.sync_copy(idx_hbm.at[pl.ds(off, c)], idx_v)
            pltpu.sync_copy(table_hbm.at[idx_v], rows_v)
            pltpu.sync_copy(rows_v, out_hbm.at[pl.ds(off, c)])

    return gather(table, idx)


def _moe_routed(h, eidx_t, rank_t, counts, w_gate, w_up, w_down):
    counts = counts.reshape(N_EXPERTS)
    nblk = (counts + TM - 1) // TM
    blk_end = jnp.cumsum(nblk)
    blk_start = (blk_end - nblk).astype(jnp.int32)
    blocks = jnp.arange(NB, dtype=jnp.int32)
    block_expert = jnp.minimum(jnp.searchsorted(blk_end, blocks, side="right"),
                               N_EXPERTS - 1).astype(jnp.int32)
    block_rows = jnp.clip(counts[block_expert] - (blocks - blk_start[block_expert]) * TM, 0, TM)
    n_used = blk_end[-1].reshape(1).astype(jnp.int32)
    pos = _positions(blk_start, eidx_t, rank_t)
    pos_rows = pos.reshape(TOP_K, T // SC_CHUNK, SC_CHUNK).transpose(1, 0, 2).reshape(-1, SC_CHUNK)
    xs = _sc_scatter_rows(h, pos_rows)
    ys = _experts(block_expert, block_rows.astype(jnp.int32), n_used, xs, w_gate, w_up, w_down)
    return _sc_gather_rows(ys, pos.reshape(-1)).reshape(TOP_K, T, D_MODEL)


N_SLABS = (3 * C_FDIM + 2 * D_MODEL) // 128


def _combine(yg_ref, w8_ref):
    w8 = w8_ref[...]
    routed = w8[:, 0:1] * yg_ref[0]
    for k in range(1, TOP_K):
        routed = routed + w8[:, k:k + 1] * yg_ref[k]
    return routed


def _hgrn_in_body(acc_ref, yg_ref, w8_ref, mod0_ref, mod1_ref, nmix_ref, win_ref, x_ref, z_ref):
    x = acc_ref[...] + mod0_ref[5:6, :] * _combine(yg_ref, w8_ref)
    x_ref[...] = x
    hb = _modulate(x, nmix_ref[...], mod1_ref[0:1, :], mod1_ref[1:2, :]).astype(BF16)
    for s in range(N_SLABS // C_HEADS):
        zz = _bdot(hb, win_ref[:, s * D_MODEL:(s + 1) * D_MODEL])
        for hh in range(C_HEADS):
            z_ref[s * C_HEADS + hh] = zz[:, hh * 128:(hh + 1) * 128]


def _hgrn_in(acc, yg, w8, mod0, mod1, norm_mix, w_in):
    return pl.pallas_call(
        _hgrn_in_body,
        out_shape=(
            jax.ShapeDtypeStruct((T, D_MODEL), F32),
            jax.ShapeDtypeStruct((N_SLABS, T, 128), F32),
        ),
        grid=(NT,),
        in_specs=[
            _tile_spec(),
            pl.BlockSpec((TOP_K, TM, D_MODEL), lambda i: (0, i, 0)),
            pl.BlockSpec((TM, TOP_K), lambda i: (i, 0)),
            _mod_spec(), _mod_spec(),
            _const_spec((1, D_MODEL)),
            _const_spec((D_MODEL, 3 * C_FDIM + 2 * D_MODEL)),
        ],
        out_specs=(
            _tile_spec(),
            pl.BlockSpec((N_SLABS, TM, 128), lambda i: (0, i, 0)),
        ),
        compiler_params=_params("parallel"),
        name="hgrn_in_proj",
    )(acc, yg, w8, mod0, mod1, norm_mix.reshape(1, -1), w_in.astype(BF16))


def _gla_body(q_ref, f_ref, v_ref, lb_ref, s0_ref, o_ref, ns_ref, st_ref, *, rev):
    j = pl.program_id(0)
    ti = NT - 1 - j if rev else j
    is_ctx = ti < NT_P
    first_lat = (ti - NT_P) % TILES_PER_LAT == (TILES_PER_LAT - 1 if rev else 0)

    @pl.when(is_ctx)
    def _():
        st_ref[...] = jnp.zeros(st_ref.shape, F32)

    @pl.when(jnp.logical_and(jnp.logical_not(is_ctx), first_lat))
    def _():
        st_ref[...] = s0_ref[...]

    row = lax.broadcasted_iota(jnp.int32, (SCAN_CHUNK, SCAN_CHUNK), 0)
    col = lax.broadcasted_iota(jnp.int32, (SCAN_CHUNK, SCAN_CHUNK), 1)
    seen = (col >= row) if rev else (col <= row)
    cum_w = seen.astype(BF16)
    mid = SCAN_CHUNK // 2 if rev else SCAN_CHUNK // 2 - 1
    last = 0 if rev else SCAN_CHUNK - 1
    n_chunks = TM // SCAN_CHUNK
    order = range(n_chunks - 1, -1, -1) if rev else range(n_chunks)

    def head(hd, carry):
        lb = lb_ref[hd]
        qs = _silu(q_ref[hd]) * (C_DK ** -0.5)
        fg = lb + (1.0 - lb) * jax.nn.sigmoid(f_ref[hd])
        kk = 1.0 - fg
        lf = jnp.log(fg)
        vv = v_ref[hd].astype(BF16)
        st = st_ref[hd]
        for c in order:
            sl = slice(c * SCAN_CHUNK, (c + 1) * SCAN_CHUNK)
            g = lf[sl]
            g_hi = g.astype(BF16)
            r1 = g - g_hi.astype(F32)
            g_mid = r1.astype(BF16)
            g_lo = (r1 - g_mid.astype(F32)).astype(BF16)
            bcum = _bdot(cum_w, g_hi) + _bdot(cum_w, g_mid) + _bdot(cum_w, g_lo)
            b_mid = bcum[mid:mid + 1, :]
            b_last = bcum[last:last + 1, :]
            qc, kc, vc = qs[sl], kk[sl], vv[sl]
            qe = (qc * jnp.exp(bcum - b_mid)).astype(BF16)
            ke = (kc * jnp.exp(b_mid - bcum)).astype(BF16)
            att = jnp.where(seen, _dot_nt(qe, ke), 0.0)
            o = _bdot(att.astype(BF16), vc)
            q_dec = (qc * jnp.exp(bcum)).astype(BF16)
            k_dec = (kc * jnp.exp(b_last - bcum)).astype(BF16)
            o = o + _dot_nt(q_dec, st.astype(BF16))
            st = jnp.exp(b_last) * st + _dot_tn(vc, k_dec)
            o_ref[hd, pl.ds(c * SCAN_CHUNK, SCAN_CHUNK), :] = o
        st_ref[hd] = st
        return carry

    lax.fori_loop(0, C_HEADS, head, 0)

    @pl.when(is_ctx)
    def _():
        ns_ref[...] = st_ref[...]


def _gla(z3, lb_dir, s0t_dir, *, rev):
    def ti_of(j):
        return NT - 1 - j if rev else j

    f_slab = 2 if rev else 1

    def lat_map(j):
        return (jnp.clip((ti_of(j) - NT_P) // TILES_PER_LAT, 0, DEC_BATCH - 1), 0, 0, 0)

    return pl.pallas_call(
        functools.partial(_gla_body, rev=rev),
        out_shape=(
            jax.ShapeDtypeStruct((C_HEADS, T, C_DV), F32),
            jax.ShapeDtypeStruct((BATCH, C_HEADS, C_DV, C_DK), F32),
        ),
        grid=(NT,),
        in_specs=[
            pl.BlockSpec((C_HEADS, TM, 128), lambda j: (0, ti_of(j), 0)),
            pl.BlockSpec((C_HEADS, TM, 128), lambda j: (f_slab, ti_of(j), 0)),
            pl.BlockSpec((C_HEADS, TM, 128), lambda j: (3, ti_of(j), 0)),
            _const_spec((C_HEADS, 1, C_DK)),
            pl.BlockSpec((None, C_HEADS, C_DV, C_DK), lat_map),
        ],
        out_specs=(
            pl.BlockSpec((C_HEADS, TM, C_DV), lambda j: (0, ti_of(j), 0)),
            pl.BlockSpec((None, C_HEADS, C_DV, C_DK),
                         lambda j: (jnp.minimum(ti_of(j), NT_P - 1), 0, 0, 0)),
        ),
        scratch_shapes=[pltpu.VMEM((C_HEADS, C_DV, C_DK), F32)],
        compiler_params=_params("arbitrary"),
        name="gla_bwd" if rev else "gla_fwd",
    )(z3, z3, z3, lb_dir, s0t_dir)


def _hgrn_out_body(ofw_ref, obw_ref, gate_ref, x_ref, mod_ref, no_ref, wout_ref, o_ref, cat_ref):
    for hd in range(C_HEADS):
        o = ofw_ref[hd] + obw_ref[hd]
        cat_ref[:, hd * C_DV:(hd + 1) * C_DV] = (_rms(o, no_ref[...]) * _silu(gate_ref[hd])).astype(BF16)
    o_ref[...] = x_ref[...] + mod_ref[2:3, :] * _bdot(cat_ref[...], wout_ref[...])


def _hgrn_out(o_fw, o_bw, z3, x, mod, norm_o, w_out):
    head_spec = pl.BlockSpec((C_HEADS, TM, C_DV), lambda i: (0, i, 0))
    return pl.pallas_call(
        _hgrn_out_body,
        out_shape=jax.ShapeDtypeStruct((T, D_MODEL), F32),
        grid=(NT,),
        in_specs=[
            head_spec, head_spec,
            pl.BlockSpec((C_HEADS, TM, 128), lambda i: (4, i, 0)),
            _tile_spec(), _mod_spec(),
            _const_spec((1, C_DV)),
            _const_spec((D_MODEL, D_MODEL)),
        ],
        out_specs=_tile_spec(),
        scratch_shapes=[pltpu.VMEM((TM, D_MODEL), BF16)],
        compiler_params=_params("parallel"),
        name="hgrn_out_proj",
    )(o_fw, o_bw, z3, x, mod, norm_o.reshape(1, -1), w_out.astype(BF16))


def _final_body(acc_ref, yg_ref, w8_ref, mod_ref, nf_ref, o_ref):
    x = acc_ref[...] + mod_ref[5:6, :] * _combine(yg_ref, w8_ref)
    o_ref[...] = _rms(x, nf_ref[...])


def _final(acc, yg, w8, mod, norm_final, tile0, n_tiles):
    in_tile = pl.BlockSpec((TM, D_MODEL), lambda i: (i + tile0, 0))
    return pl.pallas_call(
        _final_body,
        out_shape=jax.ShapeDtypeStruct((n_tiles * TM, D_MODEL), F32),
        grid=(n_tiles,),
        in_specs=[
            in_tile,
            pl.BlockSpec((TOP_K, TM, D_MODEL), lambda i: (0, i + tile0, 0)),
            pl.BlockSpec((TM, TOP_K), lambda i: (i + tile0, 0)),
            pl.BlockSpec((None, 6, D_MODEL), lambda i: (_mod_row(i + tile0), 0, 0)),
            _const_spec((1, D_MODEL)),
        ],
        out_specs=pl.BlockSpec((TM, D_MODEL), lambda i: (i, 0)),
        compiler_params=_params("parallel"),
        name="final_norm",
    )(acc, yg, w8, mod, norm_final.reshape(1, -1))


def kernel(x_prompt, x_sample, state_hgrn, c, c_ctx, w_ada, b_ada, norm_mix, norm_ffn, w_out, w_in_ab, w_sg, b_sg, norm_sg, w_dw, b_dw, norm_cv, w_in_hgrn, lb_raw, norm_o, w_router, b_router, w_gate, w_up, w_down, w_sh_gate, w_sh_up, w_sh_down, norm_final):
    x = jnp.concatenate([x_prompt.reshape(T_P, D_MODEL), x_sample.reshape(T_S, D_MODEL)], axis=0)
    cvecs = jnp.concatenate(
        [c_ctx.reshape(1, D_MODEL), c, jnp.zeros((N_MOD_ROWS - 1 - DEC_BATCH, D_MODEL), F32)], axis=0)
    mods = _ada_tables(cvecs, w_ada, b_ada)
    lb_sm = jax.nn.softmax(lb_raw.astype(F32), axis=0)
    lb1 = (jnp.cumsum(lb_sm, axis=0) - lb_sm[0])[1].reshape(2, C_HEADS, 1, C_DK)

    def moe(l, xin):
        h, eidx_t, w8_t, rank_t, counts, acc = _moe_pre(
            xin, mods[l], norm_ffn[l], w_router[l], b_router[l], w_sh_gate[l], w_sh_up[l], w_sh_down[l])
        return acc, _moe_routed(h, eidx_t, rank_t, counts, w_gate[l], w_up[l], w_down[l]), w8_t.T

    x = _l0_mixer(x, mods[0], norm_mix[0], w_in_ab[0], w_sg[0], b_sg[0], norm_sg[0], w_dw[0],
                  b_dw[0], norm_cv[0], w_out[0])
    acc, yg, w8 = moe(0, x)
    x, z3 = _hgrn_in(acc, yg, w8, mods[0], mods[1], norm_mix[1], w_in_hgrn[0])
    s0t = jnp.swapaxes(state_hgrn[:, 0].astype(F32), -1, -2)
    o_fw, ns_fw = _gla(z3, lb1[0], s0t[:, 0], rev=False)
    o_bw, ns_bw = _gla(z3, lb1[1], s0t[:, 1], rev=True)
    x = _hgrn_out(o_fw, o_bw, z3, x, mods[1], norm_o[0], w_out[1])
    acc, yg, w8 = moe(1, x)
    y_p = _final(acc, yg, w8, mods[1], norm_final, 0, NT_P).reshape(BATCH, SEQ, D_MODEL)
    y_s = _final(acc, yg, w8, mods[1], norm_final, NT_P, NT_S).reshape(DEC_BATCH, DEC_SEQ, D_MODEL)
    new_state = jnp.swapaxes(jnp.stack([ns_fw, ns_bw], axis=1), -1, -2)[:, None]
    return (y_p, y_s, new_state)
```

```python
import functools

import jax
import jax.numpy as jnp
from jax import lax
from jax.experimental import pallas as pl
from jax.experimental.pallas import tpu as pltpu
from jax.experimental.pallas import tpu_sc as plsc

F32 = jnp.float32
BF16 = jnp.bfloat16
HIGHEST = lax.Precision.HIGHEST

D_MODEL = 1024
BATCH = 32
SEQ = 256
DEPTH = 2
DEC_BATCH = 8
DEC_SEQ = 2048
GRID_W = 64
A_WIDTH = D_MODEL // 2
A_GROUPS = 4
A_GC = A_WIDTH // A_GROUPS
CHUNK_A = 128
B_WIDTH = D_MODEL - A_WIDTH
CONV_W = 31
CONV_PAD = CONV_W // 2
C_HEADS = 8
C_DK = 128
C_DV = D_MODEL // C_HEADS
C_FDIM = C_HEADS * C_DK
SCAN_CHUNK = 64
N_EXPERTS = 64
TOP_K = 8
N_GROUPS = 8
GROUP_SIZE = N_EXPERTS // N_GROUPS
TOPK_GROUPS = 4
D_EXPERT = 256
D_SHARED = 256
ROUTED_SCALE = 2.5
EPS = 1e-6

TM = 256
T_P = BATCH * SEQ
T_S = DEC_BATCH * DEC_SEQ
T = T_P + T_S
NT_P = T_P // TM
NT_S = T_S // TM
NT = NT_P + NT_S
TILES_PER_LAT = DEC_SEQ // TM
N_PAIRS = T * TOP_K
TE = 512
NB = N_PAIRS // TE + N_EXPERTS
P_ROWS = NB * TE
D_PACK = D_MODEL // 2
N_MOD_ROWS = 16
CONV_HALO = 16
VMEM_LIMIT = 48 * 1024 * 1024
SC_CORES = 2
SC_SUBCORES = 16
SC_WORKERS = SC_CORES * SC_SUBCORES
SC_CHUNK = 64


def _mod_row(i):
    return jnp.where(i < NT_P, 0, 1 + (i - NT_P) // TILES_PER_LAT)


def _silu(x):
    return x * jax.nn.sigmoid(x)


def _gelu(x):
    return x * (0.5 * (1.0 + jnp.tanh(0.7978845608028654 * (x + 0.044715 * (x * x * x)))))


def _rms(x, g):
    return x * lax.rsqrt(jnp.mean(x * x, axis=-1, keepdims=True) + EPS) * g


def _layernorm(x, g):
    xc = x - jnp.mean(x, axis=-1, keepdims=True)
    return xc * lax.rsqrt(jnp.mean(xc * xc, axis=-1, keepdims=True) + EPS) * g


def _modulate(x, g, shift, scale):
    return _rms(x, g) * (1.0 + scale) + shift


def _bdot(a, b):
    return jnp.dot(a, b, preferred_element_type=F32)


def _dot_nt(a, b, precision=None):
    return lax.dot_general(a, b, (((1,), (1,)), ((), ())), precision=precision,
                           preferred_element_type=F32)


def _dot_tn(a, b):
    return lax.dot_general(a, b, (((0,), (0,)), ((), ())), preferred_element_type=F32)


def _pack_pairs(x):
    m = x.shape[1] // 2
    lo = lax.bitcast_convert_type(x[:, :m].astype(BF16).astype(F32), jnp.uint32)
    hi = lax.bitcast_convert_type(x[:, m:].astype(BF16).astype(F32), jnp.uint32)
    return lax.bitcast_convert_type(hi | (lo >> 16), jnp.int32)


def _unpack_pairs(w):
    u = lax.bitcast_convert_type(w, jnp.uint32)
    lo = lax.bitcast_convert_type(u << 16, F32)
    hi = lax.bitcast_convert_type(u & jnp.uint32(0xFFFF0000), F32)
    return lo, hi


def _params(*sem):
    return pltpu.CompilerParams(dimension_semantics=sem, vmem_limit_bytes=VMEM_LIMIT)


def _const_spec(shape):
    nd = len(shape)
    return pl.BlockSpec(shape, lambda *_: (0,) * nd)


def _ada_body(c_ref, w_ref, b_ref, o_ref):
    s = _silu(c_ref[...])
    o_ref[...] = jnp.dot(s, w_ref[...], precision=HIGHEST, preferred_element_type=F32) + b_ref[...]


def _ada_tables(cvecs, w_ada, b_ada):
    out = pl.pallas_call(
        _ada_body,
        out_shape=jax.ShapeDtypeStruct((DEPTH, N_MOD_ROWS, 6 * D_MODEL), F32),
        grid=(DEPTH, 6),
        in_specs=[
            _const_spec((N_MOD_ROWS, D_MODEL)),
            pl.BlockSpec((None, D_MODEL, D_MODEL), lambda l, j: (l, 0, j)),
            pl.BlockSpec((None, 1, D_MODEL), lambda l, j: (l, 0, j)),
        ],
        out_specs=pl.BlockSpec((None, N_MOD_ROWS, D_MODEL), lambda l, j: (l, 0, j)),
        compiler_params=_params("parallel", "parallel"),
        name="ada_tables",
    )(cvecs, w_ada, b_ada.reshape(DEPTH, 1, 6 * D_MODEL))
    return out.reshape(DEPTH, N_MOD_ROWS, 6, D_MODEL)


def _mod_spec():
    return pl.BlockSpec((None, 6, D_MODEL), lambda i: (_mod_row(i), 0, 0))


def _tile_spec():
    return pl.BlockSpec((TM, D_MODEL), lambda i: (i, 0))


def _conv_segment(pad_ref, conv_ref, wdw_ref, pad_base, out_base, seg):
    rb = min(seg, 64)
    for cb in range(B_WIDTH // 128):
        cs = slice(cb * 128, (cb + 1) * 128)
        for r0 in range(0, seg, rb):
            acc = jnp.zeros((rb, 128), F32)
            for k in range(CONV_W):
                off = pad_base + CONV_HALO - CONV_PAD + k + r0
                acc = acc + wdw_ref[k:k + 1, cs] * pad_ref[off:off + rb, cs]
            conv_ref[out_base + r0:out_base + r0 + rb, cs] = acc


def _l0_body(x_ref, mod_ref, nmix_ref, win_ref, wsg_ref, bsg_ref, nsg_ref, wdw_ref, bdw_ref,
             ncv_ref, wout_ref, o_ref, cat_ref, pad_ref, conv_ref):
    i = pl.program_id(0)
    x = x_ref[...]
    h = _modulate(x, nmix_ref[...], mod_ref[0:1, :], mod_ref[1:2, :])
    z = _bdot(h.astype(BF16), win_ref[...])
    u = _gelu(z[:, :A_WIDTH])
    vb = _layernorm(_gelu(z[:, A_WIDTH:2 * A_WIDTH]), nsg_ref[...]).astype(BF16)
    for n in range(TM // CHUNK_A):
        rs = slice(n * CHUNK_A, (n + 1) * CHUNK_A)
        for g in range(A_GROUPS):
            cs = slice(g * A_GC, (g + 1) * A_GC)
            m = _bdot(wsg_ref[g], vb[rs, cs]) + bsg_ref[g]
            cat_ref[rs, cs] = (u[rs, cs] * m).astype(BF16)
    hb = z[:, 2 * A_WIDTH:2 * A_WIDTH + B_WIDTH] * jax.nn.sigmoid(z[:, 2 * A_WIDTH + B_WIDTH:])

    def conv_tile(seg):
        stride = seg + 2 * CONV_HALO
        halo = jnp.zeros((CONV_HALO, B_WIDTH), F32)
        for s in range(TM // seg):
            b = s * stride
            pad_ref[b:b + CONV_HALO, :] = halo
            pad_ref[b + CONV_HALO:b + CONV_HALO + seg, :] = hb[s * seg:(s + 1) * seg, :]
            pad_ref[b + CONV_HALO + seg:b + stride, :] = halo
        for s in range(TM // seg):
            _conv_segment(pad_ref, conv_ref, wdw_ref, s * stride, s * seg, seg)

    @pl.when(i < NT_P)
    def _():
        conv_tile(SEQ)

    @pl.when(i >= NT_P)
    def _():
        conv_tile(GRID_W)

    yb = _layernorm(conv_ref[...] + bdw_ref[...], ncv_ref[...])
    cat_ref[:, A_WIDTH:] = _silu(yb).astype(BF16)
    out = _bdot(cat_ref[...], wout_ref[...])
    o_ref[...] = x + mod_ref[2:3, :] * out


def _l0_mixer(x, mod, norm_mix, w_in, w_sg, b_sg, norm_sg, w_dw, b_dw, norm_cv, w_out):
    assert TM == SEQ and TM % GRID_W == 0 and TM % CHUNK_A == 0
    pad_rows = (TM // GRID_W) * (GRID_W + 2 * CONV_HALO)
    assert pad_rows >= SEQ + 2 * CONV_HALO
    return pl.pallas_call(
        _l0_body,
        out_shape=jax.ShapeDtypeStruct((T, D_MODEL), F32),
        grid=(NT,),
        in_specs=[
            _tile_spec(), _mod_spec(),
            _const_spec((1, D_MODEL)),
            _const_spec((D_MODEL, 2 * A_WIDTH + 2 * B_WIDTH)),
            _const_spec((A_GROUPS, CHUNK_A, CHUNK_A)),
            _const_spec((A_GROUPS, CHUNK_A, 1)),
            _const_spec((1, A_WIDTH)),
            _const_spec((CONV_W, B_WIDTH)),
            _const_spec((1, B_WIDTH)),
            _const_spec((1, B_WIDTH)),
            _const_spec((D_MODEL, D_MODEL)),
        ],
        out_specs=_tile_spec(),
        scratch_shapes=[
            pltpu.VMEM((TM, D_MODEL), BF16),
            pltpu.VMEM((pad_rows, B_WIDTH), F32),
            pltpu.VMEM((TM, B_WIDTH), F32),
        ],
        compiler_params=_params("parallel"),
        name="l0_mixer",
    )(x, mod, norm_mix.reshape(1, -1), w_in.astype(BF16), w_sg.astype(BF16),
      b_sg.reshape(A_GROUPS, CHUNK_A, 1), norm_sg.reshape(1, -1), w_dw, b_dw.reshape(1, -1),
      norm_cv.reshape(1, -1), w_out.astype(BF16))


def _route(scores, biased):
    n = scores.shape[-1]
    shp = (N_GROUPS, GROUP_SIZE, n)
    s3 = scores.reshape(shp)
    b3 = biased.reshape(shp)
    m_iota = lax.broadcasted_iota(jnp.int32, shp, 1).astype(F32)
    g_iota = lax.broadcasted_iota(jnp.int32, shp, 0).astype(F32)
    e_iota = g_iota * GROUP_SIZE + m_iota
    neg = -jnp.inf

    def amax1(v):
        return jnp.max(v, axis=1, keepdims=True)

    def amin1(v):
        return jnp.min(v, axis=1, keepdims=True)

    m1 = amax1(b3)
    i1 = amin1(jnp.where(b3 == m1, m_iota, float(GROUP_SIZE)))
    m2 = amax1(jnp.where(m_iota == i1, neg, b3))
    grp = m1 + m2
    gi1 = lax.broadcasted_iota(jnp.int32, grp.shape, 0).astype(F32)
    gmask = jnp.zeros(grp.shape, jnp.bool_)
    for _ in range(TOPK_GROUPS):
        gm = jnp.max(grp, axis=0, keepdims=True)
        gi = jnp.min(jnp.where(grp == gm, gi1, float(N_GROUPS)), axis=0, keepdims=True)
        hit = gi1 == gi
        gmask = jnp.logical_or(gmask, hit)
        grp = jnp.where(hit, neg, grp)
    cand = jnp.where(gmask, b3, neg)
    ids, vals, hits = [], [], []
    for _ in range(TOP_K):
        mx = jnp.max(amax1(cand), axis=0, keepdims=True)
        ei = jnp.min(amin1(jnp.where(cand == mx, e_iota, float(N_EXPERTS))), axis=0, keepdims=True)
        hit = e_iota == ei
        ids.append(ei.reshape(1, n))
        vals.append(_pick(hit, s3))
        hits.append(hit)
        cand = jnp.where(hit, neg, cand)
    return jnp.concatenate(ids, axis=0).astype(jnp.int32), jnp.concatenate(vals, axis=0), hits


def _pick(hit, v3):
    s = jnp.sum(jnp.sum(jnp.where(hit, v3, 0.0), axis=1, keepdims=True), axis=0, keepdims=True)
    return s.reshape(1, v3.shape[-1])


def _moe_pre_body(x_ref, mod_ref, nffn_ref, wrt_ref, br_ref, wsg_ref, wsu_ref, wsd_ref,
                  h_ref, eidx_ref, w8_ref, rank_ref, cnt_ref, acc_ref, run_ref):
    @pl.when(pl.program_id(0) == 0)
    def _():
        run_ref[...] = jnp.zeros(run_ref.shape, F32)

    x = x_ref[...]
    h = _modulate(x, nffn_ref[...], mod_ref[3:4, :], mod_ref[4:5, :])
    h_ref[...] = _pack_pairs(h)
    logits_t = _dot_nt(wrt_ref[...], h, precision=HIGHEST)
    scores = jax.nn.sigmoid(logits_t)
    eidx, sv, hits = _route(scores, scores + br_ref[...])
    eidx_ref[...] = eidx
    w8_ref[...] = sv / jnp.sum(sv, axis=0, keepdims=True) * ROUTED_SCALE
    sel3 = hits[0]
    for hit in hits[1:]:
        sel3 = jnp.logical_or(sel3, hit)
    sel = sel3.astype(F32).reshape(N_EXPERTS, TM)
    earlier = (lax.broadcasted_iota(jnp.int32, (TM, TM), 0)
               < lax.broadcasted_iota(jnp.int32, (TM, TM), 1)).astype(BF16)
    rank3 = (_bdot(sel.astype(BF16), earlier) + run_ref[...]).reshape(N_GROUPS, GROUP_SIZE, TM)
    rank_ref[...] = jnp.concatenate([_pick(hit, rank3) for hit in hits], axis=0).astype(jnp.int32)
    run_ref[...] = run_ref[...] + jnp.sum(sel, axis=1, keepdims=True)
    cnt_ref[...] = run_ref[...].astype(jnp.int32)
    hb = h.astype(BF16)
    sh = _bdot((_silu(_bdot(hb, wsg_ref[...])) * _bdot(hb, wsu_ref[...])).astype(BF16), wsd_ref[...])
    acc_ref[...] = x + mod_ref[5:6, :] * sh


def _moe_pre(x, mod, norm_ffn, w_router, b_router, w_sh_gate, w_sh_up, w_sh_down):
    return pl.pallas_call(
        _moe_pre_body,
        out_shape=(
            jax.ShapeDtypeStruct((T, D_PACK), jnp.int32),
            jax.ShapeDtypeStruct((TOP_K, T), jnp.int32),
            jax.ShapeDtypeStruct((TOP_K, T), F32),
            jax.ShapeDtypeStruct((TOP_K, T), jnp.int32),
            jax.ShapeDtypeStruct((N_EXPERTS, 1), jnp.int32),
            jax.ShapeDtypeStruct((T, D_MODEL), F32),
        ),
        grid=(NT,),
        in_specs=[
            _tile_spec(), _mod_spec(),
            _const_spec((1, D_MODEL)),
            _const_spec((N_EXPERTS, D_MODEL)),
            _const_spec((N_EXPERTS, 1)),
            _const_spec((D_MODEL, D_SHARED)),
            _const_spec((D_MODEL, D_SHARED)),
            _const_spec((D_SHARED, D_MODEL)),
        ],
        out_specs=(
            pl.BlockSpec((TM, D_PACK), lambda i: (i, 0)),
            pl.BlockSpec((TOP_K, TM), lambda i: (0, i)),
            pl.BlockSpec((TOP_K, TM), lambda i: (0, i)),
            pl.BlockSpec((TOP_K, TM), lambda i: (0, i)),
            _const_spec((N_EXPERTS, 1)),
            _tile_spec(),
        ),
        scratch_shapes=[pltpu.VMEM((N_EXPERTS, 1), F32)],
        compiler_params=_params("arbitrary"),
        name="moe_router_shared",
    )(x, mod, norm_ffn.reshape(1, -1), w_router.T, b_router.reshape(N_EXPERTS, 1),
      w_sh_gate.astype(BF16), w_sh_up.astype(BF16), w_sh_down.astype(BF16))


def _expert_body(be_ref, nv_ref, nu_ref, xs_ref, wg_ref, wu_ref, wd_ref, ys_ref, wgu_s, wd_s):
    j = pl.program_id(0)
    live_tile = j < nu_ref[0]

    @pl.when(jnp.logical_and(live_tile, jnp.logical_or(j == 0, be_ref[j] != be_ref[jnp.maximum(j - 1, 0)])))
    def _():
        wgu_s[:, :D_EXPERT] = wg_ref[...].astype(BF16)
        wgu_s[:, D_EXPERT:] = wu_ref[...].astype(BF16)
        wd_s[...] = wd_ref[...].astype(BF16)

    @pl.when(live_tile)
    def _():
        live = lax.broadcasted_iota(jnp.int32, (TE, 1), 0) < nv_ref[j]
        lo, hi = _unpack_pairs(jnp.where(live, xs_ref[...], 0))
        xb = jnp.concatenate([lo.astype(BF16), hi.astype(BF16)], axis=1)
        hgu = _bdot(xb, wgu_s[...])
        hh = _silu(hgu[:, :D_EXPERT]) * hgu[:, D_EXPERT:]
        ys_ref[...] = _pack_pairs(_bdot(hh.astype(BF16), wd_s[...]))


def _experts(block_expert, block_rows, n_used, xs, w_gate, w_up, w_down):
    def row_map(j, be, nv, nu):
        return (jnp.minimum(j, nu[0] - 1), 0)

    def w_map(j, be, nv, nu):
        return (be[jnp.minimum(j, nu[0] - 1)], 0, 0)

    return pl.pallas_call(
        _expert_body,
        out_shape=jax.ShapeDtypeStruct((P_ROWS, D_PACK), jnp.int32),
        grid_spec=pltpu.PrefetchScalarGridSpec(
            num_scalar_prefetch=3,
            grid=(NB,),
            in_specs=[
                pl.BlockSpec((TE, D_PACK), row_map),
                pl.BlockSpec((None, D_MODEL, D_EXPERT), w_map),
                pl.BlockSpec((None, D_MODEL, D_EXPERT), w_map),
                pl.BlockSpec((None, D_EXPERT, D_MODEL), w_map),
            ],
            out_specs=pl.BlockSpec((TE, D_PACK), row_map),
            scratch_shapes=[
                pltpu.VMEM((D_MODEL, 2 * D_EXPERT), BF16),
                pltpu.VMEM((D_EXPERT, D_MODEL), BF16),
            ],
        ),
        compiler_params=_params("arbitrary"),
        name="moe_experts",
    )(block_expert, block_rows, n_used, xs, w_gate, w_up, w_down)


def _positions_body(start_ref, eidx_ref, rank_ref, pos_ref):
    eidx = eidx_ref[...]
    base = jnp.zeros(eidx.shape, jnp.int32)
    for e in range(N_EXPERTS):
        base = jnp.where(eidx == e, start_ref[e], base)
    pos_ref[...] = base * TE + rank_ref[...]


def _positions(blk_start, eidx_t, rank_t):
    full = pl.BlockSpec((TOP_K, T), lambda i, s: (0, 0))
    return pl.pallas_call(
        _positions_body,
        out_shape=jax.ShapeDtypeStruct((TOP_K, T), jnp.int32),
        grid_spec=pltpu.PrefetchScalarGridSpec(
            num_scalar_prefetch=1, grid=(1,), in_specs=[full, full], out_specs=full),
        compiler_params=_params("arbitrary"),
        name="moe_positions",
    )(blk_start, eidx_t, rank_t)


def _sc_mesh():
    return plsc.VectorSubcoreMesh(core_axis_name="c", subcore_axis_name="s")


def _sc_worker():
    return lax.axis_index("s") * SC_CORES + lax.axis_index("c")


def _sc_scatter_rows(h, pos_rows):
    c = SC_CHUNK
    n_chunks = T // SC_WORKERS // c
    width = h.shape[1]

    @functools.partial(
        pl.kernel, mesh=_sc_mesh(),
        out_type=jax.ShapeDtypeStruct((P_ROWS, width), h.dtype),
        scratch_types=[pltpu.VMEM((n_chunks * TOP_K, c), jnp.int32), pltpu.VMEM((c, width), h.dtype),
                       pltpu.SemaphoreType.DMA],
        name="moe_dispatch_scatter",
    )
    def scatter(h_hbm, pos_hbm, xs_hbm, idx_v, rows_v, sem):
        first = _sc_worker() * n_chunks
        pltpu.sync_copy(pos_hbm.at[pl.ds(first * TOP_K, n_chunks * TOP_K)], idx_v)

        @pl.loop(0, n_chunks)
        def _(i):
            pltpu.sync_copy(h_hbm.at[pl.ds((first + i) * c, c)], rows_v)
            copies = [pltpu.async_copy(rows_v, xs_hbm.at[idx_v.at[i * TOP_K + k]], sem)
                      for k in range(TOP_K)]
            for cp in copies:
                cp.wait()

    return scatter(h, pos_rows)


def _sc_gather_rows(table, idx):
    c = SC_CHUNK
    n = idx.shape[0]
    per_worker = n // SC_WORKERS
    n_chunks = per_worker // c

    @functools.partial(
        pl.kernel, mesh=_sc_mesh(),
        out_type=jax.ShapeDtypeStruct((n, table.shape[1]), table.dtype),
        scratch_types=[pltpu.VMEM((per_worker,), jnp.int32), pltpu.VMEM((c, table.shape[1]), table.dtype)],
        name="moe_combine_gather",
    )
    def gather(table_hbm, idx_hbm, out_hbm, idx_v, rows_v):
        base = _sc_worker() * per_worker
        pltpu.sync_copy(idx_hbm.at[pl.ds(base, per_worker)], idx_v)

        @pl.loop(0, n_chunks)
        def _(i):
            pltpu.sync_copy(table_hbm.at[idx_v.at[pl.ds(i * c, c)]], rows_v)
            pltpu.sync_copy(rows_v, out_hbm.at[pl.ds(base + i * c, c)])

    return gather(table, idx)


def _moe_routed(h, eidx_t, rank_t, counts, w_gate, w_up, w_down):
    counts = counts.reshape(1, N_EXPERTS)
    nblk = (counts + TE - 1) // TE
    blk_end = jnp.cumsum(nblk, axis=1)
    blk_start = blk_end - nblk
    blocks = jnp.arange(NB, dtype=jnp.int32).reshape(NB, 1)
    block_expert = jnp.minimum(jnp.sum(blocks >= blk_end, axis=1, keepdims=True), N_EXPERTS - 1)
    mine = block_expert == jnp.arange(N_EXPERTS, dtype=jnp.int32).reshape(1, N_EXPERTS)
    cnt_b = jnp.sum(jnp.where(mine, counts, 0), axis=1, keepdims=True)
    start_b = jnp.sum(jnp.where(mine, blk_start, 0), axis=1, keepdims=True)
    block_rows = jnp.clip(cnt_b - (blocks - start_b) * TE, 0, TE)
    n_used = blk_end[0, -1].reshape(1).astype(jnp.int32)
    pos = _positions(blk_start.reshape(N_EXPERTS).astype(jnp.int32), eidx_t, rank_t)
    pos_rows = pos.reshape(TOP_K, T // SC_CHUNK, SC_CHUNK).transpose(1, 0, 2).reshape(-1, SC_CHUNK)
    xs = _sc_scatter_rows(h, pos_rows)
    ys = _experts(block_expert.reshape(NB).astype(jnp.int32), block_rows.reshape(NB).astype(jnp.int32),
                  n_used, xs, w_gate, w_up, w_down)
    return _sc_gather_rows(ys, pos.reshape(-1)).reshape(TOP_K, T, D_PACK)


N_SLABS = (3 * C_FDIM + 2 * D_MODEL) // 128


def _combine(yg_ref, w8_ref):
    w8 = w8_ref[...]
    lo, hi = _unpack_pairs(yg_ref[0])
    r_lo, r_hi = w8[:, 0:1] * lo, w8[:, 0:1] * hi
    for k in range(1, TOP_K):
        lo, hi = _unpack_pairs(yg_ref[k])
        r_lo, r_hi = r_lo + w8[:, k:k + 1] * lo, r_hi + w8[:, k:k + 1] * hi
    return jnp.concatenate([r_lo, r_hi], axis=1)


def _hgrn_in_body(acc_ref, yg_ref, w8_ref, mod0_ref, mod1_ref, nmix_ref, win_ref, x_ref, z_ref):
    x = acc_ref[...] + mod0_ref[5:6, :] * _combine(yg_ref, w8_ref)
    x_ref[...] = x
    hb = _modulate(x, nmix_ref[...], mod1_ref[0:1, :], mod1_ref[1:2, :]).astype(BF16)
    for s in range(N_SLABS // C_HEADS):
        zz = _bdot(hb, win_ref[:, s * D_MODEL:(s + 1) * D_MODEL])
        for hh in range(C_HEADS):
            z_ref[s * C_HEADS + hh] = zz[:, hh * 128:(hh + 1) * 128]


def _hgrn_in(acc, yg, w8, mod0, mod1, norm_mix, w_in):
    return pl.pallas_call(
        _hgrn_in_body,
        out_shape=(
            jax.ShapeDtypeStruct((T, D_MODEL), F32),
            jax.ShapeDtypeStruct((N_SLABS, T, 128), F32),
        ),
        grid=(NT,),
        in_specs=[
            _tile_spec(),
            pl.BlockSpec((TOP_K, TM, D_PACK), lambda i: (0, i, 0)),
            pl.BlockSpec((TM, TOP_K), lambda i: (i, 0)),
            _mod_spec(), _mod_spec(),
            _const_spec((1, D_MODEL)),
            _const_spec((D_MODEL, 3 * C_FDIM + 2 * D_MODEL)),
        ],
        out_specs=(
            _tile_spec(),
            pl.BlockSpec((N_SLABS, TM, 128), lambda i: (0, i, 0)),
        ),
        compiler_params=_params("parallel"),
        name="hgrn_in_proj",
    )(acc, yg, w8, mod0, mod1, norm_mix.reshape(1, -1), w_in.astype(BF16))


def _gla_body(q_ref, f_ref, v_ref, lb_ref, s0_ref, o_ref, ns_ref, st_ref, *, rev):
    j = pl.program_id(0)
    ti = NT - 1 - j if rev else j
    is_ctx = ti < NT_P
    first_lat = (ti - NT_P) % TILES_PER_LAT == (TILES_PER_LAT - 1 if rev else 0)

    @pl.when(is_ctx)
    def _():
        st_ref[...] = jnp.zeros(st_ref.shape, F32)

    @pl.when(jnp.logical_and(jnp.logical_not(is_ctx), first_lat))
    def _():
        st_ref[...] = s0_ref[...]

    row = lax.broadcasted_iota(jnp.int32, (SCAN_CHUNK, SCAN_CHUNK), 0)
    col = lax.broadcasted_iota(jnp.int32, (SCAN_CHUNK, SCAN_CHUNK), 1)
    seen = (col >= row) if rev else (col <= row)
    cum_w = seen.astype(BF16)
    mid = SCAN_CHUNK // 2 if rev else SCAN_CHUNK // 2 - 1
    last = 0 if rev else SCAN_CHUNK - 1
    n_chunks = TM // SCAN_CHUNK
    order = range(n_chunks - 1, -1, -1) if rev else range(n_chunks)

    def head(hd, carry):
        lb = lb_ref[hd]
        qs = _silu(q_ref[hd]) * (C_DK ** -0.5)
        fg = lb + (1.0 - lb) * jax.nn.sigmoid(f_ref[hd])
        kk = 1.0 - fg
        lf = jnp.log(fg)
        vv = v_ref[hd].astype(BF16)
        st = st_ref[hd]
        for c in order:
            sl = slice(c * SCAN_CHUNK, (c + 1) * SCAN_CHUNK)
            g = lf[sl]
            g_hi = g.astype(BF16)
            r1 = g - g_hi.astype(F32)
            g_mid = r1.astype(BF16)
            g_lo = (r1 - g_mid.astype(F32)).astype(BF16)
            bcum = _bdot(cum_w, g_hi) + _bdot(cum_w, g_mid) + _bdot(cum_w, g_lo)
            b_mid = bcum[mid:mid + 1, :]
            b_last = bcum[last:last + 1, :]
            qc, kc, vc = qs[sl], kk[sl], vv[sl]
            qe = (qc * jnp.exp(bcum - b_mid)).astype(BF16)
            ke = (kc * jnp.exp(b_mid - bcum)).astype(BF16)
            att = jnp.where(seen, _dot_nt(qe, ke), 0.0)
            o = _bdot(att.astype(BF16), vc)
            q_dec = (qc * jnp.exp(bcum)).astype(BF16)
            k_dec = (kc * jnp.exp(b_last - bcum)).astype(BF16)
            o = o + _dot_nt(q_dec, st.astype(BF16))
            st = jnp.exp(b_last) * st + _dot_tn(vc, k_dec)
            o_ref[hd, pl.ds(c * SCAN_CHUNK, SCAN_CHUNK), :] = o
        st_ref[hd] = st
        return carry

    lax.fori_loop(0, C_HEADS, head, 0)

    @pl.when(is_ctx)
    def _():
        ns_ref[...] = st_ref[...]


def _gla(z3, lb_dir, s0t_dir, *, rev):
    def ti_of(j):
        return NT - 1 - j if rev else j

    f_slab = 2 if rev else 1

    def lat_map(j):
        return (jnp.clip((ti_of(j) - NT_P) // TILES_PER_LAT, 0, DEC_BATCH - 1), 0, 0, 0)

    return pl.pallas_call(
        functools.partial(_gla_body, rev=rev),
        out_shape=(
            jax.ShapeDtypeStruct((C_HEADS, T, C_DV), F32),
            jax.ShapeDtypeStruct((BATCH, C_HEADS, C_DV, C_DK), F32),
        ),
        grid=(NT,),
        in_specs=[
            pl.BlockSpec((C_HEADS, TM, 128), lambda j: (0, ti_of(j), 0)),
            pl.BlockSpec((C_HEADS, TM, 128), lambda j: (f_slab, ti_of(j), 0)),
            pl.BlockSpec((C_HEADS, TM, 128), lambda j: (3, ti_of(j), 0)),
            _const_spec((C_HEADS, 1, C_DK)),
            pl.BlockSpec((None, C_HEADS, C_DV, C_DK), lat_map),
        ],
        out_specs=(
            pl.BlockSpec((C_HEADS, TM, C_DV), lambda j: (0, ti_of(j), 0)),
            pl.BlockSpec((None, C_HEADS, C_DV, C_DK),
                         lambda j: (jnp.minimum(ti_of(j), NT_P - 1), 0, 0, 0)),
        ),
        scratch_shapes=[pltpu.VMEM((C_HEADS, C_DV, C_DK), F32)],
        compiler_params=_params("arbitrary"),
        name="gla_bwd" if rev else "gla_fwd",
    )(z3, z3, z3, lb_dir, s0t_dir)


def _hgrn_out_body(ofw_ref, obw_ref, gate_ref, x_ref, mod_ref, no_ref, wout_ref, o_ref, cat_ref):
    for hd in range(C_HEADS):
        o = ofw_ref[hd] + obw_ref[hd]
        cat_ref[:, hd * C_DV:(hd + 1) * C_DV] = (_rms(o, no_ref[...]) * _silu(gate_ref[hd])).astype(BF16)
    o_ref[...] = x_ref[...] + mod_ref[2:3, :] * _bdot(cat_ref[...], wout_ref[...])


def _hgrn_out(o_fw, o_bw, z3, x, mod, norm_o, w_out):
    head_spec = pl.BlockSpec((C_HEADS, TM, C_DV), lambda i: (0, i, 0))
    return pl.pallas_call(
        _hgrn_out_body,
        out_shape=jax.ShapeDtypeStruct((T, D_MODEL), F32),
        grid=(NT,),
        in_specs=[
            head_spec, head_spec,
            pl.BlockSpec((C_HEADS, TM, 128), lambda i: (4, i, 0)),
            _tile_spec(), _mod_spec(),
            _const_spec((1, C_DV)),
            _const_spec((D_MODEL, D_MODEL)),
        ],
        out_specs=_tile_spec(),
        scratch_shapes=[pltpu.VMEM((TM, D_MODEL), BF16)],
        compiler_params=_params("parallel"),
        name="hgrn_out_proj",
    )(o_fw, o_bw, z3, x, mod, norm_o.reshape(1, -1), w_out.astype(BF16))


def _final_body(acc_ref, yg_ref, w8_ref, mod_ref, nf_ref, o_ref):
    x = acc_ref[...] + mod_ref[5:6, :] * _combine(yg_ref, w8_ref)
    o_ref[...] = _rms(x, nf_ref[...])


def _final(acc, yg, w8, mod, norm_final, tile0, n_tiles):
    in_tile = pl.BlockSpec((TM, D_MODEL), lambda i: (i + tile0, 0))
    return pl.pallas_call(
        _final_body,
        out_shape=jax.ShapeDtypeStruct((n_tiles * TM, D_MODEL), F32),
        grid=(n_tiles,),
        in_specs=[
            in_tile,
            pl.BlockSpec((TOP_K, TM, D_PACK), lambda i: (0, i + tile0, 0)),
            pl.BlockSpec((TM, TOP_K), lambda i: (i + tile0, 0)),
            pl.BlockSpec((None, 6, D_MODEL), lambda i: (_mod_row(i + tile0), 0, 0)),
            _const_spec((1, D_MODEL)),
        ],
        out_specs=pl.BlockSpec((TM, D_MODEL), lambda i: (i, 0)),
        compiler_params=_params("parallel"),
        name="final_norm",
    )(acc, yg, w8, mod, norm_final.reshape(1, -1))


def kernel(x_prompt, x_sample, state_hgrn, c, c_ctx, w_ada, b_ada, norm_mix, norm_ffn, w_out, w_in_ab, w_sg, b_sg, norm_sg, w_dw, b_dw, norm_cv, w_in_hgrn, lb_raw, norm_o, w_router, b_router, w_gate, w_up, w_down, w_sh_gate, w_sh_up, w_sh_down, norm_final):
    x = jnp.concatenate([x_prompt.reshape(T_P, D_MODEL), x_sample.reshape(T_S, D_MODEL)], axis=0)
    cvecs = jnp.concatenate(
        [c_ctx.reshape(1, D_MODEL), c, jnp.zeros((N_MOD_ROWS - 1 - DEC_BATCH, D_MODEL), F32)], axis=0)
    mods = _ada_tables(cvecs, w_ada, b_ada)
    lb_sm = jax.nn.softmax(lb_raw.astype(F32), axis=0)
    lb1 = (jnp.cumsum(lb_sm, axis=0) - lb_sm[0])[1].reshape(2, C_HEADS, 1, C_DK)

    def moe(l, xin):
        h, eidx_t, w8_t, rank_t, counts, acc = _moe_pre(
            xin, mods[l], norm_ffn[l], w_router[l], b_router[l], w_sh_gate[l], w_sh_up[l], w_sh_down[l])
        return acc, _moe_routed(h, eidx_t, rank_t, counts, w_gate[l], w_up[l], w_down[l]), w8_t.T

    x = _l0_mixer(x, mods[0], norm_mix[0], w_in_ab[0], w_sg[0], b_sg[0], norm_sg[0], w_dw[0],
                  b_dw[0], norm_cv[0], w_out[0])
    acc, yg, w8 = moe(0, x)
    x, z3 = _hgrn_in(acc, yg, w8, mods[0], mods[1], norm_mix[1], w_in_hgrn[0])
    s0t = jnp.swapaxes(state_hgrn[:, 0].astype(F32), -1, -2)
    o_fw, ns_fw = _gla(z3, lb1[0], s0t[:, 0], rev=False)
    o_bw, ns_bw = _gla(z3, lb1[1], s0t[:, 1], rev=True)
    x = _hgrn_out(o_fw, o_bw, z3, x, mods[1], norm_o[0], w_out[1])
    acc, yg, w8 = moe(1, x)
    y_p = _final(acc, yg, w8, mods[1], norm_final, 0, NT_P).reshape(BATCH, SEQ, D_MODEL)
    y_s = _final(acc, yg, w8, mods[1], norm_final, NT_P, NT_S).reshape(DEC_BATCH, DEC_SEQ, D_MODEL)
    new_state = jnp.swapaxes(jnp.stack([ns_fw, ns_bw], axis=1), -1, -2)[:, None]
    return (y_p, y_s, new_state)
```

```python
import functools

import jax
import jax.numpy as jnp
from jax import lax
from jax.experimental import pallas as pl
from jax.experimental.pallas import tpu as pltpu
from jax.experimental.pallas import tpu_sc as plsc

F32 = jnp.float32
BF16 = jnp.bfloat16
HIGHEST = lax.Precision.HIGHEST

D_MODEL = 1024
BATCH = 32
SEQ = 256
DEPTH = 2
DEC_BATCH = 8
DEC_SEQ = 2048
GRID_W = 64
A_WIDTH = D_MODEL // 2
A_GROUPS = 4
A_GC = A_WIDTH // A_GROUPS
CHUNK_A = 128
B_WIDTH = D_MODEL - A_WIDTH
CONV_W = 31
CONV_PAD = CONV_W // 2
C_HEADS = 8
C_DK = 128
C_DV = D_MODEL // C_HEADS
C_FDIM = C_HEADS * C_DK
SCAN_CHUNK = 64
N_EXPERTS = 64
TOP_K = 8
N_GROUPS = 8
GROUP_SIZE = N_EXPERTS // N_GROUPS
TOPK_GROUPS = 4
D_EXPERT = 256
D_SHARED = 256
ROUTED_SCALE = 2.5
EPS = 1e-6

TM = 256
T_P = BATCH * SEQ
T_S = DEC_BATCH * DEC_SEQ
T = T_P + T_S
NT_P = T_P // TM
NT_S = T_S // TM
NT = NT_P + NT_S
TILES_PER_LAT = DEC_SEQ // TM
N_PAIRS = T * TOP_K
TE = 512
NB = N_PAIRS // TE + N_EXPERTS
P_ROWS = NB * TE
D_PACK = D_MODEL // 2
N_MOD_ROWS = 16
CONV_HALO = 16
VMEM_LIMIT = 48 * 1024 * 1024
SC_CORES = 2
SC_SUBCORES = 16
SC_WORKERS = SC_CORES * SC_SUBCORES
SC_CHUNK = 64


def _mod_row(i):
    return jnp.where(i < NT_P, 0, 1 + (i - NT_P) // TILES_PER_LAT)


def _silu(x):
    return x * jax.nn.sigmoid(x)


def _gelu(x):
    return x * (0.5 * (1.0 + jnp.tanh(0.7978845608028654 * (x + 0.044715 * (x * x * x)))))


def _rms(x, g):
    return x * lax.rsqrt(jnp.mean(x * x, axis=-1, keepdims=True) + EPS) * g


def _layernorm(x, g):
    xc = x - jnp.mean(x, axis=-1, keepdims=True)
    return xc * lax.rsqrt(jnp.mean(xc * xc, axis=-1, keepdims=True) + EPS) * g


def _modulate(x, g, shift, scale):
    return _rms(x, g) * (1.0 + scale) + shift


def _bdot(a, b):
    return jnp.dot(a, b, preferred_element_type=F32)


def _dot_nt(a, b, precision=None):
    return lax.dot_general(a, b, (((1,), (1,)), ((), ())), precision=precision,
                           preferred_element_type=F32)


def _dot_tn(a, b):
    return lax.dot_general(a, b, (((0,), (0,)), ((), ())), preferred_element_type=F32)


def _pack_pairs(x):
    m = x.shape[1] // 2
    lo = lax.bitcast_convert_type(x[:, :m].astype(BF16).astype(F32), jnp.uint32)
    hi = lax.bitcast_convert_type(x[:, m:].astype(BF16).astype(F32), jnp.uint32)
    return lax.bitcast_convert_type(hi | (lo >> 16), jnp.int32)


def _unpack_pairs(w):
    u = lax.bitcast_convert_type(w, jnp.uint32)
    lo = lax.bitcast_convert_type(u << 16, F32)
    hi = lax.bitcast_convert_type(u & jnp.uint32(0xFFFF0000), F32)
    return lo, hi


def _params(*sem):
    return pltpu.CompilerParams(dimension_semantics=sem, vmem_limit_bytes=VMEM_LIMIT)


def _const_spec(shape):
    nd = len(shape)
    return pl.BlockSpec(shape, lambda *_: (0,) * nd)


def _ada_body(c_ref, w_ref, b_ref, o_ref):
    s = _silu(c_ref[...])
    o_ref[...] = jnp.dot(s, w_ref[...], precision=HIGHEST, preferred_element_type=F32) + b_ref[...]


def _ada_tables(cvecs, w_ada, b_ada):
    out = pl.pallas_call(
        _ada_body,
        out_shape=jax.ShapeDtypeStruct((DEPTH, N_MOD_ROWS, 6 * D_MODEL), F32),
        grid=(DEPTH, 6),
        in_specs=[
            _const_spec((N_MOD_ROWS, D_MODEL)),
            pl.BlockSpec((None, D_MODEL, D_MODEL), lambda l, j: (l, 0, j)),
            pl.BlockSpec((None, 1, D_MODEL), lambda l, j: (l, 0, j)),
        ],
        out_specs=pl.BlockSpec((None, N_MOD_ROWS, D_MODEL), lambda l, j: (l, 0, j)),
        compiler_params=_params("parallel", "parallel"),
        name="ada_tables",
    )(cvecs, w_ada, b_ada.reshape(DEPTH, 1, 6 * D_MODEL))
    return out.reshape(DEPTH, N_MOD_ROWS, 6, D_MODEL)


def _mod_spec():
    return pl.BlockSpec((None, 6, D_MODEL), lambda i: (_mod_row(i), 0, 0))


def _tile_spec():
    return pl.BlockSpec((TM, D_MODEL), lambda i: (i, 0))


def _conv_segment(pad_ref, conv_ref, wdw_ref, pad_base, out_base, seg):
    rb = min(seg, 64)
    for cb in range(B_WIDTH // 128):
        cs = slice(cb * 128, (cb + 1) * 128)
        for r0 in range(0, seg, rb):
            acc = jnp.zeros((rb, 128), F32)
            for k in range(CONV_W):
                off = pad_base + CONV_HALO - CONV_PAD + k + r0
                acc = acc + wdw_ref[k:k + 1, cs] * pad_ref[off:off + rb, cs]
            conv_ref[out_base + r0:out_base + r0 + rb, cs] = acc


def _l0_body(x_ref, mod_ref, nmix_ref, win_ref, wsg_ref, bsg_ref, nsg_ref, wdw_ref, bdw_ref,
             ncv_ref, wout_ref, o_ref, cat_ref, pad_ref, conv_ref):
    i = pl.program_id(0)
    x = x_ref[...]
    h = _modulate(x, nmix_ref[...], mod_ref[0:1, :], mod_ref[1:2, :])
    z = _bdot(h.astype(BF16), win_ref[...])
    u = _gelu(z[:, :A_WIDTH])
    vb = _layernorm(_gelu(z[:, A_WIDTH:2 * A_WIDTH]), nsg_ref[...]).astype(BF16)
    for n in range(TM // CHUNK_A):
        rs = slice(n * CHUNK_A, (n + 1) * CHUNK_A)
        for g in range(A_GROUPS):
            cs = slice(g * A_GC, (g + 1) * A_GC)
            m = _bdot(wsg_ref[g], vb[rs, cs]) + bsg_ref[g]
            cat_ref[rs, cs] = (u[rs, cs] * m).astype(BF16)
    hb = z[:, 2 * A_WIDTH:2 * A_WIDTH + B_WIDTH] * jax.nn.sigmoid(z[:, 2 * A_WIDTH + B_WIDTH:])

    def conv_tile(seg):
        stride = seg + 2 * CONV_HALO
        halo = jnp.zeros((CONV_HALO, B_WIDTH), F32)
        for s in range(TM // seg):
            b = s * stride
            pad_ref[b:b + CONV_HALO, :] = halo
            pad_ref[b + CONV_HALO:b + CONV_HALO + seg, :] = hb[s * seg:(s + 1) * seg, :]
            pad_ref[b + CONV_HALO + seg:b + stride, :] = halo
        for s in range(TM // seg):
            _conv_segment(pad_ref, conv_ref, wdw_ref, s * stride, s * seg, seg)

    @pl.when(i < NT_P)
    def _():
        conv_tile(SEQ)

    @pl.when(i >= NT_P)
    def _():
        conv_tile(GRID_W)

    yb = _layernorm(conv_ref[...] + bdw_ref[...], ncv_ref[...])
    cat_ref[:, A_WIDTH:] = _silu(yb).astype(BF16)
    out = _bdot(cat_ref[...], wout_ref[...])
    o_ref[...] = x + mod_ref[2:3, :] * out


def _l0_mixer(x, mod, norm_mix, w_in, w_sg, b_sg, norm_sg, w_dw, b_dw, norm_cv, w_out):
    assert TM == SEQ and TM % GRID_W == 0 and TM % CHUNK_A == 0
    pad_rows = (TM // GRID_W) * (GRID_W + 2 * CONV_HALO)
    assert pad_rows >= SEQ + 2 * CONV_HALO
    return pl.pallas_call(
        _l0_body,
        out_shape=jax.ShapeDtypeStruct((T, D_MODEL), F32),
        grid=(NT,),
        in_specs=[
            _tile_spec(), _mod_spec(),
            _const_spec((1, D_MODEL)),
            _const_spec((D_MODEL, 2 * A_WIDTH + 2 * B_WIDTH)),
            _const_spec((A_GROUPS, CHUNK_A, CHUNK_A)),
            _const_spec((A_GROUPS, CHUNK_A, 1)),
            _const_spec((1, A_WIDTH)),
            _const_spec((CONV_W, B_WIDTH)),
            _const_spec((1, B_WIDTH)),
            _const_spec((1, B_WIDTH)),
            _const_spec((D_MODEL, D_MODEL)),
        ],
        out_specs=_tile_spec(),
        scratch_shapes=[
            pltpu.VMEM((TM, D_MODEL), BF16),
            pltpu.VMEM((pad_rows, B_WIDTH), F32),
            pltpu.VMEM((TM, B_WIDTH), F32),
        ],
        compiler_params=_params("parallel"),
        name="l0_mixer",
    )(x, mod, norm_mix.reshape(1, -1), w_in.astype(BF16), w_sg.astype(BF16),
      b_sg.reshape(A_GROUPS, CHUNK_A, 1), norm_sg.reshape(1, -1), w_dw, b_dw.reshape(1, -1),
      norm_cv.reshape(1, -1), w_out.astype(BF16))


def _route(scores, biased):
    n = scores.shape[-1]
    shp = (N_GROUPS, GROUP_SIZE, n)
    s3 = scores.reshape(shp)
    b3 = biased.reshape(shp)
    m_iota = lax.broadcasted_iota(jnp.int32, shp, 1).astype(F32)
    g_iota = lax.broadcasted_iota(jnp.int32, shp, 0).astype(F32)
    e_iota = g_iota * GROUP_SIZE + m_iota
    neg = -jnp.inf

    def amax1(v):
        return jnp.max(v, axis=1, keepdims=True)

    def amin1(v):
        return jnp.min(v, axis=1, keepdims=True)

    m1 = amax1(b3)
    i1 = amin1(jnp.where(b3 == m1, m_iota, float(GROUP_SIZE)))
    m2 = amax1(jnp.where(m_iota == i1, neg, b3))
    grp = m1 + m2
    gi1 = lax.broadcasted_iota(jnp.int32, grp.shape, 0).astype(F32)
    gmask = jnp.zeros(grp.shape, jnp.bool_)
    for _ in range(TOPK_GROUPS):
        gm = jnp.max(grp, axis=0, keepdims=True)
        gi = jnp.min(jnp.where(grp == gm, gi1, float(N_GROUPS)), axis=0, keepdims=True)
        hit = gi1 == gi
        gmask = jnp.logical_or(gmask, hit)
        grp = jnp.where(hit, neg, grp)
    cand = jnp.where(gmask, b3, neg)
    ids, vals, hits = [], [], []
    for _ in range(TOP_K):
        mx = jnp.max(amax1(cand), axis=0, keepdims=True)
        ei = jnp.min(amin1(jnp.where(cand == mx, e_iota, float(N_EXPERTS))), axis=0, keepdims=True)
        hit = e_iota == ei
        ids.append(ei.reshape(1, n))
        vals.append(_pick(hit, s3))
        hits.append(hit)
        cand = jnp.where(hit, neg, cand)
    return jnp.concatenate(ids, axis=0).astype(jnp.int32), jnp.concatenate(vals, axis=0), hits


def _pick(hit, v3):
    s = jnp.sum(jnp.sum(jnp.where(hit, v3, 0.0), axis=1, keepdims=True), axis=0, keepdims=True)
    return s.reshape(1, v3.shape[-1])


def _moe_pre_body(x_ref, mod_ref, nffn_ref, wrt_ref, br_ref, wsg_ref, wsu_ref, wsd_ref,
                  h_ref, eidx_ref, w8_ref, rank_ref, cnt_ref, acc_ref, run_ref):
    @pl.when(pl.program_id(0) == 0)
    def _():
        run_ref[...] = jnp.zeros(run_ref.shape, F32)

    x = x_ref[...]
    h = _modulate(x, nffn_ref[...], mod_ref[3:4, :], mod_ref[4:5, :])
    h_ref[...] = _pack_pairs(h)
    logits_t = _dot_nt(wrt_ref[...], h, precision=HIGHEST)
    scores = jax.nn.sigmoid(logits_t)
    eidx, sv, hits = _route(scores, scores + br_ref[...])
    eidx_ref[...] = eidx
    w8_ref[...] = sv / jnp.sum(sv, axis=0, keepdims=True) * ROUTED_SCALE
    sel3 = hits[0]
    for hit in hits[1:]:
        sel3 = jnp.logical_or(sel3, hit)
    sel = sel3.astype(F32).reshape(N_EXPERTS, TM)
    earlier = (lax.broadcasted_iota(jnp.int32, (TM, TM), 0)
               < lax.broadcasted_iota(jnp.int32, (TM, TM), 1)).astype(BF16)
    rank3 = (_bdot(sel.astype(BF16), earlier) + run_ref[...]).reshape(N_GROUPS, GROUP_SIZE, TM)
    rank_ref[...] = jnp.concatenate([_pick(hit, rank3) for hit in hits], axis=0).astype(jnp.int32)
    run_ref[...] = run_ref[...] + jnp.sum(sel, axis=1, keepdims=True)
    cnt_ref[...] = run_ref[...].astype(jnp.int32)
    hb = h.astype(BF16)
    sh = _bdot((_silu(_bdot(hb, wsg_ref[...])) * _bdot(hb, wsu_ref[...])).astype(BF16), wsd_ref[...])
    acc_ref[...] = x + mod_ref[5:6, :] * sh


def _moe_pre(x, mod, norm_ffn, w_router, b_router, w_sh_gate, w_sh_up, w_sh_down):
    return pl.pallas_call(
        _moe_pre_body,
        out_shape=(
            jax.ShapeDtypeStruct((T, D_PACK), jnp.int32),
            jax.ShapeDtypeStruct((TOP_K, T), jnp.int32),
            jax.ShapeDtypeStruct((TOP_K, T), F32),
            jax.ShapeDtypeStruct((TOP_K, T), jnp.int32),
            jax.ShapeDtypeStruct((N_EXPERTS, 1), jnp.int32),
            jax.ShapeDtypeStruct((T, D_MODEL), F32),
        ),
        grid=(NT,),
        in_specs=[
            _tile_spec(), _mod_spec(),
            _const_spec((1, D_MODEL)),
            _const_spec((N_EXPERTS, D_MODEL)),
            _const_spec((N_EXPERTS, 1)),
            _const_spec((D_MODEL, D_SHARED)),
            _const_spec((D_MODEL, D_SHARED)),
            _const_spec((D_SHARED, D_MODEL)),
        ],
        out_specs=(
            pl.BlockSpec((TM, D_PACK), lambda i: (i, 0)),
            pl.BlockSpec((TOP_K, TM), lambda i: (0, i)),
            pl.BlockSpec((TOP_K, TM), lambda i: (0, i)),
            pl.BlockSpec((TOP_K, TM), lambda i: (0, i)),
            _const_spec((N_EXPERTS, 1)),
            _tile_spec(),
        ),
        scratch_shapes=[pltpu.VMEM((N_EXPERTS, 1), F32)],
        compiler_params=_params("arbitrary"),
        name="moe_router_shared",
    )(x, mod, norm_ffn.reshape(1, -1), w_router.T, b_router.reshape(N_EXPERTS, 1),
      w_sh_gate.astype(BF16), w_sh_up.astype(BF16), w_sh_down.astype(BF16))


def _expert_body(be_ref, nv_ref, nu_ref, xs_ref, wg_ref, wu_ref, wd_ref, ys_ref, wgu_s, wd_s):
    j = pl.program_id(0)
    live_tile = j < nu_ref[0]

    @pl.when(jnp.logical_and(live_tile, jnp.logical_or(j == 0, be_ref[j] != be_ref[jnp.maximum(j - 1, 0)])))
    def _():
        wgu_s[:, :D_EXPERT] = wg_ref[...].astype(BF16)
        wgu_s[:, D_EXPERT:] = wu_ref[...].astype(BF16)
        wd_s[...] = wd_ref[...].astype(BF16)

    @pl.when(live_tile)
    def _():
        live = lax.broadcasted_iota(jnp.int32, (TE, 1), 0) < nv_ref[j]
        lo, hi = _unpack_pairs(jnp.where(live, xs_ref[...], 0))
        xb = jnp.concatenate([lo.astype(BF16), hi.astype(BF16)], axis=1)
        hgu = _bdot(xb, wgu_s[...])
        hh = _silu(hgu[:, :D_EXPERT]) * hgu[:, D_EXPERT:]
        ys_ref[...] = _pack_pairs(_bdot(hh.astype(BF16), wd_s[...]))


def _experts(block_expert, block_rows, n_used, xs, w_gate, w_up, w_down, layer):
    def row_map(j, be, nv, nu):
        return (jnp.minimum(j, nu[0] - 1), 0)

    def w_map(j, be, nv, nu):
        return (layer, be[jnp.minimum(j, nu[0] - 1)], 0, 0)

    return pl.pallas_call(
        _expert_body,
        out_shape=jax.ShapeDtypeStruct((P_ROWS, D_PACK), jnp.int32),
        grid_spec=pltpu.PrefetchScalarGridSpec(
            num_scalar_prefetch=3,
            grid=(NB,),
            in_specs=[
                pl.BlockSpec((TE, D_PACK), row_map),
                pl.BlockSpec((None, None, D_MODEL, D_EXPERT), w_map),
                pl.BlockSpec((None, None, D_MODEL, D_EXPERT), w_map),
                pl.BlockSpec((None, None, D_EXPERT, D_MODEL), w_map),
            ],
            out_specs=pl.BlockSpec((TE, D_PACK), row_map),
            scratch_shapes=[
                pltpu.VMEM((D_MODEL, 2 * D_EXPERT), BF16),
                pltpu.VMEM((D_EXPERT, D_MODEL), BF16),
            ],
        ),
        compiler_params=_params("arbitrary"),
        name="moe_experts",
    )(block_expert, block_rows, n_used, xs, w_gate, w_up, w_down)


def _positions_body(start_ref, eidx_ref, rank_ref, pos_ref):
    eidx = eidx_ref[...]
    base = jnp.zeros(eidx.shape, jnp.int32)
    for e in range(N_EXPERTS):
        base = jnp.where(eidx == e, start_ref[e], base)
    pos_ref[...] = base * TE + rank_ref[...]


def _positions(blk_start, eidx_t, rank_t):
    full = pl.BlockSpec((TOP_K, T), lambda i, s: (0, 0))
    return pl.pallas_call(
        _positions_body,
        out_shape=jax.ShapeDtypeStruct((TOP_K, T), jnp.int32),
        grid_spec=pltpu.PrefetchScalarGridSpec(
            num_scalar_prefetch=1, grid=(1,), in_specs=[full, full], out_specs=full),
        compiler_params=_params("arbitrary"),
        name="moe_positions",
    )(blk_start, eidx_t, rank_t)


def _sc_mesh():
    return plsc.VectorSubcoreMesh(core_axis_name="c", subcore_axis_name="s")


def _sc_worker():
    return lax.axis_index("s") * SC_CORES + lax.axis_index("c")


def _sc_scatter_rows(h, pos_rows):
    c = SC_CHUNK
    n_chunks = T // SC_WORKERS // c
    width = h.shape[1]

    @functools.partial(
        pl.kernel, mesh=_sc_mesh(),
        out_type=jax.ShapeDtypeStruct((P_ROWS, width), h.dtype),
        scratch_types=[pltpu.VMEM((n_chunks * TOP_K, c), jnp.int32), pltpu.VMEM((c, width), h.dtype),
                       pltpu.SemaphoreType.DMA],
        name="moe_dispatch_scatter",
    )
    def scatter(h_hbm, pos_hbm, xs_hbm, idx_v, rows_v, sem):
        first = _sc_worker() * n_chunks
        pltpu.sync_copy(pos_hbm.at[pl.ds(first * TOP_K, n_chunks * TOP_K)], idx_v)

        @pl.loop(0, n_chunks)
        def _(i):
            pltpu.sync_copy(h_hbm.at[pl.ds((first + i) * c, c)], rows_v)
            copies = [pltpu.async_copy(rows_v, xs_hbm.at[idx_v.at[i * TOP_K + k]], sem)
                      for k in range(TOP_K)]
            for cp in copies:
                cp.wait()

    return scatter(h, pos_rows)


def _sc_gather_rows(table, idx):
    c = SC_CHUNK
    n = idx.shape[0]
    per_worker = n // SC_WORKERS
    n_chunks = per_worker // c

    @functools.partial(
        pl.kernel, mesh=_sc_mesh(),
        out_type=jax.ShapeDtypeStruct((n, table.shape[1]), table.dtype),
        scratch_types=[pltpu.VMEM((per_worker,), jnp.int32), pltpu.VMEM((c, table.shape[1]), table.dtype)],
        name="moe_combine_gather",
    )
    def gather(table_hbm, idx_hbm, out_hbm, idx_v, rows_v):
        base = _sc_worker() * per_worker
        pltpu.sync_copy(idx_hbm.at[pl.ds(base, per_worker)], idx_v)

        @pl.loop(0, n_chunks)
        def _(i):
            pltpu.sync_copy(table_hbm.at[idx_v.at[pl.ds(i * c, c)]], rows_v)
            pltpu.sync_copy(rows_v, out_hbm.at[pl.ds(base + i * c, c)])

    return gather(table, idx)


def _moe_routed(h, eidx_t, rank_t, counts, w_gate, w_up, w_down, layer):
    counts = counts.reshape(1, N_EXPERTS)
    nblk = (counts + TE - 1) // TE
    blk_end = jnp.cumsum(nblk, axis=1)
    blk_start = blk_end - nblk
    blocks = jnp.arange(NB, dtype=jnp.int32).reshape(NB, 1)
    block_expert = jnp.minimum(jnp.sum(blocks >= blk_end, axis=1, keepdims=True), N_EXPERTS - 1)
    mine = block_expert == jnp.arange(N_EXPERTS, dtype=jnp.int32).reshape(1, N_EXPERTS)
    cnt_b = jnp.sum(jnp.where(mine, counts, 0), axis=1, keepdims=True)
    start_b = jnp.sum(jnp.where(mine, blk_start, 0), axis=1, keepdims=True)
    block_rows = jnp.clip(cnt_b - (blocks - start_b) * TE, 0, TE)
    n_used = blk_end[0, -1].reshape(1).astype(jnp.int32)
    pos = _positions(blk_start.reshape(N_EXPERTS).astype(jnp.int32), eidx_t, rank_t)
    pos_rows = pos.reshape(TOP_K, T // SC_CHUNK, SC_CHUNK).transpose(1, 0, 2).reshape(-1, SC_CHUNK)
    xs = _sc_scatter_rows(h, pos_rows)
    ys = _experts(block_expert.reshape(NB).astype(jnp.int32), block_rows.reshape(NB).astype(jnp.int32),
                  n_used, xs, w_gate, w_up, w_down, layer)
    return _sc_gather_rows(ys, pos.reshape(-1)).reshape(TOP_K, T, D_PACK)


N_SLABS = (3 * C_FDIM + 2 * D_MODEL) // 128


def _combine(yg_ref, w8_ref):
    w8 = w8_ref[...]
    lo, hi = _unpack_pairs(yg_ref[0])
    r_lo, r_hi = w8[:, 0:1] * lo, w8[:, 0:1] * hi
    for k in range(1, TOP_K):
        lo, hi = _unpack_pairs(yg_ref[k])
        r_lo, r_hi = r_lo + w8[:, k:k + 1] * lo, r_hi + w8[:, k:k + 1] * hi
    return jnp.concatenate([r_lo, r_hi], axis=1)


def _hgrn_in_body(acc_ref, yg_ref, w8_ref, mod0_ref, mod1_ref, nmix_ref, win_ref, x_ref, z_ref):
    x = acc_ref[...] + mod0_ref[5:6, :] * _combine(yg_ref, w8_ref)
    x_ref[...] = x
    hb = _modulate(x, nmix_ref[...], mod1_ref[0:1, :], mod1_ref[1:2, :]).astype(BF16)
    for s in range(N_SLABS // C_HEADS):
        zz = _bdot(hb, win_ref[:, s * D_MODEL:(s + 1) * D_MODEL])
        for hh in range(C_HEADS):
            z_ref[s * C_HEADS + hh] = zz[:, hh * 128:(hh + 1) * 128]


def _hgrn_in(acc, yg, w8, mod0, mod1, norm_mix, w_in):
    return pl.pallas_call(
        _hgrn_in_body,
        out_shape=(
            jax.ShapeDtypeStruct((T, D_MODEL), F32),
            jax.ShapeDtypeStruct((N_SLABS, T, 128), F32),
        ),
        grid=(NT,),
        in_specs=[
            _tile_spec(),
            pl.BlockSpec((TOP_K, TM, D_PACK), lambda i: (0, i, 0)),
            pl.BlockSpec((TM, TOP_K), lambda i: (i, 0)),
            _mod_spec(), _mod_spec(),
            _const_spec((1, D_MODEL)),
            _const_spec((D_MODEL, 3 * C_FDIM + 2 * D_MODEL)),
        ],
        out_specs=(
            _tile_spec(),
            pl.BlockSpec((N_SLABS, TM, 128), lambda i: (0, i, 0)),
        ),
        compiler_params=_params("parallel"),
        name="hgrn_in_proj",
    )(acc, yg, w8, mod0, mod1, norm_mix.reshape(1, -1), w_in.astype(BF16))


def _gla_body(q_ref, f_ref, v_ref, lb_ref, s0_ref, o_ref, ns_ref, st_ref, *, rev):
    j = pl.program_id(0)
    ti = NT - 1 - j if rev else j
    is_ctx = ti < NT_P
    first_lat = (ti - NT_P) % TILES_PER_LAT == (TILES_PER_LAT - 1 if rev else 0)

    @pl.when(is_ctx)
    def _():
        st_ref[...] = jnp.zeros(st_ref.shape, F32)

    @pl.when(jnp.logical_and(jnp.logical_not(is_ctx), first_lat))
    def _():
        st_ref[...] = s0_ref[...]

    row = lax.broadcasted_iota(jnp.int32, (TM, TM), 0)
    col = lax.broadcasted_iota(jnp.int32, (TM, TM), 1)
    same_chunk = (row // SCAN_CHUNK) == (col // SCAN_CHUNK)
    seen = jnp.logical_and(same_chunk, (col >= row) if rev else (col <= row))
    cum_w = seen.astype(BF16)
    mid = SCAN_CHUNK // 2 if rev else SCAN_CHUNK // 2 - 1
    last = 0 if rev else SCAN_CHUNK - 1
    n_chunks = TM // SCAN_CHUNK
    order = range(n_chunks - 1, -1, -1) if rev else range(n_chunks)
    group = 4

    def chunk_rows(b, off):
        return jnp.concatenate(
            [jnp.broadcast_to(b[c * SCAN_CHUNK + off:c * SCAN_CHUNK + off + 1, :], (SCAN_CHUNK, b.shape[1]))
             for c in range(n_chunks)], axis=0)

    def head_group(gi, carry):
        heads = [gi * group + u for u in range(group)]
        qs, kk, vv, bcum = [], [], [], []
        for hd in heads:
            lb = lb_ref[hd]
            qs.append(_silu(q_ref[hd]) * (C_DK ** -0.5))
            fg = lb + (1.0 - lb) * jax.nn.sigmoid(f_ref[hd])
            kk.append(1.0 - fg)
            vv.append(v_ref[hd].astype(BF16))
            g = jnp.log(fg)
            g_hi = g.astype(BF16)
            r1 = g - g_hi.astype(F32)
            g_mid = r1.astype(BF16)
            g_lo = (r1 - g_mid.astype(F32)).astype(BF16)
            bcum.append(_bdot(cum_w, g_hi) + _bdot(cum_w, g_mid) + _bdot(cum_w, g_lo))
        o_intra, q_dec, kv, decay = [], [], [], []
        for u in range(group):
            b_mid = chunk_rows(bcum[u], mid)
            b_last = chunk_rows(bcum[u], last)
            qe = (qs[u] * jnp.exp(bcum[u] - b_mid)).astype(BF16)
            ke = (kk[u] * jnp.exp(b_mid - bcum[u])).astype(BF16)
            att = jnp.where(seen, _dot_nt(qe, ke), 0.0)
            o_intra.append(_bdot(att.astype(BF16), vv[u]))
            q_dec.append((qs[u] * jnp.exp(bcum[u])).astype(BF16))
            k_dec = (kk[u] * jnp.exp(b_last - bcum[u])).astype(BF16)
            kv.append([_dot_tn(vv[u][c * SCAN_CHUNK:(c + 1) * SCAN_CHUNK], k_dec[c * SCAN_CHUNK:(c + 1) * SCAN_CHUNK])
                       for c in range(n_chunks)])
            decay.append([jnp.exp(bcum[u][c * SCAN_CHUNK + last:c * SCAN_CHUNK + last + 1, :])
                          for c in range(n_chunks)])
        st = [st_ref[hd] for hd in heads]
        for c in order:
            sl = slice(c * SCAN_CHUNK, (c + 1) * SCAN_CHUNK)
            for u, hd in enumerate(heads):
                o_ref[hd, pl.ds(c * SCAN_CHUNK, SCAN_CHUNK), :] = (
                    o_intra[u][sl] + _dot_nt(q_dec[u][sl], st[u].astype(BF16)))
                st[u] = decay[u][c] * st[u] + kv[u][c]
        for u, hd in enumerate(heads):
            st_ref[hd] = st[u]
        return carry

    lax.fori_loop(0, C_HEADS // group, head_group, 0)

    @pl.when(is_ctx)
    def _():
        ns_ref[...] = st_ref[...]


def _gla(z3, lb_dir, s0t_dir, *, rev):
    def ti_of(j):
        return NT - 1 - j if rev else j

    f_slab = 2 if rev else 1

    def lat_map(j):
        return (jnp.clip((ti_of(j) - NT_P) // TILES_PER_LAT, 0, DEC_BATCH - 1), 0, 0, 0)

    return pl.pallas_call(
        functools.partial(_gla_body, rev=rev),
        out_shape=(
            jax.ShapeDtypeStruct((C_HEADS, T, C_DV), F32),
            jax.ShapeDtypeStruct((BATCH, C_HEADS, C_DV, C_DK), F32),
        ),
        grid=(NT,),
        in_specs=[
            pl.BlockSpec((C_HEADS, TM, 128), lambda j: (0, ti_of(j), 0)),
            pl.BlockSpec((C_HEADS, TM, 128), lambda j: (f_slab, ti_of(j), 0)),
            pl.BlockSpec((C_HEADS, TM, 128), lambda j: (3, ti_of(j), 0)),
            _const_spec((C_HEADS, 1, C_DK)),
            pl.BlockSpec((None, C_HEADS, C_DV, C_DK), lat_map),
        ],
        out_specs=(
            pl.BlockSpec((C_HEADS, TM, C_DV), lambda j: (0, ti_of(j), 0)),
            pl.BlockSpec((None, C_HEADS, C_DV, C_DK),
                         lambda j: (jnp.minimum(ti_of(j), NT_P - 1), 0, 0, 0)),
        ),
        scratch_shapes=[pltpu.VMEM((C_HEADS, C_DV, C_DK), F32)],
        compiler_params=_params("arbitrary"),
        name="gla_bwd" if rev else "gla_fwd",
    )(z3, z3, z3, lb_dir, s0t_dir)


def _hgrn_out_body(ofw_ref, obw_ref, gate_ref, x_ref, mod_ref, no_ref, wout_ref, o_ref, cat_ref):
    for hd in range(C_HEADS):
        o = ofw_ref[hd] + obw_ref[hd]
        cat_ref[:, hd * C_DV:(hd + 1) * C_DV] = (_rms(o, no_ref[...]) * _silu(gate_ref[hd])).astype(BF16)
    o_ref[...] = x_ref[...] + mod_ref[2:3, :] * _bdot(cat_ref[...], wout_ref[...])


def _hgrn_out(o_fw, o_bw, z3, x, mod, norm_o, w_out):
    head_spec = pl.BlockSpec((C_HEADS, TM, C_DV), lambda i: (0, i, 0))
    return pl.pallas_call(
        _hgrn_out_body,
        out_shape=jax.ShapeDtypeStruct((T, D_MODEL), F32),
        grid=(NT,),
        in_specs=[
            head_spec, head_spec,
            pl.BlockSpec((C_HEADS, TM, 128), lambda i: (4, i, 0)),
            _tile_spec(), _mod_spec(),
            _const_spec((1, C_DV)),
            _const_spec((D_MODEL, D_MODEL)),
        ],
        out_specs=_tile_spec(),
        scratch_shapes=[pltpu.VMEM((TM, D_MODEL), BF16)],
        compiler_params=_params("parallel"),
        name="hgrn_out_proj",
    )(o_fw, o_bw, z3, x, mod, norm_o.reshape(1, -1), w_out.astype(BF16))


def _final_body(acc_ref, yg_ref, w8_ref, mod_ref, nf_ref, o_ref):
    x = acc_ref[...] + mod_ref[5:6, :] * _combine(yg_ref, w8_ref)
    o_ref[...] = _rms(x, nf_ref[...])


def _final(acc, yg, w8, mod, norm_final, tile0, n_tiles):
    in_tile = pl.BlockSpec((TM, D_MODEL), lambda i: (i + tile0, 0))
    return pl.pallas_call(
        _final_body,
        out_shape=jax.ShapeDtypeStruct((n_tiles * TM, D_MODEL), F32),
        grid=(n_tiles,),
        in_specs=[
            in_tile,
            pl.BlockSpec((TOP_K, TM, D_PACK), lambda i: (0, i + tile0, 0)),
            pl.BlockSpec((TM, TOP_K), lambda i: (i + tile0, 0)),
            pl.BlockSpec((None, 6, D_MODEL), lambda i: (_mod_row(i + tile0), 0, 0)),
            _const_spec((1, D_MODEL)),
        ],
        out_specs=pl.BlockSpec((TM, D_MODEL), lambda i: (i, 0)),
        compiler_params=_params("parallel"),
        name="final_norm",
    )(acc, yg, w8, mod, norm_final.reshape(1, -1))


def kernel(x_prompt, x_sample, state_hgrn, c, c_ctx, w_ada, b_ada, norm_mix, norm_ffn, w_out, w_in_ab, w_sg, b_sg, norm_sg, w_dw, b_dw, norm_cv, w_in_hgrn, lb_raw, norm_o, w_router, b_router, w_gate, w_up, w_down, w_sh_gate, w_sh_up, w_sh_down, norm_final):
    x = jnp.concatenate([x_prompt.reshape(T_P, D_MODEL), x_sample.reshape(T_S, D_MODEL)], axis=0)
    cvecs = jnp.concatenate(
        [c_ctx.reshape(1, D_MODEL), c, jnp.zeros((N_MOD_ROWS - 1 - DEC_BATCH, D_MODEL), F32)], axis=0)
    mods = _ada_tables(cvecs, w_ada, b_ada)
    lb_sm = jax.nn.softmax(lb_raw.astype(F32), axis=0)
    lb1 = (jnp.cumsum(lb_sm, axis=0) - lb_sm[0])[1].reshape(2, C_HEADS, 1, C_DK)

    def moe(l, xin):
        h, eidx_t, w8_t, rank_t, counts, acc = _moe_pre(
            xin, mods[l], norm_ffn[l], w_router[l], b_router[l], w_sh_gate[l], w_sh_up[l], w_sh_down[l])
        return acc, _moe_routed(h, eidx_t, rank_t, counts, w_gate, w_up, w_down, l), w8_t.T

    x = _l0_mixer(x, mods[0], norm_mix[0], w_in_ab[0], w_sg[0], b_sg[0], norm_sg[0], w_dw[0],
                  b_dw[0], norm_cv[0], w_out[0])
    acc, yg, w8 = moe(0, x)
    x, z3 = _hgrn_in(acc, yg, w8, mods[0], mods[1], norm_mix[1], w_in_hgrn[0])
    s0t = jnp.swapaxes(state_hgrn[:, 0].astype(F32), -1, -2)
    o_fw, ns_fw = _gla(z3, lb1[0], s0t[:, 0], rev=False)
    o_bw, ns_bw = _gla(z3, lb1[1], s0t[:, 1], rev=True)
    x = _hgrn_out(o_fw, o_bw, z3, x, mods[1], norm_o[0], w_out[1])
    acc, yg, w8 = moe(1, x)
    y_p = _final(acc, yg, w8, mods[1], norm_final, 0, NT_P).reshape(BATCH, SEQ, D_MODEL)
    y_s = _final(acc, yg, w8, mods[1], norm_final, NT_P, NT_S).reshape(DEC_BATCH, DEC_SEQ, D_MODEL)
    new_state = jnp.swapaxes(jnp.stack([ns_fw, ns_bw], axis=1), -1, -2)[:, None]
    return (y_p, y_s, new_state)
```

```python
import functools

import jax
import jax.numpy as jnp
from jax import lax
from jax.experimental import pallas as pl
from jax.experimental.pallas import tpu as pltpu
from jax.experimental.pallas import tpu_sc as plsc

F32 = jnp.float32
BF16 = jnp.bfloat16
HIGHEST = lax.Precision.HIGHEST

D_MODEL = 1024
BATCH = 32
SEQ = 256
DEPTH = 2
DEC_BATCH = 8
DEC_SEQ = 2048
GRID_W = 64
A_WIDTH = D_MODEL // 2
A_GROUPS = 4
A_GC = A_WIDTH // A_GROUPS
CHUNK_A = 128
B_WIDTH = D_MODEL - A_WIDTH
CONV_W = 31
CONV_PAD = CONV_W // 2
C_HEADS = 8
C_DK = 128
C_DV = D_MODEL // C_HEADS
C_FDIM = C_HEADS * C_DK
SCAN_CHUNK = 64
N_EXPERTS = 64
TOP_K = 8
N_GROUPS = 8
GROUP_SIZE = N_EXPERTS // N_GROUPS
TOPK_GROUPS = 4
D_EXPERT = 256
D_SHARED = 256
ROUTED_SCALE = 2.5
EPS = 1e-6

TM = 256
T_P = BATCH * SEQ
T_S = DEC_BATCH * DEC_SEQ
T = T_P + T_S
NT_P = T_P // TM
NT_S = T_S // TM
NT = NT_P + NT_S
TILES_PER_LAT = DEC_SEQ // TM
N_PAIRS = T * TOP_K
TE = 512
NB = N_PAIRS // TE + N_EXPERTS
P_ROWS = NB * TE
D_PACK = D_MODEL // 2
N_MOD_ROWS = 16
CONV_HALO = 16
VMEM_LIMIT = 48 * 1024 * 1024
SC_CORES = 2
SC_SUBCORES = 16
SC_WORKERS = SC_CORES * SC_SUBCORES
SC_CHUNK = 64


def _mod_row(i):
    return jnp.where(i < NT_P, 0, 1 + (i - NT_P) // TILES_PER_LAT)


def _silu(x):
    return x * jax.nn.sigmoid(x)


def _gelu(x):
    return x * (0.5 * (1.0 + jnp.tanh(0.7978845608028654 * (x + 0.044715 * (x * x * x)))))


def _rms(x, g):
    return x * lax.rsqrt(jnp.mean(x * x, axis=-1, keepdims=True) + EPS) * g


def _layernorm(x, g):
    xc = x - jnp.mean(x, axis=-1, keepdims=True)
    return xc * lax.rsqrt(jnp.mean(xc * xc, axis=-1, keepdims=True) + EPS) * g


def _modulate(x, g, shift, scale):
    return _rms(x, g) * (1.0 + scale) + shift


def _bdot(a, b):
    return jnp.dot(a, b, preferred_element_type=F32)


def _dot_nt(a, b, precision=None):
    return lax.dot_general(a, b, (((1,), (1,)), ((), ())), precision=precision,
                           preferred_element_type=F32)


def _dot_tn(a, b):
    return lax.dot_general(a, b, (((0,), (0,)), ((), ())), preferred_element_type=F32)


def _pack_pairs(x):
    m = x.shape[1] // 2
    lo = lax.bitcast_convert_type(x[:, :m].astype(BF16).astype(F32), jnp.uint32)
    hi = lax.bitcast_convert_type(x[:, m:].astype(BF16).astype(F32), jnp.uint32)
    return lax.bitcast_convert_type(hi | (lo >> 16), jnp.int32)


def _unpack_pairs(w):
    u = lax.bitcast_convert_type(w, jnp.uint32)
    lo = lax.bitcast_convert_type(u << 16, F32)
    hi = lax.bitcast_convert_type(u & jnp.uint32(0xFFFF0000), F32)
    return lo, hi


def _params(*sem):
    return pltpu.CompilerParams(dimension_semantics=sem, vmem_limit_bytes=VMEM_LIMIT)


def _const_spec(shape):
    nd = len(shape)
    return pl.BlockSpec(shape, lambda *_: (0,) * nd)


def _ada_body(c_ref, w_ref, b_ref, o_ref):
    s = _silu(c_ref[...])
    o_ref[...] = jnp.dot(s, w_ref[...], precision=HIGHEST, preferred_element_type=F32) + b_ref[...]


def _ada_tables(cvecs, w_ada, b_ada):
    out = pl.pallas_call(
        _ada_body,
        out_shape=jax.ShapeDtypeStruct((DEPTH, N_MOD_ROWS, 6 * D_MODEL), F32),
        grid=(DEPTH, 6),
        in_specs=[
            _const_spec((N_MOD_ROWS, D_MODEL)),
            pl.BlockSpec((None, D_MODEL, D_MODEL), lambda l, j: (l, 0, j)),
            pl.BlockSpec((None, 1, D_MODEL), lambda l, j: (l, 0, j)),
        ],
        out_specs=pl.BlockSpec((None, N_MOD_ROWS, D_MODEL), lambda l, j: (l, 0, j)),
        compiler_params=_params("parallel", "parallel"),
        name="ada_tables",
    )(cvecs, w_ada, b_ada.reshape(DEPTH, 1, 6 * D_MODEL))
    return out.reshape(DEPTH, N_MOD_ROWS, 6, D_MODEL)


def _mod_spec():
    return pl.BlockSpec((None, 6, D_MODEL), lambda i: (_mod_row(i), 0, 0))


def _tile_spec():
    return pl.BlockSpec((TM, D_MODEL), lambda i: (i, 0))


SUBLANES = 8


def _conv_segment(pad_ref, shift_ref, conv_ref, wdw_ref, pad_base, out_base, seg):
    rb = min(seg, 64)
    for cb in range(B_WIDTH // 128):
        cs = slice(cb * 128, (cb + 1) * 128)
        for r0 in range(0, seg, rb):
            acc = jnp.zeros((rb, 128), F32)
            for k in range(CONV_W):
                b = (CONV_HALO - CONV_PAD + k) % SUBLANES
                off = pad_base + r0 + CONV_HALO - CONV_PAD + k - b
                src = pad_ref if b == 0 else shift_ref.at[b - 1]
                acc = acc + wdw_ref[k:k + 1, cs] * src[off:off + rb, cs]
            conv_ref[out_base + r0:out_base + r0 + rb, cs] = acc


def _l0_body(x_ref, mod_ref, nmix_ref, win_ref, wsg_ref, bsg_ref, nsg_ref, wdw_ref, bdw_ref,
             ncv_ref, wout_ref, o_ref, cat_ref, pad_ref, shift_ref, conv_ref):
    i = pl.program_id(0)
    x = x_ref[...]
    h = _modulate(x, nmix_ref[...], mod_ref[0:1, :], mod_ref[1:2, :])
    z = _bdot(h.astype(BF16), win_ref[...])
    u = _gelu(z[:, :A_WIDTH])
    vb = _layernorm(_gelu(z[:, A_WIDTH:2 * A_WIDTH]), nsg_ref[...]).astype(BF16)
    for n in range(TM // CHUNK_A):
        rs = slice(n * CHUNK_A, (n + 1) * CHUNK_A)
        for g in range(A_GROUPS):
            cs = slice(g * A_GC, (g + 1) * A_GC)
            m = _bdot(wsg_ref[g], vb[rs, cs]) + bsg_ref[g]
            cat_ref[rs, cs] = (u[rs, cs] * m).astype(BF16)
    hb = z[:, 2 * A_WIDTH:2 * A_WIDTH + B_WIDTH] * jax.nn.sigmoid(z[:, 2 * A_WIDTH + B_WIDTH:])

    def conv_tile(seg):
        stride = seg + 2 * CONV_HALO
        halo = jnp.zeros((CONV_HALO, B_WIDTH), F32)
        for s in range(TM // seg):
            b = s * stride
            pad_ref[b:b + CONV_HALO, :] = halo
            pad_ref[b + CONV_HALO:b + CONV_HALO + seg, :] = hb[s * seg:(s + 1) * seg, :]
            pad_ref[b + CONV_HALO + seg:b + stride, :] = halo
        rows = (TM // seg) * stride - SUBLANES
        for b in range(1, SUBLANES):
            shift_ref[b - 1, 0:rows, :] = pad_ref[b:b + rows, :]
        for s in range(TM // seg):
            _conv_segment(pad_ref, shift_ref, conv_ref, wdw_ref, s * stride, s * seg, seg)

    @pl.when(i < NT_P)
    def _():
        conv_tile(SEQ)

    @pl.when(i >= NT_P)
    def _():
        conv_tile(GRID_W)

    yb = _layernorm(conv_ref[...] + bdw_ref[...], ncv_ref[...])
    cat_ref[:, A_WIDTH:] = _silu(yb).astype(BF16)
    out = _bdot(cat_ref[...], wout_ref[...])
    o_ref[...] = x + mod_ref[2:3, :] * out


def _l0_mixer(x, mod, norm_mix, w_in, w_sg, b_sg, norm_sg, w_dw, b_dw, norm_cv, w_out):
    assert TM == SEQ and TM % GRID_W == 0 and TM % CHUNK_A == 0
    pad_rows = (TM // GRID_W) * (GRID_W + 2 * CONV_HALO)
    assert pad_rows >= SEQ + 2 * CONV_HALO
    return pl.pallas_call(
        _l0_body,
        out_shape=jax.ShapeDtypeStruct((T, D_MODEL), F32),
        grid=(NT,),
        in_specs=[
            _tile_spec(), _mod_spec(),
            _const_spec((1, D_MODEL)),
            _const_spec((D_MODEL, 2 * A_WIDTH + 2 * B_WIDTH)),
            _const_spec((A_GROUPS, CHUNK_A, CHUNK_A)),
            _const_spec((A_GROUPS, CHUNK_A, 1)),
            _const_spec((1, A_WIDTH)),
            _const_spec((CONV_W, B_WIDTH)),
            _const_spec((1, B_WIDTH)),
            _const_spec((1, B_WIDTH)),
            _const_spec((D_MODEL, D_MODEL)),
        ],
        out_specs=_tile_spec(),
        scratch_shapes=[
            pltpu.VMEM((TM, D_MODEL), BF16),
            pltpu.VMEM((pad_rows, B_WIDTH), F32),
            pltpu.VMEM((SUBLANES - 1, pad_rows, B_WIDTH), F32),
            pltpu.VMEM((TM, B_WIDTH), F32),
        ],
        compiler_params=_params("parallel"),
        name="l0_mixer",
    )(x, mod, norm_mix.reshape(1, -1), w_in.astype(BF16), w_sg.astype(BF16),
      b_sg.reshape(A_GROUPS, CHUNK_A, 1), norm_sg.reshape(1, -1), w_dw, b_dw.reshape(1, -1),
      norm_cv.reshape(1, -1), w_out.astype(BF16))


def _route(scores, biased):
    n = scores.shape[-1]
    shp = (N_GROUPS, GROUP_SIZE, n)
    s3 = scores.reshape(shp)
    b3 = biased.reshape(shp)
    m_iota = lax.broadcasted_iota(jnp.int32, shp, 1).astype(F32)
    g_iota = lax.broadcasted_iota(jnp.int32, shp, 0).astype(F32)
    e_iota = g_iota * GROUP_SIZE + m_iota
    neg = -jnp.inf

    def amax1(v):
        return jnp.max(v, axis=1, keepdims=True)

    def amin1(v):
        return jnp.min(v, axis=1, keepdims=True)

    m1 = amax1(b3)
    i1 = amin1(jnp.where(b3 == m1, m_iota, float(GROUP_SIZE)))
    m2 = amax1(jnp.where(m_iota == i1, neg, b3))
    grp = m1 + m2
    gi1 = lax.broadcasted_iota(jnp.int32, grp.shape, 0).astype(F32)
    gmask = jnp.zeros(grp.shape, jnp.bool_)
    for _ in range(TOPK_GROUPS):
        gm = jnp.max(grp, axis=0, keepdims=True)
        gi = jnp.min(jnp.where(grp == gm, gi1, float(N_GROUPS)), axis=0, keepdims=True)
        hit = gi1 == gi
        gmask = jnp.logical_or(gmask, hit)
        grp = jnp.where(hit, neg, grp)
    cand = jnp.where(gmask, b3, neg)
    ids, vals, hits = [], [], []
    for _ in range(TOP_K):
        mx = jnp.max(amax1(cand), axis=0, keepdims=True)
        ei = jnp.min(amin1(jnp.where(cand == mx, e_iota, float(N_EXPERTS))), axis=0, keepdims=True)
        hit = e_iota == ei
        ids.append(ei.reshape(1, n))
        vals.append(_pick(hit, s3))
        hits.append(hit)
        cand = jnp.where(hit, neg, cand)
    return jnp.concatenate(ids, axis=0).astype(jnp.int32), jnp.concatenate(vals, axis=0), hits


def _pick(hit, v3):
    s = jnp.sum(jnp.sum(jnp.where(hit, v3, 0.0), axis=1, keepdims=True), axis=0, keepdims=True)
    return s.reshape(1, v3.shape[-1])


def _moe_pre_body(x_ref, mod_ref, nffn_ref, wrt_ref, br_ref, wsg_ref, wsu_ref, wsd_ref,
                  h_ref, eidx_ref, w8_ref, rank_ref, cnt_ref, acc_ref, run_ref):
    @pl.when(pl.program_id(0) == 0)
    def _():
        run_ref[...] = jnp.zeros(run_ref.shape, F32)

    x = x_ref[...]
    h = _modulate(x, nffn_ref[...], mod_ref[3:4, :], mod_ref[4:5, :])
    h_ref[...] = _pack_pairs(h)
    logits_t = _dot_nt(wrt_ref[...], h, precision=HIGHEST)
    scores = jax.nn.sigmoid(logits_t)
    eidx, sv, hits = _route(scores, scores + br_ref[...])
    eidx_ref[...] = eidx
    w8_ref[...] = sv / jnp.sum(sv, axis=0, keepdims=True) * ROUTED_SCALE
    sel3 = hits[0]
    for hit in hits[1:]:
        sel3 = jnp.logical_or(sel3, hit)
    sel = sel3.astype(F32).reshape(N_EXPERTS, TM)
    earlier = (lax.broadcasted_iota(jnp.int32, (TM, TM), 0)
               < lax.broadcasted_iota(jnp.int32, (TM, TM), 1)).astype(BF16)
    rank3 = (_bdot(sel.astype(BF16), earlier) + run_ref[...]).reshape(N_GROUPS, GROUP_SIZE, TM)
    rank_ref[...] = jnp.concatenate([_pick(hit, rank3) for hit in hits], axis=0).astype(jnp.int32)
    run_ref[...] = run_ref[...] + jnp.sum(sel, axis=1, keepdims=True)
    cnt_ref[...] = run_ref[...].astype(jnp.int32)
    hb = h.astype(BF16)
    sh = _bdot((_silu(_bdot(hb, wsg_ref[...])) * _bdot(hb, wsu_ref[...])).astype(BF16), wsd_ref[...])
    acc_ref[...] = x + mod_ref[5:6, :] * sh


def _moe_pre(x, mod, norm_ffn, w_router, b_router, w_sh_gate, w_sh_up, w_sh_down):
    return pl.pallas_call(
        _moe_pre_body,
        out_shape=(
            jax.ShapeDtypeStruct((T, D_PACK), jnp.int32),
            jax.ShapeDtypeStruct((TOP_K, T), jnp.int32),
            jax.ShapeDtypeStruct((TOP_K, T), F32),
            jax.ShapeDtypeStruct((TOP_K, T), jnp.int32),
            jax.ShapeDtypeStruct((N_EXPERTS, 1), jnp.int32),
            jax.ShapeDtypeStruct((T, D_MODEL), F32),
        ),
        grid=(NT,),
        in_specs=[
            _tile_spec(), _mod_spec(),
            _const_spec((1, D_MODEL)),
            _const_spec((N_EXPERTS, D_MODEL)),
            _const_spec((N_EXPERTS, 1)),
            _const_spec((D_MODEL, D_SHARED)),
            _const_spec((D_MODEL, D_SHARED)),
            _const_spec((D_SHARED, D_MODEL)),
        ],
        out_specs=(
            pl.BlockSpec((TM, D_PACK), lambda i: (i, 0)),
            pl.BlockSpec((TOP_K, TM), lambda i: (0, i)),
            pl.BlockSpec((TOP_K, TM), lambda i: (0, i)),
            pl.BlockSpec((TOP_K, TM), lambda i: (0, i)),
            _const_spec((N_EXPERTS, 1)),
            _tile_spec(),
        ),
        scratch_shapes=[pltpu.VMEM((N_EXPERTS, 1), F32)],
        compiler_params=_params("arbitrary"),
        name="moe_router_shared",
    )(x, mod, norm_ffn.reshape(1, -1), w_router.T, b_router.reshape(N_EXPERTS, 1),
      w_sh_gate.astype(BF16), w_sh_up.astype(BF16), w_sh_down.astype(BF16))


def _expert_body(be_ref, nv_ref, nu_ref, xs_ref, wg_ref, wu_ref, wd_ref, ys_ref, wgu_s, wd_s):
    j = pl.program_id(0)
    live_tile = j < nu_ref[0]

    @pl.when(jnp.logical_and(live_tile, jnp.logical_or(j == 0, be_ref[j] != be_ref[jnp.maximum(j - 1, 0)])))
    def _():
        wgu_s[:, :D_EXPERT] = wg_ref[...].astype(BF16)
        wgu_s[:, D_EXPERT:] = wu_ref[...].astype(BF16)
        wd_s[...] = wd_ref[...].astype(BF16)

    @pl.when(live_tile)
    def _():
        live = lax.broadcasted_iota(jnp.int32, (TE, 1), 0) < nv_ref[j]
        lo, hi = _unpack_pairs(jnp.where(live, xs_ref[...], 0))
        xb = jnp.concatenate([lo.astype(BF16), hi.astype(BF16)], axis=1)
        hgu = _bdot(xb, wgu_s[...])
        hh = _silu(hgu[:, :D_EXPERT]) * hgu[:, D_EXPERT:]
        ys_ref[...] = _pack_pairs(_bdot(hh.astype(BF16), wd_s[...]))


def _experts(block_expert, block_rows, n_used, xs, w_gate, w_up, w_down, layer):
    def row_map(j, be, nv, nu):
        return (jnp.minimum(j, nu[0] - 1), 0)

    def w_map(j, be, nv, nu):
        return (layer, be[jnp.minimum(j, nu[0] - 1)], 0, 0)

    return pl.pallas_call(
        _expert_body,
        out_shape=jax.ShapeDtypeStruct((P_ROWS, D_PACK), jnp.int32),
        grid_spec=pltpu.PrefetchScalarGridSpec(
            num_scalar_prefetch=3,
            grid=(NB,),
            in_specs=[
                pl.BlockSpec((TE, D_PACK), row_map),
                pl.BlockSpec((None, None, D_MODEL, D_EXPERT), w_map),
                pl.BlockSpec((None, None, D_MODEL, D_EXPERT), w_map),
                pl.BlockSpec((None, None, D_EXPERT, D_MODEL), w_map),
            ],
            out_specs=pl.BlockSpec((TE, D_PACK), row_map),
            scratch_shapes=[
                pltpu.VMEM((D_MODEL, 2 * D_EXPERT), BF16),
                pltpu.VMEM((D_EXPERT, D_MODEL), BF16),
            ],
        ),
        compiler_params=_params("arbitrary"),
        name="moe_experts",
    )(block_expert, block_rows, n_used, xs, w_gate, w_up, w_down)


def _positions_body(start_ref, eidx_ref, rank_ref, pos_ref):
    eidx = eidx_ref[...]
    base = jnp.zeros(eidx.shape, jnp.int32)
    for e in range(N_EXPERTS):
        base = jnp.where(eidx == e, start_ref[e], base)
    pos_ref[...] = base * TE + rank_ref[...]


def _positions(blk_start, eidx_t, rank_t):
    full = pl.BlockSpec((TOP_K, T), lambda i, s: (0, 0))
    return pl.pallas_call(
        _positions_body,
        out_shape=jax.ShapeDtypeStruct((TOP_K, T), jnp.int32),
        grid_spec=pltpu.PrefetchScalarGridSpec(
            num_scalar_prefetch=1, grid=(1,), in_specs=[full, full], out_specs=full),
        compiler_params=_params("arbitrary"),
        name="moe_positions",
    )(blk_start, eidx_t, rank_t)


def _sc_mesh():
    return plsc.VectorSubcoreMesh(core_axis_name="c", subcore_axis_name="s")


def _sc_worker():
    return lax.axis_index("s") * SC_CORES + lax.axis_index("c")


def _sc_scatter_rows(h, pos_rows):
    c = SC_CHUNK
    n_chunks = T // SC_WORKERS // c
    width = h.shape[1]

    @functools.partial(
        pl.kernel, mesh=_sc_mesh(),
        out_type=jax.ShapeDtypeStruct((P_ROWS, width), h.dtype),
        scratch_types=[pltpu.VMEM((n_chunks * TOP_K, c), jnp.int32),
                       pltpu.VMEM((c, width), h.dtype), pltpu.VMEM((c, width), h.dtype)]
        + [pltpu.SemaphoreType.DMA] * 4,
        name="moe_dispatch_scatter",
    )
    def scatter(h_hbm, pos_hbm, xs_hbm, idx_v, buf0, buf1, sem_in0, sem_in1, sem_out0, sem_out1):
        assert n_chunks % 2 == 0
        bufs, sem_in, sem_out = (buf0, buf1), (sem_in0, sem_in1), (sem_out0, sem_out1)
        first = _sc_worker() * n_chunks
        pltpu.sync_copy(pos_hbm.at[pl.ds(first * TOP_K, n_chunks * TOP_K)], idx_v)

        def load(i, b):
            return pltpu.make_async_copy(h_hbm.at[pl.ds((first + i) * c, c)], bufs[b], sem_in[b])

        def puts(i, b):
            return [pltpu.make_async_copy(bufs[b], xs_hbm.at[idx_v.at[i * TOP_K + k]], sem_out[b])
                    for k in range(TOP_K)]

        load(0, 0).start()
        load(1, 1).start()

        @pl.loop(0, n_chunks, step=2)
        def _(i):
            for b in range(2):
                load(i + b, b).wait()
                for cp in puts(i + b, b):
                    cp.start()
            for b in range(2):
                for cp in puts(i + b, b):
                    cp.wait()

                @pl.when(i + 2 + b < n_chunks)
                def _():
                    load(i + 2 + b, b).start()

    return scatter(h, pos_rows)


def _sc_gather_rows(table, idx):
    c = SC_CHUNK
    n = idx.shape[0]
    per_worker = n // SC_WORKERS
    n_chunks = per_worker // c

    @functools.partial(
        pl.kernel, mesh=_sc_mesh(),
        out_type=jax.ShapeDtypeStruct((n, table.shape[1]), table.dtype),
        scratch_types=[pltpu.VMEM((per_worker,), jnp.int32),
                       pltpu.VMEM((c, table.shape[1]), table.dtype), pltpu.VMEM((c, table.shape[1]), table.dtype)]
        + [pltpu.SemaphoreType.DMA] * 4,
        name="moe_combine_gather",
    )
    def gather(table_hbm, idx_hbm, out_hbm, idx_v, buf0, buf1, sem_in0, sem_in1, sem_out0, sem_out1):
        assert n_chunks % 2 == 0
        bufs, sem_in, sem_out = (buf0, buf1), (sem_in0, sem_in1), (sem_out0, sem_out1)
        base = _sc_worker() * per_worker
        pltpu.sync_copy(idx_hbm.at[pl.ds(base, per_worker)], idx_v)

        def fetch(i, b):
            return pltpu.make_async_copy(table_hbm.at[idx_v.at[pl.ds(i * c, c)]], bufs[b], sem_in[b])

        def flush(i, b):
            return pltpu.make_async_copy(bufs[b], out_hbm.at[pl.ds(base + i * c, c)], sem_out[b])

        fetch(0, 0).start()
        fetch(1, 1).start()

        @pl.loop(0, n_chunks, step=2)
        def _(i):
            for b in range(2):
                fetch(i + b, b).wait()
                flush(i + b, b).start()
            for b in range(2):
                flush(i + b, b).wait()

                @pl.when(i + 2 + b < n_chunks)
                def _():
                    fetch(i + 2 + b, b).start()

    return gather(table, idx)


def _moe_routed(h, eidx_t, rank_t, counts, w_gate, w_up, w_down, layer):
    counts = counts.reshape(1, N_EXPERTS)
    nblk = (counts + TE - 1) // TE
    blk_end = jnp.cumsum(nblk, axis=1)
    blk_start = blk_end - nblk
    blocks = jnp.arange(NB, dtype=jnp.int32).reshape(NB, 1)
    block_expert = jnp.minimum(jnp.sum(blocks >= blk_end, axis=1, keepdims=True), N_EXPERTS - 1)
    mine = block_expert == jnp.arange(N_EXPERTS, dtype=jnp.int32).reshape(1, N_EXPERTS)
    cnt_b = jnp.sum(jnp.where(mine, counts, 0), axis=1, keepdims=True)
    start_b = jnp.sum(jnp.where(mine, blk_start, 0), axis=1, keepdims=True)
    block_rows = jnp.clip(cnt_b - (blocks - start_b) * TE, 0, TE)
    n_used = blk_end[0, -1].reshape(1).astype(jnp.int32)
    pos = _positions(blk_start.reshape(N_EXPERTS).astype(jnp.int32), eidx_t, rank_t)
    pos_rows = pos.reshape(TOP_K, T // SC_CHUNK, SC_CHUNK).transpose(1, 0, 2).reshape(-1, SC_CHUNK)
    xs = _sc_scatter_rows(h, pos_rows)
    ys = _experts(block_expert.reshape(NB).astype(jnp.int32), block_rows.reshape(NB).astype(jnp.int32),
                  n_used, xs, w_gate, w_up, w_down, layer)
    return _sc_gather_rows(ys, pos.reshape(-1)).reshape(TOP_K, T, D_PACK)


N_SLABS = (3 * C_FDIM + 2 * D_MODEL) // 128


def _combine(yg_ref, w8_ref):
    w8 = w8_ref[...]
    lo, hi = _unpack_pairs(yg_ref[0])
    r_lo, r_hi = w8[:, 0:1] * lo, w8[:, 0:1] * hi
    for k in range(1, TOP_K):
        lo, hi = _unpack_pairs(yg_ref[k])
        r_lo, r_hi = r_lo + w8[:, k:k + 1] * lo, r_hi + w8[:, k:k + 1] * hi
    return jnp.concatenate([r_lo, r_hi], axis=1)


def _hgrn_in_body(acc_ref, yg_ref, w8_ref, mod0_ref, mod1_ref, nmix_ref, win_ref, x_ref, z_ref):
    x = acc_ref[...] + mod0_ref[5:6, :] * _combine(yg_ref, w8_ref)
    x_ref[...] = x
    hb = _modulate(x, nmix_ref[...], mod1_ref[0:1, :], mod1_ref[1:2, :]).astype(BF16)
    for s in range(N_SLABS // C_HEADS):
        zz = _bdot(hb, win_ref[:, s * D_MODEL:(s + 1) * D_MODEL])
        for hh in range(C_HEADS):
            z_ref[s * C_HEADS + hh] = zz[:, hh * 128:(hh + 1) * 128]


def _hgrn_in(acc, yg, w8, mod0, mod1, norm_mix, w_in):
    return pl.pallas_call(
        _hgrn_in_body,
        out_shape=(
            jax.ShapeDtypeStruct((T, D_MODEL), F32),
            jax.ShapeDtypeStruct((N_SLABS, T, 128), F32),
        ),
        grid=(NT,),
        in_specs=[
            _tile_spec(),
            pl.BlockSpec((TOP_K, TM, D_PACK), lambda i: (0, i, 0)),
            pl.BlockSpec((TM, TOP_K), lambda i: (i, 0)),
            _mod_spec(), _mod_spec(),
            _const_spec((1, D_MODEL)),
            _const_spec((D_MODEL, 3 * C_FDIM + 2 * D_MODEL)),
        ],
        out_specs=(
            _tile_spec(),
            pl.BlockSpec((N_SLABS, TM, 128), lambda i: (0, i, 0)),
        ),
        compiler_params=_params("parallel"),
        name="hgrn_in_proj",
    )(acc, yg, w8, mod0, mod1, norm_mix.reshape(1, -1), w_in.astype(BF16))


def _gla_body(q_ref, f_ref, v_ref, lb_ref, s0_ref, o_ref, ns_ref, st_ref, *, rev):
    j = pl.program_id(0)
    ti = NT - 1 - j if rev else j
    is_ctx = ti < NT_P
    first_lat = (ti - NT_P) % TILES_PER_LAT == (TILES_PER_LAT - 1 if rev else 0)

    @pl.when(is_ctx)
    def _():
        st_ref[...] = jnp.zeros(st_ref.shape, F32)

    @pl.when(jnp.logical_and(jnp.logical_not(is_ctx), first_lat))
    def _():
        st_ref[...] = s0_ref[...]

    row = lax.broadcasted_iota(jnp.int32, (TM, TM), 0)
    col = lax.broadcasted_iota(jnp.int32, (TM, TM), 1)
    same_chunk = (row // SCAN_CHUNK) == (col // SCAN_CHUNK)
    seen = jnp.logical_and(same_chunk, (col >= row) if rev else (col <= row))
    cum_w = seen.astype(BF16)
    mid = SCAN_CHUNK // 2 if rev else SCAN_CHUNK // 2 - 1
    last = 0 if rev else SCAN_CHUNK - 1
    n_chunks = TM // SCAN_CHUNK
    order = range(n_chunks - 1, -1, -1) if rev else range(n_chunks)
    group = 4

    def chunk_rows(b, off):
        return jnp.concatenate(
            [jnp.broadcast_to(b[c * SCAN_CHUNK + off:c * SCAN_CHUNK + off + 1, :], (SCAN_CHUNK, b.shape[1]))
             for c in range(n_chunks)], axis=0)

    def head_group(gi, carry):
        heads = [gi * group + u for u in range(group)]
        qs, kk, vv, bcum = [], [], [], []
        for hd in heads:
            lb = lb_ref[hd]
            qs.append(_silu(q_ref[hd]) * (C_DK ** -0.5))
            fg = lb + (1.0 - lb) * jax.nn.sigmoid(f_ref[hd])
            kk.append(1.0 - fg)
            vv.append(v_ref[hd].astype(BF16))
            g = jnp.log(fg)
            g_hi = g.astype(BF16)
            r1 = g - g_hi.astype(F32)
            g_mid = r1.astype(BF16)
            g_lo = (r1 - g_mid.astype(F32)).astype(BF16)
            bcum.append(_bdot(cum_w, g_hi) + _bdot(cum_w, g_mid) + _bdot(cum_w, g_lo))
        o_intra, q_dec, kv, decay = [], [], [], []
        for u in range(group):
            b_mid = chunk_rows(bcum[u], mid)
            b_last = chunk_rows(bcum[u], last)
            qe = (qs[u] * jnp.exp(bcum[u] - b_mid)).astype(BF16)
            ke = (kk[u] * jnp.exp(b_mid - bcum[u])).astype(BF16)
            att = jnp.where(seen, _dot_nt(qe, ke), 0.0)
            o_intra.append(_bdot(att.astype(BF16), vv[u]))
            q_dec.append((qs[u] * jnp.exp(bcum[u])).astype(BF16))
            k_dec = (kk[u] * jnp.exp(b_last - bcum[u])).astype(BF16)
            kv.append([_dot_tn(vv[u][c * SCAN_CHUNK:(c + 1) * SCAN_CHUNK], k_dec[c * SCAN_CHUNK:(c + 1) * SCAN_CHUNK])
                       for c in range(n_chunks)])
            decay.append([jnp.exp(bcum[u][c * SCAN_CHUNK + last:c * SCAN_CHUNK + last + 1, :])
                          for c in range(n_chunks)])
        st = [st_ref[hd] for hd in heads]
        for c in order:
            sl = slice(c * SCAN_CHUNK, (c + 1) * SCAN_CHUNK)
            for u, hd in enumerate(heads):
                o_ref[hd, pl.ds(c * SCAN_CHUNK, SCAN_CHUNK), :] = (
                    o_intra[u][sl] + _dot_nt(q_dec[u][sl], st[u].astype(BF16)))
                st[u] = decay[u][c] * st[u] + kv[u][c]
        for u, hd in enumerate(heads):
            st_ref[hd] = st[u]
        return carry

    lax.fori_loop(0, C_HEADS // group, head_group, 0)

    @pl.when(is_ctx)
    def _():
        ns_ref[...] = st_ref[...]


def _gla(z3, lb_dir, s0t_dir, *, rev):
    def ti_of(j):
        return NT - 1 - j if rev else j

    f_slab = 2 if rev else 1

    def lat_map(j):
        return (jnp.clip((ti_of(j) - NT_P) // TILES_PER_LAT, 0, DEC_BATCH - 1), 0, 0, 0)

    return pl.pallas_call(
        functools.partial(_gla_body, rev=rev),
        out_shape=(
            jax.ShapeDtypeStruct((C_HEADS, T, C_DV), F32),
            jax.ShapeDtypeStruct((BATCH, C_HEADS, C_DV, C_DK), F32),
        ),
        grid=(NT,),
        in_specs=[
            pl.BlockSpec((C_HEADS, TM, 128), lambda j: (0, ti_of(j), 0)),
            pl.BlockSpec((C_HEADS, TM, 128), lambda j: (f_slab, ti_of(j), 0)),
            pl.BlockSpec((C_HEADS, TM, 128), lambda j: (3, ti_of(j), 0)),
            _const_spec((C_HEADS, 1, C_DK)),
            pl.BlockSpec((None, C_HEADS, C_DV, C_DK), lat_map),
        ],
        out_specs=(
            pl.BlockSpec((C_HEADS, TM, C_DV), lambda j: (0, ti_of(j), 0)),
            pl.BlockSpec((None, C_HEADS, C_DV, C_DK),
                         lambda j: (jnp.minimum(ti_of(j), NT_P - 1), 0, 0, 0)),
        ),
        scratch_shapes=[pltpu.VMEM((C_HEADS, C_DV, C_DK), F32)],
        compiler_params=_params("arbitrary"),
        name="gla_bwd" if rev else "gla_fwd",
    )(z3, z3, z3, lb_dir, s0t_dir)


def _hgrn_out_body(ofw_ref, obw_ref, gate_ref, x_ref, mod_ref, no_ref, wout_ref, o_ref, cat_ref):
    for hd in range(C_HEADS):
        o = ofw_ref[hd] + obw_ref[hd]
        cat_ref[:, hd * C_DV:(hd + 1) * C_DV] = (_rms(o, no_ref[...]) * _silu(gate_ref[hd])).astype(BF16)
    o_ref[...] = x_ref[...] + mod_ref[2:3, :] * _bdot(cat_ref[...], wout_ref[...])


def _hgrn_out(o_fw, o_bw, z3, x, mod, norm_o, w_out):
    head_spec = pl.BlockSpec((C_HEADS, TM, C_DV), lambda i: (0, i, 0))
    return pl.pallas_call(
        _hgrn_out_body,
        out_shape=jax.ShapeDtypeStruct((T, D_MODEL), F32),
        grid=(NT,),
        in_specs=[
            head_spec, head_spec,
            pl.BlockSpec((C_HEADS, TM, 128), lambda i: (4, i, 0)),
            _tile_spec(), _mod_spec(),
            _const_spec((1, C_DV)),
            _const_spec((D_MODEL, D_MODEL)),
        ],
        out_specs=_tile_spec(),
        scratch_shapes=[pltpu.VMEM((TM, D_MODEL), BF16)],
        compiler_params=_params("parallel"),
        name="hgrn_out_proj",
    )(o_fw, o_bw, z3, x, mod, norm_o.reshape(1, -1), w_out.astype(BF16))


def _final_body(acc_ref, yg_ref, w8_ref, mod_ref, nf_ref, o_ref):
    x = acc_ref[...] + mod_ref[5:6, :] * _combine(yg_ref, w8_ref)
    o_ref[...] = _rms(x, nf_ref[...])


def _final(acc, yg, w8, mod, norm_final, tile0, n_tiles):
    in_tile = pl.BlockSpec((TM, D_MODEL), lambda i: (i + tile0, 0))
    return pl.pallas_call(
        _final_body,
        out_shape=jax.ShapeDtypeStruct((n_tiles * TM, D_MODEL), F32),
        grid=(n_tiles,),
        in_specs=[
            in_tile,
            pl.BlockSpec((TOP_K, TM, D_PACK), lambda i: (0, i + tile0, 0)),
            pl.BlockSpec((TM, TOP_K), lambda i: (i + tile0, 0)),
            pl.BlockSpec((None, 6, D_MODEL), lambda i: (_mod_row(i + tile0), 0, 0)),
            _const_spec((1, D_MODEL)),
        ],
        out_specs=pl.BlockSpec((TM, D_MODEL), lambda i: (i, 0)),
        compiler_params=_params("parallel"),
        name="final_norm",
    )(acc, yg, w8, mod, norm_final.reshape(1, -1))


def kernel(x_prompt, x_sample, state_hgrn, c, c_ctx, w_ada, b_ada, norm_mix, norm_ffn, w_out, w_in_ab, w_sg, b_sg, norm_sg, w_dw, b_dw, norm_cv, w_in_hgrn, lb_raw, norm_o, w_router, b_router, w_gate, w_up, w_down, w_sh_gate, w_sh_up, w_sh_down, norm_final):
    x = jnp.concatenate([x_prompt.reshape(T_P, D_MODEL), x_sample.reshape(T_S, D_MODEL)], axis=0)
    cvecs = jnp.concatenate(
        [c_ctx.reshape(1, D_MODEL), c, jnp.zeros((N_MOD_ROWS - 1 - DEC_BATCH, D_MODEL), F32)], axis=0)
    mods = _ada_tables(cvecs, w_ada, b_ada)
    lb_sm = jax.nn.softmax(lb_raw.astype(F32), axis=0)
    lb1 = (jnp.cumsum(lb_sm, axis=0) - lb_sm[0])[1].reshape(2, C_HEADS, 1, C_DK)

    def moe(l, xin):
        h, eidx_t, w8_t, rank_t, counts, acc = _moe_pre(
            xin, mods[l], norm_ffn[l], w_router[l], b_router[l], w_sh_gate[l], w_sh_up[l], w_sh_down[l])
        return acc, _moe_routed(h, eidx_t, rank_t, counts, w_gate, w_up, w_down, l), w8_t.T

    x = _l0_mixer(x, mods[0], norm_mix[0], w_in_ab[0], w_sg[0], b_sg[0], norm_sg[0], w_dw[0],
                  b_dw[0], norm_cv[0], w_out[0])
    acc, yg, w8 = moe(0, x)
    x, z3 = _hgrn_in(acc, yg, w8, mods[0], mods[1], norm_mix[1], w_in_hgrn[0])
    s0t = jnp.swapaxes(state_hgrn[:, 0].astype(F32), -1, -2)
    o_fw, ns_fw = _gla(z3, lb1[0], s0t[:, 0], rev=False)
    o_bw, ns_bw = _gla(z3, lb1[1], s0t[:, 1], rev=True)
    x = _hgrn_out(o_fw, o_bw, z3, x, mods[1], norm_o[0], w_out[1])
    acc, yg, w8 = moe(1, x)
    y_p = _final(acc, yg, w8, mods[1], norm_final, 0, NT_P).reshape(BATCH, SEQ, D_MODEL)
    y_s = _final(acc, yg, w8, mods[1], norm_final, NT_P, NT_S).reshape(DEC_BATCH, DEC_SEQ, D_MODEL)
    new_state = jnp.swapaxes(jnp.stack([ns_fw, ns_bw], axis=1), -1, -2)[:, None]
    return (y_p, y_s, new_state)
```

```python
import functools

import jax
import jax.numpy as jnp
from jax import lax
from jax.experimental import pallas as pl
from jax.experimental.pallas import tpu as pltpu
from jax.experimental.pallas import tpu_sc as plsc

F32 = jnp.float32
BF16 = jnp.bfloat16
HIGHEST = lax.Precision.HIGHEST

D_MODEL = 1024
BATCH = 32
SEQ = 256
DEPTH = 2
DEC_BATCH = 8
DEC_SEQ = 2048
GRID_W = 64
A_WIDTH = D_MODEL // 2
A_GROUPS = 4
A_GC = A_WIDTH // A_GROUPS
CHUNK_A = 128
B_WIDTH = D_MODEL - A_WIDTH
CONV_W = 31
CONV_PAD = CONV_W // 2
C_HEADS = 8
C_DK = 128
C_DV = D_MODEL // C_HEADS
C_FDIM = C_HEADS * C_DK
SCAN_CHUNK = 64
N_EXPERTS = 64
TOP_K = 8
N_GROUPS = 8
GROUP_SIZE = N_EXPERTS // N_GROUPS
TOPK_GROUPS = 4
D_EXPERT = 256
D_SHARED = 256
ROUTED_SCALE = 2.5
EPS = 1e-6

TM = 256
T_P = BATCH * SEQ
T_S = DEC_BATCH * DEC_SEQ
T = T_P + T_S
NT_P = T_P // TM
NT_S = T_S // TM
NT = NT_P + NT_S
TILES_PER_LAT = DEC_SEQ // TM
N_PAIRS = T * TOP_K
TE = 512
NB = N_PAIRS // TE + N_EXPERTS
P_ROWS = NB * TE
D_PACK = D_MODEL // 2
N_MOD_ROWS = 16
CONV_HALO = 16
VMEM_LIMIT = 48 * 1024 * 1024
SC_CORES = 2
SC_SUBCORES = 16
SC_WORKERS = SC_CORES * SC_SUBCORES
SC_CHUNK = 64


def _mod_row(i):
    return jnp.where(i < NT_P, 0, 1 + (i - NT_P) // TILES_PER_LAT)


def _silu(x):
    return x * jax.nn.sigmoid(x)


def _gelu(x):
    return x * (0.5 * (1.0 + jnp.tanh(0.7978845608028654 * (x + 0.044715 * (x * x * x)))))


def _rms(x, g):
    return x * lax.rsqrt(jnp.mean(x * x, axis=-1, keepdims=True) + EPS) * g


def _layernorm(x, g):
    xc = x - jnp.mean(x, axis=-1, keepdims=True)
    return xc * lax.rsqrt(jnp.mean(xc * xc, axis=-1, keepdims=True) + EPS) * g


def _modulate(x, g, shift, scale):
    return _rms(x, g) * (1.0 + scale) + shift


def _bdot(a, b):
    return jnp.dot(a, b, preferred_element_type=F32)


def _dot_nt(a, b, precision=None):
    return lax.dot_general(a, b, (((1,), (1,)), ((), ())), precision=precision,
                           preferred_element_type=F32)


def _dot_tn(a, b):
    return lax.dot_general(a, b, (((0,), (0,)), ((), ())), preferred_element_type=F32)


def _pack_pairs(x):
    m = x.shape[1] // 2
    lo = lax.bitcast_convert_type(x[:, :m].astype(BF16).astype(F32), jnp.uint32)
    hi = lax.bitcast_convert_type(x[:, m:].astype(BF16).astype(F32), jnp.uint32)
    return lax.bitcast_convert_type(hi | (lo >> 16), jnp.int32)


def _unpack_pairs(w):
    u = lax.bitcast_convert_type(w, jnp.uint32)
    lo = lax.bitcast_convert_type(u << 16, F32)
    hi = lax.bitcast_convert_type(u & jnp.uint32(0xFFFF0000), F32)
    return lo, hi


def _params(*sem):
    return pltpu.CompilerParams(dimension_semantics=sem, vmem_limit_bytes=VMEM_LIMIT)


def _const_spec(shape):
    nd = len(shape)
    return pl.BlockSpec(shape, lambda *_: (0,) * nd)


def _ada_body(c_ref, w_ref, b_ref, o_ref):
    s = _silu(c_ref[...])
    o_ref[...] = jnp.dot(s, w_ref[...], precision=HIGHEST, preferred_element_type=F32) + b_ref[...]


def _ada_tables(cvecs, w_ada, b_ada):
    out = pl.pallas_call(
        _ada_body,
        out_shape=jax.ShapeDtypeStruct((DEPTH, N_MOD_ROWS, 6 * D_MODEL), F32),
        grid=(DEPTH, 6),
        in_specs=[
            _const_spec((N_MOD_ROWS, D_MODEL)),
            pl.BlockSpec((None, D_MODEL, D_MODEL), lambda l, j: (l, 0, j)),
            pl.BlockSpec((None, 1, D_MODEL), lambda l, j: (l, 0, j)),
        ],
        out_specs=pl.BlockSpec((None, N_MOD_ROWS, D_MODEL), lambda l, j: (l, 0, j)),
        compiler_params=_params("parallel", "parallel"),
        name="ada_tables",
    )(cvecs, w_ada, b_ada.reshape(DEPTH, 1, 6 * D_MODEL))
    return out.reshape(DEPTH, N_MOD_ROWS, 6, D_MODEL)


def _mod_spec():
    return pl.BlockSpec((None, 6, D_MODEL), lambda i: (_mod_row(i), 0, 0))


def _tile_spec():
    return pl.BlockSpec((TM, D_MODEL), lambda i: (i, 0))


SUBLANES = 8


def _conv_segment(pad_ref, shift_ref, conv_ref, wdw_ref, pad_base, out_base, seg):
    rb = min(seg, 64)
    for cb in range(B_WIDTH // 128):
        cs = slice(cb * 128, (cb + 1) * 128)
        for r0 in range(0, seg, rb):
            acc = jnp.zeros((rb, 128), F32)
            for k in range(CONV_W):
                b = (CONV_HALO - CONV_PAD + k) % SUBLANES
                off = pad_base + r0 + CONV_HALO - CONV_PAD + k - b
                src = pad_ref if b == 0 else shift_ref.at[b - 1]
                acc = acc + wdw_ref[k:k + 1, cs] * src[off:off + rb, cs]
            conv_ref[out_base + r0:out_base + r0 + rb, cs] = acc


def _l0_body(xc_ref, xl_ref, mod_ref, nmix_ref, win_ref, wsg_ref, bsg_ref, nsg_ref, wdw_ref, bdw_ref,
             ncv_ref, wout_ref, o_ref, cat_ref, pad_ref, shift_ref, conv_ref):
    i = pl.program_id(0)
    x = jnp.where(i < NT_P, xc_ref[...], xl_ref[...])
    h = _modulate(x, nmix_ref[...], mod_ref[0:1, :], mod_ref[1:2, :])
    z = _bdot(h.astype(BF16), win_ref[...])
    u = _gelu(z[:, :A_WIDTH])
    vb = _layernorm(_gelu(z[:, A_WIDTH:2 * A_WIDTH]), nsg_ref[...]).astype(BF16)
    for n in range(TM // CHUNK_A):
        rs = slice(n * CHUNK_A, (n + 1) * CHUNK_A)
        for g in range(A_GROUPS):
            cs = slice(g * A_GC, (g + 1) * A_GC)
            m = _bdot(wsg_ref[g], vb[rs, cs]) + bsg_ref[g]
            cat_ref[rs, cs] = (u[rs, cs] * m).astype(BF16)
    hb = z[:, 2 * A_WIDTH:2 * A_WIDTH + B_WIDTH] * jax.nn.sigmoid(z[:, 2 * A_WIDTH + B_WIDTH:])

    def conv_tile(seg):
        stride = seg + 2 * CONV_HALO
        halo = jnp.zeros((CONV_HALO, B_WIDTH), F32)
        for s in range(TM // seg):
            b = s * stride
            pad_ref[b:b + CONV_HALO, :] = halo
            pad_ref[b + CONV_HALO:b + CONV_HALO + seg, :] = hb[s * seg:(s + 1) * seg, :]
            pad_ref[b + CONV_HALO + seg:b + stride, :] = halo
        rows = (TM // seg) * stride - SUBLANES
        for b in range(1, SUBLANES):
            shift_ref[b - 1, 0:rows, :] = pad_ref[b:b + rows, :]
        for s in range(TM // seg):
            _conv_segment(pad_ref, shift_ref, conv_ref, wdw_ref, s * stride, s * seg, seg)

    @pl.when(i < NT_P)
    def _():
        conv_tile(SEQ)

    @pl.when(i >= NT_P)
    def _():
        conv_tile(GRID_W)

    yb = _layernorm(conv_ref[...] + bdw_ref[...], ncv_ref[...])
    cat_ref[:, A_WIDTH:] = _silu(yb).astype(BF16)
    out = _bdot(cat_ref[...], wout_ref[...])
    o_ref[...] = x + mod_ref[2:3, :] * out


def _l0_mixer(x_ctx, x_lat, mod, norm_mix, w_in, w_sg, b_sg, norm_sg, w_dw, b_dw, norm_cv, w_out):
    assert TM == SEQ and TM % GRID_W == 0 and TM % CHUNK_A == 0
    pad_rows = (TM // GRID_W) * (GRID_W + 2 * CONV_HALO)
    assert pad_rows >= SEQ + 2 * CONV_HALO
    return pl.pallas_call(
        _l0_body,
        out_shape=jax.ShapeDtypeStruct((T, D_MODEL), F32),
        grid=(NT,),
        in_specs=[
            pl.BlockSpec((TM, D_MODEL), lambda i: (jnp.minimum(i, NT_P - 1), 0)),
            pl.BlockSpec((TM, D_MODEL), lambda i: (jnp.maximum(i - NT_P, 0), 0)),
            _mod_spec(),
            _const_spec((1, D_MODEL)),
            _const_spec((D_MODEL, 2 * A_WIDTH + 2 * B_WIDTH)),
            _const_spec((A_GROUPS, CHUNK_A, CHUNK_A)),
            _const_spec((A_GROUPS, CHUNK_A, 1)),
            _const_spec((1, A_WIDTH)),
            _const_spec((CONV_W, B_WIDTH)),
            _const_spec((1, B_WIDTH)),
            _const_spec((1, B_WIDTH)),
            _const_spec((D_MODEL, D_MODEL)),
        ],
        out_specs=_tile_spec(),
        scratch_shapes=[
            pltpu.VMEM((TM, D_MODEL), BF16),
            pltpu.VMEM((pad_rows, B_WIDTH), F32),
            pltpu.VMEM((SUBLANES - 1, pad_rows, B_WIDTH), F32),
            pltpu.VMEM((TM, B_WIDTH), F32),
        ],
        compiler_params=_params("parallel"),
        name="l0_mixer",
    )(x_ctx, x_lat, mod, norm_mix.reshape(1, -1), w_in.astype(BF16), w_sg.astype(BF16),
      b_sg.reshape(A_GROUPS, CHUNK_A, 1), norm_sg.reshape(1, -1), w_dw, b_dw.reshape(1, -1),
      norm_cv.reshape(1, -1), w_out.astype(BF16))


def _route(scores, biased):
    n = scores.shape[-1]
    shp = (N_GROUPS, GROUP_SIZE, n)
    s3 = scores.reshape(shp)
    b3 = biased.reshape(shp)
    m_iota = lax.broadcasted_iota(jnp.int32, shp, 1).astype(F32)
    g_iota = lax.broadcasted_iota(jnp.int32, shp, 0).astype(F32)
    e_iota = g_iota * GROUP_SIZE + m_iota
    neg = -jnp.inf

    def amax1(v):
        return jnp.max(v, axis=1, keepdims=True)

    def amin1(v):
        return jnp.min(v, axis=1, keepdims=True)

    m1 = amax1(b3)
    i1 = amin1(jnp.where(b3 == m1, m_iota, float(GROUP_SIZE)))
    m2 = amax1(jnp.where(m_iota == i1, neg, b3))
    grp = m1 + m2
    gi1 = lax.broadcasted_iota(jnp.int32, grp.shape, 0).astype(F32)
    gmask = jnp.zeros(grp.shape, jnp.bool_)
    for _ in range(TOPK_GROUPS):
        gm = jnp.max(grp, axis=0, keepdims=True)
        gi = jnp.min(jnp.where(grp == gm, gi1, float(N_GROUPS)), axis=0, keepdims=True)
        hit = gi1 == gi
        gmask = jnp.logical_or(gmask, hit)
        grp = jnp.where(hit, neg, grp)
    cand = jnp.where(gmask, b3, neg)
    ids, vals, hits = [], [], []
    for _ in range(TOP_K):
        mx = jnp.max(amax1(cand), axis=0, keepdims=True)
        ei = jnp.min(amin1(jnp.where(cand == mx, e_iota, float(N_EXPERTS))), axis=0, keepdims=True)
        hit = e_iota == ei
        ids.append(ei.reshape(1, n))
        vals.append(_pick(hit, s3))
        hits.append(hit)
        cand = jnp.where(hit, neg, cand)
    return jnp.concatenate(ids, axis=0).astype(jnp.int32), jnp.concatenate(vals, axis=0), hits


def _pick(hit, v3):
    s = jnp.sum(jnp.sum(jnp.where(hit, v3, 0.0), axis=1, keepdims=True), axis=0, keepdims=True)
    return s.reshape(1, v3.shape[-1])


def _moe_pre_body(x_ref, mod_ref, nffn_ref, wrt_ref, br_ref, wsg_ref, wsu_ref, wsd_ref,
                  h_ref, eidx_ref, w8_ref, rank_ref, cnt_ref, acc_ref, run_ref):
    @pl.when(pl.program_id(0) == 0)
    def _():
        run_ref[...] = jnp.zeros(run_ref.shape, F32)

    x = x_ref[...]
    h = _modulate(x, nffn_ref[...], mod_ref[3:4, :], mod_ref[4:5, :])
    h_ref[...] = _pack_pairs(h)
    logits_t = _dot_nt(wrt_ref[...], h, precision=HIGHEST)
    scores = jax.nn.sigmoid(logits_t)
    eidx, sv, hits = _route(scores, scores + br_ref[...])
    eidx_ref[...] = eidx
    w8_ref[...] = sv / jnp.sum(sv, axis=0, keepdims=True) * ROUTED_SCALE
    sel3 = hits[0]
    for hit in hits[1:]:
        sel3 = jnp.logical_or(sel3, hit)
    sel = sel3.astype(F32).reshape(N_EXPERTS, TM)
    earlier = (lax.broadcasted_iota(jnp.int32, (TM, TM), 0)
               < lax.broadcasted_iota(jnp.int32, (TM, TM), 1)).astype(BF16)
    rank3 = (_bdot(sel.astype(BF16), earlier) + run_ref[...]).reshape(N_GROUPS, GROUP_SIZE, TM)
    rank_ref[...] = jnp.concatenate([_pick(hit, rank3) for hit in hits], axis=0).astype(jnp.int32)
    run_ref[...] = run_ref[...] + jnp.sum(sel, axis=1, keepdims=True)
    cnt_ref[...] = run_ref[...].astype(jnp.int32)
    hb = h.astype(BF16)
    sh = _bdot((_silu(_bdot(hb, wsg_ref[...])) * _bdot(hb, wsu_ref[...])).astype(BF16), wsd_ref[...])
    acc_ref[...] = x + mod_ref[5:6, :] * sh


def _moe_pre(x, mod, norm_ffn, w_router, b_router, w_sh_gate, w_sh_up, w_sh_down):
    return pl.pallas_call(
        _moe_pre_body,
        out_shape=(
            jax.ShapeDtypeStruct((T, D_PACK), jnp.int32),
            jax.ShapeDtypeStruct((TOP_K, T), jnp.int32),
            jax.ShapeDtypeStruct((TOP_K, T), F32),
            jax.ShapeDtypeStruct((TOP_K, T), jnp.int32),
            jax.ShapeDtypeStruct((N_EXPERTS, 1), jnp.int32),
            jax.ShapeDtypeStruct((T, D_MODEL), F32),
        ),
        grid=(NT,),
        in_specs=[
            _tile_spec(), _mod_spec(),
            _const_spec((1, D_MODEL)),
            _const_spec((N_EXPERTS, D_MODEL)),
            _const_spec((N_EXPERTS, 1)),
            _const_spec((D_MODEL, D_SHARED)),
            _const_spec((D_MODEL, D_SHARED)),
            _const_spec((D_SHARED, D_MODEL)),
        ],
        out_specs=(
            pl.BlockSpec((TM, D_PACK), lambda i: (i, 0)),
            pl.BlockSpec((TOP_K, TM), lambda i: (0, i)),
            pl.BlockSpec((TOP_K, TM), lambda i: (0, i)),
            pl.BlockSpec((TOP_K, TM), lambda i: (0, i)),
            _const_spec((N_EXPERTS, 1)),
            _tile_spec(),
        ),
        scratch_shapes=[pltpu.VMEM((N_EXPERTS, 1), F32)],
        compiler_params=_params("arbitrary"),
        name="moe_router_shared",
    )(x, mod, norm_ffn.reshape(1, -1), w_router.T, b_router.reshape(N_EXPERTS, 1),
      w_sh_gate.astype(BF16), w_sh_up.astype(BF16), w_sh_down.astype(BF16))


EXPERT_RING = 3


def _expert_body(be_ref, nv_ref, nu_ref, xs_hbm, wg_ref, wu_ref, wd_ref, ys_ref, wgu_s, wd_s, ring, sems):
    j = pl.program_id(0)
    n_used = nu_ref[0]
    live_tile = j < n_used

    def fetch(t):
        slot = t % EXPERT_RING
        return pltpu.make_async_copy(xs_hbm.at[pl.ds(t * TE, TE)], ring.at[slot], sems.at[slot])

    @pl.when(j == 0)
    def _():
        for t in range(EXPERT_RING - 1):
            @pl.when(t < n_used)
            def _():
                fetch(t).start()

    @pl.when(j + (EXPERT_RING - 1) < n_used)
    def _():
        fetch(j + (EXPERT_RING - 1)).start()

    @pl.when(jnp.logical_and(live_tile, jnp.logical_or(j == 0, be_ref[j] != be_ref[jnp.maximum(j - 1, 0)])))
    def _():
        wgu_s[:, :D_EXPERT] = wg_ref[...].astype(BF16)
        wgu_s[:, D_EXPERT:] = wu_ref[...].astype(BF16)
        wd_s[...] = wd_ref[...].astype(BF16)

    @pl.when(live_tile)
    def _():
        fetch(j).wait()
        live = lax.broadcasted_iota(jnp.int32, (TE, 1), 0) < nv_ref[j]
        lo, hi = _unpack_pairs(jnp.where(live, ring[j % EXPERT_RING], 0))
        xb = jnp.concatenate([lo.astype(BF16), hi.astype(BF16)], axis=1)
        hgu = _bdot(xb, wgu_s[...])
        hh = _silu(hgu[:, :D_EXPERT]) * hgu[:, D_EXPERT:]
        ys_ref[...] = _pack_pairs(_bdot(hh.astype(BF16), wd_s[...]))


def _experts(block_expert, block_rows, n_used, xs, w_gate, w_up, w_down, layer):
    def row_map(j, be, nv, nu):
        return (jnp.minimum(j, nu[0] - 1), 0)

    def w_map(j, be, nv, nu):
        return (layer, be[jnp.minimum(j, nu[0] - 1)], 0, 0)

    return pl.pallas_call(
        _expert_body,
        out_shape=jax.ShapeDtypeStruct((P_ROWS, D_PACK), jnp.int32),
        grid_spec=pltpu.PrefetchScalarGridSpec(
            num_scalar_prefetch=3,
            grid=(NB,),
            in_specs=[
                pl.BlockSpec(memory_space=pl.ANY),
                pl.BlockSpec((None, None, D_MODEL, D_EXPERT), w_map),
                pl.BlockSpec((None, None, D_MODEL, D_EXPERT), w_map),
                pl.BlockSpec((None, None, D_EXPERT, D_MODEL), w_map),
            ],
            out_specs=pl.BlockSpec((TE, D_PACK), row_map),
            scratch_shapes=[
                pltpu.VMEM((D_MODEL, 2 * D_EXPERT), BF16),
                pltpu.VMEM((D_EXPERT, D_MODEL), BF16),
                pltpu.VMEM((EXPERT_RING, TE, D_PACK), jnp.int32),
                pltpu.SemaphoreType.DMA((EXPERT_RING,)),
            ],
        ),
        compiler_params=_params("arbitrary"),
        name="moe_experts",
    )(block_expert, block_rows, n_used, xs, w_gate, w_up, w_down)


def _positions_body(start_ref, eidx_ref, rank_ref, pos_ref):
    eidx = eidx_ref[...]
    base = jnp.zeros(eidx.shape, jnp.int32)
    for e in range(N_EXPERTS):
        base = jnp.where(eidx == e, start_ref[e], base)
    pos_ref[...] = base * TE + rank_ref[...]


def _positions(blk_start, eidx_t, rank_t):
    full = pl.BlockSpec((TOP_K, T), lambda i, s: (0, 0))
    return pl.pallas_call(
        _positions_body,
        out_shape=jax.ShapeDtypeStruct((TOP_K, T), jnp.int32),
        grid_spec=pltpu.PrefetchScalarGridSpec(
            num_scalar_prefetch=1, grid=(1,), in_specs=[full, full], out_specs=full),
        compiler_params=_params("arbitrary"),
        name="moe_positions",
    )(blk_start, eidx_t, rank_t)


def _sc_mesh():
    return plsc.VectorSubcoreMesh(core_axis_name="c", subcore_axis_name="s")


def _sc_worker():
    return lax.axis_index("s") * SC_CORES + lax.axis_index("c")


def _sc_scatter_rows(h, pos_rows):
    c = SC_CHUNK
    n_chunks = T // SC_WORKERS // c
    width = h.shape[1]

    @functools.partial(
        pl.kernel, mesh=_sc_mesh(),
        out_type=jax.ShapeDtypeStruct((P_ROWS, width), h.dtype),
        scratch_types=[pltpu.VMEM((n_chunks * TOP_K, c), jnp.int32),
                       pltpu.VMEM((c, width), h.dtype), pltpu.VMEM((c, width), h.dtype)]
        + [pltpu.SemaphoreType.DMA] * 4,
        name="moe_dispatch_scatter",
    )
    def scatter(h_hbm, pos_hbm, xs_hbm, idx_v, buf0, buf1, sem_in0, sem_in1, sem_out0, sem_out1):
        assert n_chunks % 2 == 0
        bufs, sem_in, sem_out = (buf0, buf1), (sem_in0, sem_in1), (sem_out0, sem_out1)
        first = _sc_worker() * n_chunks
        pltpu.sync_copy(pos_hbm.at[pl.ds(first * TOP_K, n_chunks * TOP_K)], idx_v)

        def load(i, b):
            return pltpu.make_async_copy(h_hbm.at[pl.ds((first + i) * c, c)], bufs[b], sem_in[b])

        def puts(i, b):
            return [pltpu.make_async_copy(bufs[b], xs_hbm.at[idx_v.at[i * TOP_K + k]], sem_out[b])
                    for k in range(TOP_K)]

        load(0, 0).start()
        load(1, 1).start()

        @pl.loop(0, n_chunks, step=2)
        def _(i):
            for b in range(2):
                load(i + b, b).wait()
                for cp in puts(i + b, b):
                    cp.start()
            for b in range(2):
                for cp in puts(i + b, b):
                    cp.wait()

                @pl.when(i + 2 + b < n_chunks)
                def _():
                    load(i + 2 + b, b).start()

    return scatter(h, pos_rows)


def _sc_gather_rows(table, idx):
    c = SC_CHUNK
    n = idx.shape[0]
    per_worker = n // SC_WORKERS
    n_chunks = per_worker // c

    @functools.partial(
        pl.kernel, mesh=_sc_mesh(),
        out_type=jax.ShapeDtypeStruct((n, table.shape[1]), table.dtype),
        scratch_types=[pltpu.VMEM((per_worker,), jnp.int32),
                       pltpu.VMEM((c, table.shape[1]), table.dtype), pltpu.VMEM((c, table.shape[1]), table.dtype)]
        + [pltpu.SemaphoreType.DMA] * 4,
        name="moe_combine_gather",
    )
    def gather(table_hbm, idx_hbm, out_hbm, idx_v, buf0, buf1, sem_in0, sem_in1, sem_out0, sem_out1):
        assert n_chunks % 2 == 0
        bufs, sem_in, sem_out = (buf0, buf1), (sem_in0, sem_in1), (sem_out0, sem_out1)
        base = _sc_worker() * per_worker
        pltpu.sync_copy(idx_hbm.at[pl.ds(base, per_worker)], idx_v)

        def fetch(i, b):
            return pltpu.make_async_copy(table_hbm.at[idx_v.at[pl.ds(i * c, c)]], bufs[b], sem_in[b])

        def flush(i, b):
            return pltpu.make_async_copy(bufs[b], out_hbm.at[pl.ds(base + i * c, c)], sem_out[b])

        fetch(0, 0).start()
        fetch(1, 1).start()

        @pl.loop(0, n_chunks, step=2)
        def _(i):
            for b in range(2):
                fetch(i + b, b).wait()
                flush(i + b, b).start()
            for b in range(2):
                flush(i + b, b).wait()

                @pl.when(i + 2 + b < n_chunks)
                def _():
                    fetch(i + 2 + b, b).start()

    return gather(table, idx)


def _moe_routed(h, eidx_t, rank_t, counts, w_gate, w_up, w_down, layer):
    counts = counts.reshape(1, N_EXPERTS)
    nblk = (counts + TE - 1) // TE
    blk_end = jnp.cumsum(nblk, axis=1)
    blk_start = blk_end - nblk
    blocks = jnp.arange(NB, dtype=jnp.int32).reshape(NB, 1)
    block_expert = jnp.minimum(jnp.sum(blocks >= blk_end, axis=1, keepdims=True), N_EXPERTS - 1)
    mine = block_expert == jnp.arange(N_EXPERTS, dtype=jnp.int32).reshape(1, N_EXPERTS)
    cnt_b = jnp.sum(jnp.where(mine, counts, 0), axis=1, keepdims=True)
    start_b = jnp.sum(jnp.where(mine, blk_start, 0), axis=1, keepdims=True)
    block_rows = jnp.clip(cnt_b - (blocks - start_b) * TE, 0, TE)
    n_used = blk_end[0, -1].reshape(1).astype(jnp.int32)
    pos = _positions(blk_start.reshape(N_EXPERTS).astype(jnp.int32), eidx_t, rank_t)
    pos_rows = pos.reshape(TOP_K, T // SC_CHUNK, SC_CHUNK).transpose(1, 0, 2).reshape(-1, SC_CHUNK)
    xs = _sc_scatter_rows(h, pos_rows)
    ys = _experts(block_expert.reshape(NB).astype(jnp.int32), block_rows.reshape(NB).astype(jnp.int32),
                  n_used, xs, w_gate, w_up, w_down, layer)
    return _sc_gather_rows(ys, pos.reshape(-1)).reshape(TOP_K, T, D_PACK)


N_SLABS = (3 * C_FDIM + 2 * D_MODEL) // 128


def _combine(yg_ref, w8_ref):
    w8 = w8_ref[...]
    lo, hi = _unpack_pairs(yg_ref[0])
    r_lo, r_hi = w8[:, 0:1] * lo, w8[:, 0:1] * hi
    for k in range(1, TOP_K):
        lo, hi = _unpack_pairs(yg_ref[k])
        r_lo, r_hi = r_lo + w8[:, k:k + 1] * lo, r_hi + w8[:, k:k + 1] * hi
    return jnp.concatenate([r_lo, r_hi], axis=1)


def _hgrn_in_body(acc_ref, yg_ref, w8_ref, mod0_ref, mod1_ref, nmix_ref, win_ref, x_ref, z_ref):
    x = acc_ref[...] + mod0_ref[5:6, :] * _combine(yg_ref, w8_ref)
    x_ref[...] = x
    hb = _modulate(x, nmix_ref[...], mod1_ref[0:1, :], mod1_ref[1:2, :]).astype(BF16)
    for s in range(N_SLABS // C_HEADS):
        zz = _bdot(hb, win_ref[:, s * D_MODEL:(s + 1) * D_MODEL])
        for hh in range(C_HEADS):
            z_ref[s * C_HEADS + hh] = zz[:, hh * 128:(hh + 1) * 128]


def _hgrn_in(acc, yg, w8, mod0, mod1, norm_mix, w_in):
    return pl.pallas_call(
        _hgrn_in_body,
        out_shape=(
            jax.ShapeDtypeStruct((T, D_MODEL), F32),
            jax.ShapeDtypeStruct((N_SLABS, T, 128), F32),
        ),
        grid=(NT,),
        in_specs=[
            _tile_spec(),
            pl.BlockSpec((TOP_K, TM, D_PACK), lambda i: (0, i, 0)),
            pl.BlockSpec((TM, TOP_K), lambda i: (i, 0)),
            _mod_spec(), _mod_spec(),
            _const_spec((1, D_MODEL)),
            _const_spec((D_MODEL, 3 * C_FDIM + 2 * D_MODEL)),
        ],
        out_specs=(
            _tile_spec(),
            pl.BlockSpec((N_SLABS, TM, 128), lambda i: (0, i, 0)),
        ),
        compiler_params=_params("parallel"),
        name="hgrn_in_proj",
    )(acc, yg, w8, mod0, mod1, norm_mix.reshape(1, -1), w_in.astype(BF16))


def _gla_body(q_ref, f_ref, v_ref, lb_ref, s0_ref, o_ref, ns_ref, st_ref, *, rev):
    j = pl.program_id(0)
    ti = NT - 1 - j if rev else j
    is_ctx = ti < NT_P
    first_lat = (ti - NT_P) % TILES_PER_LAT == (TILES_PER_LAT - 1 if rev else 0)

    @pl.when(is_ctx)
    def _():
        st_ref[...] = jnp.zeros(st_ref.shape, F32)

    @pl.when(jnp.logical_and(jnp.logical_not(is_ctx), first_lat))
    def _():
        st_ref[...] = s0_ref[...]

    row = lax.broadcasted_iota(jnp.int32, (TM, TM), 0)
    col = lax.broadcasted_iota(jnp.int32, (TM, TM), 1)
    same_chunk = (row // SCAN_CHUNK) == (col // SCAN_CHUNK)
    seen = jnp.logical_and(same_chunk, (col >= row) if rev else (col <= row))
    cum_w = seen.astype(BF16)
    mid = SCAN_CHUNK // 2 if rev else SCAN_CHUNK // 2 - 1
    last = 0 if rev else SCAN_CHUNK - 1
    n_chunks = TM // SCAN_CHUNK
    order = range(n_chunks - 1, -1, -1) if rev else range(n_chunks)
    group = 4

    def chunk_rows(b, off):
        return jnp.concatenate(
            [jnp.broadcast_to(b[c * SCAN_CHUNK + off:c * SCAN_CHUNK + off + 1, :], (SCAN_CHUNK, b.shape[1]))
             for c in range(n_chunks)], axis=0)

    def head_group(gi, carry):
        heads = [gi * group + u for u in range(group)]
        qs, kk, vv, bcum = [], [], [], []
        for hd in heads:
            lb = lb_ref[hd]
            qs.append(_silu(q_ref[hd]) * (C_DK ** -0.5))
            fg = lb + (1.0 - lb) * jax.nn.sigmoid(f_ref[hd])
            kk.append(1.0 - fg)
            vv.append(v_ref[hd].astype(BF16))
            g = jnp.log(fg)
            g_hi = g.astype(BF16)
            r1 = g - g_hi.astype(F32)
            g_mid = r1.astype(BF16)
            g_lo = (r1 - g_mid.astype(F32)).astype(BF16)
            bcum.append(_bdot(cum_w, g_hi) + _bdot(cum_w, g_mid) + _bdot(cum_w, g_lo))
        o_intra, q_dec, kv, decay = [], [], [], []
        for u in range(group):
            b_mid = chunk_rows(bcum[u], mid)
            b_last = chunk_rows(bcum[u], last)
            qe = (qs[u] * jnp.exp(bcum[u] - b_mid)).astype(BF16)
            ke = (kk[u] * jnp.exp(b_mid - bcum[u])).astype(BF16)
            att = jnp.where(seen, _dot_nt(qe, ke), 0.0)
            o_intra.append(_bdot(att.astype(BF16), vv[u]))
            q_dec.append((qs[u] * jnp.exp(bcum[u])).astype(BF16))
            k_dec = (kk[u] * jnp.exp(b_last - bcum[u])).astype(BF16)
            kv.append([_dot_tn(vv[u][c * SCAN_CHUNK:(c + 1) * SCAN_CHUNK], k_dec[c * SCAN_CHUNK:(c + 1) * SCAN_CHUNK])
                       for c in range(n_chunks)])
            decay.append([jnp.exp(bcum[u][c * SCAN_CHUNK + last:c * SCAN_CHUNK + last + 1, :])
                          for c in range(n_chunks)])
        st = [st_ref[hd] for hd in heads]
        for c in order:
            sl = slice(c * SCAN_CHUNK, (c + 1) * SCAN_CHUNK)
            for u, hd in enumerate(heads):
                o_ref[hd, pl.ds(c * SCAN_CHUNK, SCAN_CHUNK), :] = (
                    o_intra[u][sl] + _dot_nt(q_dec[u][sl], st[u].astype(BF16)))
                st[u] = decay[u][c] * st[u] + kv[u][c]
        for u, hd in enumerate(heads):
            st_ref[hd] = st[u]
        return carry

    lax.fori_loop(0, C_HEADS // group, head_group, 0)

    @pl.when(is_ctx)
    def _():
        ns_ref[...] = st_ref[...]


def _gla(z3, lb_dir, s0t_dir, *, rev):
    def ti_of(j):
        return NT - 1 - j if rev else j

    f_slab = 2 if rev else 1

    def lat_map(j):
        return (jnp.clip((ti_of(j) - NT_P) // TILES_PER_LAT, 0, DEC_BATCH - 1), 0, 0, 0)

    return pl.pallas_call(
        functools.partial(_gla_body, rev=rev),
        out_shape=(
            jax.ShapeDtypeStruct((C_HEADS, T, C_DV), F32),
            jax.ShapeDtypeStruct((BATCH, C_HEADS, C_DV, C_DK), F32),
        ),
        grid=(NT,),
        in_specs=[
            pl.BlockSpec((C_HEADS, TM, 128), lambda j: (0, ti_of(j), 0)),
            pl.BlockSpec((C_HEADS, TM, 128), lambda j: (f_slab, ti_of(j), 0)),
            pl.BlockSpec((C_HEADS, TM, 128), lambda j: (3, ti_of(j), 0)),
            _const_spec((C_HEADS, 1, C_DK)),
            pl.BlockSpec((None, C_HEADS, C_DV, C_DK), lat_map),
        ],
        out_specs=(
            pl.BlockSpec((C_HEADS, TM, C_DV), lambda j: (0, ti_of(j), 0)),
            pl.BlockSpec((None, C_HEADS, C_DV, C_DK),
                         lambda j: (jnp.minimum(ti_of(j), NT_P - 1), 0, 0, 0)),
        ),
        scratch_shapes=[pltpu.VMEM((C_HEADS, C_DV, C_DK), F32)],
        compiler_params=_params("arbitrary"),
        name="gla_bwd" if rev else "gla_fwd",
    )(z3, z3, z3, lb_dir, s0t_dir)


def _hgrn_out_body(ofw_ref, obw_ref, gate_ref, x_ref, mod_ref, no_ref, wout_ref, o_ref, cat_ref):
    for hd in range(C_HEADS):
        o = ofw_ref[hd] + obw_ref[hd]
        cat_ref[:, hd * C_DV:(hd + 1) * C_DV] = (_rms(o, no_ref[...]) * _silu(gate_ref[hd])).astype(BF16)
    o_ref[...] = x_ref[...] + mod_ref[2:3, :] * _bdot(cat_ref[...], wout_ref[...])


def _hgrn_out(o_fw, o_bw, z3, x, mod, norm_o, w_out):
    head_spec = pl.BlockSpec((C_HEADS, TM, C_DV), lambda i: (0, i, 0))
    return pl.pallas_call(
        _hgrn_out_body,
        out_shape=jax.ShapeDtypeStruct((T, D_MODEL), F32),
        grid=(NT,),
        in_specs=[
            head_spec, head_spec,
            pl.BlockSpec((C_HEADS, TM, 128), lambda i: (4, i, 0)),
            _tile_spec(), _mod_spec(),
            _const_spec((1, C_DV)),
            _const_spec((D_MODEL, D_MODEL)),
        ],
        out_specs=_tile_spec(),
        scratch_shapes=[pltpu.VMEM((TM, D_MODEL), BF16)],
        compiler_params=_params("parallel"),
        name="hgrn_out_proj",
    )(o_fw, o_bw, z3, x, mod, norm_o.reshape(1, -1), w_out.astype(BF16))


def _final_body(acc_ref, yg_ref, w8_ref, mod_ref, nf_ref, o_ref):
    x = acc_ref[...] + mod_ref[5:6, :] * _combine(yg_ref, w8_ref)
    o_ref[...] = _rms(x, nf_ref[...])


def _final(acc, yg, w8, mod, norm_final, tile0, n_tiles):
    in_tile = pl.BlockSpec((TM, D_MODEL), lambda i: (i + tile0, 0))
    return pl.pallas_call(
        _final_body,
        out_shape=jax.ShapeDtypeStruct((n_tiles * TM, D_MODEL), F32),
        grid=(n_tiles,),
        in_specs=[
            in_tile,
            pl.BlockSpec((TOP_K, TM, D_PACK), lambda i: (0, i + tile0, 0)),
            pl.BlockSpec((TM, TOP_K), lambda i: (i + tile0, 0)),
            pl.BlockSpec((None, 6, D_MODEL), lambda i: (_mod_row(i + tile0), 0, 0)),
            _const_spec((1, D_MODEL)),
        ],
        out_specs=pl.BlockSpec((TM, D_MODEL), lambda i: (i, 0)),
        compiler_params=_params("parallel"),
        name="final_norm",
    )(acc, yg, w8, mod, norm_final.reshape(1, -1))


def kernel(x_prompt, x_sample, state_hgrn, c, c_ctx, w_ada, b_ada, norm_mix, norm_ffn, w_out, w_in_ab, w_sg, b_sg, norm_sg, w_dw, b_dw, norm_cv, w_in_hgrn, lb_raw, norm_o, w_router, b_router, w_gate, w_up, w_down, w_sh_gate, w_sh_up, w_sh_down, norm_final):
    cvecs = jnp.concatenate(
        [c_ctx.reshape(1, D_MODEL), c, jnp.zeros((N_MOD_ROWS - 1 - DEC_BATCH, D_MODEL), F32)], axis=0)
    mods = _ada_tables(cvecs, w_ada, b_ada)
    lb_sm = jax.nn.softmax(lb_raw.astype(F32), axis=0)
    lb1 = (jnp.cumsum(lb_sm, axis=0) - lb_sm[0])[1].reshape(2, C_HEADS, 1, C_DK)

    def moe(l, xin):
        h, eidx_t, w8_t, rank_t, counts, acc = _moe_pre(
            xin, mods[l], norm_ffn[l], w_router[l], b_router[l], w_sh_gate[l], w_sh_up[l], w_sh_down[l])
        return acc, _moe_routed(h, eidx_t, rank_t, counts, w_gate, w_up, w_down, l), w8_t.T

    x = _l0_mixer(x_prompt.reshape(T_P, D_MODEL), x_sample.reshape(T_S, D_MODEL), mods[0], norm_mix[0], w_in_ab[0], w_sg[0], b_sg[0], norm_sg[0], w_dw[0],
                  b_dw[0], norm_cv[0], w_out[0])
    acc, yg, w8 = moe(0, x)
    x, z3 = _hgrn_in(acc, yg, w8, mods[0], mods[1], norm_mix[1], w_in_hgrn[0])
    s0t = jnp.swapaxes(state_hgrn[:, 0].astype(F32), -1, -2)
    o_fw, ns_fw = _gla(z3, lb1[0], s0t[:, 0], rev=False)
    o_bw, ns_bw = _gla(z3, lb1[1], s0t[:, 1], rev=True)
    x = _hgrn_out(o_fw, o_bw, z3, x, mods[1], norm_o[0], w_out[1])
    acc, yg, w8 = moe(1, x)
    y_p = _final(acc, yg, w8, mods[1], norm_final, 0, NT_P).reshape(BATCH, SEQ, D_MODEL)
    y_s = _final(acc, yg, w8, mods[1], norm_final, NT_P, NT_S).reshape(DEC_BATCH, DEC_SEQ, D_MODEL)
    new_state = jnp.swapaxes(jnp.stack([ns_fw, ns_bw], axis=1), -1, -2)[:, None]
    return (y_p, y_s, new_state)
```

```python
import functools

import jax
import jax.numpy as jnp
from jax import lax
from jax.experimental import pallas as pl
from jax.experimental.pallas import tpu as pltpu
from jax.experimental.pallas import tpu_sc as plsc

F32 = jnp.float32
BF16 = jnp.bfloat16
HIGHEST = lax.Precision.HIGHEST

D_MODEL = 1024
BATCH = 32
SEQ = 256
DEPTH = 2
DEC_BATCH = 8
DEC_SEQ = 2048
GRID_W = 64
A_WIDTH = D_MODEL // 2
A_GROUPS = 4
A_GC = A_WIDTH // A_GROUPS
CHUNK_A = 128
B_WIDTH = D_MODEL - A_WIDTH
CONV_W = 31
CONV_PAD = CONV_W // 2
C_HEADS = 8
C_DK = 128
C_DV = D_MODEL // C_HEADS
C_FDIM = C_HEADS * C_DK
SCAN_CHUNK = 64
N_EXPERTS = 64
TOP_K = 8
N_GROUPS = 8
GROUP_SIZE = N_EXPERTS // N_GROUPS
TOPK_GROUPS = 4
D_EXPERT = 256
D_SHARED = 256
ROUTED_SCALE = 2.5
EPS = 1e-6

TM = 256
T_P = BATCH * SEQ
T_S = DEC_BATCH * DEC_SEQ
T = T_P + T_S
NT_P = T_P // TM
NT_S = T_S // TM
NT = NT_P + NT_S
TILES_PER_LAT = DEC_SEQ // TM
TE = 512
MOE_PARTS = 2
NT_PART = NT // MOE_PARTS
T_PART = NT_PART * TM
NB = T_PART * TOP_K // TE + N_EXPERTS
P_ROWS = NB * TE
D_PACK = D_MODEL // 2
N_MOD_ROWS = 16
CONV_HALO = 16
VMEM_LIMIT = 48 * 1024 * 1024
SC_CORES = 2
SC_SUBCORES = 16
SC_WORKERS = SC_CORES * SC_SUBCORES
SC_CHUNK = 64


def _mod_row(i):
    return jnp.where(i < NT_P, 0, 1 + (i - NT_P) // TILES_PER_LAT)


def _silu(x):
    return x * jax.nn.sigmoid(x)


def _gelu(x):
    return x * (0.5 * (1.0 + jnp.tanh(0.7978845608028654 * (x + 0.044715 * (x * x * x)))))


def _rms(x, g):
    return x * lax.rsqrt(jnp.mean(x * x, axis=-1, keepdims=True) + EPS) * g


def _layernorm(x, g):
    xc = x - jnp.mean(x, axis=-1, keepdims=True)
    return xc * lax.rsqrt(jnp.mean(xc * xc, axis=-1, keepdims=True) + EPS) * g


def _modulate(x, g, shift, scale):
    return _rms(x, g) * (1.0 + scale) + shift


def _bdot(a, b):
    return jnp.dot(a, b, preferred_element_type=F32)


def _dot_nt(a, b, precision=None):
    return lax.dot_general(a, b, (((1,), (1,)), ((), ())), precision=precision,
                           preferred_element_type=F32)


def _dot_tn(a, b):
    return lax.dot_general(a, b, (((0,), (0,)), ((), ())), preferred_element_type=F32)


def _pack_pairs(x):
    m = x.shape[1] // 2
    lo = lax.bitcast_convert_type(x[:, :m].astype(BF16).astype(F32), jnp.uint32)
    hi = lax.bitcast_convert_type(x[:, m:].astype(BF16).astype(F32), jnp.uint32)
    return lax.bitcast_convert_type(hi | (lo >> 16), jnp.int32)


def _unpack_pairs(w):
    u = lax.bitcast_convert_type(w, jnp.uint32)
    lo = lax.bitcast_convert_type(u << 16, F32)
    hi = lax.bitcast_convert_type(u & jnp.uint32(0xFFFF0000), F32)
    return lo, hi


def _params(*sem):
    return pltpu.CompilerParams(dimension_semantics=sem, vmem_limit_bytes=VMEM_LIMIT)


def _const_spec(shape):
    nd = len(shape)
    return pl.BlockSpec(shape, lambda *_: (0,) * nd)


def _ada_body(c_ref, w_ref, b_ref, o_ref):
    s = _silu(c_ref[...])
    o_ref[...] = jnp.dot(s, w_ref[...], precision=HIGHEST, preferred_element_type=F32) + b_ref[...]


def _ada_tables(cvecs, w_ada, b_ada):
    out = pl.pallas_call(
        _ada_body,
        out_shape=jax.ShapeDtypeStruct((DEPTH, N_MOD_ROWS, 6 * D_MODEL), F32),
        grid=(DEPTH, 6),
        in_specs=[
            _const_spec((N_MOD_ROWS, D_MODEL)),
            pl.BlockSpec((None, D_MODEL, D_MODEL), lambda l, j: (l, 0, j)),
            pl.BlockSpec((None, 1, D_MODEL), lambda l, j: (l, 0, j)),
        ],
        out_specs=pl.BlockSpec((None, N_MOD_ROWS, D_MODEL), lambda l, j: (l, 0, j)),
        compiler_params=_params("parallel", "parallel"),
        name="ada_tables",
    )(cvecs, w_ada, b_ada.reshape(DEPTH, 1, 6 * D_MODEL))
    return out.reshape(DEPTH, N_MOD_ROWS, 6, D_MODEL)


def _mod_spec():
    return pl.BlockSpec((None, 6, D_MODEL), lambda i: (_mod_row(i), 0, 0))


def _tile_spec():
    return pl.BlockSpec((TM, D_MODEL), lambda i: (i, 0))


SUBLANES = 8


def _conv_segment(pad_ref, shift_ref, conv_ref, wdw_ref, pad_base, out_base, seg):
    rb = min(seg, 64)
    for cb in range(B_WIDTH // 128):
        cs = slice(cb * 128, (cb + 1) * 128)
        for r0 in range(0, seg, rb):
            acc = jnp.zeros((rb, 128), F32)
            for k in range(CONV_W):
                b = (CONV_HALO - CONV_PAD + k) % SUBLANES
                off = pad_base + r0 + CONV_HALO - CONV_PAD + k - b
                src = pad_ref if b == 0 else shift_ref.at[b - 1]
                acc = acc + wdw_ref[k:k + 1, cs] * src[off:off + rb, cs]
            conv_ref[out_base + r0:out_base + r0 + rb, cs] = acc


def _l0_body(xc_ref, xl_ref, mod_ref, nmix_ref, win_ref, wsg_ref, bsg_ref, nsg_ref, wdw_ref, bdw_ref,
             ncv_ref, wout_ref, o_ref, cat_ref, pad_ref, shift_ref, conv_ref):
    i = pl.program_id(0)
    x = jnp.where(i < NT_P, xc_ref[...], xl_ref[...])
    h = _modulate(x, nmix_ref[...], mod_ref[0:1, :], mod_ref[1:2, :])
    z = _bdot(h.astype(BF16), win_ref[...])
    u = _gelu(z[:, :A_WIDTH])
    vb = _layernorm(_gelu(z[:, A_WIDTH:2 * A_WIDTH]), nsg_ref[...]).astype(BF16)
    for n in range(TM // CHUNK_A):
        rs = slice(n * CHUNK_A, (n + 1) * CHUNK_A)
        for g in range(A_GROUPS):
            cs = slice(g * A_GC, (g + 1) * A_GC)
            m = _bdot(wsg_ref[g], vb[rs, cs]) + bsg_ref[g]
            cat_ref[rs, cs] = (u[rs, cs] * m).astype(BF16)
    hb = z[:, 2 * A_WIDTH:2 * A_WIDTH + B_WIDTH] * jax.nn.sigmoid(z[:, 2 * A_WIDTH + B_WIDTH:])

    def conv_tile(seg):
        stride = seg + 2 * CONV_HALO
        halo = jnp.zeros((CONV_HALO, B_WIDTH), F32)
        for s in range(TM // seg):
            b = s * stride
            pad_ref[b:b + CONV_HALO, :] = halo
            pad_ref[b + CONV_HALO:b + CONV_HALO + seg, :] = hb[s * seg:(s + 1) * seg, :]
            pad_ref[b + CONV_HALO + seg:b + stride, :] = halo
        rows = (TM // seg) * stride - SUBLANES
        for b in range(1, SUBLANES):
            shift_ref[b - 1, 0:rows, :] = pad_ref[b:b + rows, :]
        for s in range(TM // seg):
            _conv_segment(pad_ref, shift_ref, conv_ref, wdw_ref, s * stride, s * seg, seg)

    @pl.when(i < NT_P)
    def _():
        conv_tile(SEQ)

    @pl.when(i >= NT_P)
    def _():
        conv_tile(GRID_W)

    yb = _layernorm(conv_ref[...] + bdw_ref[...], ncv_ref[...])
    cat_ref[:, A_WIDTH:] = _silu(yb).astype(BF16)
    out = _bdot(cat_ref[...], wout_ref[...])
    o_ref[...] = x + mod_ref[2:3, :] * out


def _l0_mixer(x_ctx, x_lat, mod, norm_mix, w_in, w_sg, b_sg, norm_sg, w_dw, b_dw, norm_cv, w_out):
    assert TM == SEQ and TM % GRID_W == 0 and TM % CHUNK_A == 0
    pad_rows = (TM // GRID_W) * (GRID_W + 2 * CONV_HALO)
    assert pad_rows >= SEQ + 2 * CONV_HALO
    return pl.pallas_call(
        _l0_body,
        out_shape=jax.ShapeDtypeStruct((T, D_MODEL), F32),
        grid=(NT,),
        in_specs=[
            pl.BlockSpec((TM, D_MODEL), lambda i: (jnp.minimum(i, NT_P - 1), 0)),
            pl.BlockSpec((TM, D_MODEL), lambda i: (jnp.maximum(i - NT_P, 0), 0)),
            _mod_spec(),
            _const_spec((1, D_MODEL)),
            _const_spec((D_MODEL, 2 * A_WIDTH + 2 * B_WIDTH)),
            _const_spec((A_GROUPS, CHUNK_A, CHUNK_A)),
            _const_spec((A_GROUPS, CHUNK_A, 1)),
            _const_spec((1, A_WIDTH)),
            _const_spec((CONV_W, B_WIDTH)),
            _const_spec((1, B_WIDTH)),
            _const_spec((1, B_WIDTH)),
            _const_spec((D_MODEL, D_MODEL)),
        ],
        out_specs=_tile_spec(),
        scratch_shapes=[
            pltpu.VMEM((TM, D_MODEL), BF16),
            pltpu.VMEM((pad_rows, B_WIDTH), F32),
            pltpu.VMEM((SUBLANES - 1, pad_rows, B_WIDTH), F32),
            pltpu.VMEM((TM, B_WIDTH), F32),
        ],
        compiler_params=_params("parallel"),
        name="l0_mixer",
    )(x_ctx, x_lat, mod, norm_mix.reshape(1, -1), w_in.astype(BF16), w_sg.astype(BF16),
      b_sg.reshape(A_GROUPS, CHUNK_A, 1), norm_sg.reshape(1, -1), w_dw, b_dw.reshape(1, -1),
      norm_cv.reshape(1, -1), w_out.astype(BF16))


def _route(scores, biased):
    n = scores.shape[-1]
    shp = (N_GROUPS, GROUP_SIZE, n)
    s3 = scores.reshape(shp)
    b3 = biased.reshape(shp)
    m_iota = lax.broadcasted_iota(jnp.int32, shp, 1).astype(F32)
    g_iota = lax.broadcasted_iota(jnp.int32, shp, 0).astype(F32)
    e_iota = g_iota * GROUP_SIZE + m_iota
    neg = -jnp.inf

    def amax1(v):
        return jnp.max(v, axis=1, keepdims=True)

    def amin1(v):
        return jnp.min(v, axis=1, keepdims=True)

    m1 = amax1(b3)
    i1 = amin1(jnp.where(b3 == m1, m_iota, float(GROUP_SIZE)))
    m2 = amax1(jnp.where(m_iota == i1, neg, b3))
    grp = m1 + m2
    gi1 = lax.broadcasted_iota(jnp.int32, grp.shape, 0).astype(F32)
    gmask = jnp.zeros(grp.shape, jnp.bool_)
    for _ in range(TOPK_GROUPS):
        gm = jnp.max(grp, axis=0, keepdims=True)
        gi = jnp.min(jnp.where(grp == gm, gi1, float(N_GROUPS)), axis=0, keepdims=True)
        hit = gi1 == gi
        gmask = jnp.logical_or(gmask, hit)
        grp = jnp.where(hit, neg, grp)
    cand = jnp.where(gmask, b3, neg)
    ids, vals, hits = [], [], []
    for _ in range(TOP_K):
        mx = jnp.max(amax1(cand), axis=0, keepdims=True)
        ei = jnp.min(amin1(jnp.where(cand == mx, e_iota, float(N_EXPERTS))), axis=0, keepdims=True)
        hit = e_iota == ei
        ids.append(ei.reshape(1, n))
        vals.append(_pick(hit, s3))
        hits.append(hit)
        cand = jnp.where(hit, neg, cand)
    return jnp.concatenate(ids, axis=0).astype(jnp.int32), jnp.concatenate(vals, axis=0), hits


def _pick(hit, v3):
    s = jnp.sum(jnp.sum(jnp.where(hit, v3, 0.0), axis=1, keepdims=True), axis=0, keepdims=True)
    return s.reshape(1, v3.shape[-1])


def _moe_pre_body(x_ref, mod_ref, nffn_ref, wrt_ref, br_ref, wsg_ref, wsu_ref, wsd_ref,
                  h_ref, eidx_ref, w8_ref, rank_ref, cnt_ref, acc_ref, run_ref):
    @pl.when(pl.program_id(0) == 0)
    def _():
        run_ref[...] = jnp.zeros(run_ref.shape, F32)

    x = x_ref[...]
    h = _modulate(x, nffn_ref[...], mod_ref[3:4, :], mod_ref[4:5, :])
    h_ref[...] = _pack_pairs(h)
    logits_t = _dot_nt(wrt_ref[...], h, precision=HIGHEST)
    scores = jax.nn.sigmoid(logits_t)
    eidx, sv, hits = _route(scores, scores + br_ref[...])
    eidx_ref[...] = eidx
    w8_ref[...] = sv / jnp.sum(sv, axis=0, keepdims=True) * ROUTED_SCALE
    sel3 = hits[0]
    for hit in hits[1:]:
        sel3 = jnp.logical_or(sel3, hit)
    sel = sel3.astype(F32).reshape(N_EXPERTS, TM)
    earlier = (lax.broadcasted_iota(jnp.int32, (TM, TM), 0)
               < lax.broadcasted_iota(jnp.int32, (TM, TM), 1)).astype(BF16)
    rank3 = (_bdot(sel.astype(BF16), earlier) + run_ref[...]).reshape(N_GROUPS, GROUP_SIZE, TM)
    rank_ref[...] = jnp.concatenate([_pick(hit, rank3) for hit in hits], axis=0).astype(jnp.int32)
    run_ref[...] = run_ref[...] + jnp.sum(sel, axis=1, keepdims=True)
    cnt_ref[...] = run_ref[...].astype(jnp.int32)
    hb = h.astype(BF16)
    sh = _bdot((_silu(_bdot(hb, wsg_ref[...])) * _bdot(hb, wsu_ref[...])).astype(BF16), wsd_ref[...])
    acc_ref[...] = x + mod_ref[5:6, :] * sh


def _moe_pre(x, mod, norm_ffn, w_router, b_router, w_sh_gate, w_sh_up, w_sh_down, tile0):
    return pl.pallas_call(
        _moe_pre_body,
        out_shape=(
            jax.ShapeDtypeStruct((T_PART, D_PACK), jnp.int32),
            jax.ShapeDtypeStruct((TOP_K, T_PART), jnp.int32),
            jax.ShapeDtypeStruct((TOP_K, T_PART), F32),
            jax.ShapeDtypeStruct((TOP_K, T_PART), jnp.int32),
            jax.ShapeDtypeStruct((N_EXPERTS, 1), jnp.int32),
            jax.ShapeDtypeStruct((T_PART, D_MODEL), F32),
        ),
        grid=(NT_PART,),
        in_specs=[
            pl.BlockSpec((TM, D_MODEL), lambda i: (i + tile0, 0)),
            pl.BlockSpec((None, 6, D_MODEL), lambda i: (_mod_row(i + tile0), 0, 0)),
            _const_spec((1, D_MODEL)),
            _const_spec((N_EXPERTS, D_MODEL)),
            _const_spec((N_EXPERTS, 1)),
            _const_spec((D_MODEL, D_SHARED)),
            _const_spec((D_MODEL, D_SHARED)),
            _const_spec((D_SHARED, D_MODEL)),
        ],
        out_specs=(
            pl.BlockSpec((TM, D_PACK), lambda i: (i, 0)),
            pl.BlockSpec((TOP_K, TM), lambda i: (0, i)),
            pl.BlockSpec((TOP_K, TM), lambda i: (0, i)),
            pl.BlockSpec((TOP_K, TM), lambda i: (0, i)),
            _const_spec((N_EXPERTS, 1)),
            _tile_spec(),
        ),
        scratch_shapes=[pltpu.VMEM((N_EXPERTS, 1), F32)],
        compiler_params=_params("arbitrary"),
        name="moe_router_shared",
    )(x, mod, norm_ffn.reshape(1, -1), w_router.T, b_router.reshape(N_EXPERTS, 1),
      w_sh_gate.astype(BF16), w_sh_up.astype(BF16), w_sh_down.astype(BF16))


EXPERT_RING = 3


def _expert_body(be_ref, nv_ref, nu_ref, xs_hbm, wg_ref, wu_ref, wd_ref, ys_ref, wgu_s, wd_s, ring, sems):
    j = pl.program_id(0)
    n_used = nu_ref[0]
    live_tile = j < n_used

    def fetch(t):
        slot = t % EXPERT_RING
        return pltpu.make_async_copy(xs_hbm.at[pl.ds(t * TE, TE)], ring.at[slot], sems.at[slot])

    @pl.when(j == 0)
    def _():
        for t in range(EXPERT_RING - 1):
            @pl.when(t < n_used)
            def _():
                fetch(t).start()

    @pl.when(j + (EXPERT_RING - 1) < n_used)
    def _():
        fetch(j + (EXPERT_RING - 1)).start()

    @pl.when(jnp.logical_and(live_tile, jnp.logical_or(j == 0, be_ref[j] != be_ref[jnp.maximum(j - 1, 0)])))
    def _():
        wgu_s[:, :D_EXPERT] = wg_ref[...].astype(BF16)
        wgu_s[:, D_EXPERT:] = wu_ref[...].astype(BF16)
        wd_s[...] = wd_ref[...].astype(BF16)

    @pl.when(live_tile)
    def _():
        fetch(j).wait()
        live = lax.broadcasted_iota(jnp.int32, (TE, 1), 0) < nv_ref[j]
        lo, hi = _unpack_pairs(jnp.where(live, ring[j % EXPERT_RING], 0))
        xb = jnp.concatenate([lo.astype(BF16), hi.astype(BF16)], axis=1)
        hgu = _bdot(xb, wgu_s[...])
        hh = _silu(hgu[:, :D_EXPERT]) * hgu[:, D_EXPERT:]
        ys_ref[...] = _pack_pairs(_bdot(hh.astype(BF16), wd_s[...]))


def _experts(block_expert, block_rows, n_used, xs, w_gate, w_up, w_down, layer):
    def row_map(j, be, nv, nu):
        return (jnp.minimum(j, nu[0] - 1), 0)

    def w_map(j, be, nv, nu):
        return (layer, be[jnp.minimum(j, nu[0] - 1)], 0, 0)

    return pl.pallas_call(
        _expert_body,
        out_shape=jax.ShapeDtypeStruct((P_ROWS, D_PACK), jnp.int32),
        grid_spec=pltpu.PrefetchScalarGridSpec(
            num_scalar_prefetch=3,
            grid=(NB,),
            in_specs=[
                pl.BlockSpec(memory_space=pl.ANY),
                pl.BlockSpec((None, None, D_MODEL, D_EXPERT), w_map),
                pl.BlockSpec((None, None, D_MODEL, D_EXPERT), w_map),
                pl.BlockSpec((None, None, D_EXPERT, D_MODEL), w_map),
            ],
            out_specs=pl.BlockSpec((TE, D_PACK), row_map),
            scratch_shapes=[
                pltpu.VMEM((D_MODEL, 2 * D_EXPERT), BF16),
                pltpu.VMEM((D_EXPERT, D_MODEL), BF16),
                pltpu.VMEM((EXPERT_RING, TE, D_PACK), jnp.int32),
                pltpu.SemaphoreType.DMA((EXPERT_RING,)),
            ],
        ),
        compiler_params=_params("arbitrary"),
        name="moe_experts",
    )(block_expert, block_rows, n_used, xs, w_gate, w_up, w_down)


def _positions_body(start_ref, eidx_ref, rank_ref, pos_ref):
    eidx = eidx_ref[...]
    base = jnp.zeros(eidx.shape, jnp.int32)
    for e in range(N_EXPERTS):
        base = jnp.where(eidx == e, start_ref[e], base)
    pos_ref[...] = base * TE + rank_ref[...]


def _positions(blk_start, eidx_t, rank_t):
    full = pl.BlockSpec((TOP_K, T_PART), lambda i, s: (0, 0))
    return pl.pallas_call(
        _positions_body,
        out_shape=jax.ShapeDtypeStruct((TOP_K, T_PART), jnp.int32),
        grid_spec=pltpu.PrefetchScalarGridSpec(
            num_scalar_prefetch=1, grid=(1,), in_specs=[full, full], out_specs=full),
        compiler_params=_params("arbitrary"),
        name="moe_positions",
    )(blk_start, eidx_t, rank_t)


def _sc_mesh():
    return plsc.VectorSubcoreMesh(core_axis_name="c", subcore_axis_name="s")


def _sc_worker():
    return lax.axis_index("s") * SC_CORES + lax.axis_index("c")


def _sc_scatter_rows(h, pos_rows):
    c = SC_CHUNK
    n_chunks = T_PART // SC_WORKERS // c
    width = h.shape[1]

    @functools.partial(
        pl.kernel, mesh=_sc_mesh(),
        out_type=jax.ShapeDtypeStruct((P_ROWS, width), h.dtype),
        scratch_types=[pltpu.VMEM((n_chunks * TOP_K, c), jnp.int32),
                       pltpu.VMEM((c, width), h.dtype), pltpu.VMEM((c, width), h.dtype)]
        + [pltpu.SemaphoreType.DMA] * 4,
        name="moe_dispatch_scatter",
    )
    def scatter(h_hbm, pos_hbm, xs_hbm, idx_v, buf0, buf1, sem_in0, sem_in1, sem_out0, sem_out1):
        assert n_chunks % 2 == 0
        bufs, sem_in, sem_out = (buf0, buf1), (sem_in0, sem_in1), (sem_out0, sem_out1)
        first = _sc_worker() * n_chunks
        pltpu.sync_copy(pos_hbm.at[pl.ds(first * TOP_K, n_chunks * TOP_K)], idx_v)

        def load(i, b):
            return pltpu.make_async_copy(h_hbm.at[pl.ds((first + i) * c, c)], bufs[b], sem_in[b])

        def puts(i, b):
            return [pltpu.make_async_copy(bufs[b], xs_hbm.at[idx_v.at[i * TOP_K + k]], sem_out[b])
                    for k in range(TOP_K)]

        load(0, 0).start()
        load(1, 1).start()

        @pl.loop(0, n_chunks, step=2)
        def _(i):
            for b in range(2):
                load(i + b, b).wait()
                for cp in puts(i + b, b):
                    cp.start()
            for b in range(2):
                for cp in puts(i + b, b):
                    cp.wait()

                @pl.when(i + 2 + b < n_chunks)
                def _():
                    load(i + 2 + b, b).start()

    return scatter(h, pos_rows)


def _sc_gather_rows(table, idx):
    c = SC_CHUNK
    n = idx.shape[0]
    per_worker = n // SC_WORKERS
    n_chunks = per_worker // c

    @functools.partial(
        pl.kernel, mesh=_sc_mesh(),
        out_type=jax.ShapeDtypeStruct((n, table.shape[1]), table.dtype),
        scratch_types=[pltpu.VMEM((per_worker,), jnp.int32),
                       pltpu.VMEM((c, table.shape[1]), table.dtype), pltpu.VMEM((c, table.shape[1]), table.dtype)]
        + [pltpu.SemaphoreType.DMA] * 4,
        name="moe_combine_gather",
    )
    def gather(table_hbm, idx_hbm, out_hbm, idx_v, buf0, buf1, sem_in0, sem_in1, sem_out0, sem_out1):
        assert n_chunks % 2 == 0
        bufs, sem_in, sem_out = (buf0, buf1), (sem_in0, sem_in1), (sem_out0, sem_out1)
        base = _sc_worker() * per_worker
        pltpu.sync_copy(idx_hbm.at[pl.ds(base, per_worker)], idx_v)

        def fetch(i, b):
            return pltpu.make_async_copy(table_hbm.at[idx_v.at[pl.ds(i * c, c)]], bufs[b], sem_in[b])

        def flush(i, b):
            return pltpu.make_async_copy(bufs[b], out_hbm.at[pl.ds(base + i * c, c)], sem_out[b])

        fetch(0, 0).start()
        fetch(1, 1).start()

        @pl.loop(0, n_chunks, step=2)
        def _(i):
            for b in range(2):
                fetch(i + b, b).wait()
                flush(i + b, b).start()
            for b in range(2):
                flush(i + b, b).wait()

                @pl.when(i + 2 + b < n_chunks)
                def _():
                    fetch(i + 2 + b, b).start()

    return gather(table, idx)


def _moe_routed(h, eidx_t, rank_t, counts, w_gate, w_up, w_down, layer):
    counts = counts.reshape(1, N_EXPERTS)
    nblk = (counts + TE - 1) // TE
    blk_end = jnp.cumsum(nblk, axis=1)
    blk_start = blk_end - nblk
    blocks = jnp.arange(NB, dtype=jnp.int32).reshape(NB, 1)
    block_expert = jnp.minimum(jnp.sum(blocks >= blk_end, axis=1, keepdims=True), N_EXPERTS - 1)
    mine = block_expert == jnp.arange(N_EXPERTS, dtype=jnp.int32).reshape(1, N_EXPERTS)
    cnt_b = jnp.sum(jnp.where(mine, counts, 0), axis=1, keepdims=True)
    start_b = jnp.sum(jnp.where(mine, blk_start, 0), axis=1, keepdims=True)
    block_rows = jnp.clip(cnt_b - (blocks - start_b) * TE, 0, TE)
    n_used = blk_end[0, -1].reshape(1).astype(jnp.int32)
    pos = _positions(blk_start.reshape(N_EXPERTS).astype(jnp.int32), eidx_t, rank_t)
    pos_rows = pos.reshape(TOP_K, T_PART // SC_CHUNK, SC_CHUNK).transpose(1, 0, 2).reshape(-1, SC_CHUNK)
    xs = _sc_scatter_rows(h, pos_rows)
    ys = _experts(block_expert.reshape(NB).astype(jnp.int32), block_rows.reshape(NB).astype(jnp.int32),
                  n_used, xs, w_gate, w_up, w_down, layer)
    return _sc_gather_rows(ys, pos.reshape(-1)).reshape(TOP_K, T_PART, D_PACK)


N_SLABS = (3 * C_FDIM + 2 * D_MODEL) // 128


def _moe_result(tile, part_refs):
    part = tile // NT_PART

    def pick(vals):
        out = vals[0]
        for p in range(1, MOE_PARTS):
            out = jnp.where(part == p, vals[p], out)
        return out

    acc = pick([r[0][...] for r in part_refs])
    w8 = pick([r[2][...] for r in part_refs])
    r_lo = r_hi = None
    for k in range(TOP_K):
        lo, hi = _unpack_pairs(pick([r[1][k] for r in part_refs]))
        wk = w8[:, k:k + 1]
        r_lo, r_hi = (wk * lo, wk * hi) if k == 0 else (r_lo + wk * lo, r_hi + wk * hi)
    return acc, jnp.concatenate([r_lo, r_hi], axis=1)


def _moe_part_specs(tile0):
    specs = []
    for p in range(MOE_PARTS):
        def local(i, p=p):
            return jnp.clip(i + tile0 - p * NT_PART, 0, NT_PART - 1)
        specs += [
            pl.BlockSpec((TM, D_MODEL), lambda i, f=local: (f(i), 0)),
            pl.BlockSpec((TOP_K, TM, D_PACK), lambda i, f=local: (0, f(i), 0)),
            pl.BlockSpec((TM, TOP_K), lambda i, f=local: (f(i), 0)),
        ]
    return specs


def _split_part_refs(refs):
    n = 3 * MOE_PARTS
    return [refs[3 * p:3 * p + 3] for p in range(MOE_PARTS)], refs[n:]


def _hgrn_in_body(*refs):
    part_refs, (mod0_ref, mod1_ref, nmix_ref, win_ref, x_ref, z_ref) = _split_part_refs(refs)
    acc, routed = _moe_result(pl.program_id(0), part_refs)
    x = acc + mod0_ref[5:6, :] * routed
    x_ref[...] = x
    hb = _modulate(x, nmix_ref[...], mod1_ref[0:1, :], mod1_ref[1:2, :]).astype(BF16)
    for s in range(N_SLABS // C_HEADS):
        zz = _bdot(hb, win_ref[:, s * D_MODEL:(s + 1) * D_MODEL])
        for hh in range(C_HEADS):
            z_ref[s * C_HEADS + hh] = zz[:, hh * 128:(hh + 1) * 128]


def _hgrn_in(moe_parts, mod0, mod1, norm_mix, w_in):
    return pl.pallas_call(
        _hgrn_in_body,
        out_shape=(
            jax.ShapeDtypeStruct((T, D_MODEL), F32),
            jax.ShapeDtypeStruct((N_SLABS, T, 128), F32),
        ),
        grid=(NT,),
        in_specs=_moe_part_specs(0) + [
            _mod_spec(), _mod_spec(),
            _const_spec((1, D_MODEL)),
            _const_spec((D_MODEL, 3 * C_FDIM + 2 * D_MODEL)),
        ],
        out_specs=(
            _tile_spec(),
            pl.BlockSpec((N_SLABS, TM, 128), lambda i: (0, i, 0)),
        ),
        compiler_params=_params("parallel"),
        name="hgrn_in_proj",
    )(*moe_parts, mod0, mod1, norm_mix.reshape(1, -1), w_in.astype(BF16))


def _gla_body(q_ref, f_ref, v_ref, lb_ref, s0_ref, o_ref, ns_ref, st_ref, *, rev):
    j = pl.program_id(0)
    ti = NT - 1 - j if rev else j
    is_ctx = ti < NT_P
    first_lat = (ti - NT_P) % TILES_PER_LAT == (TILES_PER_LAT - 1 if rev else 0)

    @pl.when(is_ctx)
    def _():
        st_ref[...] = jnp.zeros(st_ref.shape, F32)

    @pl.when(jnp.logical_and(jnp.logical_not(is_ctx), first_lat))
    def _():
        st_ref[...] = s0_ref[...]

    row = lax.broadcasted_iota(jnp.int32, (TM, TM), 0)
    col = lax.broadcasted_iota(jnp.int32, (TM, TM), 1)
    same_chunk = (row // SCAN_CHUNK) == (col // SCAN_CHUNK)
    seen = jnp.logical_and(same_chunk, (col >= row) if rev else (col <= row))
    cum_w = seen.astype(BF16)
    mid = SCAN_CHUNK // 2 if rev else SCAN_CHUNK // 2 - 1
    last = 0 if rev else SCAN_CHUNK - 1
    n_chunks = TM // SCAN_CHUNK
    order = range(n_chunks - 1, -1, -1) if rev else range(n_chunks)
    group = 4

    def chunk_rows(b, off):
        return jnp.concatenate(
            [jnp.broadcast_to(b[c * SCAN_CHUNK + off:c * SCAN_CHUNK + off + 1, :], (SCAN_CHUNK, b.shape[1]))
             for c in range(n_chunks)], axis=0)

    def head_group(gi, carry):
        heads = [gi * group + u for u in range(group)]
        qs, kk, vv, bcum = [], [], [], []
        for hd in heads:
            lb = lb_ref[hd]
            qs.append(_silu(q_ref[hd]) * (C_DK ** -0.5))
            fg = lb + (1.0 - lb) * jax.nn.sigmoid(f_ref[hd])
            kk.append(1.0 - fg)
            vv.append(v_ref[hd].astype(BF16))
            g = jnp.log(fg)
            g_hi = g.astype(BF16)
            r1 = g - g_hi.astype(F32)
            g_mid = r1.astype(BF16)
            g_lo = (r1 - g_mid.astype(F32)).astype(BF16)
            bcum.append(_bdot(cum_w, g_hi) + _bdot(cum_w, g_mid) + _bdot(cum_w, g_lo))
        o_intra, q_dec, kv, decay = [], [], [], []
        for u in range(group):
            b_mid = chunk_rows(bcum[u], mid)
            b_last = chunk_rows(bcum[u], last)
            qe = (qs[u] * jnp.exp(bcum[u] - b_mid)).astype(BF16)
            ke = (kk[u] * jnp.exp(b_mid - bcum[u])).astype(BF16)
            att = jnp.where(seen, _dot_nt(qe, ke), 0.0)
            o_intra.append(_bdot(att.astype(BF16), vv[u]))
            q_dec.append((qs[u] * jnp.exp(bcum[u])).astype(BF16))
            k_dec = (kk[u] * jnp.exp(b_last - bcum[u])).astype(BF16)
            kv.append([_dot_tn(vv[u][c * SCAN_CHUNK:(c + 1) * SCAN_CHUNK], k_dec[c * SCAN_CHUNK:(c + 1) * SCAN_CHUNK])
                       for c in range(n_chunks)])
            decay.append([jnp.exp(bcum[u][c * SCAN_CHUNK + last:c * SCAN_CHUNK + last + 1, :])
                          for c in range(n_chunks)])
        st = [st_ref[hd] for hd in heads]
        for c in order:
            sl = slice(c * SCAN_CHUNK, (c + 1) * SCAN_CHUNK)
            for u, hd in enumerate(heads):
                o_ref[hd, pl.ds(c * SCAN_CHUNK, SCAN_CHUNK), :] = (
                    o_intra[u][sl] + _dot_nt(q_dec[u][sl], st[u].astype(BF16)))
                st[u] = decay[u][c] * st[u] + kv[u][c]
        for u, hd in enumerate(heads):
            st_ref[hd] = st[u]
        return carry

    lax.fori_loop(0, C_HEADS // group, head_group, 0)

    @pl.when(is_ctx)
    def _():
        ns_ref[...] = st_ref[...]


def _gla(z3, lb_dir, s0t_dir, *, rev):
    def ti_of(j):
        return NT - 1 - j if rev else j

    f_slab = 2 if rev else 1

    def lat_map(j):
        return (jnp.clip((ti_of(j) - NT_P) // TILES_PER_LAT, 0, DEC_BATCH - 1), 0, 0, 0)

    return pl.pallas_call(
        functools.partial(_gla_body, rev=rev),
        out_shape=(
            jax.ShapeDtypeStruct((C_HEADS, T, C_DV), F32),
            jax.ShapeDtypeStruct((BATCH, C_HEADS, C_DV, C_DK), F32),
        ),
        grid=(NT,),
        in_specs=[
            pl.BlockSpec((C_HEADS, TM, 128), lambda j: (0, ti_of(j), 0)),
            pl.BlockSpec((C_HEADS, TM, 128), lambda j: (f_slab, ti_of(j), 0)),
            pl.BlockSpec((C_HEADS, TM, 128), lambda j: (3, ti_of(j), 0)),
            _const_spec((C_HEADS, 1, C_DK)),
            pl.BlockSpec((None, C_HEADS, C_DV, C_DK), lat_map),
        ],
        out_specs=(
            pl.BlockSpec((C_HEADS, TM, C_DV), lambda j: (0, ti_of(j), 0)),
            pl.BlockSpec((None, C_HEADS, C_DV, C_DK),
                         lambda j: (jnp.minimum(ti_of(j), NT_P - 1), 0, 0, 0)),
        ),
        scratch_shapes=[pltpu.VMEM((C_HEADS, C_DV, C_DK), F32)],
        compiler_params=_params("arbitrary"),
        name="gla_bwd" if rev else "gla_fwd",
    )(z3, z3, z3, lb_dir, s0t_dir)


def _hgrn_out_body(ofw_ref, obw_ref, gate_ref, x_ref, mod_ref, no_ref, wout_ref, o_ref, cat_ref):
    for hd in range(C_HEADS):
        o = ofw_ref[hd] + obw_ref[hd]
        cat_ref[:, hd * C_DV:(hd + 1) * C_DV] = (_rms(o, no_ref[...]) * _silu(gate_ref[hd])).astype(BF16)
    o_ref[...] = x_ref[...] + mod_ref[2:3, :] * _bdot(cat_ref[...], wout_ref[...])


def _hgrn_out(o_fw, o_bw, z3, x, mod, norm_o, w_out):
    head_spec = pl.BlockSpec((C_HEADS, TM, C_DV), lambda i: (0, i, 0))
    return pl.pallas_call(
        _hgrn_out_body,
        out_shape=jax.ShapeDtypeStruct((T, D_MODEL), F32),
        grid=(NT,),
        in_specs=[
            head_spec, head_spec,
            pl.BlockSpec((C_HEADS, TM, 128), lambda i: (4, i, 0)),
            _tile_spec(), _mod_spec(),
            _const_spec((1, C_DV)),
            _const_spec((D_MODEL, D_MODEL)),
        ],
        out_specs=_tile_spec(),
        scratch_shapes=[pltpu.VMEM((TM, D_MODEL), BF16)],
        compiler_params=_params("parallel"),
        name="hgrn_out_proj",
    )(o_fw, o_bw, z3, x, mod, norm_o.reshape(1, -1), w_out.astype(BF16))


def _final_body(*refs, tile0):
    part_refs, (mod_ref, nf_ref, o_ref) = _split_part_refs(refs)
    acc, routed = _moe_result(pl.program_id(0) + tile0, part_refs)
    o_ref[...] = _rms(acc + mod_ref[5:6, :] * routed, nf_ref[...])


def _final(moe_parts, mod, norm_final, tile0, n_tiles):
    return pl.pallas_call(
        functools.partial(_final_body, tile0=tile0),
        out_shape=jax.ShapeDtypeStruct((n_tiles * TM, D_MODEL), F32),
        grid=(n_tiles,),
        in_specs=_moe_part_specs(tile0) + [
            pl.BlockSpec((None, 6, D_MODEL), lambda i: (_mod_row(i + tile0), 0, 0)),
            _const_spec((1, D_MODEL)),
        ],
        out_specs=pl.BlockSpec((TM, D_MODEL), lambda i: (i, 0)),
        compiler_params=_params("parallel"),
        name="final_norm",
    )(*moe_parts, mod, norm_final.reshape(1, -1))


def kernel(x_prompt, x_sample, state_hgrn, c, c_ctx, w_ada, b_ada, norm_mix, norm_ffn, w_out, w_in_ab, w_sg, b_sg, norm_sg, w_dw, b_dw, norm_cv, w_in_hgrn, lb_raw, norm_o, w_router, b_router, w_gate, w_up, w_down, w_sh_gate, w_sh_up, w_sh_down, norm_final):
    cvecs = jnp.concatenate(
        [c_ctx.reshape(1, D_MODEL), c, jnp.zeros((N_MOD_ROWS - 1 - DEC_BATCH, D_MODEL), F32)], axis=0)
    mods = _ada_tables(cvecs, w_ada, b_ada)
    lb_sm = jax.nn.softmax(lb_raw.astype(F32), axis=0)
    lb1 = (jnp.cumsum(lb_sm, axis=0) - lb_sm[0])[1].reshape(2, C_HEADS, 1, C_DK)

    def moe(l, xin):
        parts = []
        for p in range(MOE_PARTS):
            h, eidx_t, w8_t, rank_t, counts, acc = _moe_pre(
                xin, mods[l], norm_ffn[l], w_router[l], b_router[l], w_sh_gate[l], w_sh_up[l], w_sh_down[l],
                p * NT_PART)
            parts += [acc, _moe_routed(h, eidx_t, rank_t, counts, w_gate, w_up, w_down, l), w8_t.T]
        return parts

    x = _l0_mixer(x_prompt.reshape(T_P, D_MODEL), x_sample.reshape(T_S, D_MODEL), mods[0], norm_mix[0], w_in_ab[0], w_sg[0], b_sg[0], norm_sg[0], w_dw[0],
                  b_dw[0], norm_cv[0], w_out[0])
    x, z3 = _hgrn_in(moe(0, x), mods[0], mods[1], norm_mix[1], w_in_hgrn[0])
    s0t = jnp.swapaxes(state_hgrn[:, 0].astype(F32), -1, -2)
    o_fw, ns_fw = _gla(z3, lb1[0], s0t[:, 0], rev=False)
    o_bw, ns_bw = _gla(z3, lb1[1], s0t[:, 1], rev=True)
    x = _hgrn_out(o_fw, o_bw, z3, x, mods[1], norm_o[0], w_out[1])
    parts = moe(1, x)
    y_p = _final(parts, mods[1], norm_final, 0, NT_P).reshape(BATCH, SEQ, D_MODEL)
    y_s = _final(parts, mods[1], norm_final, NT_P, NT_S).reshape(DEC_BATCH, DEC_SEQ, D_MODEL)
    new_state = jnp.swapaxes(jnp.stack([ns_fw, ns_bw], axis=1), -1, -2)[:, None]
    return (y_p, y_s, new_state)
```

```python
import functools

import jax
import jax.numpy as jnp
from jax import lax
from jax.experimental import pallas as pl
from jax.experimental.pallas import tpu as pltpu
from jax.experimental.pallas import tpu_sc as plsc

F32 = jnp.float32
BF16 = jnp.bfloat16
HIGHEST = lax.Precision.HIGHEST

D_MODEL = 1024
BATCH = 32
SEQ = 256
DEPTH = 2
DEC_BATCH = 8
DEC_SEQ = 2048
GRID_W = 64
A_WIDTH = D_MODEL // 2
A_GROUPS = 4
A_GC = A_WIDTH // A_GROUPS
CHUNK_A = 128
B_WIDTH = D_MODEL - A_WIDTH
CONV_W = 31
CONV_PAD = CONV_W // 2
C_HEADS = 8
C_DK = 128
C_DV = D_MODEL // C_HEADS
C_FDIM = C_HEADS * C_DK
SCAN_CHUNK = 64
N_EXPERTS = 64
TOP_K = 8
N_GROUPS = 8
GROUP_SIZE = N_EXPERTS // N_GROUPS
TOPK_GROUPS = 4
D_EXPERT = 256
D_SHARED = 256
ROUTED_SCALE = 2.5
EPS = 1e-6

TM = 256
T_P = BATCH * SEQ
T_S = DEC_BATCH * DEC_SEQ
T = T_P + T_S
NT_P = T_P // TM
NT_S = T_S // TM
NT = NT_P + NT_S
TILES_PER_LAT = DEC_SEQ // TM
TE = 512
MOE_PARTS = 2
NT_PART = NT // MOE_PARTS
T_PART = NT_PART * TM
NB = T_PART * TOP_K // TE + N_EXPERTS
P_ROWS = NB * TE
D_PACK = D_MODEL // 2
N_MOD_ROWS = 16
CONV_HALO = 16
VMEM_LIMIT = 48 * 1024 * 1024
SC_CORES = 2
SC_SUBCORES = 16
SC_WORKERS = SC_CORES * SC_SUBCORES
SC_LANES = 16
SC_CHUNK = 64
COMBINE_TOKENS = SC_CHUNK // TOP_K


def _mod_row(i):
    return jnp.where(i < NT_P, 0, 1 + (i - NT_P) // TILES_PER_LAT)


def _silu(x):
    return x * jax.nn.sigmoid(x)


def _gelu(x):
    return x * (0.5 * (1.0 + jnp.tanh(0.7978845608028654 * (x + 0.044715 * (x * x * x)))))


def _rms(x, g):
    return x * lax.rsqrt(jnp.mean(x * x, axis=-1, keepdims=True) + EPS) * g


def _layernorm(x, g):
    xc = x - jnp.mean(x, axis=-1, keepdims=True)
    return xc * lax.rsqrt(jnp.mean(xc * xc, axis=-1, keepdims=True) + EPS) * g


def _modulate(x, g, shift, scale):
    return _rms(x, g) * (1.0 + scale) + shift


def _bdot(a, b):
    return jnp.dot(a, b, preferred_element_type=F32)


def _dot_nt(a, b, precision=None):
    return lax.dot_general(a, b, (((1,), (1,)), ((), ())), precision=precision,
                           preferred_element_type=F32)


def _dot_tn(a, b):
    return lax.dot_general(a, b, (((0,), (0,)), ((), ())), preferred_element_type=F32)


def _pack_pairs(x):
    m = x.shape[1] // 2
    lo = lax.bitcast_convert_type(x[:, :m].astype(BF16).astype(F32), jnp.uint32)
    hi = lax.bitcast_convert_type(x[:, m:].astype(BF16).astype(F32), jnp.uint32)
    return lax.bitcast_convert_type(hi | (lo >> 16), jnp.int32)


def _unpack_pairs(w):
    u = lax.bitcast_convert_type(w, jnp.uint32)
    lo = lax.bitcast_convert_type(u << 16, F32)
    hi = lax.bitcast_convert_type(u & jnp.uint32(0xFFFF0000), F32)
    return lo, hi


def _params(*sem):
    return pltpu.CompilerParams(dimension_semantics=sem, vmem_limit_bytes=VMEM_LIMIT)


def _const_spec(shape):
    nd = len(shape)
    return pl.BlockSpec(shape, lambda *_: (0,) * nd)


def _ada_body(c_ref, w_ref, b_ref, o_ref):
    s = _silu(c_ref[...])
    o_ref[...] = jnp.dot(s, w_ref[...], precision=HIGHEST, preferred_element_type=F32) + b_ref[...]


def _ada_tables(cvecs, w_ada, b_ada):
    out = pl.pallas_call(
        _ada_body,
        out_shape=jax.ShapeDtypeStruct((DEPTH, N_MOD_ROWS, 6 * D_MODEL), F32),
        grid=(DEPTH, 6),
        in_specs=[
            _const_spec((N_MOD_ROWS, D_MODEL)),
            pl.BlockSpec((None, D_MODEL, D_MODEL), lambda l, j: (l, 0, j)),
            pl.BlockSpec((None, 1, D_MODEL), lambda l, j: (l, 0, j)),
        ],
        out_specs=pl.BlockSpec((None, N_MOD_ROWS, D_MODEL), lambda l, j: (l, 0, j)),
        compiler_params=_params("parallel", "parallel"),
        name="ada_tables",
    )(cvecs, w_ada, b_ada.reshape(DEPTH, 1, 6 * D_MODEL))
    return out.reshape(DEPTH, N_MOD_ROWS, 6, D_MODEL)


def _mod_spec():
    return pl.BlockSpec((None, 6, D_MODEL), lambda i: (_mod_row(i), 0, 0))


def _tile_spec():
    return pl.BlockSpec((TM, D_MODEL), lambda i: (i, 0))


SUBLANES = 8


def _conv_segment(pad_ref, shift_ref, conv_ref, wdw_ref, pad_base, out_base, seg):
    rb = min(seg, 64)
    for cb in range(B_WIDTH // 128):
        cs = slice(cb * 128, (cb + 1) * 128)
        for r0 in range(0, seg, rb):
            acc = jnp.zeros((rb, 128), F32)
            for k in range(CONV_W):
                b = (CONV_HALO - CONV_PAD + k) % SUBLANES
                off = pad_base + r0 + CONV_HALO - CONV_PAD + k - b
                src = pad_ref if b == 0 else shift_ref.at[b - 1]
                acc = acc + wdw_ref[k:k + 1, cs] * src[off:off + rb, cs]
            conv_ref[out_base + r0:out_base + r0 + rb, cs] = acc


def _l0_body(xc_ref, xl_ref, mod_ref, nmix_ref, win_ref, wsg_ref, bsg_ref, nsg_ref, wdw_ref, bdw_ref,
             ncv_ref, wout_ref, o_ref, cat_ref, pad_ref, shift_ref, conv_ref):
    i = pl.program_id(0)
    x = jnp.where(i < NT_P, xc_ref[...], xl_ref[...])
    h = _modulate(x, nmix_ref[...], mod_ref[0:1, :], mod_ref[1:2, :])
    z = _bdot(h.astype(BF16), win_ref[...])
    u = _gelu(z[:, :A_WIDTH])
    vb = _layernorm(_gelu(z[:, A_WIDTH:2 * A_WIDTH]), nsg_ref[...]).astype(BF16)
    for n in range(TM // CHUNK_A):
        rs = slice(n * CHUNK_A, (n + 1) * CHUNK_A)
        for g in range(A_GROUPS):
            cs = slice(g * A_GC, (g + 1) * A_GC)
            m = _bdot(wsg_ref[g], vb[rs, cs]) + bsg_ref[g]
            cat_ref[rs, cs] = (u[rs, cs] * m).astype(BF16)
    hb = z[:, 2 * A_WIDTH:2 * A_WIDTH + B_WIDTH] * jax.nn.sigmoid(z[:, 2 * A_WIDTH + B_WIDTH:])

    def conv_tile(seg):
        stride = seg + 2 * CONV_HALO
        halo = jnp.zeros((CONV_HALO, B_WIDTH), F32)
        for s in range(TM // seg):
            b = s * stride
            pad_ref[b:b + CONV_HALO, :] = halo
            pad_ref[b + CONV_HALO:b + CONV_HALO + seg, :] = hb[s * seg:(s + 1) * seg, :]
            pad_ref[b + CONV_HALO + seg:b + stride, :] = halo
        rows = (TM // seg) * stride - SUBLANES
        for b in range(1, SUBLANES):
            shift_ref[b - 1, 0:rows, :] = pad_ref[b:b + rows, :]
        for s in range(TM // seg):
            _conv_segment(pad_ref, shift_ref, conv_ref, wdw_ref, s * stride, s * seg, seg)

    @pl.when(i < NT_P)
    def _():
        conv_tile(SEQ)

    @pl.when(i >= NT_P)
    def _():
        conv_tile(GRID_W)

    yb = _layernorm(conv_ref[...] + bdw_ref[...], ncv_ref[...])
    cat_ref[:, A_WIDTH:] = _silu(yb).astype(BF16)
    out = _bdot(cat_ref[...], wout_ref[...])
    o_ref[...] = x + mod_ref[2:3, :] * out


def _l0_mixer(x_ctx, x_lat, mod, norm_mix, w_in, w_sg, b_sg, norm_sg, w_dw, b_dw, norm_cv, w_out):
    assert TM == SEQ and TM % GRID_W == 0 and TM % CHUNK_A == 0
    pad_rows = (TM // GRID_W) * (GRID_W + 2 * CONV_HALO)
    assert pad_rows >= SEQ + 2 * CONV_HALO
    return pl.pallas_call(
        _l0_body,
        out_shape=jax.ShapeDtypeStruct((T, D_MODEL), F32),
        grid=(NT,),
        in_specs=[
            pl.BlockSpec((TM, D_MODEL), lambda i: (jnp.minimum(i, NT_P - 1), 0)),
            pl.BlockSpec((TM, D_MODEL), lambda i: (jnp.maximum(i - NT_P, 0), 0)),
            _mod_spec(),
            _const_spec((1, D_MODEL)),
            _const_spec((D_MODEL, 2 * A_WIDTH + 2 * B_WIDTH)),
            _const_spec((A_GROUPS, CHUNK_A, CHUNK_A)),
            _const_spec((A_GROUPS, CHUNK_A, 1)),
            _const_spec((1, A_WIDTH)),
            _const_spec((CONV_W, B_WIDTH)),
            _const_spec((1, B_WIDTH)),
            _const_spec((1, B_WIDTH)),
            _const_spec((D_MODEL, D_MODEL)),
        ],
        out_specs=_tile_spec(),
        scratch_shapes=[
            pltpu.VMEM((TM, D_MODEL), BF16),
            pltpu.VMEM((pad_rows, B_WIDTH), F32),
            pltpu.VMEM((SUBLANES - 1, pad_rows, B_WIDTH), F32),
            pltpu.VMEM((TM, B_WIDTH), F32),
        ],
        compiler_params=_params("parallel"),
        name="l0_mixer",
    )(x_ctx, x_lat, mod, norm_mix.reshape(1, -1), w_in.astype(BF16), w_sg.astype(BF16),
      b_sg.reshape(A_GROUPS, CHUNK_A, 1), norm_sg.reshape(1, -1), w_dw, b_dw.reshape(1, -1),
      norm_cv.reshape(1, -1), w_out.astype(BF16))


def _route(scores, biased):
    n = scores.shape[-1]
    shp = (N_GROUPS, GROUP_SIZE, n)
    s3 = scores.reshape(shp)
    b3 = biased.reshape(shp)
    m_iota = lax.broadcasted_iota(jnp.int32, shp, 1).astype(F32)
    g_iota = lax.broadcasted_iota(jnp.int32, shp, 0).astype(F32)
    e_iota = g_iota * GROUP_SIZE + m_iota
    neg = -jnp.inf

    def amax1(v):
        return jnp.max(v, axis=1, keepdims=True)

    def amin1(v):
        return jnp.min(v, axis=1, keepdims=True)

    m1 = amax1(b3)
    i1 = amin1(jnp.where(b3 == m1, m_iota, float(GROUP_SIZE)))
    m2 = amax1(jnp.where(m_iota == i1, neg, b3))
    grp = m1 + m2
    gi1 = lax.broadcasted_iota(jnp.int32, grp.shape, 0).astype(F32)
    gmask = jnp.zeros(grp.shape, jnp.bool_)
    for _ in range(TOPK_GROUPS):
        gm = jnp.max(grp, axis=0, keepdims=True)
        gi = jnp.min(jnp.where(grp == gm, gi1, float(N_GROUPS)), axis=0, keepdims=True)
        hit = gi1 == gi
        gmask = jnp.logical_or(gmask, hit)
        grp = jnp.where(hit, neg, grp)
    cand = jnp.where(gmask, b3, neg)
    ids, vals, hits = [], [], []
    for _ in range(TOP_K):
        mx = jnp.max(amax1(cand), axis=0, keepdims=True)
        ei = jnp.min(amin1(jnp.where(cand == mx, e_iota, float(N_EXPERTS))), axis=0, keepdims=True)
        hit = e_iota == ei
        ids.append(ei.reshape(1, n))
        vals.append(_pick(hit, s3))
        hits.append(hit)
        cand = jnp.where(hit, neg, cand)
    return jnp.concatenate(ids, axis=0).astype(jnp.int32), jnp.concatenate(vals, axis=0), hits


def _pick(hit, v3):
    s = jnp.sum(jnp.sum(jnp.where(hit, v3, 0.0), axis=1, keepdims=True), axis=0, keepdims=True)
    return s.reshape(1, v3.shape[-1])


def _moe_pre_body(x_ref, mod_ref, nffn_ref, wrt_ref, br_ref, wsg_ref, wsu_ref, wsd_ref,
                  h_ref, eidx_ref, wsp_ref, rank_ref, cnt_ref, acc_ref, run_ref):
    @pl.when(pl.program_id(0) == 0)
    def _():
        run_ref[...] = jnp.zeros(run_ref.shape, F32)

    x = x_ref[...]
    h = _modulate(x, nffn_ref[...], mod_ref[3:4, :], mod_ref[4:5, :])
    h_ref[...] = _pack_pairs(h)
    logits_t = _dot_nt(wrt_ref[...], h, precision=HIGHEST)
    scores = jax.nn.sigmoid(logits_t)
    eidx, sv, hits = _route(scores, scores + br_ref[...])
    eidx_ref[...] = eidx
    w8 = sv / jnp.sum(sv, axis=0, keepdims=True) * ROUTED_SCALE
    wpad = jnp.concatenate([w8, jnp.zeros((N_EXPERTS - TOP_K, TM), F32)], axis=0)
    spread = (lax.broadcasted_iota(jnp.int32, (N_EXPERTS, TOP_K * SC_LANES), 1) // SC_LANES
              == lax.broadcasted_iota(jnp.int32, (N_EXPERTS, TOP_K * SC_LANES), 0)).astype(BF16)
    w_hi = wpad.astype(BF16)
    w_r = wpad - w_hi.astype(F32)
    w_mid = w_r.astype(BF16)
    w_lo = (w_r - w_mid.astype(F32)).astype(BF16)
    wsp_ref[...] = _dot_tn(w_hi, spread) + _dot_tn(w_mid, spread) + _dot_tn(w_lo, spread)
    sel3 = hits[0]
    for hit in hits[1:]:
        sel3 = jnp.logical_or(sel3, hit)
    sel = sel3.astype(F32).reshape(N_EXPERTS, TM)
    earlier = (lax.broadcasted_iota(jnp.int32, (TM, TM), 0)
               < lax.broadcasted_iota(jnp.int32, (TM, TM), 1)).astype(BF16)
    rank3 = (_bdot(sel.astype(BF16), earlier) + run_ref[...]).reshape(N_GROUPS, GROUP_SIZE, TM)
    rank_ref[...] = jnp.concatenate([_pick(hit, rank3) for hit in hits], axis=0).astype(jnp.int32)
    run_ref[...] = run_ref[...] + jnp.sum(sel, axis=1, keepdims=True)
    cnt_ref[...] = run_ref[...].astype(jnp.int32)
    hb = h.astype(BF16)
    sh = _bdot((_silu(_bdot(hb, wsg_ref[...])) * _bdot(hb, wsu_ref[...])).astype(BF16), wsd_ref[...])
    acc_ref[...] = x + mod_ref[5:6, :] * sh


def _moe_pre(x, mod, norm_ffn, w_router, b_router, w_sh_gate, w_sh_up, w_sh_down, tile0):
    return pl.pallas_call(
        _moe_pre_body,
        out_shape=(
            jax.ShapeDtypeStruct((T_PART, D_PACK), jnp.int32),
            jax.ShapeDtypeStruct((TOP_K, T_PART), jnp.int32),
            jax.ShapeDtypeStruct((T_PART, TOP_K * SC_LANES), F32),
            jax.ShapeDtypeStruct((TOP_K, T_PART), jnp.int32),
            jax.ShapeDtypeStruct((N_EXPERTS, 1), jnp.int32),
            jax.ShapeDtypeStruct((T_PART, D_MODEL), F32),
        ),
        grid=(NT_PART,),
        in_specs=[
            pl.BlockSpec((TM, D_MODEL), lambda i: (i + tile0, 0)),
            pl.BlockSpec((None, 6, D_MODEL), lambda i: (_mod_row(i + tile0), 0, 0)),
            _const_spec((1, D_MODEL)),
            _const_spec((N_EXPERTS, D_MODEL)),
            _const_spec((N_EXPERTS, 1)),
            _const_spec((D_MODEL, D_SHARED)),
            _const_spec((D_MODEL, D_SHARED)),
            _const_spec((D_SHARED, D_MODEL)),
        ],
        out_specs=(
            pl.BlockSpec((TM, D_PACK), lambda i: (i, 0)),
            pl.BlockSpec((TOP_K, TM), lambda i: (0, i)),
            pl.BlockSpec((TM, TOP_K * SC_LANES), lambda i: (i, 0)),
            pl.BlockSpec((TOP_K, TM), lambda i: (0, i)),
            _const_spec((N_EXPERTS, 1)),
            _tile_spec(),
        ),
        scratch_shapes=[pltpu.VMEM((N_EXPERTS, 1), F32)],
        compiler_params=_params("arbitrary"),
        name="moe_router_shared",
    )(x, mod, norm_ffn.reshape(1, -1), w_router.T, b_router.reshape(N_EXPERTS, 1),
      w_sh_gate.astype(BF16), w_sh_up.astype(BF16), w_sh_down.astype(BF16))


EXPERT_RING = 3


def _expert_body(be_ref, nv_ref, nu_ref, xs_hbm, wg_ref, wu_ref, wd_ref, ys_ref, wgu_s, wd_s, ring, sems):
    j = pl.program_id(0)
    n_used = nu_ref[0]
    live_tile = j < n_used

    def fetch(t):
        slot = t % EXPERT_RING
        return pltpu.make_async_copy(xs_hbm.at[pl.ds(t * TE, TE)], ring.at[slot], sems.at[slot])

    @pl.when(j == 0)
    def _():
        for t in range(EXPERT_RING - 1):
            @pl.when(t < n_used)
            def _():
                fetch(t).start()

    @pl.when(j + (EXPERT_RING - 1) < n_used)
    def _():
        fetch(j + (EXPERT_RING - 1)).start()

    @pl.when(jnp.logical_and(live_tile, jnp.logical_or(j == 0, be_ref[j] != be_ref[jnp.maximum(j - 1, 0)])))
    def _():
        wgu_s[:, :D_EXPERT] = wg_ref[...].astype(BF16)
        wgu_s[:, D_EXPERT:] = wu_ref[...].astype(BF16)
        wd_s[...] = wd_ref[...].astype(BF16)

    @pl.when(live_tile)
    def _():
        fetch(j).wait()
        live = lax.broadcasted_iota(jnp.int32, (TE, 1), 0) < nv_ref[j]
        lo, hi = _unpack_pairs(jnp.where(live, ring[j % EXPERT_RING], 0))
        xb = jnp.concatenate([lo.astype(BF16), hi.astype(BF16)], axis=1)
        hgu = _bdot(xb, wgu_s[...])
        hh = _silu(hgu[:, :D_EXPERT]) * hgu[:, D_EXPERT:]
        ys_ref[...] = _pack_pairs(_bdot(hh.astype(BF16), wd_s[...]))


def _experts(block_expert, block_rows, n_used, xs, w_gate, w_up, w_down, layer):
    def row_map(j, be, nv, nu):
        return (jnp.minimum(j, nu[0] - 1), 0)

    def w_map(j, be, nv, nu):
        return (layer, be[jnp.minimum(j, nu[0] - 1)], 0, 0)

    return pl.pallas_call(
        _expert_body,
        out_shape=jax.ShapeDtypeStruct((P_ROWS, D_PACK), jnp.int32),
        grid_spec=pltpu.PrefetchScalarGridSpec(
            num_scalar_prefetch=3,
            grid=(NB,),
            in_specs=[
                pl.BlockSpec(memory_space=pl.ANY),
                pl.BlockSpec((None, None, D_MODEL, D_EXPERT), w_map),
                pl.BlockSpec((None, None, D_MODEL, D_EXPERT), w_map),
                pl.BlockSpec((None, None, D_EXPERT, D_MODEL), w_map),
            ],
            out_specs=pl.BlockSpec((TE, D_PACK), row_map),
            scratch_shapes=[
                pltpu.VMEM((D_MODEL, 2 * D_EXPERT), BF16),
                pltpu.VMEM((D_EXPERT, D_MODEL), BF16),
                pltpu.VMEM((EXPERT_RING, TE, D_PACK), jnp.int32),
                pltpu.SemaphoreType.DMA((EXPERT_RING,)),
            ],
        ),
        compiler_params=_params("arbitrary"),
        name="moe_experts",
    )(block_expert, block_rows, n_used, xs, w_gate, w_up, w_down)


def _positions_body(start_ref, eidx_ref, rank_ref, pos_ref):
    eidx = eidx_ref[...]
    base = jnp.zeros(eidx.shape, jnp.int32)
    for e in range(N_EXPERTS):
        base = jnp.where(eidx == e, start_ref[e], base)
    pos_ref[...] = base * TE + rank_ref[...]


def _positions(blk_start, eidx_t, rank_t):
    full = pl.BlockSpec((TOP_K, T_PART), lambda i, s: (0, 0))
    return pl.pallas_call(
        _positions_body,
        out_shape=jax.ShapeDtypeStruct((TOP_K, T_PART), jnp.int32),
        grid_spec=pltpu.PrefetchScalarGridSpec(
            num_scalar_prefetch=1, grid=(1,), in_specs=[full, full], out_specs=full),
        compiler_params=_params("arbitrary"),
        name="moe_positions",
    )(blk_start, eidx_t, rank_t)


def _sc_mesh():
    return plsc.VectorSubcoreMesh(core_axis_name="c", subcore_axis_name="s")


def _sc_worker():
    return lax.axis_index("s") * SC_CORES + lax.axis_index("c")


def _sc_scatter_rows(h, pos_rows):
    c = SC_CHUNK
    n_chunks = T_PART // SC_WORKERS // c
    width = h.shape[1]

    @functools.partial(
        pl.kernel, mesh=_sc_mesh(),
        out_type=jax.ShapeDtypeStruct((P_ROWS, width), h.dtype),
        scratch_types=[pltpu.VMEM((n_chunks * TOP_K, c), jnp.int32),
                       pltpu.VMEM((c, width), h.dtype), pltpu.VMEM((c, width), h.dtype)]
        + [pltpu.SemaphoreType.DMA] * 4,
        name="moe_dispatch_scatter",
    )
    def scatter(h_hbm, pos_hbm, xs_hbm, idx_v, buf0, buf1, sem_in0, sem_in1, sem_out0, sem_out1):
        assert n_chunks % 2 == 0
        bufs, sem_in, sem_out = (buf0, buf1), (sem_in0, sem_in1), (sem_out0, sem_out1)
        first = _sc_worker() * n_chunks
        pltpu.sync_copy(pos_hbm.at[pl.ds(first * TOP_K, n_chunks * TOP_K)], idx_v)

        def load(i, b):
            return pltpu.make_async_copy(h_hbm.at[pl.ds((first + i) * c, c)], bufs[b], sem_in[b])

        def puts(i, b):
            return [pltpu.make_async_copy(bufs[b], xs_hbm.at[idx_v.at[i * TOP_K + k]], sem_out[b])
                    for k in range(TOP_K)]

        load(0, 0).start()
        load(1, 1).start()

        @pl.loop(0, n_chunks, step=2)
        def _(i):
            for b in range(2):
                load(i + b, b).wait()
                for cp in puts(i + b, b):
                    cp.start()
            for b in range(2):
                for cp in puts(i + b, b):
                    cp.wait()

                @pl.when(i + 2 + b < n_chunks)
                def _():
                    load(i + 2 + b, b).start()

    return scatter(h, pos_rows)


def _sc_combine(ys, pos_tk, wsplat):
    ct = COMBINE_TOKENS
    rows = ct * TOP_K
    tok_per_worker = T_PART // SC_WORKERS
    n_chunks = tok_per_worker // ct
    width = ys.shape[1]
    vmem = pltpu.VMEM

    @functools.partial(
        pl.kernel, mesh=_sc_mesh(),
        out_type=jax.ShapeDtypeStruct((T_PART, D_MODEL), F32),
        scratch_types=[vmem((tok_per_worker * TOP_K,), jnp.int32),
                       vmem((rows, width), ys.dtype), vmem((rows, width), ys.dtype),
                       vmem((ct, TOP_K * SC_LANES), F32), vmem((ct, TOP_K * SC_LANES), F32),
                       vmem((ct, D_MODEL), F32), vmem((ct, D_MODEL), F32)]
        + [pltpu.SemaphoreType.DMA] * 6,
        compiler_params=pltpu.CompilerParams(needs_layout_passes=False),
        name="moe_combine",
    )
    def combine(ys_hbm, pos_hbm, w_hbm, out_hbm, idx_v, g0, g1, w0, w1, o0, o1,
                sem_g0, sem_g1, sem_w0, sem_w1, sem_o0, sem_o1):
        assert n_chunks % 2 == 0
        gbuf, wbuf, obuf = (g0, g1), (w0, w1), (o0, o1)
        sem_g, sem_w, sem_o = (sem_g0, sem_g1), (sem_w0, sem_w1), (sem_o0, sem_o1)
        tok0 = _sc_worker() * tok_per_worker
        pltpu.sync_copy(pos_hbm.at[pl.ds(tok0 * TOP_K, tok_per_worker * TOP_K)], idx_v)

        def fetch(i, b):
            return [pltpu.make_async_copy(ys_hbm.at[idx_v.at[pl.ds(i * rows, rows)]], gbuf[b], sem_g[b]),
                    pltpu.make_async_copy(w_hbm.at[pl.ds(tok0 + i * ct, ct)], wbuf[b], sem_w[b])]

        def flush(i, b):
            return pltpu.make_async_copy(obuf[b], out_hbm.at[pl.ds(tok0 + i * ct, ct)], sem_o[b])

        def reduce_chunk(b):
            @pl.loop(0, ct)
            def _(t):
                wv = [wbuf[b][t, pl.ds(k * SC_LANES, SC_LANES)] for k in range(TOP_K)]

                @pl.loop(0, width // SC_LANES)
                def _(j):
                    col = j * SC_LANES
                    acc_lo = acc_hi = None
                    for k in range(TOP_K):
                        word = gbuf[b][t * TOP_K + k, pl.ds(col, SC_LANES)]
                        lo = lax.bitcast_convert_type(word << 16, F32)
                        hi = lax.bitcast_convert_type(word & jnp.int32(-65536), F32)
                        acc_lo = wv[k] * lo if k == 0 else acc_lo + wv[k] * lo
                        acc_hi = wv[k] * hi if k == 0 else acc_hi + wv[k] * hi
                    obuf[b][t, pl.ds(col, SC_LANES)] = acc_lo
                    obuf[b][t, pl.ds(D_PACK + col, SC_LANES)] = acc_hi

        for b in range(2):
            for cp in fetch(b, b):
                cp.start()

        @pl.loop(0, n_chunks, step=2)
        def _(i):
            for b in range(2):
                for cp in fetch(i + b, b):
                    cp.wait()

                @pl.when(i > 0)
                def _():
                    flush(i + b - 2, b).wait()

                reduce_chunk(b)
                flush(i + b, b).start()

                @pl.when(i + 2 + b < n_chunks)
                def _():
                    for cp in fetch(i + 2 + b, b):
                        cp.start()

        for b in range(2):
            flush(n_chunks - 2 + b, b).wait()

    return combine(ys, pos_tk, wsplat)


def _moe_routed(h, eidx_t, rank_t, counts, wsplat, w_gate, w_up, w_down, layer):
    counts = counts.reshape(1, N_EXPERTS)
    nblk = (counts + TE - 1) // TE
    blk_end = jnp.cumsum(nblk, axis=1)
    blk_start = blk_end - nblk
    blocks = jnp.arange(NB, dtype=jnp.int32).reshape(NB, 1)
    block_expert = jnp.minimum(jnp.sum(blocks >= blk_end, axis=1, keepdims=True), N_EXPERTS - 1)
    mine = block_expert == jnp.arange(N_EXPERTS, dtype=jnp.int32).reshape(1, N_EXPERTS)
    cnt_b = jnp.sum(jnp.where(mine, counts, 0), axis=1, keepdims=True)
    start_b = jnp.sum(jnp.where(mine, blk_start, 0), axis=1, keepdims=True)
    block_rows = jnp.clip(cnt_b - (blocks - start_b) * TE, 0, TE)
    n_used = blk_end[0, -1].reshape(1).astype(jnp.int32)
    pos = _positions(blk_start.reshape(N_EXPERTS).astype(jnp.int32), eidx_t, rank_t)
    pos_rows = pos.reshape(TOP_K, T_PART // SC_CHUNK, SC_CHUNK).transpose(1, 0, 2).reshape(-1, SC_CHUNK)
    xs = _sc_scatter_rows(h, pos_rows)
    ys = _experts(block_expert.reshape(NB).astype(jnp.int32), block_rows.reshape(NB).astype(jnp.int32),
                  n_used, xs, w_gate, w_up, w_down, layer)
    return _sc_combine(ys, pos.T.reshape(-1), wsplat)


N_SLABS = (3 * C_FDIM + 2 * D_MODEL) // 128


def _moe_result(tile, part_refs):
    part = tile // NT_PART
    acc, routed = part_refs[0][0][...], part_refs[0][1][...]
    for p in range(1, MOE_PARTS):
        acc = jnp.where(part == p, part_refs[p][0][...], acc)
        routed = jnp.where(part == p, part_refs[p][1][...], routed)
    return acc, routed


def _moe_part_specs(tile0):
    specs = []
    for p in range(MOE_PARTS):
        def local(i, p=p):
            return jnp.clip(i + tile0 - p * NT_PART, 0, NT_PART - 1)
        specs += [pl.BlockSpec((TM, D_MODEL), lambda i, f=local: (f(i), 0))] * 2
    return specs


def _split_part_refs(refs):
    n = 2 * MOE_PARTS
    return [refs[2 * p:2 * p + 2] for p in range(MOE_PARTS)], refs[n:]


def _hgrn_in_body(*refs):
    part_refs, (mod0_ref, mod1_ref, nmix_ref, win_ref, x_ref, z_ref) = _split_part_refs(refs)
    acc, routed = _moe_result(pl.program_id(0), part_refs)
    x = acc + mod0_ref[5:6, :] * routed
    x_ref[...] = x
    hb = _modulate(x, nmix_ref[...], mod1_ref[0:1, :], mod1_ref[1:2, :]).astype(BF16)
    for s in range(N_SLABS // C_HEADS):
        zz = _bdot(hb, win_ref[:, s * D_MODEL:(s + 1) * D_MODEL])
        for hh in range(C_HEADS):
            z_ref[s * C_HEADS + hh] = zz[:, hh * 128:(hh + 1) * 128]


def _hgrn_in(moe_parts, mod0, mod1, norm_mix, w_in):
    return pl.pallas_call(
        _hgrn_in_body,
        out_shape=(
            jax.ShapeDtypeStruct((T, D_MODEL), F32),
            jax.ShapeDtypeStruct((N_SLABS, T, 128), F32),
        ),
        grid=(NT,),
        in_specs=_moe_part_specs(0) + [
            _mod_spec(), _mod_spec(),
            _const_spec((1, D_MODEL)),
            _const_spec((D_MODEL, 3 * C_FDIM + 2 * D_MODEL)),
        ],
        out_specs=(
            _tile_spec(),
            pl.BlockSpec((N_SLABS, TM, 128), lambda i: (0, i, 0)),
        ),
        compiler_params=_params("parallel"),
        name="hgrn_in_proj",
    )(*moe_parts, mod0, mod1, norm_mix.reshape(1, -1), w_in.astype(BF16))


def _gla_body(q_ref, f_ref, v_ref, lb_ref, s0_ref, o_ref, ns_ref, st_ref, *, rev):
    j = pl.program_id(0)
    ti = NT - 1 - j if rev else j
    is_ctx = ti < NT_P
    first_lat = (ti - NT_P) % TILES_PER_LAT == (TILES_PER_LAT - 1 if rev else 0)

    @pl.when(is_ctx)
    def _():
        st_ref[...] = jnp.zeros(st_ref.shape, F32)

    @pl.when(jnp.logical_and(jnp.logical_not(is_ctx), first_lat))
    def _():
        st_ref[...] = s0_ref[...]

    row = lax.broadcasted_iota(jnp.int32, (TM, TM), 0)
    col = lax.broadcasted_iota(jnp.int32, (TM, TM), 1)
    same_chunk = (row // SCAN_CHUNK) == (col // SCAN_CHUNK)
    seen = jnp.logical_and(same_chunk, (col >= row) if rev else (col <= row))
    cum_w = seen.astype(BF16)
    mid = SCAN_CHUNK // 2 if rev else SCAN_CHUNK // 2 - 1
    last = 0 if rev else SCAN_CHUNK - 1
    n_chunks = TM // SCAN_CHUNK
    order = range(n_chunks - 1, -1, -1) if rev else range(n_chunks)
    group = 4

    def chunk_rows(b, off):
        return jnp.concatenate(
            [jnp.broadcast_to(b[c * SCAN_CHUNK + off:c * SCAN_CHUNK + off + 1, :], (SCAN_CHUNK, b.shape[1]))
             for c in range(n_chunks)], axis=0)

    def head_group(gi, carry):
        heads = [gi * group + u for u in range(group)]
        qs, kk, vv, bcum = [], [], [], []
        for hd in heads:
            lb = lb_ref[hd]
            qs.append(_silu(q_ref[hd]) * (C_DK ** -0.5))
            fg = lb + (1.0 - lb) * jax.nn.sigmoid(f_ref[hd])
            kk.append(1.0 - fg)
            vv.append(v_ref[hd].astype(BF16))
            g = jnp.log(fg)
            g_hi = g.astype(BF16)
            r1 = g - g_hi.astype(F32)
            g_mid = r1.astype(BF16)
            g_lo = (r1 - g_mid.astype(F32)).astype(BF16)
            bcum.append(_bdot(cum_w, g_hi) + _bdot(cum_w, g_mid) + _bdot(cum_w, g_lo))
        o_intra, q_dec, kv, decay = [], [], [], []
        for u in range(group):
            b_mid = chunk_rows(bcum[u], mid)
            b_last = chunk_rows(bcum[u], last)
            qe = (qs[u] * jnp.exp(bcum[u] - b_mid)).astype(BF16)
            ke = (kk[u] * jnp.exp(b_mid - bcum[u])).astype(BF16)
            att = jnp.where(seen, _dot_nt(qe, ke), 0.0)
            o_intra.append(_bdot(att.astype(BF16), vv[u]))
            q_dec.append((qs[u] * jnp.exp(bcum[u])).astype(BF16))
            k_dec = (kk[u] * jnp.exp(b_last - bcum[u])).astype(BF16)
            kv.append([_dot_tn(vv[u][c * SCAN_CHUNK:(c + 1) * SCAN_CHUNK], k_dec[c * SCAN_CHUNK:(c + 1) * SCAN_CHUNK])
                       for c in range(n_chunks)])
            decay.append([jnp.exp(bcum[u][c * SCAN_CHUNK + last:c * SCAN_CHUNK + last + 1, :])
                          for c in range(n_chunks)])
        st = [st_ref[hd] for hd in heads]
        for c in order:
            sl = slice(c * SCAN_CHUNK, (c + 1) * SCAN_CHUNK)
            for u, hd in enumerate(heads):
                o_ref[hd, pl.ds(c * SCAN_CHUNK, SCAN_CHUNK), :] = (
                    o_intra[u][sl] + _dot_nt(q_dec[u][sl], st[u].astype(BF16)))
                st[u] = decay[u][c] * st[u] + kv[u][c]
        for u, hd in enumerate(heads):
            st_ref[hd] = st[u]
        return carry

    lax.fori_loop(0, C_HEADS // group, head_group, 0)

    @pl.when(is_ctx)
    def _():
        ns_ref[...] = st_ref[...]


def _gla(z3, lb_dir, s0t_dir, *, rev):
    def ti_of(j):
        return NT - 1 - j if rev else j

    f_slab = 2 if rev else 1

    def lat_map(j):
        return (jnp.clip((ti_of(j) - NT_P) // TILES_PER_LAT, 0, DEC_BATCH - 1), 0, 0, 0)

    return pl.pallas_call(
        functools.partial(_gla_body, rev=rev),
        out_shape=(
            jax.ShapeDtypeStruct((C_HEADS, T, C_DV), F32),
            jax.ShapeDtypeStruct((BATCH, C_HEADS, C_DV, C_DK), F32),
        ),
        grid=(NT,),
        in_specs=[
            pl.BlockSpec((C_HEADS, TM, 128), lambda j: (0, ti_of(j), 0)),
            pl.BlockSpec((C_HEADS, TM, 128), lambda j: (f_slab, ti_of(j), 0)),
            pl.BlockSpec((C_HEADS, TM, 128), lambda j: (3, ti_of(j), 0)),
            _const_spec((C_HEADS, 1, C_DK)),
            pl.BlockSpec((None, C_HEADS, C_DV, C_DK), lat_map),
        ],
        out_specs=(
            pl.BlockSpec((C_HEADS, TM, C_DV), lambda j: (0, ti_of(j), 0)),
            pl.BlockSpec((None, C_HEADS, C_DV, C_DK),
                         lambda j: (jnp.minimum(ti_of(j), NT_P - 1), 0, 0, 0)),
        ),
        scratch_shapes=[pltpu.VMEM((C_HEADS, C_DV, C_DK), F32)],
        compiler_params=_params("arbitrary"),
        name="gla_bwd" if rev else "gla_fwd",
    )(z3, z3, z3, lb_dir, s0t_dir)


def _hgrn_out_body(ofw_ref, obw_ref, gate_ref, x_ref, mod_ref, no_ref, wout_ref, o_ref, cat_ref):
    for hd in range(C_HEADS):
        o = ofw_ref[hd] + obw_ref[hd]
        cat_ref[:, hd * C_DV:(hd + 1) * C_DV] = (_rms(o, no_ref[...]) * _silu(gate_ref[hd])).astype(BF16)
    o_ref[...] = x_ref[...] + mod_ref[2:3, :] * _bdot(cat_ref[...], wout_ref[...])


def _hgrn_out(o_fw, o_bw, z3, x, mod, norm_o, w_out):
    head_spec = pl.BlockSpec((C_HEADS, TM, C_DV), lambda i: (0, i, 0))
    return pl.pallas_call(
        _hgrn_out_body,
        out_shape=jax.ShapeDtypeStruct((T, D_MODEL), F32),
        grid=(NT,),
        in_specs=[
            head_spec, head_spec,
            pl.BlockSpec((C_HEADS, TM, 128), lambda i: (4, i, 0)),
            _tile_spec(), _mod_spec(),
            _const_spec((1, C_DV)),
            _const_spec((D_MODEL, D_MODEL)),
        ],
        out_specs=_tile_spec(),
        scratch_shapes=[pltpu.VMEM((TM, D_MODEL), BF16)],
        compiler_params=_params("parallel"),
        name="hgrn_out_proj",
    )(o_fw, o_bw, z3, x, mod, norm_o.reshape(1, -1), w_out.astype(BF16))


def _final_body(*refs, tile0):
    part_refs, (mod_ref, nf_ref, o_ref) = _split_part_refs(refs)
    acc, routed = _moe_result(pl.program_id(0) + tile0, part_refs)
    o_ref[...] = _rms(acc + mod_ref[5:6, :] * routed, nf_ref[...])


def _final(moe_parts, mod, norm_final, tile0, n_tiles):
    return pl.pallas_call(
        functools.partial(_final_body, tile0=tile0),
        out_shape=jax.ShapeDtypeStruct((n_tiles * TM, D_MODEL), F32),
        grid=(n_tiles,),
        in_specs=_moe_part_specs(tile0) + [
            pl.BlockSpec((None, 6, D_MODEL), lambda i: (_mod_row(i + tile0), 0, 0)),
            _const_spec((1, D_MODEL)),
        ],
        out_specs=pl.BlockSpec((TM, D_MODEL), lambda i: (i, 0)),
        compiler_params=_params("parallel"),
        name="final_norm",
    )(*moe_parts, mod, norm_final.reshape(1, -1))


def kernel(x_prompt, x_sample, state_hgrn, c, c_ctx, w_ada, b_ada, norm_mix, norm_ffn, w_out, w_in_ab, w_sg, b_sg, norm_sg, w_dw, b_dw, norm_cv, w_in_hgrn, lb_raw, norm_o, w_router, b_router, w_gate, w_up, w_down, w_sh_gate, w_sh_up, w_sh_down, norm_final):
    cvecs = jnp.concatenate(
        [c_ctx.reshape(1, D_MODEL), c, jnp.zeros((N_MOD_ROWS - 1 - DEC_BATCH, D_MODEL), F32)], axis=0)
    mods = _ada_tables(cvecs, w_ada, b_ada)
    lb_sm = jax.nn.softmax(lb_raw.astype(F32), axis=0)
    lb1 = (jnp.cumsum(lb_sm, axis=0) - lb_sm[0])[1].reshape(2, C_HEADS, 1, C_DK)

    def moe(l, xin):
        parts = []
        for p in range(MOE_PARTS):
            h, eidx_t, wsplat, rank_t, counts, acc = _moe_pre(
                xin, mods[l], norm_ffn[l], w_router[l], b_router[l], w_sh_gate[l], w_sh_up[l], w_sh_down[l],
                p * NT_PART)
            parts += [acc, _moe_routed(h, eidx_t, rank_t, counts, wsplat, w_gate, w_up, w_down, l)]
        return parts

    x = _l0_mixer(x_prompt.reshape(T_P, D_MODEL), x_sample.reshape(T_S, D_MODEL), mods[0], norm_mix[0], w_in_ab[0], w_sg[0], b_sg[0], norm_sg[0], w_dw[0],
                  b_dw[0], norm_cv[0], w_out[0])
    x, z3 = _hgrn_in(moe(0, x), mods[0], mods[1], norm_mix[1], w_in_hgrn[0])
    s0t = jnp.swapaxes(state_hgrn[:, 0].astype(F32), -1, -2)
    o_fw, ns_fw = _gla(z3, lb1[0], s0t[:, 0], rev=False)
    o_bw, ns_bw = _gla(z3, lb1[1], s0t[:, 1], rev=True)
    x = _hgrn_out(o_fw, o_bw, z3, x, mods[1], norm_o[0], w_out[1])
    parts = moe(1, x)
    y_p = _final(parts, mods[1], norm_final, 0, NT_P).reshape(BATCH, SEQ, D_MODEL)
    y_s = _final(parts, mods[1], norm_final, NT_P, NT_S).reshape(DEC_BATCH, DEC_SEQ, D_MODEL)
    new_state = jnp.swapaxes(jnp.stack([ns_fw, ns_bw], axis=1), -1, -2)[:, None]
    return (y_p, y_s, new_state)
```

```python
import functools

import jax
import jax.numpy as jnp
from jax import lax
from jax.experimental import pallas as pl
from jax.experimental.pallas import tpu as pltpu
from jax.experimental.pallas import tpu_sc as plsc

F32 = jnp.float32
BF16 = jnp.bfloat16
HIGHEST = lax.Precision.HIGHEST

D_MODEL = 1024
BATCH = 32
SEQ = 256
DEPTH = 2
DEC_BATCH = 8
DEC_SEQ = 2048
GRID_W = 64
A_WIDTH = D_MODEL // 2
A_GROUPS = 4
A_GC = A_WIDTH // A_GROUPS
CHUNK_A = 128
B_WIDTH = D_MODEL - A_WIDTH
CONV_W = 31
CONV_PAD = CONV_W // 2
C_HEADS = 8
C_DK = 128
C_DV = D_MODEL // C_HEADS
C_FDIM = C_HEADS * C_DK
SCAN_CHUNK = 64
N_EXPERTS = 64
TOP_K = 8
N_GROUPS = 8
GROUP_SIZE = N_EXPERTS // N_GROUPS
TOPK_GROUPS = 4
D_EXPERT = 256
D_SHARED = 256
ROUTED_SCALE = 2.5
EPS = 1e-6

TM = 256
T_P = BATCH * SEQ
T_S = DEC_BATCH * DEC_SEQ
T = T_P + T_S
NT_P = T_P // TM
NT_S = T_S // TM
NT = NT_P + NT_S
TILES_PER_LAT = DEC_SEQ // TM
TE = 512
MOE_PARTS = 2
NT_PART = NT // MOE_PARTS
T_PART = NT_PART * TM
NB = T_PART * TOP_K // TE + N_EXPERTS
P_ROWS = NB * TE
D_PACK = D_MODEL // 2
N_MOD_ROWS = 16
CONV_HALO = 16
VMEM_LIMIT = 48 * 1024 * 1024
SC_CORES = 2
SC_SUBCORES = 16
SC_WORKERS = SC_CORES * SC_SUBCORES
SC_LANES = 16
SC_CHUNK = 64
COMBINE_TOKENS = SC_CHUNK // TOP_K


def _mod_row(i):
    return jnp.where(i < NT_P, 0, 1 + (i - NT_P) // TILES_PER_LAT)


def _silu(x):
    return x * jax.nn.sigmoid(x)


def _gelu(x):
    return x * (0.5 * (1.0 + jnp.tanh(0.7978845608028654 * (x + 0.044715 * (x * x * x)))))


def _rms(x, g):
    return x * lax.rsqrt(jnp.mean(x * x, axis=-1, keepdims=True) + EPS) * g


def _layernorm(x, g):
    xc = x - jnp.mean(x, axis=-1, keepdims=True)
    return xc * lax.rsqrt(jnp.mean(xc * xc, axis=-1, keepdims=True) + EPS) * g


def _modulate(x, g, shift, scale):
    return _rms(x, g) * (1.0 + scale) + shift


def _bdot(a, b):
    return jnp.dot(a, b, preferred_element_type=F32)


def _dot_nt(a, b, precision=None):
    return lax.dot_general(a, b, (((1,), (1,)), ((), ())), precision=precision,
                           preferred_element_type=F32)


def _dot_tn(a, b):
    return lax.dot_general(a, b, (((0,), (0,)), ((), ())), preferred_element_type=F32)


def _pack_pairs(x):
    m = x.shape[1] // 2
    lo = lax.bitcast_convert_type(x[:, :m].astype(BF16).astype(F32), jnp.uint32)
    hi = lax.bitcast_convert_type(x[:, m:].astype(BF16).astype(F32), jnp.uint32)
    return lax.bitcast_convert_type(hi | (lo >> 16), jnp.int32)


def _unpack_pairs(w):
    u = lax.bitcast_convert_type(w, jnp.uint32)
    lo = lax.bitcast_convert_type(u << 16, F32)
    hi = lax.bitcast_convert_type(u & jnp.uint32(0xFFFF0000), F32)
    return lo, hi


def _params(*sem):
    return pltpu.CompilerParams(dimension_semantics=sem, vmem_limit_bytes=VMEM_LIMIT)


def _const_spec(shape):
    nd = len(shape)
    return pl.BlockSpec(shape, lambda *_: (0,) * nd)


def _ada_body(c_ref, w_ref, b_ref, o_ref):
    s = _silu(c_ref[...])
    o_ref[...] = jnp.dot(s, w_ref[...], precision=HIGHEST, preferred_element_type=F32) + b_ref[...]


def _ada_tables(cvecs, w_ada, b_ada):
    out = pl.pallas_call(
        _ada_body,
        out_shape=jax.ShapeDtypeStruct((DEPTH, N_MOD_ROWS, 6 * D_MODEL), F32),
        grid=(DEPTH, 6),
        in_specs=[
            _const_spec((N_MOD_ROWS, D_MODEL)),
            pl.BlockSpec((None, D_MODEL, D_MODEL), lambda l, j: (l, 0, j)),
            pl.BlockSpec((None, 1, D_MODEL), lambda l, j: (l, 0, j)),
        ],
        out_specs=pl.BlockSpec((None, N_MOD_ROWS, D_MODEL), lambda l, j: (l, 0, j)),
        compiler_params=_params("parallel", "parallel"),
        name="ada_tables",
    )(cvecs, w_ada, b_ada.reshape(DEPTH, 1, 6 * D_MODEL))
    return out.reshape(DEPTH, N_MOD_ROWS, 6, D_MODEL)


def _mod_spec():
    return pl.BlockSpec((None, 6, D_MODEL), lambda i: (_mod_row(i), 0, 0))


def _tile_spec():
    return pl.BlockSpec((TM, D_MODEL), lambda i: (i, 0))


SUBLANES = 8


def _conv_segment(pad_ref, shift_ref, conv_ref, wdw_ref, pad_base, out_base, seg):
    rb = min(seg, 64)
    for cb in range(B_WIDTH // 128):
        cs = slice(cb * 128, (cb + 1) * 128)
        for r0 in range(0, seg, rb):
            acc = jnp.zeros((rb, 128), F32)
            for k in range(CONV_W):
                b = (CONV_HALO - CONV_PAD + k) % SUBLANES
                off = pad_base + r0 + CONV_HALO - CONV_PAD + k - b
                src = pad_ref if b == 0 else shift_ref.at[b - 1]
                acc = acc + wdw_ref[k:k + 1, cs] * src[off:off + rb, cs]
            conv_ref[out_base + r0:out_base + r0 + rb, cs] = acc


def _l0_body(xc_ref, xl_ref, mod_ref, nmix_ref, win_ref, wsg_ref, bsg_ref, nsg_ref, wdw_ref, bdw_ref,
             ncv_ref, wout_ref, o_ref, cat_ref, pad_ref, shift_ref, conv_ref):
    i = pl.program_id(0)
    x = jnp.where(i < NT_P, xc_ref[...], xl_ref[...])
    h = _modulate(x, nmix_ref[...], mod_ref[0:1, :], mod_ref[1:2, :])
    z = _bdot(h.astype(BF16), win_ref[...])
    u = _gelu(z[:, :A_WIDTH])
    vb = _layernorm(_gelu(z[:, A_WIDTH:2 * A_WIDTH]), nsg_ref[...]).astype(BF16)
    for n in range(TM // CHUNK_A):
        rs = slice(n * CHUNK_A, (n + 1) * CHUNK_A)
        for g in range(A_GROUPS):
            cs = slice(g * A_GC, (g + 1) * A_GC)
            m = _bdot(wsg_ref[g], vb[rs, cs]) + bsg_ref[g]
            cat_ref[rs, cs] = (u[rs, cs] * m).astype(BF16)
    hb = z[:, 2 * A_WIDTH:2 * A_WIDTH + B_WIDTH] * jax.nn.sigmoid(z[:, 2 * A_WIDTH + B_WIDTH:])

    def conv_tile(seg):
        stride = seg + 2 * CONV_HALO
        halo = jnp.zeros((CONV_HALO, B_WIDTH), F32)
        for s in range(TM // seg):
            b = s * stride
            pad_ref[b:b + CONV_HALO, :] = halo
            pad_ref[b + CONV_HALO:b + CONV_HALO + seg, :] = hb[s * seg:(s + 1) * seg, :]
            pad_ref[b + CONV_HALO + seg:b + stride, :] = halo
        rows = (TM // seg) * stride - SUBLANES
        for b in range(1, SUBLANES):
            shift_ref[b - 1, 0:rows, :] = pad_ref[b:b + rows, :]
        for s in range(TM // seg):
            _conv_segment(pad_ref, shift_ref, conv_ref, wdw_ref, s * stride, s * seg, seg)

    @pl.when(i < NT_P)
    def _():
        conv_tile(SEQ)

    @pl.when(i >= NT_P)
    def _():
        conv_tile(GRID_W)

    yb = _layernorm(conv_ref[...] + bdw_ref[...], ncv_ref[...])
    cat_ref[:, A_WIDTH:] = _silu(yb).astype(BF16)
    out = _bdot(cat_ref[...], wout_ref[...])
    o_ref[...] = x + mod_ref[2:3, :] * out


def _l0_mixer(x_ctx, x_lat, mod, norm_mix, w_in, w_sg, b_sg, norm_sg, w_dw, b_dw, norm_cv, w_out):
    assert TM == SEQ and TM % GRID_W == 0 and TM % CHUNK_A == 0
    pad_rows = (TM // GRID_W) * (GRID_W + 2 * CONV_HALO)
    assert pad_rows >= SEQ + 2 * CONV_HALO
    return pl.pallas_call(
        _l0_body,
        out_shape=jax.ShapeDtypeStruct((T, D_MODEL), F32),
        grid=(NT,),
        in_specs=[
            pl.BlockSpec((TM, D_MODEL), lambda i: (jnp.minimum(i, NT_P - 1), 0)),
            pl.BlockSpec((TM, D_MODEL), lambda i: (jnp.maximum(i - NT_P, 0), 0)),
            _mod_spec(),
            _const_spec((1, D_MODEL)),
            _const_spec((D_MODEL, 2 * A_WIDTH + 2 * B_WIDTH)),
            _const_spec((A_GROUPS, CHUNK_A, CHUNK_A)),
            _const_spec((A_GROUPS, CHUNK_A, 1)),
            _const_spec((1, A_WIDTH)),
            _const_spec((CONV_W, B_WIDTH)),
            _const_spec((1, B_WIDTH)),
            _const_spec((1, B_WIDTH)),
            _const_spec((D_MODEL, D_MODEL)),
        ],
        out_specs=_tile_spec(),
        scratch_shapes=[
            pltpu.VMEM((TM, D_MODEL), BF16),
            pltpu.VMEM((pad_rows, B_WIDTH), F32),
            pltpu.VMEM((SUBLANES - 1, pad_rows, B_WIDTH), F32),
            pltpu.VMEM((TM, B_WIDTH), F32),
        ],
        compiler_params=_params("parallel"),
        name="l0_mixer",
    )(x_ctx, x_lat, mod, norm_mix.reshape(1, -1), w_in.astype(BF16), w_sg.astype(BF16),
      b_sg.reshape(A_GROUPS, CHUNK_A, 1), norm_sg.reshape(1, -1), w_dw, b_dw.reshape(1, -1),
      norm_cv.reshape(1, -1), w_out.astype(BF16))


def _route(scores, biased):
    n = scores.shape[-1]
    shp = (N_GROUPS, GROUP_SIZE, n)
    s3 = scores.reshape(shp)
    b3 = biased.reshape(shp)
    m_iota = lax.broadcasted_iota(jnp.int32, shp, 1).astype(F32)
    g_iota = lax.broadcasted_iota(jnp.int32, shp, 0).astype(F32)
    e_iota = g_iota * GROUP_SIZE + m_iota
    neg = -jnp.inf

    def amax1(v):
        return jnp.max(v, axis=1, keepdims=True)

    def amin1(v):
        return jnp.min(v, axis=1, keepdims=True)

    m1 = amax1(b3)
    i1 = amin1(jnp.where(b3 == m1, m_iota, float(GROUP_SIZE)))
    m2 = amax1(jnp.where(m_iota == i1, neg, b3))
    grp = m1 + m2
    gi1 = lax.broadcasted_iota(jnp.int32, grp.shape, 0).astype(F32)
    gmask = jnp.zeros(grp.shape, jnp.bool_)
    for _ in range(TOPK_GROUPS):
        gm = jnp.max(grp, axis=0, keepdims=True)
        gi = jnp.min(jnp.where(grp == gm, gi1, float(N_GROUPS)), axis=0, keepdims=True)
        hit = gi1 == gi
        gmask = jnp.logical_or(gmask, hit)
        grp = jnp.where(hit, neg, grp)
    cand = jnp.where(gmask, b3, neg)
    ids, vals, hits = [], [], []
    for _ in range(TOP_K):
        mx = jnp.max(amax1(cand), axis=0, keepdims=True)
        ei = jnp.min(amin1(jnp.where(cand == mx, e_iota, float(N_EXPERTS))), axis=0, keepdims=True)
        hit = e_iota == ei
        ids.append(ei.reshape(1, n))
        vals.append(_pick(hit, s3))
        hits.append(hit)
        cand = jnp.where(hit, neg, cand)
    return jnp.concatenate(ids, axis=0).astype(jnp.int32), jnp.concatenate(vals, axis=0), hits


def _pick(hit, v3):
    s = jnp.sum(jnp.sum(jnp.where(hit, v3, 0.0), axis=1, keepdims=True), axis=0, keepdims=True)
    return s.reshape(1, v3.shape[-1])


def _moe_pre_body(x_ref, mod_ref, nffn_ref, wrt_ref, br_ref, wsg_ref, wsu_ref, wsd_ref,
                  h_ref, eidx_ref, wsp_ref, rank_ref, cnt_ref, acc_ref, run_ref):
    @pl.when(pl.program_id(0) == 0)
    def _():
        run_ref[...] = jnp.zeros(run_ref.shape, F32)

    x = x_ref[...]
    h = _modulate(x, nffn_ref[...], mod_ref[3:4, :], mod_ref[4:5, :])
    h_ref[...] = _pack_pairs(h)
    logits_t = _dot_nt(wrt_ref[...], h, precision=HIGHEST)
    scores = jax.nn.sigmoid(logits_t)
    eidx, sv, hits = _route(scores, scores + br_ref[...])
    eidx_ref[...] = eidx
    w8 = sv / jnp.sum(sv, axis=0, keepdims=True) * ROUTED_SCALE
    w_hi = w8.astype(BF16).astype(F32)
    w_mid = (w8 - w_hi).astype(BF16).astype(F32)
    w_lo = (w8 - w_hi - w_mid).astype(BF16).astype(F32)
    pieces = jnp.concatenate([w_hi, w_mid, w_lo, jnp.zeros((N_EXPERTS - 3 * TOP_K, TM), F32)], axis=0)
    piece_row = lax.broadcasted_iota(jnp.int32, (N_EXPERTS, TOP_K * SC_LANES), 0)
    lane_slot = lax.broadcasted_iota(jnp.int32, (N_EXPERTS, TOP_K * SC_LANES), 1) // SC_LANES
    spread = jnp.logical_and(piece_row % TOP_K == lane_slot, piece_row < 3 * TOP_K).astype(BF16)
    wsp_ref[...] = _dot_tn(pieces.astype(BF16), spread)
    sel3 = hits[0]
    for hit in hits[1:]:
        sel3 = jnp.logical_or(sel3, hit)
    sel = sel3.astype(F32).reshape(N_EXPERTS, TM)
    earlier = (lax.broadcasted_iota(jnp.int32, (TM, TM), 0)
               < lax.broadcasted_iota(jnp.int32, (TM, TM), 1)).astype(BF16)
    rank3 = (_bdot(sel.astype(BF16), earlier) + run_ref[...]).reshape(N_GROUPS, GROUP_SIZE, TM)
    rank_ref[...] = jnp.concatenate([_pick(hit, rank3) for hit in hits], axis=0).astype(jnp.int32)
    run_ref[...] = run_ref[...] + jnp.sum(sel, axis=1, keepdims=True)
    cnt_ref[...] = run_ref[...].astype(jnp.int32)
    hb = h.astype(BF16)
    sh = _bdot((_silu(_bdot(hb, wsg_ref[...])) * _bdot(hb, wsu_ref[...])).astype(BF16), wsd_ref[...])
    acc_ref[...] = x + mod_ref[5:6, :] * sh


def _moe_pre(x, mod, norm_ffn, w_router, b_router, w_sh_gate, w_sh_up, w_sh_down, tile0):
    return pl.pallas_call(
        _moe_pre_body,
        out_shape=(
            jax.ShapeDtypeStruct((T_PART, D_PACK), jnp.int32),
            jax.ShapeDtypeStruct((TOP_K, T_PART), jnp.int32),
            jax.ShapeDtypeStruct((T_PART, TOP_K * SC_LANES), F32),
            jax.ShapeDtypeStruct((TOP_K, T_PART), jnp.int32),
            jax.ShapeDtypeStruct((N_EXPERTS, 1), jnp.int32),
            jax.ShapeDtypeStruct((T_PART, D_MODEL), F32),
        ),
        grid=(NT_PART,),
        in_specs=[
            pl.BlockSpec((TM, D_MODEL), lambda i: (i + tile0, 0)),
            pl.BlockSpec((None, 6, D_MODEL), lambda i: (_mod_row(i + tile0), 0, 0)),
            _const_spec((1, D_MODEL)),
            _const_spec((N_EXPERTS, D_MODEL)),
            _const_spec((N_EXPERTS, 1)),
            _const_spec((D_MODEL, D_SHARED)),
            _const_spec((D_MODEL, D_SHARED)),
            _const_spec((D_SHARED, D_MODEL)),
        ],
        out_specs=(
            pl.BlockSpec((TM, D_PACK), lambda i: (i, 0)),
            pl.BlockSpec((TOP_K, TM), lambda i: (0, i)),
            pl.BlockSpec((TM, TOP_K * SC_LANES), lambda i: (i, 0)),
            pl.BlockSpec((TOP_K, TM), lambda i: (0, i)),
            _const_spec((N_EXPERTS, 1)),
            _tile_spec(),
        ),
        scratch_shapes=[pltpu.VMEM((N_EXPERTS, 1), F32)],
        compiler_params=_params("arbitrary"),
        name="moe_router_shared",
    )(x, mod, norm_ffn.reshape(1, -1), w_router.T, b_router.reshape(N_EXPERTS, 1),
      w_sh_gate.astype(BF16), w_sh_up.astype(BF16), w_sh_down.astype(BF16))


EXPERT_RING = 3


def _expert_body(be_ref, nv_ref, nu_ref, xs_hbm, wg_ref, wu_ref, wd_ref, ys_ref, wgu_s, wd_s, ring, sems):
    j = pl.program_id(0)
    n_used = nu_ref[0]
    live_tile = j < n_used

    def fetch(t):
        slot = t % EXPERT_RING
        return pltpu.make_async_copy(xs_hbm.at[pl.ds(t * TE, TE)], ring.at[slot], sems.at[slot])

    @pl.when(j == 0)
    def _():
        for t in range(EXPERT_RING - 1):
            @pl.when(t < n_used)
            def _():
                fetch(t).start()

    @pl.when(j + (EXPERT_RING - 1) < n_used)
    def _():
        fetch(j + (EXPERT_RING - 1)).start()

    @pl.when(jnp.logical_and(live_tile, jnp.logical_or(j == 0, be_ref[j] != be_ref[jnp.maximum(j - 1, 0)])))
    def _():
        wgu_s[:, :D_EXPERT] = wg_ref[...].astype(BF16)
        wgu_s[:, D_EXPERT:] = wu_ref[...].astype(BF16)
        wd_s[...] = wd_ref[...].astype(BF16)

    @pl.when(live_tile)
    def _():
        fetch(j).wait()
        live = lax.broadcasted_iota(jnp.int32, (TE, 1), 0) < nv_ref[j]
        lo, hi = _unpack_pairs(jnp.where(live, ring[j % EXPERT_RING], 0))
        xb = jnp.concatenate([lo.astype(BF16), hi.astype(BF16)], axis=1)
        hgu = _bdot(xb, wgu_s[...])
        hh = _silu(hgu[:, :D_EXPERT]) * hgu[:, D_EXPERT:]
        ys_ref[...] = _pack_pairs(_bdot(hh.astype(BF16), wd_s[...]))


def _experts(block_expert, block_rows, n_used, xs, w_gate, w_up, w_down, layer):
    def row_map(j, be, nv, nu):
        return (jnp.minimum(j, nu[0] - 1), 0)

    def w_map(j, be, nv, nu):
        return (layer, be[jnp.minimum(j, nu[0] - 1)], 0, 0)

    return pl.pallas_call(
        _expert_body,
        out_shape=jax.ShapeDtypeStruct((P_ROWS, D_PACK), jnp.int32),
        grid_spec=pltpu.PrefetchScalarGridSpec(
            num_scalar_prefetch=3,
            grid=(NB,),
            in_specs=[
                pl.BlockSpec(memory_space=pl.ANY),
                pl.BlockSpec((None, None, D_MODEL, D_EXPERT), w_map),
                pl.BlockSpec((None, None, D_MODEL, D_EXPERT), w_map),
                pl.BlockSpec((None, None, D_EXPERT, D_MODEL), w_map),
            ],
            out_specs=pl.BlockSpec((TE, D_PACK), row_map),
            scratch_shapes=[
                pltpu.VMEM((D_MODEL, 2 * D_EXPERT), BF16),
                pltpu.VMEM((D_EXPERT, D_MODEL), BF16),
                pltpu.VMEM((EXPERT_RING, TE, D_PACK), jnp.int32),
                pltpu.SemaphoreType.DMA((EXPERT_RING,)),
            ],
        ),
        compiler_params=_params("arbitrary"),
        name="moe_experts",
    )(block_expert, block_rows, n_used, xs, w_gate, w_up, w_down)


def _positions_body(start_ref, eidx_ref, rank_ref, pos_ref):
    eidx = eidx_ref[...]
    base = jnp.zeros(eidx.shape, jnp.int32)
    for e in range(N_EXPERTS):
        base = jnp.where(eidx == e, start_ref[e], base)
    pos_ref[...] = base * TE + rank_ref[...]


def _positions(blk_start, eidx_t, rank_t):
    full = pl.BlockSpec((TOP_K, T_PART), lambda i, s: (0, 0))
    return pl.pallas_call(
        _positions_body,
        out_shape=jax.ShapeDtypeStruct((TOP_K, T_PART), jnp.int32),
        grid_spec=pltpu.PrefetchScalarGridSpec(
            num_scalar_prefetch=1, grid=(1,), in_specs=[full, full], out_specs=full),
        compiler_params=_params("arbitrary"),
        name="moe_positions",
    )(blk_start, eidx_t, rank_t)


def _sc_mesh():
    return plsc.VectorSubcoreMesh(core_axis_name="c", subcore_axis_name="s")


def _sc_worker():
    return lax.axis_index("s") * SC_CORES + lax.axis_index("c")


def _sc_scatter_rows(h, pos_rows):
    c = SC_CHUNK
    n_chunks = T_PART // SC_WORKERS // c
    width = h.shape[1]

    @functools.partial(
        pl.kernel, mesh=_sc_mesh(),
        out_type=jax.ShapeDtypeStruct((P_ROWS, width), h.dtype),
        scratch_types=[pltpu.VMEM((n_chunks * TOP_K, c), jnp.int32),
                       pltpu.VMEM((c, width), h.dtype), pltpu.VMEM((c, width), h.dtype)]
        + [pltpu.SemaphoreType.DMA] * 4,
        name="moe_dispatch_scatter",
    )
    def scatter(h_hbm, pos_hbm, xs_hbm, idx_v, buf0, buf1, sem_in0, sem_in1, sem_out0, sem_out1):
        assert n_chunks % 2 == 0
        bufs, sem_in, sem_out = (buf0, buf1), (sem_in0, sem_in1), (sem_out0, sem_out1)
        first = _sc_worker() * n_chunks
        pltpu.sync_copy(pos_hbm.at[pl.ds(first * TOP_K, n_chunks * TOP_K)], idx_v)

        def load(i, b):
            return pltpu.make_async_copy(h_hbm.at[pl.ds((first + i) * c, c)], bufs[b], sem_in[b])

        def puts(i, b):
            return [pltpu.make_async_copy(bufs[b], xs_hbm.at[idx_v.at[i * TOP_K + k]], sem_out[b])
                    for k in range(TOP_K)]

        load(0, 0).start()
        load(1, 1).start()

        @pl.loop(0, n_chunks, step=2)
        def _(i):
            for b in range(2):
                load(i + b, b).wait()
                for cp in puts(i + b, b):
                    cp.start()
            for b in range(2):
                for cp in puts(i + b, b):
                    cp.wait()

                @pl.when(i + 2 + b < n_chunks)
                def _():
                    load(i + 2 + b, b).start()

    return scatter(h, pos_rows)


def _sc_combine(ys, pos_tk, wsplat):
    ct = COMBINE_TOKENS
    rows = ct * TOP_K
    tok_per_worker = T_PART // SC_WORKERS
    n_chunks = tok_per_worker // ct
    width = ys.shape[1]
    vmem = pltpu.VMEM

    @functools.partial(
        pl.kernel, mesh=_sc_mesh(),
        out_type=jax.ShapeDtypeStruct((T_PART, D_MODEL), F32),
        scratch_types=[vmem((tok_per_worker * TOP_K,), jnp.int32),
                       vmem((rows, width), ys.dtype), vmem((rows, width), ys.dtype),
                       vmem((ct, TOP_K * SC_LANES), F32), vmem((ct, TOP_K * SC_LANES), F32),
                       vmem((ct, D_MODEL), F32), vmem((ct, D_MODEL), F32)]
        + [pltpu.SemaphoreType.DMA] * 6,
        compiler_params=pltpu.CompilerParams(needs_layout_passes=False),
        name="moe_combine",
    )
    def combine(ys_hbm, pos_hbm, w_hbm, out_hbm, idx_v, g0, g1, w0, w1, o0, o1,
                sem_g0, sem_g1, sem_w0, sem_w1, sem_o0, sem_o1):
        assert n_chunks % 2 == 0
        gbuf, wbuf, obuf = (g0, g1), (w0, w1), (o0, o1)
        sem_g, sem_w, sem_o = (sem_g0, sem_g1), (sem_w0, sem_w1), (sem_o0, sem_o1)
        tok0 = _sc_worker() * tok_per_worker
        pltpu.sync_copy(pos_hbm.at[pl.ds(tok0 * TOP_K, tok_per_worker * TOP_K)], idx_v)

        def fetch(i, b):
            return [pltpu.make_async_copy(ys_hbm.at[idx_v.at[pl.ds(i * rows, rows)]], gbuf[b], sem_g[b]),
                    pltpu.make_async_copy(w_hbm.at[pl.ds(tok0 + i * ct, ct)], wbuf[b], sem_w[b])]

        def flush(i, b):
            return pltpu.make_async_copy(obuf[b], out_hbm.at[pl.ds(tok0 + i * ct, ct)], sem_o[b])

        def reduce_chunk(b):
            @pl.loop(0, ct)
            def _(t):
                wv = [wbuf[b][t, pl.ds(k * SC_LANES, SC_LANES)] for k in range(TOP_K)]

                @pl.loop(0, width // SC_LANES)
                def _(j):
                    col = j * SC_LANES
                    acc_lo = acc_hi = None
                    for k in range(TOP_K):
                        word = gbuf[b][t * TOP_K + k, pl.ds(col, SC_LANES)]
                        lo = lax.bitcast_convert_type(word << 16, F32)
                        hi = lax.bitcast_convert_type(word & jnp.int32(-65536), F32)
                        acc_lo = wv[k] * lo if k == 0 else acc_lo + wv[k] * lo
                        acc_hi = wv[k] * hi if k == 0 else acc_hi + wv[k] * hi
                    obuf[b][t, pl.ds(col, SC_LANES)] = acc_lo
                    obuf[b][t, pl.ds(D_PACK + col, SC_LANES)] = acc_hi

        for b in range(2):
            for cp in fetch(b, b):
                cp.start()

        @pl.loop(0, n_chunks, step=2)
        def _(i):
            for b in range(2):
                for cp in fetch(i + b, b):
                    cp.wait()

                @pl.when(i > 0)
                def _():
                    flush(i + b - 2, b).wait()

                reduce_chunk(b)
                flush(i + b, b).start()

                @pl.when(i + 2 + b < n_chunks)
                def _():
                    for cp in fetch(i + 2 + b, b):
                        cp.start()

        for b in range(2):
            flush(n_chunks - 2 + b, b).wait()

    return combine(ys, pos_tk, wsplat)


def _moe_routed(h, eidx_t, rank_t, counts, wsplat, w_gate, w_up, w_down, layer):
    counts = counts.reshape(1, N_EXPERTS)
    nblk = (counts + TE - 1) // TE
    blk_end = jnp.cumsum(nblk, axis=1)
    blk_start = blk_end - nblk
    blocks = jnp.arange(NB, dtype=jnp.int32).reshape(NB, 1)
    block_expert = jnp.minimum(jnp.sum(blocks >= blk_end, axis=1, keepdims=True), N_EXPERTS - 1)
    mine = block_expert == jnp.arange(N_EXPERTS, dtype=jnp.int32).reshape(1, N_EXPERTS)
    cnt_b = jnp.sum(jnp.where(mine, counts, 0), axis=1, keepdims=True)
    start_b = jnp.sum(jnp.where(mine, blk_start, 0), axis=1, keepdims=True)
    block_rows = jnp.clip(cnt_b - (blocks - start_b) * TE, 0, TE)
    n_used = blk_end[0, -1].reshape(1).astype(jnp.int32)
    pos = _positions(blk_start.reshape(N_EXPERTS).astype(jnp.int32), eidx_t, rank_t)
    pos_rows = pos.reshape(TOP_K, T_PART // SC_CHUNK, SC_CHUNK).transpose(1, 0, 2).reshape(-1, SC_CHUNK)
    xs = _sc_scatter_rows(h, pos_rows)
    ys = _experts(block_expert.reshape(NB).astype(jnp.int32), block_rows.reshape(NB).astype(jnp.int32),
                  n_used, xs, w_gate, w_up, w_down, layer)
    return _sc_combine(ys, pos.T.reshape(-1), wsplat)


N_SLABS = (3 * C_FDIM + 2 * D_MODEL) // 128


def _hgrn_in_body(acc_ref, rt_ref, mod0_ref, mod1_ref, nmix_ref, win_ref, *rest):
    x_ref, z_ref = rest[-2:]
    x = acc_ref[...] + mod0_ref[5:6, :] * rt_ref[...]
    x_ref[...] = x
    hb = _modulate(x, nmix_ref[...], mod1_ref[0:1, :], mod1_ref[1:2, :]).astype(BF16)
    for s in range(N_SLABS // C_HEADS):
        zz = _bdot(hb, win_ref[:, s * D_MODEL:(s + 1) * D_MODEL])
        for hh in range(C_HEADS):
            z_ref[s * C_HEADS + hh] = zz[:, hh * 128:(hh + 1) * 128]


def _hgrn_in(acc, routed, tile0, prev, mod0, mod1, norm_mix, w_in):
    def shifted_mod():
        return pl.BlockSpec((None, 6, D_MODEL), lambda i: (_mod_row(i + tile0), 0, 0))

    any_spec = pl.BlockSpec(memory_space=pl.ANY)
    prev = () if prev is None else tuple(prev)
    n_in = 6
    return pl.pallas_call(
        _hgrn_in_body,
        out_shape=(
            jax.ShapeDtypeStruct((T, D_MODEL), F32),
            jax.ShapeDtypeStruct((N_SLABS, T, 128), F32),
        ),
        grid=(NT_PART,),
        in_specs=[
            _tile_spec(), _tile_spec(), shifted_mod(), shifted_mod(),
            _const_spec((1, D_MODEL)),
            _const_spec((D_MODEL, 3 * C_FDIM + 2 * D_MODEL)),
        ] + [any_spec] * len(prev),
        out_specs=(
            pl.BlockSpec((TM, D_MODEL), lambda i: (i + tile0, 0)),
            pl.BlockSpec((N_SLABS, TM, 128), lambda i: (0, i + tile0, 0)),
        ),
        input_output_aliases={n_in + k: k for k in range(len(prev))},
        compiler_params=_params("parallel"),
        name="hgrn_in_proj",
    )(acc, routed, mod0, mod1, norm_mix.reshape(1, -1), w_in.astype(BF16), *prev)


def _gla_body(q_ref, f_ref, v_ref, lb_ref, s0_ref, o_ref, ns_ref, st_ref, *, rev):
    j = pl.program_id(0)
    ti = NT - 1 - j if rev else j
    is_ctx = ti < NT_P
    first_lat = (ti - NT_P) % TILES_PER_LAT == (TILES_PER_LAT - 1 if rev else 0)

    @pl.when(is_ctx)
    def _():
        st_ref[...] = jnp.zeros(st_ref.shape, F32)

    @pl.when(jnp.logical_and(jnp.logical_not(is_ctx), first_lat))
    def _():
        st_ref[...] = s0_ref[...]

    row = lax.broadcasted_iota(jnp.int32, (TM, TM), 0)
    col = lax.broadcasted_iota(jnp.int32, (TM, TM), 1)
    same_chunk = (row // SCAN_CHUNK) == (col // SCAN_CHUNK)
    seen = jnp.logical_and(same_chunk, (col >= row) if rev else (col <= row))
    cum_w = seen.astype(BF16)
    mid = SCAN_CHUNK // 2 if rev else SCAN_CHUNK // 2 - 1
    last = 0 if rev else SCAN_CHUNK - 1
    n_chunks = TM // SCAN_CHUNK
    order = range(n_chunks - 1, -1, -1) if rev else range(n_chunks)
    group = 4

    def chunk_rows(b, off):
        return jnp.concatenate(
            [jnp.broadcast_to(b[c * SCAN_CHUNK + off:c * SCAN_CHUNK + off + 1, :], (SCAN_CHUNK, b.shape[1]))
             for c in range(n_chunks)], axis=0)

    def head_group(gi, carry):
        heads = [gi * group + u for u in range(group)]
        qs, kk, vv, bcum = [], [], [], []
        for hd in heads:
            lb = lb_ref[hd]
            qs.append(_silu(q_ref[hd]) * (C_DK ** -0.5))
            fg = lb + (1.0 - lb) * jax.nn.sigmoid(f_ref[hd])
            kk.append(1.0 - fg)
            vv.append(v_ref[hd].astype(BF16))
            g = jnp.log(fg)
            g_hi = g.astype(BF16)
            r1 = g - g_hi.astype(F32)
            g_mid = r1.astype(BF16)
            g_lo = (r1 - g_mid.astype(F32)).astype(BF16)
            bcum.append(_bdot(cum_w, g_hi) + _bdot(cum_w, g_mid) + _bdot(cum_w, g_lo))
        o_intra, q_dec, kv, decay = [], [], [], []
        for u in range(group):
            b_mid = chunk_rows(bcum[u], mid)
            b_last = chunk_rows(bcum[u], last)
            qe = (qs[u] * jnp.exp(bcum[u] - b_mid)).astype(BF16)
            ke = (kk[u] * jnp.exp(b_mid - bcum[u])).astype(BF16)
            att = jnp.where(seen, _dot_nt(qe, ke), 0.0)
            o_intra.append(_bdot(att.astype(BF16), vv[u]))
            q_dec.append((qs[u] * jnp.exp(bcum[u])).astype(BF16))
            k_dec = (kk[u] * jnp.exp(b_last - bcum[u])).astype(BF16)
            kv.append([_dot_tn(vv[u][c * SCAN_CHUNK:(c + 1) * SCAN_CHUNK], k_dec[c * SCAN_CHUNK:(c + 1) * SCAN_CHUNK])
                       for c in range(n_chunks)])
            decay.append([jnp.exp(bcum[u][c * SCAN_CHUNK + last:c * SCAN_CHUNK + last + 1, :])
                          for c in range(n_chunks)])
        st = [st_ref[hd] for hd in heads]
        for c in order:
            sl = slice(c * SCAN_CHUNK, (c + 1) * SCAN_CHUNK)
            for u, hd in enumerate(heads):
                o_ref[hd, pl.ds(c * SCAN_CHUNK, SCAN_CHUNK), :] = (
                    o_intra[u][sl] + _dot_nt(q_dec[u][sl], st[u].astype(BF16)))
                st[u] = decay[u][c] * st[u] + kv[u][c]
        for u, hd in enumerate(heads):
            st_ref[hd] = st[u]
        return carry

    lax.fori_loop(0, C_HEADS // group, head_group, 0)

    @pl.when(is_ctx)
    def _():
        ns_ref[...] = st_ref[...]


def _gla(z3, lb_dir, s0t_dir, *, rev):
    def ti_of(j):
        return NT - 1 - j if rev else j

    f_slab = 2 if rev else 1

    def lat_map(j):
        return (jnp.clip((ti_of(j) - NT_P) // TILES_PER_LAT, 0, DEC_BATCH - 1), 0, 0, 0)

    return pl.pallas_call(
        functools.partial(_gla_body, rev=rev),
        out_shape=(
            jax.ShapeDtypeStruct((C_HEADS, T, C_DV), F32),
            jax.ShapeDtypeStruct((BATCH, C_HEADS, C_DV, C_DK), F32),
        ),
        grid=(NT,),
        in_specs=[
            pl.BlockSpec((C_HEADS, TM, 128), lambda j: (0, ti_of(j), 0)),
            pl.BlockSpec((C_HEADS, TM, 128), lambda j: (f_slab, ti_of(j), 0)),
            pl.BlockSpec((C_HEADS, TM, 128), lambda j: (3, ti_of(j), 0)),
            _const_spec((C_HEADS, 1, C_DK)),
            pl.BlockSpec((None, C_HEADS, C_DV, C_DK), lat_map),
        ],
        out_specs=(
            pl.BlockSpec((C_HEADS, TM, C_DV), lambda j: (0, ti_of(j), 0)),
            pl.BlockSpec((None, C_HEADS, C_DV, C_DK),
                         lambda j: (jnp.minimum(ti_of(j), NT_P - 1), 0, 0, 0)),
        ),
        scratch_shapes=[pltpu.VMEM((C_HEADS, C_DV, C_DK), F32)],
        compiler_params=_params("arbitrary"),
        name="gla_bwd" if rev else "gla_fwd",
    )(z3, z3, z3, lb_dir, s0t_dir)


def _hgrn_out_body(ofw_ref, obw_ref, gate_ref, x_ref, mod_ref, no_ref, wout_ref, o_ref, cat_ref):
    for hd in range(C_HEADS):
        o = ofw_ref[hd] + obw_ref[hd]
        cat_ref[:, hd * C_DV:(hd + 1) * C_DV] = (_rms(o, no_ref[...]) * _silu(gate_ref[hd])).astype(BF16)
    o_ref[...] = x_ref[...] + mod_ref[2:3, :] * _bdot(cat_ref[...], wout_ref[...])


def _hgrn_out(o_fw, o_bw, z3, x, mod, norm_o, w_out):
    head_spec = pl.BlockSpec((C_HEADS, TM, C_DV), lambda i: (0, i, 0))
    return pl.pallas_call(
        _hgrn_out_body,
        out_shape=jax.ShapeDtypeStruct((T, D_MODEL), F32),
        grid=(NT,),
        in_specs=[
            head_spec, head_spec,
            pl.BlockSpec((C_HEADS, TM, 128), lambda i: (4, i, 0)),
            _tile_spec(), _mod_spec(),
            _const_spec((1, C_DV)),
            _const_spec((D_MODEL, D_MODEL)),
        ],
        out_specs=_tile_spec(),
        scratch_shapes=[pltpu.VMEM((TM, D_MODEL), BF16)],
        compiler_params=_params("parallel"),
        name="hgrn_out_proj",
    )(o_fw, o_bw, z3, x, mod, norm_o.reshape(1, -1), w_out.astype(BF16))


def _final_body(acc_ref, rt_ref, mod_ref, nf_ref, *rest):
    o_ref = rest[-1]
    o_ref[...] = _rms(acc_ref[...] + mod_ref[5:6, :] * rt_ref[...], nf_ref[...])


def _final(acc, routed, mod, norm_final, part_tile0, local0, n_tiles, out_tile0, out_tiles, prev=None):
    local = pl.BlockSpec((TM, D_MODEL), lambda i: (i + local0, 0))
    prev = () if prev is None else (prev,)
    return pl.pallas_call(
        _final_body,
        out_shape=jax.ShapeDtypeStruct((out_tiles * TM, D_MODEL), F32),
        grid=(n_tiles,),
        in_specs=[
            local, local,
            pl.BlockSpec((None, 6, D_MODEL), lambda i: (_mod_row(i + local0 + part_tile0), 0, 0)),
            _const_spec((1, D_MODEL)),
        ] + [pl.BlockSpec(memory_space=pl.ANY)] * len(prev),
        out_specs=pl.BlockSpec((TM, D_MODEL), lambda i: (i + out_tile0, 0)),
        input_output_aliases={4 + k: 0 for k in range(len(prev))},
        compiler_params=_params("parallel"),
        name="final_norm",
    )(acc, routed, mod, norm_final.reshape(1, -1), *prev)


def kernel(x_prompt, x_sample, state_hgrn, c, c_ctx, w_ada, b_ada, norm_mix, norm_ffn, w_out, w_in_ab, w_sg, b_sg, norm_sg, w_dw, b_dw, norm_cv, w_in_hgrn, lb_raw, norm_o, w_router, b_router, w_gate, w_up, w_down, w_sh_gate, w_sh_up, w_sh_down, norm_final):
    cvecs = jnp.concatenate(
        [c_ctx.reshape(1, D_MODEL), c, jnp.zeros((N_MOD_ROWS - 1 - DEC_BATCH, D_MODEL), F32)], axis=0)
    mods = _ada_tables(cvecs, w_ada, b_ada)
    lb_sm = jax.nn.softmax(lb_raw.astype(F32), axis=0)
    lb1 = (jnp.cumsum(lb_sm, axis=0) - lb_sm[0])[1].reshape(2, C_HEADS, 1, C_DK)

    def moe(l, xin):
        parts = []
        for p in range(MOE_PARTS):
            h, eidx_t, wsplat, rank_t, counts, acc = _moe_pre(
                xin, mods[l], norm_ffn[l], w_router[l], b_router[l], w_sh_gate[l], w_sh_up[l], w_sh_down[l],
                p * NT_PART)
            parts += [acc, _moe_routed(h, eidx_t, rank_t, counts, wsplat, w_gate, w_up, w_down, l)]
        return parts

    x = _l0_mixer(x_prompt.reshape(T_P, D_MODEL), x_sample.reshape(T_S, D_MODEL), mods[0], norm_mix[0], w_in_ab[0], w_sg[0], b_sg[0], norm_sg[0], w_dw[0],
                  b_dw[0], norm_cv[0], w_out[0])
    parts = moe(0, x)
    xz = None
    for p in range(MOE_PARTS):
        xz = _hgrn_in(parts[2 * p], parts[2 * p + 1], p * NT_PART, xz, mods[0], mods[1], norm_mix[1], w_in_hgrn[0])
    x, z3 = xz
    s0t = jnp.swapaxes(state_hgrn[:, 0].astype(F32), -1, -2)
    o_fw, ns_fw = _gla(z3, lb1[0], s0t[:, 0], rev=False)
    o_bw, ns_bw = _gla(z3, lb1[1], s0t[:, 1], rev=True)
    x = _hgrn_out(o_fw, o_bw, z3, x, mods[1], norm_o[0], w_out[1])
    parts = moe(1, x)
    y_p = y_s = None
    for p in range(MOE_PARTS):
        lo, hi = p * NT_PART, (p + 1) * NT_PART
        if lo < NT_P:
            n = min(hi, NT_P) - lo
            y_p = _final(parts[2 * p], parts[2 * p + 1], mods[1], norm_final, lo, 0, n, lo, NT_P, y_p)
        if hi > NT_P:
            first = max(lo, NT_P)
            y_s = _final(parts[2 * p], parts[2 * p + 1], mods[1], norm_final, lo, first - lo, hi - first,
                         first - NT_P, NT_S, y_s)
    y_p = y_p.reshape(BATCH, SEQ, D_MODEL)
    y_s = y_s.reshape(DEC_BATCH, DEC_SEQ, D_MODEL)
    new_state = jnp.swapaxes(jnp.stack([ns_fw, ns_bw], axis=1), -1, -2)[:, None]
    return (y_p, y_s, new_state)
```

```python
import functools

import jax
import jax.numpy as jnp
from jax import lax
from jax.experimental import pallas as pl
from jax.experimental.pallas import tpu as pltpu
from jax.experimental.pallas import tpu_sc as plsc

F32 = jnp.float32
BF16 = jnp.bfloat16
HIGHEST = lax.Precision.HIGHEST

D_MODEL = 1024
BATCH = 32
SEQ = 256
DEPTH = 2
DEC_BATCH = 8
DEC_SEQ = 2048
GRID_W = 64
A_WIDTH = D_MODEL // 2
A_GROUPS = 4
A_GC = A_WIDTH // A_GROUPS
CHUNK_A = 128
B_WIDTH = D_MODEL - A_WIDTH
CONV_W = 31
CONV_PAD = CONV_W // 2
C_HEADS = 8
C_DK = 128
C_DV = D_MODEL // C_HEADS
C_FDIM = C_HEADS * C_DK
SCAN_CHUNK = 64
N_EXPERTS = 64
TOP_K = 8
N_GROUPS = 8
GROUP_SIZE = N_EXPERTS // N_GROUPS
TOPK_GROUPS = 4
D_EXPERT = 256
D_SHARED = 256
ROUTED_SCALE = 2.5
EPS = 1e-6

TM = 256
T_P = BATCH * SEQ
T_S = DEC_BATCH * DEC_SEQ
T = T_P + T_S
NT_P = T_P // TM
NT_S = T_S // TM
NT = NT_P + NT_S
TILES_PER_LAT = DEC_SEQ // TM
TE = 512
MOE_PARTS = 2
NT_PART = NT // MOE_PARTS
T_PART = NT_PART * TM
NB = T_PART * TOP_K // TE + N_EXPERTS
P_ROWS = NB * TE
D_PACK = D_MODEL // 2
N_MOD_ROWS = 16
CONV_HALO = 16
VMEM_LIMIT = 48 * 1024 * 1024
SC_CORES = 2
SC_SUBCORES = 16
SC_WORKERS = SC_CORES * SC_SUBCORES
SC_LANES = 16
SC_CHUNK = 64
COMBINE_TOKENS = SC_CHUNK // TOP_K


def _mod_row(i):
    return jnp.where(i < NT_P, 0, 1 + (i - NT_P) // TILES_PER_LAT)


def _silu(x):
    return x * jax.nn.sigmoid(x)


def _gelu(x):
    return x * (0.5 * (1.0 + jnp.tanh(0.7978845608028654 * (x + 0.044715 * (x * x * x)))))


def _rms(x, g):
    return x * lax.rsqrt(jnp.mean(x * x, axis=-1, keepdims=True) + EPS) * g


def _layernorm(x, g):
    xc = x - jnp.mean(x, axis=-1, keepdims=True)
    return xc * lax.rsqrt(jnp.mean(xc * xc, axis=-1, keepdims=True) + EPS) * g


def _modulate(x, g, shift, scale):
    return _rms(x, g) * (1.0 + scale) + shift


def _bdot(a, b):
    return jnp.dot(a, b, preferred_element_type=F32)


def _dot_nt(a, b, precision=None):
    return lax.dot_general(a, b, (((1,), (1,)), ((), ())), precision=precision,
                           preferred_element_type=F32)


def _dot_tn(a, b):
    return lax.dot_general(a, b, (((0,), (0,)), ((), ())), preferred_element_type=F32)


def _pack_pairs(x):
    m = x.shape[1] // 2
    lo = lax.bitcast_convert_type(x[:, :m].astype(BF16).astype(F32), jnp.uint32)
    hi = lax.bitcast_convert_type(x[:, m:].astype(BF16).astype(F32), jnp.uint32)
    return lax.bitcast_convert_type(hi | (lo >> 16), jnp.int32)


def _unpack_pairs(w):
    u = lax.bitcast_convert_type(w, jnp.uint32)
    lo = lax.bitcast_convert_type(u << 16, F32)
    hi = lax.bitcast_convert_type(u & jnp.uint32(0xFFFF0000), F32)
    return lo, hi


def _params(*sem):
    return pltpu.CompilerParams(dimension_semantics=sem, vmem_limit_bytes=VMEM_LIMIT)


def _const_spec(shape):
    nd = len(shape)
    return pl.BlockSpec(shape, lambda *_: (0,) * nd)


def _ada_body(c_ref, w_ref, b_ref, o_ref):
    s = _silu(c_ref[...])
    o_ref[...] = jnp.dot(s, w_ref[...], precision=HIGHEST, preferred_element_type=F32) + b_ref[...]


def _ada_tables(cvecs, w_ada, b_ada):
    out = pl.pallas_call(
        _ada_body,
        out_shape=jax.ShapeDtypeStruct((DEPTH, N_MOD_ROWS, 6 * D_MODEL), F32),
        grid=(DEPTH, 6),
        in_specs=[
            _const_spec((N_MOD_ROWS, D_MODEL)),
            pl.BlockSpec((None, D_MODEL, D_MODEL), lambda l, j: (l, 0, j)),
            pl.BlockSpec((None, 1, D_MODEL), lambda l, j: (l, 0, j)),
        ],
        out_specs=pl.BlockSpec((None, N_MOD_ROWS, D_MODEL), lambda l, j: (l, 0, j)),
        compiler_params=_params("parallel", "parallel"),
        name="ada_tables",
    )(cvecs, w_ada, b_ada.reshape(DEPTH, 1, 6 * D_MODEL))
    return out.reshape(DEPTH, N_MOD_ROWS, 6, D_MODEL)


def _mod_spec():
    return pl.BlockSpec((None, 6, D_MODEL), lambda i: (_mod_row(i), 0, 0))


def _tile_spec():
    return pl.BlockSpec((TM, D_MODEL), lambda i: (i, 0))


SUBLANES = 8


def _conv_segment(pad_ref, shift_ref, conv_ref, wdw_ref, pad_base, out_base, seg):
    rb = min(seg, 64)
    for cb in range(B_WIDTH // 128):
        cs = slice(cb * 128, (cb + 1) * 128)
        for r0 in range(0, seg, rb):
            acc = jnp.zeros((rb, 128), F32)
            for k in range(CONV_W):
                b = (CONV_HALO - CONV_PAD + k) % SUBLANES
                off = pad_base + r0 + CONV_HALO - CONV_PAD + k - b
                src = pad_ref if b == 0 else shift_ref.at[b - 1]
                acc = acc + wdw_ref[k:k + 1, cs] * src[off:off + rb, cs]
            conv_ref[out_base + r0:out_base + r0 + rb, cs] = acc


def _l0_body(xc_ref, xl_ref, mod_ref, nmix_ref, win_ref, wsg_ref, bsg_ref, nsg_ref, wdw_ref, bdw_ref,
             ncv_ref, wout_ref, o_ref, cat_ref, pad_ref, shift_ref, conv_ref):
    i = pl.program_id(0)
    x = jnp.where(i < NT_P, xc_ref[...], xl_ref[...])
    h = _modulate(x, nmix_ref[...], mod_ref[0:1, :], mod_ref[1:2, :])
    z = _bdot(h.astype(BF16), win_ref[...])
    u = _gelu(z[:, :A_WIDTH])
    vb = _layernorm(_gelu(z[:, A_WIDTH:2 * A_WIDTH]), nsg_ref[...]).astype(BF16)
    for n in range(TM // CHUNK_A):
        rs = slice(n * CHUNK_A, (n + 1) * CHUNK_A)
        for g in range(A_GROUPS):
            cs = slice(g * A_GC, (g + 1) * A_GC)
            m = _bdot(wsg_ref[g], vb[rs, cs]) + bsg_ref[g]
            cat_ref[rs, cs] = (u[rs, cs] * m).astype(BF16)
    hb = z[:, 2 * A_WIDTH:2 * A_WIDTH + B_WIDTH] * jax.nn.sigmoid(z[:, 2 * A_WIDTH + B_WIDTH:])

    def conv_tile(seg):
        stride = seg + 2 * CONV_HALO
        halo = jnp.zeros((CONV_HALO, B_WIDTH), F32)
        for s in range(TM // seg):
            b = s * stride
            pad_ref[b:b + CONV_HALO, :] = halo
            pad_ref[b + CONV_HALO:b + CONV_HALO + seg, :] = hb[s * seg:(s + 1) * seg, :]
            pad_ref[b + CONV_HALO + seg:b + stride, :] = halo
        rows = (TM // seg) * stride - SUBLANES
        for b in range(1, SUBLANES):
            shift_ref[b - 1, 0:rows, :] = pad_ref[b:b + rows, :]
        for s in range(TM // seg):
            _conv_segment(pad_ref, shift_ref, conv_ref, wdw_ref, s * stride, s * seg, seg)

    @pl.when(i < NT_P)
    def _():
        conv_tile(SEQ)

    @pl.when(i >= NT_P)
    def _():
        conv_tile(GRID_W)

    yb = _layernorm(conv_ref[...] + bdw_ref[...], ncv_ref[...])
    cat_ref[:, A_WIDTH:] = _silu(yb).astype(BF16)
    out = _bdot(cat_ref[...], wout_ref[...])
    o_ref[...] = x + mod_ref[2:3, :] * out


def _l0_mixer(x_ctx, x_lat, mod, norm_mix, w_in, w_sg, b_sg, norm_sg, w_dw, b_dw, norm_cv, w_out):
    assert TM == SEQ and TM % GRID_W == 0 and TM % CHUNK_A == 0
    pad_rows = (TM // GRID_W) * (GRID_W + 2 * CONV_HALO)
    assert pad_rows >= SEQ + 2 * CONV_HALO
    return pl.pallas_call(
        _l0_body,
        out_shape=jax.ShapeDtypeStruct((T, D_MODEL), F32),
        grid=(NT,),
        in_specs=[
            pl.BlockSpec((TM, D_MODEL), lambda i: (jnp.minimum(i, NT_P - 1), 0)),
            pl.BlockSpec((TM, D_MODEL), lambda i: (jnp.maximum(i - NT_P, 0), 0)),
            _mod_spec(),
            _const_spec((1, D_MODEL)),
            _const_spec((D_MODEL, 2 * A_WIDTH + 2 * B_WIDTH)),
            _const_spec((A_GROUPS, CHUNK_A, CHUNK_A)),
            _const_spec((A_GROUPS, CHUNK_A, 1)),
            _const_spec((1, A_WIDTH)),
            _const_spec((CONV_W, B_WIDTH)),
            _const_spec((1, B_WIDTH)),
            _const_spec((1, B_WIDTH)),
            _const_spec((D_MODEL, D_MODEL)),
        ],
        out_specs=_tile_spec(),
        scratch_shapes=[
            pltpu.VMEM((TM, D_MODEL), BF16),
            pltpu.VMEM((pad_rows, B_WIDTH), F32),
            pltpu.VMEM((SUBLANES - 1, pad_rows, B_WIDTH), F32),
            pltpu.VMEM((TM, B_WIDTH), F32),
        ],
        compiler_params=_params("parallel"),
        name="l0_mixer",
    )(x_ctx, x_lat, mod, norm_mix.reshape(1, -1), w_in.astype(BF16), w_sg.astype(BF16),
      b_sg.reshape(A_GROUPS, CHUNK_A, 1), norm_sg.reshape(1, -1), w_dw, b_dw.reshape(1, -1),
      norm_cv.reshape(1, -1), w_out.astype(BF16))


def _route(scores, biased):
    n = scores.shape[-1]
    shp = (N_GROUPS, GROUP_SIZE, n)
    s3 = scores.reshape(shp)
    b3 = biased.reshape(shp)
    m_iota = lax.broadcasted_iota(jnp.int32, shp, 1).astype(F32)
    g_iota = lax.broadcasted_iota(jnp.int32, shp, 0).astype(F32)
    e_iota = g_iota * GROUP_SIZE + m_iota
    neg = -jnp.inf

    def amax1(v):
        return jnp.max(v, axis=1, keepdims=True)

    def amin1(v):
        return jnp.min(v, axis=1, keepdims=True)

    m1 = amax1(b3)
    i1 = amin1(jnp.where(b3 == m1, m_iota, float(GROUP_SIZE)))
    m2 = amax1(jnp.where(m_iota == i1, neg, b3))
    grp = m1 + m2
    gi1 = lax.broadcasted_iota(jnp.int32, grp.shape, 0).astype(F32)
    gmask = jnp.zeros(grp.shape, jnp.bool_)
    for _ in range(TOPK_GROUPS):
        gm = jnp.max(grp, axis=0, keepdims=True)
        gi = jnp.min(jnp.where(grp == gm, gi1, float(N_GROUPS)), axis=0, keepdims=True)
        hit = gi1 == gi
        gmask = jnp.logical_or(gmask, hit)
        grp = jnp.where(hit, neg, grp)
    cand = jnp.where(gmask, b3, neg)
    ids, vals, hits = [], [], []
    for _ in range(TOP_K):
        mx = jnp.max(amax1(cand), axis=0, keepdims=True)
        ei = jnp.min(amin1(jnp.where(cand == mx, e_iota, float(N_EXPERTS))), axis=0, keepdims=True)
        hit = e_iota == ei
        ids.append(ei.reshape(1, n))
        vals.append(_pick(hit, s3))
        hits.append(hit)
        cand = jnp.where(hit, neg, cand)
    return jnp.concatenate(ids, axis=0).astype(jnp.int32), jnp.concatenate(vals, axis=0), hits


def _pick(hit, v3):
    s = jnp.sum(jnp.sum(jnp.where(hit, v3, 0.0), axis=1, keepdims=True), axis=0, keepdims=True)
    return s.reshape(1, v3.shape[-1])


def _moe_pre_body(x_ref, mod_ref, nffn_ref, wrt_ref, br_ref, wsg_ref, wsu_ref, wsd_ref,
                  h_ref, eidx_ref, w8_ref, rank_ref, cnt_ref, acc_ref, run_ref):
    @pl.when(pl.program_id(0) == 0)
    def _():
        run_ref[...] = jnp.zeros(run_ref.shape, F32)

    x = x_ref[...]
    h = _modulate(x, nffn_ref[...], mod_ref[3:4, :], mod_ref[4:5, :])
    h_ref[...] = _pack_pairs(h)
    hb = h.astype(BF16)
    h_lo = (h - hb.astype(F32)).astype(BF16)
    wr = wrt_ref[...]
    wr_hi = wr.astype(BF16)
    wr_lo = (wr - wr_hi.astype(F32)).astype(BF16)
    logits_t = _dot_nt(wr_hi, hb) + (_dot_nt(wr_hi, h_lo) + _dot_nt(wr_lo, hb))
    scores = jax.nn.sigmoid(logits_t)
    eidx, sv, hits = _route(scores, scores + br_ref[...])
    eidx_ref[...] = eidx
    w8_ref[...] = sv / jnp.sum(sv, axis=0, keepdims=True) * ROUTED_SCALE
    sel3 = hits[0]
    for hit in hits[1:]:
        sel3 = jnp.logical_or(sel3, hit)
    sel = sel3.astype(F32).reshape(N_EXPERTS, TM)
    earlier = (lax.broadcasted_iota(jnp.int32, (TM, TM), 0)
               < lax.broadcasted_iota(jnp.int32, (TM, TM), 1)).astype(BF16)
    rank3 = (_bdot(sel.astype(BF16), earlier) + run_ref[...]).reshape(N_GROUPS, GROUP_SIZE, TM)
    rank_ref[...] = jnp.concatenate([_pick(hit, rank3) for hit in hits], axis=0).astype(jnp.int32)
    run_ref[...] = run_ref[...] + jnp.sum(sel, axis=1, keepdims=True)
    cnt_ref[...] = run_ref[...].astype(jnp.int32)
    sh = _bdot((_silu(_bdot(hb, wsg_ref[...])) * _bdot(hb, wsu_ref[...])).astype(BF16), wsd_ref[...])
    acc_ref[...] = x + mod_ref[5:6, :] * sh


def _moe_pre(x, mod, norm_ffn, w_router, b_router, w_sh_gate, w_sh_up, w_sh_down, tile0):
    return pl.pallas_call(
        _moe_pre_body,
        out_shape=(
            jax.ShapeDtypeStruct((T_PART, D_PACK), jnp.int32),
            jax.ShapeDtypeStruct((TOP_K, T_PART), jnp.int32),
            jax.ShapeDtypeStruct((TOP_K, T_PART), F32),
            jax.ShapeDtypeStruct((TOP_K, T_PART), jnp.int32),
            jax.ShapeDtypeStruct((N_EXPERTS, 1), jnp.int32),
            jax.ShapeDtypeStruct((T_PART, D_MODEL), F32),
        ),
        grid=(NT_PART,),
        in_specs=[
            pl.BlockSpec((TM, D_MODEL), lambda i: (i + tile0, 0)),
            pl.BlockSpec((None, 6, D_MODEL), lambda i: (_mod_row(i + tile0), 0, 0)),
            _const_spec((1, D_MODEL)),
            _const_spec((N_EXPERTS, D_MODEL)),
            _const_spec((N_EXPERTS, 1)),
            _const_spec((D_MODEL, D_SHARED)),
            _const_spec((D_MODEL, D_SHARED)),
            _const_spec((D_SHARED, D_MODEL)),
        ],
        out_specs=(
            pl.BlockSpec((TM, D_PACK), lambda i: (i, 0)),
            pl.BlockSpec((TOP_K, TM), lambda i: (0, i)),
            pl.BlockSpec((TOP_K, TM), lambda i: (0, i)),
            pl.BlockSpec((TOP_K, TM), lambda i: (0, i)),
            _const_spec((N_EXPERTS, 1)),
            _tile_spec(),
        ),
        scratch_shapes=[pltpu.VMEM((N_EXPERTS, 1), F32)],
        compiler_params=_params("arbitrary"),
        name="moe_router_shared",
    )(x, mod, norm_ffn.reshape(1, -1), w_router.T, b_router.reshape(N_EXPERTS, 1),
      w_sh_gate.astype(BF16), w_sh_up.astype(BF16), w_sh_down.astype(BF16))


EXPERT_RING = 3


def _expert_body(be_ref, nv_ref, nu_ref, xs_hbm, wg_ref, wu_ref, wd_ref, ys_ref, wgu_s, wd_s, ring, sems):
    j = pl.program_id(0)
    n_used = nu_ref[0]
    live_tile = j < n_used

    def fetch(t):
        slot = t % EXPERT_RING
        return pltpu.make_async_copy(xs_hbm.at[pl.ds(t * TE, TE)], ring.at[slot], sems.at[slot])

    @pl.when(j == 0)
    def _():
        for t in range(EXPERT_RING - 1):
            @pl.when(t < n_used)
            def _():
                fetch(t).start()

    @pl.when(j + (EXPERT_RING - 1) < n_used)
    def _():
        fetch(j + (EXPERT_RING - 1)).start()

    @pl.when(jnp.logical_and(live_tile, jnp.logical_or(j == 0, be_ref[j] != be_ref[jnp.maximum(j - 1, 0)])))
    def _():
        wgu_s[:, :D_EXPERT] = wg_ref[...].astype(BF16)
        wgu_s[:, D_EXPERT:] = wu_ref[...].astype(BF16)
        wd_s[...] = wd_ref[...].astype(BF16)

    @pl.when(live_tile)
    def _():
        fetch(j).wait()
        live = lax.broadcasted_iota(jnp.int32, (TE, 1), 0) < nv_ref[j]
        lo, hi = _unpack_pairs(jnp.where(live, ring[j % EXPERT_RING], 0))
        xb = jnp.concatenate([lo.astype(BF16), hi.astype(BF16)], axis=1)
        hgu = _bdot(xb, wgu_s[...])
        hh = _silu(hgu[:, :D_EXPERT]) * hgu[:, D_EXPERT:]
        ys_ref[...] = _pack_pairs(_bdot(hh.astype(BF16), wd_s[...]))


def _experts(block_expert, block_rows, n_used, xs, w_gate, w_up, w_down, layer):
    def row_map(j, be, nv, nu):
        return (jnp.minimum(j, nu[0] - 1), 0)

    def w_map(j, be, nv, nu):
        return (layer, be[jnp.minimum(j, nu[0] - 1)], 0, 0)

    return pl.pallas_call(
        _expert_body,
        out_shape=jax.ShapeDtypeStruct((P_ROWS, D_PACK), jnp.int32),
        grid_spec=pltpu.PrefetchScalarGridSpec(
            num_scalar_prefetch=3,
            grid=(NB,),
            in_specs=[
                pl.BlockSpec(memory_space=pl.ANY),
                pl.BlockSpec((None, None, D_MODEL, D_EXPERT), w_map),
                pl.BlockSpec((None, None, D_MODEL, D_EXPERT), w_map),
                pl.BlockSpec((None, None, D_EXPERT, D_MODEL), w_map),
            ],
            out_specs=pl.BlockSpec((TE, D_PACK), row_map),
            scratch_shapes=[
                pltpu.VMEM((D_MODEL, 2 * D_EXPERT), BF16),
                pltpu.VMEM((D_EXPERT, D_MODEL), BF16),
                pltpu.VMEM((EXPERT_RING, TE, D_PACK), jnp.int32),
                pltpu.SemaphoreType.DMA((EXPERT_RING,)),
            ],
        ),
        compiler_params=_params("arbitrary"),
        name="moe_experts",
    )(block_expert, block_rows, n_used, xs, w_gate, w_up, w_down)


def _positions_body(start_ref, eidx_ref, rank_ref, pos_ref):
    eidx = eidx_ref[...]
    base = jnp.zeros(eidx.shape, jnp.int32)
    for e in range(N_EXPERTS):
        base = jnp.where(eidx == e, start_ref[e], base)
    pos_ref[...] = base * TE + rank_ref[...]


def _positions(blk_start, eidx_t, rank_t):
    full = pl.BlockSpec((TOP_K, T_PART), lambda i, s: (0, 0))
    return pl.pallas_call(
        _positions_body,
        out_shape=jax.ShapeDtypeStruct((TOP_K, T_PART), jnp.int32),
        grid_spec=pltpu.PrefetchScalarGridSpec(
            num_scalar_prefetch=1, grid=(1,), in_specs=[full, full], out_specs=full),
        compiler_params=_params("arbitrary"),
        name="moe_positions",
    )(blk_start, eidx_t, rank_t)


def _sc_mesh():
    return plsc.VectorSubcoreMesh(core_axis_name="c", subcore_axis_name="s")


def _sc_worker():
    return lax.axis_index("s") * SC_CORES + lax.axis_index("c")


def _sc_scatter_rows(h, pos_rows):
    c = SC_CHUNK
    n_chunks = T_PART // SC_WORKERS // c
    width = h.shape[1]

    @functools.partial(
        pl.kernel, mesh=_sc_mesh(),
        out_type=jax.ShapeDtypeStruct((P_ROWS, width), h.dtype),
        scratch_types=[pltpu.VMEM((n_chunks * TOP_K, c), jnp.int32),
                       pltpu.VMEM((c, width), h.dtype), pltpu.VMEM((c, width), h.dtype)]
        + [pltpu.SemaphoreType.DMA] * 4,
        name="moe_dispatch_scatter",
    )
    def scatter(h_hbm, pos_hbm, xs_hbm, idx_v, buf0, buf1, sem_in0, sem_in1, sem_out0, sem_out1):
        assert n_chunks % 2 == 0
        bufs, sem_in, sem_out = (buf0, buf1), (sem_in0, sem_in1), (sem_out0, sem_out1)
        first = _sc_worker() * n_chunks
        pltpu.sync_copy(pos_hbm.at[pl.ds(first * TOP_K, n_chunks * TOP_K)], idx_v)

        def load(i, b):
            return pltpu.make_async_copy(h_hbm.at[pl.ds((first + i) * c, c)], bufs[b], sem_in[b])

        def puts(i, b):
            return [pltpu.make_async_copy(bufs[b], xs_hbm.at[idx_v.at[i * TOP_K + k]], sem_out[b])
                    for k in range(TOP_K)]

        load(0, 0).start()
        load(1, 1).start()

        @pl.loop(0, n_chunks, step=2)
        def _(i):
            for b in range(2):
                load(i + b, b).wait()
                for cp in puts(i + b, b):
                    cp.start()
            for b in range(2):
                for cp in puts(i + b, b):
                    cp.wait()

                @pl.when(i + 2 + b < n_chunks)
                def _():
                    load(i + 2 + b, b).start()

    return scatter(h, pos_rows)


def _sc_combine(ys, pos_tk, wsplat):
    ct = COMBINE_TOKENS
    rows = ct * TOP_K
    tok_per_worker = T_PART // SC_WORKERS
    n_chunks = tok_per_worker // ct
    width = ys.shape[1]
    vmem = pltpu.VMEM

    @functools.partial(
        pl.kernel, mesh=_sc_mesh(),
        out_type=jax.ShapeDtypeStruct((T_PART, D_MODEL), F32),
        scratch_types=[vmem((tok_per_worker * TOP_K,), jnp.int32),
                       vmem((rows, width), ys.dtype), vmem((rows, width), ys.dtype),
                       vmem((ct, TOP_K * SC_LANES), F32), vmem((ct, TOP_K * SC_LANES), F32),
                       vmem((ct, D_MODEL), F32), vmem((ct, D_MODEL), F32)]
        + [pltpu.SemaphoreType.DMA] * 6,
        compiler_params=pltpu.CompilerParams(needs_layout_passes=False),
        name="moe_combine",
    )
    def combine(ys_hbm, pos_hbm, w_hbm, out_hbm, idx_v, g0, g1, w0, w1, o0, o1,
                sem_g0, sem_g1, sem_w0, sem_w1, sem_o0, sem_o1):
        assert n_chunks % 2 == 0
        gbuf, wbuf, obuf = (g0, g1), (w0, w1), (o0, o1)
        sem_g, sem_w, sem_o = (sem_g0, sem_g1), (sem_w0, sem_w1), (sem_o0, sem_o1)
        tok0 = _sc_worker() * tok_per_worker
        pltpu.sync_copy(pos_hbm.at[pl.ds(tok0 * TOP_K, tok_per_worker * TOP_K)], idx_v)

        def fetch(i, b):
            return [pltpu.make_async_copy(ys_hbm.at[idx_v.at[pl.ds(i * rows, rows)]], gbuf[b], sem_g[b]),
                    pltpu.make_async_copy(w_hbm.at[pl.ds(tok0 + i * ct, ct)], wbuf[b], sem_w[b])]

        def flush(i, b):
            return pltpu.make_async_copy(obuf[b], out_hbm.at[pl.ds(tok0 + i * ct, ct)], sem_o[b])

        def reduce_chunk(b):
            @pl.loop(0, ct)
            def _(t):
                wv = [wbuf[b][t, pl.ds(k * SC_LANES, SC_LANES)] for k in range(TOP_K)]

                @pl.loop(0, width // SC_LANES)
                def _(j):
                    col = j * SC_LANES
                    acc_lo = acc_hi = None
                    for k in range(TOP_K):
                        word = gbuf[b][t * TOP_K + k, pl.ds(col, SC_LANES)]
                        lo = lax.bitcast_convert_type(word << 16, F32)
                        hi = lax.bitcast_convert_type(word & jnp.int32(-65536), F32)
                        acc_lo = wv[k] * lo if k == 0 else acc_lo + wv[k] * lo
                        acc_hi = wv[k] * hi if k == 0 else acc_hi + wv[k] * hi
                    obuf[b][t, pl.ds(col, SC_LANES)] = acc_lo
                    obuf[b][t, pl.ds(D_PACK + col, SC_LANES)] = acc_hi

        for b in range(2):
            for cp in fetch(b, b):
                cp.start()

        @pl.loop(0, n_chunks, step=2)
        def _(i):
            for b in range(2):
                for cp in fetch(i + b, b):
                    cp.wait()

                @pl.when(i > 0)
                def _():
                    flush(i + b - 2, b).wait()

                reduce_chunk(b)
                flush(i + b, b).start()

                @pl.when(i + 2 + b < n_chunks)
                def _():
                    for cp in fetch(i + 2 + b, b):
                        cp.start()

        for b in range(2):
            flush(n_chunks - 2 + b, b).wait()

    return combine(ys, pos_tk, wsplat)


def _moe_routed(h, eidx_t, rank_t, counts, wsplat, w_gate, w_up, w_down, layer):
    counts = counts.reshape(1, N_EXPERTS)
    nblk = (counts + TE - 1) // TE
    blk_end = jnp.cumsum(nblk, axis=1)
    blk_start = blk_end - nblk
    blocks = jnp.arange(NB, dtype=jnp.int32).reshape(NB, 1)
    block_expert = jnp.minimum(jnp.sum(blocks >= blk_end, axis=1, keepdims=True), N_EXPERTS - 1)
    mine = block_expert == jnp.arange(N_EXPERTS, dtype=jnp.int32).reshape(1, N_EXPERTS)
    cnt_b = jnp.sum(jnp.where(mine, counts, 0), axis=1, keepdims=True)
    start_b = jnp.sum(jnp.where(mine, blk_start, 0), axis=1, keepdims=True)
    block_rows = jnp.clip(cnt_b - (blocks - start_b) * TE, 0, TE)
    n_used = blk_end[0, -1].reshape(1).astype(jnp.int32)
    pos = _positions(blk_start.reshape(N_EXPERTS).astype(jnp.int32), eidx_t, rank_t)
    pos_rows = pos.reshape(TOP_K, T_PART // SC_CHUNK, SC_CHUNK).transpose(1, 0, 2).reshape(-1, SC_CHUNK)
    xs = _sc_scatter_rows(h, pos_rows)
    ys = _experts(block_expert.reshape(NB).astype(jnp.int32), block_rows.reshape(NB).astype(jnp.int32),
                  n_used, xs, w_gate, w_up, w_down, layer)
    return _sc_combine(ys, pos.T.reshape(-1), wsplat)


N_SLABS = (3 * C_FDIM + 2 * D_MODEL) // 128


def _hgrn_in_body(acc_ref, rt_ref, mod0_ref, mod1_ref, nmix_ref, win_ref, *rest):
    x_ref, z_ref = rest[-2:]
    x = acc_ref[...] + mod0_ref[5:6, :] * rt_ref[...]
    x_ref[...] = x
    hb = _modulate(x, nmix_ref[...], mod1_ref[0:1, :], mod1_ref[1:2, :]).astype(BF16)
    for s in range(N_SLABS // C_HEADS):
        zz = _bdot(hb, win_ref[:, s * D_MODEL:(s + 1) * D_MODEL])
        for hh in range(C_HEADS):
            z_ref[s * C_HEADS + hh] = zz[:, hh * 128:(hh + 1) * 128]


def _hgrn_in(acc, routed, tile0, prev, mod0, mod1, norm_mix, w_in):
    def shifted_mod():
        return pl.BlockSpec((None, 6, D_MODEL), lambda i: (_mod_row(i + tile0), 0, 0))

    any_spec = pl.BlockSpec(memory_space=pl.ANY)
    prev = () if prev is None else tuple(prev)
    n_in = 6
    return pl.pallas_call(
        _hgrn_in_body,
        out_shape=(
            jax.ShapeDtypeStruct((T, D_MODEL), F32),
            jax.ShapeDtypeStruct((N_SLABS, T, 128), F32),
        ),
        grid=(NT_PART,),
        in_specs=[
            _tile_spec(), _tile_spec(), shifted_mod(), shifted_mod(),
            _const_spec((1, D_MODEL)),
            _const_spec((D_MODEL, 3 * C_FDIM + 2 * D_MODEL)),
        ] + [any_spec] * len(prev),
        out_specs=(
            pl.BlockSpec((TM, D_MODEL), lambda i: (i + tile0, 0)),
            pl.BlockSpec((N_SLABS, TM, 128), lambda i: (0, i + tile0, 0)),
        ),
        input_output_aliases={n_in + k: k for k in range(len(prev))},
        compiler_params=_params("parallel"),
        name="hgrn_in_proj",
    )(acc, routed, mod0, mod1, norm_mix.reshape(1, -1), w_in.astype(BF16), *prev)


def _gla_body(q_ref, f_ref, v_ref, lb_ref, s0_ref, o_ref, ns_ref, st_ref, *, rev):
    j = pl.program_id(0)
    ti = NT - 1 - j if rev else j
    is_ctx = ti < NT_P
    first_lat = (ti - NT_P) % TILES_PER_LAT == (TILES_PER_LAT - 1 if rev else 0)

    @pl.when(is_ctx)
    def _():
        st_ref[...] = jnp.zeros(st_ref.shape, F32)

    @pl.when(jnp.logical_and(jnp.logical_not(is_ctx), first_lat))
    def _():
        st_ref[...] = s0_ref[...]

    row = lax.broadcasted_iota(jnp.int32, (TM, TM), 0)
    col = lax.broadcasted_iota(jnp.int32, (TM, TM), 1)
    same_chunk = (row // SCAN_CHUNK) == (col // SCAN_CHUNK)
    seen = jnp.logical_and(same_chunk, (col >= row) if rev else (col <= row))
    cum_w = seen.astype(BF16)
    mid = SCAN_CHUNK // 2 if rev else SCAN_CHUNK // 2 - 1
    last = 0 if rev else SCAN_CHUNK - 1
    n_chunks = TM // SCAN_CHUNK
    order = range(n_chunks - 1, -1, -1) if rev else range(n_chunks)
    group = 4

    def chunk_rows(b, off):
        return jnp.concatenate(
            [jnp.broadcast_to(b[c * SCAN_CHUNK + off:c * SCAN_CHUNK + off + 1, :], (SCAN_CHUNK, b.shape[1]))
             for c in range(n_chunks)], axis=0)

    def head_group(gi, carry):
        heads = [gi * group + u for u in range(group)]
        qs, kk, vv, bcum = [], [], [], []
        for hd in heads:
            lb = lb_ref[hd]
            qs.append(_silu(q_ref[hd]) * (C_DK ** -0.5))
            fg = lb + (1.0 - lb) * jax.nn.sigmoid(f_ref[hd])
            kk.append(1.0 - fg)
            vv.append(v_ref[hd].astype(BF16))
            g = jnp.log(fg)
            g_hi = g.astype(BF16)
            r1 = g - g_hi.astype(F32)
            g_mid = r1.astype(BF16)
            g_lo = (r1 - g_mid.astype(F32)).astype(BF16)
            bcum.append(_bdot(cum_w, g_hi) + _bdot(cum_w, g_mid) + _bdot(cum_w, g_lo))
        o_intra, q_dec, kv, decay = [], [], [], []
        for u in range(group):
            b_mid = chunk_rows(bcum[u], mid)
            b_last = chunk_rows(bcum[u], last)
            qe = (qs[u] * jnp.exp(bcum[u] - b_mid)).astype(BF16)
            ke = (kk[u] * jnp.exp(b_mid - bcum[u])).astype(BF16)
            att = jnp.where(seen, _dot_nt(qe, ke), 0.0)
            o_intra.append(_bdot(att.astype(BF16), vv[u]))
            q_dec.append((qs[u] * jnp.exp(bcum[u])).astype(BF16))
            k_dec = (kk[u] * jnp.exp(b_last - bcum[u])).astype(BF16)
            kv.append([_dot_tn(vv[u][c * SCAN_CHUNK:(c + 1) * SCAN_CHUNK], k_dec[c * SCAN_CHUNK:(c + 1) * SCAN_CHUNK])
                       for c in range(n_chunks)])
            decay.append([jnp.exp(bcum[u][c * SCAN_CHUNK + last:c * SCAN_CHUNK + last + 1, :])
                          for c in range(n_chunks)])
        st = [st_ref[hd] for hd in heads]
        for c in order:
            sl = slice(c * SCAN_CHUNK, (c + 1) * SCAN_CHUNK)
            for u, hd in enumerate(heads):
                o_ref[hd, pl.ds(c * SCAN_CHUNK, SCAN_CHUNK), :] = (
                    o_intra[u][sl] + _dot_nt(q_dec[u][sl], st[u].astype(BF16)))
                st[u] = decay[u][c] * st[u] + kv[u][c]
        for u, hd in enumerate(heads):
            st_ref[hd] = st[u]
        return carry

    lax.fori_loop(0, C_HEADS // group, head_group, 0)

    @pl.when(is_ctx)
    def _():
        ns_ref[...] = st_ref[...]


def _gla(z3, lb_dir, s0t_dir, *, rev):
    def ti_of(j):
        return NT - 1 - j if rev else j

    f_slab = 2 if rev else 1

    def lat_map(j):
        return (jnp.clip((ti_of(j) - NT_P) // TILES_PER_LAT, 0, DEC_BATCH - 1), 0, 0, 0)

    return pl.pallas_call(
        functools.partial(_gla_body, rev=rev),
        out_shape=(
            jax.ShapeDtypeStruct((C_HEADS, T, C_DV), F32),
            jax.ShapeDtypeStruct((BATCH, C_HEADS, C_DV, C_DK), F32),
        ),
        grid=(NT,),
        in_specs=[
            pl.BlockSpec((C_HEADS, TM, 128), lambda j: (0, ti_of(j), 0)),
            pl.BlockSpec((C_HEADS, TM, 128), lambda j: (f_slab, ti_of(j), 0)),
            pl.BlockSpec((C_HEADS, TM, 128), lambda j: (3, ti_of(j), 0)),
            _const_spec((C_HEADS, 1, C_DK)),
            pl.BlockSpec((None, C_HEADS, C_DV, C_DK), lat_map),
        ],
        out_specs=(
            pl.BlockSpec((C_HEADS, TM, C_DV), lambda j: (0, ti_of(j), 0)),
            pl.BlockSpec((None, C_HEADS, C_DV, C_DK),
                         lambda j: (jnp.minimum(ti_of(j), NT_P - 1), 0, 0, 0)),
        ),
        scratch_shapes=[pltpu.VMEM((C_HEADS, C_DV, C_DK), F32)],
        compiler_params=_params("arbitrary"),
        name="gla_bwd" if rev else "gla_fwd",
    )(z3, z3, z3, lb_dir, s0t_dir)


def _hgrn_out_body(ofw_ref, obw_ref, gate_ref, x_ref, mod_ref, no_ref, wout_ref, o_ref, cat_ref):
    for hd in range(C_HEADS):
        o = ofw_ref[hd] + obw_ref[hd]
        cat_ref[:, hd * C_DV:(hd + 1) * C_DV] = (_rms(o, no_ref[...]) * _silu(gate_ref[hd])).astype(BF16)
    o_ref[...] = x_ref[...] + mod_ref[2:3, :] * _bdot(cat_ref[...], wout_ref[...])


def _hgrn_out(o_fw, o_bw, z3, x, mod, norm_o, w_out):
    head_spec = pl.BlockSpec((C_HEADS, TM, C_DV), lambda i: (0, i, 0))
    return pl.pallas_call(
        _hgrn_out_body,
        out_shape=jax.ShapeDtypeStruct((T, D_MODEL), F32),
        grid=(NT,),
        in_specs=[
            head_spec, head_spec,
            pl.BlockSpec((C_HEADS, TM, 128), lambda i: (4, i, 0)),
            _tile_spec(), _mod_spec(),
            _const_spec((1, C_DV)),
            _const_spec((D_MODEL, D_MODEL)),
        ],
        out_specs=_tile_spec(),
        scratch_shapes=[pltpu.VMEM((TM, D_MODEL), BF16)],
        compiler_params=_params("parallel"),
        name="hgrn_out_proj",
    )(o_fw, o_bw, z3, x, mod, norm_o.reshape(1, -1), w_out.astype(BF16))


def _final_body(acc_ref, rt_ref, mod_ref, nf_ref, *rest):
    o_ref = rest[-1]
    o_ref[...] = _rms(acc_ref[...] + mod_ref[5:6, :] * rt_ref[...], nf_ref[...])


def _final(acc, routed, mod, norm_final, part_tile0, local0, n_tiles, out_tile0, out_tiles, prev=None):
    local = pl.BlockSpec((TM, D_MODEL), lambda i: (i + local0, 0))
    prev = () if prev is None else (prev,)
    return pl.pallas_call(
        _final_body,
        out_shape=jax.ShapeDtypeStruct((out_tiles * TM, D_MODEL), F32),
        grid=(n_tiles,),
        in_specs=[
            local, local,
            pl.BlockSpec((None, 6, D_MODEL), lambda i: (_mod_row(i + local0 + part_tile0), 0, 0)),
            _const_spec((1, D_MODEL)),
        ] + [pl.BlockSpec(memory_space=pl.ANY)] * len(prev),
        out_specs=pl.BlockSpec((TM, D_MODEL), lambda i: (i + out_tile0, 0)),
        input_output_aliases={4 + k: 0 for k in range(len(prev))},
        compiler_params=_params("parallel"),
        name="final_norm",
    )(acc, routed, mod, norm_final.reshape(1, -1), *prev)


def kernel(x_prompt, x_sample, state_hgrn, c, c_ctx, w_ada, b_ada, norm_mix, norm_ffn, w_out, w_in_ab, w_sg, b_sg, norm_sg, w_dw, b_dw, norm_cv, w_in_hgrn, lb_raw, norm_o, w_router, b_router, w_gate, w_up, w_down, w_sh_gate, w_sh_up, w_sh_down, norm_final):
    cvecs = jnp.concatenate(
        [c_ctx.reshape(1, D_MODEL), c, jnp.zeros((N_MOD_ROWS - 1 - DEC_BATCH, D_MODEL), F32)], axis=0)
    mods = _ada_tables(cvecs, w_ada, b_ada)
    lb_sm = jax.nn.softmax(lb_raw.astype(F32), axis=0)
    lb1 = (jnp.cumsum(lb_sm, axis=0) - lb_sm[0])[1].reshape(2, C_HEADS, 1, C_DK)

    def moe(l, xin):
        parts = []
        for p in range(MOE_PARTS):
            h, eidx_t, w8_t, rank_t, counts, acc = _moe_pre(
                xin, mods[l], norm_ffn[l], w_router[l], b_router[l], w_sh_gate[l], w_sh_up[l], w_sh_down[l],
                p * NT_PART)
            wsplat = jnp.repeat(w8_t.T, SC_LANES, axis=1)
            parts += [acc, _moe_routed(h, eidx_t, rank_t, counts, wsplat, w_gate, w_up, w_down, l)]
        return parts

    x = _l0_mixer(x_prompt.reshape(T_P, D_MODEL), x_sample.reshape(T_S, D_MODEL), mods[0], norm_mix[0], w_in_ab[0], w_sg[0], b_sg[0], norm_sg[0], w_dw[0],
                  b_dw[0], norm_cv[0], w_out[0])
    parts = moe(0, x)
    xz = None
    for p in range(MOE_PARTS):
        xz = _hgrn_in(parts[2 * p], parts[2 * p + 1], p * NT_PART, xz, mods[0], mods[1], norm_mix[1], w_in_hgrn[0])
    x, z3 = xz
    s0t = jnp.swapaxes(state_hgrn[:, 0].astype(F32), -1, -2)
    o_fw, ns_fw = _gla(z3, lb1[0], s0t[:, 0], rev=False)
    o_bw, ns_bw = _gla(z3, lb1[1], s0t[:, 1], rev=True)
    x = _hgrn_out(o_fw, o_bw, z3, x, mods[1], norm_o[0], w_out[1])
    parts = moe(1, x)
    y_p = y_s = None
    for p in range(MOE_PARTS):
        lo, hi = p * NT_PART, (p + 1) * NT_PART
        if lo < NT_P:
            n = min(hi, NT_P) - lo
            y_p = _final(parts[2 * p], parts[2 * p + 1], mods[1], norm_final, lo, 0, n, lo, NT_P, y_p)
        if hi > NT_P:
            first = max(lo, NT_P)
            y_s = _final(parts[2 * p], parts[2 * p + 1], mods[1], norm_final, lo, first - lo, hi - first,
                         first - NT_P, NT_S, y_s)
    y_p = y_p.reshape(BATCH, SEQ, D_MODEL)
    y_s = y_s.reshape(DEC_BATCH, DEC_SEQ, D_MODEL)
    new_state = jnp.swapaxes(jnp.stack([ns_fw, ns_bw], axis=1), -1, -2)[:, None]
    return (y_p, y_s, new_state)
```

```python
import functools

import jax
import jax.numpy as jnp
from jax import lax
from jax.experimental import pallas as pl
from jax.experimental.pallas import tpu as pltpu
from jax.experimental.pallas import tpu_sc as plsc

F32 = jnp.float32
BF16 = jnp.bfloat16
HIGHEST = lax.Precision.HIGHEST

D_MODEL = 1024
BATCH = 32
SEQ = 256
DEPTH = 2
DEC_BATCH = 8
DEC_SEQ = 2048
GRID_W = 64
A_WIDTH = D_MODEL // 2
A_GROUPS = 4
A_GC = A_WIDTH // A_GROUPS
CHUNK_A = 128
B_WIDTH = D_MODEL - A_WIDTH
CONV_W = 31
CONV_PAD = CONV_W // 2
C_HEADS = 8
C_DK = 128
C_DV = D_MODEL // C_HEADS
C_FDIM = C_HEADS * C_DK
SCAN_CHUNK = 64
N_EXPERTS = 64
TOP_K = 8
N_GROUPS = 8
GROUP_SIZE = N_EXPERTS // N_GROUPS
TOPK_GROUPS = 4
D_EXPERT = 256
D_SHARED = 256
ROUTED_SCALE = 2.5
EPS = 1e-6

TM = 256
T_P = BATCH * SEQ
T_S = DEC_BATCH * DEC_SEQ
T = T_P + T_S
NT_P = T_P // TM
NT_S = T_S // TM
NT = NT_P + NT_S
TILES_PER_LAT = DEC_SEQ // TM
TE = 512
MOE_PARTS = 2
NT_PART = NT // MOE_PARTS
T_PART = NT_PART * TM
NB = T_PART * TOP_K // TE + N_EXPERTS
P_ROWS = NB * TE
D_PACK = D_MODEL // 2
N_MOD_ROWS = 16
CONV_HALO = 16
VMEM_LIMIT = 48 * 1024 * 1024
SC_CORES = 2
SC_SUBCORES = 16
SC_WORKERS = SC_CORES * SC_SUBCORES
SC_LANES = 16
SC_CHUNK = 64
COMBINE_TOKENS = SC_CHUNK // TOP_K


def _mod_row(i):
    return jnp.where(i < NT_P, 0, 1 + (i - NT_P) // TILES_PER_LAT)


def _silu(x):
    return x * jax.nn.sigmoid(x)


def _gelu(x):
    return x * (0.5 * (1.0 + jnp.tanh(0.7978845608028654 * (x + 0.044715 * (x * x * x)))))


def _rms(x, g):
    return x * lax.rsqrt(jnp.mean(x * x, axis=-1, keepdims=True) + EPS) * g


def _layernorm(x, g):
    xc = x - jnp.mean(x, axis=-1, keepdims=True)
    return xc * lax.rsqrt(jnp.mean(xc * xc, axis=-1, keepdims=True) + EPS) * g


def _modulate(x, g, shift, scale):
    return _rms(x, g) * (1.0 + scale) + shift


def _bdot(a, b):
    return jnp.dot(a, b, preferred_element_type=F32)


def _dot_nt(a, b, precision=None):
    return lax.dot_general(a, b, (((1,), (1,)), ((), ())), precision=precision,
                           preferred_element_type=F32)


def _dot_tn(a, b):
    return lax.dot_general(a, b, (((0,), (0,)), ((), ())), preferred_element_type=F32)


def _pack_pairs(x):
    m = x.shape[1] // 2
    lo = lax.bitcast_convert_type(x[:, :m].astype(BF16).astype(F32), jnp.uint32)
    hi = lax.bitcast_convert_type(x[:, m:].astype(BF16).astype(F32), jnp.uint32)
    return lax.bitcast_convert_type(hi | (lo >> 16), jnp.int32)


def _unpack_pairs(w):
    u = lax.bitcast_convert_type(w, jnp.uint32)
    lo = lax.bitcast_convert_type(u << 16, F32)
    hi = lax.bitcast_convert_type(u & jnp.uint32(0xFFFF0000), F32)
    return lo, hi


def _params(*sem):
    return pltpu.CompilerParams(dimension_semantics=sem, vmem_limit_bytes=VMEM_LIMIT)


def _const_spec(shape):
    nd = len(shape)
    return pl.BlockSpec(shape, lambda *_: (0,) * nd)


def _ada_body(c_ref, w_ref, b_ref, o_ref):
    s = _silu(c_ref[...])
    o_ref[...] = jnp.dot(s, w_ref[...], precision=HIGHEST, preferred_element_type=F32) + b_ref[...]


def _ada_tables(cvecs, w_ada, b_ada):
    out = pl.pallas_call(
        _ada_body,
        out_shape=jax.ShapeDtypeStruct((DEPTH, N_MOD_ROWS, 6 * D_MODEL), F32),
        grid=(DEPTH, 6),
        in_specs=[
            _const_spec((N_MOD_ROWS, D_MODEL)),
            pl.BlockSpec((None, D_MODEL, D_MODEL), lambda l, j: (l, 0, j)),
            pl.BlockSpec((None, 1, D_MODEL), lambda l, j: (l, 0, j)),
        ],
        out_specs=pl.BlockSpec((None, N_MOD_ROWS, D_MODEL), lambda l, j: (l, 0, j)),
        compiler_params=_params("parallel", "parallel"),
        name="ada_tables",
    )(cvecs, w_ada, b_ada.reshape(DEPTH, 1, 6 * D_MODEL))
    return out.reshape(DEPTH, N_MOD_ROWS, 6, D_MODEL)


def _mod_spec(tile0=0):
    return pl.BlockSpec((None, 6, D_MODEL), lambda i: (_mod_row(i + tile0), 0, 0))


def _tile_spec():
    return pl.BlockSpec((TM, D_MODEL), lambda i: (i, 0))


SUBLANES = 8


def _conv_segment(pad_ref, shift_ref, conv_ref, wdw_ref, pad_base, out_base, seg):
    rb = min(seg, 64)
    for cb in range(B_WIDTH // 128):
        cs = slice(cb * 128, (cb + 1) * 128)
        for r0 in range(0, seg, rb):
            acc = jnp.zeros((rb, 128), F32)
            for k in range(CONV_W):
                b = (CONV_HALO - CONV_PAD + k) % SUBLANES
                off = pad_base + r0 + CONV_HALO - CONV_PAD + k - b
                src = pad_ref if b == 0 else shift_ref.at[b - 1]
                acc = acc + wdw_ref[k:k + 1, cs] * src[off:off + rb, cs]
            conv_ref[out_base + r0:out_base + r0 + rb, cs] = acc


def _l0_tile(i, xc_ref, xl_ref, mod_ref, nmix_ref, win_ref, wsg_ref, bsg_ref, nsg_ref, wdw_ref, bdw_ref,
             ncv_ref, wout_ref, cat_ref, pad_ref, shift_ref, conv_ref):
    x = jnp.where(i < NT_P, xc_ref[...], xl_ref[...])
    h = _modulate(x, nmix_ref[...], mod_ref[0:1, :], mod_ref[1:2, :])
    z = _bdot(h.astype(BF16), win_ref[...])
    u = _gelu(z[:, :A_WIDTH])
    vb = _layernorm(_gelu(z[:, A_WIDTH:2 * A_WIDTH]), nsg_ref[...]).astype(BF16)
    for n in range(TM // CHUNK_A):
        rs = slice(n * CHUNK_A, (n + 1) * CHUNK_A)
        for g in range(A_GROUPS):
            cs = slice(g * A_GC, (g + 1) * A_GC)
            m = _bdot(wsg_ref[g], vb[rs, cs]) + bsg_ref[g]
            cat_ref[rs, cs] = (u[rs, cs] * m).astype(BF16)
    hb = z[:, 2 * A_WIDTH:2 * A_WIDTH + B_WIDTH] * jax.nn.sigmoid(z[:, 2 * A_WIDTH + B_WIDTH:])

    def conv_tile(seg):
        stride = seg + 2 * CONV_HALO
        halo = jnp.zeros((CONV_HALO, B_WIDTH), F32)
        for s in range(TM // seg):
            b = s * stride
            pad_ref[b:b + CONV_HALO, :] = halo
            pad_ref[b + CONV_HALO:b + CONV_HALO + seg, :] = hb[s * seg:(s + 1) * seg, :]
            pad_ref[b + CONV_HALO + seg:b + stride, :] = halo
        rows = (TM // seg) * stride - SUBLANES
        for b in range(1, SUBLANES):
            shift_ref[b - 1, 0:rows, :] = pad_ref[b:b + rows, :]
        for s in range(TM // seg):
            _conv_segment(pad_ref, shift_ref, conv_ref, wdw_ref, s * stride, s * seg, seg)

    @pl.when(i < NT_P)
    def _():
        conv_tile(SEQ)

    @pl.when(i >= NT_P)
    def _():
        conv_tile(GRID_W)

    yb = _layernorm(conv_ref[...] + bdw_ref[...], ncv_ref[...])
    cat_ref[:, A_WIDTH:] = _silu(yb).astype(BF16)
    out = _bdot(cat_ref[...], wout_ref[...])
    return x + mod_ref[2:3, :] * out


def _l0_producer(tile0, x_ctx, x_lat, mod, norm_mix, w_in, w_sg, b_sg, norm_sg, w_dw, b_dw, norm_cv, w_out):
    assert TM == SEQ and TM % GRID_W == 0 and TM % CHUNK_A == 0
    pad_rows = (TM // GRID_W) * (GRID_W + 2 * CONV_HALO)
    assert pad_rows >= SEQ + 2 * CONV_HALO
    operands = (x_ctx, x_lat, mod, norm_mix.reshape(1, -1), w_in.astype(BF16), w_sg.astype(BF16),
                b_sg.reshape(A_GROUPS, CHUNK_A, 1), norm_sg.reshape(1, -1), w_dw, b_dw.reshape(1, -1),
                norm_cv.reshape(1, -1), w_out.astype(BF16))
    specs = [
        pl.BlockSpec((TM, D_MODEL), lambda i: (jnp.minimum(i + tile0, NT_P - 1), 0)),
        pl.BlockSpec((TM, D_MODEL), lambda i: (jnp.maximum(i + tile0 - NT_P, 0), 0)),
        _mod_spec(tile0),
        _const_spec((1, D_MODEL)),
        _const_spec((D_MODEL, 2 * A_WIDTH + 2 * B_WIDTH)),
        _const_spec((A_GROUPS, CHUNK_A, CHUNK_A)),
        _const_spec((A_GROUPS, CHUNK_A, 1)),
        _const_spec((1, A_WIDTH)),
        _const_spec((CONV_W, B_WIDTH)),
        _const_spec((1, B_WIDTH)),
        _const_spec((1, B_WIDTH)),
        _const_spec((D_MODEL, D_MODEL)),
    ]
    scratch = [
        pltpu.VMEM((TM, D_MODEL), BF16),
        pltpu.VMEM((pad_rows, B_WIDTH), F32),
        pltpu.VMEM((SUBLANES - 1, pad_rows, B_WIDTH), F32),
        pltpu.VMEM((TM, B_WIDTH), F32),
    ]
    return _l0_tile, operands, specs, scratch


def _route(scores, biased):
    n = scores.shape[-1]
    shp = (N_GROUPS, GROUP_SIZE, n)
    s3 = scores.reshape(shp)
    b3 = biased.reshape(shp)
    m_iota = lax.broadcasted_iota(jnp.int32, shp, 1).astype(F32)
    g_iota = lax.broadcasted_iota(jnp.int32, shp, 0).astype(F32)
    e_iota = g_iota * GROUP_SIZE + m_iota
    neg = -jnp.inf

    def amax1(v):
        return jnp.max(v, axis=1, keepdims=True)

    def amin1(v):
        return jnp.min(v, axis=1, keepdims=True)

    m1 = amax1(b3)
    i1 = amin1(jnp.where(b3 == m1, m_iota, float(GROUP_SIZE)))
    m2 = amax1(jnp.where(m_iota == i1, neg, b3))
    grp = m1 + m2
    gi1 = lax.broadcasted_iota(jnp.int32, grp.shape, 0).astype(F32)
    gmask = jnp.zeros(grp.shape, jnp.bool_)
    for _ in range(TOPK_GROUPS):
        gm = jnp.max(grp, axis=0, keepdims=True)
        gi = jnp.min(jnp.where(grp == gm, gi1, float(N_GROUPS)), axis=0, keepdims=True)
        hit = gi1 == gi
        gmask = jnp.logical_or(gmask, hit)
        grp = jnp.where(hit, neg, grp)
    cand = jnp.where(gmask, b3, neg)
    ids, vals, hits = [], [], []
    for _ in range(TOP_K):
        mx = jnp.max(amax1(cand), axis=0, keepdims=True)
        ei = jnp.min(amin1(jnp.where(cand == mx, e_iota, float(N_EXPERTS))), axis=0, keepdims=True)
        hit = e_iota == ei
        ids.append(ei.reshape(1, n))
        vals.append(_pick(hit, s3))
        hits.append(hit)
        cand = jnp.where(hit, neg, cand)
    return jnp.concatenate(ids, axis=0).astype(jnp.int32), jnp.concatenate(vals, axis=0), hits


def _pick(hit, v3):
    s = jnp.sum(jnp.sum(jnp.where(hit, v3, 0.0), axis=1, keepdims=True), axis=0, keepdims=True)
    return s.reshape(1, v3.shape[-1])


def _moe_pre_body(*refs, tile_fn, n_tile_in, tile0):
    tile_in, refs = refs[:n_tile_in], refs[n_tile_in:]
    (mod_ref, nffn_ref, wrt_ref, br_ref, wsg_ref, wsu_ref, wsd_ref,
     h_ref, eidx_ref, w8_ref, rank_ref, cnt_ref, acc_ref, run_ref), tile_scratch = refs[:14], refs[14:]

    @pl.when(pl.program_id(0) == 0)
    def _():
        run_ref[...] = jnp.zeros(run_ref.shape, F32)

    x = tile_fn(pl.program_id(0) + tile0, *tile_in, *tile_scratch)
    h = _modulate(x, nffn_ref[...], mod_ref[3:4, :], mod_ref[4:5, :])
    h_ref[...] = _pack_pairs(h)
    hb = h.astype(BF16)
    h_lo = (h - hb.astype(F32)).astype(BF16)
    wr = wrt_ref[...]
    wr_hi = wr.astype(BF16)
    wr_lo = (wr - wr_hi.astype(F32)).astype(BF16)
    logits_t = _dot_nt(wr_hi, hb) + (_dot_nt(wr_hi, h_lo) + _dot_nt(wr_lo, hb))
    scores = jax.nn.sigmoid(logits_t)
    eidx, sv, hits = _route(scores, scores + br_ref[...])
    eidx_ref[...] = eidx
    w8_ref[...] = sv / jnp.sum(sv, axis=0, keepdims=True) * ROUTED_SCALE
    sel3 = hits[0]
    for hit in hits[1:]:
        sel3 = jnp.logical_or(sel3, hit)
    sel = sel3.astype(F32).reshape(N_EXPERTS, TM)
    earlier = (lax.broadcasted_iota(jnp.int32, (TM, TM), 0)
               < lax.broadcasted_iota(jnp.int32, (TM, TM), 1)).astype(BF16)
    rank3 = (_bdot(sel.astype(BF16), earlier) + run_ref[...]).reshape(N_GROUPS, GROUP_SIZE, TM)
    rank_ref[...] = jnp.concatenate([_pick(hit, rank3) for hit in hits], axis=0).astype(jnp.int32)
    run_ref[...] = run_ref[...] + jnp.sum(sel, axis=1, keepdims=True)
    cnt_ref[...] = run_ref[...].astype(jnp.int32)
    sh = _bdot((_silu(_bdot(hb, wsg_ref[...])) * _bdot(hb, wsu_ref[...])).astype(BF16), wsd_ref[...])
    acc_ref[...] = x + mod_ref[5:6, :] * sh


def _moe_pre(producer, mod, norm_ffn, w_router, b_router, w_sh_gate, w_sh_up, w_sh_down, tile0):
    tile_fn, tile_operands, tile_specs, tile_scratch = producer
    return pl.pallas_call(
        functools.partial(_moe_pre_body, tile_fn=tile_fn, n_tile_in=len(tile_operands), tile0=tile0),
        out_shape=(
            jax.ShapeDtypeStruct((T_PART, D_PACK), jnp.int32),
            jax.ShapeDtypeStruct((TOP_K, T_PART), jnp.int32),
            jax.ShapeDtypeStruct((TOP_K, T_PART), F32),
            jax.ShapeDtypeStruct((TOP_K, T_PART), jnp.int32),
            jax.ShapeDtypeStruct((N_EXPERTS, 1), jnp.int32),
            jax.ShapeDtypeStruct((T_PART, D_MODEL), F32),
        ),
        grid=(NT_PART,),
        in_specs=list(tile_specs) + [
            _mod_spec(tile0),
            _const_spec((1, D_MODEL)),
            _const_spec((N_EXPERTS, D_MODEL)),
            _const_spec((N_EXPERTS, 1)),
            _const_spec((D_MODEL, D_SHARED)),
            _const_spec((D_MODEL, D_SHARED)),
            _const_spec((D_SHARED, D_MODEL)),
        ],
        out_specs=(
            pl.BlockSpec((TM, D_PACK), lambda i: (i, 0)),
            pl.BlockSpec((TOP_K, TM), lambda i: (0, i)),
            pl.BlockSpec((TOP_K, TM), lambda i: (0, i)),
            pl.BlockSpec((TOP_K, TM), lambda i: (0, i)),
            _const_spec((N_EXPERTS, 1)),
            _tile_spec(),
        ),
        scratch_shapes=[pltpu.VMEM((N_EXPERTS, 1), F32)] + list(tile_scratch),
        compiler_params=_params("arbitrary"),
        name="mixer_router_shared",
    )(*tile_operands, mod, norm_ffn.reshape(1, -1), w_router.T, b_router.reshape(N_EXPERTS, 1),
      w_sh_gate.astype(BF16), w_sh_up.astype(BF16), w_sh_down.astype(BF16))


EXPERT_RING = 3


def _expert_body(be_ref, nv_ref, nu_ref, xs_hbm, wg_ref, wu_ref, wd_ref, ys_ref, wgu_s, wd_s, ring, sems):
    j = pl.program_id(0)
    n_used = nu_ref[0]
    live_tile = j < n_used

    def fetch(t):
        slot = t % EXPERT_RING
        return pltpu.make_async_copy(xs_hbm.at[pl.ds(t * TE, TE)], ring.at[slot], sems.at[slot])

    @pl.when(j == 0)
    def _():
        for t in range(EXPERT_RING - 1):
            @pl.when(t < n_used)
            def _():
                fetch(t).start()

    @pl.when(j + (EXPERT_RING - 1) < n_used)
    def _():
        fetch(j + (EXPERT_RING - 1)).start()

    @pl.when(jnp.logical_and(live_tile, jnp.logical_or(j == 0, be_ref[j] != be_ref[jnp.maximum(j - 1, 0)])))
    def _():
        wgu_s[:, :D_EXPERT] = wg_ref[...].astype(BF16)
        wgu_s[:, D_EXPERT:] = wu_ref[...].astype(BF16)
        wd_s[...] = wd_ref[...].astype(BF16)

    @pl.when(live_tile)
    def _():
        fetch(j).wait()
        live = lax.broadcasted_iota(jnp.int32, (TE, 1), 0) < nv_ref[j]
        lo, hi = _unpack_pairs(jnp.where(live, ring[j % EXPERT_RING], 0))
        xb = jnp.concatenate([lo.astype(BF16), hi.astype(BF16)], axis=1)
        hgu = _bdot(xb, wgu_s[...])
        hh = _silu(hgu[:, :D_EXPERT]) * hgu[:, D_EXPERT:]
        ys_ref[...] = _pack_pairs(_bdot(hh.astype(BF16), wd_s[...]))


def _experts(block_expert, block_rows, n_used, xs, w_gate, w_up, w_down, layer):
    def row_map(j, be, nv, nu):
        return (jnp.minimum(j, nu[0] - 1), 0)

    def w_map(j, be, nv, nu):
        return (layer, be[jnp.minimum(j, nu[0] - 1)], 0, 0)

    return pl.pallas_call(
        _expert_body,
        out_shape=jax.ShapeDtypeStruct((P_ROWS, D_PACK), jnp.int32),
        grid_spec=pltpu.PrefetchScalarGridSpec(
            num_scalar_prefetch=3,
            grid=(NB,),
            in_specs=[
                pl.BlockSpec(memory_space=pl.ANY),
                pl.BlockSpec((None, None, D_MODEL, D_EXPERT), w_map),
                pl.BlockSpec((None, None, D_MODEL, D_EXPERT), w_map),
                pl.BlockSpec((None, None, D_EXPERT, D_MODEL), w_map),
            ],
            out_specs=pl.BlockSpec((TE, D_PACK), row_map),
            scratch_shapes=[
                pltpu.VMEM((D_MODEL, 2 * D_EXPERT), BF16),
                pltpu.VMEM((D_EXPERT, D_MODEL), BF16),
                pltpu.VMEM((EXPERT_RING, TE, D_PACK), jnp.int32),
                pltpu.SemaphoreType.DMA((EXPERT_RING,)),
            ],
        ),
        compiler_params=_params("arbitrary"),
        name="moe_experts",
    )(block_expert, block_rows, n_used, xs, w_gate, w_up, w_down)


def _positions_body(start_ref, eidx_ref, rank_ref, pos_ref):
    eidx = eidx_ref[...]
    base = jnp.zeros(eidx.shape, jnp.int32)
    for e in range(N_EXPERTS):
        base = jnp.where(eidx == e, start_ref[e], base)
    pos_ref[...] = base * TE + rank_ref[...]


def _positions(blk_start, eidx_t, rank_t):
    full = pl.BlockSpec((TOP_K, T_PART), lambda i, s: (0, 0))
    return pl.pallas_call(
        _positions_body,
        out_shape=jax.ShapeDtypeStruct((TOP_K, T_PART), jnp.int32),
        grid_spec=pltpu.PrefetchScalarGridSpec(
            num_scalar_prefetch=1, grid=(1,), in_specs=[full, full], out_specs=full),
        compiler_params=_params("arbitrary"),
        name="moe_positions",
    )(blk_start, eidx_t, rank_t)


def _sc_mesh():
    return plsc.VectorSubcoreMesh(core_axis_name="c", subcore_axis_name="s")


def _sc_worker():
    return lax.axis_index("s") * SC_CORES + lax.axis_index("c")


def _sc_scatter_rows(h, pos_rows):
    c = SC_CHUNK
    n_chunks = T_PART // SC_WORKERS // c
    width = h.shape[1]

    @functools.partial(
        pl.kernel, mesh=_sc_mesh(),
        out_type=jax.ShapeDtypeStruct((P_ROWS, width), h.dtype),
        scratch_types=[pltpu.VMEM((n_chunks * TOP_K, c), jnp.int32),
                       pltpu.VMEM((c, width), h.dtype), pltpu.VMEM((c, width), h.dtype)]
        + [pltpu.SemaphoreType.DMA] * 4,
        name="moe_dispatch_scatter",
    )
    def scatter(h_hbm, pos_hbm, xs_hbm, idx_v, buf0, buf1, sem_in0, sem_in1, sem_out0, sem_out1):
        assert n_chunks % 2 == 0
        bufs, sem_in, sem_out = (buf0, buf1), (sem_in0, sem_in1), (sem_out0, sem_out1)
        first = _sc_worker() * n_chunks
        pltpu.sync_copy(pos_hbm.at[pl.ds(first * TOP_K, n_chunks * TOP_K)], idx_v)

        def load(i, b):
            return pltpu.make_async_copy(h_hbm.at[pl.ds((first + i) * c, c)], bufs[b], sem_in[b])

        def puts(i, b):
            return [pltpu.make_async_copy(bufs[b], xs_hbm.at[idx_v.at[i * TOP_K + k]], sem_out[b])
                    for k in range(TOP_K)]

        load(0, 0).start()
        load(1, 1).start()

        @pl.loop(0, n_chunks, step=2)
        def _(i):
            for b in range(2):
                load(i + b, b).wait()
                for cp in puts(i + b, b):
                    cp.start()
            for b in range(2):
                for cp in puts(i + b, b):
                    cp.wait()

                @pl.when(i + 2 + b < n_chunks)
                def _():
                    load(i + 2 + b, b).start()

    return scatter(h, pos_rows)


def _sc_combine(ys, pos_tk, wsplat):
    ct = COMBINE_TOKENS
    rows = ct * TOP_K
    tok_per_worker = T_PART // SC_WORKERS
    n_chunks = tok_per_worker // ct
    width = ys.shape[1]
    vmem = pltpu.VMEM

    @functools.partial(
        pl.kernel, mesh=_sc_mesh(),
        out_type=jax.ShapeDtypeStruct((T_PART, D_MODEL), F32),
        scratch_types=[vmem((tok_per_worker * TOP_K,), jnp.int32),
                       vmem((rows, width), ys.dtype), vmem((rows, width), ys.dtype),
                       vmem((ct, TOP_K * SC_LANES), F32), vmem((ct, TOP_K * SC_LANES), F32),
                       vmem((ct, D_MODEL), F32), vmem((ct, D_MODEL), F32)]
        + [pltpu.SemaphoreType.DMA] * 6,
        compiler_params=pltpu.CompilerParams(needs_layout_passes=False),
        name="moe_combine",
    )
    def combine(ys_hbm, pos_hbm, w_hbm, out_hbm, idx_v, g0, g1, w0, w1, o0, o1,
                sem_g0, sem_g1, sem_w0, sem_w1, sem_o0, sem_o1):
        assert n_chunks % 2 == 0
        gbuf, wbuf, obuf = (g0, g1), (w0, w1), (o0, o1)
        sem_g, sem_w, sem_o = (sem_g0, sem_g1), (sem_w0, sem_w1), (sem_o0, sem_o1)
        tok0 = _sc_worker() * tok_per_worker
        pltpu.sync_copy(pos_hbm.at[pl.ds(tok0 * TOP_K, tok_per_worker * TOP_K)], idx_v)

        def fetch(i, b):
            return [pltpu.make_async_copy(ys_hbm.at[idx_v.at[pl.ds(i * rows, rows)]], gbuf[b], sem_g[b]),
                    pltpu.make_async_copy(w_hbm.at[pl.ds(tok0 + i * ct, ct)], wbuf[b], sem_w[b])]

        def flush(i, b):
            return pltpu.make_async_copy(obuf[b], out_hbm.at[pl.ds(tok0 + i * ct, ct)], sem_o[b])

        def reduce_chunk(b):
            @pl.loop(0, ct)
            def _(t):
                wv = [wbuf[b][t, pl.ds(k * SC_LANES, SC_LANES)] for k in range(TOP_K)]

                @pl.loop(0, width // SC_LANES)
                def _(j):
                    col = j * SC_LANES
                    acc_lo = acc_hi = None
                    for k in range(TOP_K):
                        word = gbuf[b][t * TOP_K + k, pl.ds(col, SC_LANES)]
                        lo = lax.bitcast_convert_type(word << 16, F32)
                        hi = lax.bitcast_convert_type(word & jnp.int32(-65536), F32)
                        acc_lo = wv[k] * lo if k == 0 else acc_lo + wv[k] * lo
                        acc_hi = wv[k] * hi if k == 0 else acc_hi + wv[k] * hi
                    obuf[b][t, pl.ds(col, SC_LANES)] = acc_lo
                    obuf[b][t, pl.ds(D_PACK + col, SC_LANES)] = acc_hi

        for b in range(2):
            for cp in fetch(b, b):
                cp.start()

        @pl.loop(0, n_chunks, step=2)
        def _(i):
            for b in range(2):
                for cp in fetch(i + b, b):
                    cp.wait()

                @pl.when(i > 0)
                def _():
                    flush(i + b - 2, b).wait()

                reduce_chunk(b)
                flush(i + b, b).start()

                @pl.when(i + 2 + b < n_chunks)
                def _():
                    for cp in fetch(i + 2 + b, b):
                        cp.start()

        for b in range(2):
            flush(n_chunks - 2 + b, b).wait()

    return combine(ys, pos_tk, wsplat)


def _moe_routed(h, eidx_t, rank_t, counts, wsplat, w_gate, w_up, w_down, layer):
    counts = counts.reshape(1, N_EXPERTS)
    nblk = (counts + TE - 1) // TE
    blk_end = jnp.cumsum(nblk, axis=1)
    blk_start = blk_end - nblk
    blocks = jnp.arange(NB, dtype=jnp.int32).reshape(NB, 1)
    block_expert = jnp.minimum(jnp.sum(blocks >= blk_end, axis=1, keepdims=True), N_EXPERTS - 1)
    mine = block_expert == jnp.arange(N_EXPERTS, dtype=jnp.int32).reshape(1, N_EXPERTS)
    cnt_b = jnp.sum(jnp.where(mine, counts, 0), axis=1, keepdims=True)
    start_b = jnp.sum(jnp.where(mine, blk_start, 0), axis=1, keepdims=True)
    block_rows = jnp.clip(cnt_b - (blocks - start_b) * TE, 0, TE)
    n_used = blk_end[0, -1].reshape(1).astype(jnp.int32)
    pos = _positions(blk_start.reshape(N_EXPERTS).astype(jnp.int32), eidx_t, rank_t)
    pos_rows = pos.reshape(TOP_K, T_PART // SC_CHUNK, SC_CHUNK).transpose(1, 0, 2).reshape(-1, SC_CHUNK)
    xs = _sc_scatter_rows(h, pos_rows)
    ys = _experts(block_expert.reshape(NB).astype(jnp.int32), block_rows.reshape(NB).astype(jnp.int32),
                  n_used, xs, w_gate, w_up, w_down, layer)
    return _sc_combine(ys, pos.T.reshape(-1), wsplat)


N_SLABS = (3 * C_FDIM + 2 * D_MODEL) // 128


def _hgrn_in_body(acc_ref, rt_ref, mod0_ref, mod1_ref, nmix_ref, win_ref, *rest):
    x_ref, z_ref = rest[-2:]
    x = acc_ref[...] + mod0_ref[5:6, :] * rt_ref[...]
    x_ref[...] = x
    hb = _modulate(x, nmix_ref[...], mod1_ref[0:1, :], mod1_ref[1:2, :]).astype(BF16)
    for s in range(N_SLABS // C_HEADS):
        zz = _bdot(hb, win_ref[:, s * D_MODEL:(s + 1) * D_MODEL])
        for hh in range(C_HEADS):
            z_ref[s * C_HEADS + hh] = zz[:, hh * 128:(hh + 1) * 128]


def _hgrn_in(acc, routed, tile0, prev, mod0, mod1, norm_mix, w_in):
    def shifted_mod():
        return pl.BlockSpec((None, 6, D_MODEL), lambda i: (_mod_row(i + tile0), 0, 0))

    any_spec = pl.BlockSpec(memory_space=pl.ANY)
    prev = () if prev is None else tuple(prev)
    n_in = 6
    return pl.pallas_call(
        _hgrn_in_body,
        out_shape=(
            jax.ShapeDtypeStruct((T, D_MODEL), F32),
            jax.ShapeDtypeStruct((N_SLABS, T, 128), F32),
        ),
        grid=(NT_PART,),
        in_specs=[
            _tile_spec(), _tile_spec(), shifted_mod(), shifted_mod(),
            _const_spec((1, D_MODEL)),
            _const_spec((D_MODEL, 3 * C_FDIM + 2 * D_MODEL)),
        ] + [any_spec] * len(prev),
        out_specs=(
            pl.BlockSpec((TM, D_MODEL), lambda i: (i + tile0, 0)),
            pl.BlockSpec((N_SLABS, TM, 128), lambda i: (0, i + tile0, 0)),
        ),
        input_output_aliases={n_in + k: k for k in range(len(prev))},
        compiler_params=_params("parallel"),
        name="hgrn_in_proj",
    )(acc, routed, mod0, mod1, norm_mix.reshape(1, -1), w_in.astype(BF16), *prev)


def _gla_body(q_ref, f_ref, v_ref, lb_ref, s0_ref, o_ref, ns_ref, st_ref, *, rev):
    j = pl.program_id(0)
    ti = NT - 1 - j if rev else j
    is_ctx = ti < NT_P
    first_lat = (ti - NT_P) % TILES_PER_LAT == (TILES_PER_LAT - 1 if rev else 0)

    @pl.when(is_ctx)
    def _():
        st_ref[...] = jnp.zeros(st_ref.shape, F32)

    @pl.when(jnp.logical_and(jnp.logical_not(is_ctx), first_lat))
    def _():
        st_ref[...] = s0_ref[...]

    row = lax.broadcasted_iota(jnp.int32, (TM, TM), 0)
    col = lax.broadcasted_iota(jnp.int32, (TM, TM), 1)
    same_chunk = (row // SCAN_CHUNK) == (col // SCAN_CHUNK)
    seen = jnp.logical_and(same_chunk, (col >= row) if rev else (col <= row))
    cum_w = seen.astype(BF16)
    mid = SCAN_CHUNK // 2 if rev else SCAN_CHUNK // 2 - 1
    last = 0 if rev else SCAN_CHUNK - 1
    n_chunks = TM // SCAN_CHUNK
    order = range(n_chunks - 1, -1, -1) if rev else range(n_chunks)
    group = 4

    def chunk_rows(b, off):
        return jnp.concatenate(
            [jnp.broadcast_to(b[c * SCAN_CHUNK + off:c * SCAN_CHUNK + off + 1, :], (SCAN_CHUNK, b.shape[1]))
             for c in range(n_chunks)], axis=0)

    def head_group(gi, carry):
        heads = [gi * group + u for u in range(group)]
        qs, kk, vv, bcum = [], [], [], []
        for hd in heads:
            lb = lb_ref[hd]
            qs.append(_silu(q_ref[hd]) * (C_DK ** -0.5))
            fg = lb + (1.0 - lb) * jax.nn.sigmoid(f_ref[hd])
            kk.append(1.0 - fg)
            vv.append(v_ref[hd].astype(BF16))
            g = jnp.log(fg)
            g_hi = g.astype(BF16)
            r1 = g - g_hi.astype(F32)
            g_mid = r1.astype(BF16)
            g_lo = (r1 - g_mid.astype(F32)).astype(BF16)
            bcum.append(_bdot(cum_w, g_hi) + _bdot(cum_w, g_mid) + _bdot(cum_w, g_lo))
        o_intra, q_dec, kv, decay = [], [], [], []
        for u in range(group):
            b_mid = chunk_rows(bcum[u], mid)
            b_last = chunk_rows(bcum[u], last)
            qe = (qs[u] * jnp.exp(bcum[u] - b_mid)).astype(BF16)
            ke = (kk[u] * jnp.exp(b_mid - bcum[u])).astype(BF16)
            att = jnp.where(seen, _dot_nt(qe, ke), 0.0)
            o_intra.append(_bdot(att.astype(BF16), vv[u]))
            q_dec.append((qs[u] * jnp.exp(bcum[u])).astype(BF16))
            k_dec = (kk[u] * jnp.exp(b_last - bcum[u])).astype(BF16)
            kv.append([_dot_tn(vv[u][c * SCAN_CHUNK:(c + 1) * SCAN_CHUNK], k_dec[c * SCAN_CHUNK:(c + 1) * SCAN_CHUNK])
                       for c in range(n_chunks)])
            decay.append([jnp.exp(bcum[u][c * SCAN_CHUNK + last:c * SCAN_CHUNK + last + 1, :])
                          for c in range(n_chunks)])
        st = [st_ref[hd] for hd in heads]
        for c in order:
            sl = slice(c * SCAN_CHUNK, (c + 1) * SCAN_CHUNK)
            for u, hd in enumerate(heads):
                o_ref[hd, pl.ds(c * SCAN_CHUNK, SCAN_CHUNK), :] = (
                    o_intra[u][sl] + _dot_nt(q_dec[u][sl], st[u].astype(BF16)))
                st[u] = decay[u][c] * st[u] + kv[u][c]
        for u, hd in enumerate(heads):
            st_ref[hd] = st[u]
        return carry

    lax.fori_loop(0, C_HEADS // group, head_group, 0)

    @pl.when(is_ctx)
    def _():
        ns_ref[...] = st_ref[...]


def _gla(z3, lb_dir, s0t_dir, *, rev):
    def ti_of(j):
        return NT - 1 - j if rev else j

    f_slab = 2 if rev else 1

    def lat_map(j):
        return (jnp.clip((ti_of(j) - NT_P) // TILES_PER_LAT, 0, DEC_BATCH - 1), 0, 0, 0)

    return pl.pallas_call(
        functools.partial(_gla_body, rev=rev),
        out_shape=(
            jax.ShapeDtypeStruct((C_HEADS, T, C_DV), F32),
            jax.ShapeDtypeStruct((BATCH, C_HEADS, C_DV, C_DK), F32),
        ),
        grid=(NT,),
        in_specs=[
            pl.BlockSpec((C_HEADS, TM, 128), lambda j: (0, ti_of(j), 0)),
            pl.BlockSpec((C_HEADS, TM, 128), lambda j: (f_slab, ti_of(j), 0)),
            pl.BlockSpec((C_HEADS, TM, 128), lambda j: (3, ti_of(j), 0)),
            _const_spec((C_HEADS, 1, C_DK)),
            pl.BlockSpec((None, C_HEADS, C_DV, C_DK), lat_map),
        ],
        out_specs=(
            pl.BlockSpec((C_HEADS, TM, C_DV), lambda j: (0, ti_of(j), 0)),
            pl.BlockSpec((None, C_HEADS, C_DV, C_DK),
                         lambda j: (jnp.minimum(ti_of(j), NT_P - 1), 0, 0, 0)),
        ),
        scratch_shapes=[pltpu.VMEM((C_HEADS, C_DV, C_DK), F32)],
        compiler_params=_params("arbitrary"),
        name="gla_bwd" if rev else "gla_fwd",
    )(z3, z3, z3, lb_dir, s0t_dir)


def _hgrn_out_tile(i, ofw_ref, obw_ref, gate_ref, x_ref, mod_ref, no_ref, wout_ref, cat_ref):
    del i
    for hd in range(C_HEADS):
        o = ofw_ref[hd] + obw_ref[hd]
        cat_ref[:, hd * C_DV:(hd + 1) * C_DV] = (_rms(o, no_ref[...]) * _silu(gate_ref[hd])).astype(BF16)
    return x_ref[...] + mod_ref[2:3, :] * _bdot(cat_ref[...], wout_ref[...])


def _hgrn_out_producer(tile0, o_fw, o_bw, z3, x, mod, norm_o, w_out):
    head_spec = pl.BlockSpec((C_HEADS, TM, C_DV), lambda i: (0, i + tile0, 0))
    specs = [
        head_spec, head_spec,
        pl.BlockSpec((C_HEADS, TM, 128), lambda i: (4, i + tile0, 0)),
        pl.BlockSpec((TM, D_MODEL), lambda i: (i + tile0, 0)),
        _mod_spec(tile0),
        _const_spec((1, C_DV)),
        _const_spec((D_MODEL, D_MODEL)),
    ]
    operands = (o_fw, o_bw, z3, x, mod, norm_o.reshape(1, -1), w_out.astype(BF16))
    return _hgrn_out_tile, operands, specs, [pltpu.VMEM((TM, D_MODEL), BF16)]


def _final_body(acc_ref, rt_ref, mod_ref, nf_ref, *rest):
    o_ref = rest[-1]
    o_ref[...] = _rms(acc_ref[...] + mod_ref[5:6, :] * rt_ref[...], nf_ref[...])


def _final(acc, routed, mod, norm_final, part_tile0, local0, n_tiles, out_tile0, out_tiles, prev=None):
    local = pl.BlockSpec((TM, D_MODEL), lambda i: (i + local0, 0))
    prev = () if prev is None else (prev,)
    return pl.pallas_call(
        _final_body,
        out_shape=jax.ShapeDtypeStruct((out_tiles * TM, D_MODEL), F32),
        grid=(n_tiles,),
        in_specs=[
            local, local,
            pl.BlockSpec((None, 6, D_MODEL), lambda i: (_mod_row(i + local0 + part_tile0), 0, 0)),
            _const_spec((1, D_MODEL)),
        ] + [pl.BlockSpec(memory_space=pl.ANY)] * len(prev),
        out_specs=pl.BlockSpec((TM, D_MODEL), lambda i: (i + out_tile0, 0)),
        input_output_aliases={4 + k: 0 for k in range(len(prev))},
        compiler_params=_params("parallel"),
        name="final_norm",
    )(acc, routed, mod, norm_final.reshape(1, -1), *prev)


def kernel(x_prompt, x_sample, state_hgrn, c, c_ctx, w_ada, b_ada, norm_mix, norm_ffn, w_out, w_in_ab, w_sg, b_sg, norm_sg, w_dw, b_dw, norm_cv, w_in_hgrn, lb_raw, norm_o, w_router, b_router, w_gate, w_up, w_down, w_sh_gate, w_sh_up, w_sh_down, norm_final):
    cvecs = jnp.concatenate(
        [c_ctx.reshape(1, D_MODEL), c, jnp.zeros((N_MOD_ROWS - 1 - DEC_BATCH, D_MODEL), F32)], axis=0)
    mods = _ada_tables(cvecs, w_ada, b_ada)
    lb_sm = jax.nn.softmax(lb_raw.astype(F32), axis=0)
    lb1 = (jnp.cumsum(lb_sm, axis=0) - lb_sm[0])[1].reshape(2, C_HEADS, 1, C_DK)

    def moe(l, producer_of):
        parts = []
        for p in range(MOE_PARTS):
            h, eidx_t, w8_t, rank_t, counts, acc = _moe_pre(
                producer_of(p * NT_PART), mods[l], norm_ffn[l], w_router[l], b_router[l],
                w_sh_gate[l], w_sh_up[l], w_sh_down[l], p * NT_PART)
            wsplat = jnp.repeat(w8_t.T, SC_LANES, axis=1)
            parts += [acc, _moe_routed(h, eidx_t, rank_t, counts, wsplat, w_gate, w_up, w_down, l)]
        return parts

    parts = moe(0, lambda tile0: _l0_producer(
        tile0, x_prompt.reshape(T_P, D_MODEL), x_sample.reshape(T_S, D_MODEL), mods[0], norm_mix[0], w_in_ab[0],
        w_sg[0], b_sg[0], norm_sg[0], w_dw[0], b_dw[0], norm_cv[0], w_out[0]))
    xz = None
    for p in range(MOE_PARTS):
        xz = _hgrn_in(parts[2 * p], parts[2 * p + 1], p * NT_PART, xz, mods[0], mods[1], norm_mix[1], w_in_hgrn[0])
    x, z3 = xz
    s0t = jnp.swapaxes(state_hgrn[:, 0].astype(F32), -1, -2)
    o_fw, ns_fw = _gla(z3, lb1[0], s0t[:, 0], rev=False)
    o_bw, ns_bw = _gla(z3, lb1[1], s0t[:, 1], rev=True)
    parts = moe(1, lambda tile0: _hgrn_out_producer(tile0, o_fw, o_bw, z3, x, mods[1], norm_o[0], w_out[1]))
    y_p = y_s = None
    for p in range(MOE_PARTS):
        lo, hi = p * NT_PART, (p + 1) * NT_PART
        if lo < NT_P:
            n = min(hi, NT_P) - lo
            y_p = _final(parts[2 * p], parts[2 * p + 1], mods[1], norm_final, lo, 0, n, lo, NT_P, y_p)
        if hi > NT_P:
            first = max(lo, NT_P)
            y_s = _final(parts[2 * p], parts[2 * p + 1], mods[1], norm_final, lo, first - lo, hi - first,
                         first - NT_P, NT_S, y_s)
    y_p = y_p.reshape(BATCH, SEQ, D_MODEL)
    y_s = y_s.reshape(DEC_BATCH, DEC_SEQ, D_MODEL)
    new_state = jnp.swapaxes(jnp.stack([ns_fw, ns_bw], axis=1), -1, -2)[:, None]
    return (y_p, y_s, new_state)
```

```python
import functools

import jax
import jax.numpy as jnp
from jax import lax
from jax.experimental import pallas as pl
from jax.experimental.pallas import tpu as pltpu
from jax.experimental.pallas import tpu_sc as plsc

F32 = jnp.float32
BF16 = jnp.bfloat16
HIGHEST = lax.Precision.HIGHEST

D_MODEL = 1024
BATCH = 32
SEQ = 256
DEPTH = 2
DEC_BATCH = 8
DEC_SEQ = 2048
GRID_W = 64
A_WIDTH = D_MODEL // 2
A_GROUPS = 4
A_GC = A_WIDTH // A_GROUPS
CHUNK_A = 128
B_WIDTH = D_MODEL - A_WIDTH
CONV_W = 31
CONV_PAD = CONV_W // 2
C_HEADS = 8
C_DK = 128
C_DV = D_MODEL // C_HEADS
C_FDIM = C_HEADS * C_DK
SCAN_CHUNK = 64
N_EXPERTS = 64
TOP_K = 8
N_GROUPS = 8
GROUP_SIZE = N_EXPERTS // N_GROUPS
TOPK_GROUPS = 4
D_EXPERT = 256
D_SHARED = 256
ROUTED_SCALE = 2.5
EPS = 1e-6

TM = 256
T_P = BATCH * SEQ
T_S = DEC_BATCH * DEC_SEQ
T = T_P + T_S
NT_P = T_P // TM
NT_S = T_S // TM
NT = NT_P + NT_S
TILES_PER_LAT = DEC_SEQ // TM
TE = 512
MOE_PARTS = 2
NT_PART = NT // MOE_PARTS
T_PART = NT_PART * TM
NB = T_PART * TOP_K // TE + N_EXPERTS
P_ROWS = NB * TE
D_PACK = D_MODEL // 2
N_MOD_ROWS = 16
CONV_HALO = 16
VMEM_LIMIT = 48 * 1024 * 1024
SC_CORES = 2
SC_SUBCORES = 16
SC_WORKERS = SC_CORES * SC_SUBCORES
SC_LANES = 16
SC_CHUNK = 64
COMBINE_TOKENS = SC_CHUNK // TOP_K


def _mod_row(i):
    return jnp.where(i < NT_P, 0, 1 + (i - NT_P) // TILES_PER_LAT)


def _silu(x):
    return x * jax.nn.sigmoid(x)


def _gelu(x):
    return x * (0.5 * (1.0 + jnp.tanh(0.7978845608028654 * (x + 0.044715 * (x * x * x)))))


def _rms(x, g):
    return x * lax.rsqrt(jnp.mean(x * x, axis=-1, keepdims=True) + EPS) * g


def _layernorm(x, g):
    xc = x - jnp.mean(x, axis=-1, keepdims=True)
    return xc * lax.rsqrt(jnp.mean(xc * xc, axis=-1, keepdims=True) + EPS) * g


def _modulate(x, g, shift, scale):
    return _rms(x, g) * (1.0 + scale) + shift


def _bdot(a, b):
    return jnp.dot(a, b, preferred_element_type=F32)


def _dot_nt(a, b, precision=None):
    return lax.dot_general(a, b, (((1,), (1,)), ((), ())), precision=precision,
                           preferred_element_type=F32)


def _dot_tn(a, b):
    return lax.dot_general(a, b, (((0,), (0,)), ((), ())), preferred_element_type=F32)


def _pack_pairs(x):
    m = x.shape[1] // 2
    lo = lax.bitcast_convert_type(x[:, :m].astype(BF16).astype(F32), jnp.uint32)
    hi = lax.bitcast_convert_type(x[:, m:].astype(BF16).astype(F32), jnp.uint32)
    return lax.bitcast_convert_type(hi | (lo >> 16), jnp.int32)


def _unpack_pairs(w):
    u = lax.bitcast_convert_type(w, jnp.uint32)
    lo = lax.bitcast_convert_type(u << 16, F32)
    hi = lax.bitcast_convert_type(u & jnp.uint32(0xFFFF0000), F32)
    return lo, hi


def _params(*sem):
    return pltpu.CompilerParams(dimension_semantics=sem, vmem_limit_bytes=VMEM_LIMIT)


def _const_spec(shape):
    nd = len(shape)
    return pl.BlockSpec(shape, lambda *_: (0,) * nd)


def _ada_body(c_ref, w_ref, b_ref, o_ref):
    s = _silu(c_ref[...])
    o_ref[...] = jnp.dot(s, w_ref[...], precision=HIGHEST, preferred_element_type=F32) + b_ref[...]


def _ada_tables(cvecs, w_ada, b_ada):
    out = pl.pallas_call(
        _ada_body,
        out_shape=jax.ShapeDtypeStruct((DEPTH, N_MOD_ROWS, 6 * D_MODEL), F32),
        grid=(DEPTH, 6),
        in_specs=[
            _const_spec((N_MOD_ROWS, D_MODEL)),
            pl.BlockSpec((None, D_MODEL, D_MODEL), lambda l, j: (l, 0, j)),
            pl.BlockSpec((None, 1, D_MODEL), lambda l, j: (l, 0, j)),
        ],
        out_specs=pl.BlockSpec((None, N_MOD_ROWS, D_MODEL), lambda l, j: (l, 0, j)),
        compiler_params=_params("parallel", "parallel"),
        name="ada_tables",
    )(cvecs, w_ada, b_ada.reshape(DEPTH, 1, 6 * D_MODEL))
    return out.reshape(DEPTH, N_MOD_ROWS, 6, D_MODEL)


def _mod_spec():
    return pl.BlockSpec((None, 6, D_MODEL), lambda i: (_mod_row(i), 0, 0))


def _tile_spec():
    return pl.BlockSpec((TM, D_MODEL), lambda i: (i, 0))


SUBLANES = 8


def _conv_segment(pad_ref, shift_ref, conv_ref, wdw_ref, pad_base, out_base, seg):
    rb = min(seg, 64)
    for cb in range(B_WIDTH // 128):
        cs = slice(cb * 128, (cb + 1) * 128)
        for r0 in range(0, seg, rb):
            acc = jnp.zeros((rb, 128), F32)
            for k in range(CONV_W):
                b = (CONV_HALO - CONV_PAD + k) % SUBLANES
                off = pad_base + r0 + CONV_HALO - CONV_PAD + k - b
                src = pad_ref if b == 0 else shift_ref.at[b - 1]
                acc = acc + wdw_ref[k:k + 1, cs] * src[off:off + rb, cs]
            conv_ref[out_base + r0:out_base + r0 + rb, cs] = acc


def _l0_body(xc_ref, xl_ref, mod_ref, nmix_ref, win_ref, wsg_ref, bsg_ref, nsg_ref, wdw_ref, bdw_ref,
             ncv_ref, wout_ref, o_ref, cat_ref, pad_ref, shift_ref, conv_ref):
    i = pl.program_id(0)
    x = jnp.where(i < NT_P, xc_ref[...], xl_ref[...])
    h = _modulate(x, nmix_ref[...], mod_ref[0:1, :], mod_ref[1:2, :])
    z = _bdot(h.astype(BF16), win_ref[...])
    u = _gelu(z[:, :A_WIDTH])
    vb = _layernorm(_gelu(z[:, A_WIDTH:2 * A_WIDTH]), nsg_ref[...]).astype(BF16)
    for n in range(TM // CHUNK_A):
        rs = slice(n * CHUNK_A, (n + 1) * CHUNK_A)
        for g in range(A_GROUPS):
            cs = slice(g * A_GC, (g + 1) * A_GC)
            m = _bdot(wsg_ref[g], vb[rs, cs]) + bsg_ref[g]
            cat_ref[rs, cs] = (u[rs, cs] * m).astype(BF16)
    hb = z[:, 2 * A_WIDTH:2 * A_WIDTH + B_WIDTH] * jax.nn.sigmoid(z[:, 2 * A_WIDTH + B_WIDTH:])

    def conv_tile(seg):
        stride = seg + 2 * CONV_HALO
        halo = jnp.zeros((CONV_HALO, B_WIDTH), F32)
        for s in range(TM // seg):
            b = s * stride
            pad_ref[b:b + CONV_HALO, :] = halo
            pad_ref[b + CONV_HALO:b + CONV_HALO + seg, :] = hb[s * seg:(s + 1) * seg, :]
            pad_ref[b + CONV_HALO + seg:b + stride, :] = halo
        rows = (TM // seg) * stride - SUBLANES
        for b in range(1, SUBLANES):
            shift_ref[b - 1, 0:rows, :] = pad_ref[b:b + rows, :]
        for s in range(TM // seg):
            _conv_segment(pad_ref, shift_ref, conv_ref, wdw_ref, s * stride, s * seg, seg)

    @pl.when(i < NT_P)
    def _():
        conv_tile(SEQ)

    @pl.when(i >= NT_P)
    def _():
        conv_tile(GRID_W)

    yb = _layernorm(conv_ref[...] + bdw_ref[...], ncv_ref[...])
    cat_ref[:, A_WIDTH:] = _silu(yb).astype(BF16)
    out = _bdot(cat_ref[...], wout_ref[...])
    o_ref[...] = x + mod_ref[2:3, :] * out


def _l0_mixer(x_ctx, x_lat, mod, norm_mix, w_in, w_sg, b_sg, norm_sg, w_dw, b_dw, norm_cv, w_out):
    assert TM == SEQ and TM % GRID_W == 0 and TM % CHUNK_A == 0
    pad_rows = (TM // GRID_W) * (GRID_W + 2 * CONV_HALO)
    assert pad_rows >= SEQ + 2 * CONV_HALO
    return pl.pallas_call(
        _l0_body,
        out_shape=jax.ShapeDtypeStruct((T, D_MODEL), F32),
        grid=(NT,),
        in_specs=[
            pl.BlockSpec((TM, D_MODEL), lambda i: (jnp.minimum(i, NT_P - 1), 0)),
            pl.BlockSpec((TM, D_MODEL), lambda i: (jnp.maximum(i - NT_P, 0), 0)),
            _mod_spec(),
            _const_spec((1, D_MODEL)),
            _const_spec((D_MODEL, 2 * A_WIDTH + 2 * B_WIDTH)),
            _const_spec((A_GROUPS, CHUNK_A, CHUNK_A)),
            _const_spec((A_GROUPS, CHUNK_A, 1)),
            _const_spec((1, A_WIDTH)),
            _const_spec((CONV_W, B_WIDTH)),
            _const_spec((1, B_WIDTH)),
            _const_spec((1, B_WIDTH)),
            _const_spec((D_MODEL, D_MODEL)),
        ],
        out_specs=_tile_spec(),
        scratch_shapes=[
            pltpu.VMEM((TM, D_MODEL), BF16),
            pltpu.VMEM((pad_rows, B_WIDTH), F32),
            pltpu.VMEM((SUBLANES - 1, pad_rows, B_WIDTH), F32),
            pltpu.VMEM((TM, B_WIDTH), F32),
        ],
        compiler_params=_params("parallel"),
        name="l0_mixer",
    )(x_ctx, x_lat, mod, norm_mix.reshape(1, -1), w_in.astype(BF16), w_sg.astype(BF16),
      b_sg.reshape(A_GROUPS, CHUNK_A, 1), norm_sg.reshape(1, -1), w_dw, b_dw.reshape(1, -1),
      norm_cv.reshape(1, -1), w_out.astype(BF16))


def _route(scores, biased):
    n = scores.shape[-1]
    shp = (N_GROUPS, GROUP_SIZE, n)
    s3 = scores.reshape(shp)
    b3 = biased.reshape(shp)
    m_iota = lax.broadcasted_iota(jnp.int32, shp, 1).astype(F32)
    g_iota = lax.broadcasted_iota(jnp.int32, shp, 0).astype(F32)
    e_iota = g_iota * GROUP_SIZE + m_iota
    neg = -jnp.inf

    def amax1(v):
        return jnp.max(v, axis=1, keepdims=True)

    def amin1(v):
        return jnp.min(v, axis=1, keepdims=True)

    m1 = amax1(b3)
    i1 = amin1(jnp.where(b3 == m1, m_iota, float(GROUP_SIZE)))
    m2 = amax1(jnp.where(m_iota == i1, neg, b3))
    grp = m1 + m2
    gi1 = lax.broadcasted_iota(jnp.int32, grp.shape, 0).astype(F32)
    gmask = jnp.zeros(grp.shape, jnp.bool_)
    for _ in range(TOPK_GROUPS):
        gm = jnp.max(grp, axis=0, keepdims=True)
        gi = jnp.min(jnp.where(grp == gm, gi1, float(N_GROUPS)), axis=0, keepdims=True)
        hit = gi1 == gi
        gmask = jnp.logical_or(gmask, hit)
        grp = jnp.where(hit, neg, grp)
    cand = jnp.where(gmask, b3, neg)
    ids, vals, hits = [], [], []
    for _ in range(TOP_K):
        mx = jnp.max(amax1(cand), axis=0, keepdims=True)
        ei = jnp.min(amin1(jnp.where(cand == mx, e_iota, float(N_EXPERTS))), axis=0, keepdims=True)
        hit = e_iota == ei
        ids.append(ei.reshape(1, n))
        vals.append(_pick(hit, s3))
        hits.append(hit)
        cand = jnp.where(hit, neg, cand)
    return jnp.concatenate(ids, axis=0).astype(jnp.int32), jnp.concatenate(vals, axis=0), hits


def _pick(hit, v3):
    s = jnp.sum(jnp.sum(jnp.where(hit, v3, 0.0), axis=1, keepdims=True), axis=0, keepdims=True)
    return s.reshape(1, v3.shape[-1])


def _moe_pre_body(x_ref, mod_ref, nffn_ref, wrt_ref, br_ref, wsg_ref, wsu_ref, wsd_ref,
                  h_ref, eidx_ref, w8_ref, rank_ref, cnt_ref, acc_ref, run_ref):
    @pl.when(pl.program_id(0) == 0)
    def _():
        run_ref[...] = jnp.zeros(run_ref.shape, F32)

    x = x_ref[...]
    h = _modulate(x, nffn_ref[...], mod_ref[3:4, :], mod_ref[4:5, :])
    h_ref[...] = _pack_pairs(h)
    hb = h.astype(BF16)
    h_lo = (h - hb.astype(F32)).astype(BF16)
    wr = wrt_ref[...]
    wr_hi = wr.astype(BF16)
    wr_lo = (wr - wr_hi.astype(F32)).astype(BF16)
    logits_t = _dot_nt(wr_hi, hb) + (_dot_nt(wr_hi, h_lo) + _dot_nt(wr_lo, hb))
    scores = jax.nn.sigmoid(logits_t)
    eidx, sv, hits = _route(scores, scores + br_ref[...])
    eidx_ref[...] = eidx
    w8_ref[...] = sv / jnp.sum(sv, axis=0, keepdims=True) * ROUTED_SCALE
    sel3 = hits[0]
    for hit in hits[1:]:
        sel3 = jnp.logical_or(sel3, hit)
    sel = sel3.astype(F32).reshape(N_EXPERTS, TM)
    earlier = (lax.broadcasted_iota(jnp.int32, (TM, TM), 0)
               < lax.broadcasted_iota(jnp.int32, (TM, TM), 1)).astype(BF16)
    rank3 = (_bdot(sel.astype(BF16), earlier) + run_ref[...]).reshape(N_GROUPS, GROUP_SIZE, TM)
    rank_ref[...] = jnp.concatenate([_pick(hit, rank3) for hit in hits], axis=0).astype(jnp.int32)
    run_ref[...] = run_ref[...] + jnp.sum(sel, axis=1, keepdims=True)
    cnt_ref[...] = run_ref[...].astype(jnp.int32)
    sh = _bdot((_silu(_bdot(hb, wsg_ref[...])) * _bdot(hb, wsu_ref[...])).astype(BF16), wsd_ref[...])
    acc_ref[...] = x + mod_ref[5:6, :] * sh


def _moe_pre(x, mod, norm_ffn, w_router, b_router, w_sh_gate, w_sh_up, w_sh_down, tile0):
    return pl.pallas_call(
        _moe_pre_body,
        out_shape=(
            jax.ShapeDtypeStruct((T_PART, D_PACK), jnp.int32),
            jax.ShapeDtypeStruct((TOP_K, T_PART), jnp.int32),
            jax.ShapeDtypeStruct((TOP_K, T_PART), F32),
            jax.ShapeDtypeStruct((TOP_K, T_PART), jnp.int32),
            jax.ShapeDtypeStruct((N_EXPERTS, 1), jnp.int32),
            jax.ShapeDtypeStruct((T_PART, D_MODEL), F32),
        ),
        grid=(NT_PART,),
        in_specs=[
            pl.BlockSpec((TM, D_MODEL), lambda i: (i + tile0, 0)),
            pl.BlockSpec((None, 6, D_MODEL), lambda i: (_mod_row(i + tile0), 0, 0)),
            _const_spec((1, D_MODEL)),
            _const_spec((N_EXPERTS, D_MODEL)),
            _const_spec((N_EXPERTS, 1)),
            _const_spec((D_MODEL, D_SHARED)),
            _const_spec((D_MODEL, D_SHARED)),
            _const_spec((D_SHARED, D_MODEL)),
        ],
        out_specs=(
            pl.BlockSpec((TM, D_PACK), lambda i: (i, 0)),
            pl.BlockSpec((TOP_K, TM), lambda i: (0, i)),
            pl.BlockSpec((TOP_K, TM), lambda i: (0, i)),
            pl.BlockSpec((TOP_K, TM), lambda i: (0, i)),
            _const_spec((N_EXPERTS, 1)),
            _tile_spec(),
        ),
        scratch_shapes=[pltpu.VMEM((N_EXPERTS, 1), F32)],
        compiler_params=_params("arbitrary"),
        name="moe_router_shared",
    )(x, mod, norm_ffn.reshape(1, -1), w_router.T, b_router.reshape(N_EXPERTS, 1),
      w_sh_gate.astype(BF16), w_sh_up.astype(BF16), w_sh_down.astype(BF16))


EXPERT_RING = 3


WEIGHT_SLOTS = 2


def _expert_body(be_ref, nv_ref, nu_ref, run_ref, rexp_ref, nrun_ref, xs_hbm, wg_hbm, wu_hbm, wd_hbm, ys_ref,
                 wgu_s, wd_s, ring, sems, wg_buf, wu_buf, wd_buf, wsems, *, layer):
    j = pl.program_id(0)
    n_used = nu_ref[0]
    n_runs = nrun_ref[0]
    live_tile = j < n_used

    def fetch(t):
        slot = t % EXPERT_RING
        return pltpu.make_async_copy(xs_hbm.at[pl.ds(t * TE, TE)], ring.at[slot], sems.at[slot])

    def wfetch(r):
        slot = r % WEIGHT_SLOTS
        e = rexp_ref[r]
        return [pltpu.make_async_copy(wg_hbm.at[layer, e], wg_buf.at[slot], wsems.at[slot, 0]),
                pltpu.make_async_copy(wu_hbm.at[layer, e], wu_buf.at[slot], wsems.at[slot, 1]),
                pltpu.make_async_copy(wd_hbm.at[layer, e], wd_buf.at[slot], wsems.at[slot, 2])]

    @pl.when(j == 0)
    def _():
        for t in range(EXPERT_RING - 1):
            @pl.when(t < n_used)
            def _():
                fetch(t).start()
        for r in range(WEIGHT_SLOTS):
            @pl.when(r < n_runs)
            def _():
                for cp in wfetch(r):
                    cp.start()

    @pl.when(j + (EXPERT_RING - 1) < n_used)
    def _():
        fetch(j + (EXPERT_RING - 1)).start()

    @pl.when(jnp.logical_and(live_tile, jnp.logical_or(j == 0, be_ref[j] != be_ref[jnp.maximum(j - 1, 0)])))
    def _():
        r = run_ref[j]
        slot = r % WEIGHT_SLOTS
        for cp in wfetch(r):
            cp.wait()
        wgu_s[:, :D_EXPERT] = wg_buf[slot].astype(BF16)
        wgu_s[:, D_EXPERT:] = wu_buf[slot].astype(BF16)
        wd_s[...] = wd_buf[slot].astype(BF16)

        @pl.when(r + WEIGHT_SLOTS < n_runs)
        def _():
            for cp in wfetch(r + WEIGHT_SLOTS):
                cp.start()

    @pl.when(live_tile)
    def _():
        fetch(j).wait()
        live = lax.broadcasted_iota(jnp.int32, (TE, 1), 0) < nv_ref[j]
        lo, hi = _unpack_pairs(jnp.where(live, ring[j % EXPERT_RING], 0))
        xb = jnp.concatenate([lo.astype(BF16), hi.astype(BF16)], axis=1)
        hgu = _bdot(xb, wgu_s[...])
        hh = _silu(hgu[:, :D_EXPERT]) * hgu[:, D_EXPERT:]
        ys_ref[...] = _pack_pairs(_bdot(hh.astype(BF16), wd_s[...]))


def _experts(block_expert, block_rows, n_used, block_run, run_expert, n_runs, xs, w_gate, w_up, w_down, layer):
    def row_map(j, be, nv, nu, run, rexp, nrun):
        return (jnp.minimum(j, nu[0] - 1), 0)

    any_spec = pl.BlockSpec(memory_space=pl.ANY)
    return pl.pallas_call(
        functools.partial(_expert_body, layer=layer),
        out_shape=jax.ShapeDtypeStruct((P_ROWS, D_PACK), jnp.int32),
        grid_spec=pltpu.PrefetchScalarGridSpec(
            num_scalar_prefetch=6,
            grid=(NB,),
            in_specs=[any_spec, any_spec, any_spec, any_spec],
            out_specs=pl.BlockSpec((TE, D_PACK), row_map),
            scratch_shapes=[
                pltpu.VMEM((D_MODEL, 2 * D_EXPERT), BF16),
                pltpu.VMEM((D_EXPERT, D_MODEL), BF16),
                pltpu.VMEM((EXPERT_RING, TE, D_PACK), jnp.int32),
                pltpu.SemaphoreType.DMA((EXPERT_RING,)),
                pltpu.VMEM((WEIGHT_SLOTS, D_MODEL, D_EXPERT), F32),
                pltpu.VMEM((WEIGHT_SLOTS, D_MODEL, D_EXPERT), F32),
                pltpu.VMEM((WEIGHT_SLOTS, D_EXPERT, D_MODEL), F32),
                pltpu.SemaphoreType.DMA((WEIGHT_SLOTS, 3)),
            ],
        ),
        compiler_params=_params("arbitrary"),
        name="moe_experts",
    )(block_expert, block_rows, n_used, block_run, run_expert, n_runs, xs, w_gate, w_up, w_down)


def _positions_body(start_ref, eidx_ref, rank_ref, pos_ref):
    eidx = eidx_ref[...]
    base = jnp.zeros(eidx.shape, jnp.int32)
    for e in range(N_EXPERTS):
        base = jnp.where(eidx == e, start_ref[e], base)
    pos_ref[...] = base * TE + rank_ref[...]


def _positions(blk_start, eidx_t, rank_t):
    full = pl.BlockSpec((TOP_K, T_PART), lambda i, s: (0, 0))
    return pl.pallas_call(
        _positions_body,
        out_shape=jax.ShapeDtypeStruct((TOP_K, T_PART), jnp.int32),
        grid_spec=pltpu.PrefetchScalarGridSpec(
            num_scalar_prefetch=1, grid=(1,), in_specs=[full, full], out_specs=full),
        compiler_params=_params("arbitrary"),
        name="moe_positions",
    )(blk_start, eidx_t, rank_t)


def _sc_mesh():
    return plsc.VectorSubcoreMesh(core_axis_name="c", subcore_axis_name="s")


def _sc_worker():
    return lax.axis_index("s") * SC_CORES + lax.axis_index("c")


def _sc_scatter_rows(h, pos_rows):
    c = SC_CHUNK
    n_chunks = T_PART // SC_WORKERS // c
    width = h.shape[1]

    @functools.partial(
        pl.kernel, mesh=_sc_mesh(),
        out_type=jax.ShapeDtypeStruct((P_ROWS, width), h.dtype),
        scratch_types=[pltpu.VMEM((n_chunks * TOP_K, c), jnp.int32),
                       pltpu.VMEM((c, width), h.dtype), pltpu.VMEM((c, width), h.dtype)]
        + [pltpu.SemaphoreType.DMA] * 4,
        name="moe_dispatch_scatter",
    )
    def scatter(h_hbm, pos_hbm, xs_hbm, idx_v, buf0, buf1, sem_in0, sem_in1, sem_out0, sem_out1):
        assert n_chunks % 2 == 0
        bufs, sem_in, sem_out = (buf0, buf1), (sem_in0, sem_in1), (sem_out0, sem_out1)
        first = _sc_worker() * n_chunks
        pltpu.sync_copy(pos_hbm.at[pl.ds(first * TOP_K, n_chunks * TOP_K)], idx_v)

        def load(i, b):
            return pltpu.make_async_copy(h_hbm.at[pl.ds((first + i) * c, c)], bufs[b], sem_in[b])

        def puts(i, b):
            return [pltpu.make_async_copy(bufs[b], xs_hbm.at[idx_v.at[i * TOP_K + k]], sem_out[b])
                    for k in range(TOP_K)]

        load(0, 0).start()
        load(1, 1).start()

        @pl.loop(0, n_chunks, step=2)
        def _(i):
            for b in range(2):
                load(i + b, b).wait()
                for cp in puts(i + b, b):
                    cp.start()
            for b in range(2):
                for cp in puts(i + b, b):
                    cp.wait()

                @pl.when(i + 2 + b < n_chunks)
                def _():
                    load(i + 2 + b, b).start()

    return scatter(h, pos_rows)


def _sc_combine(ys, pos_tk, wsplat):
    ct = COMBINE_TOKENS
    rows = ct * TOP_K
    tok_per_worker = T_PART // SC_WORKERS
    n_chunks = tok_per_worker // ct
    width = ys.shape[1]
    vmem = pltpu.VMEM

    @functools.partial(
        pl.kernel, mesh=_sc_mesh(),
        out_type=jax.ShapeDtypeStruct((T_PART, D_MODEL), F32),
        scratch_types=[vmem((tok_per_worker * TOP_K,), jnp.int32),
                       vmem((rows, width), ys.dtype), vmem((rows, width), ys.dtype),
                       vmem((ct, TOP_K * SC_LANES), F32), vmem((ct, TOP_K * SC_LANES), F32),
                       vmem((ct, D_MODEL), F32), vmem((ct, D_MODEL), F32)]
        + [pltpu.SemaphoreType.DMA] * 6,
        compiler_params=pltpu.CompilerParams(needs_layout_passes=False),
        name="moe_combine",
    )
    def combine(ys_hbm, pos_hbm, w_hbm, out_hbm, idx_v, g0, g1, w0, w1, o0, o1,
                sem_g0, sem_g1, sem_w0, sem_w1, sem_o0, sem_o1):
        assert n_chunks % 2 == 0
        gbuf, wbuf, obuf = (g0, g1), (w0, w1), (o0, o1)
        sem_g, sem_w, sem_o = (sem_g0, sem_g1), (sem_w0, sem_w1), (sem_o0, sem_o1)
        tok0 = _sc_worker() * tok_per_worker
        pltpu.sync_copy(pos_hbm.at[pl.ds(tok0 * TOP_K, tok_per_worker * TOP_K)], idx_v)

        def fetch(i, b):
            return [pltpu.make_async_copy(ys_hbm.at[idx_v.at[pl.ds(i * rows, rows)]], gbuf[b], sem_g[b]),
                    pltpu.make_async_copy(w_hbm.at[pl.ds(tok0 + i * ct, ct)], wbuf[b], sem_w[b])]

        def flush(i, b):
            return pltpu.make_async_copy(obuf[b], out_hbm.at[pl.ds(tok0 + i * ct, ct)], sem_o[b])

        def reduce_chunk(b):
            @pl.loop(0, ct)
            def _(t):
                wv = [wbuf[b][t, pl.ds(k * SC_LANES, SC_LANES)] for k in range(TOP_K)]

                @pl.loop(0, width // SC_LANES)
                def _(j):
                    col = j * SC_LANES
                    acc_lo = acc_hi = None
                    for k in range(TOP_K):
                        word = gbuf[b][t * TOP_K + k, pl.ds(col, SC_LANES)]
                        lo = lax.bitcast_convert_type(word << 16, F32)
                        hi = lax.bitcast_convert_type(word & jnp.int32(-65536), F32)
                        acc_lo = wv[k] * lo if k == 0 else acc_lo + wv[k] * lo
                        acc_hi = wv[k] * hi if k == 0 else acc_hi + wv[k] * hi
                    obuf[b][t, pl.ds(col, SC_LANES)] = acc_lo
                    obuf[b][t, pl.ds(D_PACK + col, SC_LANES)] = acc_hi

        for b in range(2):
            for cp in fetch(b, b):
                cp.start()

        @pl.loop(0, n_chunks, step=2)
        def _(i):
            for b in range(2):
                for cp in fetch(i + b, b):
                    cp.wait()

                @pl.when(i > 0)
                def _():
                    flush(i + b - 2, b).wait()

                reduce_chunk(b)
                flush(i + b, b).start()

                @pl.when(i + 2 + b < n_chunks)
                def _():
                    for cp in fetch(i + 2 + b, b):
                        cp.start()

        for b in range(2):
            flush(n_chunks - 2 + b, b).wait()

    return combine(ys, pos_tk, wsplat)


def _moe_routed(h, eidx_t, rank_t, counts, wsplat, w_gate, w_up, w_down, layer):
    counts = counts.reshape(1, N_EXPERTS)
    nblk = (counts + TE - 1) // TE
    blk_end = jnp.cumsum(nblk, axis=1)
    blk_start = blk_end - nblk
    blocks = jnp.arange(NB, dtype=jnp.int32).reshape(NB, 1)
    block_expert = jnp.minimum(jnp.sum(blocks >= blk_end, axis=1, keepdims=True), N_EXPERTS - 1)
    mine = block_expert == jnp.arange(N_EXPERTS, dtype=jnp.int32).reshape(1, N_EXPERTS)
    cnt_b = jnp.sum(jnp.where(mine, counts, 0), axis=1, keepdims=True)
    start_b = jnp.sum(jnp.where(mine, blk_start, 0), axis=1, keepdims=True)
    block_rows = jnp.clip(cnt_b - (blocks - start_b) * TE, 0, TE)
    n_used = blk_end[0, -1].reshape(1).astype(jnp.int32)
    present = (nblk > 0).astype(jnp.int32)
    run_of_expert = jnp.cumsum(present, axis=1) - 1
    block_run = jnp.sum(jnp.where(mine, run_of_expert, 0), axis=1)
    experts = jnp.arange(N_EXPERTS, dtype=jnp.int32).reshape(1, N_EXPERTS)
    run_hit = jnp.logical_and(run_of_expert == experts.reshape(N_EXPERTS, 1), present > 0)
    run_expert = jnp.sum(jnp.where(run_hit, experts, 0), axis=1)
    n_runs = jnp.sum(present).reshape(1)
    pos = _positions(blk_start.reshape(N_EXPERTS).astype(jnp.int32), eidx_t, rank_t)
    pos_rows = pos.reshape(TOP_K, T_PART // SC_CHUNK, SC_CHUNK).transpose(1, 0, 2).reshape(-1, SC_CHUNK)
    xs = _sc_scatter_rows(h, pos_rows)
    ys = _experts(block_expert.reshape(NB).astype(jnp.int32), block_rows.reshape(NB).astype(jnp.int32),
                  n_used, block_run.astype(jnp.int32), run_expert.astype(jnp.int32), n_runs.astype(jnp.int32),
                  xs, w_gate, w_up, w_down, layer)
    return _sc_combine(ys, pos.T.reshape(-1), wsplat)


N_SLABS = (3 * C_FDIM + 2 * D_MODEL) // 128


def _hgrn_in_body(acc_ref, rt_ref, mod0_ref, mod1_ref, nmix_ref, win_ref, *rest):
    x_ref, z_ref = rest[-2:]
    x = acc_ref[...] + mod0_ref[5:6, :] * rt_ref[...]
    x_ref[...] = x
    hb = _modulate(x, nmix_ref[...], mod1_ref[0:1, :], mod1_ref[1:2, :]).astype(BF16)
    for s in range(N_SLABS // C_HEADS):
        zz = _bdot(hb, win_ref[:, s * D_MODEL:(s + 1) * D_MODEL])
        for hh in range(C_HEADS):
            z_ref[s * C_HEADS + hh] = zz[:, hh * 128:(hh + 1) * 128]


def _hgrn_in(acc, routed, tile0, prev, mod0, mod1, norm_mix, w_in):
    def shifted_mod():
        return pl.BlockSpec((None, 6, D_MODEL), lambda i: (_mod_row(i + tile0), 0, 0))

    any_spec = pl.BlockSpec(memory_space=pl.ANY)
    prev = () if prev is None else tuple(prev)
    n_in = 6
    return pl.pallas_call(
        _hgrn_in_body,
        out_shape=(
            jax.ShapeDtypeStruct((T, D_MODEL), F32),
            jax.ShapeDtypeStruct((N_SLABS, T, 128), F32),
        ),
        grid=(NT_PART,),
        in_specs=[
            _tile_spec(), _tile_spec(), shifted_mod(), shifted_mod(),
            _const_spec((1, D_MODEL)),
            _const_spec((D_MODEL, 3 * C_FDIM + 2 * D_MODEL)),
        ] + [any_spec] * len(prev),
        out_specs=(
            pl.BlockSpec((TM, D_MODEL), lambda i: (i + tile0, 0)),
            pl.BlockSpec((N_SLABS, TM, 128), lambda i: (0, i + tile0, 0)),
        ),
        input_output_aliases={n_in + k: k for k in range(len(prev))},
        compiler_params=_params("parallel"),
        name="hgrn_in_proj",
    )(acc, routed, mod0, mod1, norm_mix.reshape(1, -1), w_in.astype(BF16), *prev)


def _gla_body(q_ref, f_ref, v_ref, lb_ref, s0_ref, o_ref, ns_ref, st_ref, *, rev):
    j = pl.program_id(0)
    ti = NT - 1 - j if rev else j
    is_ctx = ti < NT_P
    first_lat = (ti - NT_P) % TILES_PER_LAT == (TILES_PER_LAT - 1 if rev else 0)

    @pl.when(is_ctx)
    def _():
        st_ref[...] = jnp.zeros(st_ref.shape, F32)

    @pl.when(jnp.logical_and(jnp.logical_not(is_ctx), first_lat))
    def _():
        st_ref[...] = s0_ref[...]

    row = lax.broadcasted_iota(jnp.int32, (TM, TM), 0)
    col = lax.broadcasted_iota(jnp.int32, (TM, TM), 1)
    same_chunk = (row // SCAN_CHUNK) == (col // SCAN_CHUNK)
    seen = jnp.logical_and(same_chunk, (col >= row) if rev else (col <= row))
    cum_w = seen.astype(BF16)
    mid = SCAN_CHUNK // 2 if rev else SCAN_CHUNK // 2 - 1
    last = 0 if rev else SCAN_CHUNK - 1
    n_chunks = TM // SCAN_CHUNK
    order = range(n_chunks - 1, -1, -1) if rev else range(n_chunks)
    group = 4

    def chunk_rows(b, off):
        return jnp.concatenate(
            [jnp.broadcast_to(b[c * SCAN_CHUNK + off:c * SCAN_CHUNK + off + 1, :], (SCAN_CHUNK, b.shape[1]))
             for c in range(n_chunks)], axis=0)

    def head_group(gi, carry):
        heads = [gi * group + u for u in range(group)]
        qs, kk, vv, bcum = [], [], [], []
        for hd in heads:
            lb = lb_ref[hd]
            qs.append(_silu(q_ref[hd]) * (C_DK ** -0.5))
            fg = lb + (1.0 - lb) * jax.nn.sigmoid(f_ref[hd])
            kk.append(1.0 - fg)
            vv.append(v_ref[hd].astype(BF16))
            g = jnp.log(fg)
            g_hi = g.astype(BF16)
            r1 = g - g_hi.astype(F32)
            g_mid = r1.astype(BF16)
            g_lo = (r1 - g_mid.astype(F32)).astype(BF16)
            bcum.append(_bdot(cum_w, g_hi) + _bdot(cum_w, g_mid) + _bdot(cum_w, g_lo))
        o_intra, q_dec, kv, decay = [], [], [], []
        for u in range(group):
            b_mid = chunk_rows(bcum[u], mid)
            b_last = chunk_rows(bcum[u], last)
            qe = (qs[u] * jnp.exp(bcum[u] - b_mid)).astype(BF16)
            ke = (kk[u] * jnp.exp(b_mid - bcum[u])).astype(BF16)
            att = jnp.where(seen, _dot_nt(qe, ke), 0.0)
            o_intra.append(_bdot(att.astype(BF16), vv[u]))
            q_dec.append((qs[u] * jnp.exp(bcum[u])).astype(BF16))
            k_dec = (kk[u] * jnp.exp(b_last - bcum[u])).astype(BF16)
            kv.append([_dot_tn(vv[u][c * SCAN_CHUNK:(c + 1) * SCAN_CHUNK], k_dec[c * SCAN_CHUNK:(c + 1) * SCAN_CHUNK])
                       for c in range(n_chunks)])
            decay.append([jnp.exp(bcum[u][c * SCAN_CHUNK + last:c * SCAN_CHUNK + last + 1, :])
                          for c in range(n_chunks)])
        st = [st_ref[hd] for hd in heads]
        for c in order:
            sl = slice(c * SCAN_CHUNK, (c + 1) * SCAN_CHUNK)
            for u, hd in enumerate(heads):
                o_ref[hd, pl.ds(c * SCAN_CHUNK, SCAN_CHUNK), :] = (
                    o_intra[u][sl] + _dot_nt(q_dec[u][sl], st[u].astype(BF16)))
                st[u] = decay[u][c] * st[u] + kv[u][c]
        for u, hd in enumerate(heads):
            st_ref[hd] = st[u]
        return carry

    lax.fori_loop(0, C_HEADS // group, head_group, 0)

    @pl.when(is_ctx)
    def _():
        ns_ref[...] = st_ref[...]


def _gla(z3, lb_dir, s0t_dir, *, rev):
    def ti_of(j):
        return NT - 1 - j if rev else j

    f_slab = 2 if rev else 1

    def lat_map(j):
        return (jnp.clip((ti_of(j) - NT_P) // TILES_PER_LAT, 0, DEC_BATCH - 1), 0, 0, 0)

    return pl.pallas_call(
        functools.partial(_gla_body, rev=rev),
        out_shape=(
            jax.ShapeDtypeStruct((C_HEADS, T, C_DV), F32),
            jax.ShapeDtypeStruct((BATCH, C_HEADS, C_DV, C_DK), F32),
        ),
        grid=(NT,),
        in_specs=[
            pl.BlockSpec((C_HEADS, TM, 128), lambda j: (0, ti_of(j), 0)),
            pl.BlockSpec((C_HEADS, TM, 128), lambda j: (f_slab, ti_of(j), 0)),
            pl.BlockSpec((C_HEADS, TM, 128), lambda j: (3, ti_of(j), 0)),
            _const_spec((C_HEADS, 1, C_DK)),
            pl.BlockSpec((None, C_HEADS, C_DV, C_DK), lat_map),
        ],
        out_specs=(
            pl.BlockSpec((C_HEADS, TM, C_DV), lambda j: (0, ti_of(j), 0)),
            pl.BlockSpec((None, C_HEADS, C_DV, C_DK),
                         lambda j: (jnp.minimum(ti_of(j), NT_P - 1), 0, 0, 0)),
        ),
        scratch_shapes=[pltpu.VMEM((C_HEADS, C_DV, C_DK), F32)],
        compiler_params=_params("arbitrary"),
        name="gla_bwd" if rev else "gla_fwd",
    )(z3, z3, z3, lb_dir, s0t_dir)


def _hgrn_out_body(ofw_ref, obw_ref, gate_ref, x_ref, mod_ref, no_ref, wout_ref, o_ref, cat_ref):
    for hd in range(C_HEADS):
        o = ofw_ref[hd] + obw_ref[hd]
        cat_ref[:, hd * C_DV:(hd + 1) * C_DV] = (_rms(o, no_ref[...]) * _silu(gate_ref[hd])).astype(BF16)
    o_ref[...] = x_ref[...] + mod_ref[2:3, :] * _bdot(cat_ref[...], wout_ref[...])


def _hgrn_out(o_fw, o_bw, z3, x, mod, norm_o, w_out):
    head_spec = pl.BlockSpec((C_HEADS, TM, C_DV), lambda i: (0, i, 0))
    return pl.pallas_call(
        _hgrn_out_body,
        out_shape=jax.ShapeDtypeStruct((T, D_MODEL), F32),
        grid=(NT,),
        in_specs=[
            head_spec, head_spec,
            pl.BlockSpec((C_HEADS, TM, 128), lambda i: (4, i, 0)),
            _tile_spec(), _mod_spec(),
            _const_spec((1, C_DV)),
            _const_spec((D_MODEL, D_MODEL)),
        ],
        out_specs=_tile_spec(),
        scratch_shapes=[pltpu.VMEM((TM, D_MODEL), BF16)],
        compiler_params=_params("parallel"),
        name="hgrn_out_proj",
    )(o_fw, o_bw, z3, x, mod, norm_o.reshape(1, -1), w_out.astype(BF16))


def _final_body(acc_ref, rt_ref, mod_ref, nf_ref, *rest):
    o_ref = rest[-1]
    o_ref[...] = _rms(acc_ref[...] + mod_ref[5:6, :] * rt_ref[...], nf_ref[...])


def _final(acc, routed, mod, norm_final, part_tile0, local0, n_tiles, out_tile0, out_tiles, prev=None):
    local = pl.BlockSpec((TM, D_MODEL), lambda i: (i + local0, 0))
    prev = () if prev is None else (prev,)
    return pl.pallas_call(
        _final_body,
        out_shape=jax.ShapeDtypeStruct((out_tiles * TM, D_MODEL), F32),
        grid=(n_tiles,),
        in_specs=[
            local, local,
            pl.BlockSpec((None, 6, D_MODEL), lambda i: (_mod_row(i + local0 + part_tile0), 0, 0)),
            _const_spec((1, D_MODEL)),
        ] + [pl.BlockSpec(memory_space=pl.ANY)] * len(prev),
        out_specs=pl.BlockSpec((TM, D_MODEL), lambda i: (i + out_tile0, 0)),
        input_output_aliases={4 + k: 0 for k in range(len(prev))},
        compiler_params=_params("parallel"),
        name="final_norm",
    )(acc, routed, mod, norm_final.reshape(1, -1), *prev)


def kernel(x_prompt, x_sample, state_hgrn, c, c_ctx, w_ada, b_ada, norm_mix, norm_ffn, w_out, w_in_ab, w_sg, b_sg, norm_sg, w_dw, b_dw, norm_cv, w_in_hgrn, lb_raw, norm_o, w_router, b_router, w_gate, w_up, w_down, w_sh_gate, w_sh_up, w_sh_down, norm_final):
    cvecs = jnp.concatenate(
        [c_ctx.reshape(1, D_MODEL), c, jnp.zeros((N_MOD_ROWS - 1 - DEC_BATCH, D_MODEL), F32)], axis=0)
    mods = _ada_tables(cvecs, w_ada, b_ada)
    lb_sm = jax.nn.softmax(lb_raw.astype(F32), axis=0)
    lb1 = (jnp.cumsum(lb_sm, axis=0) - lb_sm[0])[1].reshape(2, C_HEADS, 1, C_DK)

    def moe(l, xin):
        parts = []
        for p in range(MOE_PARTS):
            h, eidx_t, w8_t, rank_t, counts, acc = _moe_pre(
                xin, mods[l], norm_ffn[l], w_router[l], b_router[l], w_sh_gate[l], w_sh_up[l], w_sh_down[l],
                p * NT_PART)
            wsplat = jnp.repeat(w8_t.T, SC_LANES, axis=1)
            parts += [acc, _moe_routed(h, eidx_t, rank_t, counts, wsplat, w_gate, w_up, w_down, l)]
        return parts

    x = _l0_mixer(x_prompt.reshape(T_P, D_MODEL), x_sample.reshape(T_S, D_MODEL), mods[0], norm_mix[0], w_in_ab[0], w_sg[0], b_sg[0], norm_sg[0], w_dw[0],
                  b_dw[0], norm_cv[0], w_out[0])
    parts = moe(0, x)
    xz = None
    for p in range(MOE_PARTS):
        xz = _hgrn_in(parts[2 * p], parts[2 * p + 1], p * NT_PART, xz, mods[0], mods[1], norm_mix[1], w_in_hgrn[0])
    x, z3 = xz
    s0t = jnp.swapaxes(state_hgrn[:, 0].astype(F32), -1, -2)
    o_fw, ns_fw = _gla(z3, lb1[0], s0t[:, 0], rev=False)
    o_bw, ns_bw = _gla(z3, lb1[1], s0t[:, 1], rev=True)
    x = _hgrn_out(o_fw, o_bw, z3, x, mods[1], norm_o[0], w_out[1])
    parts = moe(1, x)
    y_p = y_s = None
    for p in range(MOE_PARTS):
        lo, hi = p * NT_PART, (p + 1) * NT_PART
        if lo < NT_P:
            n = min(hi, NT_P) - lo
            y_p = _final(parts[2 * p], parts[2 * p + 1], mods[1], norm_final, lo, 0, n, lo, NT_P, y_p)
        if hi > NT_P:
            first = max(lo, NT_P)
            y_s = _final(parts[2 * p], parts[2 * p + 1], mods[1], norm_final, lo, first - lo, hi - first,
                         first - NT_P, NT_S, y_s)
    y_p = y_p.reshape(BATCH, SEQ, D_MODEL)
    y_s = y_s.reshape(DEC_BATCH, DEC_SEQ, D_MODEL)
    new_state = jnp.swapaxes(jnp.stack([ns_fw, ns_bw], axis=1), -1, -2)[:, None]
    return (y_p, y_s, new_state)
```

```python
import functools

import jax
import jax.numpy as jnp
from jax import lax
from jax.experimental import pallas as pl
from jax.experimental.pallas import tpu as pltpu
from jax.experimental.pallas import tpu_sc as plsc

F32 = jnp.float32
BF16 = jnp.bfloat16
HIGHEST = lax.Precision.HIGHEST

D_MODEL = 1024
BATCH = 32
SEQ = 256
DEPTH = 2
DEC_BATCH = 8
DEC_SEQ = 2048
GRID_W = 64
A_WIDTH = D_MODEL // 2
A_GROUPS = 4
A_GC = A_WIDTH // A_GROUPS
CHUNK_A = 128
B_WIDTH = D_MODEL - A_WIDTH
CONV_W = 31
CONV_PAD = CONV_W // 2
C_HEADS = 8
C_DK = 128
C_DV = D_MODEL // C_HEADS
C_FDIM = C_HEADS * C_DK
SCAN_CHUNK = 64
N_EXPERTS = 64
TOP_K = 8
N_GROUPS = 8
GROUP_SIZE = N_EXPERTS // N_GROUPS
TOPK_GROUPS = 4
D_EXPERT = 256
D_SHARED = 256
ROUTED_SCALE = 2.5
EPS = 1e-6

TM = 256
T_P = BATCH * SEQ
T_S = DEC_BATCH * DEC_SEQ
T = T_P + T_S
NT_P = T_P // TM
NT_S = T_S // TM
NT = NT_P + NT_S
TILES_PER_LAT = DEC_SEQ // TM
TE = 512
MOE_PARTS = 2
NT_PART = NT // MOE_PARTS
T_PART = NT_PART * TM
NB = T_PART * TOP_K // TE + N_EXPERTS
P_ROWS = NB * TE
D_PACK = D_MODEL // 2
N_MOD_ROWS = 16
CONV_HALO = 16
VMEM_LIMIT = 48 * 1024 * 1024
SC_CORES = 2
SC_SUBCORES = 16
SC_WORKERS = SC_CORES * SC_SUBCORES
SC_LANES = 16
SC_CHUNK = 64
COMBINE_TOKENS = SC_CHUNK // TOP_K


def _mod_row(i):
    return jnp.where(i < NT_P, 0, 1 + (i - NT_P) // TILES_PER_LAT)


def _silu(x):
    return x * jax.nn.sigmoid(x)


def _gelu(x):
    return x * (0.5 * (1.0 + jnp.tanh(0.7978845608028654 * (x + 0.044715 * (x * x * x)))))


def _rms(x, g):
    return x * lax.rsqrt(jnp.mean(x * x, axis=-1, keepdims=True) + EPS) * g


def _layernorm(x, g):
    xc = x - jnp.mean(x, axis=-1, keepdims=True)
    return xc * lax.rsqrt(jnp.mean(xc * xc, axis=-1, keepdims=True) + EPS) * g


def _modulate(x, g, shift, scale):
    return _rms(x, g) * (1.0 + scale) + shift


def _bdot(a, b):
    return jnp.dot(a, b, preferred_element_type=F32)


def _dot_nt(a, b, precision=None):
    return lax.dot_general(a, b, (((1,), (1,)), ((), ())), precision=precision,
                           preferred_element_type=F32)


def _dot_tn(a, b):
    return lax.dot_general(a, b, (((0,), (0,)), ((), ())), preferred_element_type=F32)


def _pack_pairs(x):
    m = x.shape[1] // 2
    lo = lax.bitcast_convert_type(x[:, :m].astype(BF16).astype(F32), jnp.uint32)
    hi = lax.bitcast_convert_type(x[:, m:].astype(BF16).astype(F32), jnp.uint32)
    return lax.bitcast_convert_type(hi | (lo >> 16), jnp.int32)


def _unpack_pairs(w):
    u = lax.bitcast_convert_type(w, jnp.uint32)
    lo = lax.bitcast_convert_type(u << 16, F32)
    hi = lax.bitcast_convert_type(u & jnp.uint32(0xFFFF0000), F32)
    return lo, hi


def _params(*sem):
    return pltpu.CompilerParams(dimension_semantics=sem, vmem_limit_bytes=VMEM_LIMIT)


def _const_spec(shape):
    nd = len(shape)
    return pl.BlockSpec(shape, lambda *_: (0,) * nd)


def _ada_body(c_ref, w_ref, b_ref, o_ref):
    s = _silu(c_ref[...])
    o_ref[...] = jnp.dot(s, w_ref[...], precision=HIGHEST, preferred_element_type=F32) + b_ref[...]


def _ada_tables(cvecs, w_ada, b_ada):
    out = pl.pallas_call(
        _ada_body,
        out_shape=jax.ShapeDtypeStruct((DEPTH, N_MOD_ROWS, 6 * D_MODEL), F32),
        grid=(DEPTH, 6),
        in_specs=[
            _const_spec((N_MOD_ROWS, D_MODEL)),
            pl.BlockSpec((None, D_MODEL, D_MODEL), lambda l, j: (l, 0, j)),
            pl.BlockSpec((None, 1, D_MODEL), lambda l, j: (l, 0, j)),
        ],
        out_specs=pl.BlockSpec((None, N_MOD_ROWS, D_MODEL), lambda l, j: (l, 0, j)),
        compiler_params=_params("parallel", "parallel"),
        name="ada_tables",
    )(cvecs, w_ada, b_ada.reshape(DEPTH, 1, 6 * D_MODEL))
    return out.reshape(DEPTH, N_MOD_ROWS, 6, D_MODEL)


def _mod_spec():
    return pl.BlockSpec((None, 6, D_MODEL), lambda i: (_mod_row(i), 0, 0))


def _tile_spec():
    return pl.BlockSpec((TM, D_MODEL), lambda i: (i, 0))


SUBLANES = 8


def _conv_segment(pad_ref, shift_ref, conv_ref, wdw_ref, pad_base, out_base, seg):
    rb = min(seg, 64)
    for cb in range(B_WIDTH // 128):
        cs = slice(cb * 128, (cb + 1) * 128)
        for r0 in range(0, seg, rb):
            acc = jnp.zeros((rb, 128), F32)
            for k in range(CONV_W):
                b = (CONV_HALO - CONV_PAD + k) % SUBLANES
                off = pad_base + r0 + CONV_HALO - CONV_PAD + k - b
                src = pad_ref if b == 0 else shift_ref.at[b - 1]
                acc = acc + wdw_ref[k:k + 1, cs] * src[off:off + rb, cs]
            conv_ref[out_base + r0:out_base + r0 + rb, cs] = acc


def _l0_body(xc_ref, xl_ref, mod_ref, nmix_ref, win_ref, wsg_ref, bsg_ref, nsg_ref, wdw_ref, bdw_ref,
             ncv_ref, wout_ref, o_ref, cat_ref, pad_ref, shift_ref, conv_ref):
    i = pl.program_id(0)
    x = jnp.where(i < NT_P, xc_ref[...], xl_ref[...])
    h = _modulate(x, nmix_ref[...], mod_ref[0:1, :], mod_ref[1:2, :])
    z = _bdot(h.astype(BF16), win_ref[...])
    u = _gelu(z[:, :A_WIDTH])
    vb = _layernorm(_gelu(z[:, A_WIDTH:2 * A_WIDTH]), nsg_ref[...]).astype(BF16)
    for n in range(TM // CHUNK_A):
        rs = slice(n * CHUNK_A, (n + 1) * CHUNK_A)
        for g in range(A_GROUPS):
            cs = slice(g * A_GC, (g + 1) * A_GC)
            m = _bdot(wsg_ref[g], vb[rs, cs]) + bsg_ref[g]
            cat_ref[rs, cs] = (u[rs, cs] * m).astype(BF16)
    hb = z[:, 2 * A_WIDTH:2 * A_WIDTH + B_WIDTH] * jax.nn.sigmoid(z[:, 2 * A_WIDTH + B_WIDTH:])

    def conv_tile(seg):
        stride = seg + 2 * CONV_HALO
        halo = jnp.zeros((CONV_HALO, B_WIDTH), F32)
        for s in range(TM // seg):
            b = s * stride
            pad_ref[b:b + CONV_HALO, :] = halo
            pad_ref[b + CONV_HALO:b + CONV_HALO + seg, :] = hb[s * seg:(s + 1) * seg, :]
            pad_ref[b + CONV_HALO + seg:b + stride, :] = halo
        rows = (TM // seg) * stride - SUBLANES
        for b in range(1, SUBLANES):
            shift_ref[b - 1, 0:rows, :] = pad_ref[b:b + rows, :]
        for s in range(TM // seg):
            _conv_segment(pad_ref, shift_ref, conv_ref, wdw_ref, s * stride, s * seg, seg)

    @pl.when(i < NT_P)
    def _():
        conv_tile(SEQ)

    @pl.when(i >= NT_P)
    def _():
        conv_tile(GRID_W)

    yb = _layernorm(conv_ref[...] + bdw_ref[...], ncv_ref[...])
    cat_ref[:, A_WIDTH:] = _silu(yb).astype(BF16)
    out = _bdot(cat_ref[...], wout_ref[...])
    o_ref[...] = x + mod_ref[2:3, :] * out


def _l0_mixer(x_ctx, x_lat, mod, norm_mix, w_in, w_sg, b_sg, norm_sg, w_dw, b_dw, norm_cv, w_out):
    assert TM == SEQ and TM % GRID_W == 0 and TM % CHUNK_A == 0
    pad_rows = (TM // GRID_W) * (GRID_W + 2 * CONV_HALO)
    assert pad_rows >= SEQ + 2 * CONV_HALO
    return pl.pallas_call(
        _l0_body,
        out_shape=jax.ShapeDtypeStruct((T, D_MODEL), F32),
        grid=(NT,),
        in_specs=[
            pl.BlockSpec((TM, D_MODEL), lambda i: (jnp.minimum(i, NT_P - 1), 0)),
            pl.BlockSpec((TM, D_MODEL), lambda i: (jnp.maximum(i - NT_P, 0), 0)),
            _mod_spec(),
            _const_spec((1, D_MODEL)),
            _const_spec((D_MODEL, 2 * A_WIDTH + 2 * B_WIDTH)),
            _const_spec((A_GROUPS, CHUNK_A, CHUNK_A)),
            _const_spec((A_GROUPS, CHUNK_A, 1)),
            _const_spec((1, A_WIDTH)),
            _const_spec((CONV_W, B_WIDTH)),
            _const_spec((1, B_WIDTH)),
            _const_spec((1, B_WIDTH)),
            _const_spec((D_MODEL, D_MODEL)),
        ],
        out_specs=_tile_spec(),
        scratch_shapes=[
            pltpu.VMEM((TM, D_MODEL), BF16),
            pltpu.VMEM((pad_rows, B_WIDTH), F32),
            pltpu.VMEM((SUBLANES - 1, pad_rows, B_WIDTH), F32),
            pltpu.VMEM((TM, B_WIDTH), F32),
        ],
        compiler_params=_params("parallel"),
        name="l0_mixer",
    )(x_ctx, x_lat, mod, norm_mix.reshape(1, -1), w_in.astype(BF16), w_sg.astype(BF16),
      b_sg.reshape(A_GROUPS, CHUNK_A, 1), norm_sg.reshape(1, -1), w_dw, b_dw.reshape(1, -1),
      norm_cv.reshape(1, -1), w_out.astype(BF16))


def _route(scores, biased):
    n = scores.shape[-1]
    shp = (N_GROUPS, GROUP_SIZE, n)
    s3 = scores.reshape(shp)
    b3 = biased.reshape(shp)
    m_iota = lax.broadcasted_iota(jnp.int32, shp, 1).astype(F32)
    g_iota = lax.broadcasted_iota(jnp.int32, shp, 0).astype(F32)
    e_iota = g_iota * GROUP_SIZE + m_iota
    neg = -jnp.inf

    def amax1(v):
        return jnp.max(v, axis=1, keepdims=True)

    def amin1(v):
        return jnp.min(v, axis=1, keepdims=True)

    m1 = amax1(b3)
    i1 = amin1(jnp.where(b3 == m1, m_iota, float(GROUP_SIZE)))
    m2 = amax1(jnp.where(m_iota == i1, neg, b3))
    grp = m1 + m2
    gi1 = lax.broadcasted_iota(jnp.int32, grp.shape, 0).astype(F32)
    gmask = jnp.zeros(grp.shape, jnp.bool_)
    for _ in range(TOPK_GROUPS):
        gm = jnp.max(grp, axis=0, keepdims=True)
        gi = jnp.min(jnp.where(grp == gm, gi1, float(N_GROUPS)), axis=0, keepdims=True)
        hit = gi1 == gi
        gmask = jnp.logical_or(gmask, hit)
        grp = jnp.where(hit, neg, grp)
    cand = jnp.where(gmask, b3, neg)
    ids, vals, hits = [], [], []
    for _ in range(TOP_K):
        mx = jnp.max(amax1(cand), axis=0, keepdims=True)
        ei = jnp.min(amin1(jnp.where(cand == mx, e_iota, float(N_EXPERTS))), axis=0, keepdims=True)
        hit = e_iota == ei
        ids.append(ei.reshape(1, n))
        vals.append(_pick(hit, s3))
        hits.append(hit)
        cand = jnp.where(hit, neg, cand)
    return jnp.concatenate(ids, axis=0).astype(jnp.int32), jnp.concatenate(vals, axis=0), hits


def _pick(hit, v3):
    s = jnp.sum(jnp.sum(jnp.where(hit, v3, 0.0), axis=1, keepdims=True), axis=0, keepdims=True)
    return s.reshape(1, v3.shape[-1])


def _moe_pre_body(x_ref, mod_ref, nffn_ref, wrt_ref, br_ref, wsg_ref, wsu_ref, wsd_ref,
                  h_ref, eidx_ref, w8_ref, rank_ref, cnt_ref, acc_ref, run_ref):
    @pl.when(pl.program_id(0) == 0)
    def _():
        run_ref[...] = jnp.zeros(run_ref.shape, F32)

    x = x_ref[...]
    h = _modulate(x, nffn_ref[...], mod_ref[3:4, :], mod_ref[4:5, :])
    h_ref[...] = _pack_pairs(h)
    hb = h.astype(BF16)
    h_lo = (h - hb.astype(F32)).astype(BF16)
    wr = wrt_ref[...]
    wr_hi = wr.astype(BF16)
    wr_lo = (wr - wr_hi.astype(F32)).astype(BF16)
    logits_t = _dot_nt(wr_hi, hb) + (_dot_nt(wr_hi, h_lo) + _dot_nt(wr_lo, hb))
    scores = jax.nn.sigmoid(logits_t)
    eidx, sv, hits = _route(scores, scores + br_ref[...])
    eidx_ref[...] = eidx
    w8_ref[...] = sv / jnp.sum(sv, axis=0, keepdims=True) * ROUTED_SCALE
    sel3 = hits[0]
    for hit in hits[1:]:
        sel3 = jnp.logical_or(sel3, hit)
    sel = sel3.astype(F32).reshape(N_EXPERTS, TM)
    earlier = (lax.broadcasted_iota(jnp.int32, (TM, TM), 0)
               < lax.broadcasted_iota(jnp.int32, (TM, TM), 1)).astype(BF16)
    rank3 = (_bdot(sel.astype(BF16), earlier) + run_ref[...]).reshape(N_GROUPS, GROUP_SIZE, TM)
    rank_ref[...] = jnp.concatenate([_pick(hit, rank3) for hit in hits], axis=0).astype(jnp.int32)
    run_ref[...] = run_ref[...] + jnp.sum(sel, axis=1, keepdims=True)
    cnt_ref[...] = run_ref[...].astype(jnp.int32)
    sh = _bdot((_silu(_bdot(hb, wsg_ref[...])) * _bdot(hb, wsu_ref[...])).astype(BF16), wsd_ref[...])
    acc_ref[...] = x + mod_ref[5:6, :] * sh


def _moe_pre(x, mod, norm_ffn, w_router, b_router, w_sh_gate, w_sh_up, w_sh_down, tile0):
    return pl.pallas_call(
        _moe_pre_body,
        out_shape=(
            jax.ShapeDtypeStruct((T_PART, D_PACK), jnp.int32),
            jax.ShapeDtypeStruct((TOP_K, T_PART), jnp.int32),
            jax.ShapeDtypeStruct((TOP_K, T_PART), F32),
            jax.ShapeDtypeStruct((TOP_K, T_PART), jnp.int32),
            jax.ShapeDtypeStruct((N_EXPERTS, 1), jnp.int32),
            jax.ShapeDtypeStruct((T_PART, D_MODEL), F32),
        ),
        grid=(NT_PART,),
        in_specs=[
            pl.BlockSpec((TM, D_MODEL), lambda i: (i + tile0, 0)),
            pl.BlockSpec((None, 6, D_MODEL), lambda i: (_mod_row(i + tile0), 0, 0)),
            _const_spec((1, D_MODEL)),
            _const_spec((N_EXPERTS, D_MODEL)),
            _const_spec((N_EXPERTS, 1)),
            _const_spec((D_MODEL, D_SHARED)),
            _const_spec((D_MODEL, D_SHARED)),
            _const_spec((D_SHARED, D_MODEL)),
        ],
        out_specs=(
            pl.BlockSpec((TM, D_PACK), lambda i: (i, 0)),
            pl.BlockSpec((TOP_K, TM), lambda i: (0, i)),
            pl.BlockSpec((TOP_K, TM), lambda i: (0, i)),
            pl.BlockSpec((TOP_K, TM), lambda i: (0, i)),
            _const_spec((N_EXPERTS, 1)),
            _tile_spec(),
        ),
        scratch_shapes=[pltpu.VMEM((N_EXPERTS, 1), F32)],
        compiler_params=_params("arbitrary"),
        name="moe_router_shared",
    )(x, mod, norm_ffn.reshape(1, -1), w_router.T, b_router.reshape(N_EXPERTS, 1),
      w_sh_gate.astype(BF16), w_sh_up.astype(BF16), w_sh_down.astype(BF16))


EXPERT_RING = 4


WEIGHT_SLOTS = 2


def _expert_body(be_ref, nv_ref, nu_ref, run_ref, rexp_ref, nrun_ref, xs_hbm, wg_hbm, wu_hbm, wd_hbm, ys_ref,
                 wgu_s, wd_s, ring, sems, wg_buf, wu_buf, wd_buf, wsems, *, layer):
    j = pl.program_id(0)
    n_used = nu_ref[0]
    n_runs = nrun_ref[0]
    live_tile = j < n_used

    def fetch(t):
        slot = t % EXPERT_RING
        return pltpu.make_async_copy(xs_hbm.at[pl.ds(t * TE, TE)], ring.at[slot], sems.at[slot])

    def wfetch(r):
        slot = r % WEIGHT_SLOTS
        e = rexp_ref[r]
        return [pltpu.make_async_copy(wg_hbm.at[layer, e], wg_buf.at[slot], wsems.at[slot, 0]),
                pltpu.make_async_copy(wu_hbm.at[layer, e], wu_buf.at[slot], wsems.at[slot, 1]),
                pltpu.make_async_copy(wd_hbm.at[layer, e], wd_buf.at[slot], wsems.at[slot, 2])]

    @pl.when(j == 0)
    def _():
        for t in range(EXPERT_RING - 1):
            @pl.when(t < n_used)
            def _():
                fetch(t).start()
        for r in range(WEIGHT_SLOTS):
            @pl.when(r < n_runs)
            def _():
                for cp in wfetch(r):
                    cp.start()

    @pl.when(j + (EXPERT_RING - 1) < n_used)
    def _():
        fetch(j + (EXPERT_RING - 1)).start()

    @pl.when(jnp.logical_and(live_tile, jnp.logical_or(j == 0, be_ref[j] != be_ref[jnp.maximum(j - 1, 0)])))
    def _():
        r = run_ref[j]
        slot = r % WEIGHT_SLOTS
        for cp in wfetch(r):
            cp.wait()
        wgu_s[:, :D_EXPERT] = wg_buf[slot].astype(BF16)
        wgu_s[:, D_EXPERT:] = wu_buf[slot].astype(BF16)
        wd_s[...] = wd_buf[slot].astype(BF16)

        @pl.when(r + WEIGHT_SLOTS < n_runs)
        def _():
            for cp in wfetch(r + WEIGHT_SLOTS):
                cp.start()

    @pl.when(live_tile)
    def _():
        fetch(j).wait()
        live = lax.broadcasted_iota(jnp.int32, (TE, 1), 0) < nv_ref[j]
        lo, hi = _unpack_pairs(jnp.where(live, ring[j % EXPERT_RING], 0))
        xb = jnp.concatenate([lo.astype(BF16), hi.astype(BF16)], axis=1)
        hgu = _bdot(xb, wgu_s[...])
        hh = _silu(hgu[:, :D_EXPERT]) * hgu[:, D_EXPERT:]
        ys_ref[...] = _pack_pairs(_bdot(hh.astype(BF16), wd_s[...]))


def _experts(block_expert, block_rows, n_used, block_run, run_expert, n_runs, xs, w_gate, w_up, w_down, layer):
    def row_map(j, be, nv, nu, run, rexp, nrun):
        return (jnp.minimum(j, nu[0] - 1), 0)

    any_spec = pl.BlockSpec(memory_space=pl.ANY)
    return pl.pallas_call(
        functools.partial(_expert_body, layer=layer),
        out_shape=jax.ShapeDtypeStruct((P_ROWS, D_PACK), jnp.int32),
        grid_spec=pltpu.PrefetchScalarGridSpec(
            num_scalar_prefetch=6,
            grid=(NB,),
            in_specs=[any_spec, any_spec, any_spec, any_spec],
            out_specs=pl.BlockSpec((TE, D_PACK), row_map),
            scratch_shapes=[
                pltpu.VMEM((D_MODEL, 2 * D_EXPERT), BF16),
                pltpu.VMEM((D_EXPERT, D_MODEL), BF16),
                pltpu.VMEM((EXPERT_RING, TE, D_PACK), jnp.int32),
                pltpu.SemaphoreType.DMA((EXPERT_RING,)),
                pltpu.VMEM((WEIGHT_SLOTS, D_MODEL, D_EXPERT), F32),
                pltpu.VMEM((WEIGHT_SLOTS, D_MODEL, D_EXPERT), F32),
                pltpu.VMEM((WEIGHT_SLOTS, D_EXPERT, D_MODEL), F32),
                pltpu.SemaphoreType.DMA((WEIGHT_SLOTS, 3)),
            ],
        ),
        compiler_params=_params("arbitrary"),
        name="moe_experts",
    )(block_expert, block_rows, n_used, block_run, run_expert, n_runs, xs, w_gate, w_up, w_down)


def _positions_body(start_ref, eidx_ref, rank_ref, pos_ref):
    eidx = eidx_ref[...]
    base = jnp.zeros(eidx.shape, jnp.int32)
    for e in range(N_EXPERTS):
        base = jnp.where(eidx == e, start_ref[e], base)
    pos_ref[...] = base * TE + rank_ref[...]


def _positions(blk_start, eidx_t, rank_t):
    full = pl.BlockSpec((TOP_K, T_PART), lambda i, s: (0, 0))
    return pl.pallas_call(
        _positions_body,
        out_shape=jax.ShapeDtypeStruct((TOP_K, T_PART), jnp.int32),
        grid_spec=pltpu.PrefetchScalarGridSpec(
            num_scalar_prefetch=1, grid=(1,), in_specs=[full, full], out_specs=full),
        compiler_params=_params("arbitrary"),
        name="moe_positions",
    )(blk_start, eidx_t, rank_t)


def _sc_mesh():
    return plsc.VectorSubcoreMesh(core_axis_name="c", subcore_axis_name="s")


def _sc_worker():
    return lax.axis_index("s") * SC_CORES + lax.axis_index("c")


def _sc_scatter_rows(h, pos_rows):
    c = SC_CHUNK
    n_chunks = T_PART // SC_WORKERS // c
    width = h.shape[1]

    @functools.partial(
        pl.kernel, mesh=_sc_mesh(),
        out_type=jax.ShapeDtypeStruct((P_ROWS, width), h.dtype),
        scratch_types=[pltpu.VMEM((n_chunks * TOP_K, c), jnp.int32),
                       pltpu.VMEM((c, width), h.dtype), pltpu.VMEM((c, width), h.dtype)]
        + [pltpu.SemaphoreType.DMA] * 4,
        name="moe_dispatch_scatter",
    )
    def scatter(h_hbm, pos_hbm, xs_hbm, idx_v, buf0, buf1, sem_in0, sem_in1, sem_out0, sem_out1):
        assert n_chunks % 2 == 0
        bufs, sem_in, sem_out = (buf0, buf1), (sem_in0, sem_in1), (sem_out0, sem_out1)
        first = _sc_worker() * n_chunks
        pltpu.sync_copy(pos_hbm.at[pl.ds(first * TOP_K, n_chunks * TOP_K)], idx_v)

        def load(i, b):
            return pltpu.make_async_copy(h_hbm.at[pl.ds((first + i) * c, c)], bufs[b], sem_in[b])

        def puts(i, b):
            return [pltpu.make_async_copy(bufs[b], xs_hbm.at[idx_v.at[i * TOP_K + k]], sem_out[b])
                    for k in range(TOP_K)]

        load(0, 0).start()
        load(1, 1).start()

        @pl.loop(0, n_chunks, step=2)
        def _(i):
            for b in range(2):
                load(i + b, b).wait()
                for cp in puts(i + b, b):
                    cp.start()
            for b in range(2):
                for cp in puts(i + b, b):
                    cp.wait()

                @pl.when(i + 2 + b < n_chunks)
                def _():
                    load(i + 2 + b, b).start()

    return scatter(h, pos_rows)


def _sc_combine(ys, pos_tk, wsplat):
    ct = COMBINE_TOKENS
    rows = ct * TOP_K
    tok_per_worker = T_PART // SC_WORKERS
    n_chunks = tok_per_worker // ct
    width = ys.shape[1]
    vmem = pltpu.VMEM

    @functools.partial(
        pl.kernel, mesh=_sc_mesh(),
        out_type=jax.ShapeDtypeStruct((T_PART, D_MODEL), F32),
        scratch_types=[vmem((tok_per_worker * TOP_K,), jnp.int32),
                       vmem((rows, width), ys.dtype), vmem((rows, width), ys.dtype),
                       vmem((ct, TOP_K * SC_LANES), F32), vmem((ct, TOP_K * SC_LANES), F32),
                       vmem((ct, D_MODEL), F32), vmem((ct, D_MODEL), F32)]
        + [pltpu.SemaphoreType.DMA] * 6,
        compiler_params=pltpu.CompilerParams(needs_layout_passes=False),
        name="moe_combine",
    )
    def combine(ys_hbm, pos_hbm, w_hbm, out_hbm, idx_v, g0, g1, w0, w1, o0, o1,
                sem_g0, sem_g1, sem_w0, sem_w1, sem_o0, sem_o1):
        assert n_chunks % 2 == 0
        gbuf, wbuf, obuf = (g0, g1), (w0, w1), (o0, o1)
        sem_g, sem_w, sem_o = (sem_g0, sem_g1), (sem_w0, sem_w1), (sem_o0, sem_o1)
        tok0 = _sc_worker() * tok_per_worker
        pltpu.sync_copy(pos_hbm.at[pl.ds(tok0 * TOP_K, tok_per_worker * TOP_K)], idx_v)

        def fetch(i, b):
            return [pltpu.make_async_copy(ys_hbm.at[idx_v.at[pl.ds(i * rows, rows)]], gbuf[b], sem_g[b]),
                    pltpu.make_async_copy(w_hbm.at[pl.ds(tok0 + i * ct, ct)], wbuf[b], sem_w[b])]

        def flush(i, b):
            return pltpu.make_async_copy(obuf[b], out_hbm.at[pl.ds(tok0 + i * ct, ct)], sem_o[b])

        def reduce_chunk(b):
            @pl.loop(0, ct)
            def _(t):
                wv = [wbuf[b][t, pl.ds(k * SC_LANES, SC_LANES)] for k in range(TOP_K)]

                @pl.loop(0, width // SC_LANES)
                def _(j):
                    col = j * SC_LANES
                    acc_lo = acc_hi = None
                    for k in range(TOP_K):
                        word = gbuf[b][t * TOP_K + k, pl.ds(col, SC_LANES)]
                        lo = lax.bitcast_convert_type(word << 16, F32)
                        hi = lax.bitcast_convert_type(word & jnp.int32(-65536), F32)
                        acc_lo = wv[k] * lo if k == 0 else acc_lo + wv[k] * lo
                        acc_hi = wv[k] * hi if k == 0 else acc_hi + wv[k] * hi
                    obuf[b][t, pl.ds(col, SC_LANES)] = acc_lo
                    obuf[b][t, pl.ds(D_PACK + col, SC_LANES)] = acc_hi

        for b in range(2):
            for cp in fetch(b, b):
                cp.start()

        @pl.loop(0, n_chunks, step=2)
        def _(i):
            for b in range(2):
                for cp in fetch(i + b, b):
                    cp.wait()

                @pl.when(i > 0)
                def _():
                    flush(i + b - 2, b).wait()

                reduce_chunk(b)
                flush(i + b, b).start()

                @pl.when(i + 2 + b < n_chunks)
                def _():
                    for cp in fetch(i + 2 + b, b):
                        cp.start()

        for b in range(2):
            flush(n_chunks - 2 + b, b).wait()

    return combine(ys, pos_tk, wsplat)


def _moe_routed(h, eidx_t, rank_t, counts, wsplat, w_gate, w_up, w_down, layer):
    counts = counts.reshape(1, N_EXPERTS)
    nblk = (counts + TE - 1) // TE
    blk_end = jnp.cumsum(nblk, axis=1)
    blk_start = blk_end - nblk
    blocks = jnp.arange(NB, dtype=jnp.int32).reshape(NB, 1)
    block_expert = jnp.minimum(jnp.sum(blocks >= blk_end, axis=1, keepdims=True), N_EXPERTS - 1)
    mine = block_expert == jnp.arange(N_EXPERTS, dtype=jnp.int32).reshape(1, N_EXPERTS)
    cnt_b = jnp.sum(jnp.where(mine, counts, 0), axis=1, keepdims=True)
    start_b = jnp.sum(jnp.where(mine, blk_start, 0), axis=1, keepdims=True)
    block_rows = jnp.clip(cnt_b - (blocks - start_b) * TE, 0, TE)
    n_used = blk_end[0, -1].reshape(1).astype(jnp.int32)
    present = (nblk > 0).astype(jnp.int32)
    run_of_expert = jnp.cumsum(present, axis=1) - 1
    block_run = jnp.sum(jnp.where(mine, run_of_expert, 0), axis=1)
    experts = jnp.arange(N_EXPERTS, dtype=jnp.int32).reshape(1, N_EXPERTS)
    run_hit = jnp.logical_and(run_of_expert == experts.reshape(N_EXPERTS, 1), present > 0)
    run_expert = jnp.sum(jnp.where(run_hit, experts, 0), axis=1)
    n_runs = jnp.sum(present).reshape(1)
    pos = _positions(blk_start.reshape(N_EXPERTS).astype(jnp.int32), eidx_t, rank_t)
    pos_rows = pos.reshape(TOP_K, T_PART // SC_CHUNK, SC_CHUNK).transpose(1, 0, 2).reshape(-1, SC_CHUNK)
    xs = _sc_scatter_rows(h, pos_rows)
    ys = _experts(block_expert.reshape(NB).astype(jnp.int32), block_rows.reshape(NB).astype(jnp.int32),
                  n_used, block_run.astype(jnp.int32), run_expert.astype(jnp.int32), n_runs.astype(jnp.int32),
                  xs, w_gate, w_up, w_down, layer)
    return _sc_combine(ys, pos.T.reshape(-1), wsplat)


N_SLABS = (3 * C_FDIM + 2 * D_MODEL) // 128


def _hgrn_in_body(acc_ref, rt_ref, mod0_ref, mod1_ref, nmix_ref, win_ref, *rest):
    x_ref, z_ref = rest[-2:]
    x = acc_ref[...] + mod0_ref[5:6, :] * rt_ref[...]
    x_ref[...] = x
    hb = _modulate(x, nmix_ref[...], mod1_ref[0:1, :], mod1_ref[1:2, :]).astype(BF16)
    for s in range(N_SLABS // C_HEADS):
        zz = _bdot(hb, win_ref[:, s * D_MODEL:(s + 1) * D_MODEL])
        for hh in range(C_HEADS):
            z_ref[s * C_HEADS + hh] = zz[:, hh * 128:(hh + 1) * 128]


HGRN_IN_TILES = 2


def _hgrn_in(acc, routed, tile0, prev, mod0, mod1, norm_mix, w_in):
    g = HGRN_IN_TILES
    assert NT_PART % g == 0 and tile0 % g == 0 and NT_P % g == 0 and TILES_PER_LAT % g == 0
    rows = g * TM
    block0 = tile0 // g

    def shifted_mod():
        return pl.BlockSpec((None, 6, D_MODEL), lambda i: (_mod_row(i * g + tile0), 0, 0))

    local = pl.BlockSpec((rows, D_MODEL), lambda i: (i, 0))
    any_spec = pl.BlockSpec(memory_space=pl.ANY)
    prev = () if prev is None else tuple(prev)
    n_in = 6
    return pl.pallas_call(
        _hgrn_in_body,
        out_shape=(
            jax.ShapeDtypeStruct((T, D_MODEL), F32),
            jax.ShapeDtypeStruct((N_SLABS, T, 128), F32),
        ),
        grid=(NT_PART // g,),
        in_specs=[
            local, local, shifted_mod(), shifted_mod(),
            _const_spec((1, D_MODEL)),
            pl.BlockSpec((D_MODEL, 3 * C_FDIM + 2 * D_MODEL), lambda i: (0, 0), pipeline_mode=pl.Buffered(1)),
        ] + [any_spec] * len(prev),
        out_specs=(
            pl.BlockSpec((rows, D_MODEL), lambda i: (i + block0, 0)),
            pl.BlockSpec((N_SLABS, rows, 128), lambda i: (0, i + block0, 0)),
        ),
        input_output_aliases={n_in + k: k for k in range(len(prev))},
        compiler_params=_params("parallel"),
        name="hgrn_in_proj",
    )(acc, routed, mod0, mod1, norm_mix.reshape(1, -1), w_in.astype(BF16), *prev)


def _gla_body(q_ref, f_ref, v_ref, lb_ref, s0_ref, o_ref, ns_ref, st_ref, *, rev):
    j = pl.program_id(0)
    ti = NT - 1 - j if rev else j
    is_ctx = ti < NT_P
    first_lat = (ti - NT_P) % TILES_PER_LAT == (TILES_PER_LAT - 1 if rev else 0)

    @pl.when(is_ctx)
    def _():
        st_ref[...] = jnp.zeros(st_ref.shape, F32)

    @pl.when(jnp.logical_and(jnp.logical_not(is_ctx), first_lat))
    def _():
        st_ref[...] = s0_ref[...]

    row = lax.broadcasted_iota(jnp.int32, (TM, TM), 0)
    col = lax.broadcasted_iota(jnp.int32, (TM, TM), 1)
    same_chunk = (row // SCAN_CHUNK) == (col // SCAN_CHUNK)
    seen = jnp.logical_and(same_chunk, (col >= row) if rev else (col <= row))
    cum_w = seen.astype(BF16)
    mid = SCAN_CHUNK // 2 if rev else SCAN_CHUNK // 2 - 1
    last = 0 if rev else SCAN_CHUNK - 1
    n_chunks = TM // SCAN_CHUNK
    order = range(n_chunks - 1, -1, -1) if rev else range(n_chunks)
    group = 4

    def chunk_rows(b, off):
        return jnp.concatenate(
            [jnp.broadcast_to(b[c * SCAN_CHUNK + off:c * SCAN_CHUNK + off + 1, :], (SCAN_CHUNK, b.shape[1]))
             for c in range(n_chunks)], axis=0)

    def head_group(gi, carry):
        heads = [gi * group + u for u in range(group)]
        qs, kk, vv, bcum = [], [], [], []
        for hd in heads:
            lb = lb_ref[hd]
            qs.append(_silu(q_ref[hd]) * (C_DK ** -0.5))
            fg = lb + (1.0 - lb) * jax.nn.sigmoid(f_ref[hd])
            kk.append(1.0 - fg)
            vv.append(v_ref[hd].astype(BF16))
            g = jnp.log(fg)
            g_hi = g.astype(BF16)
            r1 = g - g_hi.astype(F32)
            g_mid = r1.astype(BF16)
            g_lo = (r1 - g_mid.astype(F32)).astype(BF16)
            bcum.append(_bdot(cum_w, g_hi) + _bdot(cum_w, g_mid) + _bdot(cum_w, g_lo))
        o_intra, q_dec, kv, decay = [], [], [], []
        for u in range(group):
            b_mid = chunk_rows(bcum[u], mid)
            b_last = chunk_rows(bcum[u], last)
            qe = (qs[u] * jnp.exp(bcum[u] - b_mid)).astype(BF16)
            ke = (kk[u] * jnp.exp(b_mid - bcum[u])).astype(BF16)
            att = jnp.where(seen, _dot_nt(qe, ke), 0.0)
            o_intra.append(_bdot(att.astype(BF16), vv[u]))
            q_dec.append((qs[u] * jnp.exp(bcum[u])).astype(BF16))
            k_dec = (kk[u] * jnp.exp(b_last - bcum[u])).astype(BF16)
            kv.append([_dot_tn(vv[u][c * SCAN_CHUNK:(c + 1) * SCAN_CHUNK], k_dec[c * SCAN_CHUNK:(c + 1) * SCAN_CHUNK])
                       for c in range(n_chunks)])
            decay.append([jnp.exp(bcum[u][c * SCAN_CHUNK + last:c * SCAN_CHUNK + last + 1, :])
                          for c in range(n_chunks)])
        st = [st_ref[hd] for hd in heads]
        for c in order:
            sl = slice(c * SCAN_CHUNK, (c + 1) * SCAN_CHUNK)
            for u, hd in enumerate(heads):
                o_ref[hd, pl.ds(c * SCAN_CHUNK, SCAN_CHUNK), :] = (
                    o_intra[u][sl] + _dot_nt(q_dec[u][sl], st[u].astype(BF16)))
                st[u] = decay[u][c] * st[u] + kv[u][c]
        for u, hd in enumerate(heads):
            st_ref[hd] = st[u]
        return carry

    lax.fori_loop(0, C_HEADS // group, head_group, 0)

    @pl.when(is_ctx)
    def _():
        ns_ref[...] = st_ref[...]


def _gla(z3, lb_dir, s0t_dir, *, rev):
    def ti_of(j):
        return NT - 1 - j if rev else j

    f_slab = 2 if rev else 1

    def lat_map(j):
        return (jnp.clip((ti_of(j) - NT_P) // TILES_PER_LAT, 0, DEC_BATCH - 1), 0, 0, 0)

    return pl.pallas_call(
        functools.partial(_gla_body, rev=rev),
        out_shape=(
            jax.ShapeDtypeStruct((C_HEADS, T, C_DV), F32),
            jax.ShapeDtypeStruct((BATCH, C_HEADS, C_DV, C_DK), F32),
        ),
        grid=(NT,),
        in_specs=[
            pl.BlockSpec((C_HEADS, TM, 128), lambda j: (0, ti_of(j), 0)),
            pl.BlockSpec((C_HEADS, TM, 128), lambda j: (f_slab, ti_of(j), 0)),
            pl.BlockSpec((C_HEADS, TM, 128), lambda j: (3, ti_of(j), 0)),
            _const_spec((C_HEADS, 1, C_DK)),
            pl.BlockSpec((None, C_HEADS, C_DV, C_DK), lat_map),
        ],
        out_specs=(
            pl.BlockSpec((C_HEADS, TM, C_DV), lambda j: (0, ti_of(j), 0)),
            pl.BlockSpec((None, C_HEADS, C_DV, C_DK),
                         lambda j: (jnp.minimum(ti_of(j), NT_P - 1), 0, 0, 0)),
        ),
        scratch_shapes=[pltpu.VMEM((C_HEADS, C_DV, C_DK), F32)],
        compiler_params=_params("arbitrary"),
        name="gla_bwd" if rev else "gla_fwd",
    )(z3, z3, z3, lb_dir, s0t_dir)


def _hgrn_out_body(ofw_ref, obw_ref, gate_ref, x_ref, mod_ref, no_ref, wout_ref, o_ref, cat_ref):
    for hd in range(C_HEADS):
        o = ofw_ref[hd] + obw_ref[hd]
        cat_ref[:, hd * C_DV:(hd + 1) * C_DV] = (_rms(o, no_ref[...]) * _silu(gate_ref[hd])).astype(BF16)
    o_ref[...] = x_ref[...] + mod_ref[2:3, :] * _bdot(cat_ref[...], wout_ref[...])


def _hgrn_out(o_fw, o_bw, z3, x, mod, norm_o, w_out):
    head_spec = pl.BlockSpec((C_HEADS, TM, C_DV), lambda i: (0, i, 0))
    return pl.pallas_call(
        _hgrn_out_body,
        out_shape=jax.ShapeDtypeStruct((T, D_MODEL), F32),
        grid=(NT,),
        in_specs=[
            head_spec, head_spec,
            pl.BlockSpec((C_HEADS, TM, 128), lambda i: (4, i, 0)),
            _tile_spec(), _mod_spec(),
            _const_spec((1, C_DV)),
            _const_spec((D_MODEL, D_MODEL)),
        ],
        out_specs=_tile_spec(),
        scratch_shapes=[pltpu.VMEM((TM, D_MODEL), BF16)],
        compiler_params=_params("parallel"),
        name="hgrn_out_proj",
    )(o_fw, o_bw, z3, x, mod, norm_o.reshape(1, -1), w_out.astype(BF16))


def _final_body(acc_ref, rt_ref, mod_ref, nf_ref, *rest):
    o_ref = rest[-1]
    o_ref[...] = _rms(acc_ref[...] + mod_ref[5:6, :] * rt_ref[...], nf_ref[...])


FINAL_TILES = 4


def _final(acc, routed, mod, norm_final, part_tile0, local0, n_tiles, out_tile0, out_tiles, prev=None):
    g = FINAL_TILES
    assert all(v % g == 0 for v in (part_tile0, local0, n_tiles, out_tile0, NT_P, TILES_PER_LAT))
    rows = g * TM
    local = pl.BlockSpec((rows, D_MODEL), lambda i: (i + local0 // g, 0))
    prev = () if prev is None else (prev,)
    return pl.pallas_call(
        _final_body,
        out_shape=jax.ShapeDtypeStruct((out_tiles * TM, D_MODEL), F32),
        grid=(n_tiles // g,),
        in_specs=[
            local, local,
            pl.BlockSpec((None, 6, D_MODEL), lambda i: (_mod_row(i * g + local0 + part_tile0), 0, 0)),
            _const_spec((1, D_MODEL)),
        ] + [pl.BlockSpec(memory_space=pl.ANY)] * len(prev),
        out_specs=pl.BlockSpec((rows, D_MODEL), lambda i: (i + out_tile0 // g, 0)),
        input_output_aliases={4 + k: 0 for k in range(len(prev))},
        compiler_params=_params("parallel"),
        name="final_norm",
    )(acc, routed, mod, norm_final.reshape(1, -1), *prev)


def kernel(x_prompt, x_sample, state_hgrn, c, c_ctx, w_ada, b_ada, norm_mix, norm_ffn, w_out, w_in_ab, w_sg, b_sg, norm_sg, w_dw, b_dw, norm_cv, w_in_hgrn, lb_raw, norm_o, w_router, b_router, w_gate, w_up, w_down, w_sh_gate, w_sh_up, w_sh_down, norm_final):
    cvecs = jnp.concatenate(
        [c_ctx.reshape(1, D_MODEL), c, jnp.zeros((N_MOD_ROWS - 1 - DEC_BATCH, D_MODEL), F32)], axis=0)
    mods = _ada_tables(cvecs, w_ada, b_ada)
    lb_sm = jax.nn.softmax(lb_raw.astype(F32), axis=0)
    lb1 = (jnp.cumsum(lb_sm, axis=0) - lb_sm[0])[1].reshape(2, C_HEADS, 1, C_DK)

    def moe(l, xin):
        parts = []
        for p in range(MOE_PARTS):
            h, eidx_t, w8_t, rank_t, counts, acc = _moe_pre(
                xin, mods[l], norm_ffn[l], w_router[l], b_router[l], w_sh_gate[l], w_sh_up[l], w_sh_down[l],
                p * NT_PART)
            wsplat = jnp.repeat(w8_t.T, SC_LANES, axis=1)
            parts += [acc, _moe_routed(h, eidx_t, rank_t, counts, wsplat, w_gate, w_up, w_down, l)]
        return parts

    x = _l0_mixer(x_prompt.reshape(T_P, D_MODEL), x_sample.reshape(T_S, D_MODEL), mods[0], norm_mix[0], w_in_ab[0], w_sg[0], b_sg[0], norm_sg[0], w_dw[0],
                  b_dw[0], norm_cv[0], w_out[0])
    parts = moe(0, x)
    xz = None
    for p in range(MOE_PARTS):
        xz = _hgrn_in(parts[2 * p], parts[2 * p + 1], p * NT_PART, xz, mods[0], mods[1], norm_mix[1], w_in_hgrn[0])
    x, z3 = xz
    s0t = jnp.swapaxes(state_hgrn[:, 0].astype(F32), -1, -2)
    o_fw, ns_fw = _gla(z3, lb1[0], s0t[:, 0], rev=False)
    o_bw, ns_bw = _gla(z3, lb1[1], s0t[:, 1], rev=True)
    x = _hgrn_out(o_fw, o_bw, z3, x, mods[1], norm_o[0], w_out[1])
    parts = moe(1, x)
    y_p = y_s = None
    for p in range(MOE_PARTS):
        lo, hi = p * NT_PART, (p + 1) * NT_PART
        if lo < NT_P:
            n = min(hi, NT_P) - lo
            y_p = _final(parts[2 * p], parts[2 * p + 1], mods[1], norm_final, lo, 0, n, lo, NT_P, y_p)
        if hi > NT_P:
            first = max(lo, NT_P)
            y_s = _final(parts[2 * p], parts[2 * p + 1], mods[1], norm_final, lo, first - lo, hi - first,
                         first - NT_P, NT_S, y_s)
    y_p = y_p.reshape(BATCH, SEQ, D_MODEL)
    y_s = y_s.reshape(DEC_BATCH, DEC_SEQ, D_MODEL)
    new_state = jnp.swapaxes(jnp.stack([ns_fw, ns_bw], axis=1), -1, -2)[:, None]
    return (y_p, y_s, new_state)
```

```python
import functools

import jax
import jax.numpy as jnp
from jax import lax
from jax.experimental import pallas as pl
from jax.experimental.pallas import tpu as pltpu
from jax.experimental.pallas import tpu_sc as plsc

F32 = jnp.float32
BF16 = jnp.bfloat16
HIGHEST = lax.Precision.HIGHEST

D_MODEL = 1024
BATCH = 32
SEQ = 256
DEPTH = 2
DEC_BATCH = 8
DEC_SEQ = 2048
GRID_W = 64
A_WIDTH = D_MODEL // 2
A_GROUPS = 4
A_GC = A_WIDTH // A_GROUPS
CHUNK_A = 128
B_WIDTH = D_MODEL - A_WIDTH
CONV_W = 31
CONV_PAD = CONV_W // 2
C_HEADS = 8
C_DK = 128
C_DV = D_MODEL // C_HEADS
C_FDIM = C_HEADS * C_DK
SCAN_CHUNK = 64
N_EXPERTS = 64
TOP_K = 8
N_GROUPS = 8
GROUP_SIZE = N_EXPERTS // N_GROUPS
TOPK_GROUPS = 4
D_EXPERT = 256
D_SHARED = 256
ROUTED_SCALE = 2.5
EPS = 1e-6

TM = 256
T_P = BATCH * SEQ
T_S = DEC_BATCH * DEC_SEQ
T = T_P + T_S
NT_P = T_P // TM
NT_S = T_S // TM
NT = NT_P + NT_S
TILES_PER_LAT = DEC_SEQ // TM
TE = 512
MOE_PARTS = 2
NT_PART = NT // MOE_PARTS
T_PART = NT_PART * TM
NB = T_PART * TOP_K // TE + N_EXPERTS
P_ROWS = NB * TE
D_PACK = D_MODEL // 2
N_MOD_ROWS = 16
CONV_HALO = 16
VMEM_LIMIT = 48 * 1024 * 1024
SC_CORES = 2
SC_SUBCORES = 16
SC_WORKERS = SC_CORES * SC_SUBCORES
SC_LANES = 16
SC_CHUNK = 64
COMBINE_TOKENS = SC_CHUNK // TOP_K


def _mod_row(i):
    return jnp.where(i < NT_P, 0, 1 + (i - NT_P) // TILES_PER_LAT)


def _silu(x):
    return x * jax.nn.sigmoid(x)


def _gelu(x):
    return x * (0.5 * (1.0 + jnp.tanh(0.7978845608028654 * (x + 0.044715 * (x * x * x)))))


def _rms(x, g):
    return x * lax.rsqrt(jnp.mean(x * x, axis=-1, keepdims=True) + EPS) * g


def _layernorm(x, g):
    xc = x - jnp.mean(x, axis=-1, keepdims=True)
    return xc * lax.rsqrt(jnp.mean(xc * xc, axis=-1, keepdims=True) + EPS) * g


def _modulate(x, g, shift, scale):
    return _rms(x, g) * (1.0 + scale) + shift


def _bdot(a, b):
    return jnp.dot(a, b, preferred_element_type=F32)


def _dot_nt(a, b, precision=None):
    return lax.dot_general(a, b, (((1,), (1,)), ((), ())), precision=precision,
                           preferred_element_type=F32)


def _dot_tn(a, b):
    return lax.dot_general(a, b, (((0,), (0,)), ((), ())), preferred_element_type=F32)


def _pack_pairs(x):
    m = x.shape[1] // 2
    lo = lax.bitcast_convert_type(x[:, :m].astype(BF16).astype(F32), jnp.uint32)
    hi = lax.bitcast_convert_type(x[:, m:].astype(BF16).astype(F32), jnp.uint32)
    return lax.bitcast_convert_type(hi | (lo >> 16), jnp.int32)


def _unpack_pairs(w):
    u = lax.bitcast_convert_type(w, jnp.uint32)
    lo = lax.bitcast_convert_type(u << 16, F32)
    hi = lax.bitcast_convert_type(u & jnp.uint32(0xFFFF0000), F32)
    return lo, hi


def _params(*sem):
    return pltpu.CompilerParams(dimension_semantics=sem, vmem_limit_bytes=VMEM_LIMIT)


def _const_spec(shape):
    nd = len(shape)
    return pl.BlockSpec(shape, lambda *_: (0,) * nd)


def _ada_body(c_ref, w_ref, b_ref, o_ref):
    s = _silu(c_ref[...])
    o_ref[...] = jnp.dot(s, w_ref[...], precision=HIGHEST, preferred_element_type=F32) + b_ref[...]


def _ada_tables(cvecs, w_ada, b_ada):
    out = pl.pallas_call(
        _ada_body,
        out_shape=jax.ShapeDtypeStruct((DEPTH, N_MOD_ROWS, 6 * D_MODEL), F32),
        grid=(DEPTH, 6),
        in_specs=[
            _const_spec((N_MOD_ROWS, D_MODEL)),
            pl.BlockSpec((None, D_MODEL, D_MODEL), lambda l, j: (l, 0, j)),
            pl.BlockSpec((None, 1, D_MODEL), lambda l, j: (l, 0, j)),
        ],
        out_specs=pl.BlockSpec((None, N_MOD_ROWS, D_MODEL), lambda l, j: (l, 0, j)),
        compiler_params=_params("parallel", "parallel"),
        name="ada_tables",
    )(cvecs, w_ada, b_ada.reshape(DEPTH, 1, 6 * D_MODEL))
    return out.reshape(DEPTH, N_MOD_ROWS, 6, D_MODEL)


def _mod_spec():
    return pl.BlockSpec((None, 6, D_MODEL), lambda i: (_mod_row(i), 0, 0))


def _tile_spec():
    return pl.BlockSpec((TM, D_MODEL), lambda i: (i, 0))


SUBLANES = 8


def _conv_segment(pad_ref, shift_ref, conv_ref, wdw_ref, pad_base, out_base, seg):
    rb = min(seg, 64)
    for cb in range(B_WIDTH // 128):
        cs = slice(cb * 128, (cb + 1) * 128)
        for r0 in range(0, seg, rb):
            acc = jnp.zeros((rb, 128), F32)
            for k in range(CONV_W):
                b = (CONV_HALO - CONV_PAD + k) % SUBLANES
                off = pad_base + r0 + CONV_HALO - CONV_PAD + k - b
                src = pad_ref if b == 0 else shift_ref.at[b - 1]
                acc = acc + wdw_ref[k:k + 1, cs] * src[off:off + rb, cs]
            conv_ref[out_base + r0:out_base + r0 + rb, cs] = acc


def _l0_body(xc_ref, xl_ref, mod_ref, nmix_ref, win_ref, wsg_ref, bsg_ref, nsg_ref, wdw_ref, bdw_ref,
             ncv_ref, wout_ref, o_ref, cat_ref, pad_ref, shift_ref, conv_ref):
    i = pl.program_id(0)
    x = jnp.where(i < NT_P, xc_ref[...], xl_ref[...])
    h = _modulate(x, nmix_ref[...], mod_ref[0:1, :], mod_ref[1:2, :])
    z = _bdot(h.astype(BF16), win_ref[...])
    u = _gelu(z[:, :A_WIDTH])
    vb = _layernorm(_gelu(z[:, A_WIDTH:2 * A_WIDTH]), nsg_ref[...]).astype(BF16)
    for n in range(TM // CHUNK_A):
        rs = slice(n * CHUNK_A, (n + 1) * CHUNK_A)
        for g in range(A_GROUPS):
            cs = slice(g * A_GC, (g + 1) * A_GC)
            m = _bdot(wsg_ref[g], vb[rs, cs]) + bsg_ref[g]
            cat_ref[rs, cs] = (u[rs, cs] * m).astype(BF16)
    hb = z[:, 2 * A_WIDTH:2 * A_WIDTH + B_WIDTH] * jax.nn.sigmoid(z[:, 2 * A_WIDTH + B_WIDTH:])

    def conv_tile(seg):
        stride = seg + 2 * CONV_HALO
        halo = jnp.zeros((CONV_HALO, B_WIDTH), F32)
        for s in range(TM // seg):
            b = s * stride
            pad_ref[b:b + CONV_HALO, :] = halo
            pad_ref[b + CONV_HALO:b + CONV_HALO + seg, :] = hb[s * seg:(s + 1) * seg, :]
            pad_ref[b + CONV_HALO + seg:b + stride, :] = halo
        rows = (TM // seg) * stride - SUBLANES
        for b in range(1, SUBLANES):
            shift_ref[b - 1, 0:rows, :] = pad_ref[b:b + rows, :]
        for s in range(TM // seg):
            _conv_segment(pad_ref, shift_ref, conv_ref, wdw_ref, s * stride, s * seg, seg)

    @pl.when(i < NT_P)
    def _():
        conv_tile(SEQ)

    @pl.when(i >= NT_P)
    def _():
        conv_tile(GRID_W)

    yb = _layernorm(conv_ref[...] + bdw_ref[...], ncv_ref[...])
    cat_ref[:, A_WIDTH:] = _silu(yb).astype(BF16)
    out = _bdot(cat_ref[...], wout_ref[...])
    o_ref[...] = x + mod_ref[2:3, :] * out


def _l0_mixer(x_ctx, x_lat, mod, norm_mix, w_in, w_sg, b_sg, norm_sg, w_dw, b_dw, norm_cv, w_out):
    assert TM == SEQ and TM % GRID_W == 0 and TM % CHUNK_A == 0
    pad_rows = (TM // GRID_W) * (GRID_W + 2 * CONV_HALO)
    assert pad_rows >= SEQ + 2 * CONV_HALO
    return pl.pallas_call(
        _l0_body,
        out_shape=jax.ShapeDtypeStruct((T, D_MODEL), F32),
        grid=(NT,),
        in_specs=[
            pl.BlockSpec((TM, D_MODEL), lambda i: (jnp.minimum(i, NT_P - 1), 0)),
            pl.BlockSpec((TM, D_MODEL), lambda i: (jnp.maximum(i - NT_P, 0), 0)),
            _mod_spec(),
            _const_spec((1, D_MODEL)),
            _const_spec((D_MODEL, 2 * A_WIDTH + 2 * B_WIDTH)),
            _const_spec((A_GROUPS, CHUNK_A, CHUNK_A)),
            _const_spec((A_GROUPS, CHUNK_A, 1)),
            _const_spec((1, A_WIDTH)),
            _const_spec((CONV_W, B_WIDTH)),
            _const_spec((1, B_WIDTH)),
            _const_spec((1, B_WIDTH)),
            _const_spec((D_MODEL, D_MODEL)),
        ],
        out_specs=_tile_spec(),
        scratch_shapes=[
            pltpu.VMEM((TM, D_MODEL), BF16),
            pltpu.VMEM((pad_rows, B_WIDTH), F32),
            pltpu.VMEM((SUBLANES - 1, pad_rows, B_WIDTH), F32),
            pltpu.VMEM((TM, B_WIDTH), F32),
        ],
        compiler_params=_params("parallel"),
        name="l0_mixer",
    )(x_ctx, x_lat, mod, norm_mix.reshape(1, -1), w_in.astype(BF16), w_sg.astype(BF16),
      b_sg.reshape(A_GROUPS, CHUNK_A, 1), norm_sg.reshape(1, -1), w_dw, b_dw.reshape(1, -1),
      norm_cv.reshape(1, -1), w_out.astype(BF16))


def _route(scores, biased):
    n = scores.shape[-1]
    shp = (N_GROUPS, GROUP_SIZE, n)
    s3 = scores.reshape(shp)
    b3 = biased.reshape(shp)
    m_iota = lax.broadcasted_iota(jnp.int32, shp, 1).astype(F32)
    g_iota = lax.broadcasted_iota(jnp.int32, shp, 0).astype(F32)
    e_iota = g_iota * GROUP_SIZE + m_iota
    neg = -jnp.inf

    def amax1(v):
        return jnp.max(v, axis=1, keepdims=True)

    def amin1(v):
        return jnp.min(v, axis=1, keepdims=True)

    m1 = amax1(b3)
    i1 = amin1(jnp.where(b3 == m1, m_iota, float(GROUP_SIZE)))
    m2 = amax1(jnp.where(m_iota == i1, neg, b3))
    grp = m1 + m2
    gi1 = lax.broadcasted_iota(jnp.int32, grp.shape, 0).astype(F32)
    gmask = jnp.zeros(grp.shape, jnp.bool_)
    for _ in range(TOPK_GROUPS):
        gm = jnp.max(grp, axis=0, keepdims=True)
        gi = jnp.min(jnp.where(grp == gm, gi1, float(N_GROUPS)), axis=0, keepdims=True)
        hit = gi1 == gi
        gmask = jnp.logical_or(gmask, hit)
        grp = jnp.where(hit, neg, grp)
    cand = jnp.where(gmask, b3, neg)
    ids, vals, hits = [], [], []
    for _ in range(TOP_K):
        mx = jnp.max(amax1(cand), axis=0, keepdims=True)
        ei = jnp.min(amin1(jnp.where(cand == mx, e_iota, float(N_EXPERTS))), axis=0, keepdims=True)
        hit = e_iota == ei
        ids.append(ei.reshape(1, n))
        vals.append(_pick(hit, s3))
        hits.append(hit)
        cand = jnp.where(hit, neg, cand)
    return jnp.concatenate(ids, axis=0).astype(jnp.int32), jnp.concatenate(vals, axis=0), hits


def _pick(hit, v3):
    s = jnp.sum(jnp.sum(jnp.where(hit, v3, 0.0), axis=1, keepdims=True), axis=0, keepdims=True)
    return s.reshape(1, v3.shape[-1])


def _moe_pre_body(x_ref, mod_ref, nffn_ref, wrt_ref, br_ref, wsg_ref, wsu_ref, wsd_ref,
                  h_ref, eidx_ref, w8_ref, rank_ref, cnt_ref, acc_ref, run_ref):
    @pl.when(pl.program_id(0) == 0)
    def _():
        run_ref[...] = jnp.zeros(run_ref.shape, F32)

    x = x_ref[...]
    h = _modulate(x, nffn_ref[...], mod_ref[3:4, :], mod_ref[4:5, :])
    h_ref[...] = _pack_pairs(h)
    hb = h.astype(BF16)
    h_lo = (h - hb.astype(F32)).astype(BF16)
    wr = wrt_ref[...]
    wr_hi = wr.astype(BF16)
    wr_lo = (wr - wr_hi.astype(F32)).astype(BF16)
    logits_t = _dot_nt(wr_hi, hb) + (_dot_nt(wr_hi, h_lo) + _dot_nt(wr_lo, hb))
    scores = jax.nn.sigmoid(logits_t)
    eidx, sv, hits = _route(scores, scores + br_ref[...])
    eidx_ref[...] = eidx
    w8_ref[...] = sv / jnp.sum(sv, axis=0, keepdims=True) * ROUTED_SCALE
    sel3 = hits[0]
    for hit in hits[1:]:
        sel3 = jnp.logical_or(sel3, hit)
    sel = sel3.astype(F32).reshape(N_EXPERTS, TM)
    earlier = (lax.broadcasted_iota(jnp.int32, (TM, TM), 0)
               < lax.broadcasted_iota(jnp.int32, (TM, TM), 1)).astype(BF16)
    rank3 = (_bdot(sel.astype(BF16), earlier) + run_ref[...]).reshape(N_GROUPS, GROUP_SIZE, TM)
    rank_ref[...] = jnp.concatenate([_pick(hit, rank3) for hit in hits], axis=0).astype(jnp.int32)
    run_ref[...] = run_ref[...] + jnp.sum(sel, axis=1, keepdims=True)
    cnt_ref[...] = run_ref[...].astype(jnp.int32)
    sh = _bdot((_silu(_bdot(hb, wsg_ref[...])) * _bdot(hb, wsu_ref[...])).astype(BF16), wsd_ref[...])
    acc_ref[...] = x + mod_ref[5:6, :] * sh


def _moe_pre(x, mod, norm_ffn, w_router, b_router, w_sh_gate, w_sh_up, w_sh_down, tile0):
    return pl.pallas_call(
        _moe_pre_body,
        out_shape=(
            jax.ShapeDtypeStruct((T_PART, D_PACK), jnp.int32),
            jax.ShapeDtypeStruct((TOP_K, T_PART), jnp.int32),
            jax.ShapeDtypeStruct((TOP_K, T_PART), F32),
            jax.ShapeDtypeStruct((TOP_K, T_PART), jnp.int32),
            jax.ShapeDtypeStruct((N_EXPERTS, 1), jnp.int32),
            jax.ShapeDtypeStruct((T_PART, D_MODEL), F32),
        ),
        grid=(NT_PART,),
        in_specs=[
            pl.BlockSpec((TM, D_MODEL), lambda i: (i + tile0, 0)),
            pl.BlockSpec((None, 6, D_MODEL), lambda i: (_mod_row(i + tile0), 0, 0)),
            _const_spec((1, D_MODEL)),
            _const_spec((N_EXPERTS, D_MODEL)),
            _const_spec((N_EXPERTS, 1)),
            _const_spec((D_MODEL, D_SHARED)),
            _const_spec((D_MODEL, D_SHARED)),
            _const_spec((D_SHARED, D_MODEL)),
        ],
        out_specs=(
            pl.BlockSpec((TM, D_PACK), lambda i: (i, 0)),
            pl.BlockSpec((TOP_K, TM), lambda i: (0, i)),
            pl.BlockSpec((TOP_K, TM), lambda i: (0, i)),
            pl.BlockSpec((TOP_K, TM), lambda i: (0, i)),
            _const_spec((N_EXPERTS, 1)),
            _tile_spec(),
        ),
        scratch_shapes=[pltpu.VMEM((N_EXPERTS, 1), F32)],
        compiler_params=_params("arbitrary"),
        name="moe_router_shared",
    )(x, mod, norm_ffn.reshape(1, -1), w_router.T, b_router.reshape(N_EXPERTS, 1),
      w_sh_gate.astype(BF16), w_sh_up.astype(BF16), w_sh_down.astype(BF16))


EXPERT_RING = 4


WEIGHT_SLOTS = 2


OUT_RING = 3


def _expert_body(be_ref, nv_ref, nu_ref, run_ref, rexp_ref, nrun_ref, xs_hbm, wg_hbm, wu_hbm, wd_hbm, ys_hbm,
                 wgu_s, wd_s, ring, sems, wg_buf, wu_buf, wd_buf, wsems, oring, osems, *, layer):
    j = pl.program_id(0)
    n_used = nu_ref[0]
    n_runs = nrun_ref[0]
    live_tile = j < n_used

    def fetch(t):
        slot = t % EXPERT_RING
        return pltpu.make_async_copy(xs_hbm.at[pl.ds(t * TE, TE)], ring.at[slot], sems.at[slot])

    def wfetch(r):
        slot = r % WEIGHT_SLOTS
        e = rexp_ref[r]
        return [pltpu.make_async_copy(wg_hbm.at[layer, e], wg_buf.at[slot], wsems.at[slot, 0]),
                pltpu.make_async_copy(wu_hbm.at[layer, e], wu_buf.at[slot], wsems.at[slot, 1]),
                pltpu.make_async_copy(wd_hbm.at[layer, e], wd_buf.at[slot], wsems.at[slot, 2])]

    def put(t):
        slot = t % OUT_RING
        return pltpu.make_async_copy(oring.at[slot], ys_hbm.at[pl.ds(t * TE, TE)], osems.at[slot])

    @pl.when(j == 0)
    def _():
        for t in range(EXPERT_RING - 1):
            @pl.when(t < n_used)
            def _():
                fetch(t).start()
        for r in range(WEIGHT_SLOTS):
            @pl.when(r < n_runs)
            def _():
                for cp in wfetch(r):
                    cp.start()

    @pl.when(j + (EXPERT_RING - 1) < n_used)
    def _():
        fetch(j + (EXPERT_RING - 1)).start()

    @pl.when(jnp.logical_and(live_tile, jnp.logical_or(j == 0, be_ref[j] != be_ref[jnp.maximum(j - 1, 0)])))
    def _():
        r = run_ref[j]
        slot = r % WEIGHT_SLOTS
        for cp in wfetch(r):
            cp.wait()
        wgu_s[:, :D_EXPERT] = wg_buf[slot].astype(BF16)
        wgu_s[:, D_EXPERT:] = wu_buf[slot].astype(BF16)
        wd_s[...] = wd_buf[slot].astype(BF16)

        @pl.when(r + WEIGHT_SLOTS < n_runs)
        def _():
            for cp in wfetch(r + WEIGHT_SLOTS):
                cp.start()

    @pl.when(live_tile)
    def _():
        fetch(j).wait()
        live = lax.broadcasted_iota(jnp.int32, (TE, 1), 0) < nv_ref[j]
        lo, hi = _unpack_pairs(jnp.where(live, ring[j % EXPERT_RING], 0))
        xb = jnp.concatenate([lo.astype(BF16), hi.astype(BF16)], axis=1)
        hgu = _bdot(xb, wgu_s[...])
        hh = _silu(hgu[:, :D_EXPERT]) * hgu[:, D_EXPERT:]
        y = _pack_pairs(_bdot(hh.astype(BF16), wd_s[...]))

        @pl.when(j >= OUT_RING)
        def _():
            put(j - OUT_RING).wait()

        oring[j % OUT_RING] = y
        put(j).start()

    @pl.when(j == pl.num_programs(0) - 1)
    def _():
        for d in range(OUT_RING):
            @pl.when(n_used - 1 - d >= 0)
            def _():
                put(n_used - 1 - d).wait()


def _experts(block_expert, block_rows, n_used, block_run, run_expert, n_runs, xs, w_gate, w_up, w_down, layer):
    any_spec = pl.BlockSpec(memory_space=pl.ANY)
    return pl.pallas_call(
        functools.partial(_expert_body, layer=layer),
        out_shape=jax.ShapeDtypeStruct((P_ROWS, D_PACK), jnp.int32),
        grid_spec=pltpu.PrefetchScalarGridSpec(
            num_scalar_prefetch=6,
            grid=(NB,),
            in_specs=[any_spec, any_spec, any_spec, any_spec],
            out_specs=any_spec,
            scratch_shapes=[
                pltpu.VMEM((D_MODEL, 2 * D_EXPERT), BF16),
                pltpu.VMEM((D_EXPERT, D_MODEL), BF16),
                pltpu.VMEM((EXPERT_RING, TE, D_PACK), jnp.int32),
                pltpu.SemaphoreType.DMA((EXPERT_RING,)),
                pltpu.VMEM((WEIGHT_SLOTS, D_MODEL, D_EXPERT), F32),
                pltpu.VMEM((WEIGHT_SLOTS, D_MODEL, D_EXPERT), F32),
                pltpu.VMEM((WEIGHT_SLOTS, D_EXPERT, D_MODEL), F32),
                pltpu.SemaphoreType.DMA((WEIGHT_SLOTS, 3)),
                pltpu.VMEM((OUT_RING, TE, D_PACK), jnp.int32),
                pltpu.SemaphoreType.DMA((OUT_RING,)),
            ],
        ),
        compiler_params=_params("arbitrary"),
        name="moe_experts",
    )(block_expert, block_rows, n_used, block_run, run_expert, n_runs, xs, w_gate, w_up, w_down)


def _positions_body(start_ref, eidx_ref, rank_ref, pos_ref):
    eidx = eidx_ref[...]
    base = jnp.zeros(eidx.shape, jnp.int32)
    for e in range(N_EXPERTS):
        base = jnp.where(eidx == e, start_ref[e], base)
    pos_ref[...] = base * TE + rank_ref[...]


def _positions(blk_start, eidx_t, rank_t):
    full = pl.BlockSpec((TOP_K, T_PART), lambda i, s: (0, 0))
    return pl.pallas_call(
        _positions_body,
        out_shape=jax.ShapeDtypeStruct((TOP_K, T_PART), jnp.int32),
        grid_spec=pltpu.PrefetchScalarGridSpec(
            num_scalar_prefetch=1, grid=(1,), in_specs=[full, full], out_specs=full),
        compiler_params=_params("arbitrary"),
        name="moe_positions",
    )(blk_start, eidx_t, rank_t)


def _sc_mesh():
    return plsc.VectorSubcoreMesh(core_axis_name="c", subcore_axis_name="s")


def _sc_worker():
    return lax.axis_index("s") * SC_CORES + lax.axis_index("c")


def _sc_scatter_rows(h, pos_rows):
    c = SC_CHUNK
    n_chunks = T_PART // SC_WORKERS // c
    width = h.shape[1]

    @functools.partial(
        pl.kernel, mesh=_sc_mesh(),
        out_type=jax.ShapeDtypeStruct((P_ROWS, width), h.dtype),
        scratch_types=[pltpu.VMEM((n_chunks * TOP_K, c), jnp.int32),
                       pltpu.VMEM((c, width), h.dtype), pltpu.VMEM((c, width), h.dtype)]
        + [pltpu.SemaphoreType.DMA] * 4,
        name="moe_dispatch_scatter",
    )
    def scatter(h_hbm, pos_hbm, xs_hbm, idx_v, buf0, buf1, sem_in0, sem_in1, sem_out0, sem_out1):
        assert n_chunks % 2 == 0
        bufs, sem_in, sem_out = (buf0, buf1), (sem_in0, sem_in1), (sem_out0, sem_out1)
        first = _sc_worker() * n_chunks
        pltpu.sync_copy(pos_hbm.at[pl.ds(first * TOP_K, n_chunks * TOP_K)], idx_v)

        def load(i, b):
            return pltpu.make_async_copy(h_hbm.at[pl.ds((first + i) * c, c)], bufs[b], sem_in[b])

        def puts(i, b):
            return [pltpu.make_async_copy(bufs[b], xs_hbm.at[idx_v.at[i * TOP_K + k]], sem_out[b])
                    for k in range(TOP_K)]

        load(0, 0).start()
        load(1, 1).start()

        @pl.loop(0, n_chunks, step=2)
        def _(i):
            for b in range(2):
                load(i + b, b).wait()
                for cp in puts(i + b, b):
                    cp.start()
            for b in range(2):
                for cp in puts(i + b, b):
                    cp.wait()

                @pl.when(i + 2 + b < n_chunks)
                def _():
                    load(i + 2 + b, b).start()

    return scatter(h, pos_rows)


def _sc_combine(ys, pos_tk, wsplat):
    ct = COMBINE_TOKENS
    rows = ct * TOP_K
    tok_per_worker = T_PART // SC_WORKERS
    n_chunks = tok_per_worker // ct
    width = ys.shape[1]
    vmem = pltpu.VMEM

    @functools.partial(
        pl.kernel, mesh=_sc_mesh(),
        out_type=jax.ShapeDtypeStruct((T_PART, D_MODEL), F32),
        scratch_types=[vmem((tok_per_worker * TOP_K,), jnp.int32),
                       vmem((rows, width), ys.dtype), vmem((rows, width), ys.dtype),
                       vmem((ct, TOP_K * SC_LANES), F32), vmem((ct, TOP_K * SC_LANES), F32),
                       vmem((ct, D_MODEL), F32), vmem((ct, D_MODEL), F32)]
        + [pltpu.SemaphoreType.DMA] * 6,
        compiler_params=pltpu.CompilerParams(needs_layout_passes=False),
        name="moe_combine",
    )
    def combine(ys_hbm, pos_hbm, w_hbm, out_hbm, idx_v, g0, g1, w0, w1, o0, o1,
                sem_g0, sem_g1, sem_w0, sem_w1, sem_o0, sem_o1):
        assert n_chunks % 2 == 0
        gbuf, wbuf, obuf = (g0, g1), (w0, w1), (o0, o1)
        sem_g, sem_w, sem_o = (sem_g0, sem_g1), (sem_w0, sem_w1), (sem_o0, sem_o1)
        tok0 = _sc_worker() * tok_per_worker
        pltpu.sync_copy(pos_hbm.at[pl.ds(tok0 * TOP_K, tok_per_worker * TOP_K)], idx_v)

        def fetch(i, b):
            return [pltpu.make_async_copy(ys_hbm.at[idx_v.at[pl.ds(i * rows, rows)]], gbuf[b], sem_g[b]),
                    pltpu.make_async_copy(w_hbm.at[pl.ds(tok0 + i * ct, ct)], wbuf[b], sem_w[b])]

        def flush(i, b):
            return pltpu.make_async_copy(obuf[b], out_hbm.at[pl.ds(tok0 + i * ct, ct)], sem_o[b])

        def reduce_chunk(b):
            @pl.loop(0, ct)
            def _(t):
                wv = [wbuf[b][t, pl.ds(k * SC_LANES, SC_LANES)] for k in range(TOP_K)]

                @pl.loop(0, width // SC_LANES)
                def _(j):
                    col = j * SC_LANES
                    los, his = [], []
                    for k in range(TOP_K):
                        word = gbuf[b][t * TOP_K + k, pl.ds(col, SC_LANES)]
                        los.append(wv[k] * lax.bitcast_convert_type(word << 16, F32))
                        his.append(wv[k] * lax.bitcast_convert_type(word & jnp.int32(-65536), F32))
                    while len(los) > 1:
                        los = [los[n] + los[n + 1] for n in range(0, len(los), 2)]
                        his = [his[n] + his[n + 1] for n in range(0, len(his), 2)]
                    obuf[b][t, pl.ds(col, SC_LANES)] = los[0]
                    obuf[b][t, pl.ds(D_PACK + col, SC_LANES)] = his[0]

        for b in range(2):
            for cp in fetch(b, b):
                cp.start()

        @pl.loop(0, n_chunks, step=2)
        def _(i):
            for b in range(2):
                for cp in fetch(i + b, b):
                    cp.wait()

                @pl.when(i > 0)
                def _():
                    flush(i + b - 2, b).wait()

                reduce_chunk(b)
                flush(i + b, b).start()

                @pl.when(i + 2 + b < n_chunks)
                def _():
                    for cp in fetch(i + 2 + b, b):
                        cp.start()

        for b in range(2):
            flush(n_chunks - 2 + b, b).wait()

    return combine(ys, pos_tk, wsplat)


def _moe_routed(h, eidx_t, rank_t, counts, wsplat, w_gate, w_up, w_down, layer):
    counts = counts.reshape(1, N_EXPERTS)
    nblk = (counts + TE - 1) // TE
    blk_end = jnp.cumsum(nblk, axis=1)
    blk_start = blk_end - nblk
    blocks = jnp.arange(NB, dtype=jnp.int32).reshape(NB, 1)
    block_expert = jnp.minimum(jnp.sum(blocks >= blk_end, axis=1, keepdims=True), N_EXPERTS - 1)
    mine = block_expert == jnp.arange(N_EXPERTS, dtype=jnp.int32).reshape(1, N_EXPERTS)
    cnt_b = jnp.sum(jnp.where(mine, counts, 0), axis=1, keepdims=True)
    start_b = jnp.sum(jnp.where(mine, blk_start, 0), axis=1, keepdims=True)
    block_rows = jnp.clip(cnt_b - (blocks - start_b) * TE, 0, TE)
    n_used = blk_end[0, -1].reshape(1).astype(jnp.int32)
    present = (nblk > 0).astype(jnp.int32)
    run_of_expert = jnp.cumsum(present, axis=1) - 1
    block_run = jnp.sum(jnp.where(mine, run_of_expert, 0), axis=1)
    experts = jnp.arange(N_EXPERTS, dtype=jnp.int32).reshape(1, N_EXPERTS)
    run_hit = jnp.logical_and(run_of_expert == experts.reshape(N_EXPERTS, 1), present > 0)
    run_expert = jnp.sum(jnp.where(run_hit, experts, 0), axis=1)
    n_runs = jnp.sum(present).reshape(1)
    pos = _positions(blk_start.reshape(N_EXPERTS).astype(jnp.int32), eidx_t, rank_t)
    pos_rows = pos.reshape(TOP_K, T_PART // SC_CHUNK, SC_CHUNK).transpose(1, 0, 2).reshape(-1, SC_CHUNK)
    xs = _sc_scatter_rows(h, pos_rows)
    ys = _experts(block_expert.reshape(NB).astype(jnp.int32), block_rows.reshape(NB).astype(jnp.int32),
                  n_used, block_run.astype(jnp.int32), run_expert.astype(jnp.int32), n_runs.astype(jnp.int32),
                  xs, w_gate, w_up, w_down, layer)
    return _sc_combine(ys, pos.T.reshape(-1), wsplat)


N_SLABS = (3 * C_FDIM + 2 * D_MODEL) // 128


def _hgrn_in_body(acc_ref, rt_ref, mod0_ref, mod1_ref, nmix_ref, win_ref, *rest):
    x_ref, z_ref = rest[-2:]
    x = acc_ref[...] + mod0_ref[5:6, :] * rt_ref[...]
    x_ref[...] = x
    hb = _modulate(x, nmix_ref[...], mod1_ref[0:1, :], mod1_ref[1:2, :]).astype(BF16)
    for s in range(N_SLABS // C_HEADS):
        zz = _bdot(hb, win_ref[:, s * D_MODEL:(s + 1) * D_MODEL])
        for hh in range(C_HEADS):
            z_ref[s * C_HEADS + hh] = zz[:, hh * 128:(hh + 1) * 128]


HGRN_IN_TILES = 2


def _hgrn_in(acc, routed, tile0, prev, mod0, mod1, norm_mix, w_in):
    g = HGRN_IN_TILES
    assert NT_PART % g == 0 and tile0 % g == 0 and NT_P % g == 0 and TILES_PER_LAT % g == 0
    rows = g * TM
    block0 = tile0 // g

    def shifted_mod():
        return pl.BlockSpec((None, 6, D_MODEL), lambda i: (_mod_row(i * g + tile0), 0, 0))

    local = pl.BlockSpec((rows, D_MODEL), lambda i: (i, 0))
    any_spec = pl.BlockSpec(memory_space=pl.ANY)
    prev = () if prev is None else tuple(prev)
    n_in = 6
    return pl.pallas_call(
        _hgrn_in_body,
        out_shape=(
            jax.ShapeDtypeStruct((T, D_MODEL), F32),
            jax.ShapeDtypeStruct((N_SLABS, T, 128), F32),
        ),
        grid=(NT_PART // g,),
        in_specs=[
            local, local, shifted_mod(), shifted_mod(),
            _const_spec((1, D_MODEL)),
            pl.BlockSpec((D_MODEL, 3 * C_FDIM + 2 * D_MODEL), lambda i: (0, 0), pipeline_mode=pl.Buffered(1)),
        ] + [any_spec] * len(prev),
        out_specs=(
            pl.BlockSpec((rows, D_MODEL), lambda i: (i + block0, 0)),
            pl.BlockSpec((N_SLABS, rows, 128), lambda i: (0, i + block0, 0)),
        ),
        input_output_aliases={n_in + k: k for k in range(len(prev))},
        compiler_params=_params("parallel"),
        name="hgrn_in_proj",
    )(acc, routed, mod0, mod1, norm_mix.reshape(1, -1), w_in.astype(BF16), *prev)


def _gla_body(q_ref, f_ref, v_ref, lb_ref, s0_ref, o_ref, ns_ref, st_ref, *, rev):
    j = pl.program_id(0)
    ti = NT - 1 - j if rev else j
    is_ctx = ti < NT_P
    first_lat = (ti - NT_P) % TILES_PER_LAT == (TILES_PER_LAT - 1 if rev else 0)

    @pl.when(is_ctx)
    def _():
        st_ref[...] = jnp.zeros(st_ref.shape, F32)

    @pl.when(jnp.logical_and(jnp.logical_not(is_ctx), first_lat))
    def _():
        st_ref[...] = s0_ref[...]

    row = lax.broadcasted_iota(jnp.int32, (TM, TM), 0)
    col = lax.broadcasted_iota(jnp.int32, (TM, TM), 1)
    same_chunk = (row // SCAN_CHUNK) == (col // SCAN_CHUNK)
    seen = jnp.logical_and(same_chunk, (col >= row) if rev else (col <= row))
    cum_w = seen.astype(BF16)
    mid = SCAN_CHUNK // 2 if rev else SCAN_CHUNK // 2 - 1
    last = 0 if rev else SCAN_CHUNK - 1
    n_chunks = TM // SCAN_CHUNK
    order = range(n_chunks - 1, -1, -1) if rev else range(n_chunks)
    group = 4

    def chunk_rows(b, off):
        return jnp.concatenate(
            [jnp.broadcast_to(b[c * SCAN_CHUNK + off:c * SCAN_CHUNK + off + 1, :], (SCAN_CHUNK, b.shape[1]))
             for c in range(n_chunks)], axis=0)

    def head_group(gi, carry):
        heads = [gi * group + u for u in range(group)]
        qs, kk, vv, bcum = [], [], [], []
        for hd in heads:
            lb = lb_ref[hd]
            qs.append(_silu(q_ref[hd]) * (C_DK ** -0.5))
            fg = lb + (1.0 - lb) * jax.nn.sigmoid(f_ref[hd])
            kk.append(1.0 - fg)
            vv.append(v_ref[hd].astype(BF16))
            g = jnp.log(fg)
            g_hi = g.astype(BF16)
            r1 = g - g_hi.astype(F32)
            g_mid = r1.astype(BF16)
            g_lo = (r1 - g_mid.astype(F32)).astype(BF16)
            bcum.append(_bdot(cum_w, g_hi) + _bdot(cum_w, g_mid) + _bdot(cum_w, g_lo))
        o_intra, q_dec, kv, decay = [], [], [], []
        for u in range(group):
            b_mid = chunk_rows(bcum[u], mid)
            b_last = chunk_rows(bcum[u], last)
            qe = (qs[u] * jnp.exp(bcum[u] - b_mid)).astype(BF16)
            ke = (kk[u] * jnp.exp(b_mid - bcum[u])).astype(BF16)
            att = jnp.where(seen, _dot_nt(qe, ke), 0.0)
            o_intra.append(_bdot(att.astype(BF16), vv[u]))
            q_dec.append((qs[u] * jnp.exp(bcum[u])).astype(BF16))
            k_dec = (kk[u] * jnp.exp(b_last - bcum[u])).astype(BF16)
            kv.append([_dot_tn(vv[u][c * SCAN_CHUNK:(c + 1) * SCAN_CHUNK], k_dec[c * SCAN_CHUNK:(c + 1) * SCAN_CHUNK])
                       for c in range(n_chunks)])
            decay.append([jnp.exp(bcum[u][c * SCAN_CHUNK + last:c * SCAN_CHUNK + last + 1, :])
                          for c in range(n_chunks)])
        st = [st_ref[hd] for hd in heads]
        for c in order:
            sl = slice(c * SCAN_CHUNK, (c + 1) * SCAN_CHUNK)
            for u, hd in enumerate(heads):
                o_ref[hd, pl.ds(c * SCAN_CHUNK, SCAN_CHUNK), :] = (
                    o_intra[u][sl] + _dot_nt(q_dec[u][sl], st[u].astype(BF16)))
                st[u] = decay[u][c] * st[u] + kv[u][c]
        for u, hd in enumerate(heads):
            st_ref[hd] = st[u]
        return carry

    lax.fori_loop(0, C_HEADS // group, head_group, 0)

    @pl.when(is_ctx)
    def _():
        ns_ref[...] = st_ref[...]


def _gla(z3, lb_dir, s0t_dir, *, rev):
    def ti_of(j):
        return NT - 1 - j if rev else j

    f_slab = 2 if rev else 1

    def lat_map(j):
        return (jnp.clip((ti_of(j) - NT_P) // TILES_PER_LAT, 0, DEC_BATCH - 1), 0, 0, 0)

    return pl.pallas_call(
        functools.partial(_gla_body, rev=rev),
        out_shape=(
            jax.ShapeDtypeStruct((C_HEADS, T, C_DV), F32),
            jax.ShapeDtypeStruct((BATCH, C_HEADS, C_DV, C_DK), F32),
        ),
        grid=(NT,),
        in_specs=[
            pl.BlockSpec((C_HEADS, TM, 128), lambda j: (0, ti_of(j), 0)),
            pl.BlockSpec((C_HEADS, TM, 128), lambda j: (f_slab, ti_of(j), 0)),
            pl.BlockSpec((C_HEADS, TM, 128), lambda j: (3, ti_of(j), 0)),
            _const_spec((C_HEADS, 1, C_DK)),
            pl.BlockSpec((None, C_HEADS, C_DV, C_DK), lat_map),
        ],
        out_specs=(
            pl.BlockSpec((C_HEADS, TM, C_DV), lambda j: (0, ti_of(j), 0)),
            pl.BlockSpec((None, C_HEADS, C_DV, C_DK),
                         lambda j: (jnp.minimum(ti_of(j), NT_P - 1), 0, 0, 0)),
        ),
        scratch_shapes=[pltpu.VMEM((C_HEADS, C_DV, C_DK), F32)],
        compiler_params=_params("arbitrary"),
        name="gla_bwd" if rev else "gla_fwd",
    )(z3, z3, z3, lb_dir, s0t_dir)


def _hgrn_out_body(ofw_ref, obw_ref, gate_ref, x_ref, mod_ref, no_ref, wout_ref, o_ref, cat_ref):
    for hd in range(C_HEADS):
        o = ofw_ref[hd] + obw_ref[hd]
        cat_ref[:, hd * C_DV:(hd + 1) * C_DV] = (_rms(o, no_ref[...]) * _silu(gate_ref[hd])).astype(BF16)
    o_ref[...] = x_ref[...] + mod_ref[2:3, :] * _bdot(cat_ref[...], wout_ref[...])


def _hgrn_out(o_fw, o_bw, z3, x, mod, norm_o, w_out):
    head_spec = pl.BlockSpec((C_HEADS, TM, C_DV), lambda i: (0, i, 0))
    return pl.pallas_call(
        _hgrn_out_body,
        out_shape=jax.ShapeDtypeStruct((T, D_MODEL), F32),
        grid=(NT,),
        in_specs=[
            head_spec, head_spec,
            pl.BlockSpec((C_HEADS, TM, 128), lambda i: (4, i, 0)),
            _tile_spec(), _mod_spec(),
            _const_spec((1, C_DV)),
            _const_spec((D_MODEL, D_MODEL)),
        ],
        out_specs=_tile_spec(),
        scratch_shapes=[pltpu.VMEM((TM, D_MODEL), BF16)],
        compiler_params=_params("parallel"),
        name="hgrn_out_proj",
    )(o_fw, o_bw, z3, x, mod, norm_o.reshape(1, -1), w_out.astype(BF16))


def _final_body(acc_ref, rt_ref, mod_ref, nf_ref, *rest):
    o_ref = rest[-1]
    o_ref[...] = _rms(acc_ref[...] + mod_ref[5:6, :] * rt_ref[...], nf_ref[...])


FINAL_TILES = 4


def _final(acc, routed, mod, norm_final, part_tile0, local0, n_tiles, out_tile0, out_tiles, prev=None):
    g = FINAL_TILES
    assert all(v % g == 0 for v in (part_tile0, local0, n_tiles, out_tile0, NT_P, TILES_PER_LAT))
    rows = g * TM
    local = pl.BlockSpec((rows, D_MODEL), lambda i: (i + local0 // g, 0))
    prev = () if prev is None else (prev,)
    return pl.pallas_call(
        _final_body,
        out_shape=jax.ShapeDtypeStruct((out_tiles * TM, D_MODEL), F32),
        grid=(n_tiles // g,),
        in_specs=[
            local, local,
            pl.BlockSpec((None, 6, D_MODEL), lambda i: (_mod_row(i * g + local0 + part_tile0), 0, 0)),
            _const_spec((1, D_MODEL)),
        ] + [pl.BlockSpec(memory_space=pl.ANY)] * len(prev),
        out_specs=pl.BlockSpec((rows, D_MODEL), lambda i: (i + out_tile0 // g, 0)),
        input_output_aliases={4 + k: 0 for k in range(len(prev))},
        compiler_params=_params("parallel"),
        name="final_norm",
    )(acc, routed, mod, norm_final.reshape(1, -1), *prev)


def kernel(x_prompt, x_sample, state_hgrn, c, c_ctx, w_ada, b_ada, norm_mix, norm_ffn, w_out, w_in_ab, w_sg, b_sg, norm_sg, w_dw, b_dw, norm_cv, w_in_hgrn, lb_raw, norm_o, w_router, b_router, w_gate, w_up, w_down, w_sh_gate, w_sh_up, w_sh_down, norm_final):
    cvecs = jnp.concatenate(
        [c_ctx.reshape(1, D_MODEL), c, jnp.zeros((N_MOD_ROWS - 1 - DEC_BATCH, D_MODEL), F32)], axis=0)
    mods = _ada_tables(cvecs, w_ada, b_ada)
    lb_sm = jax.nn.softmax(lb_raw.astype(F32), axis=0)
    lb1 = (jnp.cumsum(lb_sm, axis=0) - lb_sm[0])[1].reshape(2, C_HEADS, 1, C_DK)

    def moe(l, xin):
        parts = []
        for p in range(MOE_PARTS):
            h, eidx_t, w8_t, rank_t, counts, acc = _moe_pre(
                xin, mods[l], norm_ffn[l], w_router[l], b_router[l], w_sh_gate[l], w_sh_up[l], w_sh_down[l],
                p * NT_PART)
            wsplat = jnp.repeat(w8_t.T, SC_LANES, axis=1)
            parts += [acc, _moe_routed(h, eidx_t, rank_t, counts, wsplat, w_gate, w_up, w_down, l)]
        return parts

    x = _l0_mixer(x_prompt.reshape(T_P, D_MODEL), x_sample.reshape(T_S, D_MODEL), mods[0], norm_mix[0], w_in_ab[0], w_sg[0], b_sg[0], norm_sg[0], w_dw[0],
                  b_dw[0], norm_cv[0], w_out[0])
    parts = moe(0, x)
    xz = None
    for p in range(MOE_PARTS):
        xz = _hgrn_in(parts[2 * p], parts[2 * p + 1], p * NT_PART, xz, mods[0], mods[1], norm_mix[1], w_in_hgrn[0])
    x, z3 = xz
    s0t = jnp.swapaxes(state_hgrn[:, 0].astype(F32), -1, -2)
    o_fw, ns_fw = _gla(z3, lb1[0], s0t[:, 0], rev=False)
    o_bw, ns_bw = _gla(z3, lb1[1], s0t[:, 1], rev=True)
    x = _hgrn_out(o_fw, o_bw, z3, x, mods[1], norm_o[0], w_out[1])
    parts = moe(1, x)
    y_p = y_s = None
    for p in range(MOE_PARTS):
        lo, hi = p * NT_PART, (p + 1) * NT_PART
        if lo < NT_P:
            n = min(hi, NT_P) - lo
            y_p = _final(parts[2 * p], parts[2 * p + 1], mods[1], norm_final, lo, 0, n, lo, NT_P, y_p)
        if hi > NT_P:
            first = max(lo, NT_P)
            y_s = _final(parts[2 * p], parts[2 * p + 1], mods[1], norm_final, lo, first - lo, hi - first,
                         first - NT_P, NT_S, y_s)
    y_p = y_p.reshape(BATCH, SEQ, D_MODEL)
    y_s = y_s.reshape(DEC_BATCH, DEC_SEQ, D_MODEL)
    new_state = jnp.swapaxes(jnp.stack([ns_fw, ns_bw], axis=1), -1, -2)[:, None]
    return (y_p, y_s, new_state)
```

```python
import functools

import jax
import jax.numpy as jnp
from jax import lax
from jax.experimental import pallas as pl
from jax.experimental.pallas import tpu as pltpu
from jax.experimental.pallas import tpu_sc as plsc

F32 = jnp.float32
BF16 = jnp.bfloat16
HIGHEST = lax.Precision.HIGHEST

D_MODEL = 1024
BATCH = 32
SEQ = 256
DEPTH = 2
DEC_BATCH = 8
DEC_SEQ = 2048
GRID_W = 64
A_WIDTH = D_MODEL // 2
A_GROUPS = 4
A_GC = A_WIDTH // A_GROUPS
CHUNK_A = 128
B_WIDTH = D_MODEL - A_WIDTH
CONV_W = 31
CONV_PAD = CONV_W // 2
C_HEADS = 8
C_DK = 128
C_DV = D_MODEL // C_HEADS
C_FDIM = C_HEADS * C_DK
SCAN_CHUNK = 64
N_EXPERTS = 64
TOP_K = 8
N_GROUPS = 8
GROUP_SIZE = N_EXPERTS // N_GROUPS
TOPK_GROUPS = 4
D_EXPERT = 256
D_SHARED = 256
ROUTED_SCALE = 2.5
EPS = 1e-6

TM = 256
T_P = BATCH * SEQ
T_S = DEC_BATCH * DEC_SEQ
T = T_P + T_S
NT_P = T_P // TM
NT_S = T_S // TM
NT = NT_P + NT_S
TILES_PER_LAT = DEC_SEQ // TM
TE = 512
MOE_PARTS = 2
NT_PART = NT // MOE_PARTS
T_PART = NT_PART * TM
NB = T_PART * TOP_K // TE + N_EXPERTS
P_ROWS = NB * TE
D_PACK = D_MODEL // 2
N_MOD_ROWS = 16
CONV_HALO = 16
VMEM_LIMIT = 48 * 1024 * 1024
SC_CORES = 2
SC_SUBCORES = 16
SC_WORKERS = SC_CORES * SC_SUBCORES
SC_LANES = 16
SC_CHUNK = 64
COMBINE_TOKENS = SC_CHUNK // TOP_K


def _mod_row(i):
    return jnp.where(i < NT_P, 0, 1 + (i - NT_P) // TILES_PER_LAT)


def _silu(x):
    return x * jax.nn.sigmoid(x)


def _gelu(x):
    return x * (0.5 * (1.0 + jnp.tanh(0.7978845608028654 * (x + 0.044715 * (x * x * x)))))


def _rms(x, g):
    return x * lax.rsqrt(jnp.mean(x * x, axis=-1, keepdims=True) + EPS) * g


def _layernorm(x, g):
    xc = x - jnp.mean(x, axis=-1, keepdims=True)
    return xc * lax.rsqrt(jnp.mean(xc * xc, axis=-1, keepdims=True) + EPS) * g


def _modulate(x, g, shift, scale):
    return _rms(x, g) * (1.0 + scale) + shift


def _bdot(a, b):
    return jnp.dot(a, b, preferred_element_type=F32)


def _dot_nt(a, b, precision=None):
    return lax.dot_general(a, b, (((1,), (1,)), ((), ())), precision=precision,
                           preferred_element_type=F32)


def _dot_tn(a, b):
    return lax.dot_general(a, b, (((0,), (0,)), ((), ())), preferred_element_type=F32)


def _pack_pairs(x):
    m = x.shape[1] // 2
    lo = lax.bitcast_convert_type(x[:, :m].astype(BF16).astype(F32), jnp.uint32)
    hi = lax.bitcast_convert_type(x[:, m:].astype(BF16).astype(F32), jnp.uint32)
    return lax.bitcast_convert_type(hi | (lo >> 16), jnp.int32)


def _unpack_pairs(w):
    u = lax.bitcast_convert_type(w, jnp.uint32)
    lo = lax.bitcast_convert_type(u << 16, F32)
    hi = lax.bitcast_convert_type(u & jnp.uint32(0xFFFF0000), F32)
    return lo, hi


def _params(*sem):
    return pltpu.CompilerParams(dimension_semantics=sem, vmem_limit_bytes=VMEM_LIMIT)


def _const_spec(shape):
    nd = len(shape)
    return pl.BlockSpec(shape, lambda *_: (0,) * nd)


def _ada_body(c_ref, w_ref, b_ref, o_ref):
    s = _silu(c_ref[...])
    o_ref[...] = jnp.dot(s, w_ref[...], precision=HIGHEST, preferred_element_type=F32) + b_ref[...]


def _ada_tables(cvecs, w_ada, b_ada):
    out = pl.pallas_call(
        _ada_body,
        out_shape=jax.ShapeDtypeStruct((DEPTH, N_MOD_ROWS, 6 * D_MODEL), F32),
        grid=(DEPTH, 6),
        in_specs=[
            _const_spec((N_MOD_ROWS, D_MODEL)),
            pl.BlockSpec((None, D_MODEL, D_MODEL), lambda l, j: (l, 0, j)),
            pl.BlockSpec((None, 1, D_MODEL), lambda l, j: (l, 0, j)),
        ],
        out_specs=pl.BlockSpec((None, N_MOD_ROWS, D_MODEL), lambda l, j: (l, 0, j)),
        compiler_params=_params("parallel", "parallel"),
        name="ada_tables",
    )(cvecs, w_ada, b_ada.reshape(DEPTH, 1, 6 * D_MODEL))
    return out.reshape(DEPTH, N_MOD_ROWS, 6, D_MODEL)


def _mod_spec():
    return pl.BlockSpec((None, 6, D_MODEL), lambda i: (_mod_row(i), 0, 0))


def _tile_spec():
    return pl.BlockSpec((TM, D_MODEL), lambda i: (i, 0))


SUBLANES = 8


def _conv_segment(pad_ref, shift_ref, conv_ref, wdw_ref, pad_base, out_base, seg):
    rb = min(seg, 64)
    for cb in range(B_WIDTH // 128):
        cs = slice(cb * 128, (cb + 1) * 128)
        for r0 in range(0, seg, rb):
            acc = jnp.zeros((rb, 128), F32)
            for k in range(CONV_W):
                b = (CONV_HALO - CONV_PAD + k) % SUBLANES
                off = pad_base + r0 + CONV_HALO - CONV_PAD + k - b
                src = pad_ref if b == 0 else shift_ref.at[b - 1]
                acc = acc + wdw_ref[k:k + 1, cs] * src[off:off + rb, cs]
            conv_ref[out_base + r0:out_base + r0 + rb, cs] = acc


def _l0_body(xc_ref, xl_ref, mod_ref, nmix_ref, win_ref, wsg_ref, bsg_ref, nsg_ref, wdw_ref, bdw_ref,
             ncv_ref, wout_ref, o_ref, cat_ref, pad_ref, shift_ref, conv_ref):
    i = pl.program_id(0)
    x = jnp.where(i < NT_P, xc_ref[...], xl_ref[...])
    h = _modulate(x, nmix_ref[...], mod_ref[0:1, :], mod_ref[1:2, :])
    z = _bdot(h.astype(BF16), win_ref[...])
    u = _gelu(z[:, :A_WIDTH])
    vb = _layernorm(_gelu(z[:, A_WIDTH:2 * A_WIDTH]), nsg_ref[...]).astype(BF16)
    for n in range(TM // CHUNK_A):
        rs = slice(n * CHUNK_A, (n + 1) * CHUNK_A)
        for g in range(A_GROUPS):
            cs = slice(g * A_GC, (g + 1) * A_GC)
            m = _bdot(wsg_ref[g], vb[rs, cs]) + bsg_ref[g]
            cat_ref[rs, cs] = (u[rs, cs] * m).astype(BF16)
    hb = z[:, 2 * A_WIDTH:2 * A_WIDTH + B_WIDTH] * jax.nn.sigmoid(z[:, 2 * A_WIDTH + B_WIDTH:])

    def conv_tile(seg):
        stride = seg + 2 * CONV_HALO
        halo = jnp.zeros((CONV_HALO, B_WIDTH), F32)
        for s in range(TM // seg):
            b = s * stride
            pad_ref[b:b + CONV_HALO, :] = halo
            pad_ref[b + CONV_HALO:b + CONV_HALO + seg, :] = hb[s * seg:(s + 1) * seg, :]
            pad_ref[b + CONV_HALO + seg:b + stride, :] = halo
        rows = (TM // seg) * stride - SUBLANES
        for b in range(1, SUBLANES):
            shift_ref[b - 1, 0:rows, :] = pad_ref[b:b + rows, :]
        for s in range(TM // seg):
            _conv_segment(pad_ref, shift_ref, conv_ref, wdw_ref, s * stride, s * seg, seg)

    @pl.when(i < NT_P)
    def _():
        conv_tile(SEQ)

    @pl.when(i >= NT_P)
    def _():
        conv_tile(GRID_W)

    yb = _layernorm(conv_ref[...] + bdw_ref[...], ncv_ref[...])
    cat_ref[:, A_WIDTH:] = _silu(yb).astype(BF16)
    out = _bdot(cat_ref[...], wout_ref[...])
    o_ref[...] = x + mod_ref[2:3, :] * out


def _l0_mixer(x_ctx, x_lat, mod, norm_mix, w_in, w_sg, b_sg, norm_sg, w_dw, b_dw, norm_cv, w_out):
    assert TM == SEQ and TM % GRID_W == 0 and TM % CHUNK_A == 0
    pad_rows = (TM // GRID_W) * (GRID_W + 2 * CONV_HALO)
    assert pad_rows >= SEQ + 2 * CONV_HALO
    return pl.pallas_call(
        _l0_body,
        out_shape=jax.ShapeDtypeStruct((T, D_MODEL), F32),
        grid=(NT,),
        in_specs=[
            pl.BlockSpec((TM, D_MODEL), lambda i: (jnp.minimum(i, NT_P - 1), 0)),
            pl.BlockSpec((TM, D_MODEL), lambda i: (jnp.maximum(i - NT_P, 0), 0)),
            _mod_spec(),
            _const_spec((1, D_MODEL)),
            _const_spec((D_MODEL, 2 * A_WIDTH + 2 * B_WIDTH)),
            _const_spec((A_GROUPS, CHUNK_A, CHUNK_A)),
            _const_spec((A_GROUPS, CHUNK_A, 1)),
            _const_spec((1, A_WIDTH)),
            _const_spec((CONV_W, B_WIDTH)),
            _const_spec((1, B_WIDTH)),
            _const_spec((1, B_WIDTH)),
            _const_spec((D_MODEL, D_MODEL)),
        ],
        out_specs=_tile_spec(),
        scratch_shapes=[
            pltpu.VMEM((TM, D_MODEL), BF16),
            pltpu.VMEM((pad_rows, B_WIDTH), F32),
            pltpu.VMEM((SUBLANES - 1, pad_rows, B_WIDTH), F32),
            pltpu.VMEM((TM, B_WIDTH), F32),
        ],
        compiler_params=_params("parallel"),
        name="l0_mixer",
    )(x_ctx, x_lat, mod, norm_mix.reshape(1, -1), w_in.astype(BF16), w_sg.astype(BF16),
      b_sg.reshape(A_GROUPS, CHUNK_A, 1), norm_sg.reshape(1, -1), w_dw, b_dw.reshape(1, -1),
      norm_cv.reshape(1, -1), w_out.astype(BF16))


def _route(scores, biased):
    n = scores.shape[-1]
    shp = (N_GROUPS, GROUP_SIZE, n)
    s3 = scores.reshape(shp)
    b3 = biased.reshape(shp)
    m_iota = lax.broadcasted_iota(jnp.int32, shp, 1).astype(F32)
    g_iota = lax.broadcasted_iota(jnp.int32, shp, 0).astype(F32)
    e_iota = g_iota * GROUP_SIZE + m_iota
    neg = -jnp.inf

    def amax1(v):
        return jnp.max(v, axis=1, keepdims=True)

    def amin1(v):
        return jnp.min(v, axis=1, keepdims=True)

    m1 = amax1(b3)
    i1 = amin1(jnp.where(b3 == m1, m_iota, float(GROUP_SIZE)))
    m2 = amax1(jnp.where(m_iota == i1, neg, b3))
    grp = m1 + m2
    gi1 = lax.broadcasted_iota(jnp.int32, grp.shape, 0).astype(F32)
    gmask = jnp.zeros(grp.shape, jnp.bool_)
    for _ in range(TOPK_GROUPS):
        gm = jnp.max(grp, axis=0, keepdims=True)
        gi = jnp.min(jnp.where(grp == gm, gi1, float(N_GROUPS)), axis=0, keepdims=True)
        hit = gi1 == gi
        gmask = jnp.logical_or(gmask, hit)
        grp = jnp.where(hit, neg, grp)
    cand = jnp.where(gmask, b3, neg)
    ids, vals, hits = [], [], []
    for _ in range(TOP_K):
        mx = jnp.max(amax1(cand), axis=0, keepdims=True)
        ei = jnp.min(amin1(jnp.where(cand == mx, e_iota, float(N_EXPERTS))), axis=0, keepdims=True)
        hit = e_iota == ei
        ids.append(ei.reshape(1, n))
        vals.append(_pick(hit, s3))
        hits.append(hit)
        cand = jnp.where(hit, neg, cand)
    return jnp.concatenate(ids, axis=0).astype(jnp.int32), jnp.concatenate(vals, axis=0), hits


def _pick(hit, v3):
    s = jnp.sum(jnp.sum(jnp.where(hit, v3, 0.0), axis=1, keepdims=True), axis=0, keepdims=True)
    return s.reshape(1, v3.shape[-1])


def _moe_pre_body(x_ref, mod_ref, nffn_ref, wrt_ref, br_ref, wsg_ref, wsu_ref, wsd_ref,
                  h_ref, eidx_ref, w8_ref, rank_ref, cnt_ref, acc_ref, run_ref):
    @pl.when(pl.program_id(0) == 0)
    def _():
        run_ref[...] = jnp.zeros(run_ref.shape, F32)

    x = x_ref[...]
    h = _modulate(x, nffn_ref[...], mod_ref[3:4, :], mod_ref[4:5, :])
    h_ref[...] = _pack_pairs(h)
    hb = h.astype(BF16)
    h_lo = (h - hb.astype(F32)).astype(BF16)
    wr = wrt_ref[...]
    wr_hi = wr.astype(BF16)
    wr_lo = (wr - wr_hi.astype(F32)).astype(BF16)
    logits_t = _dot_nt(wr_hi, hb) + (_dot_nt(wr_hi, h_lo) + _dot_nt(wr_lo, hb))
    scores = jax.nn.sigmoid(logits_t)
    eidx, sv, hits = _route(scores, scores + br_ref[...])
    eidx_ref[...] = eidx
    w8_ref[...] = sv / jnp.sum(sv, axis=0, keepdims=True) * ROUTED_SCALE
    sel3 = hits[0]
    for hit in hits[1:]:
        sel3 = jnp.logical_or(sel3, hit)
    sel = sel3.astype(F32).reshape(N_EXPERTS, TM)
    earlier = (lax.broadcasted_iota(jnp.int32, (TM, TM), 0)
               < lax.broadcasted_iota(jnp.int32, (TM, TM), 1)).astype(BF16)
    rank3 = (_bdot(sel.astype(BF16), earlier) + run_ref[...]).reshape(N_GROUPS, GROUP_SIZE, TM)
    rank_ref[...] = jnp.concatenate([_pick(hit, rank3) for hit in hits], axis=0).astype(jnp.int32)
    run_ref[...] = run_ref[...] + jnp.sum(sel, axis=1, keepdims=True)
    cnt_ref[...] = run_ref[...].astype(jnp.int32)
    sh = _bdot((_silu(_bdot(hb, wsg_ref[...])) * _bdot(hb, wsu_ref[...])).astype(BF16), wsd_ref[...])
    acc_ref[...] = x + mod_ref[5:6, :] * sh


def _moe_pre(x, mod, norm_ffn, w_router, b_router, w_sh_gate, w_sh_up, w_sh_down, tile0):
    return pl.pallas_call(
        _moe_pre_body,
        out_shape=(
            jax.ShapeDtypeStruct((T_PART, D_PACK), jnp.int32),
            jax.ShapeDtypeStruct((TOP_K, T_PART), jnp.int32),
            jax.ShapeDtypeStruct((TOP_K, T_PART), F32),
            jax.ShapeDtypeStruct((TOP_K, T_PART), jnp.int32),
            jax.ShapeDtypeStruct((N_EXPERTS, 1), jnp.int32),
            jax.ShapeDtypeStruct((T_PART, D_MODEL), F32),
        ),
        grid=(NT_PART,),
        in_specs=[
            pl.BlockSpec((TM, D_MODEL), lambda i: (i + tile0, 0)),
            pl.BlockSpec((None, 6, D_MODEL), lambda i: (_mod_row(i + tile0), 0, 0)),
            _const_spec((1, D_MODEL)),
            _const_spec((N_EXPERTS, D_MODEL)),
            _const_spec((N_EXPERTS, 1)),
            _const_spec((D_MODEL, D_SHARED)),
            _const_spec((D_MODEL, D_SHARED)),
            _const_spec((D_SHARED, D_MODEL)),
        ],
        out_specs=(
            pl.BlockSpec((TM, D_PACK), lambda i: (i, 0)),
            pl.BlockSpec((TOP_K, TM), lambda i: (0, i)),
            pl.BlockSpec((TOP_K, TM), lambda i: (0, i)),
            pl.BlockSpec((TOP_K, TM), lambda i: (0, i)),
            _const_spec((N_EXPERTS, 1)),
            _tile_spec(),
        ),
        scratch_shapes=[pltpu.VMEM((N_EXPERTS, 1), F32)],
        compiler_params=_params("arbitrary"),
        name="moe_router_shared",
    )(x, mod, norm_ffn.reshape(1, -1), w_router.T, b_router.reshape(N_EXPERTS, 1),
      w_sh_gate.astype(BF16), w_sh_up.astype(BF16), w_sh_down.astype(BF16))


EXPERT_RING = 4


WEIGHT_SLOTS = 2


def _expert_body(be_ref, nv_ref, nu_ref, run_ref, rexp_ref, nrun_ref, xs_hbm, wg_hbm, wu_hbm, wd_hbm, ys_ref,
                 wgu_s, wd_s, ring, sems, wg_buf, wu_buf, wd_buf, wsems, *, layer):
    j = pl.program_id(0)
    n_used = nu_ref[0]
    n_runs = nrun_ref[0]
    live_tile = j < n_used

    def fetch(t):
        slot = t % EXPERT_RING
        return pltpu.make_async_copy(xs_hbm.at[pl.ds(t * TE, TE)], ring.at[slot], sems.at[slot])

    def wfetch(r):
        slot = r % WEIGHT_SLOTS
        e = rexp_ref[r]
        return [pltpu.make_async_copy(wg_hbm.at[layer, e], wg_buf.at[slot], wsems.at[slot, 0]),
                pltpu.make_async_copy(wu_hbm.at[layer, e], wu_buf.at[slot], wsems.at[slot, 1]),
                pltpu.make_async_copy(wd_hbm.at[layer, e], wd_buf.at[slot], wsems.at[slot, 2])]

    @pl.when(j == 0)
    def _():
        for t in range(EXPERT_RING - 1):
            @pl.when(t < n_used)
            def _():
                fetch(t).start()
        for r in range(WEIGHT_SLOTS):
            @pl.when(r < n_runs)
            def _():
                for cp in wfetch(r):
                    cp.start()

    @pl.when(j + (EXPERT_RING - 1) < n_used)
    def _():
        fetch(j + (EXPERT_RING - 1)).start()

    @pl.when(jnp.logical_and(live_tile, jnp.logical_or(j == 0, be_ref[j] != be_ref[jnp.maximum(j - 1, 0)])))
    def _():
        r = run_ref[j]
        slot = r % WEIGHT_SLOTS
        for cp in wfetch(r):
            cp.wait()
        wgu_s[:, :D_EXPERT] = wg_buf[slot].astype(BF16)
        wgu_s[:, D_EXPERT:] = wu_buf[slot].astype(BF16)
        wd_s[...] = wd_buf[slot].astype(BF16)

        @pl.when(r + WEIGHT_SLOTS < n_runs)
        def _():
            for cp in wfetch(r + WEIGHT_SLOTS):
                cp.start()

    @pl.when(live_tile)
    def _():
        fetch(j).wait()
        live = lax.broadcasted_iota(jnp.int32, (TE, 1), 0) < nv_ref[j]
        lo, hi = _unpack_pairs(jnp.where(live, ring[j % EXPERT_RING], 0))
        xb = jnp.concatenate([lo.astype(BF16), hi.astype(BF16)], axis=1)
        hgu = _bdot(xb, wgu_s[...])
        hh = _silu(hgu[:, :D_EXPERT]) * hgu[:, D_EXPERT:]
        ys_ref[...] = _pack_pairs(_bdot(hh.astype(BF16), wd_s[...]))


def _experts(block_expert, block_rows, n_used, block_run, run_expert, n_runs, xs, w_gate, w_up, w_down, layer):
    def row_map(j, be, nv, nu, run, rexp, nrun):
        return (jnp.minimum(j, nu[0] - 1), 0)

    any_spec = pl.BlockSpec(memory_space=pl.ANY)
    return pl.pallas_call(
        functools.partial(_expert_body, layer=layer),
        out_shape=jax.ShapeDtypeStruct((P_ROWS, D_PACK), jnp.int32),
        grid_spec=pltpu.PrefetchScalarGridSpec(
            num_scalar_prefetch=6,
            grid=(NB,),
            in_specs=[any_spec, any_spec, any_spec, any_spec],
            out_specs=pl.BlockSpec((TE, D_PACK), row_map),
            scratch_shapes=[
                pltpu.VMEM((D_MODEL, 2 * D_EXPERT), BF16),
                pltpu.VMEM((D_EXPERT, D_MODEL), BF16),
                pltpu.VMEM((EXPERT_RING, TE, D_PACK), jnp.int32),
                pltpu.SemaphoreType.DMA((EXPERT_RING,)),
                pltpu.VMEM((WEIGHT_SLOTS, D_MODEL, D_EXPERT), F32),
                pltpu.VMEM((WEIGHT_SLOTS, D_MODEL, D_EXPERT), F32),
                pltpu.VMEM((WEIGHT_SLOTS, D_EXPERT, D_MODEL), F32),
                pltpu.SemaphoreType.DMA((WEIGHT_SLOTS, 3)),
            ],
        ),
        compiler_params=_params("arbitrary"),
        name="moe_experts",
    )(block_expert, block_rows, n_used, block_run, run_expert, n_runs, xs, w_gate, w_up, w_down)


def _positions_body(start_ref, eidx_ref, rank_ref, pos_ref):
    eidx = eidx_ref[...]
    base = jnp.zeros(eidx.shape, jnp.int32)
    for e in range(N_EXPERTS):
        base = jnp.where(eidx == e, start_ref[e], base)
    pos_ref[...] = base * TE + rank_ref[...]


def _positions(blk_start, eidx_t, rank_t):
    full = pl.BlockSpec((TOP_K, T_PART), lambda i, s: (0, 0))
    return pl.pallas_call(
        _positions_body,
        out_shape=jax.ShapeDtypeStruct((TOP_K, T_PART), jnp.int32),
        grid_spec=pltpu.PrefetchScalarGridSpec(
            num_scalar_prefetch=1, grid=(1,), in_specs=[full, full], out_specs=full),
        compiler_params=_params("arbitrary"),
        name="moe_positions",
    )(blk_start, eidx_t, rank_t)


def _sc_mesh():
    return plsc.VectorSubcoreMesh(core_axis_name="c", subcore_axis_name="s")


def _sc_worker():
    return lax.axis_index("s") * SC_CORES + lax.axis_index("c")


def _sc_scatter_rows(h, pos_rows):
    c = SC_CHUNK
    n_chunks = T_PART // SC_WORKERS // c
    width = h.shape[1]

    @functools.partial(
        pl.kernel, mesh=_sc_mesh(),
        out_type=jax.ShapeDtypeStruct((P_ROWS, width), h.dtype),
        scratch_types=[pltpu.VMEM((n_chunks * TOP_K, c), jnp.int32),
                       pltpu.VMEM((c, width), h.dtype), pltpu.VMEM((c, width), h.dtype)]
        + [pltpu.SemaphoreType.DMA] * 4,
        name="moe_dispatch_scatter",
    )
    def scatter(h_hbm, pos_hbm, xs_hbm, idx_v, buf0, buf1, sem_in0, sem_in1, sem_out0, sem_out1):
        assert n_chunks % 2 == 0
        bufs, sem_in, sem_out = (buf0, buf1), (sem_in0, sem_in1), (sem_out0, sem_out1)
        first = _sc_worker() * n_chunks
        pltpu.sync_copy(pos_hbm.at[pl.ds(first * TOP_K, n_chunks * TOP_K)], idx_v)

        def load(i, b):
            return pltpu.make_async_copy(h_hbm.at[pl.ds((first + i) * c, c)], bufs[b], sem_in[b])

        def puts(i, b):
            return [pltpu.make_async_copy(bufs[b], xs_hbm.at[idx_v.at[i * TOP_K + k]], sem_out[b])
                    for k in range(TOP_K)]

        load(0, 0).start()
        load(1, 1).start()

        @pl.loop(0, n_chunks, step=2)
        def _(i):
            for b in range(2):
                load(i + b, b).wait()
                for cp in puts(i + b, b):
                    cp.start()
            for b in range(2):
                for cp in puts(i + b, b):
                    cp.wait()

                @pl.when(i + 2 + b < n_chunks)
                def _():
                    load(i + 2 + b, b).start()

    return scatter(h, pos_rows)


def _sc_combine(ys, pos_tk, wsplat):
    ct = COMBINE_TOKENS
    rows = ct * TOP_K
    tok_per_worker = T_PART // SC_WORKERS
    n_chunks = tok_per_worker // ct
    width = ys.shape[1]
    vmem = pltpu.VMEM

    @functools.partial(
        pl.kernel, mesh=_sc_mesh(),
        out_type=jax.ShapeDtypeStruct((T_PART, D_MODEL), F32),
        scratch_types=[vmem((tok_per_worker * TOP_K,), jnp.int32),
                       vmem((rows, width), ys.dtype), vmem((rows, width), ys.dtype),
                       vmem((ct, TOP_K * SC_LANES), F32), vmem((ct, TOP_K * SC_LANES), F32),
                       vmem((ct, D_MODEL), F32), vmem((ct, D_MODEL), F32)]
        + [pltpu.SemaphoreType.DMA] * 6,
        compiler_params=pltpu.CompilerParams(needs_layout_passes=False),
        name="moe_combine",
    )
    def combine(ys_hbm, pos_hbm, w_hbm, out_hbm, idx_v, g0, g1, w0, w1, o0, o1,
                sem_g0, sem_g1, sem_w0, sem_w1, sem_o0, sem_o1):
        assert n_chunks % 2 == 0
        gbuf, wbuf, obuf = (g0, g1), (w0, w1), (o0, o1)
        sem_g, sem_w, sem_o = (sem_g0, sem_g1), (sem_w0, sem_w1), (sem_o0, sem_o1)
        tok0 = _sc_worker() * tok_per_worker
        pltpu.sync_copy(pos_hbm.at[pl.ds(tok0 * TOP_K, tok_per_worker * TOP_K)], idx_v)

        def fetch(i, b):
            return [pltpu.make_async_copy(ys_hbm.at[idx_v.at[pl.ds(i * rows, rows)]], gbuf[b], sem_g[b]),
                    pltpu.make_async_copy(w_hbm.at[pl.ds(tok0 + i * ct, ct)], wbuf[b], sem_w[b])]

        def flush(i, b):
            return pltpu.make_async_copy(obuf[b], out_hbm.at[pl.ds(tok0 + i * ct, ct)], sem_o[b])

        def reduce_chunk(b):
            @pl.loop(0, ct)
            def _(t):
                wv = [wbuf[b][t, pl.ds(k * SC_LANES, SC_LANES)] for k in range(TOP_K)]

                @pl.loop(0, width // SC_LANES)
                def _(j):
                    col = j * SC_LANES
                    acc_lo = acc_hi = None
                    for k in range(TOP_K):
                        word = gbuf[b][t * TOP_K + k, pl.ds(col, SC_LANES)]
                        lo = lax.bitcast_convert_type(word << 16, F32)
                        hi = lax.bitcast_convert_type(word & jnp.int32(-65536), F32)
                        acc_lo = wv[k] * lo if k == 0 else acc_lo + wv[k] * lo
                        acc_hi = wv[k] * hi if k == 0 else acc_hi + wv[k] * hi
                    obuf[b][t, pl.ds(col, SC_LANES)] = acc_lo
                    obuf[b][t, pl.ds(D_PACK + col, SC_LANES)] = acc_hi

        for b in range(2):
            for cp in fetch(b, b):
                cp.start()

        @pl.loop(0, n_chunks, step=2)
        def _(i):
            for b in range(2):
                for cp in fetch(i + b, b):
                    cp.wait()

                @pl.when(i > 0)
                def _():
                    flush(i + b - 2, b).wait()

                reduce_chunk(b)
                flush(i + b, b).start()

                @pl.when(i + 2 + b < n_chunks)
                def _():
                    for cp in fetch(i + 2 + b, b):
                        cp.start()

        for b in range(2):
            flush(n_chunks - 2 + b, b).wait()

    return combine(ys, pos_tk, wsplat)


def _moe_routed(h, eidx_t, rank_t, counts, wsplat, w_gate, w_up, w_down, layer):
    counts = counts.reshape(1, N_EXPERTS)
    nblk = (counts + TE - 1) // TE
    blk_end = jnp.cumsum(nblk, axis=1)
    blk_start = blk_end - nblk
    blocks = jnp.arange(NB, dtype=jnp.int32).reshape(NB, 1)
    block_expert = jnp.minimum(jnp.sum(blocks >= blk_end, axis=1, keepdims=True), N_EXPERTS - 1)
    mine = block_expert == jnp.arange(N_EXPERTS, dtype=jnp.int32).reshape(1, N_EXPERTS)
    cnt_b = jnp.sum(jnp.where(mine, counts, 0), axis=1, keepdims=True)
    start_b = jnp.sum(jnp.where(mine, blk_start, 0), axis=1, keepdims=True)
    block_rows = jnp.clip(cnt_b - (blocks - start_b) * TE, 0, TE)
    n_used = blk_end[0, -1].reshape(1).astype(jnp.int32)
    present = (nblk > 0).astype(jnp.int32)
    run_of_expert = jnp.cumsum(present, axis=1) - 1
    block_run = jnp.sum(jnp.where(mine, run_of_expert, 0), axis=1)
    experts = jnp.arange(N_EXPERTS, dtype=jnp.int32).reshape(1, N_EXPERTS)
    run_hit = jnp.logical_and(run_of_expert == experts.reshape(N_EXPERTS, 1), present > 0)
    run_expert = jnp.sum(jnp.where(run_hit, experts, 0), axis=1)
    n_runs = jnp.sum(present).reshape(1)
    pos = _positions(blk_start.reshape(N_EXPERTS).astype(jnp.int32), eidx_t, rank_t)
    pos_rows = pos.reshape(TOP_K, T_PART // SC_CHUNK, SC_CHUNK).transpose(1, 0, 2).reshape(-1, SC_CHUNK)
    xs = _sc_scatter_rows(h, pos_rows)
    ys = _experts(block_expert.reshape(NB).astype(jnp.int32), block_rows.reshape(NB).astype(jnp.int32),
                  n_used, block_run.astype(jnp.int32), run_expert.astype(jnp.int32), n_runs.astype(jnp.int32),
                  xs, w_gate, w_up, w_down, layer)
    return _sc_combine(ys, pos.T.reshape(-1), wsplat)


N_SLABS = (3 * C_FDIM + 2 * D_MODEL) // 128


F32_GROUPS = (1, 2)
B16_GROUPS = (0, 3, 4)


def _hgrn_in_body(acc_ref, rt_ref, mod0_ref, mod1_ref, nmix_ref, win_ref, *rest):
    x_ref, zf_ref, zb_ref = rest[-3:]
    x = acc_ref[...] + mod0_ref[5:6, :] * rt_ref[...]
    x_ref[...] = x
    hb = _modulate(x, nmix_ref[...], mod1_ref[0:1, :], mod1_ref[1:2, :]).astype(BF16)
    for s in range(N_SLABS // C_HEADS):
        zz = _bdot(hb, win_ref[:, s * D_MODEL:(s + 1) * D_MODEL])
        for hh in range(C_HEADS):
            if s in F32_GROUPS:
                zf_ref[F32_GROUPS.index(s) * C_HEADS + hh] = zz[:, hh * 128:(hh + 1) * 128]
            else:
                zb_ref[B16_GROUPS.index(s) * C_HEADS + hh] = zz[:, hh * 128:(hh + 1) * 128].astype(BF16)


HGRN_IN_TILES = 2


def _hgrn_in(acc, routed, tile0, prev, mod0, mod1, norm_mix, w_in):
    g = HGRN_IN_TILES
    assert NT_PART % g == 0 and tile0 % g == 0 and NT_P % g == 0 and TILES_PER_LAT % g == 0
    rows = g * TM
    block0 = tile0 // g

    def shifted_mod():
        return pl.BlockSpec((None, 6, D_MODEL), lambda i: (_mod_row(i * g + tile0), 0, 0))

    local = pl.BlockSpec((rows, D_MODEL), lambda i: (i, 0))
    any_spec = pl.BlockSpec(memory_space=pl.ANY)
    prev = () if prev is None else tuple(prev)
    n_in = 6
    return pl.pallas_call(
        _hgrn_in_body,
        out_shape=(
            jax.ShapeDtypeStruct((T, D_MODEL), F32),
            jax.ShapeDtypeStruct((len(F32_GROUPS) * C_HEADS, T, 128), F32),
            jax.ShapeDtypeStruct((len(B16_GROUPS) * C_HEADS, T, 128), BF16),
        ),
        grid=(NT_PART // g,),
        in_specs=[
            local, local, shifted_mod(), shifted_mod(),
            _const_spec((1, D_MODEL)),
            pl.BlockSpec((D_MODEL, 3 * C_FDIM + 2 * D_MODEL), lambda i: (0, 0), pipeline_mode=pl.Buffered(1)),
        ] + [any_spec] * len(prev),
        out_specs=(
            pl.BlockSpec((rows, D_MODEL), lambda i: (i + block0, 0)),
            pl.BlockSpec((len(F32_GROUPS) * C_HEADS, rows, 128), lambda i: (0, i + block0, 0)),
            pl.BlockSpec((len(B16_GROUPS) * C_HEADS, rows, 128), lambda i: (0, i + block0, 0)),
        ),
        input_output_aliases={n_in + k: k for k in range(len(prev))},
        compiler_params=_params("parallel"),
        name="hgrn_in_proj",
    )(acc, routed, mod0, mod1, norm_mix.reshape(1, -1), w_in.astype(BF16), *prev)


def _gla_body(q_ref, f_ref, v_ref, lb_ref, s0_ref, o_ref, ns_ref, st_ref, *, rev):
    j = pl.program_id(0)
    ti = NT - 1 - j if rev else j
    is_ctx = ti < NT_P
    first_lat = (ti - NT_P) % TILES_PER_LAT == (TILES_PER_LAT - 1 if rev else 0)

    @pl.when(is_ctx)
    def _():
        st_ref[...] = jnp.zeros(st_ref.shape, F32)

    @pl.when(jnp.logical_and(jnp.logical_not(is_ctx), first_lat))
    def _():
        st_ref[...] = s0_ref[...]

    row = lax.broadcasted_iota(jnp.int32, (TM, TM), 0)
    col = lax.broadcasted_iota(jnp.int32, (TM, TM), 1)
    same_chunk = (row // SCAN_CHUNK) == (col // SCAN_CHUNK)
    seen = jnp.logical_and(same_chunk, (col >= row) if rev else (col <= row))
    cum_w = seen.astype(BF16)
    mid = SCAN_CHUNK // 2 if rev else SCAN_CHUNK // 2 - 1
    last = 0 if rev else SCAN_CHUNK - 1
    n_chunks = TM // SCAN_CHUNK
    order = range(n_chunks - 1, -1, -1) if rev else range(n_chunks)
    group = 4

    def chunk_rows(b, off):
        return jnp.concatenate(
            [jnp.broadcast_to(b[c * SCAN_CHUNK + off:c * SCAN_CHUNK + off + 1, :], (SCAN_CHUNK, b.shape[1]))
             for c in range(n_chunks)], axis=0)

    def head_group(gi, carry):
        heads = [gi * group + u for u in range(group)]
        qs, kk, vv, bcum = [], [], [], []
        for hd in heads:
            lb = lb_ref[hd]
            qs.append(_silu(q_ref[hd].astype(F32)) * (C_DK ** -0.5))
            fg = lb + (1.0 - lb) * jax.nn.sigmoid(f_ref[hd])
            kk.append(1.0 - fg)
            vv.append(v_ref[hd])
            g = jnp.log(fg)
            g_hi = g.astype(BF16)
            r1 = g - g_hi.astype(F32)
            g_mid = r1.astype(BF16)
            g_lo = (r1 - g_mid.astype(F32)).astype(BF16)
            bcum.append(_bdot(cum_w, g_hi) + _bdot(cum_w, g_mid) + _bdot(cum_w, g_lo))
        o_intra, q_dec, kv, decay = [], [], [], []
        for u in range(group):
            b_mid = chunk_rows(bcum[u], mid)
            b_last = chunk_rows(bcum[u], last)
            qe = (qs[u] * jnp.exp(bcum[u] - b_mid)).astype(BF16)
            ke = (kk[u] * jnp.exp(b_mid - bcum[u])).astype(BF16)
            att = jnp.where(seen, _dot_nt(qe, ke), 0.0)
            o_intra.append(_bdot(att.astype(BF16), vv[u]))
            q_dec.append((qs[u] * jnp.exp(bcum[u])).astype(BF16))
            k_dec = (kk[u] * jnp.exp(b_last - bcum[u])).astype(BF16)
            kv.append([_dot_tn(vv[u][c * SCAN_CHUNK:(c + 1) * SCAN_CHUNK], k_dec[c * SCAN_CHUNK:(c + 1) * SCAN_CHUNK])
                       for c in range(n_chunks)])
            decay.append([jnp.exp(bcum[u][c * SCAN_CHUNK + last:c * SCAN_CHUNK + last + 1, :])
                          for c in range(n_chunks)])
        st = [st_ref[hd] for hd in heads]
        for c in order:
            sl = slice(c * SCAN_CHUNK, (c + 1) * SCAN_CHUNK)
            for u, hd in enumerate(heads):
                o_ref[hd, pl.ds(c * SCAN_CHUNK, SCAN_CHUNK), :] = (
                    o_intra[u][sl] + _dot_nt(q_dec[u][sl], st[u].astype(BF16))).astype(BF16)
                st[u] = decay[u][c] * st[u] + kv[u][c]
        for u, hd in enumerate(heads):
            st_ref[hd] = st[u]
        return carry

    lax.fori_loop(0, C_HEADS // group, head_group, 0)

    @pl.when(is_ctx)
    def _():
        ns_ref[...] = st_ref[...]


def _gla(zf, zb, lb_dir, s0t_dir, *, rev):
    def ti_of(j):
        return NT - 1 - j if rev else j

    f_slab = F32_GROUPS.index(2 if rev else 1)

    def lat_map(j):
        return (jnp.clip((ti_of(j) - NT_P) // TILES_PER_LAT, 0, DEC_BATCH - 1), 0, 0, 0)

    return pl.pallas_call(
        functools.partial(_gla_body, rev=rev),
        out_shape=(
            jax.ShapeDtypeStruct((C_HEADS, T, C_DV), BF16),
            jax.ShapeDtypeStruct((BATCH, C_HEADS, C_DV, C_DK), F32),
        ),
        grid=(NT,),
        in_specs=[
            pl.BlockSpec((C_HEADS, TM, 128), lambda j: (B16_GROUPS.index(0), ti_of(j), 0)),
            pl.BlockSpec((C_HEADS, TM, 128), lambda j: (f_slab, ti_of(j), 0)),
            pl.BlockSpec((C_HEADS, TM, 128), lambda j: (B16_GROUPS.index(3), ti_of(j), 0)),
            _const_spec((C_HEADS, 1, C_DK)),
            pl.BlockSpec((None, C_HEADS, C_DV, C_DK), lat_map),
        ],
        out_specs=(
            pl.BlockSpec((C_HEADS, TM, C_DV), lambda j: (0, ti_of(j), 0)),
            pl.BlockSpec((None, C_HEADS, C_DV, C_DK),
                         lambda j: (jnp.minimum(ti_of(j), NT_P - 1), 0, 0, 0)),
        ),
        scratch_shapes=[pltpu.VMEM((C_HEADS, C_DV, C_DK), F32)],
        compiler_params=_params("arbitrary"),
        name="gla_bwd" if rev else "gla_fwd",
    )(zb, zf, zb, lb_dir, s0t_dir)


def _hgrn_out_body(ofw_ref, obw_ref, gate_ref, x_ref, mod_ref, no_ref, wout_ref, o_ref, cat_ref):
    for hd in range(C_HEADS):
        o = ofw_ref[hd].astype(F32) + obw_ref[hd].astype(F32)
        gate = gate_ref[hd].astype(F32)
        cat_ref[:, hd * C_DV:(hd + 1) * C_DV] = (_rms(o, no_ref[...]) * _silu(gate)).astype(BF16)
    o_ref[...] = x_ref[...] + mod_ref[2:3, :] * _bdot(cat_ref[...], wout_ref[...])


def _hgrn_out(o_fw, o_bw, zb, x, mod, norm_o, w_out):
    head_spec = pl.BlockSpec((C_HEADS, TM, C_DV), lambda i: (0, i, 0))
    return pl.pallas_call(
        _hgrn_out_body,
        out_shape=jax.ShapeDtypeStruct((T, D_MODEL), F32),
        grid=(NT,),
        in_specs=[
            head_spec, head_spec,
            pl.BlockSpec((C_HEADS, TM, 128), lambda i: (B16_GROUPS.index(4), i, 0)),
            _tile_spec(), _mod_spec(),
            _const_spec((1, C_DV)),
            _const_spec((D_MODEL, D_MODEL)),
        ],
        out_specs=_tile_spec(),
        scratch_shapes=[pltpu.VMEM((TM, D_MODEL), BF16)],
        compiler_params=_params("parallel"),
        name="hgrn_out_proj",
    )(o_fw, o_bw, zb, x, mod, norm_o.reshape(1, -1), w_out.astype(BF16))


def _final_body(acc_ref, rt_ref, mod_ref, nf_ref, *rest):
    o_ref = rest[-1]
    o_ref[...] = _rms(acc_ref[...] + mod_ref[5:6, :] * rt_ref[...], nf_ref[...])


FINAL_TILES = 4


def _final(acc, routed, mod, norm_final, part_tile0, local0, n_tiles, out_tile0, out_tiles, prev=None):
    g = FINAL_TILES
    assert all(v % g == 0 for v in (part_tile0, local0, n_tiles, out_tile0, NT_P, TILES_PER_LAT))
    rows = g * TM
    local = pl.BlockSpec((rows, D_MODEL), lambda i: (i + local0 // g, 0))
    prev = () if prev is None else (prev,)
    return pl.pallas_call(
        _final_body,
        out_shape=jax.ShapeDtypeStruct((out_tiles * TM, D_MODEL), F32),
        grid=(n_tiles // g,),
        in_specs=[
            local, local,
            pl.BlockSpec((None, 6, D_MODEL), lambda i: (_mod_row(i * g + local0 + part_tile0), 0, 0)),
            _const_spec((1, D_MODEL)),
        ] + [pl.BlockSpec(memory_space=pl.ANY)] * len(prev),
        out_specs=pl.BlockSpec((rows, D_MODEL), lambda i: (i + out_tile0 // g, 0)),
        input_output_aliases={4 + k: 0 for k in range(len(prev))},
        compiler_params=_params("parallel"),
        name="final_norm",
    )(acc, routed, mod, norm_final.reshape(1, -1), *prev)


def kernel(x_prompt, x_sample, state_hgrn, c, c_ctx, w_ada, b_ada, norm_mix, norm_ffn, w_out, w_in_ab, w_sg, b_sg, norm_sg, w_dw, b_dw, norm_cv, w_in_hgrn, lb_raw, norm_o, w_router, b_router, w_gate, w_up, w_down, w_sh_gate, w_sh_up, w_sh_down, norm_final):
    cvecs = jnp.concatenate(
        [c_ctx.reshape(1, D_MODEL), c, jnp.zeros((N_MOD_ROWS - 1 - DEC_BATCH, D_MODEL), F32)], axis=0)
    mods = _ada_tables(cvecs, w_ada, b_ada)
    lb_sm = jax.nn.softmax(lb_raw.astype(F32), axis=0)
    lb1 = (jnp.cumsum(lb_sm, axis=0) - lb_sm[0])[1].reshape(2, C_HEADS, 1, C_DK)

    def moe(l, xin):
        parts = []
        for p in range(MOE_PARTS):
            h, eidx_t, w8_t, rank_t, counts, acc = _moe_pre(
                xin, mods[l], norm_ffn[l], w_router[l], b_router[l], w_sh_gate[l], w_sh_up[l], w_sh_down[l],
                p * NT_PART)
            wsplat = jnp.repeat(w8_t.T, SC_LANES, axis=1)
            parts += [acc, _moe_routed(h, eidx_t, rank_t, counts, wsplat, w_gate, w_up, w_down, l)]
        return parts

    x = _l0_mixer(x_prompt.reshape(T_P, D_MODEL), x_sample.reshape(T_S, D_MODEL), mods[0], norm_mix[0], w_in_ab[0], w_sg[0], b_sg[0], norm_sg[0], w_dw[0],
                  b_dw[0], norm_cv[0], w_out[0])
    parts = moe(0, x)
    xz = None
    for p in range(MOE_PARTS):
        xz = _hgrn_in(parts[2 * p], parts[2 * p + 1], p * NT_PART, xz, mods[0], mods[1], norm_mix[1], w_in_hgrn[0])
    x, zf, zb = xz
    s0t = jnp.swapaxes(state_hgrn[:, 0].astype(F32), -1, -2)
    o_fw, ns_fw = _gla(zf, zb, lb1[0], s0t[:, 0], rev=False)
    o_bw, ns_bw = _gla(zf, zb, lb1[1], s0t[:, 1], rev=True)
    x = _hgrn_out(o_fw, o_bw, zb, x, mods[1], norm_o[0], w_out[1])
    parts = moe(1, x)
    y_p = y_s = None
    for p in range(MOE_PARTS):
        lo, hi = p * NT_PART, (p + 1) * NT_PART
        if lo < NT_P:
            n = min(hi, NT_P) - lo
            y_p = _final(parts[2 * p], parts[2 * p + 1], mods[1], norm_final, lo, 0, n, lo, NT_P, y_p)
        if hi > NT_P:
            first = max(lo, NT_P)
            y_s = _final(parts[2 * p], parts[2 * p + 1], mods[1], norm_final, lo, first - lo, hi - first,
                         first - NT_P, NT_S, y_s)
    y_p = y_p.reshape(BATCH, SEQ, D_MODEL)
    y_s = y_s.reshape(DEC_BATCH, DEC_SEQ, D_MODEL)
    new_state = jnp.swapaxes(jnp.stack([ns_fw, ns_bw], axis=1), -1, -2)[:, None]
    return (y_p, y_s, new_state)
```

```python
import functools

import jax
import jax.numpy as jnp
from jax import lax
from jax.experimental import pallas as pl
from jax.experimental.pallas import tpu as pltpu
from jax.experimental.pallas import tpu_sc as plsc

F32 = jnp.float32
BF16 = jnp.bfloat16
HIGHEST = lax.Precision.HIGHEST

D_MODEL = 1024
BATCH = 32
SEQ = 256
DEPTH = 2
DEC_BATCH = 8
DEC_SEQ = 2048
GRID_W = 64
A_WIDTH = D_MODEL // 2
A_GROUPS = 4
A_GC = A_WIDTH // A_GROUPS
CHUNK_A = 128
B_WIDTH = D_MODEL - A_WIDTH
CONV_W = 31
CONV_PAD = CONV_W // 2
C_HEADS = 8
C_DK = 128
C_DV = D_MODEL // C_HEADS
C_FDIM = C_HEADS * C_DK
SCAN_CHUNK = 64
N_EXPERTS = 64
TOP_K = 8
N_GROUPS = 8
GROUP_SIZE = N_EXPERTS // N_GROUPS
TOPK_GROUPS = 4
D_EXPERT = 256
D_SHARED = 256
ROUTED_SCALE = 2.5
EPS = 1e-6

TM = 256
T_P = BATCH * SEQ
T_S = DEC_BATCH * DEC_SEQ
T = T_P + T_S
NT_P = T_P // TM
NT_S = T_S // TM
NT = NT_P + NT_S
TILES_PER_LAT = DEC_SEQ // TM
TE = 512
MOE_PARTS = 2
NT_PART = NT // MOE_PARTS
T_PART = NT_PART * TM
NB = T_PART * TOP_K // TE + N_EXPERTS
P_ROWS = NB * TE
D_PACK = D_MODEL // 2
N_MOD_ROWS = 16
CONV_HALO = 16
VMEM_LIMIT = 48 * 1024 * 1024
SC_CORES = 2
SC_SUBCORES = 16
SC_WORKERS = SC_CORES * SC_SUBCORES
SC_LANES = 16
SC_CHUNK = 64
COMBINE_TOKENS = SC_CHUNK // TOP_K


def _mod_row(i):
    return jnp.where(i < NT_P, 0, 1 + (i - NT_P) // TILES_PER_LAT)


def _silu(x):
    return x * jax.nn.sigmoid(x)


def _gelu(x):
    return x * (0.5 * (1.0 + jnp.tanh(0.7978845608028654 * (x + 0.044715 * (x * x * x)))))


def _rms(x, g):
    return x * lax.rsqrt(jnp.mean(x * x, axis=-1, keepdims=True) + EPS) * g


def _layernorm(x, g):
    xc = x - jnp.mean(x, axis=-1, keepdims=True)
    return xc * lax.rsqrt(jnp.mean(xc * xc, axis=-1, keepdims=True) + EPS) * g


def _modulate(x, g, shift, scale):
    return _rms(x, g) * (1.0 + scale) + shift


def _bdot(a, b):
    return jnp.dot(a, b, preferred_element_type=F32)


def _dot_nt(a, b, precision=None):
    return lax.dot_general(a, b, (((1,), (1,)), ((), ())), precision=precision,
                           preferred_element_type=F32)


def _dot_tn(a, b):
    return lax.dot_general(a, b, (((0,), (0,)), ((), ())), preferred_element_type=F32)


def _pack_pairs(x):
    m = x.shape[1] // 2
    lo = lax.bitcast_convert_type(x[:, :m].astype(BF16).astype(F32), jnp.uint32)
    hi = lax.bitcast_convert_type(x[:, m:].astype(BF16).astype(F32), jnp.uint32)
    return lax.bitcast_convert_type(hi | (lo >> 16), jnp.int32)


def _unpack_pairs(w):
    u = lax.bitcast_convert_type(w, jnp.uint32)
    lo = lax.bitcast_convert_type(u << 16, F32)
    hi = lax.bitcast_convert_type(u & jnp.uint32(0xFFFF0000), F32)
    return lo, hi


def _params(*sem):
    return pltpu.CompilerParams(dimension_semantics=sem, vmem_limit_bytes=VMEM_LIMIT)


def _const_spec(shape):
    nd = len(shape)
    return pl.BlockSpec(shape, lambda *_: (0,) * nd)


def _ada_body(c_ref, w_ref, b_ref, o_ref):
    s = _silu(c_ref[...])
    o_ref[...] = jnp.dot(s, w_ref[...], precision=HIGHEST, preferred_element_type=F32) + b_ref[...]


def _ada_tables(cvecs, w_ada, b_ada):
    out = pl.pallas_call(
        _ada_body,
        out_shape=jax.ShapeDtypeStruct((DEPTH, N_MOD_ROWS, 6 * D_MODEL), F32),
        grid=(DEPTH, 6),
        in_specs=[
            _const_spec((N_MOD_ROWS, D_MODEL)),
            pl.BlockSpec((None, D_MODEL, D_MODEL), lambda l, j: (l, 0, j)),
            pl.BlockSpec((None, 1, D_MODEL), lambda l, j: (l, 0, j)),
        ],
        out_specs=pl.BlockSpec((None, N_MOD_ROWS, D_MODEL), lambda l, j: (l, 0, j)),
        compiler_params=_params("parallel", "parallel"),
        name="ada_tables",
    )(cvecs, w_ada, b_ada.reshape(DEPTH, 1, 6 * D_MODEL))
    return out.reshape(DEPTH, N_MOD_ROWS, 6, D_MODEL)


def _mod_spec():
    return pl.BlockSpec((None, 6, D_MODEL), lambda i: (_mod_row(i), 0, 0))


def _tile_spec():
    return pl.BlockSpec((TM, D_MODEL), lambda i: (i, 0))


SUBLANES = 8


def _conv_segment(pad_ref, shift_ref, conv_ref, wdw_ref, pad_base, out_base, seg):
    rb = min(seg, 64)
    for cb in range(B_WIDTH // 128):
        cs = slice(cb * 128, (cb + 1) * 128)
        for r0 in range(0, seg, rb):
            acc = jnp.zeros((rb, 128), F32)
            for k in range(CONV_W):
                b = (CONV_HALO - CONV_PAD + k) % SUBLANES
                off = pad_base + r0 + CONV_HALO - CONV_PAD + k - b
                src = pad_ref if b == 0 else shift_ref.at[b - 1]
                acc = acc + wdw_ref[k:k + 1, cs] * src[off:off + rb, cs]
            conv_ref[out_base + r0:out_base + r0 + rb, cs] = acc


def _l0_body(xc_ref, xl_ref, mod_ref, nmix_ref, win_ref, wsg_ref, bsg_ref, nsg_ref, wdw_ref, bdw_ref,
             ncv_ref, wout_ref, o_ref, cat_ref, pad_ref, shift_ref, conv_ref):
    i = pl.program_id(0)
    x = jnp.where(i < NT_P, xc_ref[...], xl_ref[...])
    h = _modulate(x, nmix_ref[...], mod_ref[0:1, :], mod_ref[1:2, :])
    z = _bdot(h.astype(BF16), win_ref[...])
    u = _gelu(z[:, :A_WIDTH])
    vb = _layernorm(_gelu(z[:, A_WIDTH:2 * A_WIDTH]), nsg_ref[...]).astype(BF16)
    for n in range(TM // CHUNK_A):
        rs = slice(n * CHUNK_A, (n + 1) * CHUNK_A)
        for g in range(A_GROUPS):
            cs = slice(g * A_GC, (g + 1) * A_GC)
            m = _bdot(wsg_ref[g], vb[rs, cs]) + bsg_ref[g]
            cat_ref[rs, cs] = (u[rs, cs] * m).astype(BF16)
    hb = z[:, 2 * A_WIDTH:2 * A_WIDTH + B_WIDTH] * jax.nn.sigmoid(z[:, 2 * A_WIDTH + B_WIDTH:])

    def conv_tile(seg):
        stride = seg + 2 * CONV_HALO
        halo = jnp.zeros((CONV_HALO, B_WIDTH), F32)
        for s in range(TM // seg):
            b = s * stride
            pad_ref[b:b + CONV_HALO, :] = halo
            pad_ref[b + CONV_HALO:b + CONV_HALO + seg, :] = hb[s * seg:(s + 1) * seg, :]
            pad_ref[b + CONV_HALO + seg:b + stride, :] = halo
        rows = (TM // seg) * stride - SUBLANES
        for b in range(1, SUBLANES):
            shift_ref[b - 1, 0:rows, :] = pad_ref[b:b + rows, :]
        for s in range(TM // seg):
            _conv_segment(pad_ref, shift_ref, conv_ref, wdw_ref, s * stride, s * seg, seg)

    @pl.when(i < NT_P)
    def _():
        conv_tile(SEQ)

    @pl.when(i >= NT_P)
    def _():
        conv_tile(GRID_W)

    yb = _layernorm(conv_ref[...] + bdw_ref[...], ncv_ref[...])
    cat_ref[:, A_WIDTH:] = _silu(yb).astype(BF16)
    out = _bdot(cat_ref[...], wout_ref[...])
    o_ref[...] = x + mod_ref[2:3, :] * out


def _l0_mixer(x_ctx, x_lat, mod, norm_mix, w_in, w_sg, b_sg, norm_sg, w_dw, b_dw, norm_cv, w_out):
    assert TM == SEQ and TM % GRID_W == 0 and TM % CHUNK_A == 0
    pad_rows = (TM // GRID_W) * (GRID_W + 2 * CONV_HALO)
    assert pad_rows >= SEQ + 2 * CONV_HALO
    return pl.pallas_call(
        _l0_body,
        out_shape=jax.ShapeDtypeStruct((T, D_MODEL), F32),
        grid=(NT,),
        in_specs=[
            pl.BlockSpec((TM, D_MODEL), lambda i: (jnp.minimum(i, NT_P - 1), 0)),
            pl.BlockSpec((TM, D_MODEL), lambda i: (jnp.maximum(i - NT_P, 0), 0)),
            _mod_spec(),
            _const_spec((1, D_MODEL)),
            _const_spec((D_MODEL, 2 * A_WIDTH + 2 * B_WIDTH)),
            _const_spec((A_GROUPS, CHUNK_A, CHUNK_A)),
            _const_spec((A_GROUPS, CHUNK_A, 1)),
            _const_spec((1, A_WIDTH)),
            _const_spec((CONV_W, B_WIDTH)),
            _const_spec((1, B_WIDTH)),
            _const_spec((1, B_WIDTH)),
            _const_spec((D_MODEL, D_MODEL)),
        ],
        out_specs=_tile_spec(),
        scratch_shapes=[
            pltpu.VMEM((TM, D_MODEL), BF16),
            pltpu.VMEM((pad_rows, B_WIDTH), F32),
            pltpu.VMEM((SUBLANES - 1, pad_rows, B_WIDTH), F32),
            pltpu.VMEM((TM, B_WIDTH), F32),
        ],
        compiler_params=_params("parallel"),
        name="l0_mixer",
    )(x_ctx, x_lat, mod, norm_mix.reshape(1, -1), w_in.astype(BF16), w_sg.astype(BF16),
      b_sg.reshape(A_GROUPS, CHUNK_A, 1), norm_sg.reshape(1, -1), w_dw, b_dw.reshape(1, -1),
      norm_cv.reshape(1, -1), w_out.astype(BF16))


def _route(scores, biased):
    n = scores.shape[-1]
    shp = (N_GROUPS, GROUP_SIZE, n)
    s3 = scores.reshape(shp)
    b3 = biased.reshape(shp)
    m_iota = lax.broadcasted_iota(jnp.int32, shp, 1).astype(F32)
    g_iota = lax.broadcasted_iota(jnp.int32, shp, 0).astype(F32)
    e_iota = g_iota * GROUP_SIZE + m_iota
    neg = -jnp.inf

    def amax1(v):
        return jnp.max(v, axis=1, keepdims=True)

    def amin1(v):
        return jnp.min(v, axis=1, keepdims=True)

    m1 = amax1(b3)
    i1 = amin1(jnp.where(b3 == m1, m_iota, float(GROUP_SIZE)))
    m2 = amax1(jnp.where(m_iota == i1, neg, b3))
    grp = m1 + m2
    gi1 = lax.broadcasted_iota(jnp.int32, grp.shape, 0).astype(F32)
    gmask = jnp.zeros(grp.shape, jnp.bool_)
    for _ in range(TOPK_GROUPS):
        gm = jnp.max(grp, axis=0, keepdims=True)
        gi = jnp.min(jnp.where(grp == gm, gi1, float(N_GROUPS)), axis=0, keepdims=True)
        hit = gi1 == gi
        gmask = jnp.logical_or(gmask, hit)
        grp = jnp.where(hit, neg, grp)
    cand = jnp.where(gmask, b3, neg)
    ids, vals, hits = [], [], []
    for _ in range(TOP_K):
        mx = jnp.max(amax1(cand), axis=0, keepdims=True)
        ei = jnp.min(amin1(jnp.where(cand == mx, e_iota, float(N_EXPERTS))), axis=0, keepdims=True)
        hit = e_iota == ei
        ids.append(ei.reshape(1, n))
        vals.append(_pick(hit, s3))
        hits.append(hit)
        cand = jnp.where(hit, neg, cand)
    return jnp.concatenate(ids, axis=0).astype(jnp.int32), jnp.concatenate(vals, axis=0), hits


def _pick(hit, v3):
    s = jnp.sum(jnp.sum(jnp.where(hit, v3, 0.0), axis=1, keepdims=True), axis=0, keepdims=True)
    return s.reshape(1, v3.shape[-1])


def _moe_pre_body(x_ref, mod_ref, nffn_ref, wrt_ref, br_ref, wsg_ref, wsu_ref, wsd_ref,
                  h_ref, eidx_ref, w8_ref, rank_ref, cnt_ref, acc_ref, run_ref):
    @pl.when(pl.program_id(0) == 0)
    def _():
        run_ref[...] = jnp.zeros(run_ref.shape, F32)

    x = x_ref[...]
    h = _modulate(x, nffn_ref[...], mod_ref[3:4, :], mod_ref[4:5, :])
    h_ref[...] = _pack_pairs(h)
    hb = h.astype(BF16)
    h_lo = (h - hb.astype(F32)).astype(BF16)
    wr = wrt_ref[...]
    wr_hi = wr.astype(BF16)
    wr_lo = (wr - wr_hi.astype(F32)).astype(BF16)
    logits_t = _dot_nt(wr_hi, hb) + (_dot_nt(wr_hi, h_lo) + _dot_nt(wr_lo, hb))
    scores = jax.nn.sigmoid(logits_t)
    eidx, sv, hits = _route(scores, scores + br_ref[...])
    eidx_ref[...] = eidx
    w8_ref[...] = sv / jnp.sum(sv, axis=0, keepdims=True) * ROUTED_SCALE
    sel3 = hits[0]
    for hit in hits[1:]:
        sel3 = jnp.logical_or(sel3, hit)
    sel = sel3.astype(F32).reshape(N_EXPERTS, TM)
    earlier = (lax.broadcasted_iota(jnp.int32, (TM, TM), 0)
               < lax.broadcasted_iota(jnp.int32, (TM, TM), 1)).astype(BF16)
    rank3 = (_bdot(sel.astype(BF16), earlier) + run_ref[...]).reshape(N_GROUPS, GROUP_SIZE, TM)
    rank_ref[...] = jnp.concatenate([_pick(hit, rank3) for hit in hits], axis=0).astype(jnp.int32)
    run_ref[...] = run_ref[...] + jnp.sum(sel, axis=1, keepdims=True)
    cnt_ref[...] = run_ref[...].astype(jnp.int32)
    sh = _bdot((_silu(_bdot(hb, wsg_ref[...])) * _bdot(hb, wsu_ref[...])).astype(BF16), wsd_ref[...])
    acc_ref[...] = x + mod_ref[5:6, :] * sh


def _moe_pre(x, mod, norm_ffn, w_router, b_router, w_sh_gate, w_sh_up, w_sh_down, tile0):
    return pl.pallas_call(
        _moe_pre_body,
        out_shape=(
            jax.ShapeDtypeStruct((T_PART, D_PACK), jnp.int32),
            jax.ShapeDtypeStruct((TOP_K, T_PART), jnp.int32),
            jax.ShapeDtypeStruct((TOP_K, T_PART), F32),
            jax.ShapeDtypeStruct((TOP_K, T_PART), jnp.int32),
            jax.ShapeDtypeStruct((N_EXPERTS, 1), jnp.int32),
            jax.ShapeDtypeStruct((T_PART, D_MODEL), F32),
        ),
        grid=(NT_PART,),
        in_specs=[
            pl.BlockSpec((TM, D_MODEL), lambda i: (i + tile0, 0)),
            pl.BlockSpec((None, 6, D_MODEL), lambda i: (_mod_row(i + tile0), 0, 0)),
            _const_spec((1, D_MODEL)),
            _const_spec((N_EXPERTS, D_MODEL)),
            _const_spec((N_EXPERTS, 1)),
            _const_spec((D_MODEL, D_SHARED)),
            _const_spec((D_MODEL, D_SHARED)),
            _const_spec((D_SHARED, D_MODEL)),
        ],
        out_specs=(
            pl.BlockSpec((TM, D_PACK), lambda i: (i, 0)),
            pl.BlockSpec((TOP_K, TM), lambda i: (0, i)),
            pl.BlockSpec((TOP_K, TM), lambda i: (0, i)),
            pl.BlockSpec((TOP_K, TM), lambda i: (0, i)),
            _const_spec((N_EXPERTS, 1)),
            _tile_spec(),
        ),
        scratch_shapes=[pltpu.VMEM((N_EXPERTS, 1), F32)],
        compiler_params=_params("arbitrary"),
        name="moe_router_shared",
    )(x, mod, norm_ffn.reshape(1, -1), w_router.T, b_router.reshape(N_EXPERTS, 1),
      w_sh_gate.astype(BF16), w_sh_up.astype(BF16), w_sh_down.astype(BF16))


EXPERT_RING = 4


WEIGHT_SLOTS = 2


EXPERT_TILES_PER_STEP = 4


def _expert_body(be_ref, nv_ref, nu_ref, run_ref, rexp_ref, nrun_ref, xs_hbm, wg_hbm, wu_hbm, wd_hbm, ys_ref,
                 wgu_s, wd_s, ring, sems, wg_buf, wu_buf, wd_buf, wsems, *, layer):
    n_used = nu_ref[0]
    n_runs = nrun_ref[0]

    def fetch(t):
        slot = t % EXPERT_RING
        return pltpu.make_async_copy(xs_hbm.at[pl.ds(t * TE, TE)], ring.at[slot], sems.at[slot])

    def wfetch(r):
        slot = r % WEIGHT_SLOTS
        e = rexp_ref[r]
        return [pltpu.make_async_copy(wg_hbm.at[layer, e], wg_buf.at[slot], wsems.at[slot, 0]),
                pltpu.make_async_copy(wu_hbm.at[layer, e], wu_buf.at[slot], wsems.at[slot, 1]),
                pltpu.make_async_copy(wd_hbm.at[layer, e], wd_buf.at[slot], wsems.at[slot, 2])]

    def tile_step(j, out_rows):
        live_tile = j < n_used

        @pl.when(j == 0)
        def _():
            for t in range(EXPERT_RING - 1):
                @pl.when(t < n_used)
                def _():
                    fetch(t).start()
            for r in range(WEIGHT_SLOTS):
                @pl.when(r < n_runs)
                def _():
                    for cp in wfetch(r):
                        cp.start()

        @pl.when(j + (EXPERT_RING - 1) < n_used)
        def _():
            fetch(j + (EXPERT_RING - 1)).start()

        @pl.when(jnp.logical_and(live_tile, jnp.logical_or(j == 0, be_ref[j] != be_ref[jnp.maximum(j - 1, 0)])))
        def _():
            r = run_ref[j]
            slot = r % WEIGHT_SLOTS
            for cp in wfetch(r):
                cp.wait()
            wgu_s[:, :D_EXPERT] = wg_buf[slot].astype(BF16)
            wgu_s[:, D_EXPERT:] = wu_buf[slot].astype(BF16)
            wd_s[...] = wd_buf[slot].astype(BF16)

            @pl.when(r + WEIGHT_SLOTS < n_runs)
            def _():
                for cp in wfetch(r + WEIGHT_SLOTS):
                    cp.start()

        @pl.when(live_tile)
        def _():
            fetch(j).wait()
            live = lax.broadcasted_iota(jnp.int32, (TE, 1), 0) < nv_ref[j]
            lo, hi = _unpack_pairs(jnp.where(live, ring[j % EXPERT_RING], 0))
            xb = jnp.concatenate([lo.astype(BF16), hi.astype(BF16)], axis=1)
            hgu = _bdot(xb, wgu_s[...])
            hh = _silu(hgu[:, :D_EXPERT]) * hgu[:, D_EXPERT:]
            ys_ref[out_rows, :] = _pack_pairs(_bdot(hh.astype(BF16), wd_s[...]))

    for u in range(EXPERT_TILES_PER_STEP):
        tile_step(pl.program_id(0) * EXPERT_TILES_PER_STEP + u, pl.ds(u * TE, TE))


def _experts(block_expert, block_rows, n_used, block_run, run_expert, n_runs, xs, w_gate, w_up, w_down, layer):
    g = EXPERT_TILES_PER_STEP
    assert NB % g == 0

    def row_map(s, be, nv, nu, run, rexp, nrun):
        return (jnp.minimum(s, (nu[0] - 1) // g), 0)

    any_spec = pl.BlockSpec(memory_space=pl.ANY)
    return pl.pallas_call(
        functools.partial(_expert_body, layer=layer),
        out_shape=jax.ShapeDtypeStruct((P_ROWS, D_PACK), jnp.int32),
        grid_spec=pltpu.PrefetchScalarGridSpec(
            num_scalar_prefetch=6,
            grid=(NB // g,),
            in_specs=[any_spec, any_spec, any_spec, any_spec],
            out_specs=pl.BlockSpec((g * TE, D_PACK), row_map),
            scratch_shapes=[
                pltpu.VMEM((D_MODEL, 2 * D_EXPERT), BF16),
                pltpu.VMEM((D_EXPERT, D_MODEL), BF16),
                pltpu.VMEM((EXPERT_RING, TE, D_PACK), jnp.int32),
                pltpu.SemaphoreType.DMA((EXPERT_RING,)),
                pltpu.VMEM((WEIGHT_SLOTS, D_MODEL, D_EXPERT), F32),
                pltpu.VMEM((WEIGHT_SLOTS, D_MODEL, D_EXPERT), F32),
                pltpu.VMEM((WEIGHT_SLOTS, D_EXPERT, D_MODEL), F32),
                pltpu.SemaphoreType.DMA((WEIGHT_SLOTS, 3)),
            ],
        ),
        compiler_params=_params("arbitrary"),
        name="moe_experts",
    )(block_expert, block_rows, n_used, block_run, run_expert, n_runs, xs, w_gate, w_up, w_down)


def _positions_body(start_ref, eidx_ref, rank_ref, pos_ref):
    eidx = eidx_ref[...]
    base = jnp.zeros(eidx.shape, jnp.int32)
    for e in range(N_EXPERTS):
        base = jnp.where(eidx == e, start_ref[e], base)
    pos_ref[...] = base * TE + rank_ref[...]


def _positions(blk_start, eidx_t, rank_t):
    full = pl.BlockSpec((TOP_K, T_PART), lambda i, s: (0, 0))
    return pl.pallas_call(
        _positions_body,
        out_shape=jax.ShapeDtypeStruct((TOP_K, T_PART), jnp.int32),
        grid_spec=pltpu.PrefetchScalarGridSpec(
            num_scalar_prefetch=1, grid=(1,), in_specs=[full, full], out_specs=full),
        compiler_params=_params("arbitrary"),
        name="moe_positions",
    )(blk_start, eidx_t, rank_t)


def _sc_mesh():
    return plsc.VectorSubcoreMesh(core_axis_name="c", subcore_axis_name="s")


def _sc_worker():
    return lax.axis_index("s") * SC_CORES + lax.axis_index("c")


def _sc_scatter_rows(h, pos_rows):
    c = SC_CHUNK
    n_chunks = T_PART // SC_WORKERS // c
    width = h.shape[1]

    @functools.partial(
        pl.kernel, mesh=_sc_mesh(),
        out_type=jax.ShapeDtypeStruct((P_ROWS, width), h.dtype),
        scratch_types=[pltpu.VMEM((n_chunks * TOP_K, c), jnp.int32),
                       pltpu.VMEM((c, width), h.dtype), pltpu.VMEM((c, width), h.dtype)]
        + [pltpu.SemaphoreType.DMA] * 4,
        name="moe_dispatch_scatter",
    )
    def scatter(h_hbm, pos_hbm, xs_hbm, idx_v, buf0, buf1, sem_in0, sem_in1, sem_out0, sem_out1):
        assert n_chunks % 2 == 0
        bufs, sem_in, sem_out = (buf0, buf1), (sem_in0, sem_in1), (sem_out0, sem_out1)
        first = _sc_worker() * n_chunks
        pltpu.sync_copy(pos_hbm.at[pl.ds(first * TOP_K, n_chunks * TOP_K)], idx_v)

        def load(i, b):
            return pltpu.make_async_copy(h_hbm.at[pl.ds((first + i) * c, c)], bufs[b], sem_in[b])

        def puts(i, b):
            return [pltpu.make_async_copy(bufs[b], xs_hbm.at[idx_v.at[i * TOP_K + k]], sem_out[b])
                    for k in range(TOP_K)]

        load(0, 0).start()
        load(1, 1).start()

        @pl.loop(0, n_chunks, step=2)
        def _(i):
            for b in range(2):
                load(i + b, b).wait()
                for cp in puts(i + b, b):
                    cp.start()
            for b in range(2):
                for cp in puts(i + b, b):
                    cp.wait()

                @pl.when(i + 2 + b < n_chunks)
                def _():
                    load(i + 2 + b, b).start()

    return scatter(h, pos_rows)


def _sc_combine(ys, pos_tk, wsplat):
    ct = COMBINE_TOKENS
    rows = ct * TOP_K
    tok_per_worker = T_PART // SC_WORKERS
    n_chunks = tok_per_worker // ct
    width = ys.shape[1]
    vmem = pltpu.VMEM

    @functools.partial(
        pl.kernel, mesh=_sc_mesh(),
        out_type=jax.ShapeDtypeStruct((T_PART, D_MODEL), F32),
        scratch_types=[vmem((tok_per_worker * TOP_K,), jnp.int32),
                       vmem((rows, width), ys.dtype), vmem((rows, width), ys.dtype),
                       vmem((ct, TOP_K * SC_LANES), F32), vmem((ct, TOP_K * SC_LANES), F32),
                       vmem((ct, D_MODEL), F32), vmem((ct, D_MODEL), F32)]
        + [pltpu.SemaphoreType.DMA] * 6,
        compiler_params=pltpu.CompilerParams(needs_layout_passes=False),
        name="moe_combine",
    )
    def combine(ys_hbm, pos_hbm, w_hbm, out_hbm, idx_v, g0, g1, w0, w1, o0, o1,
                sem_g0, sem_g1, sem_w0, sem_w1, sem_o0, sem_o1):
        assert n_chunks % 2 == 0
        gbuf, wbuf, obuf = (g0, g1), (w0, w1), (o0, o1)
        sem_g, sem_w, sem_o = (sem_g0, sem_g1), (sem_w0, sem_w1), (sem_o0, sem_o1)
        tok0 = _sc_worker() * tok_per_worker
        pltpu.sync_copy(pos_hbm.at[pl.ds(tok0 * TOP_K, tok_per_worker * TOP_K)], idx_v)

        def fetch(i, b):
            return [pltpu.make_async_copy(ys_hbm.at[idx_v.at[pl.ds(i * rows, rows)]], gbuf[b], sem_g[b]),
                    pltpu.make_async_copy(w_hbm.at[pl.ds(tok0 + i * ct, ct)], wbuf[b], sem_w[b])]

        def flush(i, b):
            return pltpu.make_async_copy(obuf[b], out_hbm.at[pl.ds(tok0 + i * ct, ct)], sem_o[b])

        def reduce_chunk(b):
            @pl.loop(0, ct)
            def _(t):
                wv = [wbuf[b][t, pl.ds(k * SC_LANES, SC_LANES)] for k in range(TOP_K)]

                @pl.loop(0, width // SC_LANES)
                def _(j):
                    col = j * SC_LANES
                    acc_lo = acc_hi = None
                    for k in range(TOP_K):
                        word = gbuf[b][t * TOP_K + k, pl.ds(col, SC_LANES)]
                        lo = lax.bitcast_convert_type(word << 16, F32)
                        hi = lax.bitcast_convert_type(word & jnp.int32(-65536), F32)
                        acc_lo = wv[k] * lo if k == 0 else acc_lo + wv[k] * lo
                        acc_hi = wv[k] * hi if k == 0 else acc_hi + wv[k] * hi
                    obuf[b][t, pl.ds(col, SC_LANES)] = acc_lo
                    obuf[b][t, pl.ds(D_PACK + col, SC_LANES)] = acc_hi

        for b in range(2):
            for cp in fetch(b, b):
                cp.start()

        @pl.loop(0, n_chunks, step=2)
        def _(i):
            for b in range(2):
                for cp in fetch(i + b, b):
                    cp.wait()

                @pl.when(i > 0)
                def _():
                    flush(i + b - 2, b).wait()

                reduce_chunk(b)
                flush(i + b, b).start()

                @pl.when(i + 2 + b < n_chunks)
                def _():
                    for cp in fetch(i + 2 + b, b):
                        cp.start()

        for b in range(2):
            flush(n_chunks - 2 + b, b).wait()

    return combine(ys, pos_tk, wsplat)


def _moe_routed(h, eidx_t, rank_t, counts, wsplat, w_gate, w_up, w_down, layer):
    counts = counts.reshape(1, N_EXPERTS)
    nblk = (counts + TE - 1) // TE
    blk_end = jnp.cumsum(nblk, axis=1)
    blk_start = blk_end - nblk
    blocks = jnp.arange(NB, dtype=jnp.int32).reshape(NB, 1)
    block_expert = jnp.minimum(jnp.sum(blocks >= blk_end, axis=1, keepdims=True), N_EXPERTS - 1)
    mine = block_expert == jnp.arange(N_EXPERTS, dtype=jnp.int32).reshape(1, N_EXPERTS)
    cnt_b = jnp.sum(jnp.where(mine, counts, 0), axis=1, keepdims=True)
    start_b = jnp.sum(jnp.where(mine, blk_start, 0), axis=1, keepdims=True)
    block_rows = jnp.clip(cnt_b - (blocks - start_b) * TE, 0, TE)
    n_used = blk_end[0, -1].reshape(1).astype(jnp.int32)
    present = (nblk > 0).astype(jnp.int32)
    run_of_expert = jnp.cumsum(present, axis=1) - 1
    block_run = jnp.sum(jnp.where(mine, run_of_expert, 0), axis=1)
    experts = jnp.arange(N_EXPERTS, dtype=jnp.int32).reshape(1, N_EXPERTS)
    run_hit = jnp.logical_and(run_of_expert == experts.reshape(N_EXPERTS, 1), present > 0)
    run_expert = jnp.sum(jnp.where(run_hit, experts, 0), axis=1)
    n_runs = jnp.sum(present).reshape(1)
    pos = _positions(blk_start.reshape(N_EXPERTS).astype(jnp.int32), eidx_t, rank_t)
    pos_rows = pos.reshape(TOP_K, T_PART // SC_CHUNK, SC_CHUNK).transpose(1, 0, 2).reshape(-1, SC_CHUNK)
    xs = _sc_scatter_rows(h, pos_rows)
    ys = _experts(block_expert.reshape(NB).astype(jnp.int32), block_rows.reshape(NB).astype(jnp.int32),
                  n_used, block_run.astype(jnp.int32), run_expert.astype(jnp.int32), n_runs.astype(jnp.int32),
                  xs, w_gate, w_up, w_down, layer)
    return _sc_combine(ys, pos.T.reshape(-1), wsplat)


N_SLABS = (3 * C_FDIM + 2 * D_MODEL) // 128


F32_GROUPS = (1, 2)
B16_GROUPS = (0, 3, 4)


def _hgrn_in_body(acc_ref, rt_ref, mod0_ref, mod1_ref, nmix_ref, win_ref, *rest):
    x_ref, zf_ref, zb_ref = rest[-3:]
    x = acc_ref[...] + mod0_ref[5:6, :] * rt_ref[...]
    x_ref[...] = x
    hb = _modulate(x, nmix_ref[...], mod1_ref[0:1, :], mod1_ref[1:2, :]).astype(BF16)
    for s in range(N_SLABS // C_HEADS):
        zz = _bdot(hb, win_ref[:, s * D_MODEL:(s + 1) * D_MODEL])
        for hh in range(C_HEADS):
            if s in F32_GROUPS:
                zf_ref[F32_GROUPS.index(s) * C_HEADS + hh] = zz[:, hh * 128:(hh + 1) * 128]
            else:
                zb_ref[B16_GROUPS.index(s) * C_HEADS + hh] = zz[:, hh * 128:(hh + 1) * 128].astype(BF16)


HGRN_IN_TILES = 2


def _hgrn_in(acc, routed, tile0, prev, mod0, mod1, norm_mix, w_in):
    g = HGRN_IN_TILES
    assert NT_PART % g == 0 and tile0 % g == 0 and NT_P % g == 0 and TILES_PER_LAT % g == 0
    rows = g * TM
    block0 = tile0 // g

    def shifted_mod():
        return pl.BlockSpec((None, 6, D_MODEL), lambda i: (_mod_row(i * g + tile0), 0, 0))

    local = pl.BlockSpec((rows, D_MODEL), lambda i: (i, 0))
    any_spec = pl.BlockSpec(memory_space=pl.ANY)
    prev = () if prev is None else tuple(prev)
    n_in = 6
    return pl.pallas_call(
        _hgrn_in_body,
        out_shape=(
            jax.ShapeDtypeStruct((T, D_MODEL), F32),
            jax.ShapeDtypeStruct((len(F32_GROUPS) * C_HEADS, T, 128), F32),
            jax.ShapeDtypeStruct((len(B16_GROUPS) * C_HEADS, T, 128), BF16),
        ),
        grid=(NT_PART // g,),
        in_specs=[
            local, local, shifted_mod(), shifted_mod(),
            _const_spec((1, D_MODEL)),
            pl.BlockSpec((D_MODEL, 3 * C_FDIM + 2 * D_MODEL), lambda i: (0, 0), pipeline_mode=pl.Buffered(1)),
        ] + [any_spec] * len(prev),
        out_specs=(
            pl.BlockSpec((rows, D_MODEL), lambda i: (i + block0, 0)),
            pl.BlockSpec((len(F32_GROUPS) * C_HEADS, rows, 128), lambda i: (0, i + block0, 0)),
            pl.BlockSpec((len(B16_GROUPS) * C_HEADS, rows, 128), lambda i: (0, i + block0, 0)),
        ),
        input_output_aliases={n_in + k: k for k in range(len(prev))},
        compiler_params=_params("parallel"),
        name="hgrn_in_proj",
    )(acc, routed, mod0, mod1, norm_mix.reshape(1, -1), w_in.astype(BF16), *prev)


def _gla_body(q_ref, f_ref, v_ref, lb_ref, s0_ref, o_ref, ns_ref, st_ref, *, rev):
    j = pl.program_id(0)
    ti = NT - 1 - j if rev else j
    is_ctx = ti < NT_P
    first_lat = (ti - NT_P) % TILES_PER_LAT == (TILES_PER_LAT - 1 if rev else 0)

    @pl.when(is_ctx)
    def _():
        st_ref[...] = jnp.zeros(st_ref.shape, F32)

    @pl.when(jnp.logical_and(jnp.logical_not(is_ctx), first_lat))
    def _():
        st_ref[...] = s0_ref[...]

    row = lax.broadcasted_iota(jnp.int32, (TM, TM), 0)
    col = lax.broadcasted_iota(jnp.int32, (TM, TM), 1)
    same_chunk = (row // SCAN_CHUNK) == (col // SCAN_CHUNK)
    seen = jnp.logical_and(same_chunk, (col >= row) if rev else (col <= row))
    cum_w = seen.astype(BF16)
    mid = SCAN_CHUNK // 2 if rev else SCAN_CHUNK // 2 - 1
    last = 0 if rev else SCAN_CHUNK - 1
    n_chunks = TM // SCAN_CHUNK
    order = range(n_chunks - 1, -1, -1) if rev else range(n_chunks)
    group = 4

    def chunk_rows(b, off):
        return jnp.concatenate(
            [jnp.broadcast_to(b[c * SCAN_CHUNK + off:c * SCAN_CHUNK + off + 1, :], (SCAN_CHUNK, b.shape[1]))
             for c in range(n_chunks)], axis=0)

    def head_group(gi, carry):
        heads = [gi * group + u for u in range(group)]
        qs, kk, vv, bcum = [], [], [], []
        for hd in heads:
            lb = lb_ref[hd]
            qs.append(_silu(q_ref[hd].astype(F32)) * (C_DK ** -0.5))
            fg = lb + (1.0 - lb) * jax.nn.sigmoid(f_ref[hd])
            kk.append(1.0 - fg)
            vv.append(v_ref[hd])
            g = jnp.log(fg)
            g_hi = g.astype(BF16)
            r1 = g - g_hi.astype(F32)
            g_mid = r1.astype(BF16)
            g_lo = (r1 - g_mid.astype(F32)).astype(BF16)
            bcum.append(_bdot(cum_w, g_hi) + _bdot(cum_w, g_mid) + _bdot(cum_w, g_lo))
        o_intra, q_dec, kv, decay = [], [], [], []
        for u in range(group):
            b_mid = chunk_rows(bcum[u], mid)
            b_last = chunk_rows(bcum[u], last)
            qe = (qs[u] * jnp.exp(bcum[u] - b_mid)).astype(BF16)
            ke = (kk[u] * jnp.exp(b_mid - bcum[u])).astype(BF16)
            att = jnp.where(seen, _dot_nt(qe, ke), 0.0)
            o_intra.append(_bdot(att.astype(BF16), vv[u]))
            q_dec.append((qs[u] * jnp.exp(bcum[u])).astype(BF16))
            k_dec = (kk[u] * jnp.exp(b_last - bcum[u])).astype(BF16)
            kv.append([_dot_tn(vv[u][c * SCAN_CHUNK:(c + 1) * SCAN_CHUNK], k_dec[c * SCAN_CHUNK:(c + 1) * SCAN_CHUNK])
                       for c in range(n_chunks)])
            decay.append([jnp.exp(bcum[u][c * SCAN_CHUNK + last:c * SCAN_CHUNK + last + 1, :])
                          for c in range(n_chunks)])
        st = [st_ref[hd] for hd in heads]
        for c in order:
            sl = slice(c * SCAN_CHUNK, (c + 1) * SCAN_CHUNK)
            for u, hd in enumerate(heads):
                o_ref[hd, pl.ds(c * SCAN_CHUNK, SCAN_CHUNK), :] = (
                    o_intra[u][sl] + _dot_nt(q_dec[u][sl], st[u].astype(BF16))).astype(BF16)
                st[u] = decay[u][c] * st[u] + kv[u][c]
        for u, hd in enumerate(heads):
            st_ref[hd] = st[u]
        return carry

    lax.fori_loop(0, C_HEADS // group, head_group, 0)

    @pl.when(is_ctx)
    def _():
        ns_ref[...] = st_ref[...]


def _gla(zf, zb, lb_dir, s0t_dir, *, rev):
    def ti_of(j):
        return NT - 1 - j if rev else j

    f_slab = F32_GROUPS.index(2 if rev else 1)

    def lat_map(j):
        return (jnp.clip((ti_of(j) - NT_P) // TILES_PER_LAT, 0, DEC_BATCH - 1), 0, 0, 0)

    return pl.pallas_call(
        functools.partial(_gla_body, rev=rev),
        out_shape=(
            jax.ShapeDtypeStruct((C_HEADS, T, C_DV), BF16),
            jax.ShapeDtypeStruct((BATCH, C_HEADS, C_DV, C_DK), F32),
        ),
        grid=(NT,),
        in_specs=[
            pl.BlockSpec((C_HEADS, TM, 128), lambda j: (B16_GROUPS.index(0), ti_of(j), 0)),
            pl.BlockSpec((C_HEADS, TM, 128), lambda j: (f_slab, ti_of(j), 0)),
            pl.BlockSpec((C_HEADS, TM, 128), lambda j: (B16_GROUPS.index(3), ti_of(j), 0)),
            _const_spec((C_HEADS, 1, C_DK)),
            pl.BlockSpec((None, C_HEADS, C_DV, C_DK), lat_map),
        ],
        out_specs=(
            pl.BlockSpec((C_HEADS, TM, C_DV), lambda j: (0, ti_of(j), 0)),
            pl.BlockSpec((None, C_HEADS, C_DV, C_DK),
                         lambda j: (jnp.minimum(ti_of(j), NT_P - 1), 0, 0, 0)),
        ),
        scratch_shapes=[pltpu.VMEM((C_HEADS, C_DV, C_DK), F32)],
        compiler_params=_params("arbitrary"),
        name="gla_bwd" if rev else "gla_fwd",
    )(zb, zf, zb, lb_dir, s0t_dir)


def _hgrn_out_body(ofw_ref, obw_ref, gate_ref, x_ref, mod_ref, no_ref, wout_ref, o_ref, cat_ref):
    for hd in range(C_HEADS):
        o = ofw_ref[hd].astype(F32) + obw_ref[hd].astype(F32)
        gate = gate_ref[hd].astype(F32)
        cat_ref[:, hd * C_DV:(hd + 1) * C_DV] = (_rms(o, no_ref[...]) * _silu(gate)).astype(BF16)
    o_ref[...] = x_ref[...] + mod_ref[2:3, :] * _bdot(cat_ref[...], wout_ref[...])


def _hgrn_out(o_fw, o_bw, zb, x, mod, norm_o, w_out):
    head_spec = pl.BlockSpec((C_HEADS, TM, C_DV), lambda i: (0, i, 0))
    return pl.pallas_call(
        _hgrn_out_body,
        out_shape=jax.ShapeDtypeStruct((T, D_MODEL), F32),
        grid=(NT,),
        in_specs=[
            head_spec, head_spec,
            pl.BlockSpec((C_HEADS, TM, 128), lambda i: (B16_GROUPS.index(4), i, 0)),
            _tile_spec(), _mod_spec(),
            _const_spec((1, C_DV)),
            _const_spec((D_MODEL, D_MODEL)),
        ],
        out_specs=_tile_spec(),
        scratch_shapes=[pltpu.VMEM((TM, D_MODEL), BF16)],
        compiler_params=_params("parallel"),
        name="hgrn_out_proj",
    )(o_fw, o_bw, zb, x, mod, norm_o.reshape(1, -1), w_out.astype(BF16))


def _final_body(acc_ref, rt_ref, mod_ref, nf_ref, *rest):
    o_ref = rest[-1]
    o_ref[...] = _rms(acc_ref[...] + mod_ref[5:6, :] * rt_ref[...], nf_ref[...])


FINAL_TILES = 4


def _final(acc, routed, mod, norm_final, part_tile0, local0, n_tiles, out_tile0, out_tiles, prev=None):
    g = FINAL_TILES
    assert all(v % g == 0 for v in (part_tile0, local0, n_tiles, out_tile0, NT_P, TILES_PER_LAT))
    rows = g * TM
    local = pl.BlockSpec((rows, D_MODEL), lambda i: (i + local0 // g, 0))
    prev = () if prev is None else (prev,)
    return pl.pallas_call(
        _final_body,
        out_shape=jax.ShapeDtypeStruct((out_tiles * TM, D_MODEL), F32),
        grid=(n_tiles // g,),
        in_specs=[
            local, local,
            pl.BlockSpec((None, 6, D_MODEL), lambda i: (_mod_row(i * g + local0 + part_tile0), 0, 0)),
            _const_spec((1, D_MODEL)),
        ] + [pl.BlockSpec(memory_space=pl.ANY)] * len(prev),
        out_specs=pl.BlockSpec((rows, D_MODEL), lambda i: (i + out_tile0 // g, 0)),
        input_output_aliases={4 + k: 0 for k in range(len(prev))},
        compiler_params=_params("parallel"),
        name="final_norm",
    )(acc, routed, mod, norm_final.reshape(1, -1), *prev)


def kernel(x_prompt, x_sample, state_hgrn, c, c_ctx, w_ada, b_ada, norm_mix, norm_ffn, w_out, w_in_ab, w_sg, b_sg, norm_sg, w_dw, b_dw, norm_cv, w_in_hgrn, lb_raw, norm_o, w_router, b_router, w_gate, w_up, w_down, w_sh_gate, w_sh_up, w_sh_down, norm_final):
    cvecs = jnp.concatenate(
        [c_ctx.reshape(1, D_MODEL), c, jnp.zeros((N_MOD_ROWS - 1 - DEC_BATCH, D_MODEL), F32)], axis=0)
    mods = _ada_tables(cvecs, w_ada, b_ada)
    lb_sm = jax.nn.softmax(lb_raw.astype(F32), axis=0)
    lb1 = (jnp.cumsum(lb_sm, axis=0) - lb_sm[0])[1].reshape(2, C_HEADS, 1, C_DK)

    def moe(l, xin):
        parts = []
        for p in range(MOE_PARTS):
            h, eidx_t, w8_t, rank_t, counts, acc = _moe_pre(
                xin, mods[l], norm_ffn[l], w_router[l], b_router[l], w_sh_gate[l], w_sh_up[l], w_sh_down[l],
                p * NT_PART)
            wsplat = jnp.repeat(w8_t.T, SC_LANES, axis=1)
            parts += [acc, _moe_routed(h, eidx_t, rank_t, counts, wsplat, w_gate, w_up, w_down, l)]
        return parts

    x = _l0_mixer(x_prompt.reshape(T_P, D_MODEL), x_sample.reshape(T_S, D_MODEL), mods[0], norm_mix[0], w_in_ab[0], w_sg[0], b_sg[0], norm_sg[0], w_dw[0],
                  b_dw[0], norm_cv[0], w_out[0])
    parts = moe(0, x)
    xz = None
    for p in range(MOE_PARTS):
        xz = _hgrn_in(parts[2 * p], parts[2 * p + 1], p * NT_PART, xz, mods[0], mods[1], norm_mix[1], w_in_hgrn[0])
    x, zf, zb = xz
    s0t = jnp.swapaxes(state_hgrn[:, 0].astype(F32), -1, -2)
    o_fw, ns_fw = _gla(zf, zb, lb1[0], s0t[:, 0], rev=False)
    o_bw, ns_bw = _gla(zf, zb, lb1[1], s0t[:, 1], rev=True)
    x = _hgrn_out(o_fw, o_bw, zb, x, mods[1], norm_o[0], w_out[1])
    parts = moe(1, x)
    y_p = y_s = None
    for p in range(MOE_PARTS):
        lo, hi = p * NT_PART, (p + 1) * NT_PART
        if lo < NT_P:
            n = min(hi, NT_P) - lo
            y_p = _final(parts[2 * p], parts[2 * p + 1], mods[1], norm_final, lo, 0, n, lo, NT_P, y_p)
        if hi > NT_P:
            first = max(lo, NT_P)
            y_s = _final(parts[2 * p], parts[2 * p + 1], mods[1], norm_final, lo, first - lo, hi - first,
                         first - NT_P, NT_S, y_s)
    y_p = y_p.reshape(BATCH, SEQ, D_MODEL)
    y_s = y_s.reshape(DEC_BATCH, DEC_SEQ, D_MODEL)
    new_state = jnp.swapaxes(jnp.stack([ns_fw, ns_bw], axis=1), -1, -2)[:, None]
    return (y_p, y_s, new_state)
```

```python
import functools

import jax
import jax.numpy as jnp
from jax import lax
from jax.experimental import pallas as pl
from jax.experimental.pallas import tpu as pltpu
from jax.experimental.pallas import tpu_sc as plsc

F32 = jnp.float32
BF16 = jnp.bfloat16
HIGHEST = lax.Precision.HIGHEST

D_MODEL = 1024
BATCH = 32
SEQ = 256
DEPTH = 2
DEC_BATCH = 8
DEC_SEQ = 2048
GRID_W = 64
A_WIDTH = D_MODEL // 2
A_GROUPS = 4
A_GC = A_WIDTH // A_GROUPS
CHUNK_A = 128
B_WIDTH = D_MODEL - A_WIDTH
CONV_W = 31
CONV_PAD = CONV_W // 2
C_HEADS = 8
C_DK = 128
C_DV = D_MODEL // C_HEADS
C_FDIM = C_HEADS * C_DK
SCAN_CHUNK = 64
N_EXPERTS = 64
TOP_K = 8
N_GROUPS = 8
GROUP_SIZE = N_EXPERTS // N_GROUPS
TOPK_GROUPS = 4
D_EXPERT = 256
D_SHARED = 256
ROUTED_SCALE = 2.5
EPS = 1e-6

TM = 256
T_P = BATCH * SEQ
T_S = DEC_BATCH * DEC_SEQ
T = T_P + T_S
NT_P = T_P // TM
NT_S = T_S // TM
NT = NT_P + NT_S
TILES_PER_LAT = DEC_SEQ // TM
TE = 512
MOE_PARTS = 2
NT_PART = NT // MOE_PARTS
T_PART = NT_PART * TM
NB = T_PART * TOP_K // TE + N_EXPERTS
P_ROWS = NB * TE
D_PACK = D_MODEL // 2
N_MOD_ROWS = 16
CONV_HALO = 16
VMEM_LIMIT = 48 * 1024 * 1024
SC_CORES = 2
SC_SUBCORES = 16
SC_WORKERS = SC_CORES * SC_SUBCORES
SC_LANES = 16
SC_CHUNK = 64
COMBINE_TOKENS = SC_CHUNK // TOP_K


def _mod_row(i):
    return jnp.where(i < NT_P, 0, 1 + (i - NT_P) // TILES_PER_LAT)


def _silu(x):
    return x * jax.nn.sigmoid(x)


def _gelu(x):
    return x * (0.5 * (1.0 + jnp.tanh(0.7978845608028654 * (x + 0.044715 * (x * x * x)))))


def _rms(x, g):
    return x * lax.rsqrt(jnp.mean(x * x, axis=-1, keepdims=True) + EPS) * g


def _layernorm(x, g):
    xc = x - jnp.mean(x, axis=-1, keepdims=True)
    return xc * lax.rsqrt(jnp.mean(xc * xc, axis=-1, keepdims=True) + EPS) * g


def _modulate(x, g, shift, scale):
    return _rms(x, g) * (1.0 + scale) + shift


def _bdot(a, b):
    return jnp.dot(a, b, preferred_element_type=F32)


def _dot_nt(a, b, precision=None):
    return lax.dot_general(a, b, (((1,), (1,)), ((), ())), precision=precision,
                           preferred_element_type=F32)


def _dot_tn(a, b):
    return lax.dot_general(a, b, (((0,), (0,)), ((), ())), preferred_element_type=F32)


def _pack_pairs(x):
    m = x.shape[1] // 2
    lo = lax.bitcast_convert_type(x[:, :m].astype(BF16).astype(F32), jnp.uint32)
    hi = lax.bitcast_convert_type(x[:, m:].astype(BF16).astype(F32), jnp.uint32)
    return lax.bitcast_convert_type(hi | (lo >> 16), jnp.int32)


def _unpack_pairs(w):
    u = lax.bitcast_convert_type(w, jnp.uint32)
    lo = lax.bitcast_convert_type(u << 16, F32)
    hi = lax.bitcast_convert_type(u & jnp.uint32(0xFFFF0000), F32)
    return lo, hi


def _params(*sem):
    return pltpu.CompilerParams(dimension_semantics=sem, vmem_limit_bytes=VMEM_LIMIT)


def _const_spec(shape):
    nd = len(shape)
    return pl.BlockSpec(shape, lambda *_: (0,) * nd)


def _ada_body(c_ref, w_ref, b_ref, o_ref):
    s = _silu(c_ref[...])
    o_ref[...] = jnp.dot(s, w_ref[...], precision=HIGHEST, preferred_element_type=F32) + b_ref[...]


def _ada_tables(cvecs, w_ada, b_ada):
    out = pl.pallas_call(
        _ada_body,
        out_shape=jax.ShapeDtypeStruct((DEPTH, N_MOD_ROWS, 6 * D_MODEL), F32),
        grid=(DEPTH, 6),
        in_specs=[
            _const_spec((N_MOD_ROWS, D_MODEL)),
            pl.BlockSpec((None, D_MODEL, D_MODEL), lambda l, j: (l, 0, j)),
            pl.BlockSpec((None, 1, D_MODEL), lambda l, j: (l, 0, j)),
        ],
        out_specs=pl.BlockSpec((None, N_MOD_ROWS, D_MODEL), lambda l, j: (l, 0, j)),
        compiler_params=_params("parallel", "parallel"),
        name="ada_tables",
    )(cvecs, w_ada, b_ada.reshape(DEPTH, 1, 6 * D_MODEL))
    return out.reshape(DEPTH, N_MOD_ROWS, 6, D_MODEL)


def _mod_spec():
    return pl.BlockSpec((None, 6, D_MODEL), lambda i: (_mod_row(i), 0, 0))


def _tile_spec():
    return pl.BlockSpec((TM, D_MODEL), lambda i: (i, 0))


SUBLANES = 8


def _conv_segment(pad_ref, shift_ref, conv_ref, wdw_ref, pad_base, out_base, seg):
    rb = min(seg, 64)
    for cb in range(B_WIDTH // 128):
        cs = slice(cb * 128, (cb + 1) * 128)
        for r0 in range(0, seg, rb):
            acc = jnp.zeros((rb, 128), F32)
            for k in range(CONV_W):
                b = (CONV_HALO - CONV_PAD + k) % SUBLANES
                off = pad_base + r0 + CONV_HALO - CONV_PAD + k - b
                src = pad_ref if b == 0 else shift_ref.at[b - 1]
                acc = acc + wdw_ref[k:k + 1, cs] * src[off:off + rb, cs]
            conv_ref[out_base + r0:out_base + r0 + rb, cs] = acc


def _l0_body(xc_ref, xl_ref, mod_ref, nmix_ref, win_ref, wsg_ref, bsg_ref, nsg_ref, wdw_ref, bdw_ref,
             ncv_ref, wout_ref, o_ref, cat_ref, pad_ref, shift_ref, conv_ref):
    i = pl.program_id(0)
    x = jnp.where(i < NT_P, xc_ref[...], xl_ref[...])
    h = _modulate(x, nmix_ref[...], mod_ref[0:1, :], mod_ref[1:2, :])
    z = _bdot(h.astype(BF16), win_ref[...])
    u = _gelu(z[:, :A_WIDTH])
    vb = _layernorm(_gelu(z[:, A_WIDTH:2 * A_WIDTH]), nsg_ref[...]).astype(BF16)
    for n in range(TM // CHUNK_A):
        rs = slice(n * CHUNK_A, (n + 1) * CHUNK_A)
        for g in range(A_GROUPS):
            cs = slice(g * A_GC, (g + 1) * A_GC)
            m = _bdot(wsg_ref[g], vb[rs, cs]) + bsg_ref[g]
            cat_ref[rs, cs] = (u[rs, cs] * m).astype(BF16)
    hb = z[:, 2 * A_WIDTH:2 * A_WIDTH + B_WIDTH] * jax.nn.sigmoid(z[:, 2 * A_WIDTH + B_WIDTH:])

    def conv_tile(seg):
        stride = seg + 2 * CONV_HALO
        halo = jnp.zeros((CONV_HALO, B_WIDTH), F32)
        for s in range(TM // seg):
            b = s * stride
            pad_ref[b:b + CONV_HALO, :] = halo
            pad_ref[b + CONV_HALO:b + CONV_HALO + seg, :] = hb[s * seg:(s + 1) * seg, :]
            pad_ref[b + CONV_HALO + seg:b + stride, :] = halo
        rows = (TM // seg) * stride - SUBLANES
        for b in range(1, SUBLANES):
            shift_ref[b - 1, 0:rows, :] = pad_ref[b:b + rows, :]
        for s in range(TM // seg):
            _conv_segment(pad_ref, shift_ref, conv_ref, wdw_ref, s * stride, s * seg, seg)

    @pl.when(i < NT_P)
    def _():
        conv_tile(SEQ)

    @pl.when(i >= NT_P)
    def _():
        conv_tile(GRID_W)

    yb = _layernorm(conv_ref[...] + bdw_ref[...], ncv_ref[...])
    cat_ref[:, A_WIDTH:] = _silu(yb).astype(BF16)
    out = _bdot(cat_ref[...], wout_ref[...])
    o_ref[...] = x + mod_ref[2:3, :] * out


def _l0_mixer(x_ctx, x_lat, mod, norm_mix, w_in, w_sg, b_sg, norm_sg, w_dw, b_dw, norm_cv, w_out):
    assert TM == SEQ and TM % GRID_W == 0 and TM % CHUNK_A == 0
    pad_rows = (TM // GRID_W) * (GRID_W + 2 * CONV_HALO)
    assert pad_rows >= SEQ + 2 * CONV_HALO
    return pl.pallas_call(
        _l0_body,
        out_shape=jax.ShapeDtypeStruct((T, D_MODEL), F32),
        grid=(NT,),
        in_specs=[
            pl.BlockSpec((TM, D_MODEL), lambda i: (jnp.minimum(i, NT_P - 1), 0)),
            pl.BlockSpec((TM, D_MODEL), lambda i: (jnp.maximum(i - NT_P, 0), 0)),
            _mod_spec(),
            _const_spec((1, D_MODEL)),
            _const_spec((D_MODEL, 2 * A_WIDTH + 2 * B_WIDTH)),
            _const_spec((A_GROUPS, CHUNK_A, CHUNK_A)),
            _const_spec((A_GROUPS, CHUNK_A, 1)),
            _const_spec((1, A_WIDTH)),
            _const_spec((CONV_W, B_WIDTH)),
            _const_spec((1, B_WIDTH)),
            _const_spec((1, B_WIDTH)),
            _const_spec((D_MODEL, D_MODEL)),
        ],
        out_specs=_tile_spec(),
        scratch_shapes=[
            pltpu.VMEM((TM, D_MODEL), BF16),
            pltpu.VMEM((pad_rows, B_WIDTH), F32),
            pltpu.VMEM((SUBLANES - 1, pad_rows, B_WIDTH), F32),
            pltpu.VMEM((TM, B_WIDTH), F32),
        ],
        compiler_params=_params("parallel"),
        name="l0_mixer",
    )(x_ctx, x_lat, mod, norm_mix.reshape(1, -1), w_in.astype(BF16), w_sg.astype(BF16),
      b_sg.reshape(A_GROUPS, CHUNK_A, 1), norm_sg.reshape(1, -1), w_dw, b_dw.reshape(1, -1),
      norm_cv.reshape(1, -1), w_out.astype(BF16))


def _route(scores, biased):
    n = scores.shape[-1]
    shp = (N_GROUPS, GROUP_SIZE, n)
    s3 = scores.reshape(shp)
    b3 = biased.reshape(shp)
    m_iota = lax.broadcasted_iota(jnp.int32, shp, 1).astype(F32)
    g_iota = lax.broadcasted_iota(jnp.int32, shp, 0).astype(F32)
    e_iota = g_iota * GROUP_SIZE + m_iota
    neg = -jnp.inf

    def amax1(v):
        return jnp.max(v, axis=1, keepdims=True)

    def amin1(v):
        return jnp.min(v, axis=1, keepdims=True)

    m1 = amax1(b3)
    i1 = amin1(jnp.where(b3 == m1, m_iota, float(GROUP_SIZE)))
    m2 = amax1(jnp.where(m_iota == i1, neg, b3))
    grp = m1 + m2
    gi1 = lax.broadcasted_iota(jnp.int32, grp.shape, 0).astype(F32)
    gmask = jnp.zeros(grp.shape, jnp.bool_)
    for _ in range(TOPK_GROUPS):
        gm = jnp.max(grp, axis=0, keepdims=True)
        gi = jnp.min(jnp.where(grp == gm, gi1, float(N_GROUPS)), axis=0, keepdims=True)
        hit = gi1 == gi
        gmask = jnp.logical_or(gmask, hit)
        grp = jnp.where(hit, neg, grp)
    cand = jnp.where(gmask, b3, neg)
    ids, vals, hits = [], [], []
    for _ in range(TOP_K):
        mx = jnp.max(amax1(cand), axis=0, keepdims=True)
        ei = jnp.min(amin1(jnp.where(cand == mx, e_iota, float(N_EXPERTS))), axis=0, keepdims=True)
        hit = e_iota == ei
        ids.append(ei.reshape(1, n))
        vals.append(_pick(hit, s3))
        hits.append(hit)
        cand = jnp.where(hit, neg, cand)
    return jnp.concatenate(ids, axis=0).astype(jnp.int32), jnp.concatenate(vals, axis=0), hits


def _pick(hit, v3):
    s = jnp.sum(jnp.sum(jnp.where(hit, v3, 0.0), axis=1, keepdims=True), axis=0, keepdims=True)
    return s.reshape(1, v3.shape[-1])


def _moe_pre_body(x_ref, mod_ref, nffn_ref, wrt_ref, br_ref, wsg_ref, wsu_ref, wsd_ref,
                  h_ref, eidx_ref, w8_ref, rank_ref, cnt_ref, acc_ref, run_ref):
    @pl.when(pl.program_id(0) == 0)
    def _():
        run_ref[...] = jnp.zeros(run_ref.shape, F32)

    x = x_ref[...]
    h = _modulate(x, nffn_ref[...], mod_ref[3:4, :], mod_ref[4:5, :])
    h_ref[...] = _pack_pairs(h)
    hb = h.astype(BF16)
    h_lo = (h - hb.astype(F32)).astype(BF16)
    wr = wrt_ref[...]
    wr_hi = wr.astype(BF16)
    wr_lo = (wr - wr_hi.astype(F32)).astype(BF16)
    logits_t = _dot_nt(wr_hi, hb) + (_dot_nt(wr_hi, h_lo) + _dot_nt(wr_lo, hb))
    scores = jax.nn.sigmoid(logits_t)
    eidx, sv, hits = _route(scores, scores + br_ref[...])
    eidx_ref[...] = eidx
    w8_ref[...] = sv / jnp.sum(sv, axis=0, keepdims=True) * ROUTED_SCALE
    sel3 = hits[0]
    for hit in hits[1:]:
        sel3 = jnp.logical_or(sel3, hit)
    sel = sel3.astype(F32).reshape(N_EXPERTS, TM)
    earlier = (lax.broadcasted_iota(jnp.int32, (TM, TM), 0)
               < lax.broadcasted_iota(jnp.int32, (TM, TM), 1)).astype(BF16)
    rank3 = (_bdot(sel.astype(BF16), earlier) + run_ref[...]).reshape(N_GROUPS, GROUP_SIZE, TM)
    rank_ref[...] = jnp.concatenate([_pick(hit, rank3) for hit in hits], axis=0).astype(jnp.int32)
    run_ref[...] = run_ref[...] + jnp.sum(sel, axis=1, keepdims=True)
    cnt_ref[...] = run_ref[...].astype(jnp.int32)
    sh = _bdot((_silu(_bdot(hb, wsg_ref[...])) * _bdot(hb, wsu_ref[...])).astype(BF16), wsd_ref[...])
    acc_ref[...] = x + mod_ref[5:6, :] * sh


def _moe_pre(x, mod, norm_ffn, w_router, b_router, w_sh_gate, w_sh_up, w_sh_down, tile0):
    return pl.pallas_call(
        _moe_pre_body,
        out_shape=(
            jax.ShapeDtypeStruct((T_PART, D_PACK), jnp.int32),
            jax.ShapeDtypeStruct((TOP_K, T_PART), jnp.int32),
            jax.ShapeDtypeStruct((TOP_K, T_PART), F32),
            jax.ShapeDtypeStruct((TOP_K, T_PART), jnp.int32),
            jax.ShapeDtypeStruct((N_EXPERTS, 1), jnp.int32),
            jax.ShapeDtypeStruct((T_PART, D_MODEL), F32),
        ),
        grid=(NT_PART,),
        in_specs=[
            pl.BlockSpec((TM, D_MODEL), lambda i: (i + tile0, 0)),
            pl.BlockSpec((None, 6, D_MODEL), lambda i: (_mod_row(i + tile0), 0, 0)),
            _const_spec((1, D_MODEL)),
            _const_spec((N_EXPERTS, D_MODEL)),
            _const_spec((N_EXPERTS, 1)),
            _const_spec((D_MODEL, D_SHARED)),
            _const_spec((D_MODEL, D_SHARED)),
            _const_spec((D_SHARED, D_MODEL)),
        ],
        out_specs=(
            pl.BlockSpec((TM, D_PACK), lambda i: (i, 0)),
            pl.BlockSpec((TOP_K, TM), lambda i: (0, i)),
            pl.BlockSpec((TOP_K, TM), lambda i: (0, i)),
            pl.BlockSpec((TOP_K, TM), lambda i: (0, i)),
            _const_spec((N_EXPERTS, 1)),
            _tile_spec(),
        ),
        scratch_shapes=[pltpu.VMEM((N_EXPERTS, 1), F32)],
        compiler_params=_params("arbitrary"),
        name="moe_router_shared",
    )(x, mod, norm_ffn.reshape(1, -1), w_router.T, b_router.reshape(N_EXPERTS, 1),
      w_sh_gate.astype(BF16), w_sh_up.astype(BF16), w_sh_down.astype(BF16))


EXPERT_RING = 4


WEIGHT_SLOTS = 2


EXPERT_TILES_PER_STEP = 4


def _expert_body(be_ref, nv_ref, nu_ref, run_ref, rexp_ref, nrun_ref, xs_hbm, wg_hbm, wu_hbm, wd_hbm, ys_ref,
                 wgu_s, wd_s, ring, sems, wg_buf, wu_buf, wd_buf, wsems, *, layer):
    n_used = nu_ref[0]
    n_runs = nrun_ref[0]

    def fetch(t):
        slot = t % EXPERT_RING
        return pltpu.make_async_copy(xs_hbm.at[pl.ds(t * TE, TE)], ring.at[slot], sems.at[slot])

    def wfetch(r):
        slot = r % WEIGHT_SLOTS
        e = rexp_ref[r]
        return [pltpu.make_async_copy(wg_hbm.at[layer, e], wg_buf.at[slot], wsems.at[slot, 0]),
                pltpu.make_async_copy(wu_hbm.at[layer, e], wu_buf.at[slot], wsems.at[slot, 1]),
                pltpu.make_async_copy(wd_hbm.at[layer, e], wd_buf.at[slot], wsems.at[slot, 2])]

    def tile_step(j, out_rows):
        live_tile = j < n_used

        @pl.when(j == 0)
        def _():
            for t in range(EXPERT_RING - 1):
                @pl.when(t < n_used)
                def _():
                    fetch(t).start()
            for r in range(WEIGHT_SLOTS):
                @pl.when(r < n_runs)
                def _():
                    for cp in wfetch(r):
                        cp.start()

        @pl.when(j + (EXPERT_RING - 1) < n_used)
        def _():
            fetch(j + (EXPERT_RING - 1)).start()

        @pl.when(jnp.logical_and(live_tile, jnp.logical_or(j == 0, be_ref[j] != be_ref[jnp.maximum(j - 1, 0)])))
        def _():
            r = run_ref[j]
            slot = r % WEIGHT_SLOTS
            for cp in wfetch(r):
                cp.wait()
            wgu_s[:, :D_EXPERT] = wg_buf[slot].astype(BF16)
            wgu_s[:, D_EXPERT:] = wu_buf[slot].astype(BF16)
            wd_s[...] = wd_buf[slot].astype(BF16)

            @pl.when(r + WEIGHT_SLOTS < n_runs)
            def _():
                for cp in wfetch(r + WEIGHT_SLOTS):
                    cp.start()

        @pl.when(live_tile)
        def _():
            fetch(j).wait()
            live = lax.broadcasted_iota(jnp.int32, (TE, 1), 0) < nv_ref[j]
            lo, hi = _unpack_pairs(jnp.where(live, ring[j % EXPERT_RING], 0))
            xb = jnp.concatenate([lo.astype(BF16), hi.astype(BF16)], axis=1)
            hgu = _bdot(xb, wgu_s[...])
            hh = _silu(hgu[:, :D_EXPERT]) * hgu[:, D_EXPERT:]
            ys_ref[out_rows, :] = _pack_pairs(_bdot(hh.astype(BF16), wd_s[...]))

    for u in range(EXPERT_TILES_PER_STEP):
        tile_step(pl.program_id(0) * EXPERT_TILES_PER_STEP + u, pl.ds(u * TE, TE))


def _experts(block_expert, block_rows, n_used, block_run, run_expert, n_runs, xs, w_gate, w_up, w_down, layer):
    g = EXPERT_TILES_PER_STEP
    assert NB % g == 0

    def row_map(s, be, nv, nu, run, rexp, nrun):
        return (jnp.minimum(s, (nu[0] - 1) // g), 0)

    any_spec = pl.BlockSpec(memory_space=pl.ANY)
    return pl.pallas_call(
        functools.partial(_expert_body, layer=layer),
        out_shape=jax.ShapeDtypeStruct((P_ROWS, D_PACK), jnp.int32),
        grid_spec=pltpu.PrefetchScalarGridSpec(
            num_scalar_prefetch=6,
            grid=(NB // g,),
            in_specs=[any_spec, any_spec, any_spec, any_spec],
            out_specs=pl.BlockSpec((g * TE, D_PACK), row_map),
            scratch_shapes=[
                pltpu.VMEM((D_MODEL, 2 * D_EXPERT), BF16),
                pltpu.VMEM((D_EXPERT, D_MODEL), BF16),
                pltpu.VMEM((EXPERT_RING, TE, D_PACK), jnp.int32),
                pltpu.SemaphoreType.DMA((EXPERT_RING,)),
                pltpu.VMEM((WEIGHT_SLOTS, D_MODEL, D_EXPERT), F32),
                pltpu.VMEM((WEIGHT_SLOTS, D_MODEL, D_EXPERT), F32),
                pltpu.VMEM((WEIGHT_SLOTS, D_EXPERT, D_MODEL), F32),
                pltpu.SemaphoreType.DMA((WEIGHT_SLOTS, 3)),
            ],
        ),
        compiler_params=_params("arbitrary"),
        name="moe_experts",
    )(block_expert, block_rows, n_used, block_run, run_expert, n_runs, xs, w_gate, w_up, w_down)


def _positions_body(start_ref, eidx_ref, rank_ref, pos_ref):
    eidx = eidx_ref[...]
    base = jnp.zeros(eidx.shape, jnp.int32)
    for e in range(N_EXPERTS):
        base = jnp.where(eidx == e, start_ref[e], base)
    pos_ref[...] = base * TE + rank_ref[...]


def _positions(blk_start, eidx_t, rank_t):
    full = pl.BlockSpec((TOP_K, T_PART), lambda i, s: (0, 0))
    return pl.pallas_call(
        _positions_body,
        out_shape=jax.ShapeDtypeStruct((TOP_K, T_PART), jnp.int32),
        grid_spec=pltpu.PrefetchScalarGridSpec(
            num_scalar_prefetch=1, grid=(1,), in_specs=[full, full], out_specs=full),
        compiler_params=_params("arbitrary"),
        name="moe_positions",
    )(blk_start, eidx_t, rank_t)


def _sc_mesh():
    return plsc.VectorSubcoreMesh(core_axis_name="c", subcore_axis_name="s")


def _sc_worker():
    return lax.axis_index("s") * SC_CORES + lax.axis_index("c")


def _sc_scatter_rows(h, pos_rows):
    c = SC_CHUNK
    n_chunks = T_PART // SC_WORKERS // c
    width = h.shape[1]

    @functools.partial(
        pl.kernel, mesh=_sc_mesh(),
        out_type=jax.ShapeDtypeStruct((P_ROWS, width), h.dtype),
        scratch_types=[pltpu.VMEM((n_chunks * TOP_K, c), jnp.int32),
                       pltpu.VMEM((c, width), h.dtype), pltpu.VMEM((c, width), h.dtype)]
        + [pltpu.SemaphoreType.DMA] * 4,
        name="moe_dispatch_scatter",
    )
    def scatter(h_hbm, pos_hbm, xs_hbm, idx_v, buf0, buf1, sem_in0, sem_in1, sem_out0, sem_out1):
        assert n_chunks % 2 == 0
        bufs, sem_in, sem_out = (buf0, buf1), (sem_in0, sem_in1), (sem_out0, sem_out1)
        first = _sc_worker() * n_chunks
        pltpu.sync_copy(pos_hbm.at[pl.ds(first * TOP_K, n_chunks * TOP_K)], idx_v)

        def load(i, b):
            return pltpu.make_async_copy(h_hbm.at[pl.ds((first + i) * c, c)], bufs[b], sem_in[b])

        def puts(i, b):
            return [pltpu.make_async_copy(bufs[b], xs_hbm.at[idx_v.at[i * TOP_K + k]], sem_out[b])
                    for k in range(TOP_K)]

        load(0, 0).start()
        load(1, 1).start()

        @pl.loop(0, n_chunks, step=2)
        def _(i):
            for b in range(2):
                load(i + b, b).wait()
                for cp in puts(i + b, b):
                    cp.start()
            for b in range(2):
                for cp in puts(i + b, b):
                    cp.wait()

                @pl.when(i + 2 + b < n_chunks)
                def _():
                    load(i + 2 + b, b).start()

    return scatter(h, pos_rows)


def _sc_combine(ys, pos_tk, wsplat):
    ct = COMBINE_TOKENS
    rows = ct * TOP_K
    tok_per_worker = T_PART // SC_WORKERS
    n_chunks = tok_per_worker // ct
    width = ys.shape[1]
    vmem = pltpu.VMEM

    @functools.partial(
        pl.kernel, mesh=_sc_mesh(),
        out_type=jax.ShapeDtypeStruct((T_PART, D_MODEL), F32),
        scratch_types=[vmem((tok_per_worker * TOP_K,), jnp.int32),
                       vmem((rows, width), ys.dtype), vmem((rows, width), ys.dtype),
                       vmem((ct, TOP_K * SC_LANES), F32), vmem((ct, TOP_K * SC_LANES), F32),
                       vmem((ct, D_MODEL), F32), vmem((ct, D_MODEL), F32)]
        + [pltpu.SemaphoreType.DMA] * 6,
        compiler_params=pltpu.CompilerParams(needs_layout_passes=False),
        name="moe_combine",
    )
    def combine(ys_hbm, pos_hbm, w_hbm, out_hbm, idx_v, g0, g1, w0, w1, o0, o1,
                sem_g0, sem_g1, sem_w0, sem_w1, sem_o0, sem_o1):
        assert n_chunks % 2 == 0
        gbuf, wbuf, obuf = (g0, g1), (w0, w1), (o0, o1)
        sem_g, sem_w, sem_o = (sem_g0, sem_g1), (sem_w0, sem_w1), (sem_o0, sem_o1)
        tok0 = _sc_worker() * tok_per_worker
        pltpu.sync_copy(pos_hbm.at[pl.ds(tok0 * TOP_K, tok_per_worker * TOP_K)], idx_v)

        def fetch(i, b):
            return [pltpu.make_async_copy(ys_hbm.at[idx_v.at[pl.ds(i * rows, rows)]], gbuf[b], sem_g[b]),
                    pltpu.make_async_copy(w_hbm.at[pl.ds(tok0 + i * ct, ct)], wbuf[b], sem_w[b])]

        def flush(i, b):
            return pltpu.make_async_copy(obuf[b], out_hbm.at[pl.ds(tok0 + i * ct, ct)], sem_o[b])

        def reduce_chunk(b):
            @pl.loop(0, ct)
            def _(t):
                wv = [wbuf[b][t, pl.ds(k * SC_LANES, SC_LANES)] for k in range(TOP_K)]

                @pl.loop(0, width // SC_LANES)
                def _(j):
                    col = j * SC_LANES
                    acc_lo = acc_hi = None
                    for k in range(TOP_K):
                        word = gbuf[b][t * TOP_K + k, pl.ds(col, SC_LANES)]
                        lo = lax.bitcast_convert_type(word << 16, F32)
                        hi = lax.bitcast_convert_type(word & jnp.int32(-65536), F32)
                        acc_lo = wv[k] * lo if k == 0 else acc_lo + wv[k] * lo
                        acc_hi = wv[k] * hi if k == 0 else acc_hi + wv[k] * hi
                    obuf[b][t, pl.ds(col, SC_LANES)] = acc_lo
                    obuf[b][t, pl.ds(D_PACK + col, SC_LANES)] = acc_hi

        for b in range(2):
            for cp in fetch(b, b):
                cp.start()

        @pl.loop(0, n_chunks, step=2)
        def _(i):
            for b in range(2):
                for cp in fetch(i + b, b):
                    cp.wait()

                @pl.when(i > 0)
                def _():
                    flush(i + b - 2, b).wait()

                reduce_chunk(b)
                flush(i + b, b).start()

                @pl.when(i + 2 + b < n_chunks)
                def _():
                    for cp in fetch(i + 2 + b, b):
                        cp.start()

        for b in range(2):
            flush(n_chunks - 2 + b, b).wait()

    return combine(ys, pos_tk, wsplat)


def _moe_routed(h, eidx_t, rank_t, counts, wsplat, w_gate, w_up, w_down, layer):
    counts = counts.reshape(1, N_EXPERTS)
    nblk = (counts + TE - 1) // TE
    blk_end = jnp.cumsum(nblk, axis=1)
    blk_start = blk_end - nblk
    blocks = jnp.arange(NB, dtype=jnp.int32).reshape(NB, 1)
    block_expert = jnp.minimum(jnp.sum(blocks >= blk_end, axis=1, keepdims=True), N_EXPERTS - 1)
    mine = block_expert == jnp.arange(N_EXPERTS, dtype=jnp.int32).reshape(1, N_EXPERTS)
    cnt_b = jnp.sum(jnp.where(mine, counts, 0), axis=1, keepdims=True)
    start_b = jnp.sum(jnp.where(mine, blk_start, 0), axis=1, keepdims=True)
    block_rows = jnp.clip(cnt_b - (blocks - start_b) * TE, 0, TE)
    n_used = blk_end[0, -1].reshape(1).astype(jnp.int32)
    present = (nblk > 0).astype(jnp.int32)
    run_of_expert = jnp.cumsum(present, axis=1) - 1
    block_run = jnp.sum(jnp.where(mine, run_of_expert, 0), axis=1)
    experts = jnp.arange(N_EXPERTS, dtype=jnp.int32).reshape(1, N_EXPERTS)
    run_hit = jnp.logical_and(run_of_expert == experts.reshape(N_EXPERTS, 1), present > 0)
    run_expert = jnp.sum(jnp.where(run_hit, experts, 0), axis=1)
    n_runs = jnp.sum(present).reshape(1)
    pos = _positions(blk_start.reshape(N_EXPERTS).astype(jnp.int32), eidx_t, rank_t)
    pos_rows = pos.reshape(TOP_K, T_PART // SC_CHUNK, SC_CHUNK).transpose(1, 0, 2).reshape(-1, SC_CHUNK)
    xs = _sc_scatter_rows(h, pos_rows)
    ys = _experts(block_expert.reshape(NB).astype(jnp.int32), block_rows.reshape(NB).astype(jnp.int32),
                  n_used, block_run.astype(jnp.int32), run_expert.astype(jnp.int32), n_runs.astype(jnp.int32),
                  xs, w_gate, w_up, w_down, layer)
    return _sc_combine(ys, pos.T.reshape(-1), wsplat)


N_SLABS = (3 * C_FDIM + 2 * D_MODEL) // 128


F32_GROUPS = (1, 2)
B16_GROUPS = (0, 3, 4)


def _hgrn_in_body(acc_ref, rt_ref, mod0_ref, mod1_ref, nmix_ref, win_ref, *rest):
    x_ref, zf_ref, zb_ref = rest[-3:]
    x = acc_ref[...] + mod0_ref[5:6, :] * rt_ref[...]
    x_ref[...] = x
    hb = _modulate(x, nmix_ref[...], mod1_ref[0:1, :], mod1_ref[1:2, :]).astype(BF16)
    for s in range(N_SLABS // C_HEADS):
        zz = _bdot(hb, win_ref[:, s * D_MODEL:(s + 1) * D_MODEL])
        for hh in range(C_HEADS):
            if s in F32_GROUPS:
                zf_ref[F32_GROUPS.index(s) * C_HEADS + hh] = zz[:, hh * 128:(hh + 1) * 128]
            else:
                zb_ref[B16_GROUPS.index(s) * C_HEADS + hh] = zz[:, hh * 128:(hh + 1) * 128].astype(BF16)


HGRN_IN_TILES = 2


def _hgrn_in(acc, routed, tile0, prev, mod0, mod1, norm_mix, w_in):
    g = HGRN_IN_TILES
    assert NT_PART % g == 0 and tile0 % g == 0 and NT_P % g == 0 and TILES_PER_LAT % g == 0
    rows = g * TM
    block0 = tile0 // g

    def shifted_mod():
        return pl.BlockSpec((None, 6, D_MODEL), lambda i: (_mod_row(i * g + tile0), 0, 0))

    local = pl.BlockSpec((rows, D_MODEL), lambda i: (i, 0))
    any_spec = pl.BlockSpec(memory_space=pl.ANY)
    prev = () if prev is None else tuple(prev)
    n_in = 6
    return pl.pallas_call(
        _hgrn_in_body,
        out_shape=(
            jax.ShapeDtypeStruct((T, D_MODEL), F32),
            jax.ShapeDtypeStruct((len(F32_GROUPS) * C_HEADS, T, 128), F32),
            jax.ShapeDtypeStruct((len(B16_GROUPS) * C_HEADS, T, 128), BF16),
        ),
        grid=(NT_PART // g,),
        in_specs=[
            local, local, shifted_mod(), shifted_mod(),
            _const_spec((1, D_MODEL)),
            pl.BlockSpec((D_MODEL, 3 * C_FDIM + 2 * D_MODEL), lambda i: (0, 0), pipeline_mode=pl.Buffered(1)),
        ] + [any_spec] * len(prev),
        out_specs=(
            pl.BlockSpec((rows, D_MODEL), lambda i: (i + block0, 0)),
            pl.BlockSpec((len(F32_GROUPS) * C_HEADS, rows, 128), lambda i: (0, i + block0, 0)),
            pl.BlockSpec((len(B16_GROUPS) * C_HEADS, rows, 128), lambda i: (0, i + block0, 0)),
        ),
        input_output_aliases={n_in + k: k for k in range(len(prev))},
        compiler_params=_params("parallel"),
        name="hgrn_in_proj",
    )(acc, routed, mod0, mod1, norm_mix.reshape(1, -1), w_in.astype(BF16), *prev)


def _gla_body(q_ref, f_ref, v_ref, lb_ref, s0_ref, o_ref, ns_ref, st_ref, *, rev):
    j = pl.program_id(0)
    ti = NT - 1 - j if rev else j
    is_ctx = ti < NT_P
    first_lat = (ti - NT_P) % TILES_PER_LAT == (TILES_PER_LAT - 1 if rev else 0)

    @pl.when(is_ctx)
    def _():
        st_ref[...] = jnp.zeros(st_ref.shape, F32)

    @pl.when(jnp.logical_and(jnp.logical_not(is_ctx), first_lat))
    def _():
        st_ref[...] = s0_ref[...]

    row = lax.broadcasted_iota(jnp.int32, (TM, TM), 0)
    col = lax.broadcasted_iota(jnp.int32, (TM, TM), 1)
    same_chunk = (row // SCAN_CHUNK) == (col // SCAN_CHUNK)
    seen = jnp.logical_and(same_chunk, (col >= row) if rev else (col <= row))
    cum_w = seen.astype(BF16)
    mid = SCAN_CHUNK // 2 if rev else SCAN_CHUNK // 2 - 1
    last = 0 if rev else SCAN_CHUNK - 1
    n_chunks = TM // SCAN_CHUNK
    order = range(n_chunks - 1, -1, -1) if rev else range(n_chunks)
    group = 4

    def chunk_rows(b, off):
        return jnp.concatenate(
            [jnp.broadcast_to(b[c * SCAN_CHUNK + off:c * SCAN_CHUNK + off + 1, :], (SCAN_CHUNK, b.shape[1]))
             for c in range(n_chunks)], axis=0)

    def head_group(gi, carry):
        heads = [gi * group + u for u in range(group)]
        qs, kk, vv, bcum = [], [], [], []
        for hd in heads:
            lb = lb_ref[hd]
            qs.append(_silu(q_ref[hd].astype(F32)) * (C_DK ** -0.5))
            fg = lb + (1.0 - lb) * jax.nn.sigmoid(f_ref[hd])
            kk.append(1.0 - fg)
            vv.append(v_ref[hd])
            g = jnp.log(fg)
            g_hi = g.astype(BF16)
            r1 = g - g_hi.astype(F32)
            g_mid = r1.astype(BF16)
            g_lo = (r1 - g_mid.astype(F32)).astype(BF16)
            bcum.append(_bdot(cum_w, g_hi) + _bdot(cum_w, g_mid) + _bdot(cum_w, g_lo))
        o_intra, q_dec, kv, decay = [], [], [], []
        for u in range(group):
            b_mid = chunk_rows(bcum[u], mid)
            b_last = chunk_rows(bcum[u], last)
            qe = (qs[u] * jnp.exp(bcum[u] - b_mid)).astype(BF16)
            ke = (kk[u] * jnp.exp(b_mid - bcum[u])).astype(BF16)
            att = jnp.where(seen, _dot_nt(qe, ke), 0.0)
            o_intra.append(_bdot(att.astype(BF16), vv[u]))
            q_dec.append((qs[u] * jnp.exp(bcum[u])).astype(BF16))
            k_dec = (kk[u] * jnp.exp(b_last - bcum[u])).astype(BF16)
            kv.append([_dot_tn(vv[u][c * SCAN_CHUNK:(c + 1) * SCAN_CHUNK], k_dec[c * SCAN_CHUNK:(c + 1) * SCAN_CHUNK])
                       for c in range(n_chunks)])
            decay.append([jnp.exp(bcum[u][c * SCAN_CHUNK + last:c * SCAN_CHUNK + last + 1, :])
                          for c in range(n_chunks)])
        st = [st_ref[hd] for hd in heads]
        for c in order:
            sl = slice(c * SCAN_CHUNK, (c + 1) * SCAN_CHUNK)
            for u, hd in enumerate(heads):
                o_ref[hd, pl.ds(c * SCAN_CHUNK, SCAN_CHUNK), :] = (
                    o_intra[u][sl] + _dot_nt(q_dec[u][sl], st[u].astype(BF16))).astype(BF16)
                st[u] = decay[u][c] * st[u] + kv[u][c]
        for u, hd in enumerate(heads):
            st_ref[hd] = st[u]
        return carry

    lax.fori_loop(0, C_HEADS // group, head_group, 0)

    @pl.when(is_ctx)
    def _():
        ns_ref[...] = st_ref[...]


def _gla(zf, zb, lb_dir, s0t_dir, *, rev):
    def ti_of(j):
        return NT - 1 - j if rev else j

    f_slab = F32_GROUPS.index(2 if rev else 1)

    def lat_map(j):
        return (jnp.clip((ti_of(j) - NT_P) // TILES_PER_LAT, 0, DEC_BATCH - 1), 0, 0, 0)

    return pl.pallas_call(
        functools.partial(_gla_body, rev=rev),
        out_shape=(
            jax.ShapeDtypeStruct((C_HEADS, T, C_DV), BF16),
            jax.ShapeDtypeStruct((BATCH, C_HEADS, C_DV, C_DK), F32),
        ),
        grid=(NT,),
        in_specs=[
            pl.BlockSpec((C_HEADS, TM, 128), lambda j: (B16_GROUPS.index(0), ti_of(j), 0)),
            pl.BlockSpec((C_HEADS, TM, 128), lambda j: (f_slab, ti_of(j), 0)),
            pl.BlockSpec((C_HEADS, TM, 128), lambda j: (B16_GROUPS.index(3), ti_of(j), 0)),
            _const_spec((C_HEADS, 1, C_DK)),
            pl.BlockSpec((None, C_HEADS, C_DV, C_DK), lat_map),
        ],
        out_specs=(
            pl.BlockSpec((C_HEADS, TM, C_DV), lambda j: (0, ti_of(j), 0)),
            pl.BlockSpec((None, C_HEADS, C_DV, C_DK),
                         lambda j: (jnp.minimum(ti_of(j), NT_P - 1), 0, 0, 0)),
        ),
        scratch_shapes=[pltpu.VMEM((C_HEADS, C_DV, C_DK), F32)],
        compiler_params=_params("arbitrary"),
        name="gla_bwd" if rev else "gla_fwd",
    )(zb, zf, zb, lb_dir, s0t_dir)


def _hgrn_out_body(ofw_ref, obw_ref, gate_ref, x_ref, mod_ref, no_ref, wout_ref, o_ref, cat_ref):
    for hd in range(C_HEADS):
        o = ofw_ref[hd].astype(F32) + obw_ref[hd].astype(F32)
        gate = gate_ref[hd].astype(F32)
        cat_ref[:, hd * C_DV:(hd + 1) * C_DV] = (_rms(o, no_ref[...]) * _silu(gate)).astype(BF16)
    o_ref[...] = x_ref[...] + mod_ref[2:3, :] * _bdot(cat_ref[...], wout_ref[...])


HGRN_OUT_TILES = 2


def _hgrn_out(o_fw, o_bw, zb, x, mod, norm_o, w_out):
    g = HGRN_OUT_TILES
    assert NT % g == 0 and NT_P % g == 0 and TILES_PER_LAT % g == 0
    rows = g * TM
    head_spec = pl.BlockSpec((C_HEADS, rows, C_DV), lambda i: (0, i, 0))
    row_spec = pl.BlockSpec((rows, D_MODEL), lambda i: (i, 0))
    return pl.pallas_call(
        _hgrn_out_body,
        out_shape=jax.ShapeDtypeStruct((T, D_MODEL), F32),
        grid=(NT // g,),
        in_specs=[
            head_spec, head_spec,
            pl.BlockSpec((C_HEADS, rows, 128), lambda i: (B16_GROUPS.index(4), i, 0)),
            row_spec,
            pl.BlockSpec((None, 6, D_MODEL), lambda i: (_mod_row(i * g), 0, 0)),
            _const_spec((1, C_DV)),
            _const_spec((D_MODEL, D_MODEL)),
        ],
        out_specs=row_spec,
        scratch_shapes=[pltpu.VMEM((rows, D_MODEL), BF16)],
        compiler_params=_params("parallel"),
        name="hgrn_out_proj",
    )(o_fw, o_bw, zb, x, mod, norm_o.reshape(1, -1), w_out.astype(BF16))


def _final_body(acc_ref, rt_ref, mod_ref, nf_ref, *rest):
    o_ref = rest[-1]
    o_ref[...] = _rms(acc_ref[...] + mod_ref[5:6, :] * rt_ref[...], nf_ref[...])


FINAL_TILES = 4


def _final(acc, routed, mod, norm_final, part_tile0, local0, n_tiles, out_tile0, out_tiles, prev=None):
    g = FINAL_TILES
    assert all(v % g == 0 for v in (part_tile0, local0, n_tiles, out_tile0, NT_P, TILES_PER_LAT))
    rows = g * TM
    local = pl.BlockSpec((rows, D_MODEL), lambda i: (i + local0 // g, 0))
    prev = () if prev is None else (prev,)
    return pl.pallas_call(
        _final_body,
        out_shape=jax.ShapeDtypeStruct((out_tiles * TM, D_MODEL), F32),
        grid=(n_tiles // g,),
        in_specs=[
            local, local,
            pl.BlockSpec((None, 6, D_MODEL), lambda i: (_mod_row(i * g + local0 + part_tile0), 0, 0)),
            _const_spec((1, D_MODEL)),
        ] + [pl.BlockSpec(memory_space=pl.ANY)] * len(prev),
        out_specs=pl.BlockSpec((rows, D_MODEL), lambda i: (i + out_tile0 // g, 0)),
        input_output_aliases={4 + k: 0 for k in range(len(prev))},
        compiler_params=_params("parallel"),
        name="final_norm",
    )(acc, routed, mod, norm_final.reshape(1, -1), *prev)


def kernel(x_prompt, x_sample, state_hgrn, c, c_ctx, w_ada, b_ada, norm_mix, norm_ffn, w_out, w_in_ab, w_sg, b_sg, norm_sg, w_dw, b_dw, norm_cv, w_in_hgrn, lb_raw, norm_o, w_router, b_router, w_gate, w_up, w_down, w_sh_gate, w_sh_up, w_sh_down, norm_final):
    cvecs = jnp.concatenate(
        [c_ctx.reshape(1, D_MODEL), c, jnp.zeros((N_MOD_ROWS - 1 - DEC_BATCH, D_MODEL), F32)], axis=0)
    mods = _ada_tables(cvecs, w_ada, b_ada)
    lb_sm = jax.nn.softmax(lb_raw.astype(F32), axis=0)
    lb1 = (jnp.cumsum(lb_sm, axis=0) - lb_sm[0])[1].reshape(2, C_HEADS, 1, C_DK)

    def moe(l, xin):
        parts = []
        for p in range(MOE_PARTS):
            h, eidx_t, w8_t, rank_t, counts, acc = _moe_pre(
                xin, mods[l], norm_ffn[l], w_router[l], b_router[l], w_sh_gate[l], w_sh_up[l], w_sh_down[l],
                p * NT_PART)
            wsplat = jnp.repeat(w8_t.T, SC_LANES, axis=1)
            parts += [acc, _moe_routed(h, eidx_t, rank_t, counts, wsplat, w_gate, w_up, w_down, l)]
        return parts

    x = _l0_mixer(x_prompt.reshape(T_P, D_MODEL), x_sample.reshape(T_S, D_MODEL), mods[0], norm_mix[0], w_in_ab[0], w_sg[0], b_sg[0], norm_sg[0], w_dw[0],
                  b_dw[0], norm_cv[0], w_out[0])
    parts = moe(0, x)
    xz = None
    for p in range(MOE_PARTS):
        xz = _hgrn_in(parts[2 * p], parts[2 * p + 1], p * NT_PART, xz, mods[0], mods[1], norm_mix[1], w_in_hgrn[0])
    x, zf, zb = xz
    s0t = jnp.swapaxes(state_hgrn[:, 0].astype(F32), -1, -2)
    o_fw, ns_fw = _gla(zf, zb, lb1[0], s0t[:, 0], rev=False)
    o_bw, ns_bw = _gla(zf, zb, lb1[1], s0t[:, 1], rev=True)
    x = _hgrn_out(o_fw, o_bw, zb, x, mods[1], norm_o[0], w_out[1])
    parts = moe(1, x)
    y_p = y_s = None
    for p in range(MOE_PARTS):
        lo, hi = p * NT_PART, (p + 1) * NT_PART
        if lo < NT_P:
            n = min(hi, NT_P) - lo
            y_p = _final(parts[2 * p], parts[2 * p + 1], mods[1], norm_final, lo, 0, n, lo, NT_P, y_p)
        if hi > NT_P:
            first = max(lo, NT_P)
            y_s = _final(parts[2 * p], parts[2 * p + 1], mods[1], norm_final, lo, first - lo, hi - first,
                         first - NT_P, NT_S, y_s)
    y_p = y_p.reshape(BATCH, SEQ, D_MODEL)
    y_s = y_s.reshape(DEC_BATCH, DEC_SEQ, D_MODEL)
    new_state = jnp.swapaxes(jnp.stack([ns_fw, ns_bw], axis=1), -1, -2)[:, None]
    return (y_p, y_s, new_state)
```

```python
import functools

import jax
import jax.numpy as jnp
from jax import lax
from jax.experimental import pallas as pl
from jax.experimental.pallas import tpu as pltpu
from jax.experimental.pallas import tpu_sc as plsc

F32 = jnp.float32
BF16 = jnp.bfloat16
HIGHEST = lax.Precision.HIGHEST

D_MODEL = 1024
BATCH = 32
SEQ = 256
DEPTH = 2
DEC_BATCH = 8
DEC_SEQ = 2048
GRID_W = 64
A_WIDTH = D_MODEL // 2
A_GROUPS = 4
A_GC = A_WIDTH // A_GROUPS
CHUNK_A = 128
B_WIDTH = D_MODEL - A_WIDTH
CONV_W = 31
CONV_PAD = CONV_W // 2
C_HEADS = 8
C_DK = 128
C_DV = D_MODEL // C_HEADS
C_FDIM = C_HEADS * C_DK
SCAN_CHUNK = 64
N_EXPERTS = 64
TOP_K = 8
N_GROUPS = 8
GROUP_SIZE = N_EXPERTS // N_GROUPS
TOPK_GROUPS = 4
D_EXPERT = 256
D_SHARED = 256
ROUTED_SCALE = 2.5
EPS = 1e-6

TM = 256
T_P = BATCH * SEQ
T_S = DEC_BATCH * DEC_SEQ
T = T_P + T_S
NT_P = T_P // TM
NT_S = T_S // TM
NT = NT_P + NT_S
TILES_PER_LAT = DEC_SEQ // TM
TE = 512
MOE_PARTS = 2
NT_PART = NT // MOE_PARTS
T_PART = NT_PART * TM
NB = T_PART * TOP_K // TE + N_EXPERTS
P_ROWS = NB * TE
D_PACK = D_MODEL // 2
N_MOD_ROWS = 16
CONV_HALO = 16
VMEM_LIMIT = 48 * 1024 * 1024
SC_CORES = 2
SC_SUBCORES = 16
SC_WORKERS = SC_CORES * SC_SUBCORES
SC_LANES = 16
SC_CHUNK = 64
COMBINE_TOKENS = SC_CHUNK // TOP_K


def _mod_row(i):
    return jnp.where(i < NT_P, 0, 1 + (i - NT_P) // TILES_PER_LAT)


def _silu(x):
    return x * jax.nn.sigmoid(x)


def _gelu(x):
    return x * (0.5 * (1.0 + jnp.tanh(0.7978845608028654 * (x + 0.044715 * (x * x * x)))))


def _rms(x, g):
    return x * lax.rsqrt(jnp.mean(x * x, axis=-1, keepdims=True) + EPS) * g


def _layernorm(x, g):
    xc = x - jnp.mean(x, axis=-1, keepdims=True)
    return xc * lax.rsqrt(jnp.mean(xc * xc, axis=-1, keepdims=True) + EPS) * g


def _modulate(x, g, shift, scale):
    return _rms(x, g) * (1.0 + scale) + shift


def _bdot(a, b):
    return jnp.dot(a, b, preferred_element_type=F32)


def _dot_nt(a, b, precision=None):
    return lax.dot_general(a, b, (((1,), (1,)), ((), ())), precision=precision,
                           preferred_element_type=F32)


def _dot_tn(a, b):
    return lax.dot_general(a, b, (((0,), (0,)), ((), ())), preferred_element_type=F32)


def _pack_pairs(x):
    m = x.shape[1] // 2
    lo = lax.bitcast_convert_type(x[:, :m].astype(BF16).astype(F32), jnp.uint32)
    hi = lax.bitcast_convert_type(x[:, m:].astype(BF16).astype(F32), jnp.uint32)
    return lax.bitcast_convert_type(hi | (lo >> 16), jnp.int32)


def _unpack_pairs(w):
    u = lax.bitcast_convert_type(w, jnp.uint32)
    lo = lax.bitcast_convert_type(u << 16, F32)
    hi = lax.bitcast_convert_type(u & jnp.uint32(0xFFFF0000), F32)
    return lo, hi


def _params(*sem):
    return pltpu.CompilerParams(dimension_semantics=sem, vmem_limit_bytes=VMEM_LIMIT)


def _const_spec(shape):
    nd = len(shape)
    return pl.BlockSpec(shape, lambda *_: (0,) * nd)


def _ada_body(c_ref, w_ref, b_ref, o_ref):
    s = _silu(c_ref[...])
    o_ref[...] = jnp.dot(s, w_ref[...], precision=HIGHEST, preferred_element_type=F32) + b_ref[...]


def _ada_tables(cvecs, w_ada, b_ada):
    out = pl.pallas_call(
        _ada_body,
        out_shape=jax.ShapeDtypeStruct((DEPTH, N_MOD_ROWS, 6 * D_MODEL), F32),
        grid=(DEPTH, 6),
        in_specs=[
            _const_spec((N_MOD_ROWS, D_MODEL)),
            pl.BlockSpec((None, D_MODEL, D_MODEL), lambda l, j: (l, 0, j)),
            pl.BlockSpec((None, 1, D_MODEL), lambda l, j: (l, 0, j)),
        ],
        out_specs=pl.BlockSpec((None, N_MOD_ROWS, D_MODEL), lambda l, j: (l, 0, j)),
        compiler_params=_params("parallel", "parallel"),
        name="ada_tables",
    )(cvecs, w_ada, b_ada.reshape(DEPTH, 1, 6 * D_MODEL))
    return out.reshape(DEPTH, N_MOD_ROWS, 6, D_MODEL)


def _mod_spec():
    return pl.BlockSpec((None, 6, D_MODEL), lambda i: (_mod_row(i), 0, 0))


def _tile_spec():
    return pl.BlockSpec((TM, D_MODEL), lambda i: (i, 0))


SUBLANES = 8


def _conv_segment(pad_ref, shift_ref, conv_ref, wdw_ref, pad_base, out_base, seg):
    rb = min(seg, 64)
    for cb in range(B_WIDTH // 128):
        cs = slice(cb * 128, (cb + 1) * 128)
        for r0 in range(0, seg, rb):
            acc = jnp.zeros((rb, 128), F32)
            for k in range(CONV_W):
                b = (CONV_HALO - CONV_PAD + k) % SUBLANES
                off = pad_base + r0 + CONV_HALO - CONV_PAD + k - b
                src = pad_ref if b == 0 else shift_ref.at[b - 1]
                acc = acc + wdw_ref[k:k + 1, cs] * src[off:off + rb, cs]
            conv_ref[out_base + r0:out_base + r0 + rb, cs] = acc


def _l0_body(xc_ref, xl_ref, mod_ref, nmix_ref, win_ref, wsg_ref, bsg_ref, nsg_ref, wdw_ref, bdw_ref,
             ncv_ref, wout_ref, o_ref, cat_ref, pad_ref, shift_ref, conv_ref):
    i = pl.program_id(0)
    x = jnp.where(i < NT_P, xc_ref[...], xl_ref[...])
    h = _modulate(x, nmix_ref[...], mod_ref[0:1, :], mod_ref[1:2, :])
    z = _bdot(h.astype(BF16), win_ref[...])
    u = _gelu(z[:, :A_WIDTH])
    vb = _layernorm(_gelu(z[:, A_WIDTH:2 * A_WIDTH]), nsg_ref[...]).astype(BF16)
    for n in range(TM // CHUNK_A):
        rs = slice(n * CHUNK_A, (n + 1) * CHUNK_A)
        for g in range(A_GROUPS):
            cs = slice(g * A_GC, (g + 1) * A_GC)
            m = _bdot(wsg_ref[g], vb[rs, cs]) + bsg_ref[g]
            cat_ref[rs, cs] = (u[rs, cs] * m).astype(BF16)
    hb = z[:, 2 * A_WIDTH:2 * A_WIDTH + B_WIDTH] * jax.nn.sigmoid(z[:, 2 * A_WIDTH + B_WIDTH:])

    def conv_tile(seg):
        stride = seg + 2 * CONV_HALO
        halo = jnp.zeros((CONV_HALO, B_WIDTH), F32)
        for s in range(TM // seg):
            b = s * stride
            pad_ref[b:b + CONV_HALO, :] = halo
            pad_ref[b + CONV_HALO:b + CONV_HALO + seg, :] = hb[s * seg:(s + 1) * seg, :]
            pad_ref[b + CONV_HALO + seg:b + stride, :] = halo
        rows = (TM // seg) * stride - SUBLANES
        for b in range(1, SUBLANES):
            shift_ref[b - 1, 0:rows, :] = pad_ref[b:b + rows, :]
        for s in range(TM // seg):
            _conv_segment(pad_ref, shift_ref, conv_ref, wdw_ref, s * stride, s * seg, seg)

    @pl.when(i < NT_P)
    def _():
        conv_tile(SEQ)

    @pl.when(i >= NT_P)
    def _():
        conv_tile(GRID_W)

    yb = _layernorm(conv_ref[...] + bdw_ref[...], ncv_ref[...])
    cat_ref[:, A_WIDTH:] = _silu(yb).astype(BF16)
    out = _bdot(cat_ref[...], wout_ref[...])
    o_ref[...] = x + mod_ref[2:3, :] * out


def _l0_mixer(x_ctx, x_lat, mod, norm_mix, w_in, w_sg, b_sg, norm_sg, w_dw, b_dw, norm_cv, w_out):
    assert TM == SEQ and TM % GRID_W == 0 and TM % CHUNK_A == 0
    pad_rows = (TM // GRID_W) * (GRID_W + 2 * CONV_HALO)
    assert pad_rows >= SEQ + 2 * CONV_HALO
    return pl.pallas_call(
        _l0_body,
        out_shape=jax.ShapeDtypeStruct((T, D_MODEL), F32),
        grid=(NT,),
        in_specs=[
            pl.BlockSpec((TM, D_MODEL), lambda i: (jnp.minimum(i, NT_P - 1), 0)),
            pl.BlockSpec((TM, D_MODEL), lambda i: (jnp.maximum(i - NT_P, 0), 0)),
            _mod_spec(),
            _const_spec((1, D_MODEL)),
            _const_spec((D_MODEL, 2 * A_WIDTH + 2 * B_WIDTH)),
            _const_spec((A_GROUPS, CHUNK_A, CHUNK_A)),
            _const_spec((A_GROUPS, CHUNK_A, 1)),
            _const_spec((1, A_WIDTH)),
            _const_spec((CONV_W, B_WIDTH)),
            _const_spec((1, B_WIDTH)),
            _const_spec((1, B_WIDTH)),
            _const_spec((D_MODEL, D_MODEL)),
        ],
        out_specs=_tile_spec(),
        scratch_shapes=[
            pltpu.VMEM((TM, D_MODEL), BF16),
            pltpu.VMEM((pad_rows, B_WIDTH), F32),
            pltpu.VMEM((SUBLANES - 1, pad_rows, B_WIDTH), F32),
            pltpu.VMEM((TM, B_WIDTH), F32),
        ],
        compiler_params=_params("parallel"),
        name="l0_mixer",
    )(x_ctx, x_lat, mod, norm_mix.reshape(1, -1), w_in.astype(BF16), w_sg.astype(BF16),
      b_sg.reshape(A_GROUPS, CHUNK_A, 1), norm_sg.reshape(1, -1), w_dw, b_dw.reshape(1, -1),
      norm_cv.reshape(1, -1), w_out.astype(BF16))


def _route(scores, biased):
    n = scores.shape[-1]
    shp = (N_GROUPS, GROUP_SIZE, n)
    s3 = scores.reshape(shp)
    b3 = biased.reshape(shp)
    m_iota = lax.broadcasted_iota(jnp.int32, shp, 1).astype(F32)
    g_iota = lax.broadcasted_iota(jnp.int32, shp, 0).astype(F32)
    e_iota = g_iota * GROUP_SIZE + m_iota
    neg = -jnp.inf

    def amax1(v):
        return jnp.max(v, axis=1, keepdims=True)

    def amin1(v):
        return jnp.min(v, axis=1, keepdims=True)

    m1 = amax1(b3)
    i1 = amin1(jnp.where(b3 == m1, m_iota, float(GROUP_SIZE)))
    m2 = amax1(jnp.where(m_iota == i1, neg, b3))
    grp = m1 + m2
    gi1 = lax.broadcasted_iota(jnp.int32, grp.shape, 0).astype(F32)
    gmask = jnp.zeros(grp.shape, jnp.bool_)
    for _ in range(TOPK_GROUPS):
        gm = jnp.max(grp, axis=0, keepdims=True)
        gi = jnp.min(jnp.where(grp == gm, gi1, float(N_GROUPS)), axis=0, keepdims=True)
        hit = gi1 == gi
        gmask = jnp.logical_or(gmask, hit)
        grp = jnp.where(hit, neg, grp)
    cand = jnp.where(gmask, b3, neg)
    ids, vals, hits = [], [], []
    for _ in range(TOP_K):
        mx = jnp.max(amax1(cand), axis=0, keepdims=True)
        ei = jnp.min(amin1(jnp.where(cand == mx, e_iota, float(N_EXPERTS))), axis=0, keepdims=True)
        hit = e_iota == ei
        ids.append(ei.reshape(1, n))
        vals.append(_pick(hit, s3))
        hits.append(hit)
        cand = jnp.where(hit, neg, cand)
    return jnp.concatenate(ids, axis=0).astype(jnp.int32), jnp.concatenate(vals, axis=0), hits


def _pick(hit, v3):
    s = jnp.sum(jnp.sum(jnp.where(hit, v3, 0.0), axis=1, keepdims=True), axis=0, keepdims=True)
    return s.reshape(1, v3.shape[-1])


def _moe_pre_body(x_ref, mod_ref, nffn_ref, wrt_ref, br_ref, wsg_ref, wsu_ref, wsd_ref,
                  h_ref, eidx_ref, w8_ref, rank_ref, cnt_ref, acc_ref, run_ref):
    @pl.when(pl.program_id(0) == 0)
    def _():
        run_ref[...] = jnp.zeros(run_ref.shape, F32)

    x = x_ref[...]
    h = _modulate(x, nffn_ref[...], mod_ref[3:4, :], mod_ref[4:5, :])
    h_ref[...] = _pack_pairs(h)
    hb = h.astype(BF16)
    h_lo = (h - hb.astype(F32)).astype(BF16)
    wr = wrt_ref[...]
    wr_hi = wr.astype(BF16)
    wr_lo = (wr - wr_hi.astype(F32)).astype(BF16)
    logits_t = _dot_nt(wr_hi, hb) + (_dot_nt(wr_hi, h_lo) + _dot_nt(wr_lo, hb))
    scores = jax.nn.sigmoid(logits_t)
    eidx, sv, hits = _route(scores, scores + br_ref[...])
    eidx_ref[...] = eidx
    w8_ref[...] = sv / jnp.sum(sv, axis=0, keepdims=True) * ROUTED_SCALE
    sel3 = hits[0]
    for hit in hits[1:]:
        sel3 = jnp.logical_or(sel3, hit)
    sel = sel3.astype(F32).reshape(N_EXPERTS, TM)
    earlier = (lax.broadcasted_iota(jnp.int32, (TM, TM), 0)
               < lax.broadcasted_iota(jnp.int32, (TM, TM), 1)).astype(BF16)
    rank3 = (_bdot(sel.astype(BF16), earlier) + run_ref[...]).reshape(N_GROUPS, GROUP_SIZE, TM)
    rank_ref[...] = jnp.concatenate([_pick(hit, rank3) for hit in hits], axis=0).astype(jnp.int32)
    run_ref[...] = run_ref[...] + jnp.sum(sel, axis=1, keepdims=True)
    cnt_ref[...] = run_ref[...].astype(jnp.int32)
    sh = _bdot((_silu(_bdot(hb, wsg_ref[...])) * _bdot(hb, wsu_ref[...])).astype(BF16), wsd_ref[...])
    acc_ref[...] = x + mod_ref[5:6, :] * sh


def _moe_pre(x, mod, norm_ffn, w_router, b_router, w_sh_gate, w_sh_up, w_sh_down, tile0):
    return pl.pallas_call(
        _moe_pre_body,
        out_shape=(
            jax.ShapeDtypeStruct((T_PART, D_PACK), jnp.int32),
            jax.ShapeDtypeStruct((TOP_K, T_PART), jnp.int32),
            jax.ShapeDtypeStruct((TOP_K, T_PART), F32),
            jax.ShapeDtypeStruct((TOP_K, T_PART), jnp.int32),
            jax.ShapeDtypeStruct((N_EXPERTS, 1), jnp.int32),
            jax.ShapeDtypeStruct((T_PART, D_MODEL), F32),
        ),
        grid=(NT_PART,),
        in_specs=[
            pl.BlockSpec((TM, D_MODEL), lambda i: (i + tile0, 0)),
            pl.BlockSpec((None, 6, D_MODEL), lambda i: (_mod_row(i + tile0), 0, 0)),
            _const_spec((1, D_MODEL)),
            _const_spec((N_EXPERTS, D_MODEL)),
            _const_spec((N_EXPERTS, 1)),
            _const_spec((D_MODEL, D_SHARED)),
            _const_spec((D_MODEL, D_SHARED)),
            _const_spec((D_SHARED, D_MODEL)),
        ],
        out_specs=(
            pl.BlockSpec((TM, D_PACK), lambda i: (i, 0)),
            pl.BlockSpec((TOP_K, TM), lambda i: (0, i)),
            pl.BlockSpec((TOP_K, TM), lambda i: (0, i)),
            pl.BlockSpec((TOP_K, TM), lambda i: (0, i)),
            _const_spec((N_EXPERTS, 1)),
            _tile_spec(),
        ),
        scratch_shapes=[pltpu.VMEM((N_EXPERTS, 1), F32)],
        compiler_params=_params("arbitrary"),
        name="moe_router_shared",
    )(x, mod, norm_ffn.reshape(1, -1), w_router.T, b_router.reshape(N_EXPERTS, 1),
      w_sh_gate.astype(BF16), w_sh_up.astype(BF16), w_sh_down.astype(BF16))


EXPERT_RING = 4


WEIGHT_SLOTS = 2


EXPERT_TILES_PER_STEP = 8


def _expert_body(be_ref, nv_ref, nu_ref, run_ref, rexp_ref, nrun_ref, xs_hbm, wg_hbm, wu_hbm, wd_hbm, ys_ref,
                 wgu_s, wd_s, ring, sems, wg_buf, wu_buf, wd_buf, wsems, *, layer):
    n_used = nu_ref[0]
    n_runs = nrun_ref[0]

    def fetch(t):
        slot = t % EXPERT_RING
        return pltpu.make_async_copy(xs_hbm.at[pl.ds(t * TE, TE)], ring.at[slot], sems.at[slot])

    def wfetch(r):
        slot = r % WEIGHT_SLOTS
        e = rexp_ref[r]
        return [pltpu.make_async_copy(wg_hbm.at[layer, e], wg_buf.at[slot], wsems.at[slot, 0]),
                pltpu.make_async_copy(wu_hbm.at[layer, e], wu_buf.at[slot], wsems.at[slot, 1]),
                pltpu.make_async_copy(wd_hbm.at[layer, e], wd_buf.at[slot], wsems.at[slot, 2])]

    def tile_step(j, out_rows):
        live_tile = j < n_used

        @pl.when(j == 0)
        def _():
            for t in range(EXPERT_RING - 1):
                @pl.when(t < n_used)
                def _():
                    fetch(t).start()
            for r in range(WEIGHT_SLOTS):
                @pl.when(r < n_runs)
                def _():
                    for cp in wfetch(r):
                        cp.start()

        @pl.when(j + (EXPERT_RING - 1) < n_used)
        def _():
            fetch(j + (EXPERT_RING - 1)).start()

        @pl.when(jnp.logical_and(live_tile, jnp.logical_or(j == 0, be_ref[j] != be_ref[jnp.maximum(j - 1, 0)])))
        def _():
            r = run_ref[j]
            slot = r % WEIGHT_SLOTS
            for cp in wfetch(r):
                cp.wait()
            wgu_s[:, :D_EXPERT] = wg_buf[slot].astype(BF16)
            wgu_s[:, D_EXPERT:] = wu_buf[slot].astype(BF16)
            wd_s[...] = wd_buf[slot].astype(BF16)

            @pl.when(r + WEIGHT_SLOTS < n_runs)
            def _():
                for cp in wfetch(r + WEIGHT_SLOTS):
                    cp.start()

        @pl.when(live_tile)
        def _():
            fetch(j).wait()
            live = lax.broadcasted_iota(jnp.int32, (TE, 1), 0) < nv_ref[j]
            lo, hi = _unpack_pairs(jnp.where(live, ring[j % EXPERT_RING], 0))
            xb = jnp.concatenate([lo.astype(BF16), hi.astype(BF16)], axis=1)
            hgu = _bdot(xb, wgu_s[...])
            hh = _silu(hgu[:, :D_EXPERT]) * hgu[:, D_EXPERT:]
            ys_ref[out_rows, :] = _pack_pairs(_bdot(hh.astype(BF16), wd_s[...]))

    for u in range(EXPERT_TILES_PER_STEP):
        tile_step(pl.program_id(0) * EXPERT_TILES_PER_STEP + u, pl.ds(u * TE, TE))


def _experts(block_expert, block_rows, n_used, block_run, run_expert, n_runs, xs, w_gate, w_up, w_down, layer):
    g = EXPERT_TILES_PER_STEP
    assert NB % g == 0

    def row_map(s, be, nv, nu, run, rexp, nrun):
        return (jnp.minimum(s, (nu[0] - 1) // g), 0)

    any_spec = pl.BlockSpec(memory_space=pl.ANY)
    return pl.pallas_call(
        functools.partial(_expert_body, layer=layer),
        out_shape=jax.ShapeDtypeStruct((P_ROWS, D_PACK), jnp.int32),
        grid_spec=pltpu.PrefetchScalarGridSpec(
            num_scalar_prefetch=6,
            grid=(NB // g,),
            in_specs=[any_spec, any_spec, any_spec, any_spec],
            out_specs=pl.BlockSpec((g * TE, D_PACK), row_map),
            scratch_shapes=[
                pltpu.VMEM((D_MODEL, 2 * D_EXPERT), BF16),
                pltpu.VMEM((D_EXPERT, D_MODEL), BF16),
                pltpu.VMEM((EXPERT_RING, TE, D_PACK), jnp.int32),
                pltpu.SemaphoreType.DMA((EXPERT_RING,)),
                pltpu.VMEM((WEIGHT_SLOTS, D_MODEL, D_EXPERT), F32),
                pltpu.VMEM((WEIGHT_SLOTS, D_MODEL, D_EXPERT), F32),
                pltpu.VMEM((WEIGHT_SLOTS, D_EXPERT, D_MODEL), F32),
                pltpu.SemaphoreType.DMA((WEIGHT_SLOTS, 3)),
            ],
        ),
        compiler_params=_params("arbitrary"),
        name="moe_experts",
    )(block_expert, block_rows, n_used, block_run, run_expert, n_runs, xs, w_gate, w_up, w_down)


def _positions_body(start_ref, eidx_ref, rank_ref, pos_ref):
    eidx = eidx_ref[...]
    base = jnp.zeros(eidx.shape, jnp.int32)
    for e in range(N_EXPERTS):
        base = jnp.where(eidx == e, start_ref[e], base)
    pos_ref[...] = base * TE + rank_ref[...]


def _positions(blk_start, eidx_t, rank_t):
    full = pl.BlockSpec((TOP_K, T_PART), lambda i, s: (0, 0))
    return pl.pallas_call(
        _positions_body,
        out_shape=jax.ShapeDtypeStruct((TOP_K, T_PART), jnp.int32),
        grid_spec=pltpu.PrefetchScalarGridSpec(
            num_scalar_prefetch=1, grid=(1,), in_specs=[full, full], out_specs=full),
        compiler_params=_params("arbitrary"),
        name="moe_positions",
    )(blk_start, eidx_t, rank_t)


def _sc_mesh():
    return plsc.VectorSubcoreMesh(core_axis_name="c", subcore_axis_name="s")


def _sc_worker():
    return lax.axis_index("s") * SC_CORES + lax.axis_index("c")


def _sc_scatter_rows(h, pos_rows):
    c = SC_CHUNK
    n_chunks = T_PART // SC_WORKERS // c
    width = h.shape[1]

    @functools.partial(
        pl.kernel, mesh=_sc_mesh(),
        out_type=jax.ShapeDtypeStruct((P_ROWS, width), h.dtype),
        scratch_types=[pltpu.VMEM((n_chunks * TOP_K, c), jnp.int32),
                       pltpu.VMEM((c, width), h.dtype), pltpu.VMEM((c, width), h.dtype)]
        + [pltpu.SemaphoreType.DMA] * 4,
        name="moe_dispatch_scatter",
    )
    def scatter(h_hbm, pos_hbm, xs_hbm, idx_v, buf0, buf1, sem_in0, sem_in1, sem_out0, sem_out1):
        assert n_chunks % 2 == 0
        bufs, sem_in, sem_out = (buf0, buf1), (sem_in0, sem_in1), (sem_out0, sem_out1)
        first = _sc_worker() * n_chunks
        pltpu.sync_copy(pos_hbm.at[pl.ds(first * TOP_K, n_chunks * TOP_K)], idx_v)

        def load(i, b):
            return pltpu.make_async_copy(h_hbm.at[pl.ds((first + i) * c, c)], bufs[b], sem_in[b])

        def puts(i, b):
            return [pltpu.make_async_copy(bufs[b], xs_hbm.at[idx_v.at[i * TOP_K + k]], sem_out[b])
                    for k in range(TOP_K)]

        load(0, 0).start()
        load(1, 1).start()

        @pl.loop(0, n_chunks, step=2)
        def _(i):
            for b in range(2):
                load(i + b, b).wait()
                for cp in puts(i + b, b):
                    cp.start()
            for b in range(2):
                for cp in puts(i + b, b):
                    cp.wait()

                @pl.when(i + 2 + b < n_chunks)
                def _():
                    load(i + 2 + b, b).start()

    return scatter(h, pos_rows)


def _sc_combine(ys, pos_tk, wsplat):
    ct = COMBINE_TOKENS
    rows = ct * TOP_K
    tok_per_worker = T_PART // SC_WORKERS
    n_chunks = tok_per_worker // ct
    width = ys.shape[1]
    vmem = pltpu.VMEM

    @functools.partial(
        pl.kernel, mesh=_sc_mesh(),
        out_type=jax.ShapeDtypeStruct((T_PART, D_MODEL), F32),
        scratch_types=[vmem((tok_per_worker * TOP_K,), jnp.int32),
                       vmem((rows, width), ys.dtype), vmem((rows, width), ys.dtype),
                       vmem((ct, TOP_K * SC_LANES), F32), vmem((ct, TOP_K * SC_LANES), F32),
                       vmem((ct, D_MODEL), F32), vmem((ct, D_MODEL), F32)]
        + [pltpu.SemaphoreType.DMA] * 6,
        compiler_params=pltpu.CompilerParams(needs_layout_passes=False),
        name="moe_combine",
    )
    def combine(ys_hbm, pos_hbm, w_hbm, out_hbm, idx_v, g0, g1, w0, w1, o0, o1,
                sem_g0, sem_g1, sem_w0, sem_w1, sem_o0, sem_o1):
        assert n_chunks % 2 == 0
        gbuf, wbuf, obuf = (g0, g1), (w0, w1), (o0, o1)
        sem_g, sem_w, sem_o = (sem_g0, sem_g1), (sem_w0, sem_w1), (sem_o0, sem_o1)
        tok0 = _sc_worker() * tok_per_worker
        pltpu.sync_copy(pos_hbm.at[pl.ds(tok0 * TOP_K, tok_per_worker * TOP_K)], idx_v)

        def fetch(i, b):
            return [pltpu.make_async_copy(ys_hbm.at[idx_v.at[pl.ds(i * rows, rows)]], gbuf[b], sem_g[b]),
                    pltpu.make_async_copy(w_hbm.at[pl.ds(tok0 + i * ct, ct)], wbuf[b], sem_w[b])]

        def flush(i, b):
            return pltpu.make_async_copy(obuf[b], out_hbm.at[pl.ds(tok0 + i * ct, ct)], sem_o[b])

        def reduce_chunk(b):
            @pl.loop(0, ct)
            def _(t):
                wv = [wbuf[b][t, pl.ds(k * SC_LANES, SC_LANES)] for k in range(TOP_K)]

                @pl.loop(0, width // SC_LANES)
                def _(j):
                    col = j * SC_LANES
                    acc_lo = acc_hi = None
                    for k in range(TOP_K):
                        word = gbuf[b][t * TOP_K + k, pl.ds(col, SC_LANES)]
                        lo = lax.bitcast_convert_type(word << 16, F32)
                        hi = lax.bitcast_convert_type(word & jnp.int32(-65536), F32)
                        acc_lo = wv[k] * lo if k == 0 else acc_lo + wv[k] * lo
                        acc_hi = wv[k] * hi if k == 0 else acc_hi + wv[k] * hi
                    obuf[b][t, pl.ds(col, SC_LANES)] = acc_lo
                    obuf[b][t, pl.ds(D_PACK + col, SC_LANES)] = acc_hi

        for b in range(2):
            for cp in fetch(b, b):
                cp.start()

        @pl.loop(0, n_chunks, step=2)
        def _(i):
            for b in range(2):
                for cp in fetch(i + b, b):
                    cp.wait()

                @pl.when(i > 0)
                def _():
                    flush(i + b - 2, b).wait()

                reduce_chunk(b)
                flush(i + b, b).start()

                @pl.when(i + 2 + b < n_chunks)
                def _():
                    for cp in fetch(i + 2 + b, b):
                        cp.start()

        for b in range(2):
            flush(n_chunks - 2 + b, b).wait()

    return combine(ys, pos_tk, wsplat)


def _moe_routed(h, eidx_t, rank_t, counts, wsplat, w_gate, w_up, w_down, layer):
    counts = counts.reshape(1, N_EXPERTS)
    nblk = (counts + TE - 1) // TE
    blk_end = jnp.cumsum(nblk, axis=1)
    blk_start = blk_end - nblk
    blocks = jnp.arange(NB, dtype=jnp.int32).reshape(NB, 1)
    block_expert = jnp.minimum(jnp.sum(blocks >= blk_end, axis=1, keepdims=True), N_EXPERTS - 1)
    mine = block_expert == jnp.arange(N_EXPERTS, dtype=jnp.int32).reshape(1, N_EXPERTS)
    cnt_b = jnp.sum(jnp.where(mine, counts, 0), axis=1, keepdims=True)
    start_b = jnp.sum(jnp.where(mine, blk_start, 0), axis=1, keepdims=True)
    block_rows = jnp.clip(cnt_b - (blocks - start_b) * TE, 0, TE)
    n_used = blk_end[0, -1].reshape(1).astype(jnp.int32)
    present = (nblk > 0).astype(jnp.int32)
    run_of_expert = jnp.cumsum(present, axis=1) - 1
    block_run = jnp.sum(jnp.where(mine, run_of_expert, 0), axis=1)
    experts = jnp.arange(N_EXPERTS, dtype=jnp.int32).reshape(1, N_EXPERTS)
    run_hit = jnp.logical_and(run_of_expert == experts.reshape(N_EXPERTS, 1), present > 0)
    run_expert = jnp.sum(jnp.where(run_hit, experts, 0), axis=1)
    n_runs = jnp.sum(present).reshape(1)
    pos = _positions(blk_start.reshape(N_EXPERTS).astype(jnp.int32), eidx_t, rank_t)
    pos_rows = pos.reshape(TOP_K, T_PART // SC_CHUNK, SC_CHUNK).transpose(1, 0, 2).reshape(-1, SC_CHUNK)
    xs = _sc_scatter_rows(h, pos_rows)
    ys = _experts(block_expert.reshape(NB).astype(jnp.int32), block_rows.reshape(NB).astype(jnp.int32),
                  n_used, block_run.astype(jnp.int32), run_expert.astype(jnp.int32), n_runs.astype(jnp.int32),
                  xs, w_gate, w_up, w_down, layer)
    return _sc_combine(ys, pos.T.reshape(-1), wsplat)


N_SLABS = (3 * C_FDIM + 2 * D_MODEL) // 128


F32_GROUPS = (1, 2)
B16_GROUPS = (0, 3, 4)


def _hgrn_in_body(acc_ref, rt_ref, mod0_ref, mod1_ref, nmix_ref, win_ref, *rest):
    x_ref, zf_ref, zb_ref = rest[-3:]
    x = acc_ref[...] + mod0_ref[5:6, :] * rt_ref[...]
    x_ref[...] = x
    hb = _modulate(x, nmix_ref[...], mod1_ref[0:1, :], mod1_ref[1:2, :]).astype(BF16)
    for s in range(N_SLABS // C_HEADS):
        zz = _bdot(hb, win_ref[:, s * D_MODEL:(s + 1) * D_MODEL])
        for hh in range(C_HEADS):
            if s in F32_GROUPS:
                zf_ref[F32_GROUPS.index(s) * C_HEADS + hh] = zz[:, hh * 128:(hh + 1) * 128]
            else:
                zb_ref[B16_GROUPS.index(s) * C_HEADS + hh] = zz[:, hh * 128:(hh + 1) * 128].astype(BF16)


HGRN_IN_TILES = 2


def _hgrn_in(acc, routed, tile0, prev, mod0, mod1, norm_mix, w_in):
    g = HGRN_IN_TILES
    assert NT_PART % g == 0 and tile0 % g == 0 and NT_P % g == 0 and TILES_PER_LAT % g == 0
    rows = g * TM
    block0 = tile0 // g

    def shifted_mod():
        return pl.BlockSpec((None, 6, D_MODEL), lambda i: (_mod_row(i * g + tile0), 0, 0))

    local = pl.BlockSpec((rows, D_MODEL), lambda i: (i, 0))
    any_spec = pl.BlockSpec(memory_space=pl.ANY)
    prev = () if prev is None else tuple(prev)
    n_in = 6
    return pl.pallas_call(
        _hgrn_in_body,
        out_shape=(
            jax.ShapeDtypeStruct((T, D_MODEL), F32),
            jax.ShapeDtypeStruct((len(F32_GROUPS) * C_HEADS, T, 128), F32),
            jax.ShapeDtypeStruct((len(B16_GROUPS) * C_HEADS, T, 128), BF16),
        ),
        grid=(NT_PART // g,),
        in_specs=[
            local, local, shifted_mod(), shifted_mod(),
            _const_spec((1, D_MODEL)),
            pl.BlockSpec((D_MODEL, 3 * C_FDIM + 2 * D_MODEL), lambda i: (0, 0), pipeline_mode=pl.Buffered(1)),
        ] + [any_spec] * len(prev),
        out_specs=(
            pl.BlockSpec((rows, D_MODEL), lambda i: (i + block0, 0)),
            pl.BlockSpec((len(F32_GROUPS) * C_HEADS, rows, 128), lambda i: (0, i + block0, 0)),
            pl.BlockSpec((len(B16_GROUPS) * C_HEADS, rows, 128), lambda i: (0, i + block0, 0)),
        ),
        input_output_aliases={n_in + k: k for k in range(len(prev))},
        compiler_params=_params("parallel"),
        name="hgrn_in_proj",
    )(acc, routed, mod0, mod1, norm_mix.reshape(1, -1), w_in.astype(BF16), *prev)


def _gla_body(q_ref, f_ref, v_ref, lb_ref, s0_ref, o_ref, ns_ref, st_ref, *, rev):
    j = pl.program_id(0)
    ti = NT - 1 - j if rev else j
    is_ctx = ti < NT_P
    first_lat = (ti - NT_P) % TILES_PER_LAT == (TILES_PER_LAT - 1 if rev else 0)

    @pl.when(is_ctx)
    def _():
        st_ref[...] = jnp.zeros(st_ref.shape, F32)

    @pl.when(jnp.logical_and(jnp.logical_not(is_ctx), first_lat))
    def _():
        st_ref[...] = s0_ref[...]

    row = lax.broadcasted_iota(jnp.int32, (TM, TM), 0)
    col = lax.broadcasted_iota(jnp.int32, (TM, TM), 1)
    same_chunk = (row // SCAN_CHUNK) == (col // SCAN_CHUNK)
    seen = jnp.logical_and(same_chunk, (col >= row) if rev else (col <= row))
    cum_w = seen.astype(BF16)
    mid = SCAN_CHUNK // 2 if rev else SCAN_CHUNK // 2 - 1
    last = 0 if rev else SCAN_CHUNK - 1
    n_chunks = TM // SCAN_CHUNK
    order = range(n_chunks - 1, -1, -1) if rev else range(n_chunks)
    group = 4

    def chunk_rows(b, off):
        return jnp.concatenate(
            [jnp.broadcast_to(b[c * SCAN_CHUNK + off:c * SCAN_CHUNK + off + 1, :], (SCAN_CHUNK, b.shape[1]))
             for c in range(n_chunks)], axis=0)

    def head_group(gi, carry):
        heads = [gi * group + u for u in range(group)]
        qs, kk, vv, bcum = [], [], [], []
        for hd in heads:
            lb = lb_ref[hd]
            qs.append(_silu(q_ref[hd].astype(F32)) * (C_DK ** -0.5))
            fg = lb + (1.0 - lb) * jax.nn.sigmoid(f_ref[hd])
            kk.append(1.0 - fg)
            vv.append(v_ref[hd])
            g = jnp.log(fg)
            g_hi = g.astype(BF16)
            r1 = g - g_hi.astype(F32)
            g_mid = r1.astype(BF16)
            g_lo = (r1 - g_mid.astype(F32)).astype(BF16)
            bcum.append(_bdot(cum_w, g_hi) + _bdot(cum_w, g_mid) + _bdot(cum_w, g_lo))
        o_intra, q_dec, kv, decay = [], [], [], []
        for u in range(group):
            b_mid = chunk_rows(bcum[u], mid)
            b_last = chunk_rows(bcum[u], last)
            qe = (qs[u] * jnp.exp(bcum[u] - b_mid)).astype(BF16)
            ke = (kk[u] * jnp.exp(b_mid - bcum[u])).astype(BF16)
            att = jnp.where(seen, _dot_nt(qe, ke), 0.0)
            o_intra.append(_bdot(att.astype(BF16), vv[u]))
            q_dec.append((qs[u] * jnp.exp(bcum[u])).astype(BF16))
            k_dec = (kk[u] * jnp.exp(b_last - bcum[u])).astype(BF16)
            kv.append([_dot_tn(vv[u][c * SCAN_CHUNK:(c + 1) * SCAN_CHUNK], k_dec[c * SCAN_CHUNK:(c + 1) * SCAN_CHUNK])
                       for c in range(n_chunks)])
            decay.append([jnp.exp(bcum[u][c * SCAN_CHUNK + last:c * SCAN_CHUNK + last + 1, :])
                          for c in range(n_chunks)])
        st = [st_ref[hd] for hd in heads]
        for c in order:
            sl = slice(c * SCAN_CHUNK, (c + 1) * SCAN_CHUNK)
            for u, hd in enumerate(heads):
                o_ref[hd, pl.ds(c * SCAN_CHUNK, SCAN_CHUNK), :] = (
                    o_intra[u][sl] + _dot_nt(q_dec[u][sl], st[u].astype(BF16))).astype(BF16)
                st[u] = decay[u][c] * st[u] + kv[u][c]
        for u, hd in enumerate(heads):
            st_ref[hd] = st[u]
        return carry

    lax.fori_loop(0, C_HEADS // group, head_group, 0)

    @pl.when(is_ctx)
    def _():
        ns_ref[...] = st_ref[...]


def _gla(zf, zb, lb_dir, s0t_dir, *, rev):
    def ti_of(j):
        return NT - 1 - j if rev else j

    f_slab = F32_GROUPS.index(2 if rev else 1)

    def lat_map(j):
        return (jnp.clip((ti_of(j) - NT_P) // TILES_PER_LAT, 0, DEC_BATCH - 1), 0, 0, 0)

    return pl.pallas_call(
        functools.partial(_gla_body, rev=rev),
        out_shape=(
            jax.ShapeDtypeStruct((C_HEADS, T, C_DV), BF16),
            jax.ShapeDtypeStruct((BATCH, C_HEADS, C_DV, C_DK), F32),
        ),
        grid=(NT,),
        in_specs=[
            pl.BlockSpec((C_HEADS, TM, 128), lambda j: (B16_GROUPS.index(0), ti_of(j), 0)),
            pl.BlockSpec((C_HEADS, TM, 128), lambda j: (f_slab, ti_of(j), 0)),
            pl.BlockSpec((C_HEADS, TM, 128), lambda j: (B16_GROUPS.index(3), ti_of(j), 0)),
            _const_spec((C_HEADS, 1, C_DK)),
            pl.BlockSpec((None, C_HEADS, C_DV, C_DK), lat_map),
        ],
        out_specs=(
            pl.BlockSpec((C_HEADS, TM, C_DV), lambda j: (0, ti_of(j), 0)),
            pl.BlockSpec((None, C_HEADS, C_DV, C_DK),
                         lambda j: (jnp.minimum(ti_of(j), NT_P - 1), 0, 0, 0)),
        ),
        scratch_shapes=[pltpu.VMEM((C_HEADS, C_DV, C_DK), F32)],
        compiler_params=_params("arbitrary"),
        name="gla_bwd" if rev else "gla_fwd",
    )(zb, zf, zb, lb_dir, s0t_dir)


def _hgrn_out_body(ofw_ref, obw_ref, gate_ref, x_ref, mod_ref, no_ref, wout_ref, o_ref, cat_ref):
    for hd in range(C_HEADS):
        o = ofw_ref[hd].astype(F32) + obw_ref[hd].astype(F32)
        gate = gate_ref[hd].astype(F32)
        cat_ref[:, hd * C_DV:(hd + 1) * C_DV] = (_rms(o, no_ref[...]) * _silu(gate)).astype(BF16)
    o_ref[...] = x_ref[...] + mod_ref[2:3, :] * _bdot(cat_ref[...], wout_ref[...])


HGRN_OUT_TILES = 4


def _hgrn_out(o_fw, o_bw, zb, x, mod, norm_o, w_out):
    g = HGRN_OUT_TILES
    assert NT % g == 0 and NT_P % g == 0 and TILES_PER_LAT % g == 0
    rows = g * TM
    head_spec = pl.BlockSpec((C_HEADS, rows, C_DV), lambda i: (0, i, 0))
    row_spec = pl.BlockSpec((rows, D_MODEL), lambda i: (i, 0))
    return pl.pallas_call(
        _hgrn_out_body,
        out_shape=jax.ShapeDtypeStruct((T, D_MODEL), F32),
        grid=(NT // g,),
        in_specs=[
            head_spec, head_spec,
            pl.BlockSpec((C_HEADS, rows, 128), lambda i: (B16_GROUPS.index(4), i, 0)),
            row_spec,
            pl.BlockSpec((None, 6, D_MODEL), lambda i: (_mod_row(i * g), 0, 0)),
            _const_spec((1, C_DV)),
            _const_spec((D_MODEL, D_MODEL)),
        ],
        out_specs=row_spec,
        scratch_shapes=[pltpu.VMEM((rows, D_MODEL), BF16)],
        compiler_params=_params("parallel"),
        name="hgrn_out_proj",
    )(o_fw, o_bw, zb, x, mod, norm_o.reshape(1, -1), w_out.astype(BF16))


def _final_body(acc_ref, rt_ref, mod_ref, nf_ref, *rest):
    o_ref = rest[-1]
    o_ref[...] = _rms(acc_ref[...] + mod_ref[5:6, :] * rt_ref[...], nf_ref[...])


FINAL_TILES = 4


def _final(acc, routed, mod, norm_final, part_tile0, local0, n_tiles, out_tile0, out_tiles, prev=None):
    g = FINAL_TILES
    assert all(v % g == 0 for v in (part_tile0, local0, n_tiles, out_tile0, NT_P, TILES_PER_LAT))
    rows = g * TM
    local = pl.BlockSpec((rows, D_MODEL), lambda i: (i + local0 // g, 0))
    prev = () if prev is None else (prev,)
    return pl.pallas_call(
        _final_body,
        out_shape=jax.ShapeDtypeStruct((out_tiles * TM, D_MODEL), F32),
        grid=(n_tiles // g,),
        in_specs=[
            local, local,
            pl.BlockSpec((None, 6, D_MODEL), lambda i: (_mod_row(i * g + local0 + part_tile0), 0, 0)),
            _const_spec((1, D_MODEL)),
        ] + [pl.BlockSpec(memory_space=pl.ANY)] * len(prev),
        out_specs=pl.BlockSpec((rows, D_MODEL), lambda i: (i + out_tile0 // g, 0)),
        input_output_aliases={4 + k: 0 for k in range(len(prev))},
        compiler_params=_params("parallel"),
        name="final_norm",
    )(acc, routed, mod, norm_final.reshape(1, -1), *prev)


def kernel(x_prompt, x_sample, state_hgrn, c, c_ctx, w_ada, b_ada, norm_mix, norm_ffn, w_out, w_in_ab, w_sg, b_sg, norm_sg, w_dw, b_dw, norm_cv, w_in_hgrn, lb_raw, norm_o, w_router, b_router, w_gate, w_up, w_down, w_sh_gate, w_sh_up, w_sh_down, norm_final):
    cvecs = jnp.concatenate(
        [c_ctx.reshape(1, D_MODEL), c, jnp.zeros((N_MOD_ROWS - 1 - DEC_BATCH, D_MODEL), F32)], axis=0)
    mods = _ada_tables(cvecs, w_ada, b_ada)
    lb_sm = jax.nn.softmax(lb_raw.astype(F32), axis=0)
    lb1 = (jnp.cumsum(lb_sm, axis=0) - lb_sm[0])[1].reshape(2, C_HEADS, 1, C_DK)

    def moe(l, xin):
        parts = []
        for p in range(MOE_PARTS):
            h, eidx_t, w8_t, rank_t, counts, acc = _moe_pre(
                xin, mods[l], norm_ffn[l], w_router[l], b_router[l], w_sh_gate[l], w_sh_up[l], w_sh_down[l],
                p * NT_PART)
            wsplat = jnp.repeat(w8_t.T, SC_LANES, axis=1)
            parts += [acc, _moe_routed(h, eidx_t, rank_t, counts, wsplat, w_gate, w_up, w_down, l)]
        return parts

    x = _l0_mixer(x_prompt.reshape(T_P, D_MODEL), x_sample.reshape(T_S, D_MODEL), mods[0], norm_mix[0], w_in_ab[0], w_sg[0], b_sg[0], norm_sg[0], w_dw[0],
                  b_dw[0], norm_cv[0], w_out[0])
    parts = moe(0, x)
    xz = None
    for p in range(MOE_PARTS):
        xz = _hgrn_in(parts[2 * p], parts[2 * p + 1], p * NT_PART, xz, mods[0], mods[1], norm_mix[1], w_in_hgrn[0])
    x, zf, zb = xz
    s0t = jnp.swapaxes(state_hgrn[:, 0].astype(F32), -1, -2)
    o_fw, ns_fw = _gla(zf, zb, lb1[0], s0t[:, 0], rev=False)
    o_bw, ns_bw = _gla(zf, zb, lb1[1], s0t[:, 1], rev=True)
    x = _hgrn_out(o_fw, o_bw, zb, x, mods[1], norm_o[0], w_out[1])
    parts = moe(1, x)
    y_p = y_s = None
    for p in range(MOE_PARTS):
        lo, hi = p * NT_PART, (p + 1) * NT_PART
        if lo < NT_P:
            n = min(hi, NT_P) - lo
            y_p = _final(parts[2 * p], parts[2 * p + 1], mods[1], norm_final, lo, 0, n, lo, NT_P, y_p)
        if hi > NT_P:
            first = max(lo, NT_P)
            y_s = _final(parts[2 * p], parts[2 * p + 1], mods[1], norm_final, lo, first - lo, hi - first,
                         first - NT_P, NT_S, y_s)
    y_p = y_p.reshape(BATCH, SEQ, D_MODEL)
    y_s = y_s.reshape(DEC_BATCH, DEC_SEQ, D_MODEL)
    new_state = jnp.swapaxes(jnp.stack([ns_fw, ns_bw], axis=1), -1, -2)[:, None]
    return (y_p, y_s, new_state)
```

```python
import functools

import jax
import jax.numpy as jnp
from jax import lax
from jax.experimental import pallas as pl
from jax.experimental.pallas import tpu as pltpu
from jax.experimental.pallas import tpu_sc as plsc

F32 = jnp.float32
BF16 = jnp.bfloat16
HIGHEST = lax.Precision.HIGHEST

D_MODEL = 1024
BATCH = 32
SEQ = 256
DEPTH = 2
DEC_BATCH = 8
DEC_SEQ = 2048
GRID_W = 64
A_WIDTH = D_MODEL // 2
A_GROUPS = 4
A_GC = A_WIDTH // A_GROUPS
CHUNK_A = 128
B_WIDTH = D_MODEL - A_WIDTH
CONV_W = 31
CONV_PAD = CONV_W // 2
C_HEADS = 8
C_DK = 128
C_DV = D_MODEL // C_HEADS
C_FDIM = C_HEADS * C_DK
SCAN_CHUNK = 64
N_EXPERTS = 64
TOP_K = 8
N_GROUPS = 8
GROUP_SIZE = N_EXPERTS // N_GROUPS
TOPK_GROUPS = 4
D_EXPERT = 256
D_SHARED = 256
ROUTED_SCALE = 2.5
EPS = 1e-6

TM = 256
T_P = BATCH * SEQ
T_S = DEC_BATCH * DEC_SEQ
T = T_P + T_S
NT_P = T_P // TM
NT_S = T_S // TM
NT = NT_P + NT_S
TILES_PER_LAT = DEC_SEQ // TM
TE = 512
MOE_PARTS = 2
NT_PART = NT // MOE_PARTS
T_PART = NT_PART * TM
NB = T_PART * TOP_K // TE + N_EXPERTS
P_ROWS = NB * TE
D_PACK = D_MODEL // 2
N_MOD_ROWS = 16
CONV_HALO = 16
VMEM_LIMIT = 48 * 1024 * 1024
SC_CORES = 2
SC_SUBCORES = 16
SC_WORKERS = SC_CORES * SC_SUBCORES
SC_LANES = 16
SC_CHUNK = 64
COMBINE_TOKENS = SC_CHUNK // TOP_K


def _mod_row(i):
    return jnp.where(i < NT_P, 0, 1 + (i - NT_P) // TILES_PER_LAT)


def _silu(x):
    return x * jax.nn.sigmoid(x)


def _gelu(x):
    return x * (0.5 * (1.0 + jnp.tanh(0.7978845608028654 * (x + 0.044715 * (x * x * x)))))


def _rms(x, g):
    return x * lax.rsqrt(jnp.mean(x * x, axis=-1, keepdims=True) + EPS) * g


def _layernorm(x, g):
    xc = x - jnp.mean(x, axis=-1, keepdims=True)
    return xc * lax.rsqrt(jnp.mean(xc * xc, axis=-1, keepdims=True) + EPS) * g


def _modulate(x, g, shift, scale):
    return _rms(x, g) * (1.0 + scale) + shift


def _bdot(a, b):
    return jnp.dot(a, b, preferred_element_type=F32)


def _dot_nt(a, b, precision=None):
    return lax.dot_general(a, b, (((1,), (1,)), ((), ())), precision=precision,
                           preferred_element_type=F32)


def _dot_tn(a, b):
    return lax.dot_general(a, b, (((0,), (0,)), ((), ())), preferred_element_type=F32)


def _pack_pairs(x):
    m = x.shape[1] // 2
    lo = lax.bitcast_convert_type(x[:, :m].astype(BF16).astype(F32), jnp.uint32)
    hi = lax.bitcast_convert_type(x[:, m:].astype(BF16).astype(F32), jnp.uint32)
    return lax.bitcast_convert_type(hi | (lo >> 16), jnp.int32)


def _unpack_pairs(w):
    u = lax.bitcast_convert_type(w, jnp.uint32)
    lo = lax.bitcast_convert_type(u << 16, F32)
    hi = lax.bitcast_convert_type(u & jnp.uint32(0xFFFF0000), F32)
    return lo, hi


def _params(*sem):
    return pltpu.CompilerParams(dimension_semantics=sem, vmem_limit_bytes=VMEM_LIMIT)


def _const_spec(shape):
    nd = len(shape)
    return pl.BlockSpec(shape, lambda *_: (0,) * nd)


def _ada_body(c_ref, w_ref, b_ref, o_ref):
    s = _silu(c_ref[...])
    o_ref[...] = jnp.dot(s, w_ref[...], precision=HIGHEST, preferred_element_type=F32) + b_ref[...]


def _ada_tables(cvecs, w_ada, b_ada):
    out = pl.pallas_call(
        _ada_body,
        out_shape=jax.ShapeDtypeStruct((DEPTH, N_MOD_ROWS, 6 * D_MODEL), F32),
        grid=(DEPTH, 6),
        in_specs=[
            _const_spec((N_MOD_ROWS, D_MODEL)),
            pl.BlockSpec((None, D_MODEL, D_MODEL), lambda l, j: (l, 0, j)),
            pl.BlockSpec((None, 1, D_MODEL), lambda l, j: (l, 0, j)),
        ],
        out_specs=pl.BlockSpec((None, N_MOD_ROWS, D_MODEL), lambda l, j: (l, 0, j)),
        compiler_params=_params("parallel", "parallel"),
        name="ada_tables",
    )(cvecs, w_ada, b_ada.reshape(DEPTH, 1, 6 * D_MODEL))
    return out.reshape(DEPTH, N_MOD_ROWS, 6, D_MODEL)


def _mod_spec():
    return pl.BlockSpec((None, 6, D_MODEL), lambda i: (_mod_row(i), 0, 0))


def _tile_spec():
    return pl.BlockSpec((TM, D_MODEL), lambda i: (i, 0))


SUBLANES = 8


def _conv_segment(pad_ref, shift_ref, conv_ref, wdw_ref, pad_base, out_base, seg):
    rb = min(seg, 64)
    for cb in range(B_WIDTH // 128):
        cs = slice(cb * 128, (cb + 1) * 128)
        for r0 in range(0, seg, rb):
            acc = jnp.zeros((rb, 128), F32)
            for k in range(CONV_W):
                b = (CONV_HALO - CONV_PAD + k) % SUBLANES
                off = pad_base + r0 + CONV_HALO - CONV_PAD + k - b
                src = pad_ref if b == 0 else shift_ref.at[b - 1]
                acc = acc + wdw_ref[k:k + 1, cs] * src[off:off + rb, cs]
            conv_ref[out_base + r0:out_base + r0 + rb, cs] = acc


def _l0_body(xc_ref, xl_ref, mod_ref, nmix_ref, win_ref, wsg_ref, bsg_ref, nsg_ref, wdw_ref, bdw_ref,
             ncv_ref, wout_ref, o_ref, cat_ref, pad_ref, shift_ref, conv_ref):
    i = pl.program_id(0)
    x = jnp.where(i < NT_P, xc_ref[...], xl_ref[...])
    h = _modulate(x, nmix_ref[...], mod_ref[0:1, :], mod_ref[1:2, :])
    z = _bdot(h.astype(BF16), win_ref[...])
    u = _gelu(z[:, :A_WIDTH])
    vb = _layernorm(_gelu(z[:, A_WIDTH:2 * A_WIDTH]), nsg_ref[...]).astype(BF16)
    for n in range(TM // CHUNK_A):
        rs = slice(n * CHUNK_A, (n + 1) * CHUNK_A)
        for g in range(A_GROUPS):
            cs = slice(g * A_GC, (g + 1) * A_GC)
            m = _bdot(wsg_ref[g], vb[rs, cs]) + bsg_ref[g]
            cat_ref[rs, cs] = (u[rs, cs] * m).astype(BF16)
    hb = z[:, 2 * A_WIDTH:2 * A_WIDTH + B_WIDTH] * jax.nn.sigmoid(z[:, 2 * A_WIDTH + B_WIDTH:])

    def conv_tile(seg):
        stride = seg + 2 * CONV_HALO
        halo = jnp.zeros((CONV_HALO, B_WIDTH), F32)
        for s in range(TM // seg):
            b = s * stride
            pad_ref[b:b + CONV_HALO, :] = halo
            pad_ref[b + CONV_HALO:b + CONV_HALO + seg, :] = hb[s * seg:(s + 1) * seg, :]
            pad_ref[b + CONV_HALO + seg:b + stride, :] = halo
        rows = (TM // seg) * stride - SUBLANES
        for b in range(1, SUBLANES):
            shift_ref[b - 1, 0:rows, :] = pad_ref[b:b + rows, :]
        for s in range(TM // seg):
            _conv_segment(pad_ref, shift_ref, conv_ref, wdw_ref, s * stride, s * seg, seg)

    @pl.when(i < NT_P)
    def _():
        conv_tile(SEQ)

    @pl.when(i >= NT_P)
    def _():
        conv_tile(GRID_W)

    yb = _layernorm(conv_ref[...] + bdw_ref[...], ncv_ref[...])
    cat_ref[:, A_WIDTH:] = _silu(yb).astype(BF16)
    out = _bdot(cat_ref[...], wout_ref[...])
    o_ref[...] = x + mod_ref[2:3, :] * out


def _l0_mixer(x_ctx, x_lat, mod, norm_mix, w_in, w_sg, b_sg, norm_sg, w_dw, b_dw, norm_cv, w_out):
    assert TM == SEQ and TM % GRID_W == 0 and TM % CHUNK_A == 0
    pad_rows = (TM // GRID_W) * (GRID_W + 2 * CONV_HALO)
    assert pad_rows >= SEQ + 2 * CONV_HALO
    return pl.pallas_call(
        _l0_body,
        out_shape=jax.ShapeDtypeStruct((T, D_MODEL), F32),
        grid=(NT,),
        in_specs=[
            pl.BlockSpec((TM, D_MODEL), lambda i: (jnp.minimum(i, NT_P - 1), 0)),
            pl.BlockSpec((TM, D_MODEL), lambda i: (jnp.maximum(i - NT_P, 0), 0)),
            _mod_spec(),
            _const_spec((1, D_MODEL)),
            _const_spec((D_MODEL, 2 * A_WIDTH + 2 * B_WIDTH)),
            _const_spec((A_GROUPS, CHUNK_A, CHUNK_A)),
            _const_spec((A_GROUPS, CHUNK_A, 1)),
            _const_spec((1, A_WIDTH)),
            _const_spec((CONV_W, B_WIDTH)),
            _const_spec((1, B_WIDTH)),
            _const_spec((1, B_WIDTH)),
            _const_spec((D_MODEL, D_MODEL)),
        ],
        out_specs=_tile_spec(),
        scratch_shapes=[
            pltpu.VMEM((TM, D_MODEL), BF16),
            pltpu.VMEM((pad_rows, B_WIDTH), F32),
            pltpu.VMEM((SUBLANES - 1, pad_rows, B_WIDTH), F32),
            pltpu.VMEM((TM, B_WIDTH), F32),
        ],
        compiler_params=_params("parallel"),
        name="l0_mixer",
    )(x_ctx, x_lat, mod, norm_mix.reshape(1, -1), w_in.astype(BF16), w_sg.astype(BF16),
      b_sg.reshape(A_GROUPS, CHUNK_A, 1), norm_sg.reshape(1, -1), w_dw, b_dw.reshape(1, -1),
      norm_cv.reshape(1, -1), w_out.astype(BF16))


def _route(scores, biased):
    n = scores.shape[-1]
    shp = (N_GROUPS, GROUP_SIZE, n)
    s3 = scores.reshape(shp)
    b3 = biased.reshape(shp)
    m_iota = lax.broadcasted_iota(jnp.int32, shp, 1).astype(F32)
    g_iota = lax.broadcasted_iota(jnp.int32, shp, 0).astype(F32)
    e_iota = g_iota * GROUP_SIZE + m_iota
    neg = -jnp.inf

    def amax1(v):
        return jnp.max(v, axis=1, keepdims=True)

    def amin1(v):
        return jnp.min(v, axis=1, keepdims=True)

    m1 = amax1(b3)
    i1 = amin1(jnp.where(b3 == m1, m_iota, float(GROUP_SIZE)))
    m2 = amax1(jnp.where(m_iota == i1, neg, b3))
    grp = m1 + m2
    gi1 = lax.broadcasted_iota(jnp.int32, grp.shape, 0).astype(F32)
    gmask = jnp.zeros(grp.shape, jnp.bool_)
    for _ in range(TOPK_GROUPS):
        gm = jnp.max(grp, axis=0, keepdims=True)
        gi = jnp.min(jnp.where(grp == gm, gi1, float(N_GROUPS)), axis=0, keepdims=True)
        hit = gi1 == gi
        gmask = jnp.logical_or(gmask, hit)
        grp = jnp.where(hit, neg, grp)
    cand = jnp.where(gmask, b3, neg)
    ids, vals, hits = [], [], []
    for _ in range(TOP_K):
        mx = jnp.max(amax1(cand), axis=0, keepdims=True)
        ei = jnp.min(amin1(jnp.where(cand == mx, e_iota, float(N_EXPERTS))), axis=0, keepdims=True)
        hit = e_iota == ei
        ids.append(ei.reshape(1, n))
        vals.append(_pick(hit, s3))
        hits.append(hit)
        cand = jnp.where(hit, neg, cand)
    return jnp.concatenate(ids, axis=0).astype(jnp.int32), jnp.concatenate(vals, axis=0), hits


def _pick(hit, v3):
    s = jnp.sum(jnp.sum(jnp.where(hit, v3, 0.0), axis=1, keepdims=True), axis=0, keepdims=True)
    return s.reshape(1, v3.shape[-1])


def _moe_pre_body(x_ref, mod_ref, nffn_ref, wrt_ref, br_ref, wsg_ref, wsu_ref, wsd_ref,
                  h_ref, eidx_ref, w8_ref, rank_ref, cnt_ref, acc_ref, run_ref):
    @pl.when(pl.program_id(0) == 0)
    def _():
        run_ref[...] = jnp.zeros(run_ref.shape, F32)

    x = x_ref[...]
    h = _modulate(x, nffn_ref[...], mod_ref[3:4, :], mod_ref[4:5, :])
    h_ref[...] = _pack_pairs(h)
    hb = h.astype(BF16)
    h_lo = (h - hb.astype(F32)).astype(BF16)
    wr = wrt_ref[...]
    wr_hi = wr.astype(BF16)
    wr_lo = (wr - wr_hi.astype(F32)).astype(BF16)
    logits_t = _dot_nt(wr_hi, hb) + (_dot_nt(wr_hi, h_lo) + _dot_nt(wr_lo, hb))
    scores = jax.nn.sigmoid(logits_t)
    eidx, sv, hits = _route(scores, scores + br_ref[...])
    eidx_ref[...] = eidx
    w8_ref[...] = sv / jnp.sum(sv, axis=0, keepdims=True) * ROUTED_SCALE
    sel3 = hits[0]
    for hit in hits[1:]:
        sel3 = jnp.logical_or(sel3, hit)
    sel = sel3.astype(F32).reshape(N_EXPERTS, TM)
    earlier = (lax.broadcasted_iota(jnp.int32, (TM, TM), 0)
               < lax.broadcasted_iota(jnp.int32, (TM, TM), 1)).astype(BF16)
    rank3 = (_bdot(sel.astype(BF16), earlier) + run_ref[...]).reshape(N_GROUPS, GROUP_SIZE, TM)
    rank_ref[...] = jnp.concatenate([_pick(hit, rank3) for hit in hits], axis=0).astype(jnp.int32)
    run_ref[...] = run_ref[...] + jnp.sum(sel, axis=1, keepdims=True)
    cnt_ref[...] = run_ref[...].astype(jnp.int32)
    sh = _bdot((_silu(_bdot(hb, wsg_ref[...])) * _bdot(hb, wsu_ref[...])).astype(BF16), wsd_ref[...])
    acc_ref[...] = x + mod_ref[5:6, :] * sh


def _moe_pre(x, mod, norm_ffn, w_router, b_router, w_sh_gate, w_sh_up, w_sh_down, tile0):
    return pl.pallas_call(
        _moe_pre_body,
        out_shape=(
            jax.ShapeDtypeStruct((T_PART, D_PACK), jnp.int32),
            jax.ShapeDtypeStruct((TOP_K, T_PART), jnp.int32),
            jax.ShapeDtypeStruct((TOP_K, T_PART), F32),
            jax.ShapeDtypeStruct((TOP_K, T_PART), jnp.int32),
            jax.ShapeDtypeStruct((N_EXPERTS, 1), jnp.int32),
            jax.ShapeDtypeStruct((T_PART, D_MODEL), F32),
        ),
        grid=(NT_PART,),
        in_specs=[
            pl.BlockSpec((TM, D_MODEL), lambda i: (i + tile0, 0)),
            pl.BlockSpec((None, 6, D_MODEL), lambda i: (_mod_row(i + tile0), 0, 0)),
            _const_spec((1, D_MODEL)),
            _const_spec((N_EXPERTS, D_MODEL)),
            _const_spec((N_EXPERTS, 1)),
            _const_spec((D_MODEL, D_SHARED)),
            _const_spec((D_MODEL, D_SHARED)),
            _const_spec((D_SHARED, D_MODEL)),
        ],
        out_specs=(
            pl.BlockSpec((TM, D_PACK), lambda i: (i, 0)),
            pl.BlockSpec((TOP_K, TM), lambda i: (0, i)),
            pl.BlockSpec((TOP_K, TM), lambda i: (0, i)),
            pl.BlockSpec((TOP_K, TM), lambda i: (0, i)),
            _const_spec((N_EXPERTS, 1)),
            _tile_spec(),
        ),
        scratch_shapes=[pltpu.VMEM((N_EXPERTS, 1), F32)],
        compiler_params=_params("arbitrary"),
        name="moe_router_shared",
    )(x, mod, norm_ffn.reshape(1, -1), w_router.T, b_router.reshape(N_EXPERTS, 1),
      w_sh_gate.astype(BF16), w_sh_up.astype(BF16), w_sh_down.astype(BF16))


EXPERT_RING = 4


WEIGHT_SLOTS = 2


EXPERT_TILES_PER_STEP = 8


def _expert_body(be_ref, nv_ref, nu_ref, run_ref, rexp_ref, nrun_ref, xs_hbm, wg_hbm, wu_hbm, wd_hbm, ys_ref,
                 wgu_s, wd_s, ring, sems, wg_buf, wu_buf, wd_buf, wsems, *, layer):
    n_used = nu_ref[0]
    n_runs = nrun_ref[0]

    def fetch(t):
        slot = t % EXPERT_RING
        return pltpu.make_async_copy(xs_hbm.at[pl.ds(t * TE, TE)], ring.at[slot], sems.at[slot])

    def wfetch(r):
        slot = r % WEIGHT_SLOTS
        e = rexp_ref[r]
        return [pltpu.make_async_copy(wg_hbm.at[layer, e], wg_buf.at[slot], wsems.at[slot, 0]),
                pltpu.make_async_copy(wu_hbm.at[layer, e], wu_buf.at[slot], wsems.at[slot, 1]),
                pltpu.make_async_copy(wd_hbm.at[layer, e], wd_buf.at[slot], wsems.at[slot, 2])]

    def tile_step(j, out_rows):
        live_tile = j < n_used

        @pl.when(j == 0)
        def _():
            for t in range(EXPERT_RING - 1):
                @pl.when(t < n_used)
                def _():
                    fetch(t).start()
            for r in range(WEIGHT_SLOTS):
                @pl.when(r < n_runs)
                def _():
                    for cp in wfetch(r):
                        cp.start()

        @pl.when(j + (EXPERT_RING - 1) < n_used)
        def _():
            fetch(j + (EXPERT_RING - 1)).start()

        @pl.when(jnp.logical_and(live_tile, jnp.logical_or(j == 0, be_ref[j] != be_ref[jnp.maximum(j - 1, 0)])))
        def _():
            r = run_ref[j]
            slot = r % WEIGHT_SLOTS
            for cp in wfetch(r):
                cp.wait()
            wgu_s[:, :D_EXPERT] = wg_buf[slot].astype(BF16)
            wgu_s[:, D_EXPERT:] = wu_buf[slot].astype(BF16)
            wd_s[...] = wd_buf[slot].astype(BF16)

            @pl.when(r + WEIGHT_SLOTS < n_runs)
            def _():
                for cp in wfetch(r + WEIGHT_SLOTS):
                    cp.start()

        @pl.when(live_tile)
        def _():
            fetch(j).wait()
            live = lax.broadcasted_iota(jnp.int32, (TE, 1), 0) < nv_ref[j]
            lo, hi = _unpack_pairs(jnp.where(live, ring[j % EXPERT_RING], 0))
            xb = jnp.concatenate([lo.astype(BF16), hi.astype(BF16)], axis=1)
            hgu = _bdot(xb, wgu_s[...])
            hh = _silu(hgu[:, :D_EXPERT]) * hgu[:, D_EXPERT:]
            ys_ref[out_rows, :] = _pack_pairs(_bdot(hh.astype(BF16), wd_s[...]))

    for u in range(EXPERT_TILES_PER_STEP):
        tile_step(pl.program_id(0) * EXPERT_TILES_PER_STEP + u, pl.ds(u * TE, TE))


def _experts(block_expert, block_rows, n_used, block_run, run_expert, n_runs, xs, w_gate, w_up, w_down, layer):
    g = EXPERT_TILES_PER_STEP
    assert NB % g == 0

    def row_map(s, be, nv, nu, run, rexp, nrun):
        return (jnp.minimum(s, (nu[0] - 1) // g), 0)

    any_spec = pl.BlockSpec(memory_space=pl.ANY)
    return pl.pallas_call(
        functools.partial(_expert_body, layer=layer),
        out_shape=jax.ShapeDtypeStruct((P_ROWS, D_PACK), jnp.int32),
        grid_spec=pltpu.PrefetchScalarGridSpec(
            num_scalar_prefetch=6,
            grid=(NB // g,),
            in_specs=[any_spec, any_spec, any_spec, any_spec],
            out_specs=pl.BlockSpec((g * TE, D_PACK), row_map),
            scratch_shapes=[
                pltpu.VMEM((D_MODEL, 2 * D_EXPERT), BF16),
                pltpu.VMEM((D_EXPERT, D_MODEL), BF16),
                pltpu.VMEM((EXPERT_RING, TE, D_PACK), jnp.int32),
                pltpu.SemaphoreType.DMA((EXPERT_RING,)),
                pltpu.VMEM((WEIGHT_SLOTS, D_MODEL, D_EXPERT), F32),
                pltpu.VMEM((WEIGHT_SLOTS, D_MODEL, D_EXPERT), F32),
                pltpu.VMEM((WEIGHT_SLOTS, D_EXPERT, D_MODEL), F32),
                pltpu.SemaphoreType.DMA((WEIGHT_SLOTS, 3)),
            ],
        ),
        compiler_params=_params("arbitrary"),
        name="moe_experts",
    )(block_expert, block_rows, n_used, block_run, run_expert, n_runs, xs, w_gate, w_up, w_down)


def _positions_body(start_ref, eidx_ref, rank_ref, pos_ref):
    eidx = eidx_ref[...]
    base = jnp.zeros(eidx.shape, jnp.int32)
    for e in range(N_EXPERTS):
        base = jnp.where(eidx == e, start_ref[e], base)
    pos_ref[...] = base * TE + rank_ref[...]


def _positions(blk_start, eidx_t, rank_t):
    full = pl.BlockSpec((TOP_K, T_PART), lambda i, s: (0, 0))
    return pl.pallas_call(
        _positions_body,
        out_shape=jax.ShapeDtypeStruct((TOP_K, T_PART), jnp.int32),
        grid_spec=pltpu.PrefetchScalarGridSpec(
            num_scalar_prefetch=1, grid=(1,), in_specs=[full, full], out_specs=full),
        compiler_params=_params("arbitrary"),
        name="moe_positions",
    )(blk_start, eidx_t, rank_t)


def _sc_mesh():
    return plsc.VectorSubcoreMesh(core_axis_name="c", subcore_axis_name="s")


def _sc_worker():
    return lax.axis_index("s") * SC_CORES + lax.axis_index("c")


def _sc_scatter_rows(h, pos_rows):
    c = SC_CHUNK
    n_chunks = T_PART // SC_WORKERS // c
    width = h.shape[1]

    @functools.partial(
        pl.kernel, mesh=_sc_mesh(),
        out_type=jax.ShapeDtypeStruct((P_ROWS, width), h.dtype),
        scratch_types=[pltpu.VMEM((n_chunks * TOP_K, c), jnp.int32),
                       pltpu.VMEM((c, width), h.dtype), pltpu.VMEM((c, width), h.dtype)]
        + [pltpu.SemaphoreType.DMA] * 4,
        name="moe_dispatch_scatter",
    )
    def scatter(h_hbm, pos_hbm, xs_hbm, idx_v, buf0, buf1, sem_in0, sem_in1, sem_out0, sem_out1):
        assert n_chunks % 2 == 0
        bufs, sem_in, sem_out = (buf0, buf1), (sem_in0, sem_in1), (sem_out0, sem_out1)
        first = _sc_worker() * n_chunks
        pltpu.sync_copy(pos_hbm.at[pl.ds(first * TOP_K, n_chunks * TOP_K)], idx_v)

        def load(i, b):
            return pltpu.make_async_copy(h_hbm.at[pl.ds((first + i) * c, c)], bufs[b], sem_in[b])

        def puts(i, b):
            return [pltpu.make_async_copy(bufs[b], xs_hbm.at[idx_v.at[i * TOP_K + k]], sem_out[b])
                    for k in range(TOP_K)]

        load(0, 0).start()
        load(1, 1).start()

        @pl.loop(0, n_chunks, step=2)
        def _(i):
            for b in range(2):
                load(i + b, b).wait()
                for cp in puts(i + b, b):
                    cp.start()
            for b in range(2):
                for cp in puts(i + b, b):
                    cp.wait()

                @pl.when(i + 2 + b < n_chunks)
                def _():
                    load(i + 2 + b, b).start()

    return scatter(h, pos_rows)


def _sc_combine(ys, pos_tk, wsplat):
    ct = COMBINE_TOKENS
    rows = ct * TOP_K
    tok_per_worker = T_PART // SC_WORKERS
    n_chunks = tok_per_worker // ct
    width = ys.shape[1]
    vmem = pltpu.VMEM

    @functools.partial(
        pl.kernel, mesh=_sc_mesh(),
        out_type=jax.ShapeDtypeStruct((T_PART, D_MODEL), F32),
        scratch_types=[vmem((tok_per_worker * TOP_K,), jnp.int32),
                       vmem((rows, width), ys.dtype), vmem((rows, width), ys.dtype),
                       vmem((ct, TOP_K * SC_LANES), F32), vmem((ct, TOP_K * SC_LANES), F32),
                       vmem((ct, D_MODEL), F32), vmem((ct, D_MODEL), F32)]
        + [pltpu.SemaphoreType.DMA] * 6,
        compiler_params=pltpu.CompilerParams(needs_layout_passes=False),
        name="moe_combine",
    )
    def combine(ys_hbm, pos_hbm, w_hbm, out_hbm, idx_v, g0, g1, w0, w1, o0, o1,
                sem_g0, sem_g1, sem_w0, sem_w1, sem_o0, sem_o1):
        assert n_chunks % 2 == 0
        gbuf, wbuf, obuf = (g0, g1), (w0, w1), (o0, o1)
        sem_g, sem_w, sem_o = (sem_g0, sem_g1), (sem_w0, sem_w1), (sem_o0, sem_o1)
        tok0 = _sc_worker() * tok_per_worker
        pltpu.sync_copy(pos_hbm.at[pl.ds(tok0 * TOP_K, tok_per_worker * TOP_K)], idx_v)

        def fetch(i, b):
            return [pltpu.make_async_copy(ys_hbm.at[idx_v.at[pl.ds(i * rows, rows)]], gbuf[b], sem_g[b]),
                    pltpu.make_async_copy(w_hbm.at[pl.ds(tok0 + i * ct, ct)], wbuf[b], sem_w[b])]

        def flush(i, b):
            return pltpu.make_async_copy(obuf[b], out_hbm.at[pl.ds(tok0 + i * ct, ct)], sem_o[b])

        def reduce_chunk(b):
            @pl.loop(0, ct)
            def _(t):
                wv = [wbuf[b][t, pl.ds(k * SC_LANES, SC_LANES)] for k in range(TOP_K)]

                @pl.loop(0, width // SC_LANES)
                def _(j):
                    col = j * SC_LANES
                    acc_lo = acc_hi = None
                    for k in range(TOP_K):
                        word = gbuf[b][t * TOP_K + k, pl.ds(col, SC_LANES)]
                        lo = lax.bitcast_convert_type(word << 16, F32)
                        hi = lax.bitcast_convert_type(word & jnp.int32(-65536), F32)
                        acc_lo = wv[k] * lo if k == 0 else acc_lo + wv[k] * lo
                        acc_hi = wv[k] * hi if k == 0 else acc_hi + wv[k] * hi
                    obuf[b][t, pl.ds(col, SC_LANES)] = acc_lo
                    obuf[b][t, pl.ds(D_PACK + col, SC_LANES)] = acc_hi

        for b in range(2):
            for cp in fetch(b, b):
                cp.start()

        @pl.loop(0, n_chunks, step=2)
        def _(i):
            for b in range(2):
                for cp in fetch(i + b, b):
                    cp.wait()

                @pl.when(i > 0)
                def _():
                    flush(i + b - 2, b).wait()

                reduce_chunk(b)
                flush(i + b, b).start()

                @pl.when(i + 2 + b < n_chunks)
                def _():
                    for cp in fetch(i + 2 + b, b):
                        cp.start()

        for b in range(2):
            flush(n_chunks - 2 + b, b).wait()

    return combine(ys, pos_tk, wsplat)


def _moe_routed(h, eidx_t, rank_t, counts, wsplat, w_gate, w_up, w_down, layer):
    counts = counts.reshape(1, N_EXPERTS)
    nblk = (counts + TE - 1) // TE
    blk_end = jnp.cumsum(nblk, axis=1)
    blk_start = blk_end - nblk
    blocks = jnp.arange(NB, dtype=jnp.int32).reshape(NB, 1)
    block_expert = jnp.minimum(jnp.sum(blocks >= blk_end, axis=1, keepdims=True), N_EXPERTS - 1)
    mine = block_expert == jnp.arange(N_EXPERTS, dtype=jnp.int32).reshape(1, N_EXPERTS)
    cnt_b = jnp.sum(jnp.where(mine, counts, 0), axis=1, keepdims=True)
    start_b = jnp.sum(jnp.where(mine, blk_start, 0), axis=1, keepdims=True)
    block_rows = jnp.clip(cnt_b - (blocks - start_b) * TE, 0, TE)
    n_used = blk_end[0, -1].reshape(1).astype(jnp.int32)
    present = (nblk > 0).astype(jnp.int32)
    run_of_expert = jnp.cumsum(present, axis=1) - 1
    block_run = jnp.sum(jnp.where(mine, run_of_expert, 0), axis=1)
    experts = jnp.arange(N_EXPERTS, dtype=jnp.int32).reshape(1, N_EXPERTS)
    run_hit = jnp.logical_and(run_of_expert == experts.reshape(N_EXPERTS, 1), present > 0)
    run_expert = jnp.sum(jnp.where(run_hit, experts, 0), axis=1)
    n_runs = jnp.sum(present).reshape(1)
    pos = _positions(blk_start.reshape(N_EXPERTS).astype(jnp.int32), eidx_t, rank_t)
    pos_rows = pos.reshape(TOP_K, T_PART // SC_CHUNK, SC_CHUNK).transpose(1, 0, 2).reshape(-1, SC_CHUNK)
    xs = _sc_scatter_rows(h, pos_rows)
    ys = _experts(block_expert.reshape(NB).astype(jnp.int32), block_rows.reshape(NB).astype(jnp.int32),
                  n_used, block_run.astype(jnp.int32), run_expert.astype(jnp.int32), n_runs.astype(jnp.int32),
                  xs, w_gate, w_up, w_down, layer)
    return _sc_combine(ys, pos.T.reshape(-1), wsplat)


N_SLABS = (3 * C_FDIM + 2 * D_MODEL) // 128


F32_GROUPS = (1, 2)
B16_GROUPS = (0, 3, 4)


def _hgrn_in_body(acc_ref, rt_ref, mod0_ref, mod1_ref, nmix_ref, win_ref, *rest):
    x_ref, zf_ref, zb_ref = rest[-3:]
    x = acc_ref[...] + mod0_ref[5:6, :] * rt_ref[...]
    x_ref[...] = x
    hb = _modulate(x, nmix_ref[...], mod1_ref[0:1, :], mod1_ref[1:2, :]).astype(BF16)
    for s in range(N_SLABS // C_HEADS):
        zz = _bdot(hb, win_ref[:, s * D_MODEL:(s + 1) * D_MODEL])
        for hh in range(C_HEADS):
            if s in F32_GROUPS:
                zf_ref[F32_GROUPS.index(s) * C_HEADS + hh] = zz[:, hh * 128:(hh + 1) * 128]
            else:
                zb_ref[B16_GROUPS.index(s) * C_HEADS + hh] = zz[:, hh * 128:(hh + 1) * 128].astype(BF16)


HGRN_IN_TILES = 2


def _hgrn_in(acc, routed, tile0, prev, mod0, mod1, norm_mix, w_in):
    g = HGRN_IN_TILES
    assert NT_PART % g == 0 and tile0 % g == 0 and NT_P % g == 0 and TILES_PER_LAT % g == 0
    rows = g * TM
    block0 = tile0 // g

    def shifted_mod():
        return pl.BlockSpec((None, 6, D_MODEL), lambda i: (_mod_row(i * g + tile0), 0, 0))

    local = pl.BlockSpec((rows, D_MODEL), lambda i: (i, 0))
    any_spec = pl.BlockSpec(memory_space=pl.ANY)
    prev = () if prev is None else tuple(prev)
    n_in = 6
    return pl.pallas_call(
        _hgrn_in_body,
        out_shape=(
            jax.ShapeDtypeStruct((T, D_MODEL), F32),
            jax.ShapeDtypeStruct((len(F32_GROUPS) * C_HEADS, T, 128), F32),
            jax.ShapeDtypeStruct((len(B16_GROUPS) * C_HEADS, T, 128), BF16),
        ),
        grid=(NT_PART // g,),
        in_specs=[
            local, local, shifted_mod(), shifted_mod(),
            _const_spec((1, D_MODEL)),
            pl.BlockSpec((D_MODEL, 3 * C_FDIM + 2 * D_MODEL), lambda i: (0, 0), pipeline_mode=pl.Buffered(1)),
        ] + [any_spec] * len(prev),
        out_specs=(
            pl.BlockSpec((rows, D_MODEL), lambda i: (i + block0, 0)),
            pl.BlockSpec((len(F32_GROUPS) * C_HEADS, rows, 128), lambda i: (0, i + block0, 0)),
            pl.BlockSpec((len(B16_GROUPS) * C_HEADS, rows, 128), lambda i: (0, i + block0, 0)),
        ),
        input_output_aliases={n_in + k: k for k in range(len(prev))},
        compiler_params=_params("parallel"),
        name="hgrn_in_proj",
    )(acc, routed, mod0, mod1, norm_mix.reshape(1, -1), w_in.astype(BF16), *prev)


GLA_TILES = 2


def _gla_body(q_ref, f_ref, v_ref, lb_ref, s0_ref, o_ref, ns_ref, st_ref, *, rev):
    blk = NT // GLA_TILES - 1 - pl.program_id(0) if rev else pl.program_id(0)
    row = lax.broadcasted_iota(jnp.int32, (TM, TM), 0)
    col = lax.broadcasted_iota(jnp.int32, (TM, TM), 1)
    same_chunk = (row // SCAN_CHUNK) == (col // SCAN_CHUNK)
    seen = jnp.logical_and(same_chunk, (col >= row) if rev else (col <= row))
    cum_w = seen.astype(BF16)
    mid = SCAN_CHUNK // 2 if rev else SCAN_CHUNK // 2 - 1
    last = 0 if rev else SCAN_CHUNK - 1
    n_chunks = TM // SCAN_CHUNK
    order = range(n_chunks - 1, -1, -1) if rev else range(n_chunks)
    group = 4

    def chunk_rows(b, off):
        return jnp.concatenate(
            [jnp.broadcast_to(b[c * SCAN_CHUNK + off:c * SCAN_CHUNK + off + 1, :], (SCAN_CHUNK, b.shape[1]))
             for c in range(n_chunks)], axis=0)

    def tile_step(sub):
        ti = blk * GLA_TILES + sub
        r0 = sub * TM
        is_ctx = ti < NT_P
        first_lat = (ti - NT_P) % TILES_PER_LAT == (TILES_PER_LAT - 1 if rev else 0)

        @pl.when(is_ctx)
        def _():
            st_ref[...] = jnp.zeros(st_ref.shape, F32)

        @pl.when(jnp.logical_and(jnp.logical_not(is_ctx), first_lat))
        def _():
            st_ref[...] = s0_ref[...]

        def head_group(gi, carry):
            heads = [gi * group + u for u in range(group)]
            qs, kk, vv, bcum = [], [], [], []
            for hd in heads:
                lb = lb_ref[hd]
                qs.append(_silu(q_ref[hd, pl.ds(r0, TM), :].astype(F32)) * (C_DK ** -0.5))
                fg = lb + (1.0 - lb) * jax.nn.sigmoid(f_ref[hd, pl.ds(r0, TM), :])
                kk.append(1.0 - fg)
                vv.append(v_ref[hd, pl.ds(r0, TM), :])
                g = jnp.log(fg)
                g_hi = g.astype(BF16)
                r1 = g - g_hi.astype(F32)
                g_mid = r1.astype(BF16)
                g_lo = (r1 - g_mid.astype(F32)).astype(BF16)
                bcum.append(_bdot(cum_w, g_hi) + _bdot(cum_w, g_mid) + _bdot(cum_w, g_lo))
            o_intra, q_dec, kv, decay = [], [], [], []
            for u in range(group):
                b_mid = chunk_rows(bcum[u], mid)
                b_last = chunk_rows(bcum[u], last)
                qe = (qs[u] * jnp.exp(bcum[u] - b_mid)).astype(BF16)
                ke = (kk[u] * jnp.exp(b_mid - bcum[u])).astype(BF16)
                att = jnp.where(seen, _dot_nt(qe, ke), 0.0)
                o_intra.append(_bdot(att.astype(BF16), vv[u]))
                q_dec.append((qs[u] * jnp.exp(bcum[u])).astype(BF16))
                k_dec = (kk[u] * jnp.exp(b_last - bcum[u])).astype(BF16)
                kv.append([_dot_tn(vv[u][c * SCAN_CHUNK:(c + 1) * SCAN_CHUNK], k_dec[c * SCAN_CHUNK:(c + 1) * SCAN_CHUNK])
                           for c in range(n_chunks)])
                decay.append([jnp.exp(bcum[u][c * SCAN_CHUNK + last:c * SCAN_CHUNK + last + 1, :])
                              for c in range(n_chunks)])
            st = [st_ref[hd] for hd in heads]
            for c in order:
                sl = slice(c * SCAN_CHUNK, (c + 1) * SCAN_CHUNK)
                for u, hd in enumerate(heads):
                    o_ref[hd, pl.ds(r0 + c * SCAN_CHUNK, SCAN_CHUNK), :] = (
                        o_intra[u][sl] + _dot_nt(q_dec[u][sl], st[u].astype(BF16))).astype(BF16)
                    st[u] = decay[u][c] * st[u] + kv[u][c]
            for u, hd in enumerate(heads):
                st_ref[hd] = st[u]
            return carry

        lax.fori_loop(0, C_HEADS // group, head_group, 0)

        @pl.when(is_ctx)
        def _():
            ns_ref[sub] = st_ref[...]

    for sub in (reversed(range(GLA_TILES)) if rev else range(GLA_TILES)):
        tile_step(sub)


def _gla(zf, zb, lb_dir, s0t_dir, *, rev):
    g = GLA_TILES
    assert NT % g == 0 and NT_P % g == 0 and TILES_PER_LAT % g == 0
    rows = g * TM

    def blk_of(j):
        return NT // g - 1 - j if rev else j

    f_slab = F32_GROUPS.index(2 if rev else 1)

    def lat_map(j):
        return (jnp.clip((blk_of(j) * g - NT_P) // TILES_PER_LAT, 0, DEC_BATCH - 1), 0, 0, 0)

    return pl.pallas_call(
        functools.partial(_gla_body, rev=rev),
        out_shape=(
            jax.ShapeDtypeStruct((C_HEADS, T, C_DV), BF16),
            jax.ShapeDtypeStruct((BATCH, C_HEADS, C_DV, C_DK), F32),
        ),
        grid=(NT // g,),
        in_specs=[
            pl.BlockSpec((C_HEADS, rows, 128), lambda j: (B16_GROUPS.index(0), blk_of(j), 0)),
            pl.BlockSpec((C_HEADS, rows, 128), lambda j: (f_slab, blk_of(j), 0)),
            pl.BlockSpec((C_HEADS, rows, 128), lambda j: (B16_GROUPS.index(3), blk_of(j), 0)),
            _const_spec((C_HEADS, 1, C_DK)),
            pl.BlockSpec((None, C_HEADS, C_DV, C_DK), lat_map),
        ],
        out_specs=(
            pl.BlockSpec((C_HEADS, rows, C_DV), lambda j: (0, blk_of(j), 0)),
            pl.BlockSpec((g, C_HEADS, C_DV, C_DK),
                         lambda j: (jnp.minimum(blk_of(j), NT_P // g - 1), 0, 0, 0)),
        ),
        scratch_shapes=[pltpu.VMEM((C_HEADS, C_DV, C_DK), F32)],
        compiler_params=_params("arbitrary"),
        name="gla_bwd" if rev else "gla_fwd",
    )(zb, zf, zb, lb_dir, s0t_dir)


def _hgrn_out_body(ofw_ref, obw_ref, gate_ref, x_ref, mod_ref, no_ref, wout_ref, o_ref, cat_ref):
    for hd in range(C_HEADS):
        o = ofw_ref[hd].astype(F32) + obw_ref[hd].astype(F32)
        gate = gate_ref[hd].astype(F32)
        cat_ref[:, hd * C_DV:(hd + 1) * C_DV] = (_rms(o, no_ref[...]) * _silu(gate)).astype(BF16)
    o_ref[...] = x_ref[...] + mod_ref[2:3, :] * _bdot(cat_ref[...], wout_ref[...])


HGRN_OUT_TILES = 4


def _hgrn_out(o_fw, o_bw, zb, x, mod, norm_o, w_out):
    g = HGRN_OUT_TILES
    assert NT % g == 0 and NT_P % g == 0 and TILES_PER_LAT % g == 0
    rows = g * TM
    head_spec = pl.BlockSpec((C_HEADS, rows, C_DV), lambda i: (0, i, 0))
    row_spec = pl.BlockSpec((rows, D_MODEL), lambda i: (i, 0))
    return pl.pallas_call(
        _hgrn_out_body,
        out_shape=jax.ShapeDtypeStruct((T, D_MODEL), F32),
        grid=(NT // g,),
        in_specs=[
            head_spec, head_spec,
            pl.BlockSpec((C_HEADS, rows, 128), lambda i: (B16_GROUPS.index(4), i, 0)),
            row_spec,
            pl.BlockSpec((None, 6, D_MODEL), lambda i: (_mod_row(i * g), 0, 0)),
            _const_spec((1, C_DV)),
            _const_spec((D_MODEL, D_MODEL)),
        ],
        out_specs=row_spec,
        scratch_shapes=[pltpu.VMEM((rows, D_MODEL), BF16)],
        compiler_params=_params("parallel"),
        name="hgrn_out_proj",
    )(o_fw, o_bw, zb, x, mod, norm_o.reshape(1, -1), w_out.astype(BF16))


def _final_body(acc_ref, rt_ref, mod_ref, nf_ref, *rest):
    o_ref = rest[-1]
    o_ref[...] = _rms(acc_ref[...] + mod_ref[5:6, :] * rt_ref[...], nf_ref[...])


FINAL_TILES = 4


def _final(acc, routed, mod, norm_final, part_tile0, local0, n_tiles, out_tile0, out_tiles, prev=None):
    g = FINAL_TILES
    assert all(v % g == 0 for v in (part_tile0, local0, n_tiles, out_tile0, NT_P, TILES_PER_LAT))
    rows = g * TM
    local = pl.BlockSpec((rows, D_MODEL), lambda i: (i + local0 // g, 0))
    prev = () if prev is None else (prev,)
    return pl.pallas_call(
        _final_body,
        out_shape=jax.ShapeDtypeStruct((out_tiles * TM, D_MODEL), F32),
        grid=(n_tiles // g,),
        in_specs=[
            local, local,
            pl.BlockSpec((None, 6, D_MODEL), lambda i: (_mod_row(i * g + local0 + part_tile0), 0, 0)),
            _const_spec((1, D_MODEL)),
        ] + [pl.BlockSpec(memory_space=pl.ANY)] * len(prev),
        out_specs=pl.BlockSpec((rows, D_MODEL), lambda i: (i + out_tile0 // g, 0)),
        input_output_aliases={4 + k: 0 for k in range(len(prev))},
        compiler_params=_params("parallel"),
        name="final_norm",
    )(acc, routed, mod, norm_final.reshape(1, -1), *prev)


def kernel(x_prompt, x_sample, state_hgrn, c, c_ctx, w_ada, b_ada, norm_mix, norm_ffn, w_out, w_in_ab, w_sg, b_sg, norm_sg, w_dw, b_dw, norm_cv, w_in_hgrn, lb_raw, norm_o, w_router, b_router, w_gate, w_up, w_down, w_sh_gate, w_sh_up, w_sh_down, norm_final):
    cvecs = jnp.concatenate(
        [c_ctx.reshape(1, D_MODEL), c, jnp.zeros((N_MOD_ROWS - 1 - DEC_BATCH, D_MODEL), F32)], axis=0)
    mods = _ada_tables(cvecs, w_ada, b_ada)
    lb_sm = jax.nn.softmax(lb_raw.astype(F32), axis=0)
    lb1 = (jnp.cumsum(lb_sm, axis=0) - lb_sm[0])[1].reshape(2, C_HEADS, 1, C_DK)

    def moe(l, xin):
        parts = []
        for p in range(MOE_PARTS):
            h, eidx_t, w8_t, rank_t, counts, acc = _moe_pre(
                xin, mods[l], norm_ffn[l], w_router[l], b_router[l], w_sh_gate[l], w_sh_up[l], w_sh_down[l],
                p * NT_PART)
            wsplat = jnp.repeat(w8_t.T, SC_LANES, axis=1)
            parts += [acc, _moe_routed(h, eidx_t, rank_t, counts, wsplat, w_gate, w_up, w_down, l)]
        return parts

    x = _l0_mixer(x_prompt.reshape(T_P, D_MODEL), x_sample.reshape(T_S, D_MODEL), mods[0], norm_mix[0], w_in_ab[0], w_sg[0], b_sg[0], norm_sg[0], w_dw[0],
                  b_dw[0], norm_cv[0], w_out[0])
    parts = moe(0, x)
    xz = None
    for p in range(MOE_PARTS):
        xz = _hgrn_in(parts[2 * p], parts[2 * p + 1], p * NT_PART, xz, mods[0], mods[1], norm_mix[1], w_in_hgrn[0])
    x, zf, zb = xz
    s0t = jnp.swapaxes(state_hgrn[:, 0].astype(F32), -1, -2)
    o_fw, ns_fw = _gla(zf, zb, lb1[0], s0t[:, 0], rev=False)
    o_bw, ns_bw = _gla(zf, zb, lb1[1], s0t[:, 1], rev=True)
    x = _hgrn_out(o_fw, o_bw, zb, x, mods[1], norm_o[0], w_out[1])
    parts = moe(1, x)
    y_p = y_s = None
    for p in range(MOE_PARTS):
        lo, hi = p * NT_PART, (p + 1) * NT_PART
        if lo < NT_P:
            n = min(hi, NT_P) - lo
            y_p = _final(parts[2 * p], parts[2 * p + 1], mods[1], norm_final, lo, 0, n, lo, NT_P, y_p)
        if hi > NT_P:
            first = max(lo, NT_P)
            y_s = _final(parts[2 * p], parts[2 * p + 1], mods[1], norm_final, lo, first - lo, hi - first,
                         first - NT_P, NT_S, y_s)
    y_p = y_p.reshape(BATCH, SEQ, D_MODEL)
    y_s = y_s.reshape(DEC_BATCH, DEC_SEQ, D_MODEL)
    new_state = jnp.swapaxes(jnp.stack([ns_fw, ns_bw], axis=1), -1, -2)[:, None]
    return (y_p, y_s, new_state)
```

```python
import functools

import jax
import jax.numpy as jnp
from jax import lax
from jax.experimental import pallas as pl
from jax.experimental.pallas import tpu as pltpu
from jax.experimental.pallas import tpu_sc as plsc

F32 = jnp.float32
BF16 = jnp.bfloat16
HIGHEST = lax.Precision.HIGHEST

D_MODEL = 1024
BATCH = 32
SEQ = 256
DEPTH = 2
DEC_BATCH = 8
DEC_SEQ = 2048
GRID_W = 64
A_WIDTH = D_MODEL // 2
A_GROUPS = 4
A_GC = A_WIDTH // A_GROUPS
CHUNK_A = 128
B_WIDTH = D_MODEL - A_WIDTH
CONV_W = 31
CONV_PAD = CONV_W // 2
C_HEADS = 8
C_DK = 128
C_DV = D_MODEL // C_HEADS
C_FDIM = C_HEADS * C_DK
SCAN_CHUNK = 64
N_EXPERTS = 64
TOP_K = 8
N_GROUPS = 8
GROUP_SIZE = N_EXPERTS // N_GROUPS
TOPK_GROUPS = 4
D_EXPERT = 256
D_SHARED = 256
ROUTED_SCALE = 2.5
EPS = 1e-6

TM = 256
T_P = BATCH * SEQ
T_S = DEC_BATCH * DEC_SEQ
T = T_P + T_S
NT_P = T_P // TM
NT_S = T_S // TM
NT = NT_P + NT_S
TILES_PER_LAT = DEC_SEQ // TM
TE = 512
MOE_PARTS = 2
NT_PART = NT // MOE_PARTS
T_PART = NT_PART * TM
NB = T_PART * TOP_K // TE + N_EXPERTS
P_ROWS = NB * TE
D_PACK = D_MODEL // 2
N_MOD_ROWS = 16
CONV_HALO = 16
VMEM_LIMIT = 48 * 1024 * 1024
SC_CORES = 2
SC_SUBCORES = 16
SC_WORKERS = SC_CORES * SC_SUBCORES
SC_LANES = 16
SC_CHUNK = 64
COMBINE_TOKENS = SC_CHUNK // TOP_K


def _mod_row(i):
    return jnp.where(i < NT_P, 0, 1 + (i - NT_P) // TILES_PER_LAT)


def _silu(x):
    return x * jax.nn.sigmoid(x)


def _gelu(x):
    return x * (0.5 * (1.0 + jnp.tanh(0.7978845608028654 * (x + 0.044715 * (x * x * x)))))


def _rms(x, g):
    return x * lax.rsqrt(jnp.mean(x * x, axis=-1, keepdims=True) + EPS) * g


def _layernorm(x, g):
    xc = x - jnp.mean(x, axis=-1, keepdims=True)
    return xc * lax.rsqrt(jnp.mean(xc * xc, axis=-1, keepdims=True) + EPS) * g


def _modulate(x, g, shift, scale):
    return _rms(x, g) * (1.0 + scale) + shift


def _bdot(a, b):
    return jnp.dot(a, b, preferred_element_type=F32)


def _dot_nt(a, b, precision=None):
    return lax.dot_general(a, b, (((1,), (1,)), ((), ())), precision=precision,
                           preferred_element_type=F32)


def _dot_tn(a, b):
    return lax.dot_general(a, b, (((0,), (0,)), ((), ())), preferred_element_type=F32)


def _pack_pairs(x):
    m = x.shape[1] // 2
    lo = lax.bitcast_convert_type(x[:, :m].astype(BF16).astype(F32), jnp.uint32)
    hi = lax.bitcast_convert_type(x[:, m:].astype(BF16).astype(F32), jnp.uint32)
    return lax.bitcast_convert_type(hi | (lo >> 16), jnp.int32)


def _unpack_pairs(w):
    u = lax.bitcast_convert_type(w, jnp.uint32)
    lo = lax.bitcast_convert_type(u << 16, F32)
    hi = lax.bitcast_convert_type(u & jnp.uint32(0xFFFF0000), F32)
    return lo, hi


def _params(*sem):
    return pltpu.CompilerParams(dimension_semantics=sem, vmem_limit_bytes=VMEM_LIMIT)


def _const_spec(shape):
    nd = len(shape)
    return pl.BlockSpec(shape, lambda *_: (0,) * nd)


def _ada_body(c_ref, w_ref, b_ref, o_ref):
    s = _silu(c_ref[...])
    o_ref[...] = jnp.dot(s, w_ref[...], precision=HIGHEST, preferred_element_type=F32) + b_ref[...]


def _ada_tables(cvecs, w_ada, b_ada):
    out = pl.pallas_call(
        _ada_body,
        out_shape=jax.ShapeDtypeStruct((DEPTH, N_MOD_ROWS, 6 * D_MODEL), F32),
        grid=(DEPTH, 6),
        in_specs=[
            _const_spec((N_MOD_ROWS, D_MODEL)),
            pl.BlockSpec((None, D_MODEL, D_MODEL), lambda l, j: (l, 0, j)),
            pl.BlockSpec((None, 1, D_MODEL), lambda l, j: (l, 0, j)),
        ],
        out_specs=pl.BlockSpec((None, N_MOD_ROWS, D_MODEL), lambda l, j: (l, 0, j)),
        compiler_params=_params("parallel", "parallel"),
        name="ada_tables",
    )(cvecs, w_ada, b_ada.reshape(DEPTH, 1, 6 * D_MODEL))
    return out.reshape(DEPTH, N_MOD_ROWS, 6, D_MODEL)


def _mod_spec():
    return pl.BlockSpec((None, 6, D_MODEL), lambda i: (_mod_row(i), 0, 0))


def _tile_spec():
    return pl.BlockSpec((TM, D_MODEL), lambda i: (i, 0))


SUBLANES = 8


def _conv_segment(pad_ref, shift_ref, conv_ref, wdw_ref, pad_base, out_base, seg):
    rb = min(seg, 64)
    for cb in range(B_WIDTH // 128):
        cs = slice(cb * 128, (cb + 1) * 128)
        for r0 in range(0, seg, rb):
            acc = jnp.zeros((rb, 128), F32)
            for k in range(CONV_W):
                b = (CONV_HALO - CONV_PAD + k) % SUBLANES
                off = pad_base + r0 + CONV_HALO - CONV_PAD + k - b
                src = pad_ref if b == 0 else shift_ref.at[b - 1]
                acc = acc + wdw_ref[k:k + 1, cs] * src[off:off + rb, cs]
            conv_ref[out_base + r0:out_base + r0 + rb, cs] = acc


def _l0_body(xc_ref, xl_ref, mod_ref, nmix_ref, win_ref, wsg_ref, bsg_ref, nsg_ref, wdw_ref, bdw_ref,
             ncv_ref, wout_ref, o_ref, cat_ref, pad_ref, shift_ref, conv_ref):
    i = pl.program_id(0)
    x = jnp.where(i < NT_P, xc_ref[...], xl_ref[...])
    h = _modulate(x, nmix_ref[...], mod_ref[0:1, :], mod_ref[1:2, :])
    z = _bdot(h.astype(BF16), win_ref[...])
    u = _gelu(z[:, :A_WIDTH])
    vb = _layernorm(_gelu(z[:, A_WIDTH:2 * A_WIDTH]), nsg_ref[...]).astype(BF16)
    for n in range(TM // CHUNK_A):
        rs = slice(n * CHUNK_A, (n + 1) * CHUNK_A)
        for g in range(A_GROUPS):
            cs = slice(g * A_GC, (g + 1) * A_GC)
            m = _bdot(wsg_ref[g], vb[rs, cs]) + bsg_ref[g]
            cat_ref[rs, cs] = (u[rs, cs] * m).astype(BF16)
    hb = z[:, 2 * A_WIDTH:2 * A_WIDTH + B_WIDTH] * jax.nn.sigmoid(z[:, 2 * A_WIDTH + B_WIDTH:])

    def conv_tile(seg):
        stride = seg + 2 * CONV_HALO
        halo = jnp.zeros((CONV_HALO, B_WIDTH), F32)
        for s in range(TM // seg):
            b = s * stride
            pad_ref[b:b + CONV_HALO, :] = halo
            pad_ref[b + CONV_HALO:b + CONV_HALO + seg, :] = hb[s * seg:(s + 1) * seg, :]
            pad_ref[b + CONV_HALO + seg:b + stride, :] = halo
        rows = (TM // seg) * stride - SUBLANES
        for b in range(1, SUBLANES):
            shift_ref[b - 1, 0:rows, :] = pad_ref[b:b + rows, :]
        for s in range(TM // seg):
            _conv_segment(pad_ref, shift_ref, conv_ref, wdw_ref, s * stride, s * seg, seg)

    @pl.when(i < NT_P)
    def _():
        conv_tile(SEQ)

    @pl.when(i >= NT_P)
    def _():
        conv_tile(GRID_W)

    yb = _layernorm(conv_ref[...] + bdw_ref[...], ncv_ref[...])
    cat_ref[:, A_WIDTH:] = _silu(yb).astype(BF16)
    out = _bdot(cat_ref[...], wout_ref[...])
    o_ref[...] = x + mod_ref[2:3, :] * out


def _l0_mixer(x_ctx, x_lat, mod, norm_mix, w_in, w_sg, b_sg, norm_sg, w_dw, b_dw, norm_cv, w_out):
    assert TM == SEQ and TM % GRID_W == 0 and TM % CHUNK_A == 0
    pad_rows = (TM // GRID_W) * (GRID_W + 2 * CONV_HALO)
    assert pad_rows >= SEQ + 2 * CONV_HALO
    return pl.pallas_call(
        _l0_body,
        out_shape=jax.ShapeDtypeStruct((T, D_MODEL), F32),
        grid=(NT,),
        in_specs=[
            pl.BlockSpec((TM, D_MODEL), lambda i: (jnp.minimum(i, NT_P - 1), 0)),
            pl.BlockSpec((TM, D_MODEL), lambda i: (jnp.maximum(i - NT_P, 0), 0)),
            _mod_spec(),
            _const_spec((1, D_MODEL)),
            _const_spec((D_MODEL, 2 * A_WIDTH + 2 * B_WIDTH)),
            _const_spec((A_GROUPS, CHUNK_A, CHUNK_A)),
            _const_spec((A_GROUPS, CHUNK_A, 1)),
            _const_spec((1, A_WIDTH)),
            _const_spec((CONV_W, B_WIDTH)),
            _const_spec((1, B_WIDTH)),
            _const_spec((1, B_WIDTH)),
            _const_spec((D_MODEL, D_MODEL)),
        ],
        out_specs=_tile_spec(),
        scratch_shapes=[
            pltpu.VMEM((TM, D_MODEL), BF16),
            pltpu.VMEM((pad_rows, B_WIDTH), F32),
            pltpu.VMEM((SUBLANES - 1, pad_rows, B_WIDTH), F32),
            pltpu.VMEM((TM, B_WIDTH), F32),
        ],
        compiler_params=_params("parallel"),
        name="l0_mixer",
    )(x_ctx, x_lat, mod, norm_mix.reshape(1, -1), w_in.astype(BF16), w_sg.astype(BF16),
      b_sg.reshape(A_GROUPS, CHUNK_A, 1), norm_sg.reshape(1, -1), w_dw, b_dw.reshape(1, -1),
      norm_cv.reshape(1, -1), w_out.astype(BF16))


def _route(scores, biased):
    n = scores.shape[-1]
    shp = (N_GROUPS, GROUP_SIZE, n)
    s3 = scores.reshape(shp)
    b3 = biased.reshape(shp)
    m_iota = lax.broadcasted_iota(jnp.int32, shp, 1).astype(F32)
    g_iota = lax.broadcasted_iota(jnp.int32, shp, 0).astype(F32)
    e_iota = g_iota * GROUP_SIZE + m_iota
    neg = -jnp.inf

    def amax1(v):
        return jnp.max(v, axis=1, keepdims=True)

    def amin1(v):
        return jnp.min(v, axis=1, keepdims=True)

    m1 = amax1(b3)
    i1 = amin1(jnp.where(b3 == m1, m_iota, float(GROUP_SIZE)))
    m2 = amax1(jnp.where(m_iota == i1, neg, b3))
    grp = m1 + m2
    gi1 = lax.broadcasted_iota(jnp.int32, grp.shape, 0).astype(F32)
    gmask = jnp.zeros(grp.shape, jnp.bool_)
    for _ in range(TOPK_GROUPS):
        gm = jnp.max(grp, axis=0, keepdims=True)
        gi = jnp.min(jnp.where(grp == gm, gi1, float(N_GROUPS)), axis=0, keepdims=True)
        hit = gi1 == gi
        gmask = jnp.logical_or(gmask, hit)
        grp = jnp.where(hit, neg, grp)
    cand = jnp.where(gmask, b3, neg)
    ids, vals, hits = [], [], []
    for _ in range(TOP_K):
        mx = jnp.max(amax1(cand), axis=0, keepdims=True)
        ei = jnp.min(amin1(jnp.where(cand == mx, e_iota, float(N_EXPERTS))), axis=0, keepdims=True)
        hit = e_iota == ei
        ids.append(ei.reshape(1, n))
        vals.append(_pick(hit, s3))
        hits.append(hit)
        cand = jnp.where(hit, neg, cand)
    return jnp.concatenate(ids, axis=0).astype(jnp.int32), jnp.concatenate(vals, axis=0), hits


def _pick(hit, v3):
    s = jnp.sum(jnp.sum(jnp.where(hit, v3, 0.0), axis=1, keepdims=True), axis=0, keepdims=True)
    return s.reshape(1, v3.shape[-1])


def _moe_pre_body(x_ref, mod_ref, nffn_ref, wrt_ref, br_ref, wsg_ref, wsu_ref, wsd_ref,
                  h_ref, eidx_ref, w8_ref, rank_ref, cnt_ref, acc_ref, run_ref):
    @pl.when(pl.program_id(0) == 0)
    def _():
        run_ref[...] = jnp.zeros(run_ref.shape, F32)

    x = x_ref[...]
    h = _modulate(x, nffn_ref[...], mod_ref[3:4, :], mod_ref[4:5, :])
    h_ref[...] = _pack_pairs(h)
    hb = h.astype(BF16)
    h_lo = (h - hb.astype(F32)).astype(BF16)
    wr = wrt_ref[...]
    wr_hi = wr.astype(BF16)
    wr_lo = (wr - wr_hi.astype(F32)).astype(BF16)
    logits_t = _dot_nt(wr_hi, hb) + (_dot_nt(wr_hi, h_lo) + _dot_nt(wr_lo, hb))
    scores = jax.nn.sigmoid(logits_t)
    eidx, sv, hits = _route(scores, scores + br_ref[...])
    eidx_ref[...] = eidx
    w8_ref[...] = sv / jnp.sum(sv, axis=0, keepdims=True) * ROUTED_SCALE
    sel3 = hits[0]
    for hit in hits[1:]:
        sel3 = jnp.logical_or(sel3, hit)
    sel = sel3.astype(F32).reshape(N_EXPERTS, TM)
    earlier = (lax.broadcasted_iota(jnp.int32, (TM, TM), 0)
               < lax.broadcasted_iota(jnp.int32, (TM, TM), 1)).astype(BF16)
    rank3 = (_bdot(sel.astype(BF16), earlier) + run_ref[...]).reshape(N_GROUPS, GROUP_SIZE, TM)
    rank_ref[...] = jnp.concatenate([_pick(hit, rank3) for hit in hits], axis=0).astype(jnp.int32)
    run_ref[...] = run_ref[...] + jnp.sum(sel, axis=1, keepdims=True)
    cnt_ref[...] = run_ref[...].astype(jnp.int32)
    sh = _bdot((_silu(_bdot(hb, wsg_ref[...])) * _bdot(hb, wsu_ref[...])).astype(BF16), wsd_ref[...])
    acc_ref[...] = x + mod_ref[5:6, :] * sh


def _moe_pre(x, mod, norm_ffn, w_router, b_router, w_sh_gate, w_sh_up, w_sh_down, tile0):
    return pl.pallas_call(
        _moe_pre_body,
        out_shape=(
            jax.ShapeDtypeStruct((T_PART, D_PACK), jnp.int32),
            jax.ShapeDtypeStruct((TOP_K, T_PART), jnp.int32),
            jax.ShapeDtypeStruct((TOP_K, T_PART), F32),
            jax.ShapeDtypeStruct((TOP_K, T_PART), jnp.int32),
            jax.ShapeDtypeStruct((N_EXPERTS, 1), jnp.int32),
            jax.ShapeDtypeStruct((T_PART, D_MODEL), F32),
        ),
        grid=(NT_PART,),
        in_specs=[
            pl.BlockSpec((TM, D_MODEL), lambda i: (i + tile0, 0)),
            pl.BlockSpec((None, 6, D_MODEL), lambda i: (_mod_row(i + tile0), 0, 0)),
            _const_spec((1, D_MODEL)),
            _const_spec((N_EXPERTS, D_MODEL)),
            _const_spec((N_EXPERTS, 1)),
            _const_spec((D_MODEL, D_SHARED)),
            _const_spec((D_MODEL, D_SHARED)),
            _const_spec((D_SHARED, D_MODEL)),
        ],
        out_specs=(
            pl.BlockSpec((TM, D_PACK), lambda i: (i, 0)),
            pl.BlockSpec((TOP_K, TM), lambda i: (0, i)),
            pl.BlockSpec((TOP_K, TM), lambda i: (0, i)),
            pl.BlockSpec((TOP_K, TM), lambda i: (0, i)),
            _const_spec((N_EXPERTS, 1)),
            _tile_spec(),
        ),
        scratch_shapes=[pltpu.VMEM((N_EXPERTS, 1), F32)],
        compiler_params=_params("arbitrary"),
        name="moe_router_shared",
    )(x, mod, norm_ffn.reshape(1, -1), w_router.T, b_router.reshape(N_EXPERTS, 1),
      w_sh_gate.astype(BF16), w_sh_up.astype(BF16), w_sh_down.astype(BF16))


EXPERT_RING = 4


WEIGHT_SLOTS = 2


EXPERT_TILES_PER_STEP = 8


def _expert_body(be_ref, nv_ref, nu_ref, run_ref, rexp_ref, nrun_ref, xs_hbm, wg_hbm, wu_hbm, wd_hbm, ys_ref,
                 wgu_s, wd_s, ring, sems, wg_buf, wu_buf, wd_buf, wsems, *, layer):
    n_used = nu_ref[0]
    n_runs = nrun_ref[0]

    def fetch(t):
        slot = t % EXPERT_RING
        return pltpu.make_async_copy(xs_hbm.at[pl.ds(t * TE, TE)], ring.at[slot], sems.at[slot])

    def wfetch(r):
        slot = r % WEIGHT_SLOTS
        e = rexp_ref[r]
        return [pltpu.make_async_copy(wg_hbm.at[layer, e], wg_buf.at[slot], wsems.at[slot, 0]),
                pltpu.make_async_copy(wu_hbm.at[layer, e], wu_buf.at[slot], wsems.at[slot, 1]),
                pltpu.make_async_copy(wd_hbm.at[layer, e], wd_buf.at[slot], wsems.at[slot, 2])]

    def tile_step(j, out_rows):
        live_tile = j < n_used

        @pl.when(j == 0)
        def _():
            for t in range(EXPERT_RING - 1):
                @pl.when(t < n_used)
                def _():
                    fetch(t).start()
            for r in range(WEIGHT_SLOTS):
                @pl.when(r < n_runs)
                def _():
                    for cp in wfetch(r):
                        cp.start()

        @pl.when(j + (EXPERT_RING - 1) < n_used)
        def _():
            fetch(j + (EXPERT_RING - 1)).start()

        @pl.when(jnp.logical_and(live_tile, jnp.logical_or(j == 0, be_ref[j] != be_ref[jnp.maximum(j - 1, 0)])))
        def _():
            r = run_ref[j]
            slot = r % WEIGHT_SLOTS
            for cp in wfetch(r):
                cp.wait()
            wgu_s[:, :D_EXPERT] = wg_buf[slot].astype(BF16)
            wgu_s[:, D_EXPERT:] = wu_buf[slot].astype(BF16)
            wd_s[...] = wd_buf[slot].astype(BF16)

            @pl.when(r + WEIGHT_SLOTS < n_runs)
            def _():
                for cp in wfetch(r + WEIGHT_SLOTS):
                    cp.start()

        @pl.when(live_tile)
        def _():
            fetch(j).wait()
            live = lax.broadcasted_iota(jnp.int32, (TE, 1), 0) < nv_ref[j]
            lo, hi = _unpack_pairs(jnp.where(live, ring[j % EXPERT_RING], 0))
            xb = jnp.concatenate([lo.astype(BF16), hi.astype(BF16)], axis=1)
            hgu = _bdot(xb, wgu_s[...])
            hh = _silu(hgu[:, :D_EXPERT]) * hgu[:, D_EXPERT:]
            ys_ref[out_rows, :] = _pack_pairs(_bdot(hh.astype(BF16), wd_s[...]))

    for u in range(EXPERT_TILES_PER_STEP):
        tile_step(pl.program_id(0) * EXPERT_TILES_PER_STEP + u, pl.ds(u * TE, TE))


def _experts(block_expert, block_rows, n_used, block_run, run_expert, n_runs, xs, w_gate, w_up, w_down, layer):
    g = EXPERT_TILES_PER_STEP
    assert NB % g == 0

    def row_map(s, be, nv, nu, run, rexp, nrun):
        return (jnp.minimum(s, (nu[0] - 1) // g), 0)

    any_spec = pl.BlockSpec(memory_space=pl.ANY)
    return pl.pallas_call(
        functools.partial(_expert_body, layer=layer),
        out_shape=jax.ShapeDtypeStruct((P_ROWS, D_PACK), jnp.int32),
        grid_spec=pltpu.PrefetchScalarGridSpec(
            num_scalar_prefetch=6,
            grid=(NB // g,),
            in_specs=[any_spec, any_spec, any_spec, any_spec],
            out_specs=pl.BlockSpec((g * TE, D_PACK), row_map),
            scratch_shapes=[
                pltpu.VMEM((D_MODEL, 2 * D_EXPERT), BF16),
                pltpu.VMEM((D_EXPERT, D_MODEL), BF16),
                pltpu.VMEM((EXPERT_RING, TE, D_PACK), jnp.int32),
                pltpu.SemaphoreType.DMA((EXPERT_RING,)),
                pltpu.VMEM((WEIGHT_SLOTS, D_MODEL, D_EXPERT), F32),
                pltpu.VMEM((WEIGHT_SLOTS, D_MODEL, D_EXPERT), F32),
                pltpu.VMEM((WEIGHT_SLOTS, D_EXPERT, D_MODEL), F32),
                pltpu.SemaphoreType.DMA((WEIGHT_SLOTS, 3)),
            ],
        ),
        compiler_params=_params("arbitrary"),
        name="moe_experts",
    )(block_expert, block_rows, n_used, block_run, run_expert, n_runs, xs, w_gate, w_up, w_down)


def _positions_body(start_ref, eidx_ref, rank_ref, pos_ref):
    eidx = eidx_ref[...]
    base = jnp.zeros(eidx.shape, jnp.int32)
    for e in range(N_EXPERTS):
        base = jnp.where(eidx == e, start_ref[e], base)
    pos_ref[...] = base * TE + rank_ref[...]


def _positions(blk_start, eidx_t, rank_t):
    full = pl.BlockSpec((TOP_K, T_PART), lambda i, s: (0, 0))
    return pl.pallas_call(
        _positions_body,
        out_shape=jax.ShapeDtypeStruct((TOP_K, T_PART), jnp.int32),
        grid_spec=pltpu.PrefetchScalarGridSpec(
            num_scalar_prefetch=1, grid=(1,), in_specs=[full, full], out_specs=full),
        compiler_params=_params("arbitrary"),
        name="moe_positions",
    )(blk_start, eidx_t, rank_t)


def _sc_mesh():
    return plsc.VectorSubcoreMesh(core_axis_name="c", subcore_axis_name="s")


def _sc_worker():
    return lax.axis_index("s") * SC_CORES + lax.axis_index("c")


def _sc_scatter_rows(h, pos_rows):
    c = SC_CHUNK
    n_chunks = T_PART // SC_WORKERS // c
    width = h.shape[1]

    @functools.partial(
        pl.kernel, mesh=_sc_mesh(),
        out_type=jax.ShapeDtypeStruct((P_ROWS, width), h.dtype),
        scratch_types=[pltpu.VMEM((n_chunks * TOP_K, c), jnp.int32),
                       pltpu.VMEM((c, width), h.dtype), pltpu.VMEM((c, width), h.dtype)]
        + [pltpu.SemaphoreType.DMA] * 4,
        name="moe_dispatch_scatter",
    )
    def scatter(h_hbm, pos_hbm, xs_hbm, idx_v, buf0, buf1, sem_in0, sem_in1, sem_out0, sem_out1):
        assert n_chunks % 2 == 0
        bufs, sem_in, sem_out = (buf0, buf1), (sem_in0, sem_in1), (sem_out0, sem_out1)
        first = _sc_worker() * n_chunks
        pltpu.sync_copy(pos_hbm.at[pl.ds(first * TOP_K, n_chunks * TOP_K)], idx_v)

        def load(i, b):
            return pltpu.make_async_copy(h_hbm.at[pl.ds((first + i) * c, c)], bufs[b], sem_in[b])

        def puts(i, b):
            return [pltpu.make_async_copy(bufs[b], xs_hbm.at[idx_v.at[i * TOP_K + k]], sem_out[b])
                    for k in range(TOP_K)]

        load(0, 0).start()
        load(1, 1).start()

        @pl.loop(0, n_chunks, step=2)
        def _(i):
            for b in range(2):
                load(i + b, b).wait()
                for cp in puts(i + b, b):
                    cp.start()
            for b in range(2):
                for cp in puts(i + b, b):
                    cp.wait()

                @pl.when(i + 2 + b < n_chunks)
                def _():
                    load(i + 2 + b, b).start()

    return scatter(h, pos_rows)


def _sc_combine(ys, pos_tk, wsplat):
    ct = COMBINE_TOKENS
    rows = ct * TOP_K
    tok_per_worker = T_PART // SC_WORKERS
    n_chunks = tok_per_worker // ct
    width = ys.shape[1]
    vmem = pltpu.VMEM

    @functools.partial(
        pl.kernel, mesh=_sc_mesh(),
        out_type=jax.ShapeDtypeStruct((T_PART, D_MODEL), F32),
        scratch_types=[vmem((tok_per_worker * TOP_K,), jnp.int32),
                       vmem((rows, width), ys.dtype), vmem((rows, width), ys.dtype),
                       vmem((ct, TOP_K * SC_LANES), F32), vmem((ct, TOP_K * SC_LANES), F32),
                       vmem((ct, D_MODEL), F32), vmem((ct, D_MODEL), F32)]
        + [pltpu.SemaphoreType.DMA] * 6,
        compiler_params=pltpu.CompilerParams(needs_layout_passes=False),
        name="moe_combine",
    )
    def combine(ys_hbm, pos_hbm, w_hbm, out_hbm, idx_v, g0, g1, w0, w1, o0, o1,
                sem_g0, sem_g1, sem_w0, sem_w1, sem_o0, sem_o1):
        assert n_chunks % 2 == 0
        gbuf, wbuf, obuf = (g0, g1), (w0, w1), (o0, o1)
        sem_g, sem_w, sem_o = (sem_g0, sem_g1), (sem_w0, sem_w1), (sem_o0, sem_o1)
        tok0 = _sc_worker() * tok_per_worker
        pltpu.sync_copy(pos_hbm.at[pl.ds(tok0 * TOP_K, tok_per_worker * TOP_K)], idx_v)

        def fetch(i, b):
            return [pltpu.make_async_copy(ys_hbm.at[idx_v.at[pl.ds(i * rows, rows)]], gbuf[b], sem_g[b]),
                    pltpu.make_async_copy(w_hbm.at[pl.ds(tok0 + i * ct, ct)], wbuf[b], sem_w[b])]

        def flush(i, b):
            return pltpu.make_async_copy(obuf[b], out_hbm.at[pl.ds(tok0 + i * ct, ct)], sem_o[b])

        def reduce_chunk(b):
            @pl.loop(0, ct)
            def _(t):
                wv = [wbuf[b][t, pl.ds(k * SC_LANES, SC_LANES)] for k in range(TOP_K)]

                @pl.loop(0, width // SC_LANES)
                def _(j):
                    col = j * SC_LANES
                    acc_lo = acc_hi = None
                    for k in range(TOP_K):
                        word = gbuf[b][t * TOP_K + k, pl.ds(col, SC_LANES)]
                        lo = lax.bitcast_convert_type(word << 16, F32)
                        hi = lax.bitcast_convert_type(word & jnp.int32(-65536), F32)
                        acc_lo = wv[k] * lo if k == 0 else acc_lo + wv[k] * lo
                        acc_hi = wv[k] * hi if k == 0 else acc_hi + wv[k] * hi
                    obuf[b][t, pl.ds(col, SC_LANES)] = acc_lo
                    obuf[b][t, pl.ds(D_PACK + col, SC_LANES)] = acc_hi

        for b in range(2):
            for cp in fetch(b, b):
                cp.start()

        @pl.loop(0, n_chunks, step=2)
        def _(i):
            for b in range(2):
                for cp in fetch(i + b, b):
                    cp.wait()

                @pl.when(i > 0)
                def _():
                    flush(i + b - 2, b).wait()

                reduce_chunk(b)
                flush(i + b, b).start()

                @pl.when(i + 2 + b < n_chunks)
                def _():
                    for cp in fetch(i + 2 + b, b):
                        cp.start()

        for b in range(2):
            flush(n_chunks - 2 + b, b).wait()

    return combine(ys, pos_tk, wsplat)


def _moe_routed(h, eidx_t, rank_t, counts, wsplat, w_gate, w_up, w_down, layer):
    counts = counts.reshape(1, N_EXPERTS)
    nblk = (counts + TE - 1) // TE
    blk_end = jnp.cumsum(nblk, axis=1)
    blk_start = blk_end - nblk
    blocks = jnp.arange(NB, dtype=jnp.int32).reshape(NB, 1)
    block_expert = jnp.minimum(jnp.sum(blocks >= blk_end, axis=1, keepdims=True), N_EXPERTS - 1)
    mine = block_expert == jnp.arange(N_EXPERTS, dtype=jnp.int32).reshape(1, N_EXPERTS)
    cnt_b = jnp.sum(jnp.where(mine, counts, 0), axis=1, keepdims=True)
    start_b = jnp.sum(jnp.where(mine, blk_start, 0), axis=1, keepdims=True)
    block_rows = jnp.clip(cnt_b - (blocks - start_b) * TE, 0, TE)
    n_used = blk_end[0, -1].reshape(1).astype(jnp.int32)
    present = (nblk > 0).astype(jnp.int32)
    run_of_expert = jnp.cumsum(present, axis=1) - 1
    block_run = jnp.sum(jnp.where(mine, run_of_expert, 0), axis=1)
    experts = jnp.arange(N_EXPERTS, dtype=jnp.int32).reshape(1, N_EXPERTS)
    run_hit = jnp.logical_and(run_of_expert == experts.reshape(N_EXPERTS, 1), present > 0)
    run_expert = jnp.sum(jnp.where(run_hit, experts, 0), axis=1)
    n_runs = jnp.sum(present).reshape(1)
    pos = _positions(blk_start.reshape(N_EXPERTS).astype(jnp.int32), eidx_t, rank_t)
    pos_rows = pos.reshape(TOP_K, T_PART // SC_CHUNK, SC_CHUNK).transpose(1, 0, 2).reshape(-1, SC_CHUNK)
    xs = _sc_scatter_rows(h, pos_rows)
    ys = _experts(block_expert.reshape(NB).astype(jnp.int32), block_rows.reshape(NB).astype(jnp.int32),
                  n_used, block_run.astype(jnp.int32), run_expert.astype(jnp.int32), n_runs.astype(jnp.int32),
                  xs, w_gate, w_up, w_down, layer)
    return _sc_combine(ys, pos.T.reshape(-1), wsplat)


N_SLABS = (3 * C_FDIM + 2 * D_MODEL) // 128


F32_GROUPS = (1, 2)
B16_GROUPS = (0, 3, 4)


def _hgrn_in_body(acc_ref, rt_ref, mod0_ref, mod1_ref, nmix_ref, win_ref, *rest):
    x_ref, zf_ref, zb_ref = rest[-3:]
    x = acc_ref[...] + mod0_ref[5:6, :] * rt_ref[...]
    x_ref[...] = x
    hb = _modulate(x, nmix_ref[...], mod1_ref[0:1, :], mod1_ref[1:2, :]).astype(BF16)
    for s in range(N_SLABS // C_HEADS):
        zz = _bdot(hb, win_ref[:, s * D_MODEL:(s + 1) * D_MODEL])
        for hh in range(C_HEADS):
            if s in F32_GROUPS:
                zf_ref[F32_GROUPS.index(s) * C_HEADS + hh] = zz[:, hh * 128:(hh + 1) * 128]
            else:
                zb_ref[B16_GROUPS.index(s) * C_HEADS + hh] = zz[:, hh * 128:(hh + 1) * 128].astype(BF16)


HGRN_IN_TILES = 2


def _hgrn_in(acc, routed, tile0, prev, mod0, mod1, norm_mix, w_in):
    g = HGRN_IN_TILES
    assert NT_PART % g == 0 and tile0 % g == 0 and NT_P % g == 0 and TILES_PER_LAT % g == 0
    rows = g * TM
    block0 = tile0 // g

    def shifted_mod():
        return pl.BlockSpec((None, 6, D_MODEL), lambda i: (_mod_row(i * g + tile0), 0, 0))

    local = pl.BlockSpec((rows, D_MODEL), lambda i: (i, 0))
    any_spec = pl.BlockSpec(memory_space=pl.ANY)
    prev = () if prev is None else tuple(prev)
    n_in = 6
    return pl.pallas_call(
        _hgrn_in_body,
        out_shape=(
            jax.ShapeDtypeStruct((T, D_MODEL), F32),
            jax.ShapeDtypeStruct((len(F32_GROUPS) * C_HEADS, T, 128), F32),
            jax.ShapeDtypeStruct((len(B16_GROUPS) * C_HEADS, T, 128), BF16),
        ),
        grid=(NT_PART // g,),
        in_specs=[
            local, local, shifted_mod(), shifted_mod(),
            _const_spec((1, D_MODEL)),
            pl.BlockSpec((D_MODEL, 3 * C_FDIM + 2 * D_MODEL), lambda i: (0, 0), pipeline_mode=pl.Buffered(1)),
        ] + [any_spec] * len(prev),
        out_specs=(
            pl.BlockSpec((rows, D_MODEL), lambda i: (i + block0, 0)),
            pl.BlockSpec((len(F32_GROUPS) * C_HEADS, rows, 128), lambda i: (0, i + block0, 0)),
            pl.BlockSpec((len(B16_GROUPS) * C_HEADS, rows, 128), lambda i: (0, i + block0, 0)),
        ),
        input_output_aliases={n_in + k: k for k in range(len(prev))},
        compiler_params=_params("parallel"),
        name="hgrn_in_proj",
    )(acc, routed, mod0, mod1, norm_mix.reshape(1, -1), w_in.astype(BF16), *prev)


GLA_TILES = 4


def _gla_body(q_ref, f_ref, v_ref, lb_ref, s0_ref, o_ref, ns_ref, st_ref, *, rev):
    blk = NT // GLA_TILES - 1 - pl.program_id(0) if rev else pl.program_id(0)
    row = lax.broadcasted_iota(jnp.int32, (TM, TM), 0)
    col = lax.broadcasted_iota(jnp.int32, (TM, TM), 1)
    same_chunk = (row // SCAN_CHUNK) == (col // SCAN_CHUNK)
    seen = jnp.logical_and(same_chunk, (col >= row) if rev else (col <= row))
    cum_w = seen.astype(BF16)
    mid = SCAN_CHUNK // 2 if rev else SCAN_CHUNK // 2 - 1
    last = 0 if rev else SCAN_CHUNK - 1
    n_chunks = TM // SCAN_CHUNK
    order = range(n_chunks - 1, -1, -1) if rev else range(n_chunks)
    group = 4

    def chunk_rows(b, off):
        return jnp.concatenate(
            [jnp.broadcast_to(b[c * SCAN_CHUNK + off:c * SCAN_CHUNK + off + 1, :], (SCAN_CHUNK, b.shape[1]))
             for c in range(n_chunks)], axis=0)

    def tile_step(sub):
        ti = blk * GLA_TILES + sub
        r0 = sub * TM
        is_ctx = ti < NT_P
        first_lat = (ti - NT_P) % TILES_PER_LAT == (TILES_PER_LAT - 1 if rev else 0)

        @pl.when(is_ctx)
        def _():
            st_ref[...] = jnp.zeros(st_ref.shape, F32)

        @pl.when(jnp.logical_and(jnp.logical_not(is_ctx), first_lat))
        def _():
            st_ref[...] = s0_ref[...]

        def head_group(gi, carry):
            heads = [gi * group + u for u in range(group)]
            qs, kk, vv, bcum = [], [], [], []
            for hd in heads:
                lb = lb_ref[hd]
                qs.append(_silu(q_ref[hd, pl.ds(r0, TM), :].astype(F32)) * (C_DK ** -0.5))
                fg = lb + (1.0 - lb) * jax.nn.sigmoid(f_ref[hd, pl.ds(r0, TM), :])
                kk.append(1.0 - fg)
                vv.append(v_ref[hd, pl.ds(r0, TM), :])
                g = jnp.log(fg)
                g_hi = g.astype(BF16)
                r1 = g - g_hi.astype(F32)
                g_mid = r1.astype(BF16)
                g_lo = (r1 - g_mid.astype(F32)).astype(BF16)
                bcum.append(_bdot(cum_w, g_hi) + _bdot(cum_w, g_mid) + _bdot(cum_w, g_lo))
            o_intra, q_dec, kv, decay = [], [], [], []
            for u in range(group):
                b_mid = chunk_rows(bcum[u], mid)
                b_last = chunk_rows(bcum[u], last)
                qe = (qs[u] * jnp.exp(bcum[u] - b_mid)).astype(BF16)
                ke = (kk[u] * jnp.exp(b_mid - bcum[u])).astype(BF16)
                att = jnp.where(seen, _dot_nt(qe, ke), 0.0)
                o_intra.append(_bdot(att.astype(BF16), vv[u]))
                q_dec.append((qs[u] * jnp.exp(bcum[u])).astype(BF16))
                k_dec = (kk[u] * jnp.exp(b_last - bcum[u])).astype(BF16)
                kv.append([_dot_tn(vv[u][c * SCAN_CHUNK:(c + 1) * SCAN_CHUNK], k_dec[c * SCAN_CHUNK:(c + 1) * SCAN_CHUNK])
                           for c in range(n_chunks)])
                decay.append([jnp.exp(bcum[u][c * SCAN_CHUNK + last:c * SCAN_CHUNK + last + 1, :])
                              for c in range(n_chunks)])
            st = [st_ref[hd] for hd in heads]
            for c in order:
                sl = slice(c * SCAN_CHUNK, (c + 1) * SCAN_CHUNK)
                for u, hd in enumerate(heads):
                    o_ref[hd, pl.ds(r0 + c * SCAN_CHUNK, SCAN_CHUNK), :] = (
                        o_intra[u][sl] + _dot_nt(q_dec[u][sl], st[u].astype(BF16))).astype(BF16)
                    st[u] = decay[u][c] * st[u] + kv[u][c]
            for u, hd in enumerate(heads):
                st_ref[hd] = st[u]
            return carry

        lax.fori_loop(0, C_HEADS // group, head_group, 0)

        @pl.when(is_ctx)
        def _():
            ns_ref[sub] = st_ref[...]

    for sub in (reversed(range(GLA_TILES)) if rev else range(GLA_TILES)):
        tile_step(sub)


def _gla(zf, zb, lb_dir, s0t_dir, *, rev):
    g = GLA_TILES
    assert NT % g == 0 and NT_P % g == 0 and TILES_PER_LAT % g == 0
    rows = g * TM

    def blk_of(j):
        return NT // g - 1 - j if rev else j

    f_slab = F32_GROUPS.index(2 if rev else 1)

    def lat_map(j):
        return (jnp.clip((blk_of(j) * g - NT_P) // TILES_PER_LAT, 0, DEC_BATCH - 1), 0, 0, 0)

    return pl.pallas_call(
        functools.partial(_gla_body, rev=rev),
        out_shape=(
            jax.ShapeDtypeStruct((C_HEADS, T, C_DV), BF16),
            jax.ShapeDtypeStruct((BATCH, C_HEADS, C_DV, C_DK), F32),
        ),
        grid=(NT // g,),
        in_specs=[
            pl.BlockSpec((C_HEADS, rows, 128), lambda j: (B16_GROUPS.index(0), blk_of(j), 0)),
            pl.BlockSpec((C_HEADS, rows, 128), lambda j: (f_slab, blk_of(j), 0)),
            pl.BlockSpec((C_HEADS, rows, 128), lambda j: (B16_GROUPS.index(3), blk_of(j), 0)),
            _const_spec((C_HEADS, 1, C_DK)),
            pl.BlockSpec((None, C_HEADS, C_DV, C_DK), lat_map),
        ],
        out_specs=(
            pl.BlockSpec((C_HEADS, rows, C_DV), lambda j: (0, blk_of(j), 0)),
            pl.BlockSpec((g, C_HEADS, C_DV, C_DK),
                         lambda j: (jnp.minimum(blk_of(j), NT_P // g - 1), 0, 0, 0)),
        ),
        scratch_shapes=[pltpu.VMEM((C_HEADS, C_DV, C_DK), F32)],
        compiler_params=_params("arbitrary"),
        name="gla_bwd" if rev else "gla_fwd",
    )(zb, zf, zb, lb_dir, s0t_dir)


def _hgrn_out_body(ofw_ref, obw_ref, gate_ref, x_ref, mod_ref, no_ref, wout_ref, o_ref, cat_ref):
    for hd in range(C_HEADS):
        o = ofw_ref[hd].astype(F32) + obw_ref[hd].astype(F32)
        gate = gate_ref[hd].astype(F32)
        cat_ref[:, hd * C_DV:(hd + 1) * C_DV] = (_rms(o, no_ref[...]) * _silu(gate)).astype(BF16)
    o_ref[...] = x_ref[...] + mod_ref[2:3, :] * _bdot(cat_ref[...], wout_ref[...])


HGRN_OUT_TILES = 4


def _hgrn_out(o_fw, o_bw, zb, x, mod, norm_o, w_out):
    g = HGRN_OUT_TILES
    assert NT % g == 0 and NT_P % g == 0 and TILES_PER_LAT % g == 0
    rows = g * TM
    head_spec = pl.BlockSpec((C_HEADS, rows, C_DV), lambda i: (0, i, 0))
    row_spec = pl.BlockSpec((rows, D_MODEL), lambda i: (i, 0))
    return pl.pallas_call(
        _hgrn_out_body,
        out_shape=jax.ShapeDtypeStruct((T, D_MODEL), F32),
        grid=(NT // g,),
        in_specs=[
            head_spec, head_spec,
            pl.BlockSpec((C_HEADS, rows, 128), lambda i: (B16_GROUPS.index(4), i, 0)),
            row_spec,
            pl.BlockSpec((None, 6, D_MODEL), lambda i: (_mod_row(i * g), 0, 0)),
            _const_spec((1, C_DV)),
            _const_spec((D_MODEL, D_MODEL)),
        ],
        out_specs=row_spec,
        scratch_shapes=[pltpu.VMEM((rows, D_MODEL), BF16)],
        compiler_params=_params("parallel"),
        name="hgrn_out_proj",
    )(o_fw, o_bw, zb, x, mod, norm_o.reshape(1, -1), w_out.astype(BF16))


def _final_body(acc_ref, rt_ref, mod_ref, nf_ref, *rest):
    o_ref = rest[-1]
    o_ref[...] = _rms(acc_ref[...] + mod_ref[5:6, :] * rt_ref[...], nf_ref[...])


FINAL_TILES = 4


def _final(acc, routed, mod, norm_final, part_tile0, local0, n_tiles, out_tile0, out_tiles, prev=None):
    g = FINAL_TILES
    assert all(v % g == 0 for v in (part_tile0, local0, n_tiles, out_tile0, NT_P, TILES_PER_LAT))
    rows = g * TM
    local = pl.BlockSpec((rows, D_MODEL), lambda i: (i + local0 // g, 0))
    prev = () if prev is None else (prev,)
    return pl.pallas_call(
        _final_body,
        out_shape=jax.ShapeDtypeStruct((out_tiles * TM, D_MODEL), F32),
        grid=(n_tiles // g,),
        in_specs=[
            local, local,
            pl.BlockSpec((None, 6, D_MODEL), lambda i: (_mod_row(i * g + local0 + part_tile0), 0, 0)),
            _const_spec((1, D_MODEL)),
        ] + [pl.BlockSpec(memory_space=pl.ANY)] * len(prev),
        out_specs=pl.BlockSpec((rows, D_MODEL), lambda i: (i + out_tile0 // g, 0)),
        input_output_aliases={4 + k: 0 for k in range(len(prev))},
        compiler_params=_params("parallel"),
        name="final_norm",
    )(acc, routed, mod, norm_final.reshape(1, -1), *prev)


def kernel(x_prompt, x_sample, state_hgrn, c, c_ctx, w_ada, b_ada, norm_mix, norm_ffn, w_out, w_in_ab, w_sg, b_sg, norm_sg, w_dw, b_dw, norm_cv, w_in_hgrn, lb_raw, norm_o, w_router, b_router, w_gate, w_up, w_down, w_sh_gate, w_sh_up, w_sh_down, norm_final):
    cvecs = jnp.concatenate(
        [c_ctx.reshape(1, D_MODEL), c, jnp.zeros((N_MOD_ROWS - 1 - DEC_BATCH, D_MODEL), F32)], axis=0)
    mods = _ada_tables(cvecs, w_ada, b_ada)
    lb_sm = jax.nn.softmax(lb_raw.astype(F32), axis=0)
    lb1 = (jnp.cumsum(lb_sm, axis=0) - lb_sm[0])[1].reshape(2, C_HEADS, 1, C_DK)

    def moe(l, xin):
        parts = []
        for p in range(MOE_PARTS):
            h, eidx_t, w8_t, rank_t, counts, acc = _moe_pre(
                xin, mods[l], norm_ffn[l], w_router[l], b_router[l], w_sh_gate[l], w_sh_up[l], w_sh_down[l],
                p * NT_PART)
            wsplat = jnp.repeat(w8_t.T, SC_LANES, axis=1)
            parts += [acc, _moe_routed(h, eidx_t, rank_t, counts, wsplat, w_gate, w_up, w_down, l)]
        return parts

    x = _l0_mixer(x_prompt.reshape(T_P, D_MODEL), x_sample.reshape(T_S, D_MODEL), mods[0], norm_mix[0], w_in_ab[0], w_sg[0], b_sg[0], norm_sg[0], w_dw[0],
                  b_dw[0], norm_cv[0], w_out[0])
    parts = moe(0, x)
    xz = None
    for p in range(MOE_PARTS):
        xz = _hgrn_in(parts[2 * p], parts[2 * p + 1], p * NT_PART, xz, mods[0], mods[1], norm_mix[1], w_in_hgrn[0])
    x, zf, zb = xz
    s0t = jnp.swapaxes(state_hgrn[:, 0].astype(F32), -1, -2)
    o_fw, ns_fw = _gla(zf, zb, lb1[0], s0t[:, 0], rev=False)
    o_bw, ns_bw = _gla(zf, zb, lb1[1], s0t[:, 1], rev=True)
    x = _hgrn_out(o_fw, o_bw, zb, x, mods[1], norm_o[0], w_out[1])
    parts = moe(1, x)
    y_p = y_s = None
    for p in range(MOE_PARTS):
        lo, hi = p * NT_PART, (p + 1) * NT_PART
        if lo < NT_P:
            n = min(hi, NT_P) - lo
            y_p = _final(parts[2 * p], parts[2 * p + 1], mods[1], norm_final, lo, 0, n, lo, NT_P, y_p)
        if hi > NT_P:
            first = max(lo, NT_P)
            y_s = _final(parts[2 * p], parts[2 * p + 1], mods[1], norm_final, lo, first - lo, hi - first,
                         first - NT_P, NT_S, y_s)
    y_p = y_p.reshape(BATCH, SEQ, D_MODEL)
    y_s = y_s.reshape(DEC_BATCH, DEC_SEQ, D_MODEL)
    new_state = jnp.swapaxes(jnp.stack([ns_fw, ns_bw], axis=1), -1, -2)[:, None]
    return (y_p, y_s, new_state)
```

```python
import functools

import jax
import jax.numpy as jnp
from jax import lax
from jax.experimental import pallas as pl
from jax.experimental.pallas import tpu as pltpu
from jax.experimental.pallas import tpu_sc as plsc

F32 = jnp.float32
BF16 = jnp.bfloat16
HIGHEST = lax.Precision.HIGHEST

D_MODEL = 1024
BATCH = 32
SEQ = 256
DEPTH = 2
DEC_BATCH = 8
DEC_SEQ = 2048
GRID_W = 64
A_WIDTH = D_MODEL // 2
A_GROUPS = 4
A_GC = A_WIDTH // A_GROUPS
CHUNK_A = 128
B_WIDTH = D_MODEL - A_WIDTH
CONV_W = 31
CONV_PAD = CONV_W // 2
C_HEADS = 8
C_DK = 128
C_DV = D_MODEL // C_HEADS
C_FDIM = C_HEADS * C_DK
SCAN_CHUNK = 64
N_EXPERTS = 64
TOP_K = 8
N_GROUPS = 8
GROUP_SIZE = N_EXPERTS // N_GROUPS
TOPK_GROUPS = 4
D_EXPERT = 256
D_SHARED = 256
ROUTED_SCALE = 2.5
EPS = 1e-6

TM = 256
T_P = BATCH * SEQ
T_S = DEC_BATCH * DEC_SEQ
T = T_P + T_S
NT_P = T_P // TM
NT_S = T_S // TM
NT = NT_P + NT_S
TILES_PER_LAT = DEC_SEQ // TM
TE = 512
MOE_PARTS = 2
NT_PART = NT // MOE_PARTS
T_PART = NT_PART * TM
NB = T_PART * TOP_K // TE + N_EXPERTS
P_ROWS = NB * TE
D_PACK = D_MODEL // 2
N_MOD_ROWS = 16
CONV_HALO = 16
VMEM_LIMIT = 48 * 1024 * 1024
SC_CORES = 2
SC_SUBCORES = 16
SC_WORKERS = SC_CORES * SC_SUBCORES
SC_LANES = 16
SC_CHUNK = 64
COMBINE_TOKENS = SC_CHUNK // TOP_K


def _mod_row(i):
    return jnp.where(i < NT_P, 0, 1 + (i - NT_P) // TILES_PER_LAT)


def _silu(x):
    return x * jax.nn.sigmoid(x)


def _gelu(x):
    return x * (0.5 * (1.0 + jnp.tanh(0.7978845608028654 * (x + 0.044715 * (x * x * x)))))


def _rms(x, g):
    return x * lax.rsqrt(jnp.mean(x * x, axis=-1, keepdims=True) + EPS) * g


def _layernorm(x, g):
    xc = x - jnp.mean(x, axis=-1, keepdims=True)
    return xc * lax.rsqrt(jnp.mean(xc * xc, axis=-1, keepdims=True) + EPS) * g


def _modulate(x, g, shift, scale):
    return _rms(x, g) * (1.0 + scale) + shift


def _bdot(a, b):
    return jnp.dot(a, b, preferred_element_type=F32)


def _dot_nt(a, b, precision=None):
    return lax.dot_general(a, b, (((1,), (1,)), ((), ())), precision=precision,
                           preferred_element_type=F32)


def _dot_tn(a, b):
    return lax.dot_general(a, b, (((0,), (0,)), ((), ())), preferred_element_type=F32)


def _pack_pairs(x):
    m = x.shape[1] // 2
    lo = lax.bitcast_convert_type(x[:, :m].astype(BF16).astype(F32), jnp.uint32)
    hi = lax.bitcast_convert_type(x[:, m:].astype(BF16).astype(F32), jnp.uint32)
    return lax.bitcast_convert_type(hi | (lo >> 16), jnp.int32)


def _unpack_pairs(w):
    u = lax.bitcast_convert_type(w, jnp.uint32)
    lo = lax.bitcast_convert_type(u << 16, F32)
    hi = lax.bitcast_convert_type(u & jnp.uint32(0xFFFF0000), F32)
    return lo, hi


def _params(*sem):
    return pltpu.CompilerParams(dimension_semantics=sem, vmem_limit_bytes=VMEM_LIMIT)


def _const_spec(shape):
    nd = len(shape)
    return pl.BlockSpec(shape, lambda *_: (0,) * nd)


def _ada_body(c_ref, w_ref, b_ref, o_ref):
    s = _silu(c_ref[...])
    o_ref[...] = jnp.dot(s, w_ref[...], precision=HIGHEST, preferred_element_type=F32) + b_ref[...]


def _ada_tables(cvecs, w_ada, b_ada):
    out = pl.pallas_call(
        _ada_body,
        out_shape=jax.ShapeDtypeStruct((DEPTH, N_MOD_ROWS, 6 * D_MODEL), F32),
        grid=(DEPTH, 6),
        in_specs=[
            _const_spec((N_MOD_ROWS, D_MODEL)),
            pl.BlockSpec((None, D_MODEL, D_MODEL), lambda l, j: (l, 0, j)),
            pl.BlockSpec((None, 1, D_MODEL), lambda l, j: (l, 0, j)),
        ],
        out_specs=pl.BlockSpec((None, N_MOD_ROWS, D_MODEL), lambda l, j: (l, 0, j)),
        compiler_params=_params("parallel", "parallel"),
        name="ada_tables",
    )(cvecs, w_ada, b_ada.reshape(DEPTH, 1, 6 * D_MODEL))
    return out.reshape(DEPTH, N_MOD_ROWS, 6, D_MODEL)


def _mod_spec():
    return pl.BlockSpec((None, 6, D_MODEL), lambda i: (_mod_row(i), 0, 0))


def _tile_spec():
    return pl.BlockSpec((TM, D_MODEL), lambda i: (i, 0))


SUBLANES = 8


def _conv_segment(pad_ref, shift_ref, conv_ref, wdw_ref, pad_base, out_base, seg):
    rb = min(seg, 64)
    for cb in range(B_WIDTH // 128):
        cs = slice(cb * 128, (cb + 1) * 128)
        for r0 in range(0, seg, rb):
            acc = jnp.zeros((rb, 128), F32)
            for k in range(CONV_W):
                b = (CONV_HALO - CONV_PAD + k) % SUBLANES
                off = pad_base + r0 + CONV_HALO - CONV_PAD + k - b
                src = pad_ref if b == 0 else shift_ref.at[b - 1]
                acc = acc + wdw_ref[k:k + 1, cs] * src[off:off + rb, cs]
            conv_ref[out_base + r0:out_base + r0 + rb, cs] = acc


def _l0_body(xc_ref, xl_ref, mod_ref, nmix_ref, win_ref, wsg_ref, bsg_ref, nsg_ref, wdw_ref, bdw_ref,
             ncv_ref, wout_ref, o_ref, cat_ref, pad_ref, shift_ref, conv_ref):
    i = pl.program_id(0)
    x = jnp.where(i < NT_P, xc_ref[...], xl_ref[...])
    h = _modulate(x, nmix_ref[...], mod_ref[0:1, :], mod_ref[1:2, :])
    z = _bdot(h.astype(BF16), win_ref[...])
    u = _gelu(z[:, :A_WIDTH])
    vb = _layernorm(_gelu(z[:, A_WIDTH:2 * A_WIDTH]), nsg_ref[...]).astype(BF16)
    for n in range(TM // CHUNK_A):
        rs = slice(n * CHUNK_A, (n + 1) * CHUNK_A)
        for g in range(A_GROUPS):
            cs = slice(g * A_GC, (g + 1) * A_GC)
            m = _bdot(wsg_ref[g], vb[rs, cs]) + bsg_ref[g]
            cat_ref[rs, cs] = (u[rs, cs] * m).astype(BF16)
    hb = z[:, 2 * A_WIDTH:2 * A_WIDTH + B_WIDTH] * jax.nn.sigmoid(z[:, 2 * A_WIDTH + B_WIDTH:])

    def conv_tile(seg):
        stride = seg + 2 * CONV_HALO
        halo = jnp.zeros((CONV_HALO, B_WIDTH), F32)
        for s in range(TM // seg):
            b = s * stride
            pad_ref[b:b + CONV_HALO, :] = halo
            pad_ref[b + CONV_HALO:b + CONV_HALO + seg, :] = hb[s * seg:(s + 1) * seg, :]
            pad_ref[b + CONV_HALO + seg:b + stride, :] = halo
        rows = (TM // seg) * stride - SUBLANES
        for b in range(1, SUBLANES):
            shift_ref[b - 1, 0:rows, :] = pad_ref[b:b + rows, :]
        for s in range(TM // seg):
            _conv_segment(pad_ref, shift_ref, conv_ref, wdw_ref, s * stride, s * seg, seg)

    @pl.when(i < NT_P)
    def _():
        conv_tile(SEQ)

    @pl.when(i >= NT_P)
    def _():
        conv_tile(GRID_W)

    yb = _layernorm(conv_ref[...] + bdw_ref[...], ncv_ref[...])
    cat_ref[:, A_WIDTH:] = _silu(yb).astype(BF16)
    out = _bdot(cat_ref[...], wout_ref[...])
    o_ref[...] = x + mod_ref[2:3, :] * out


def _l0_mixer(x_ctx, x_lat, mod, norm_mix, w_in, w_sg, b_sg, norm_sg, w_dw, b_dw, norm_cv, w_out):
    assert TM == SEQ and TM % GRID_W == 0 and TM % CHUNK_A == 0
    pad_rows = (TM // GRID_W) * (GRID_W + 2 * CONV_HALO)
    assert pad_rows >= SEQ + 2 * CONV_HALO
    return pl.pallas_call(
        _l0_body,
        out_shape=jax.ShapeDtypeStruct((T, D_MODEL), F32),
        grid=(NT,),
        in_specs=[
            pl.BlockSpec((TM, D_MODEL), lambda i: (jnp.minimum(i, NT_P - 1), 0)),
            pl.BlockSpec((TM, D_MODEL), lambda i: (jnp.maximum(i - NT_P, 0), 0)),
            _mod_spec(),
            _const_spec((1, D_MODEL)),
            _const_spec((D_MODEL, 2 * A_WIDTH + 2 * B_WIDTH)),
            _const_spec((A_GROUPS, CHUNK_A, CHUNK_A)),
            _const_spec((A_GROUPS, CHUNK_A, 1)),
            _const_spec((1, A_WIDTH)),
            _const_spec((CONV_W, B_WIDTH)),
            _const_spec((1, B_WIDTH)),
            _const_spec((1, B_WIDTH)),
            _const_spec((D_MODEL, D_MODEL)),
        ],
        out_specs=_tile_spec(),
        scratch_shapes=[
            pltpu.VMEM((TM, D_MODEL), BF16),
            pltpu.VMEM((pad_rows, B_WIDTH), F32),
            pltpu.VMEM((SUBLANES - 1, pad_rows, B_WIDTH), F32),
            pltpu.VMEM((TM, B_WIDTH), F32),
        ],
        compiler_params=_params("parallel"),
        name="l0_mixer",
    )(x_ctx, x_lat, mod, norm_mix.reshape(1, -1), w_in.astype(BF16), w_sg.astype(BF16),
      b_sg.reshape(A_GROUPS, CHUNK_A, 1), norm_sg.reshape(1, -1), w_dw, b_dw.reshape(1, -1),
      norm_cv.reshape(1, -1), w_out.astype(BF16))


def _route(scores, biased):
    n = scores.shape[-1]
    shp = (N_GROUPS, GROUP_SIZE, n)
    s3 = scores.reshape(shp)
    b3 = biased.reshape(shp)
    m_iota = lax.broadcasted_iota(jnp.int32, shp, 1).astype(F32)
    g_iota = lax.broadcasted_iota(jnp.int32, shp, 0).astype(F32)
    e_iota = g_iota * GROUP_SIZE + m_iota
    neg = -jnp.inf

    def amax1(v):
        return jnp.max(v, axis=1, keepdims=True)

    def amin1(v):
        return jnp.min(v, axis=1, keepdims=True)

    m1 = amax1(b3)
    i1 = amin1(jnp.where(b3 == m1, m_iota, float(GROUP_SIZE)))
    m2 = amax1(jnp.where(m_iota == i1, neg, b3))
    grp = m1 + m2
    gi1 = lax.broadcasted_iota(jnp.int32, grp.shape, 0).astype(F32)
    gmask = jnp.zeros(grp.shape, jnp.bool_)
    for _ in range(TOPK_GROUPS):
        gm = jnp.max(grp, axis=0, keepdims=True)
        gi = jnp.min(jnp.where(grp == gm, gi1, float(N_GROUPS)), axis=0, keepdims=True)
        hit = gi1 == gi
        gmask = jnp.logical_or(gmask, hit)
        grp = jnp.where(hit, neg, grp)
    cand = jnp.where(gmask, b3, neg)
    ids, vals, hits = [], [], []
    for _ in range(TOP_K):
        mx = jnp.max(amax1(cand), axis=0, keepdims=True)
        ei = jnp.min(amin1(jnp.where(cand == mx, e_iota, float(N_EXPERTS))), axis=0, keepdims=True)
        hit = e_iota == ei
        ids.append(ei.reshape(1, n))
        vals.append(_pick(hit, s3))
        hits.append(hit)
        cand = jnp.where(hit, neg, cand)
    return jnp.concatenate(ids, axis=0).astype(jnp.int32), jnp.concatenate(vals, axis=0), hits


def _pick(hit, v3):
    s = jnp.sum(jnp.sum(jnp.where(hit, v3, 0.0), axis=1, keepdims=True), axis=0, keepdims=True)
    return s.reshape(1, v3.shape[-1])


ROUTER_TILES = 2


def _moe_pre_body(*refs):
    run_ref = refs[-1]

    @pl.when(pl.program_id(0) == 0)
    def _():
        run_ref[...] = jnp.zeros(run_ref.shape, F32)

    for sub in range(ROUTER_TILES):
        _moe_pre_tile(pl.ds(sub * TM, TM), *refs)


def _moe_pre_tile(rows, x_ref, mod_ref, nffn_ref, wrt_ref, br_ref, wsg_ref, wsu_ref, wsd_ref,
                  h_ref, eidx_ref, w8_ref, rank_ref, cnt_ref, acc_ref, run_ref):
    x = x_ref[rows, :]
    h = _modulate(x, nffn_ref[...], mod_ref[3:4, :], mod_ref[4:5, :])
    h_ref[rows, :] = _pack_pairs(h)
    hb = h.astype(BF16)
    h_lo = (h - hb.astype(F32)).astype(BF16)
    wr = wrt_ref[...]
    wr_hi = wr.astype(BF16)
    wr_lo = (wr - wr_hi.astype(F32)).astype(BF16)
    logits_t = _dot_nt(wr_hi, hb) + (_dot_nt(wr_hi, h_lo) + _dot_nt(wr_lo, hb))
    scores = jax.nn.sigmoid(logits_t)
    eidx, sv, hits = _route(scores, scores + br_ref[...])
    eidx_ref[:, rows] = eidx
    w8_ref[:, rows] = sv / jnp.sum(sv, axis=0, keepdims=True) * ROUTED_SCALE
    sel3 = hits[0]
    for hit in hits[1:]:
        sel3 = jnp.logical_or(sel3, hit)
    sel = sel3.astype(F32).reshape(N_EXPERTS, TM)
    earlier = (lax.broadcasted_iota(jnp.int32, (TM, TM), 0)
               < lax.broadcasted_iota(jnp.int32, (TM, TM), 1)).astype(BF16)
    rank3 = (_bdot(sel.astype(BF16), earlier) + run_ref[...]).reshape(N_GROUPS, GROUP_SIZE, TM)
    rank_ref[:, rows] = jnp.concatenate([_pick(hit, rank3) for hit in hits], axis=0).astype(jnp.int32)
    run_ref[...] = run_ref[...] + jnp.sum(sel, axis=1, keepdims=True)
    cnt_ref[...] = run_ref[...].astype(jnp.int32)
    sh = _bdot((_silu(_bdot(hb, wsg_ref[...])) * _bdot(hb, wsu_ref[...])).astype(BF16), wsd_ref[...])
    acc_ref[rows, :] = x + mod_ref[5:6, :] * sh


def _moe_pre(x, mod, norm_ffn, w_router, b_router, w_sh_gate, w_sh_up, w_sh_down, tile0):
    g = ROUTER_TILES
    assert NT_PART % g == 0 and tile0 % g == 0 and NT_P % g == 0 and TILES_PER_LAT % g == 0
    rows = g * TM
    return pl.pallas_call(
        _moe_pre_body,
        out_shape=(
            jax.ShapeDtypeStruct((T_PART, D_PACK), jnp.int32),
            jax.ShapeDtypeStruct((TOP_K, T_PART), jnp.int32),
            jax.ShapeDtypeStruct((TOP_K, T_PART), F32),
            jax.ShapeDtypeStruct((TOP_K, T_PART), jnp.int32),
            jax.ShapeDtypeStruct((N_EXPERTS, 1), jnp.int32),
            jax.ShapeDtypeStruct((T_PART, D_MODEL), F32),
        ),
        grid=(NT_PART // g,),
        in_specs=[
            pl.BlockSpec((rows, D_MODEL), lambda i: (i + tile0 // g, 0)),
            pl.BlockSpec((None, 6, D_MODEL), lambda i: (_mod_row(i * g + tile0), 0, 0)),
            _const_spec((1, D_MODEL)),
            _const_spec((N_EXPERTS, D_MODEL)),
            _const_spec((N_EXPERTS, 1)),
            _const_spec((D_MODEL, D_SHARED)),
            _const_spec((D_MODEL, D_SHARED)),
            _const_spec((D_SHARED, D_MODEL)),
        ],
        out_specs=(
            pl.BlockSpec((rows, D_PACK), lambda i: (i, 0)),
            pl.BlockSpec((TOP_K, rows), lambda i: (0, i)),
            pl.BlockSpec((TOP_K, rows), lambda i: (0, i)),
            pl.BlockSpec((TOP_K, rows), lambda i: (0, i)),
            _const_spec((N_EXPERTS, 1)),
            pl.BlockSpec((rows, D_MODEL), lambda i: (i, 0)),
        ),
        scratch_shapes=[pltpu.VMEM((N_EXPERTS, 1), F32)],
        compiler_params=_params("arbitrary"),
        name="moe_router_shared",
    )(x, mod, norm_ffn.reshape(1, -1), w_router.T, b_router.reshape(N_EXPERTS, 1),
      w_sh_gate.astype(BF16), w_sh_up.astype(BF16), w_sh_down.astype(BF16))


EXPERT_RING = 4


WEIGHT_SLOTS = 2


EXPERT_TILES_PER_STEP = 8


def _expert_body(be_ref, nv_ref, nu_ref, run_ref, rexp_ref, nrun_ref, xs_hbm, wg_hbm, wu_hbm, wd_hbm, ys_ref,
                 wgu_s, wd_s, ring, sems, wg_buf, wu_buf, wd_buf, wsems, *, layer):
    n_used = nu_ref[0]
    n_runs = nrun_ref[0]

    def fetch(t):
        slot = t % EXPERT_RING
        return pltpu.make_async_copy(xs_hbm.at[pl.ds(t * TE, TE)], ring.at[slot], sems.at[slot])

    def wfetch(r):
        slot = r % WEIGHT_SLOTS
        e = rexp_ref[r]
        return [pltpu.make_async_copy(wg_hbm.at[layer, e], wg_buf.at[slot], wsems.at[slot, 0]),
                pltpu.make_async_copy(wu_hbm.at[layer, e], wu_buf.at[slot], wsems.at[slot, 1]),
                pltpu.make_async_copy(wd_hbm.at[layer, e], wd_buf.at[slot], wsems.at[slot, 2])]

    def tile_step(j, out_rows):
        live_tile = j < n_used

        @pl.when(j == 0)
        def _():
            for t in range(EXPERT_RING - 1):
                @pl.when(t < n_used)
                def _():
                    fetch(t).start()
            for r in range(WEIGHT_SLOTS):
                @pl.when(r < n_runs)
                def _():
                    for cp in wfetch(r):
                        cp.start()

        @pl.when(j + (EXPERT_RING - 1) < n_used)
        def _():
            fetch(j + (EXPERT_RING - 1)).start()

        @pl.when(jnp.logical_and(live_tile, jnp.logical_or(j == 0, be_ref[j] != be_ref[jnp.maximum(j - 1, 0)])))
        def _():
            r = run_ref[j]
            slot = r % WEIGHT_SLOTS
            for cp in wfetch(r):
                cp.wait()
            wgu_s[:, :D_EXPERT] = wg_buf[slot].astype(BF16)
            wgu_s[:, D_EXPERT:] = wu_buf[slot].astype(BF16)
            wd_s[...] = wd_buf[slot].astype(BF16)

            @pl.when(r + WEIGHT_SLOTS < n_runs)
            def _():
                for cp in wfetch(r + WEIGHT_SLOTS):
                    cp.start()

        @pl.when(live_tile)
        def _():
            fetch(j).wait()
            live = lax.broadcasted_iota(jnp.int32, (TE, 1), 0) < nv_ref[j]
            lo, hi = _unpack_pairs(jnp.where(live, ring[j % EXPERT_RING], 0))
            xb = jnp.concatenate([lo.astype(BF16), hi.astype(BF16)], axis=1)
            hgu = _bdot(xb, wgu_s[...])
            hh = _silu(hgu[:, :D_EXPERT]) * hgu[:, D_EXPERT:]
            ys_ref[out_rows, :] = _pack_pairs(_bdot(hh.astype(BF16), wd_s[...]))

    for u in range(EXPERT_TILES_PER_STEP):
        tile_step(pl.program_id(0) * EXPERT_TILES_PER_STEP + u, pl.ds(u * TE, TE))


def _experts(block_expert, block_rows, n_used, block_run, run_expert, n_runs, xs, w_gate, w_up, w_down, layer):
    g = EXPERT_TILES_PER_STEP
    assert NB % g == 0

    def row_map(s, be, nv, nu, run, rexp, nrun):
        return (jnp.minimum(s, (nu[0] - 1) // g), 0)

    any_spec = pl.BlockSpec(memory_space=pl.ANY)
    return pl.pallas_call(
        functools.partial(_expert_body, layer=layer),
        out_shape=jax.ShapeDtypeStruct((P_ROWS, D_PACK), jnp.int32),
        grid_spec=pltpu.PrefetchScalarGridSpec(
            num_scalar_prefetch=6,
            grid=(NB // g,),
            in_specs=[any_spec, any_spec, any_spec, any_spec],
            out_specs=pl.BlockSpec((g * TE, D_PACK), row_map),
            scratch_shapes=[
                pltpu.VMEM((D_MODEL, 2 * D_EXPERT), BF16),
                pltpu.VMEM((D_EXPERT, D_MODEL), BF16),
                pltpu.VMEM((EXPERT_RING, TE, D_PACK), jnp.int32),
                pltpu.SemaphoreType.DMA((EXPERT_RING,)),
                pltpu.VMEM((WEIGHT_SLOTS, D_MODEL, D_EXPERT), F32),
                pltpu.VMEM((WEIGHT_SLOTS, D_MODEL, D_EXPERT), F32),
                pltpu.VMEM((WEIGHT_SLOTS, D_EXPERT, D_MODEL), F32),
                pltpu.SemaphoreType.DMA((WEIGHT_SLOTS, 3)),
            ],
        ),
        compiler_params=_params("arbitrary"),
        name="moe_experts",
    )(block_expert, block_rows, n_used, block_run, run_expert, n_runs, xs, w_gate, w_up, w_down)


def _positions_body(start_ref, eidx_ref, rank_ref, pos_ref):
    eidx = eidx_ref[...]
    base = jnp.zeros(eidx.shape, jnp.int32)
    for e in range(N_EXPERTS):
        base = jnp.where(eidx == e, start_ref[e], base)
    pos_ref[...] = base * TE + rank_ref[...]


def _positions(blk_start, eidx_t, rank_t):
    full = pl.BlockSpec((TOP_K, T_PART), lambda i, s: (0, 0))
    return pl.pallas_call(
        _positions_body,
        out_shape=jax.ShapeDtypeStruct((TOP_K, T_PART), jnp.int32),
        grid_spec=pltpu.PrefetchScalarGridSpec(
            num_scalar_prefetch=1, grid=(1,), in_specs=[full, full], out_specs=full),
        compiler_params=_params("arbitrary"),
        name="moe_positions",
    )(blk_start, eidx_t, rank_t)


def _sc_mesh():
    return plsc.VectorSubcoreMesh(core_axis_name="c", subcore_axis_name="s")


def _sc_worker():
    return lax.axis_index("s") * SC_CORES + lax.axis_index("c")


def _sc_scatter_rows(h, pos_rows):
    c = SC_CHUNK
    n_chunks = T_PART // SC_WORKERS // c
    width = h.shape[1]

    @functools.partial(
        pl.kernel, mesh=_sc_mesh(),
        out_type=jax.ShapeDtypeStruct((P_ROWS, width), h.dtype),
        scratch_types=[pltpu.VMEM((n_chunks * TOP_K, c), jnp.int32),
                       pltpu.VMEM((c, width), h.dtype), pltpu.VMEM((c, width), h.dtype)]
        + [pltpu.SemaphoreType.DMA] * 4,
        name="moe_dispatch_scatter",
    )
    def scatter(h_hbm, pos_hbm, xs_hbm, idx_v, buf0, buf1, sem_in0, sem_in1, sem_out0, sem_out1):
        assert n_chunks % 2 == 0
        bufs, sem_in, sem_out = (buf0, buf1), (sem_in0, sem_in1), (sem_out0, sem_out1)
        first = _sc_worker() * n_chunks
        pltpu.sync_copy(pos_hbm.at[pl.ds(first * TOP_K, n_chunks * TOP_K)], idx_v)

        def load(i, b):
            return pltpu.make_async_copy(h_hbm.at[pl.ds((first + i) * c, c)], bufs[b], sem_in[b])

        def puts(i, b):
            return [pltpu.make_async_copy(bufs[b], xs_hbm.at[idx_v.at[i * TOP_K + k]], sem_out[b])
                    for k in range(TOP_K)]

        load(0, 0).start()
        load(1, 1).start()

        @pl.loop(0, n_chunks, step=2)
        def _(i):
            for b in range(2):
                load(i + b, b).wait()
                for cp in puts(i + b, b):
                    cp.start()
            for b in range(2):
                for cp in puts(i + b, b):
                    cp.wait()

                @pl.when(i + 2 + b < n_chunks)
                def _():
                    load(i + 2 + b, b).start()

    return scatter(h, pos_rows)


def _sc_combine(ys, pos_tk, wsplat):
    ct = COMBINE_TOKENS
    rows = ct * TOP_K
    tok_per_worker = T_PART // SC_WORKERS
    n_chunks = tok_per_worker // ct
    width = ys.shape[1]
    vmem = pltpu.VMEM

    @functools.partial(
        pl.kernel, mesh=_sc_mesh(),
        out_type=jax.ShapeDtypeStruct((T_PART, D_MODEL), F32),
        scratch_types=[vmem((tok_per_worker * TOP_K,), jnp.int32),
                       vmem((rows, width), ys.dtype), vmem((rows, width), ys.dtype),
                       vmem((ct, TOP_K * SC_LANES), F32), vmem((ct, TOP_K * SC_LANES), F32),
                       vmem((ct, D_MODEL), F32), vmem((ct, D_MODEL), F32)]
        + [pltpu.SemaphoreType.DMA] * 6,
        compiler_params=pltpu.CompilerParams(needs_layout_passes=False),
        name="moe_combine",
    )
    def combine(ys_hbm, pos_hbm, w_hbm, out_hbm, idx_v, g0, g1, w0, w1, o0, o1,
                sem_g0, sem_g1, sem_w0, sem_w1, sem_o0, sem_o1):
        assert n_chunks % 2 == 0
        gbuf, wbuf, obuf = (g0, g1), (w0, w1), (o0, o1)
        sem_g, sem_w, sem_o = (sem_g0, sem_g1), (sem_w0, sem_w1), (sem_o0, sem_o1)
        tok0 = _sc_worker() * tok_per_worker
        pltpu.sync_copy(pos_hbm.at[pl.ds(tok0 * TOP_K, tok_per_worker * TOP_K)], idx_v)

        def fetch(i, b):
            return [pltpu.make_async_copy(ys_hbm.at[idx_v.at[pl.ds(i * rows, rows)]], gbuf[b], sem_g[b]),
                    pltpu.make_async_copy(w_hbm.at[pl.ds(tok0 + i * ct, ct)], wbuf[b], sem_w[b])]

        def flush(i, b):
            return pltpu.make_async_copy(obuf[b], out_hbm.at[pl.ds(tok0 + i * ct, ct)], sem_o[b])

        def reduce_chunk(b):
            @pl.loop(0, ct)
            def _(t):
                wv = [wbuf[b][t, pl.ds(k * SC_LANES, SC_LANES)] for k in range(TOP_K)]

                @pl.loop(0, width // SC_LANES)
                def _(j):
                    col = j * SC_LANES
                    acc_lo = acc_hi = None
                    for k in range(TOP_K):
                        word = gbuf[b][t * TOP_K + k, pl.ds(col, SC_LANES)]
                        lo = lax.bitcast_convert_type(word << 16, F32)
                        hi = lax.bitcast_convert_type(word & jnp.int32(-65536), F32)
                        acc_lo = wv[k] * lo if k == 0 else acc_lo + wv[k] * lo
                        acc_hi = wv[k] * hi if k == 0 else acc_hi + wv[k] * hi
                    obuf[b][t, pl.ds(col, SC_LANES)] = acc_lo
                    obuf[b][t, pl.ds(D_PACK + col, SC_LANES)] = acc_hi

        for b in range(2):
            for cp in fetch(b, b):
                cp.start()

        @pl.loop(0, n_chunks, step=2)
        def _(i):
            for b in range(2):
                for cp in fetch(i + b, b):
                    cp.wait()

                @pl.when(i > 0)
                def _():
                    flush(i + b - 2, b).wait()

                reduce_chunk(b)
                flush(i + b, b).start()

                @pl.when(i + 2 + b < n_chunks)
                def _():
                    for cp in fetch(i + 2 + b, b):
                        cp.start()

        for b in range(2):
            flush(n_chunks - 2 + b, b).wait()

    return combine(ys, pos_tk, wsplat)


def _moe_routed(h, eidx_t, rank_t, counts, wsplat, w_gate, w_up, w_down, layer):
    counts = counts.reshape(1, N_EXPERTS)
    nblk = (counts + TE - 1) // TE
    blk_end = jnp.cumsum(nblk, axis=1)
    blk_start = blk_end - nblk
    blocks = jnp.arange(NB, dtype=jnp.int32).reshape(NB, 1)
    block_expert = jnp.minimum(jnp.sum(blocks >= blk_end, axis=1, keepdims=True), N_EXPERTS - 1)
    mine = block_expert == jnp.arange(N_EXPERTS, dtype=jnp.int32).reshape(1, N_EXPERTS)
    cnt_b = jnp.sum(jnp.where(mine, counts, 0), axis=1, keepdims=True)
    start_b = jnp.sum(jnp.where(mine, blk_start, 0), axis=1, keepdims=True)
    block_rows = jnp.clip(cnt_b - (blocks - start_b) * TE, 0, TE)
    n_used = blk_end[0, -1].reshape(1).astype(jnp.int32)
    present = (nblk > 0).astype(jnp.int32)
    run_of_expert = jnp.cumsum(present, axis=1) - 1
    block_run = jnp.sum(jnp.where(mine, run_of_expert, 0), axis=1)
    experts = jnp.arange(N_EXPERTS, dtype=jnp.int32).reshape(1, N_EXPERTS)
    run_hit = jnp.logical_and(run_of_expert == experts.reshape(N_EXPERTS, 1), present > 0)
    run_expert = jnp.sum(jnp.where(run_hit, experts, 0), axis=1)
    n_runs = jnp.sum(present).reshape(1)
    pos = _positions(blk_start.reshape(N_EXPERTS).astype(jnp.int32), eidx_t, rank_t)
    pos_rows = pos.reshape(TOP_K, T_PART // SC_CHUNK, SC_CHUNK).transpose(1, 0, 2).reshape(-1, SC_CHUNK)
    xs = _sc_scatter_rows(h, pos_rows)
    ys = _experts(block_expert.reshape(NB).astype(jnp.int32), block_rows.reshape(NB).astype(jnp.int32),
                  n_used, block_run.astype(jnp.int32), run_expert.astype(jnp.int32), n_runs.astype(jnp.int32),
                  xs, w_gate, w_up, w_down, layer)
    return _sc_combine(ys, pos.T.reshape(-1), wsplat)


N_SLABS = (3 * C_FDIM + 2 * D_MODEL) // 128


F32_GROUPS = (1, 2)
B16_GROUPS = (0, 3, 4)


def _hgrn_in_body(acc_ref, rt_ref, mod0_ref, mod1_ref, nmix_ref, win_ref, *rest):
    x_ref, zf_ref, zb_ref = rest[-3:]
    x = acc_ref[...] + mod0_ref[5:6, :] * rt_ref[...]
    x_ref[...] = x
    hb = _modulate(x, nmix_ref[...], mod1_ref[0:1, :], mod1_ref[1:2, :]).astype(BF16)
    for s in range(N_SLABS // C_HEADS):
        zz = _bdot(hb, win_ref[:, s * D_MODEL:(s + 1) * D_MODEL])
        for hh in range(C_HEADS):
            if s in F32_GROUPS:
                zf_ref[F32_GROUPS.index(s) * C_HEADS + hh] = zz[:, hh * 128:(hh + 1) * 128]
            else:
                zb_ref[B16_GROUPS.index(s) * C_HEADS + hh] = zz[:, hh * 128:(hh + 1) * 128].astype(BF16)


HGRN_IN_TILES = 2


def _hgrn_in(acc, routed, tile0, prev, mod0, mod1, norm_mix, w_in):
    g = HGRN_IN_TILES
    assert NT_PART % g == 0 and tile0 % g == 0 and NT_P % g == 0 and TILES_PER_LAT % g == 0
    rows = g * TM
    block0 = tile0 // g

    def shifted_mod():
        return pl.BlockSpec((None, 6, D_MODEL), lambda i: (_mod_row(i * g + tile0), 0, 0))

    local = pl.BlockSpec((rows, D_MODEL), lambda i: (i, 0))
    any_spec = pl.BlockSpec(memory_space=pl.ANY)
    prev = () if prev is None else tuple(prev)
    n_in = 6
    return pl.pallas_call(
        _hgrn_in_body,
        out_shape=(
            jax.ShapeDtypeStruct((T, D_MODEL), F32),
            jax.ShapeDtypeStruct((len(F32_GROUPS) * C_HEADS, T, 128), F32),
            jax.ShapeDtypeStruct((len(B16_GROUPS) * C_HEADS, T, 128), BF16),
        ),
        grid=(NT_PART // g,),
        in_specs=[
            local, local, shifted_mod(), shifted_mod(),
            _const_spec((1, D_MODEL)),
            pl.BlockSpec((D_MODEL, 3 * C_FDIM + 2 * D_MODEL), lambda i: (0, 0), pipeline_mode=pl.Buffered(1)),
        ] + [any_spec] * len(prev),
        out_specs=(
            pl.BlockSpec((rows, D_MODEL), lambda i: (i + block0, 0)),
            pl.BlockSpec((len(F32_GROUPS) * C_HEADS, rows, 128), lambda i: (0, i + block0, 0)),
            pl.BlockSpec((len(B16_GROUPS) * C_HEADS, rows, 128), lambda i: (0, i + block0, 0)),
        ),
        input_output_aliases={n_in + k: k for k in range(len(prev))},
        compiler_params=_params("parallel"),
        name="hgrn_in_proj",
    )(acc, routed, mod0, mod1, norm_mix.reshape(1, -1), w_in.astype(BF16), *prev)


GLA_TILES = 4


def _gla_body(q_ref, f_ref, v_ref, lb_ref, s0_ref, o_ref, ns_ref, st_ref, *, rev):
    blk = NT // GLA_TILES - 1 - pl.program_id(0) if rev else pl.program_id(0)
    row = lax.broadcasted_iota(jnp.int32, (TM, TM), 0)
    col = lax.broadcasted_iota(jnp.int32, (TM, TM), 1)
    same_chunk = (row // SCAN_CHUNK) == (col // SCAN_CHUNK)
    seen = jnp.logical_and(same_chunk, (col >= row) if rev else (col <= row))
    cum_w = seen.astype(BF16)
    mid = SCAN_CHUNK // 2 if rev else SCAN_CHUNK // 2 - 1
    last = 0 if rev else SCAN_CHUNK - 1
    n_chunks = TM // SCAN_CHUNK
    order = range(n_chunks - 1, -1, -1) if rev else range(n_chunks)
    group = 4

    def chunk_rows(b, off):
        return jnp.concatenate(
            [jnp.broadcast_to(b[c * SCAN_CHUNK + off:c * SCAN_CHUNK + off + 1, :], (SCAN_CHUNK, b.shape[1]))
             for c in range(n_chunks)], axis=0)

    def tile_step(sub):
        ti = blk * GLA_TILES + sub
        r0 = sub * TM
        is_ctx = ti < NT_P
        first_lat = (ti - NT_P) % TILES_PER_LAT == (TILES_PER_LAT - 1 if rev else 0)

        @pl.when(is_ctx)
        def _():
            st_ref[...] = jnp.zeros(st_ref.shape, F32)

        @pl.when(jnp.logical_and(jnp.logical_not(is_ctx), first_lat))
        def _():
            st_ref[...] = s0_ref[...]

        def head_group(gi, carry):
            heads = [gi * group + u for u in range(group)]
            qs, kk, vv, bcum = [], [], [], []
            for hd in heads:
                lb = lb_ref[hd]
                qs.append(_silu(q_ref[hd, pl.ds(r0, TM), :].astype(F32)) * (C_DK ** -0.5))
                fg = lb + (1.0 - lb) * jax.nn.sigmoid(f_ref[hd, pl.ds(r0, TM), :])
                kk.append(1.0 - fg)
                vv.append(v_ref[hd, pl.ds(r0, TM), :])
                g = jnp.log(fg)
                g_hi = g.astype(BF16)
                r1 = g - g_hi.astype(F32)
                g_mid = r1.astype(BF16)
                g_lo = (r1 - g_mid.astype(F32)).astype(BF16)
                bcum.append(_bdot(cum_w, g_hi) + _bdot(cum_w, g_mid) + _bdot(cum_w, g_lo))
            o_intra, q_dec, kv, decay = [], [], [], []
            for u in range(group):
                b_mid = chunk_rows(bcum[u], mid)
                b_last = chunk_rows(bcum[u], last)
                qe = (qs[u] * jnp.exp(bcum[u] - b_mid)).astype(BF16)
                ke = (kk[u] * jnp.exp(b_mid - bcum[u])).astype(BF16)
                att = jnp.where(seen, _dot_nt(qe, ke), 0.0)
                o_intra.append(_bdot(att.astype(BF16), vv[u]))
                q_dec.append((qs[u] * jnp.exp(bcum[u])).astype(BF16))
                k_dec = (kk[u] * jnp.exp(b_last - bcum[u])).astype(BF16)
                kv.append([_dot_tn(vv[u][c * SCAN_CHUNK:(c + 1) * SCAN_CHUNK], k_dec[c * SCAN_CHUNK:(c + 1) * SCAN_CHUNK])
                           for c in range(n_chunks)])
                decay.append([jnp.exp(bcum[u][c * SCAN_CHUNK + last:c * SCAN_CHUNK + last + 1, :])
                              for c in range(n_chunks)])
            st = [st_ref[hd] for hd in heads]
            for c in order:
                sl = slice(c * SCAN_CHUNK, (c + 1) * SCAN_CHUNK)
                for u, hd in enumerate(heads):
                    o_ref[hd, pl.ds(r0 + c * SCAN_CHUNK, SCAN_CHUNK), :] = (
                        o_intra[u][sl] + _dot_nt(q_dec[u][sl], st[u].astype(BF16))).astype(BF16)
                    st[u] = decay[u][c] * st[u] + kv[u][c]
            for u, hd in enumerate(heads):
                st_ref[hd] = st[u]
            return carry

        lax.fori_loop(0, C_HEADS // group, head_group, 0)

        @pl.when(is_ctx)
        def _():
            ns_ref[sub] = st_ref[...]

    for sub in (reversed(range(GLA_TILES)) if rev else range(GLA_TILES)):
        tile_step(sub)


def _gla(zf, zb, lb_dir, s0t_dir, *, rev):
    g = GLA_TILES
    assert NT % g == 0 and NT_P % g == 0 and TILES_PER_LAT % g == 0
    rows = g * TM

    def blk_of(j):
        return NT // g - 1 - j if rev else j

    f_slab = F32_GROUPS.index(2 if rev else 1)

    def lat_map(j):
        return (jnp.clip((blk_of(j) * g - NT_P) // TILES_PER_LAT, 0, DEC_BATCH - 1), 0, 0, 0)

    return pl.pallas_call(
        functools.partial(_gla_body, rev=rev),
        out_shape=(
            jax.ShapeDtypeStruct((C_HEADS, T, C_DV), BF16),
            jax.ShapeDtypeStruct((BATCH, C_HEADS, C_DV, C_DK), F32),
        ),
        grid=(NT // g,),
        in_specs=[
            pl.BlockSpec((C_HEADS, rows, 128), lambda j: (B16_GROUPS.index(0), blk_of(j), 0)),
            pl.BlockSpec((C_HEADS, rows, 128), lambda j: (f_slab, blk_of(j), 0)),
            pl.BlockSpec((C_HEADS, rows, 128), lambda j: (B16_GROUPS.index(3), blk_of(j), 0)),
            _const_spec((C_HEADS, 1, C_DK)),
            pl.BlockSpec((None, C_HEADS, C_DV, C_DK), lat_map),
        ],
        out_specs=(
            pl.BlockSpec((C_HEADS, rows, C_DV), lambda j: (0, blk_of(j), 0)),
            pl.BlockSpec((g, C_HEADS, C_DV, C_DK),
                         lambda j: (jnp.minimum(blk_of(j), NT_P // g - 1), 0, 0, 0)),
        ),
        scratch_shapes=[pltpu.VMEM((C_HEADS, C_DV, C_DK), F32)],
        compiler_params=_params("arbitrary"),
        name="gla_bwd" if rev else "gla_fwd",
    )(zb, zf, zb, lb_dir, s0t_dir)


def _hgrn_out_body(ofw_ref, obw_ref, gate_ref, x_ref, mod_ref, no_ref, wout_ref, o_ref, cat_ref):
    for hd in range(C_HEADS):
        o = ofw_ref[hd].astype(F32) + obw_ref[hd].astype(F32)
        gate = gate_ref[hd].astype(F32)
        cat_ref[:, hd * C_DV:(hd + 1) * C_DV] = (_rms(o, no_ref[...]) * _silu(gate)).astype(BF16)
    o_ref[...] = x_ref[...] + mod_ref[2:3, :] * _bdot(cat_ref[...], wout_ref[...])


HGRN_OUT_TILES = 4


def _hgrn_out(o_fw, o_bw, zb, x, mod, norm_o, w_out):
    g = HGRN_OUT_TILES
    assert NT % g == 0 and NT_P % g == 0 and TILES_PER_LAT % g == 0
    rows = g * TM
    head_spec = pl.BlockSpec((C_HEADS, rows, C_DV), lambda i: (0, i, 0))
    row_spec = pl.BlockSpec((rows, D_MODEL), lambda i: (i, 0))
    return pl.pallas_call(
        _hgrn_out_body,
        out_shape=jax.ShapeDtypeStruct((T, D_MODEL), F32),
        grid=(NT // g,),
        in_specs=[
            head_spec, head_spec,
            pl.BlockSpec((C_HEADS, rows, 128), lambda i: (B16_GROUPS.index(4), i, 0)),
            row_spec,
            pl.BlockSpec((None, 6, D_MODEL), lambda i: (_mod_row(i * g), 0, 0)),
            _const_spec((1, C_DV)),
            _const_spec((D_MODEL, D_MODEL)),
        ],
        out_specs=row_spec,
        scratch_shapes=[pltpu.VMEM((rows, D_MODEL), BF16)],
        compiler_params=_params("parallel"),
        name="hgrn_out_proj",
    )(o_fw, o_bw, zb, x, mod, norm_o.reshape(1, -1), w_out.astype(BF16))


def _final_body(acc_ref, rt_ref, mod_ref, nf_ref, *rest):
    o_ref = rest[-1]
    o_ref[...] = _rms(acc_ref[...] + mod_ref[5:6, :] * rt_ref[...], nf_ref[...])


FINAL_TILES = 4


def _final(acc, routed, mod, norm_final, part_tile0, local0, n_tiles, out_tile0, out_tiles, prev=None):
    g = FINAL_TILES
    assert all(v % g == 0 for v in (part_tile0, local0, n_tiles, out_tile0, NT_P, TILES_PER_LAT))
    rows = g * TM
    local = pl.BlockSpec((rows, D_MODEL), lambda i: (i + local0 // g, 0))
    prev = () if prev is None else (prev,)
    return pl.pallas_call(
        _final_body,
        out_shape=jax.ShapeDtypeStruct((out_tiles * TM, D_MODEL), F32),
        grid=(n_tiles // g,),
        in_specs=[
            local, local,
            pl.BlockSpec((None, 6, D_MODEL), lambda i: (_mod_row(i * g + local0 + part_tile0), 0, 0)),
            _const_spec((1, D_MODEL)),
        ] + [pl.BlockSpec(memory_space=pl.ANY)] * len(prev),
        out_specs=pl.BlockSpec((rows, D_MODEL), lambda i: (i + out_tile0 // g, 0)),
        input_output_aliases={4 + k: 0 for k in range(len(prev))},
        compiler_params=_params("parallel"),
        name="final_norm",
    )(acc, routed, mod, norm_final.reshape(1, -1), *prev)


def kernel(x_prompt, x_sample, state_hgrn, c, c_ctx, w_ada, b_ada, norm_mix, norm_ffn, w_out, w_in_ab, w_sg, b_sg, norm_sg, w_dw, b_dw, norm_cv, w_in_hgrn, lb_raw, norm_o, w_router, b_router, w_gate, w_up, w_down, w_sh_gate, w_sh_up, w_sh_down, norm_final):
    cvecs = jnp.concatenate(
        [c_ctx.reshape(1, D_MODEL), c, jnp.zeros((N_MOD_ROWS - 1 - DEC_BATCH, D_MODEL), F32)], axis=0)
    mods = _ada_tables(cvecs, w_ada, b_ada)
    lb_sm = jax.nn.softmax(lb_raw.astype(F32), axis=0)
    lb1 = (jnp.cumsum(lb_sm, axis=0) - lb_sm[0])[1].reshape(2, C_HEADS, 1, C_DK)

    def moe(l, xin):
        parts = []
        for p in range(MOE_PARTS):
            h, eidx_t, w8_t, rank_t, counts, acc = _moe_pre(
                xin, mods[l], norm_ffn[l], w_router[l], b_router[l], w_sh_gate[l], w_sh_up[l], w_sh_down[l],
                p * NT_PART)
            wsplat = jnp.repeat(w8_t.T, SC_LANES, axis=1)
            parts += [acc, _moe_routed(h, eidx_t, rank_t, counts, wsplat, w_gate, w_up, w_down, l)]
        return parts

    x = _l0_mixer(x_prompt.reshape(T_P, D_MODEL), x_sample.reshape(T_S, D_MODEL), mods[0], norm_mix[0], w_in_ab[0], w_sg[0], b_sg[0], norm_sg[0], w_dw[0],
                  b_dw[0], norm_cv[0], w_out[0])
    parts = moe(0, x)
    xz = None
    for p in range(MOE_PARTS):
        xz = _hgrn_in(parts[2 * p], parts[2 * p + 1], p * NT_PART, xz, mods[0], mods[1], norm_mix[1], w_in_hgrn[0])
    x, zf, zb = xz
    s0t = jnp.swapaxes(state_hgrn[:, 0].astype(F32), -1, -2)
    o_fw, ns_fw = _gla(zf, zb, lb1[0], s0t[:, 0], rev=False)
    o_bw, ns_bw = _gla(zf, zb, lb1[1], s0t[:, 1], rev=True)
    x = _hgrn_out(o_fw, o_bw, zb, x, mods[1], norm_o[0], w_out[1])
    parts = moe(1, x)
    y_p = y_s = None
    for p in range(MOE_PARTS):
        lo, hi = p * NT_PART, (p + 1) * NT_PART
        if lo < NT_P:
            n = min(hi, NT_P) - lo
            y_p = _final(parts[2 * p], parts[2 * p + 1], mods[1], norm_final, lo, 0, n, lo, NT_P, y_p)
        if hi > NT_P:
            first = max(lo, NT_P)
            y_s = _final(parts[2 * p], parts[2 * p + 1], mods[1], norm_final, lo, first - lo, hi - first,
                         first - NT_P, NT_S, y_s)
    y_p = y_p.reshape(BATCH, SEQ, D_MODEL)
    y_s = y_s.reshape(DEC_BATCH, DEC_SEQ, D_MODEL)
    new_state = jnp.swapaxes(jnp.stack([ns_fw, ns_bw], axis=1), -1, -2)[:, None]
    return (y_p, y_s, new_state)
```
